```python
import math
import jax, jax.numpy as jnp
from jax import lax
import numpy as np

D_MODEL = 2048
BATCH = 8
SEQ = 2048
DEPTH = 1

CHUNK = 64
D_S5 = D_MODEL // 2
S5_GROUP = 16
S5_GROUPS = D_S5 // S5_GROUP
S5_STATE = 64
DN_HEADS = 8
DN_HEAD_DIM = 128
D_DN = DN_HEADS * DN_HEAD_DIM
CONV_K = 4
EPS = 1e-6
IN_SPLIT_SIZES = (D_S5, D_S5, D_DN, D_DN, D_DN, D_DN, DN_HEADS, DN_HEADS, D_MODEL, D_MODEL)
D_IN = 2 * D_S5 + 4 * D_DN + 2 * DN_HEADS + 2 * D_MODEL

kernel_name = "hybrid_s5_gated_deltanet_block"


def _f32(t):
    return t.astype(jnp.float32)


def rmsnorm(x, w):
    xf = _f32(x)
    return xf * lax.rsqrt(jnp.mean(xf * xf, axis=-1, keepdims=True) + EPS) * _f32(w)


def l2norm(t):
    return t * lax.rsqrt(jnp.sum(t * t, axis=-1, keepdims=True) + EPS)


def causal_depthwise_conv(x, w):
    c = x.shape[-1]
    return lax.conv_general_dilated(
        x, w[:, None, :], window_strides=(1,), padding=[(CONV_K - 1, 0)],
        dimension_numbers=("NWC", "WIO", "NWC"), feature_group_count=c)


def s5_mixer(u, z, lam_re, lam_im, log_step, b_re, b_im, c_re, c_im, d_skip, w_glu):
    bsz, l, _ = u.shape
    lam_re, lam_im = _f32(lam_re), _f32(lam_im)
    step = jnp.exp(_f32(log_step))[:, None]
    mag = jnp.exp(lam_re * step)
    abar_re = mag * jnp.cos(lam_im * step)
    abar_im = mag * jnp.sin(lam_im * step)
    den = lam_re * lam_re + lam_im * lam_im
    xr = abar_re - 1.0
    f_re = (xr * lam_re + abar_im * lam_im) / den
    f_im = (abar_im * lam_re - xr * lam_im) / den
    b_re, b_im = _f32(b_re), _f32(b_im)
    bb_re = f_re[..., None] * b_re - f_im[..., None] * b_im
    bb_im = f_re[..., None] * b_im + f_im[..., None] * b_re
    ug = u.reshape(bsz, l, S5_GROUPS, S5_GROUP)
    bu_re = jnp.einsum("blgc,gpc->blgp", ug, bb_re)
    bu_im = jnp.einsum("blgc,gpc->blgp", ug, bb_im)
    a_re = jnp.broadcast_to(abar_re, bu_re.shape)
    a_im = jnp.broadcast_to(abar_im, bu_im.shape)

    def combine(e1, e2):
        a1r, a1i, b1r, b1i = e1
        a2r, a2i, b2r, b2i = e2
        return (a2r * a1r - a2i * a1i,
                a2r * a1i + a2i * a1r,
                a2r * b1r - a2i * b1i + b2r,
                a2r * b1i + a2i * b1r + b2i)

    _, _, s_re, s_im = lax.associative_scan(combine, (a_re, a_im, bu_re, bu_im), axis=1)
    y = (jnp.einsum("blgp,gcp->blgc", s_re, _f32(c_re))
         - jnp.einsum("blgp,gcp->blgc", s_im, _f32(c_im)))
    y = y.reshape(bsz, l, D_S5) + _f32(d_skip) * u
    y = jax.nn.gelu(y)
    y = y * jax.nn.sigmoid(y @ _f32(w_glu))
    return y * jax.nn.silu(z)


def gated_delta_rule(q, k, v, g, beta):
    bsz, l, h, dk = q.shape
    dv = v.shape[-1]
    n = l // CHUNK

    def chunks(t):
        return t.reshape(bsz, n, CHUNK, h, -1).transpose(0, 3, 1, 2, 4)

    q = chunks(q) * (dk ** -0.5)
    k = chunks(k)
    v = chunks(v)
    g = g.reshape(bsz, n, CHUNK, h).transpose(0, 3, 1, 2)
    beta = beta.reshape(bsz, n, CHUNK, h).transpose(0, 3, 1, 2)
    gc = jnp.cumsum(g, axis=-1)
    causal = jnp.tril(jnp.ones((CHUNK, CHUNK), dtype=bool))
    strict = jnp.tril(jnp.ones((CHUNK, CHUNK), dtype=bool), -1)
    decay = jnp.exp(jnp.where(causal, gc[..., :, None] - gc[..., None, :], -jnp.inf))
    kk = jnp.einsum("bhncd,bhnsd->bhncs", k, k)
    a_mat = jnp.where(strict, beta[..., None] * kk * decay, 0.0)
    rhs = jnp.concatenate([v * beta[..., None], k * (beta * jnp.exp(gc))[..., None]], axis=-1)
    sol = lax.linalg.triangular_solve(a_mat, rhs, left_side=True, lower=True,
                                      unit_diagonal=True)
    u_c, w_c = sol[..., :dv], sol[..., dv:]
    qk = jnp.einsum("bhncd,bhnsd->bhncs", q, k) * decay
    q_dec = q * jnp.exp(gc)[..., None]
    k_dec = k * jnp.exp(gc[..., -1:] - gc)[..., None]
    g_last = jnp.exp(gc[..., -1])

    def step(state, inp):
        u_i, w_i, qk_i, qd_i, kd_i, gl_i = inp
        v_new = u_i - jnp.einsum("bhcd,bhde->bhce", w_i, state)
        o = (jnp.einsum("bhcd,bhde->bhce", qd_i, state)
             + jnp.einsum("bhcs,bhse->bhce", qk_i, v_new))
        state = state * gl_i[..., None, None] + jnp.einsum("bhcd,bhce->bhde", kd_i, v_new)
        return state, o

    xs = (jnp.moveaxis(u_c, 2, 0), jnp.moveaxis(w_c, 2, 0), jnp.moveaxis(qk, 2, 0),
          jnp.moveaxis(q_dec, 2, 0), jnp.moveaxis(k_dec, 2, 0), jnp.moveaxis(g_last, 2, 0))
    s0 = jnp.zeros((bsz, h, dk, dv), q.dtype)
    _, o = lax.scan(step, s0, xs)
    return o.transpose(1, 0, 3, 2, 4).reshape(bsz, l, h, dv)


def deltanet_mixer(q, k, v, z, beta_logit, a_logit, conv_w, a_log, dt_bias, norm_w):
    bsz, l, _ = q.shape
    qkv = jax.nn.silu(causal_depthwise_conv(jnp.concatenate([q, k, v], axis=-1), _f32(conv_w)))
    q, k, v = jnp.split(qkv, [D_DN, 2 * D_DN], axis=-1)
    q = l2norm(q.reshape(bsz, l, DN_HEADS, DN_HEAD_DIM))
    k = l2norm(k.reshape(bsz, l, DN_HEADS, DN_HEAD_DIM))
    v = v.reshape(bsz, l, DN_HEADS, DN_HEAD_DIM)
    beta = jax.nn.sigmoid(beta_logit)
    g = -jnp.exp(_f32(a_log)) * jax.nn.softplus(a_logit + _f32(dt_bias))
    o = gated_delta_rule(q, k, v, g, beta)
    o = rmsnorm(o, norm_w) * jax.nn.silu(z.reshape(bsz, l, DN_HEADS, DN_HEAD_DIM))
    return o.reshape(bsz, l, D_DN)


def _fwd_setup_inputs(seed: int = 0) -> dict:
    key = jax.random.key(seed)
    ks = jax.random.split(key, 24)
    f = jnp.float32
    x = jax.random.normal(ks[0], (BATCH, SEQ, D_MODEL), f)
    ln_w = 1.0 + 0.01 * jax.random.normal(ks[1], (DEPTH, D_MODEL), f)
    w_in = jax.random.normal(ks[2], (DEPTH, D_MODEL, D_IN), f) * D_MODEL ** -0.5
    n_idx = jnp.arange(S5_STATE, dtype=f)
    s5_lam_re = -0.5 + 0.01 * jax.random.normal(ks[3], (DEPTH, S5_GROUPS, S5_STATE), f)
    s5_lam_im = math.pi * n_idx + 0.01 * jax.random.normal(ks[4], (DEPTH, S5_GROUPS, S5_STATE), f)
    s5_log_step = jax.random.uniform(ks[5], (DEPTH, S5_GROUPS), f, math.log(1e-3), math.log(1e-1))
    bsc = (2.0 * S5_GROUP) ** -0.5
    s5_b_re = jax.random.normal(ks[6], (DEPTH, S5_GROUPS, S5_STATE, S5_GROUP), f) * bsc
    s5_b_im = jax.random.normal(ks[7], (DEPTH, S5_GROUPS, S5_STATE, S5_GROUP), f) * bsc
    csc = (2.0 * S5_STATE) ** -0.5
    s5_c_re = jax.random.normal(ks[8], (DEPTH, S5_GROUPS, S5_GROUP, S5_STATE), f) * csc
    s5_c_im = jax.random.normal(ks[9], (DEPTH, S5_GROUPS, S5_GROUP, S5_STATE), f) * csc
    s5_d = jax.random.normal(ks[10], (DEPTH, D_S5), f)
    s5_w_glu = jax.random.normal(ks[11], (DEPTH, D_S5, D_S5), f) * D_S5 ** -0.5
    s5_w_up = jax.random.normal(ks[12], (DEPTH, D_S5, D_MODEL), f) * D_S5 ** -0.5
    dn_conv_w = jax.random.normal(ks[13], (DEPTH, CONV_K, 3 * D_DN), f) * CONV_K ** -0.5
    dn_a_log = jnp.log(jax.random.uniform(ks[14], (DEPTH, DN_HEADS), f, 1.0, 16.0))
    dt = jnp.exp(jax.random.uniform(ks[15], (DEPTH, DN_HEADS), f, math.log(1e-3), math.log(1e-1)))
    dn_dt_bias = dt + jnp.log(-jnp.expm1(-dt))
    dn_norm_w = 1.0 + 0.01 * jax.random.normal(ks[16], (DEPTH, DN_HEAD_DIM), f)
    dn_w_up = jax.random.normal(ks[17], (DEPTH, D_DN, D_MODEL), f) * D_DN ** -0.5
    w_out = jax.random.normal(ks[18], (DEPTH, D_MODEL, D_MODEL), f) * D_MODEL ** -0.5
    final_norm_w = 1.0 + 0.01 * jax.random.normal(ks[19], (D_MODEL,), f)
    return {"x": x, "ln_w": ln_w, "w_in": w_in, "s5_lam_re": s5_lam_re, "s5_lam_im": s5_lam_im,
            "s5_log_step": s5_log_step, "s5_b_re": s5_b_re, "s5_b_im": s5_b_im,
            "s5_c_re": s5_c_re, "s5_c_im": s5_c_im, "s5_d": s5_d, "s5_w_glu": s5_w_glu,
            "s5_w_up": s5_w_up, "dn_conv_w": dn_conv_w, "dn_a_log": dn_a_log,
            "dn_dt_bias": dn_dt_bias, "dn_norm_w": dn_norm_w, "dn_w_up": dn_w_up,
            "w_out": w_out, "final_norm_w": final_norm_w}


def _fwd_reference(x, ln_w, w_in, s5_lam_re, s5_lam_im, s5_log_step, s5_b_re, s5_b_im,
              s5_c_re, s5_c_im, s5_d, s5_w_glu, s5_w_up, dn_conv_w, dn_a_log,
              dn_dt_bias, dn_norm_w, dn_w_up, w_out, final_norm_w):
    split_points = np.cumsum(np.array(IN_SPLIT_SIZES))[:-1].tolist()
    for layer in range(DEPTH):
        h = rmsnorm(x, ln_w[layer])
        proj = _f32(h @ _f32(w_in[layer]))
        (u_s, z_s, q, k, v, z_d, beta_l, a_l, gate_s, gate_d) = jnp.split(proj, split_points, axis=-1)
        y_s = s5_mixer(u_s, z_s, s5_lam_re[layer], s5_lam_im[layer], s5_log_step[layer],
                       s5_b_re[layer], s5_b_im[layer], s5_c_re[layer], s5_c_im[layer],
                       s5_d[layer], s5_w_glu[layer]) @ _f32(s5_w_up[layer])
        y_d = deltanet_mixer(q, k, v, z_d, beta_l, a_l, dn_conv_w[layer], dn_a_log[layer],
                             dn_dt_bias[layer], dn_norm_w[layer]) @ _f32(dn_w_up[layer])
        mixed = jax.nn.sigmoid(gate_s) * y_s + jax.nn.sigmoid(gate_d) * y_d
        x = x + (mixed @ _f32(w_out[layer])).astype(x.dtype)
    return rmsnorm(x, final_norm_w).astype(x.dtype)


import jax as _jax
import jax.numpy as _jnp

TWIN_FORMAT = 'train_step'
FWD_PARAMS = ['x', 'ln_w', 'w_in', 's5_lam_re', 's5_lam_im', 's5_log_step', 's5_b_re', 's5_b_im', 's5_c_re', 's5_c_im', 's5_d', 's5_w_glu', 's5_w_up', 'dn_conv_w', 'dn_a_log', 'dn_dt_bias', 'dn_norm_w', 'dn_w_up', 'w_out', 'final_norm_w']
TWIN_WEIGHTS = ['ln_w', 'w_in', 's5_lam_re', 's5_lam_im', 's5_log_step', 's5_b_re', 's5_b_im', 's5_c_re', 's5_c_im', 's5_d', 's5_w_glu', 's5_w_up', 'dn_conv_w', 'dn_a_log', 'dn_dt_bias', 'dn_norm_w', 'dn_w_up', 'w_out', 'final_norm_w']
TWIN_DIFF_INPUT = 'x'
TWIN_INPUTS = ['x', 'ln_w', 'w_in', 's5_lam_re', 's5_lam_im', 's5_log_step', 's5_b_re', 's5_b_im', 's5_c_re', 's5_c_im', 's5_d', 's5_w_glu', 's5_w_up', 'dn_conv_w', 'dn_a_log', 'dn_dt_bias', 'dn_norm_w', 'dn_w_up', 'w_out', 'final_norm_w', 'loss_target', 'm_ln_w', 'm_w_in', 'm_s5_lam_re', 'm_s5_lam_im', 'm_s5_log_step', 'm_s5_b_re', 'm_s5_b_im', 'm_s5_c_re', 'm_s5_c_im', 'm_s5_d', 'm_s5_w_glu', 'm_s5_w_up', 'm_dn_conv_w', 'm_dn_a_log', 'm_dn_dt_bias', 'm_dn_norm_w', 'm_dn_w_up', 'm_w_out', 'm_final_norm_w', 'v_ln_w', 'v_w_in', 'v_s5_lam_re', 'v_s5_lam_im', 'v_s5_log_step', 'v_s5_b_re', 'v_s5_b_im', 'v_s5_c_re', 'v_s5_c_im', 'v_s5_d', 'v_s5_w_glu', 'v_s5_w_up', 'v_dn_conv_w', 'v_dn_a_log', 'v_dn_dt_bias', 'v_dn_norm_w', 'v_dn_w_up', 'v_w_out', 'v_final_norm_w']
TWIN_OUTPUTS = ['loss', 'grad_x', 'grad_ln_w', 'grad_w_in', 'grad_s5_lam_re', 'grad_s5_lam_im', 'grad_s5_log_step', 'grad_s5_b_re', 'grad_s5_b_im', 'grad_s5_c_re', 'grad_s5_c_im', 'grad_s5_d', 'grad_s5_w_glu', 'grad_s5_w_up', 'grad_dn_conv_w', 'grad_dn_a_log', 'grad_dn_dt_bias', 'grad_dn_norm_w', 'grad_dn_w_up', 'grad_w_out', 'grad_final_norm_w', 'delta_ln_w', 'delta_w_in', 'delta_s5_lam_re', 'delta_s5_lam_im', 'delta_s5_log_step', 'delta_s5_b_re', 'delta_s5_b_im', 'delta_s5_c_re', 'delta_s5_c_im', 'delta_s5_d', 'delta_s5_w_glu', 'delta_s5_w_up', 'delta_dn_conv_w', 'delta_dn_a_log', 'delta_dn_dt_bias', 'delta_dn_norm_w', 'delta_dn_w_up', 'delta_w_out', 'delta_final_norm_w', 'new_m_ln_w', 'new_m_w_in', 'new_m_s5_lam_re', 'new_m_s5_lam_im', 'new_m_s5_log_step', 'new_m_s5_b_re', 'new_m_s5_b_im', 'new_m_s5_c_re', 'new_m_s5_c_im', 'new_m_s5_d', 'new_m_s5_w_glu', 'new_m_s5_w_up', 'new_m_dn_conv_w', 'new_m_dn_a_log', 'new_m_dn_dt_bias', 'new_m_dn_norm_w', 'new_m_dn_w_up', 'new_m_w_out', 'new_m_final_norm_w', 'new_v_ln_w', 'new_v_w_in', 'new_v_s5_lam_re', 'new_v_s5_lam_im', 'new_v_s5_log_step', 'new_v_s5_b_re', 'new_v_s5_b_im', 'new_v_s5_c_re', 'new_v_s5_c_im', 'new_v_s5_d', 'new_v_s5_w_glu', 'new_v_s5_w_up', 'new_v_dn_conv_w', 'new_v_dn_a_log', 'new_v_dn_dt_bias', 'new_v_dn_norm_w', 'new_v_dn_w_up', 'new_v_w_out', 'new_v_final_norm_w']
TWIN_LEAF_KINDS = {'loss': 'loss', 'grad_x': 'grad_x', 'grad_ln_w': 'grad_w', 'grad_w_in': 'grad_w', 'grad_s5_lam_re': 'grad_w', 'grad_s5_lam_im': 'grad_w', 'grad_s5_log_step': 'grad_w', 'grad_s5_b_re': 'grad_w', 'grad_s5_b_im': 'grad_w', 'grad_s5_c_re': 'grad_w', 'grad_s5_c_im': 'grad_w', 'grad_s5_d': 'grad_w', 'grad_s5_w_glu': 'grad_w', 'grad_s5_w_up': 'grad_w', 'grad_dn_conv_w': 'grad_w', 'grad_dn_a_log': 'grad_w', 'grad_dn_dt_bias': 'grad_w', 'grad_dn_norm_w': 'grad_w', 'grad_dn_w_up': 'grad_w', 'grad_w_out': 'grad_w', 'grad_final_norm_w': 'grad_w', 'delta_ln_w': 'delta_w', 'delta_w_in': 'delta_w', 'delta_s5_lam_re': 'delta_w', 'delta_s5_lam_im': 'delta_w', 'delta_s5_log_step': 'delta_w', 'delta_s5_b_re': 'delta_w', 'delta_s5_b_im': 'delta_w', 'delta_s5_c_re': 'delta_w', 'delta_s5_c_im': 'delta_w', 'delta_s5_d': 'delta_w', 'delta_s5_w_glu': 'delta_w', 'delta_s5_w_up': 'delta_w', 'delta_dn_conv_w': 'delta_w', 'delta_dn_a_log': 'delta_w', 'delta_dn_dt_bias': 'delta_w', 'delta_dn_norm_w': 'delta_w', 'delta_dn_w_up': 'delta_w', 'delta_w_out': 'delta_w', 'delta_final_norm_w': 'delta_w', 'new_m_ln_w': 'new_m', 'new_m_w_in': 'new_m', 'new_m_s5_lam_re': 'new_m', 'new_m_s5_lam_im': 'new_m', 'new_m_s5_log_step': 'new_m', 'new_m_s5_b_re': 'new_m', 'new_m_s5_b_im': 'new_m', 'new_m_s5_c_re': 'new_m', 'new_m_s5_c_im': 'new_m', 'new_m_s5_d': 'new_m', 'new_m_s5_w_glu': 'new_m', 'new_m_s5_w_up': 'new_m', 'new_m_dn_conv_w': 'new_m', 'new_m_dn_a_log': 'new_m', 'new_m_dn_dt_bias': 'new_m', 'new_m_dn_norm_w': 'new_m', 'new_m_dn_w_up': 'new_m', 'new_m_w_out': 'new_m', 'new_m_final_norm_w': 'new_m', 'new_v_ln_w': 'new_v', 'new_v_w_in': 'new_v', 'new_v_s5_lam_re': 'new_v', 'new_v_s5_lam_im': 'new_v', 'new_v_s5_log_step': 'new_v', 'new_v_s5_b_re': 'new_v', 'new_v_s5_b_im': 'new_v', 'new_v_s5_c_re': 'new_v', 'new_v_s5_c_im': 'new_v', 'new_v_s5_d': 'new_v', 'new_v_s5_w_glu': 'new_v', 'new_v_s5_w_up': 'new_v', 'new_v_dn_conv_w': 'new_v', 'new_v_dn_a_log': 'new_v', 'new_v_dn_dt_bias': 'new_v', 'new_v_dn_norm_w': 'new_v', 'new_v_dn_w_up': 'new_v', 'new_v_w_out': 'new_v', 'new_v_final_norm_w': 'new_v'}


def _forward(args):
    return _fwd_reference(*[args[k] for k in FWD_PARAMS])


def _output_shape():
    out = _jax.eval_shape(lambda: _forward(_fwd_setup_inputs(0)))
    return out.shape, out.dtype

N_MICROBATCH = 1
ADAM_LR = 0.001
ADAM_B1 = 0.9
ADAM_B2 = 0.999
ADAM_EPS = 1e-08
ADAM_WD = 0.01
ADAM_STEP = 10
PER_EXAMPLE_BATCH_AXIS = {'x': 0, 'loss_target': 0}
SHARED_INPUTS = []
_WEIGHT_DTYPES = {'ln_w': _jnp.float32, 'w_in': _jnp.float32, 's5_lam_re': _jnp.float32, 's5_lam_im': _jnp.float32, 's5_log_step': _jnp.float32, 's5_b_re': _jnp.float32, 's5_b_im': _jnp.float32, 's5_c_re': _jnp.float32, 's5_c_im': _jnp.float32, 's5_d': _jnp.float32, 's5_w_glu': _jnp.float32, 's5_w_up': _jnp.float32, 'dn_conv_w': _jnp.float32, 'dn_a_log': _jnp.float32, 'dn_dt_bias': _jnp.float32, 'dn_norm_w': _jnp.float32, 'dn_w_up': _jnp.float32, 'w_out': _jnp.float32, 'final_norm_w': _jnp.float32}
MOMENT_SCALE = {'ln_w': 3.391354e-02, 'w_in': 1.507790e-02, 's5_lam_re': 4.609117e-04, 's5_lam_im': 4.572611e-04, 's5_log_step': 3.702905e-01, 's5_b_re': 3.370387e-04, 's5_b_im': 3.315161e-04, 's5_c_re': 6.748737e-04, 's5_c_im': 6.689514e-04, 's5_d': 1.033120e-02, 's5_w_glu': 2.843961e-03, 's5_w_up': 6.622139e-03, 'dn_conv_w': 1.960938e-02, 'dn_a_log': 8.024636e-02, 'dn_dt_bias': 7.784419e-02, 'dn_norm_w': 7.904739e-02, 'dn_w_up': 1.885568e-02, 'w_out': 2.000089e-02, 'final_norm_w': 7.991961e+00}


def _to_microbatches(a, axis):
    t = _jnp.moveaxis(a, axis, 0)
    t = t.reshape((N_MICROBATCH, t.shape[0] // N_MICROBATCH) + t.shape[1:])
    return _jnp.moveaxis(t, 1, axis + 1)


def setup_inputs(seed: int = 0) -> dict:
    inp = _fwd_setup_inputs(seed)
    key = _jax.random.fold_in(_jax.random.key(seed), 7919)
    shape, _ = _output_shape()
    out = dict(inp)
    out["loss_target"] = _jax.random.normal(_jax.random.fold_in(key, 0), shape, _jnp.float32)
    for i, name in enumerate(TWIN_WEIGHTS):
        w = inp[name].astype(_jnp.float32)
        if MOMENT_SCALE is None:
            s = _jnp.sqrt(_jnp.mean(_jnp.square(w)) + 1e-30)
        else:
            s = MOMENT_SCALE[name]
        km, kv = _jax.random.split(_jax.random.fold_in(key, i + 1))
        out[name] = w
        out["m_" + name] = s * _jax.random.normal(km, w.shape, _jnp.float32)
        out["v_" + name] = (s * s) * _jax.random.uniform(kv, w.shape, _jnp.float32, 0.5, 1.5)
    if N_MICROBATCH > 1:
        for name, axis in PER_EXAMPLE_BATCH_AXIS.items():
            out[name] = _to_microbatches(out[name], axis)
    return {'x': out['x'], 'ln_w': out['ln_w'], 'w_in': out['w_in'], 's5_lam_re': out['s5_lam_re'], 's5_lam_im': out['s5_lam_im'], 's5_log_step': out['s5_log_step'], 's5_b_re': out['s5_b_re'], 's5_b_im': out['s5_b_im'], 's5_c_re': out['s5_c_re'], 's5_c_im': out['s5_c_im'], 's5_d': out['s5_d'], 's5_w_glu': out['s5_w_glu'], 's5_w_up': out['s5_w_up'], 'dn_conv_w': out['dn_conv_w'], 'dn_a_log': out['dn_a_log'], 'dn_dt_bias': out['dn_dt_bias'], 'dn_norm_w': out['dn_norm_w'], 'dn_w_up': out['dn_w_up'], 'w_out': out['w_out'], 'final_norm_w': out['final_norm_w'], 'loss_target': out['loss_target'], 'm_ln_w': out['m_ln_w'], 'm_w_in': out['m_w_in'], 'm_s5_lam_re': out['m_s5_lam_re'], 'm_s5_lam_im': out['m_s5_lam_im'], 'm_s5_log_step': out['m_s5_log_step'], 'm_s5_b_re': out['m_s5_b_re'], 'm_s5_b_im': out['m_s5_b_im'], 'm_s5_c_re': out['m_s5_c_re'], 'm_s5_c_im': out['m_s5_c_im'], 'm_s5_d': out['m_s5_d'], 'm_s5_w_glu': out['m_s5_w_glu'], 'm_s5_w_up': out['m_s5_w_up'], 'm_dn_conv_w': out['m_dn_conv_w'], 'm_dn_a_log': out['m_dn_a_log'], 'm_dn_dt_bias': out['m_dn_dt_bias'], 'm_dn_norm_w': out['m_dn_norm_w'], 'm_dn_w_up': out['m_dn_w_up'], 'm_w_out': out['m_w_out'], 'm_final_norm_w': out['m_final_norm_w'], 'v_ln_w': out['v_ln_w'], 'v_w_in': out['v_w_in'], 'v_s5_lam_re': out['v_s5_lam_re'], 'v_s5_lam_im': out['v_s5_lam_im'], 'v_s5_log_step': out['v_s5_log_step'], 'v_s5_b_re': out['v_s5_b_re'], 'v_s5_b_im': out['v_s5_b_im'], 'v_s5_c_re': out['v_s5_c_re'], 'v_s5_c_im': out['v_s5_c_im'], 'v_s5_d': out['v_s5_d'], 'v_s5_w_glu': out['v_s5_w_glu'], 'v_s5_w_up': out['v_s5_w_up'], 'v_dn_conv_w': out['v_dn_conv_w'], 'v_dn_a_log': out['v_dn_a_log'], 'v_dn_dt_bias': out['v_dn_dt_bias'], 'v_dn_norm_w': out['v_dn_norm_w'], 'v_dn_w_up': out['v_dn_w_up'], 'v_w_out': out['v_w_out'], 'v_final_norm_w': out['v_final_norm_w']}


def _loss(weights, diff, rest, loss_target):
    with _jax.named_scope("forward"):
        args = {**rest, TWIN_DIFF_INPUT: diff, **{k: w.astype(_WEIGHT_DTYPES[k]) for k, w in weights.items()}}
        y = _forward(args)
    with _jax.named_scope("loss_head"):
        err = _jnp.square(y.astype(_jnp.float32) - loss_target)
        return 0.5 * _jnp.sum(_jnp.mean(err, axis=-1)) if err.ndim else 0.5 * err


def _adamw(w, g, m, v):
    m = ADAM_B1 * m + (1.0 - ADAM_B1) * g
    v = ADAM_B2 * v + (1.0 - ADAM_B2) * _jnp.square(g)
    m_hat = m / (1.0 - ADAM_B1 ** ADAM_STEP)
    v_hat = v / (1.0 - ADAM_B2 ** ADAM_STEP)
    delta = -ADAM_LR * (m_hat / (_jnp.sqrt(v_hat) + ADAM_EPS) + ADAM_WD * w)
    return delta, m, v


def reference(x, ln_w, w_in, s5_lam_re, s5_lam_im, s5_log_step, s5_b_re, s5_b_im, s5_c_re, s5_c_im, s5_d, s5_w_glu, s5_w_up, dn_conv_w, dn_a_log, dn_dt_bias, dn_norm_w, dn_w_up, w_out, final_norm_w, loss_target, m_ln_w, m_w_in, m_s5_lam_re, m_s5_lam_im, m_s5_log_step, m_s5_b_re, m_s5_b_im, m_s5_c_re, m_s5_c_im, m_s5_d, m_s5_w_glu, m_s5_w_up, m_dn_conv_w, m_dn_a_log, m_dn_dt_bias, m_dn_norm_w, m_dn_w_up, m_w_out, m_final_norm_w, v_ln_w, v_w_in, v_s5_lam_re, v_s5_lam_im, v_s5_log_step, v_s5_b_re, v_s5_b_im, v_s5_c_re, v_s5_c_im, v_s5_d, v_s5_w_glu, v_s5_w_up, v_dn_conv_w, v_dn_a_log, v_dn_dt_bias, v_dn_norm_w, v_dn_w_up, v_w_out, v_final_norm_w):
    given = dict(x=x, ln_w=ln_w, w_in=w_in, s5_lam_re=s5_lam_re, s5_lam_im=s5_lam_im, s5_log_step=s5_log_step, s5_b_re=s5_b_re, s5_b_im=s5_b_im, s5_c_re=s5_c_re, s5_c_im=s5_c_im, s5_d=s5_d, s5_w_glu=s5_w_glu, s5_w_up=s5_w_up, dn_conv_w=dn_conv_w, dn_a_log=dn_a_log, dn_dt_bias=dn_dt_bias, dn_norm_w=dn_norm_w, dn_w_up=dn_w_up, w_out=w_out, final_norm_w=final_norm_w, loss_target=loss_target, m_ln_w=m_ln_w, m_w_in=m_w_in, m_s5_lam_re=m_s5_lam_re, m_s5_lam_im=m_s5_lam_im, m_s5_log_step=m_s5_log_step, m_s5_b_re=m_s5_b_re, m_s5_b_im=m_s5_b_im, m_s5_c_re=m_s5_c_re, m_s5_c_im=m_s5_c_im, m_s5_d=m_s5_d, m_s5_w_glu=m_s5_w_glu, m_s5_w_up=m_s5_w_up, m_dn_conv_w=m_dn_conv_w, m_dn_a_log=m_dn_a_log, m_dn_dt_bias=m_dn_dt_bias, m_dn_norm_w=m_dn_norm_w, m_dn_w_up=m_dn_w_up, m_w_out=m_w_out, m_final_norm_w=m_final_norm_w, v_ln_w=v_ln_w, v_w_in=v_w_in, v_s5_lam_re=v_s5_lam_re, v_s5_lam_im=v_s5_lam_im, v_s5_log_step=v_s5_log_step, v_s5_b_re=v_s5_b_re, v_s5_b_im=v_s5_b_im, v_s5_c_re=v_s5_c_re, v_s5_c_im=v_s5_c_im, v_s5_d=v_s5_d, v_s5_w_glu=v_s5_w_glu, v_s5_w_up=v_s5_w_up, v_dn_conv_w=v_dn_conv_w, v_dn_a_log=v_dn_a_log, v_dn_dt_bias=v_dn_dt_bias, v_dn_norm_w=v_dn_norm_w, v_dn_w_up=v_dn_w_up, v_w_out=v_w_out, v_final_norm_w=v_final_norm_w)
    weights = {n: given[n] for n in TWIN_WEIGHTS}
    shared = {n: given[n] for n in SHARED_INPUTS}
    per_example = {n: given[n] for n in ['x']}
    grad_fn = _jax.value_and_grad(_loss, argnums=(0, 1))

    def one_microbatch(ex, loss_target):
        ex = dict(ex)
        diff = ex.pop(TWIN_DIFF_INPUT)
        return grad_fn(weights, diff, {**shared, **ex}, loss_target)

    if N_MICROBATCH == 1:
        loss, (grad_w, grad_x) = one_microbatch(per_example, given["loss_target"])
    else:
        def body(carry, xs):
            loss_sum, grad_sum = carry
            l_k, (gw_k, gx_k) = one_microbatch(xs[0], xs[1])
            with _jax.named_scope("update"):
                return (loss_sum + l_k, _jax.tree.map(_jnp.add, grad_sum, gw_k)), gx_k

        init = (_jnp.zeros((), _jnp.float32), _jax.tree.map(_jnp.zeros_like, weights))
        (loss, grad_w), grad_x = _jax.lax.scan(body, init, (per_example, given["loss_target"]))
    with _jax.named_scope("update"):
        delta_w, new_m, new_v = {}, {}, {}
        for n in TWIN_WEIGHTS:
            delta_w[n], new_m[n], new_v[n] = _adamw(weights[n], grad_w[n], given["m_" + n], given["v_" + n])
    return (loss, grad_x, *[grad_w[n] for n in TWIN_WEIGHTS], *[delta_w[n] for n in TWIN_WEIGHTS],
            *[new_m[n] for n in TWIN_WEIGHTS], *[new_v[n] for n in TWIN_WEIGHTS])
```

```python
import functools
import math

import jax
import jax.numpy as jnp
from jax import lax
from jax.experimental import pallas as pl
from jax.experimental.pallas import tpu as pltpu

F32 = jnp.float32
BF16 = jnp.bfloat16
HI = lax.Precision.HIGHEST
MESH = pl.DeviceIdType.MESH
ANY = pl.BlockSpec(memory_space=pl.ANY)

EPS = 1e-6
D_MODEL = 2048
D_S5 = 1024
S5_GROUP = 16
S5_GROUPS = 64
S5_STATE = 64
S5_BLOCKS = 8
S5_SEG = 8
DN_HEADS = 8
DN_HEAD_DIM = 128
D_DN = 1024
CONV_K = 4
CHUNK = 64
D_IN = 10256
D_IN_PAD = 10368
OFF_US, OFF_ZS, OFF_Q, OFF_K, OFF_V, OFF_ZD, OFF_GS, OFF_GD, OFF_B = 0, 1024, 2048, 3072, 4096, 5120, 6144, 8192, 10240
N_CHIPS = 4
N_DEV = 8
VMEM_LIMIT = 56 * 1024 * 1024

ADAM_LR = 0.001
ADAM_B1 = 0.9
ADAM_B2 = 0.999
ADAM_EPS = 1e-08
ADAM_WD = 0.01
ADAM_STEP = 10


def _cp(sem=None):
    return pltpu.CompilerParams(dimension_semantics=sem, vmem_limit_bytes=VMEM_LIMIT)


def _sds(shape, dtype=F32):
    return jax.ShapeDtypeStruct(tuple(shape), dtype)


def _sigmoid(x):
    return 1.0 / (1.0 + jnp.exp(-x))


def _silu(x):
    return x * _sigmoid(x)


def _dsilu(x):
    s = _sigmoid(x)
    return s * (1.0 + x * (1.0 - s))


def _mm(a, b, *, name, ta=False, tb=False, out_dtype=F32, tm=512, tn=512, tk=2048, shard_out=False):
    if ta:
        K, M = a.shape
    else:
        M, K = a.shape
    if tb:
        N, K2 = b.shape
    else:
        K2, N = b.shape
    assert K == K2, (a.shape, b.shape)
    tm, tn, tk = min(tm, M), min(tn, N), min(tk, K)
    assert M % tm == 0 and N % tn == 0 and K % tk == 0, (M, N, K, tm, tn, tk)
    nk = K // tk
    dims = (((0 if ta else 1,), (1 if tb else 0,)), ((), ()))

    def body(a_ref, b_ref, o_ref, acc_ref):
        k = pl.program_id(2)
        p = lax.dot_general(a_ref[...].astype(BF16), b_ref[...].astype(BF16), dims, preferred_element_type=F32)

        @pl.when(k == 0)
        def _():
            acc_ref[...] = p

        @pl.when(k > 0)
        def _():
            acc_ref[...] += p

        @pl.when(k == nk - 1)
        def _():
            o_ref[...] = acc_ref[...].astype(out_dtype).reshape(o_ref.shape)

    a_spec = pl.BlockSpec((tk, tm), lambda i, j, k: (k, i)) if ta else pl.BlockSpec((tm, tk), lambda i, j, k: (i, k))
    b_spec = pl.BlockSpec((tn, tk), lambda i, j, k: (j, k)) if tb else pl.BlockSpec((tk, tn), lambda i, j, k: (k, j))
    if shard_out:
        o_spec = pl.BlockSpec((1, tm, tn), lambda i, j, k: (j, i, 0))
        o_shape = _sds((N // tn, M, tn), out_dtype)
    else:
        o_spec = pl.BlockSpec((tm, tn), lambda i, j, k: (i, j))
        o_shape = _sds((M, N), out_dtype)
    return pl.pallas_call(
        body, name=name, grid=(M // tm, N // tn, nk), in_specs=[a_spec, b_spec], out_specs=o_spec, out_shape=o_shape,
        scratch_shapes=[pltpu.VMEM((tm, tn), F32)], compiler_params=_cp(("parallel", "parallel", "arbitrary")),
    )(a, b)


def _ln_fwd(x, w):
    L, D = x.shape
    tm = min(256, L)

    def body(x_ref, w_ref, h_ref, r_ref):
        xv = x_ref[...]
        r = lax.rsqrt(jnp.mean(xv * xv, axis=-1, keepdims=True) + EPS)
        h_ref[...] = (xv * r * w_ref[...]).astype(BF16)
        r_ref[...] = r

    return pl.pallas_call(
        body, name="ln_fwd", grid=(L // tm,),
        in_specs=[pl.BlockSpec((tm, D), lambda i: (i, 0)), pl.BlockSpec((1, D), lambda i: (0, 0))],
        out_specs=[pl.BlockSpec((tm, D), lambda i: (i, 0)), pl.BlockSpec((tm, 1), lambda i: (i, 0))],
        out_shape=[_sds((L, D), BF16), _sds((L, 1))], compiler_params=_cp(("parallel",)),
    )(x, w)


def _ln_bwd(x, r, w, dh, dx2):
    L, D = x.shape
    tm = min(256, L)

    def body(x_ref, r_ref, w_ref, dh_ref, dx2_ref, dx_ref, dw_ref):
        i = pl.program_id(0)
        xv, rv, dhv = x_ref[...], r_ref[...], dh_ref[...]
        t = dhv * w_ref[...]
        m = jnp.mean(t * xv, axis=-1, keepdims=True)
        dx_ref[...] = dx2_ref[...] + rv * t - xv * (rv * rv * rv) * m
        part = jnp.sum(dhv * xv * rv, axis=0, keepdims=True)

        @pl.when(i == 0)
        def _():
            dw_ref[...] = part

        @pl.when(i > 0)
        def _():
            dw_ref[...] += part

    row = pl.BlockSpec((tm, D), lambda i: (i, 0))
    return pl.pallas_call(
        body, name="ln_bwd", grid=(L // tm,),
        in_specs=[row, pl.BlockSpec((tm, 1), lambda i: (i, 0)), pl.BlockSpec((1, D), lambda i: (0, 0)), row, row],
        out_specs=[row, pl.BlockSpec((1, D), lambda i: (0, 0))],
        out_shape=[_sds((L, D)), _sds((1, D))], compiler_params=_cp(("arbitrary",)),
    )(x, r, w, dh, dx2)


def _s5_param_math(lam_re, lam_im, log_step, b_re, b_im, expand):
    step = jnp.exp(log_step)
    mag = jnp.exp(lam_re * step)
    abar_re = mag * jnp.cos(lam_im * step)
    abar_im = mag * jnp.sin(lam_im * step)
    den = lam_re * lam_re + lam_im * lam_im
    xr = abar_re - 1.0
    f_re = (xr * lam_re + abar_im * lam_im) / den
    f_im = (abar_im * lam_re - xr * lam_im) / den
    fe_re = jnp.dot(f_re, expand, precision=HI, preferred_element_type=F32)
    fe_im = jnp.dot(f_im, expand, precision=HI, preferred_element_type=F32)
    bb_re = fe_re * b_re - fe_im * b_im
    bb_im = fe_re * b_im + fe_im * b_re
    return abar_re, abar_im, bb_re, bb_im


def _s5_expand():
    p = lax.broadcasted_iota(jnp.int32, (S5_STATE, S5_STATE * S5_GROUP), 0)
    q = lax.broadcasted_iota(jnp.int32, (S5_STATE, S5_STATE * S5_GROUP), 1)
    return (q // S5_GROUP == p).astype(F32)


def _s5_param_fwd(lam_re, lam_im, log_step, b_re, b_im):
    G, P = lam_re.shape

    def body(lr, li, ls, br, bi, ar_o, ai_o, bbr_o, bbi_o):
        outs = _s5_param_math(lr[...], li[...], ls[...], br[...], bi[...], _s5_expand())
        for o, v in zip((ar_o, ai_o, bbr_o, bbi_o), outs):
            o[...] = v

    return pl.pallas_call(
        body, name="s5_param_fwd",
        out_shape=[_sds((G, P)), _sds((G, P)), _sds(b_re.shape), _sds(b_re.shape)], compiler_params=_cp(),
    )(lam_re, lam_im, log_step, b_re, b_im)


def _s5_param_bwd(lam_re, lam_im, log_step, b_re, b_im, dar, dai, dbbr, dbbi):
    G, P = lam_re.shape

    def body(lr, li, ls, br, bi, g0, g1, g2, g3, dlr, dli, dls, dbr, dbi):
        ex = _s5_expand()
        _, f = jax.vjp(lambda a, b, c, d, e: _s5_param_math(a, b, c, d, e, ex), lr[...], li[...], ls[...], br[...], bi[...])
        grads = f((g0[...], g1[...], g2[...], g3[...]))
        for o, v in zip((dlr, dli, dls, dbr, dbi), grads):
            o[...] = v

    return pl.pallas_call(
        body, name="s5_param_bwd",
        out_shape=[_sds((G, P)), _sds((G, P)), _sds((G, 1)), _sds(b_re.shape), _sds(b_re.shape)], compiler_params=_cp(),
    )(lam_re, lam_im, log_step, b_re, b_im, dar, dai, dbbr, dbbi)


def _seg_scan(ar, ai, re_ref, im_ref, pwr_ref, pwi_ref, end_r_ref, end_i_ref, L, row0, reverse):
    S = L // S5_SEG
    NB = re_ref.shape[0]
    LN = 128
    ars = [ar[:, b * LN:(b + 1) * LN] for b in range(NB)]
    ais = [ai[:, b * LN:(b + 1) * LN] for b in range(NB)]
    ar8 = [jnp.broadcast_to(v, (S5_SEG, LN)) for v in ars]
    ai8 = [jnp.broadcast_to(v, (S5_SEG, LN)) for v in ais]

    def step(j, carry):
        r = (S - 1 - j) if reverse else j
        rows = pl.ds(row0 + r, S5_SEG, stride=S)
        out = []
        for b in range(NB):
            sr, si, pr, pi = carry[b]
            nr = ar8[b] * sr - ai8[b] * si + re_ref.at[b][rows, :]
            ni = ar8[b] * si + ai8[b] * sr + im_ref.at[b][rows, :]
            re_ref.at[b][rows, :] = nr
            im_ref.at[b][rows, :] = ni
            qr = ars[b] * pr - ais[b] * pi
            qi = ars[b] * pi + ais[b] * pr
            pwr_ref[b, pl.ds(r, 1), :] = qr
            pwi_ref[b, pl.ds(r, 1), :] = qi
            out.append((nr, ni, qr, qi))
        return tuple(out)

    z8 = jnp.zeros((S5_SEG, LN), F32)
    init = tuple((z8, z8, jnp.ones((1, LN), F32), jnp.zeros((1, LN), F32)) for _ in range(NB))
    fin = lax.fori_loop(0, S, step, init)
    order = range(S5_SEG - 2, -1, -1) if reverse else range(1, S5_SEG)
    first = S5_SEG - 1 if reverse else 0
    for b in range(NB):
        sr, si, pr, pi = fin[b]
        end_r_ref[b] = sr
        end_i_ref[b] = si
        cr, ci = end_r_ref[b, pl.ds(first, 1), :], end_i_ref[b, pl.ds(first, 1), :]
        for i in order:
            rows = pl.ds(row0 + i * S, S)
            pwr, pwi = pwr_ref[b], pwi_ref[b]
            re_ref.at[b][rows, :] += pwr * cr - pwi * ci
            im_ref.at[b][rows, :] += pwr * ci + pwi * cr
            er, ei = end_r_ref[b, pl.ds(i, 1), :], end_i_ref[b, pl.ds(i, 1), :]
            cr, ci = er + pr * cr - pi * ci, ei + pr * ci + pi * cr


def _s5_scratch(L, cs, pad):
    S, NB = L // S5_SEG, cs // 128
    return [pltpu.VMEM((NB, L + pad, 128), F32), pltpu.VMEM((NB, L + pad, 128), F32), pltpu.VMEM((NB, S, 128), F32),
            pltpu.VMEM((NB, S, 128), F32), pltpu.VMEM((NB, S5_SEG, 128), F32), pltpu.VMEM((NB, S5_SEG, 128), F32)]


def _s5_core_fwd(proj, wbr, wbi, a_re, a_im, cbr, cbi):
    L = proj.shape[0]
    nb, ci, cs = wbr.shape
    NB = cs // 128

    def body(u_ref, wbr_ref, wbi_ref, ar_ref, ai_ref, cbr_ref, cbi_ref, y_ref, sr, si, pwr, pwi, er, ei):
        u = u_ref[...].astype(BF16)
        for b in range(NB):
            lanes = pl.ds(b * 128, 128)
            sr[b] = jnp.dot(u, wbr_ref[0, :, lanes], preferred_element_type=F32)
            si[b] = jnp.dot(u, wbi_ref[0, :, lanes], preferred_element_type=F32)
        _seg_scan(ar_ref[...], ai_ref[...], sr, si, pwr, pwi, er, ei, L, 0, False)
        y = jnp.zeros((L, ci), F32)
        for b in range(NB):
            lanes = pl.ds(b * 128, 128)
            y = y + (jnp.dot(sr[b].astype(BF16), cbr_ref[0, lanes, :], preferred_element_type=F32)
                     - jnp.dot(si[b].astype(BF16), cbi_ref[0, lanes, :], preferred_element_type=F32))
        y_ref[...] = y

    wspec = pl.BlockSpec((1, ci, cs), lambda j: (j, 0, 0))
    aspec = pl.BlockSpec((1, cs), lambda j: (0, j))
    cspec = pl.BlockSpec((1, cs, ci), lambda j: (j, 0, 0))
    return pl.pallas_call(
        body, name="s5_core_fwd", grid=(nb,),
        in_specs=[pl.BlockSpec((L, ci), lambda j: (0, OFF_US // ci + j)), wspec, wspec, aspec, aspec, cspec, cspec],
        out_specs=pl.BlockSpec((L, ci), lambda j: (0, j)), out_shape=_sds((L, nb * ci)),
        scratch_shapes=_s5_scratch(L, cs, 0), compiler_params=_cp(("arbitrary",)),
    )(proj, wbr, wbi, a_re, a_im, cbr, cbi)


def _s5_core_bwd(proj, wbr, wbi, a_re, a_im, cbr, cbi, dyc, du1):
    L = proj.shape[0]
    nb, ci, cs = wbr.shape
    NB = cs // 128
    PAD = 8
    RC = min(256, L)

    def body(u_ref, wbr_ref, wbi_ref, ar_ref, ai_ref, cbr_ref, cbi_ref, dy_ref, du1_ref,
             du_ref, dwbr_ref, dwbi_ref, dcbr_ref, dcbi_ref, dar_ref, dai_ref, sr, si, pwr, pwi, er, ei, lr, li):
        tn = (((0,), (0,)), ((), ()))
        nt = (((1,), (1,)), ((), ()))
        u = u_ref[...].astype(BF16)
        dy = dy_ref[...].astype(BF16)
        ar, ai = ar_ref[...], ai_ref[...]
        for b in range(NB):
            lanes = pl.ds(b * 128, 128)
            sr[b, pl.ds(0, PAD), :] = jnp.zeros((PAD, 128), F32)
            si[b, pl.ds(0, PAD), :] = jnp.zeros((PAD, 128), F32)
            sr[b, pl.ds(PAD, L), :] = jnp.dot(u, wbr_ref[0, :, lanes], preferred_element_type=F32)
            si[b, pl.ds(PAD, L), :] = jnp.dot(u, wbi_ref[0, :, lanes], preferred_element_type=F32)
        _seg_scan(ar, ai, sr, si, pwr, pwi, er, ei, L, PAD, False)
        for b in range(NB):
            lanes = pl.ds(b * 128, 128)
            lr[b] = lax.dot_general(dy, cbr_ref[0, lanes, :], nt, preferred_element_type=F32)
            li[b] = -lax.dot_general(dy, cbi_ref[0, lanes, :], nt, preferred_element_type=F32)
            dcbr_ref[0, lanes, :] = lax.dot_general(sr[b, pl.ds(PAD, L), :].astype(BF16), dy, tn, preferred_element_type=F32)
            dcbi_ref[0, lanes, :] = -lax.dot_general(si[b, pl.ds(PAD, L), :].astype(BF16), dy, tn, preferred_element_type=F32)
        _seg_scan(ar, -ai, lr, li, pwr, pwi, er, ei, L, 0, True)
        du = du1_ref[...]
        for b in range(NB):
            lanes = pl.ds(b * 128, 128)
            dar = jnp.zeros((1, 128), F32)
            dai = jnp.zeros((1, 128), F32)
            for i in range(L // RC):
                pr_, pi_ = sr[b, pl.ds(PAD - 1 + i * RC, RC), :], si[b, pl.ds(PAD - 1 + i * RC, RC), :]
                gr, gi = lr[b, pl.ds(i * RC, RC), :], li[b, pl.ds(i * RC, RC), :]
                dar = dar + jnp.sum(gr * pr_ + gi * pi_, axis=0, keepdims=True)
                dai = dai + jnp.sum(gi * pr_ - gr * pi_, axis=0, keepdims=True)
            dar_ref[:, lanes] = dar
            dai_ref[:, lanes] = dai
            gr, gi = lr[b].astype(BF16), li[b].astype(BF16)
            du = du + (lax.dot_general(gr, wbr_ref[0, :, lanes], nt, preferred_element_type=F32)
                       + lax.dot_general(gi, wbi_ref[0, :, lanes], nt, preferred_element_type=F32))
            dwbr_ref[0, :, lanes] = lax.dot_general(u, gr, tn, preferred_element_type=F32)
            dwbi_ref[0, :, lanes] = lax.dot_general(u, gi, tn, preferred_element_type=F32)
        du_ref[...] = du.astype(BF16)

    wspec = pl.BlockSpec((1, ci, cs), lambda j: (j, 0, 0))
    aspec = pl.BlockSpec((1, cs), lambda j: (0, j))
    cspec = pl.BlockSpec((1, cs, ci), lambda j: (j, 0, 0))
    col = pl.BlockSpec((L, ci), lambda j: (0, j))
    return pl.pallas_call(
        body, name="s5_core_bwd", grid=(nb,),
        in_specs=[pl.BlockSpec((L, ci), lambda j: (0, OFF_US // ci + j)), wspec, wspec, aspec, aspec, cspec, cspec, col, col],
        out_specs=[col, wspec, wspec, cspec, cspec, aspec, aspec],
        out_shape=[_sds((L, nb * ci), BF16), _sds(wbr.shape), _sds(wbr.shape), _sds(cbr.shape), _sds(cbr.shape),
                   _sds((1, nb * cs)), _sds((1, nb * cs))],
        scratch_shapes=_s5_scratch(L, cs, PAD) + [pltpu.VMEM((NB, L, 128), F32), pltpu.VMEM((NB, L, 128), F32)],
        compiler_params=_cp(("arbitrary",)),
    )(proj, wbr, wbi, a_re, a_im, cbr, cbi, dyc, du1)


def _s5_post_math(yc, u, z, d, wg):
    y = yc + d * u
    y1 = jax.nn.gelu(y)
    t = jnp.dot(y1.astype(BF16), wg, preferred_element_type=F32)
    sg = _sigmoid(t)
    return y, y1, sg


def _s5_post_fwd(yc, proj, d, wg):
    L, W = yc.shape
    tm = min(256, L)

    def body(yc_ref, u_ref, z_ref, d_ref, wg_ref, o_ref):
        _, y1, sg = _s5_post_math(yc_ref[...], u_ref[...], z_ref[...], d_ref[...], wg_ref[...])
        o_ref[...] = (y1 * sg * _silu(z_ref[...])).astype(BF16)

    row = pl.BlockSpec((tm, W), lambda i: (i, 0))
    return pl.pallas_call(
        body, name="s5_post_fwd", grid=(L // tm,),
        in_specs=[row, pl.BlockSpec((tm, W), lambda i: (i, OFF_US // W)), pl.BlockSpec((tm, W), lambda i: (i, OFF_ZS // W)),
                  pl.BlockSpec((1, W), lambda i: (0, 0)), pl.BlockSpec((W, W), lambda i: (0, 0))],
        out_specs=row, out_shape=_sds((L, W), BF16), compiler_params=_cp(("parallel",)),
    )(yc, proj, proj, d, wg)


def _s5_post_bwd(yc, proj, d, wg, dout):
    L, W = yc.shape
    tm = min(256, L)

    def body(yc_ref, u_ref, z_ref, d_ref, wg_ref, do_ref, dyc_ref, du_ref, dz_ref, dd_ref, dwg_ref):
        i = pl.program_id(0)
        u, z, d_, wgv = u_ref[...], z_ref[...], d_ref[...], wg_ref[...]
        y, y1, sg = _s5_post_math(yc_ref[...], u, z, d_, wgv)
        dout_ = do_ref[...]
        y2 = y1 * sg
        dy2 = dout_ * _silu(z)
        dz_ref[...] = (dout_ * y2 * _dsilu(z)).astype(BF16)
        dt = (dy2 * y1 * sg * (1.0 - sg)).astype(BF16)
        dy1 = dy2 * sg + lax.dot_general(dt, wgv, (((1,), (1,)), ((), ())), preferred_element_type=F32)
        _, gelu_vjp = jax.vjp(jax.nn.gelu, y)
        dy = gelu_vjp(dy1)[0]
        dyc_ref[...] = dy
        du_ref[...] = dy * d_
        dd_part = jnp.sum(dy * u, axis=0, keepdims=True)
        dwg_part = lax.dot_general(y1.astype(BF16), dt, (((0,), (0,)), ((), ())), preferred_element_type=F32)

        @pl.when(i == 0)
        def _():
            dd_ref[...] = dd_part
            dwg_ref[...] = dwg_part

        @pl.when(i > 0)
        def _():
            dd_ref[...] += dd_part
            dwg_ref[...] += dwg_part

    row = pl.BlockSpec((tm, W), lambda i: (i, 0))
    return pl.pallas_call(
        body, name="s5_post_bwd", grid=(L // tm,),
        in_specs=[row, pl.BlockSpec((tm, W), lambda i: (i, OFF_US // W)), pl.BlockSpec((tm, W), lambda i: (i, OFF_ZS // W)),
                  pl.BlockSpec((1, W), lambda i: (0, 0)), pl.BlockSpec((W, W), lambda i: (0, 0)), row],
        out_specs=[row, row, row, pl.BlockSpec((1, W), lambda i: (0, 0)), pl.BlockSpec((W, W), lambda i: (0, 0))],
        out_shape=[_sds((L, W)), _sds((L, W)), _sds((L, W), BF16), _sds((1, W)), _sds((W, W))],
        compiler_params=_cp(("arbitrary",)),
    )(yc, proj, proj, d, wg, dout)


def _shift_down(x, s):
    if s == 0:
        return x
    rows = lax.broadcasted_iota(jnp.int32, x.shape, 0)
    return jnp.where(rows >= s, pltpu.roll(x, s, 0), 0.0)


def _shift_up(x, s):
    if s == 0:
        return x
    L = x.shape[0]
    rows = lax.broadcasted_iota(jnp.int32, x.shape, 0)
    return jnp.where(rows < L - s, pltpu.roll(x, L - s, 0), 0.0)


def _conv_pre(x, w):
    acc = w[CONV_K - 1:CONV_K, :] * x
    for s in range(1, CONV_K):
        acc = acc + w[CONV_K - 1 - s:CONV_K - s, :] * _shift_down(x, s)
    return acc


def _dn_conv_fwd(proj, conv_w):
    L = proj.shape[0]
    W = DN_HEAD_DIM
    nq = 2 * DN_HEADS

    def body(x_ref, w_ref, o_ref):
        j = pl.program_id(0)
        act = _silu(_conv_pre(x_ref[...], w_ref[...]))
        r = lax.rsqrt(jnp.sum(act * act, axis=-1, keepdims=True) + EPS)
        scale = jnp.where(j < DN_HEADS, DN_HEAD_DIM ** -0.5, 1.0)
        o_ref[...] = jnp.where(j < nq, act * r * scale, act)

    return pl.pallas_call(
        body, name="dn_conv_fwd", grid=(3 * DN_HEADS,),
        in_specs=[pl.BlockSpec((L, W), lambda j: (0, OFF_Q // W + j)), pl.BlockSpec((CONV_K, W), lambda j: (0, j))],
        out_specs=pl.BlockSpec((L, W), lambda j: (0, j)), out_shape=_sds((L, 3 * D_DN)), compiler_params=_cp(("parallel",)),
    )(proj, conv_w)


def _dn_conv_bwd(proj, conv_w, dout):
    L = proj.shape[0]
    W = DN_HEAD_DIM
    nq = 2 * DN_HEADS

    def body(x_ref, w_ref, do_ref, dx_ref, dw_ref):
        j = pl.program_id(0)
        x, w, dout_ = x_ref[...], w_ref[...], do_ref[...]
        pre = _conv_pre(x, w)
        act = _silu(pre)
        r = lax.rsqrt(jnp.sum(act * act, axis=-1, keepdims=True) + EPS)
        scale = jnp.where(j < DN_HEADS, DN_HEAD_DIM ** -0.5, 1.0)
        g = dout_ * scale
        dact_n = r * g - act * (r * r * r) * jnp.sum(g * act, axis=-1, keepdims=True)
        dact = jnp.where(j < nq, dact_n, dout_)
        dpre = dact * _dsilu(pre)
        dx = w[CONV_K - 1:CONV_K, :] * dpre
        for s in range(1, CONV_K):
            dx = dx + w[CONV_K - 1 - s:CONV_K - s, :] * _shift_up(dpre, s)
        dx_ref[...] = dx.astype(BF16)
        for s in range(CONV_K):
            dw_ref[pl.ds(CONV_K - 1 - s, 1), :] = jnp.sum(dpre * _shift_down(x, s), axis=0, keepdims=True)

    col = pl.BlockSpec((L, W), lambda j: (0, j))
    wsp = pl.BlockSpec((CONV_K, W), lambda j: (0, j))
    return pl.pallas_call(
        body, name="dn_conv_bwd", grid=(3 * DN_HEADS,),
        in_specs=[pl.BlockSpec((L, W), lambda j: (0, OFF_Q // W + j)), wsp, col], out_specs=[col, wsp],
        out_shape=[_sds((L, 3 * D_DN), BF16), _sds((CONV_K, 3 * D_DN))], compiler_params=_cp(("parallel",)),
    )(proj, conv_w, dout)


def _softplus(x):
    return jnp.maximum(x, 0.0) + jnp.log(1.0 + jnp.exp(-jnp.abs(x)))


def _dn_gates_fwd(proj, alog, dtb):
    L = proj.shape[0]
    W = 128

    def body(p_ref, al_ref, db_ref, o_ref):
        p = p_ref[...]
        lane = lax.broadcasted_iota(jnp.int32, p.shape, 1)
        g = -jnp.exp(al_ref[...]) * _softplus(p + db_ref[...])
        o_ref[...] = jnp.where(lane < DN_HEADS, _sigmoid(p), jnp.where(lane < 2 * DN_HEADS, g, 0.0))

    return pl.pallas_call(
        body, name="dn_gates_fwd", grid=(1,),
        in_specs=[pl.BlockSpec((L, W), lambda i: (0, OFF_B // W)), pl.BlockSpec((1, W), lambda i: (0, 0)),
                  pl.BlockSpec((1, W), lambda i: (0, 0))],
        out_specs=pl.BlockSpec((L, W), lambda i: (0, 0)), out_shape=_sds((L, W)), compiler_params=_cp(("arbitrary",)),
    )(proj, alog, dtb)


def _dn_gates_bwd(proj, alog, dtb, dgates):
    L = proj.shape[0]
    W = 128

    def body(p_ref, al_ref, db_ref, dg_ref, dp_ref, dal_ref, ddb_ref):
        p, dg = p_ref[...], dg_ref[...]
        lane = lax.broadcasted_iota(jnp.int32, p.shape, 1)
        is_g = jnp.logical_and(lane >= DN_HEADS, lane < 2 * DN_HEADS)
        beta = _sigmoid(p)
        na = -jnp.exp(al_ref[...])
        xs = p + db_ref[...]
        dsp = dg * na * _sigmoid(xs)
        dp_ref[...] = jnp.where(lane < DN_HEADS, dg * beta * (1.0 - beta), jnp.where(is_g, dsp, 0.0)).astype(BF16)
        dal_ref[...] = jnp.sum(jnp.where(is_g, dg * na * _softplus(xs), 0.0), axis=0, keepdims=True)
        ddb_ref[...] = jnp.sum(jnp.where(is_g, dsp, 0.0), axis=0, keepdims=True)

    one = pl.BlockSpec((1, W), lambda i: (0, 0))
    full = pl.BlockSpec((L, W), lambda i: (0, 0))
    return pl.pallas_call(
        body, name="dn_gates_bwd", grid=(1,),
        in_specs=[pl.BlockSpec((L, W), lambda i: (0, OFF_B // W)), one, one, full], out_specs=[full, one, one],
        out_shape=[_sds((L, W), BF16), _sds((1, W)), _sds((1, W))], compiler_params=_cp(("arbitrary",)),
    )(proj, alog, dtb, dgates)


def _bdot(a, b, dims):
    return lax.dot_general(a.astype(BF16), b.astype(BF16), (dims, ((), ())), preferred_element_type=F32)


_NN, _NT, _TN = ((1,), (0,)), ((1,), (1,)), ((0,), (0,))


@jax.custom_vjp
def _mm_nn(a, b):
    return _bdot(a, b, _NN)


_mm_nn.defvjp(lambda a, b: (_bdot(a, b, _NN), (a, b)), lambda res, g: (_bdot(g, res[1], _NT), _bdot(res[0], g, _TN)))


@jax.custom_vjp
def _mm_nt(a, b):
    return _bdot(a, b, _NT)


_mm_nt.defvjp(lambda a, b: (_bdot(a, b, _NT), (a, b)), lambda res, g: (_bdot(g, res[1], _NN), _bdot(g, res[0], _TN)))


@jax.custom_vjp
def _mm_tn(a, b):
    return _bdot(a, b, _TN)


_mm_tn.defvjp(lambda a, b: (_bdot(a, b, _TN), (a, b)), lambda res, g: (_bdot(res[1], g, _NT), _bdot(res[0], g, _NN)))


def _hdot(a, b):
    return jnp.dot(a, b, precision=HI, preferred_element_type=F32)


def _chunk_math(q, k, v, gcol, bcol, state):
    C, dv = v.shape
    ii = lax.broadcasted_iota(jnp.int32, (C, C), 0)
    jj = lax.broadcasted_iota(jnp.int32, (C, C), 1)
    causal = ii >= jj
    strict = ii > jj
    lower = causal.astype(F32)
    dm = _hdot(lower, gcol * strict.astype(F32))
    decay = jnp.where(causal, jnp.exp(jnp.where(causal, dm, 0.0)), 0.0)
    gcb = _hdot(lower, gcol * jnp.ones((C, dv), F32))
    glast = jnp.sum(gcol * jnp.ones((C, dv), F32), axis=0, keepdims=True)
    eg = jnp.exp(gcb)
    a = jnp.where(strict, bcol * _mm_nt(k, k) * decay, 0.0)
    t = jnp.where(ii == jj, 1.0, 0.0) - a
    m = a
    for _ in range(int(math.log2(C)) - 1):
        m = _hdot(m, m)
        t = t + _hdot(t, m)
    u_c = _hdot(t, v * bcol)
    w_c = _hdot(t, k * bcol * eg)
    qk = _mm_nt(q, k) * decay
    v_new = u_c - _mm_nn(w_c, state)
    o = _mm_nn(q * eg, state) + _mm_nn(qk, v_new)
    new_state = state * jnp.exp(glast) + _mm_tn(k * jnp.exp(glast - gcb), v_new)
    return o, new_state


def _gate_cols(gates, h):
    lane = lax.broadcasted_iota(jnp.int32, gates.shape, 1)
    bcol = jnp.sum(jnp.where(lane == h, gates, 0.0), axis=1, keepdims=True)
    gcol = jnp.sum(jnp.where(lane == h + DN_HEADS, gates, 0.0), axis=1, keepdims=True)
    return gcol, bcol


def _dn_chunk_fwd(qkv, gates):
    L = qkv.shape[0]
    N, H, d = L // CHUNK, DN_HEADS, DN_HEAD_DIM

    def body(q_ref, k_ref, v_ref, g_ref, o_ref, st_ref, s_ref):
        n, h = pl.program_id(0), pl.program_id(1)

        @pl.when(n == 0)
        def _():
            s_ref[h] = jnp.zeros((d, d), F32)

        state = s_ref[h]
        st_ref[0, 0] = state
        gcol, bcol = _gate_cols(g_ref[...], h)
        o, ns = _chunk_math(q_ref[...], k_ref[...], v_ref[...], gcol, bcol, state)
        o_ref[...] = o
        s_ref[h] = ns

    blk = lambda off: pl.BlockSpec((CHUNK, d), lambda n, h: (n, off + h))
    return pl.pallas_call(
        body, name="dn_chunk_fwd", grid=(N, H),
        in_specs=[blk(0), blk(H), blk(2 * H), pl.BlockSpec((CHUNK, 128), lambda n, h: (n, 0))],
        out_specs=[blk(0), pl.BlockSpec((1, 1, d, d), lambda n, h: (n, h, 0, 0))],
        out_shape=[_sds((L, D_DN)), _sds((N, H, d, d))], scratch_shapes=[pltpu.VMEM((H, d, d), F32)],
        compiler_params=_cp(("arbitrary", "arbitrary")),
    )(qkv, qkv, qkv, gates)


def _dn_chunk_bwd(qkv, gates, states, do):
    L = qkv.shape[0]
    N, H, d = L // CHUNK, DN_HEADS, DN_HEAD_DIM

    def body(q_ref, k_ref, v_ref, g_ref, st_ref, do_ref, dq_ref, dk_ref, dv_ref, dg_ref, ds_ref):
        n, h = pl.program_id(0), pl.program_id(1)

        @pl.when(n == 0)
        def _():
            ds_ref[h] = jnp.zeros((d, d), F32)

        gates_ = g_ref[...]
        gcol, bcol = _gate_cols(gates_, h)
        _, f = jax.vjp(_chunk_math, q_ref[...], k_ref[...], v_ref[...], gcol, bcol, st_ref[0, 0])
        dq, dk, dv, dgc, dbc, dst = f((do_ref[...], ds_ref[h]))
        dq_ref[...] = dq
        dk_ref[...] = dk
        dv_ref[...] = dv
        ds_ref[h] = dst
        lane = lax.broadcasted_iota(jnp.int32, gates_.shape, 1)
        part = jnp.where(lane == h, dbc, 0.0) + jnp.where(lane == h + DN_HEADS, dgc, 0.0)

        @pl.when(h == 0)
        def _():
            dg_ref[...] = part

        @pl.when(h > 0)
        def _():
            dg_ref[...] += part

    blk = lambda off: pl.BlockSpec((CHUNK, d), lambda n, h: (N - 1 - n, off + h))
    gsp = pl.BlockSpec((CHUNK, 128), lambda n, h: (N - 1 - n, 0))
    dqkv_shape = _sds((L, D_DN))
    return pl.pallas_call(
        body, name="dn_chunk_bwd", grid=(N, H),
        in_specs=[blk(0), blk(H), blk(2 * H), gsp, pl.BlockSpec((1, 1, d, d), lambda n, h: (N - 1 - n, h, 0, 0)), blk(0)],
        out_specs=[blk(0), blk(0), blk(0), gsp],
        out_shape=[dqkv_shape, dqkv_shape, dqkv_shape, _sds((L, 128))], scratch_shapes=[pltpu.VMEM((H, d, d), F32)],
        compiler_params=_cp(("arbitrary", "arbitrary")),
    )(qkv, qkv, qkv, gates, states, do)


def _dn_post_fwd(o, proj, nw):
    L = o.shape[0]
    d = DN_HEAD_DIM
    tm = min(512, L)

    def body(o_ref, z_ref, w_ref, y_ref):
        ov = o_ref[...]
        r = lax.rsqrt(jnp.mean(ov * ov, axis=-1, keepdims=True) + EPS)
        y_ref[...] = (ov * r * w_ref[...] * _silu(z_ref[...])).astype(BF16)

    blk = pl.BlockSpec((tm, d), lambda i, h: (i, h))
    return pl.pallas_call(
        body, name="dn_post_fwd", grid=(L // tm, DN_HEADS),
        in_specs=[blk, pl.BlockSpec((tm, d), lambda i, h: (i, OFF_ZD // d + h)), pl.BlockSpec((1, d), lambda i, h: (0, 0))],
        out_specs=blk, out_shape=_sds((L, D_DN), BF16), compiler_params=_cp(("parallel", "parallel")),
    )(o, proj, nw)


def _dn_post_bwd(o, proj, nw, dy):
    L = o.shape[0]
    d = DN_HEAD_DIM
    tm = min(512, L)

    def body(o_ref, z_ref, w_ref, dy_ref, do_ref, dz_ref, dw_ref):
        first = jnp.logical_and(pl.program_id(0) == 0, pl.program_id(1) == 0)
        ov, z, w, dyv = o_ref[...], z_ref[...], w_ref[...], dy_ref[...]
        r = lax.rsqrt(jnp.mean(ov * ov, axis=-1, keepdims=True) + EPS)
        xn = ov * r
        dz_ref[...] = (dyv * xn * w * _dsilu(z)).astype(BF16)
        dn = dyv * _silu(z)
        t = dn * w
        do_ref[...] = r * t - ov * (r * r * r) * jnp.mean(t * ov, axis=-1, keepdims=True)
        part = jnp.sum(dn * xn, axis=0, keepdims=True)

        @pl.when(first)
        def _():
            dw_ref[...] = part

        @pl.when(jnp.logical_not(first))
        def _():
            dw_ref[...] += part

    blk = pl.BlockSpec((tm, d), lambda i, h: (i, h))
    one = pl.BlockSpec((1, d), lambda i, h: (0, 0))
    return pl.pallas_call(
        body, name="dn_post_bwd", grid=(L // tm, DN_HEADS),
        in_specs=[blk, pl.BlockSpec((tm, d), lambda i, h: (i, OFF_ZD // d + h)), one, blk], out_specs=[blk, blk, one],
        out_shape=[_sds((L, D_DN)), _sds((L, D_DN), BF16), _sds((1, d))], compiler_params=_cp(("arbitrary", "arbitrary")),
    )(o, proj, nw, dy)


def _mix_fwd(s5o, dno, w_su, w_du, proj):
    L, K = s5o.shape
    N = w_su.shape[1]
    tm, tn = min(512, L), 512

    def body(a1, a2, b1, b2, gs, gd, ys_ref, yd_ref, mx_ref):
        ys = jnp.dot(a1[...], b1[...], preferred_element_type=F32)
        yd = jnp.dot(a2[...], b2[...], preferred_element_type=F32)
        ys_ref[...] = ys
        yd_ref[...] = yd
        mx_ref[...] = (_sigmoid(gs[...]) * ys + _sigmoid(gd[...]) * yd).astype(BF16)

    a = pl.BlockSpec((tm, K), lambda i, j: (i, 0))
    b = pl.BlockSpec((K, tn), lambda i, j: (0, j))
    o = pl.BlockSpec((tm, tn), lambda i, j: (i, j))
    return pl.pallas_call(
        body, name="mix_fwd", grid=(L // tm, N // tn),
        in_specs=[a, a, b, b, pl.BlockSpec((tm, tn), lambda i, j: (i, OFF_GS // tn + j)),
                  pl.BlockSpec((tm, tn), lambda i, j: (i, OFF_GD // tn + j))],
        out_specs=[o, o, o], out_shape=[_sds((L, N)), _sds((L, N)), _sds((L, N), BF16)],
        compiler_params=_cp(("parallel", "parallel")),
    )(s5o, dno, w_su, w_du, proj, proj)


def _mix_bwd(dx2b, w_out, proj, ys, yd):
    L, K = dx2b.shape
    N = w_out.shape[0]
    tm, tn = min(512, L), 512

    def body(a, b, gs, gd, ys_ref, yd_ref, dgs_ref, dgd_ref, dys_ref, dyd_ref):
        dm = lax.dot_general(a[...], b[...], (((1,), (1,)), ((), ())), preferred_element_type=F32)
        ss, sd = _sigmoid(gs[...]), _sigmoid(gd[...])
        dys_ref[...] = (dm * ss).astype(BF16)
        dyd_ref[...] = (dm * sd).astype(BF16)
        dgs_ref[...] = (dm * ys_ref[...] * ss * (1.0 - ss)).astype(BF16)
        dgd_ref[...] = (dm * yd_ref[...] * sd * (1.0 - sd)).astype(BF16)

    o = pl.BlockSpec((tm, tn), lambda i, j: (i, j))
    return pl.pallas_call(
        body, name="mix_bwd", grid=(L // tm, N // tn),
        in_specs=[pl.BlockSpec((tm, K), lambda i, j: (i, 0)), pl.BlockSpec((tn, K), lambda i, j: (j, 0)),
                  pl.BlockSpec((tm, tn), lambda i, j: (i, OFF_GS // tn + j)),
                  pl.BlockSpec((tm, tn), lambda i, j: (i, OFF_GD // tn + j)), o, o],
        out_specs=[o, o, o, o], out_shape=[_sds((L, N), BF16)] * 4, compiler_params=_cp(("parallel", "parallel")),
    )(dx2b, w_out, proj, proj, ys, yd)


def _final(mixed, w_out, x, tgt, fw):
    L, D = x.shape
    tm = min(256, L)

    def body(a_ref, b_ref, x_ref, t_ref, w_ref, dx_ref, dxb_ref, loss_ref, dw_ref):
        i = pl.program_id(0)
        x2 = x_ref[...] + jnp.dot(a_ref[...], b_ref[...], preferred_element_type=F32)
        w = w_ref[...]
        r = lax.rsqrt(jnp.mean(x2 * x2, axis=-1, keepdims=True) + EPS)
        xn = x2 * r
        e = xn * w - t_ref[...]
        lpart = 0.5 * jnp.sum(jnp.mean(e * e, axis=-1, keepdims=True), axis=0, keepdims=True)
        dy = e * (1.0 / D)
        t = dy * w
        dx2 = r * t - x2 * (r * r * r) * jnp.mean(t * x2, axis=-1, keepdims=True)
        dx_ref[...] = dx2
        dxb_ref[...] = dx2.astype(BF16)
        dwp = jnp.sum(dy * xn, axis=0, keepdims=True)
        lrow = jnp.broadcast_to(lpart, loss_ref.shape)

        @pl.when(i == 0)
        def _():
            loss_ref[...] = lrow
            dw_ref[...] = dwp

        @pl.when(i > 0)
        def _():
            loss_ref[...] += lrow
            dw_ref[...] += dwp

    row = pl.BlockSpec((tm, D), lambda i: (i, 0))
    one = pl.BlockSpec((1, D), lambda i: (0, 0))
    return pl.pallas_call(
        body, name="final", grid=(L // tm,),
        in_specs=[row, pl.BlockSpec((D, D), lambda i: (0, 0)), row, row, one],
        out_specs=[row, row, pl.BlockSpec((1, 128), lambda i: (0, 0)), one],
        out_shape=[_sds((L, D)), _sds((L, D), BF16), _sds((1, 128)), _sds((1, D))], compiler_params=_cp(("arbitrary",)),
    )(mixed, w_out, x, tgt, fw)


def _block_diag(t):
    J, g, a, b = t.shape
    eye = jnp.eye(g, dtype=t.dtype)
    return (t[:, :, :, None, :] * eye[None, :, None, :, None]).reshape(J, g * a, g * b)


def _block_diag_take(m, g):
    J, ga, gb = m.shape
    a, b = ga // g, gb // g
    m5 = m.reshape(J, g, a, g, b)
    idx = jnp.arange(g)
    return m5[:, idx, :, idx, :].transpose(1, 0, 2, 3)


def _local_step(x, tgt, ln_w, w_perm, lam_re, lam_im, log_step, b_re, b_im, c_re, c_im, s5_d, w_glu, w_su,
                conv_w, a_log, dt_bias, norm_w, w_du, w_out, fw):
    G, P, gb = S5_GROUPS, S5_STATE, S5_GROUPS // S5_BLOCKS
    h, rstd = _ln_fwd(x, ln_w)
    proj = _mm(h, w_perm, name="in_proj", tn=1152)

    b_re2, b_im2 = b_re.reshape(G, P * S5_GROUP), b_im.reshape(G, P * S5_GROUP)
    ls2 = log_step.reshape(G, 1)
    abar_re, abar_im, bb_re, bb_im = _s5_param_fwd(lam_re, lam_im, ls2, b_re2, b_im2)

    def to_wb(bb):
        return _block_diag(bb.reshape(S5_BLOCKS, gb, P, S5_GROUP).transpose(0, 1, 3, 2)).astype(BF16)

    def to_cb(cc):
        return _block_diag(cc.reshape(S5_BLOCKS, gb, S5_GROUP, P).transpose(0, 1, 3, 2)).astype(BF16)

    wbr, wbi, cbr, cbi = to_wb(bb_re), to_wb(bb_im), to_cb(c_re), to_cb(c_im)
    a_re_row, a_im_row = abar_re.reshape(1, G * P), abar_im.reshape(1, G * P)
    yc = _s5_core_fwd(proj, wbr, wbi, a_re_row, a_im_row, cbr, cbi)
    s5o = _s5_post_fwd(yc, proj, s5_d, w_glu)

    pad = lambda v: jnp.pad(v, ((0, 0), (DN_HEADS, 128 - 2 * DN_HEADS)))
    alog_row, dtb_row = pad(a_log), pad(dt_bias)
    qkv = _dn_conv_fwd(proj, conv_w)
    gates = _dn_gates_fwd(proj, alog_row, dtb_row)
    o_dn, states = _dn_chunk_fwd(qkv, gates)
    dno = _dn_post_fwd(o_dn, proj, norm_w)

    ys, yd, mixed = _mix_fwd(s5o, dno, w_su, w_du, proj)
    dx2, dx2b, loss_row, d_fw = _final(mixed, w_out, x, tgt, fw)
    d_w_out = _mm(mixed, dx2b, ta=True, name="d_w_out")
    dgs, dgd, dys, dyd = _mix_bwd(dx2b, w_out, proj, ys, yd)
    d_w_su = _mm(s5o, dys, ta=True, name="d_w_su", shard_out=True)
    d_w_du = _mm(dno, dyd, ta=True, name="d_w_du", shard_out=True)
    ds5o = _mm(dys, w_su, tb=True, name="d_s5o")
    ddno = _mm(dyd, w_du, tb=True, name="d_dno")

    dyc, du1, dz_s, d_s5d, d_w_glu = _s5_post_bwd(yc, proj, s5_d, w_glu, ds5o)
    du, dwbr, dwbi, dcbr, dcbi, dar, dai = _s5_core_bwd(proj, wbr, wbi, a_re_row, a_im_row, cbr, cbi, dyc, du1)

    def from_wb(dwb):
        return _block_diag_take(dwb, gb).transpose(0, 1, 3, 2).reshape(G, P * S5_GROUP)

    def from_cb(dcb):
        return _block_diag_take(dcb, gb).transpose(0, 1, 3, 2).reshape(G, S5_GROUP, P)

    d_lam_re, d_lam_im, d_ls, d_b_re, d_b_im = _s5_param_bwd(
        lam_re, lam_im, ls2, b_re2, b_im2, dar.reshape(G, P), dai.reshape(G, P), from_wb(dwbr), from_wb(dwbi))

    do_dn, dz_d, d_norm_w = _dn_post_bwd(o_dn, proj, norm_w, ddno)
    dq, dk, dv, dgates = _dn_chunk_bwd(qkv, gates, states, do_dn)
    dqkv, d_conv = _dn_conv_bwd(proj, conv_w, jnp.concatenate([dq, dk, dv], axis=1))
    dpb, d_alog_row, d_dtb_row = _dn_gates_bwd(proj, alog_row, dtb_row, dgates)

    dproj = jnp.concatenate([du, dz_s, dqkv, dz_d, dgs, dgd, dpb], axis=1)
    d_w_perm = _mm(h, dproj, ta=True, name="d_w_in", tn=1152)
    dh = _mm(dproj, w_perm, tb=True, name="d_h", tk=1152)
    grad_x, d_ln_w = _ln_bwd(x, rstd, ln_w, dh, dx2)

    grads = dict(
        ln_w=d_ln_w, w_perm=d_w_perm, s5_lam_re=d_lam_re, s5_lam_im=d_lam_im, s5_log_step=d_ls.reshape(1, G),
        s5_b_re=d_b_re.reshape(G, P, S5_GROUP), s5_b_im=d_b_im.reshape(G, P, S5_GROUP),
        s5_c_re=from_cb(dcbr), s5_c_im=from_cb(dcbi), s5_d=d_s5d, s5_w_glu=d_w_glu, s5_w_up=d_w_su,
        dn_conv_w=d_conv, dn_a_log=d_alog_row[:, DN_HEADS:2 * DN_HEADS], dn_dt_bias=d_dtb_row[:, DN_HEADS:2 * DN_HEADS],
        dn_norm_w=d_norm_w, dn_w_up=d_w_du, w_out=d_w_out, final_norm_w=d_fw)
    return loss_row, grad_x, grads


def _place():
    x, y, c = lax.axis_index("x"), lax.axis_index("y"), lax.axis_index("c")
    return x, y, c


def _remote(src, dst, send_sem, recv_sem, to):
    return pltpu.make_async_remote_copy(src_ref=src, dst_ref=dst, send_sem=send_sem, recv_sem=recv_sem,
                                        device_id=to, device_id_type=MESH)


def _gather_weights(shards, conv_w):
    na = len(shards)

    def body(*refs):
        ins, conv_in = refs[:na], refs[na]
        outs, conv_out = refs[na + 1:2 * na + 1], refs[2 * na + 1]
        send_sems, recv_sems, local_sems = refs[2 * na + 2:]
        x, y, c = _place()
        me = 2 * x + y
        sibling = (x, y, 1 - c)
        chips = [(1 - x, y), (x, 1 - y), (1 - x, 1 - y)]

        def part(ref, chip, half, a):
            r2 = shards[a].shape[0] // 2
            return ref.at[chip, pl.ds(half * r2, r2)]

        own = [pltpu.make_async_copy(ins[a], outs[a].at[me], local_sems.at[a]) for a in range(na)]
        own.append(pltpu.make_async_copy(conv_in, conv_out.at[me], local_sems.at[na]))
        for cp in own:
            cp.start()
        sends = []
        for a in range(na):
            r2 = shards[a].shape[0] // 2
            for j, (px, py) in enumerate(chips):
                k = 6 * a + j
                sends.append(_remote(ins[a].at[pl.ds(c * r2, r2)], part(outs[a], me, c, a), send_sems.at[k], recv_sems.at[k],
                                     (px, py, c)))
        for j, (px, py) in enumerate(chips):
            k = 6 * na + j
            sends.append(_remote(conv_in, conv_out.at[me], send_sems.at[k], recv_sems.at[k], (px, py, c)))
        for cp in sends:
            cp.start()
        for a in range(na):
            for j, (px, py) in enumerate(chips):
                src = 2 * px + py
                k = 6 * a + j
                landed = part(outs[a], src, c, a)
                _remote(landed, landed, send_sems.at[k], recv_sems.at[k], (px, py, c)).wait_recv()
                fwd = _remote(landed, landed, send_sems.at[k + 3], recv_sems.at[k + 3], sibling)
                fwd.start()
                sends.append(fwd)
        for a in range(na):
            for j, (px, py) in enumerate(chips):
                other = part(outs[a], 2 * px + py, 1 - c, a)
                _remote(other, other, send_sems.at[6 * a + 3 + j], recv_sems.at[6 * a + 3 + j], sibling).wait_recv()
        for j, (px, py) in enumerate(chips):
            k = 6 * na + j
            _remote(conv_in, conv_out.at[2 * px + py], send_sems.at[k], recv_sems.at[k], (px, py, c)).wait_recv()
        for cp in sends:
            cp.wait_send()
        for cp in own:
            cp.wait()

    nsem = 6 * na + 3
    return pl.pallas_call(
        body, name="gather_weights", in_specs=[ANY] * (na + 1), out_specs=[ANY] * (na + 1),
        out_shape=[_sds((N_CHIPS,) + s.shape, s.dtype) for s in shards] + [_sds((N_CHIPS,) + conv_w.shape, conv_w.dtype)],
        scratch_shapes=[pltpu.SemaphoreType.DMA((nsem,)), pltpu.SemaphoreType.DMA((nsem,)), pltpu.SemaphoreType.DMA((na + 1,))],
    )(*shards, conv_w)


def _swap_halves(gxs):
    na = len(gxs)

    def body(*refs):
        ins, outs = refs[:na], refs[na:2 * na]
        send_sems, recv_sems = refs[2 * na:]
        x, y, c = _place()
        cps = [_remote(ins[a].at[pl.ds(0, N_CHIPS), pl.ds(1 - c, 1)], outs[a], send_sems.at[a], recv_sems.at[a], (x, y, 1 - c))
               for a in range(na)]
        for cp in cps:
            cp.start()
        for cp in cps:
            cp.wait()

    return pl.pallas_call(
        body, name="rs_swap_halves", in_specs=[ANY] * na, out_specs=[ANY] * na,
        out_shape=[_sds((N_CHIPS, 1) + g.shape[2:], g.dtype) for g in gxs],
        scratch_shapes=[pltpu.SemaphoreType.DMA((na,)), pltpu.SemaphoreType.DMA((na,))],
    )(*gxs)


def _to_owners(csbs):
    na = len(csbs)

    def body(*refs):
        ins, outs = refs[:na], refs[na:2 * na]
        send_sems, recv_sems = refs[2 * na:]
        x, y, c = _place()
        me = 2 * x + y
        cps = []
        for a in range(na):
            for k in range(N_CHIPS - 1):
                j = (me + 1 + k) % N_CHIPS
                cps.append(_remote(ins[a].at[k], outs[a].at[2 - k], send_sems.at[3 * a + k], recv_sems.at[3 * a + 2 - k],
                                   (j // 2, j % 2, c)))
        for cp in cps:
            cp.start()
        for a in range(na):
            for k in range(N_CHIPS - 1):
                _remote(ins[a].at[k], outs[a].at[k], send_sems.at[3 * a + k], recv_sems.at[3 * a + k], (x, y, c)).wait_recv()
        for cp in cps:
            cp.wait_send()

    return pl.pallas_call(
        body, name="rs_to_owners", in_specs=[ANY] * na, out_specs=[ANY] * na,
        out_shape=[_sds(g.shape, g.dtype) for g in csbs],
        scratch_shapes=[pltpu.SemaphoreType.DMA((3 * na,)), pltpu.SemaphoreType.DMA((3 * na,))],
    )(*csbs)


def _share_halves(gfs):
    na = len(gfs)

    def body(*refs):
        ins, outs = refs[:na], refs[na:2 * na]
        send_sems, recv_sems = refs[2 * na:]
        x, y, c = _place()
        cps = [_remote(ins[a].at[pl.ds(c, 1)], outs[a].at[pl.ds(c, 1)], send_sems.at[a], recv_sems.at[a], (x, y, 1 - c))
               for a in range(na)]
        for cp in cps:
            cp.start()
        for a in range(na):
            cps[a].wait_send()
            _remote(ins[a].at[pl.ds(1 - c, 1)], outs[a].at[pl.ds(1 - c, 1)], send_sems.at[a], recv_sems.at[a], (x, y, 1 - c)).wait_recv()

    return pl.pallas_call(
        body, name="rs_share_halves", in_specs=[ANY] * na, out_specs=[ANY] * na,
        out_shape=[_sds(g.shape, g.dtype) for g in gfs], input_output_aliases={a: a for a in range(na)},
        scratch_shapes=[pltpu.SemaphoreType.DMA((na,)), pltpu.SemaphoreType.DMA((na,))],
    )(*gfs)


def _row_tile(rows, cols, budget=1 << 20):
    t = rows
    while t % 2 == 0 and t > 16 and t * cols * 4 > budget:
        t //= 2
    return t


def _chip_sums(gx, r1, where):
    _, _, r2, cd = gx.shape
    tr = _row_tile(r2, cd)

    def body(w_ref, a_ref, b_ref, o_ref):
        o_ref[...] = (a_ref[0] + b_ref[0]).astype(BF16)

    other = lambda k, i, w: ((w[1] + 1 + k) % N_CHIPS, w[0], i, 0)
    other0 = lambda k, i, w: ((w[1] + 1 + k) % N_CHIPS, 0, i, 0)
    return pl.pallas_call(
        body, name="rs_chip_sums",
        grid_spec=pltpu.PrefetchScalarGridSpec(
            num_scalar_prefetch=1, grid=(N_CHIPS - 1, r2 // tr),
            in_specs=[pl.BlockSpec((1, 1, tr, cd), other), pl.BlockSpec((1, 1, tr, cd), other0)],
            out_specs=pl.BlockSpec((1, tr, cd), lambda k, i, w: (k, i, 0))),
        out_shape=_sds((N_CHIPS - 1, r2, cd), BF16), compiler_params=_cp(("parallel", "parallel")),
    )(where, gx, r1)


def _owner_sum(gx, r1, r2x, where):
    _, _, r2, cd = gx.shape
    tr = _row_tile(r2, cd)

    def body(w_ref, a_ref, b_ref, r_ref, o_ref):
        acc = a_ref[0, 0] + b_ref[0, 0]
        for k in range(N_CHIPS - 1):
            acc = acc + r_ref[k].astype(F32)
        o_ref[0] = acc

    return pl.pallas_call(
        body, name="rs_owner_sum",
        grid_spec=pltpu.PrefetchScalarGridSpec(
            num_scalar_prefetch=1, grid=(r2 // tr,),
            in_specs=[pl.BlockSpec((1, 1, tr, cd), lambda i, w: (w[1], w[0], i, 0)),
                      pl.BlockSpec((1, 1, tr, cd), lambda i, w: (w[1], 0, i, 0)),
                      pl.BlockSpec((N_CHIPS - 1, tr, cd), lambda i, w: (0, i, 0))],
            out_specs=pl.BlockSpec((1, tr, cd), lambda i, w: (w[0], i, 0))),
        out_shape=_sds((2, r2, cd)), compiler_params=_cp(("parallel",)),
    )(where, gx, r1, r2x)


def _adamw_math(w, g, m, v):
    m = ADAM_B1 * m + (1.0 - ADAM_B1) * g
    v = ADAM_B2 * v + (1.0 - ADAM_B2) * (g * g)
    m_hat = m / (1.0 - ADAM_B1 ** ADAM_STEP)
    v_hat = v / (1.0 - ADAM_B2 ** ADAM_STEP)
    delta = -ADAM_LR * (m_hat / (jnp.sqrt(v_hat) + ADAM_EPS) + ADAM_WD * w)
    return delta, m, v


def _adamw(w, g, m, v, name):
    rows, cd = w.shape
    tr = _row_tile(rows, cd, budget=3 << 19) if rows % 16 == 0 else rows

    def body(w_ref, g_ref, m_ref, v_ref, d_ref, mo_ref, vo_ref):
        d, mm, vv = _adamw_math(w_ref[...], g_ref[...], m_ref[...], v_ref[...])
        d_ref[...] = d
        mo_ref[...] = mm
        vo_ref[...] = vv

    blk = pl.BlockSpec((tr, cd), lambda i: (i, 0))
    return pl.pallas_call(
        body, name=name, grid=(rows // tr,), in_specs=[blk] * 4, out_specs=[blk] * 3, out_shape=[_sds(w.shape)] * 3,
        compiler_params=_cp(("parallel",)),
    )(w, g, m, v)


def _small_allreduce_adamw(gp, wp, mp, vp):
    R = gp.shape[0]

    def body(g_ref, w_ref, m_ref, v_ref, go_ref, d_ref, mo_ref, vo_ref, land, send_sems, recv_sems):
        x, y, c = _place()
        me = 4 * x + 2 * y + c
        land[me] = g_ref[...]
        peers = []
        for k in range(1, N_DEV):
            px = 1 - x if k & 4 else x
            py = 1 - y if k & 2 else y
            pc = 1 - c if k & 1 else c
            peers.append((px, py, pc))
        cps = [_remote(g_ref, land.at[me], send_sems.at[k], recv_sems.at[k], peers[k]) for k in range(N_DEV - 1)]
        for cp in cps:
            cp.start()
        for k, (px, py, pc) in enumerate(peers):
            _remote(g_ref, land.at[4 * px + 2 * py + pc], send_sems.at[k], recv_sems.at[k], (px, py, pc)).wait_recv()
        for cp in cps:
            cp.wait_send()
        acc = land[0]
        for i in range(1, N_DEV):
            acc = acc + land[i]
        go_ref[...] = acc
        d, mm, vv = _adamw_math(w_ref[...], acc, m_ref[...], v_ref[...])
        d_ref[...] = d
        mo_ref[...] = mm
        vo_ref[...] = vv

    vm = pl.BlockSpec(memory_space=pltpu.VMEM)
    return pl.pallas_call(
        body, name="small_allreduce_adamw", in_specs=[vm] * 4, out_specs=[vm] * 4, out_shape=[_sds((R, 128))] * 4,
        scratch_shapes=[pltpu.VMEM((N_DEV, R, 128), F32), pltpu.SemaphoreType.DMA((N_DEV - 1,)),
                        pltpu.SemaphoreType.DMA((N_DEV - 1,))],
        compiler_params=_cp(),
    )(gp, wp, mp, vp)


def _pack(arrs):
    rows = []
    for a in arrs:
        f = a.reshape(-1)
        f = jnp.pad(f, (0, (-f.shape[0]) % 128))
        rows.append(f.reshape(-1, 128))
    p = jnp.concatenate(rows, axis=0)
    return jnp.pad(p, ((0, (-p.shape[0]) % 8), (0, 0)))


def _unpack(p, shapes):
    out, r = [], 0
    for s in shapes:
        n = math.prod(s)
        nr = -(-n // 128)
        out.append(p[r:r + nr].reshape(-1)[:n].reshape(s))
        r += nr
    return out


_SMALL = ("ln_w", "s5_lam_re", "s5_lam_im", "s5_log_step", "s5_b_re", "s5_b_im", "s5_c_re", "s5_c_im", "s5_d",
          "dn_a_log", "dn_dt_bias", "dn_norm_w", "final_norm_w")
_BIG = ("w_in", "s5_w_glu", "s5_w_up", "dn_w_up", "w_out")
_ORDER = ("ln_w", "w_in", "s5_lam_re", "s5_lam_im", "s5_log_step", "s5_b_re", "s5_b_im", "s5_c_re", "s5_c_im", "s5_d",
          "s5_w_glu", "s5_w_up", "dn_conv_w", "dn_a_log", "dn_dt_bias", "dn_norm_w", "dn_w_up", "w_out", "final_norm_w")


def kernel(x, ln_w, w_in, s5_lam_re, s5_lam_im, s5_log_step, s5_b_re, s5_b_im, s5_c_re, s5_c_im, s5_d, s5_w_glu, s5_w_up, dn_conv_w, dn_a_log, dn_dt_bias, dn_norm_w, dn_w_up, w_out, final_norm_w, loss_target, m_ln_w, m_w_in, m_s5_lam_re, m_s5_lam_im, m_s5_log_step, m_s5_b_re, m_s5_b_im, m_s5_c_re, m_s5_c_im, m_s5_d, m_s5_w_glu, m_s5_w_up, m_dn_conv_w, m_dn_a_log, m_dn_dt_bias, m_dn_norm_w, m_dn_w_up, m_w_out, m_final_norm_w, v_ln_w, v_w_in, v_s5_lam_re, v_s5_lam_im, v_s5_log_step, v_s5_b_re, v_s5_b_im, v_s5_c_re, v_s5_c_im, v_s5_d, v_s5_w_glu, v_s5_w_up, v_dn_conv_w, v_dn_a_log, v_dn_dt_bias, v_dn_norm_w, v_dn_w_up, v_w_out, v_final_norm_w):
    w = dict(ln_w=ln_w, w_in=w_in, s5_lam_re=s5_lam_re, s5_lam_im=s5_lam_im, s5_log_step=s5_log_step, s5_b_re=s5_b_re,
             s5_b_im=s5_b_im, s5_c_re=s5_c_re, s5_c_im=s5_c_im, s5_d=s5_d, s5_w_glu=s5_w_glu, s5_w_up=s5_w_up,
             dn_conv_w=dn_conv_w, dn_a_log=dn_a_log, dn_dt_bias=dn_dt_bias, dn_norm_w=dn_norm_w, dn_w_up=dn_w_up, w_out=w_out,
             final_norm_w=final_norm_w)
    m = dict(ln_w=m_ln_w, w_in=m_w_in, s5_lam_re=m_s5_lam_re, s5_lam_im=m_s5_lam_im, s5_log_step=m_s5_log_step,
             s5_b_re=m_s5_b_re, s5_b_im=m_s5_b_im, s5_c_re=m_s5_c_re, s5_c_im=m_s5_c_im, s5_d=m_s5_d, s5_w_glu=m_s5_w_glu,
             s5_w_up=m_s5_w_up, dn_conv_w=m_dn_conv_w, dn_a_log=m_dn_a_log, dn_dt_bias=m_dn_dt_bias, dn_norm_w=m_dn_norm_w,
             dn_w_up=m_dn_w_up, w_out=m_w_out, final_norm_w=m_final_norm_w)
    v = dict(ln_w=v_ln_w, w_in=v_w_in, s5_lam_re=v_s5_lam_re, s5_lam_im=v_s5_lam_im, s5_log_step=v_s5_log_step,
             s5_b_re=v_s5_b_re, s5_b_im=v_s5_b_im, s5_c_re=v_s5_c_re, s5_c_im=v_s5_c_im, s5_d=v_s5_d, s5_w_glu=v_s5_w_glu,
             s5_w_up=v_s5_w_up, dn_conv_w=v_dn_conv_w, dn_a_log=v_dn_a_log, dn_dt_bias=v_dn_dt_bias, dn_norm_w=v_dn_norm_w,
             dn_w_up=v_dn_w_up, w_out=v_w_out, final_norm_w=v_final_norm_w)
    xi, yi, ci = _place()
    chip = 2 * xi + yi
    where = jnp.stack([ci, chip]).astype(jnp.int32)

    g_in, g_glu, g_su, g_du, g_out, g_conv = _gather_weights(
        [w[n][0].astype(BF16) for n in _BIG], dn_conv_w[0])
    cat = lambda g: jnp.concatenate([g[j] for j in range(N_CHIPS)], axis=1)
    w_full = cat(g_in)
    w_perm = jnp.concatenate([w_full[:, :OFF_GS], w_full[:, OFF_GS + 2 * DN_HEADS:], w_full[:, OFF_GS:OFF_GS + 2 * DN_HEADS],
                              jnp.zeros((D_MODEL, D_IN_PAD - D_IN), BF16)], axis=1)

    loss_row, grad_x, g = _local_step(
        x[0], loss_target[0], ln_w, w_perm, s5_lam_re[0], s5_lam_im[0], s5_log_step, s5_b_re[0], s5_b_im[0], s5_c_re[0],
        s5_c_im[0], s5_d, g_glu.reshape(D_S5, D_S5), cat(g_su), cat(g_conv), dn_a_log, dn_dt_bias, dn_norm_w, cat(g_du),
        g_out.reshape(D_MODEL, D_MODEL), final_norm_w[None])
    loss = lax.psum(loss_row[0, 0], ("x", "y", "c"))

    dwp = g["w_perm"]
    d_w_in = jnp.concatenate([dwp[:, :OFF_GS], dwp[:, OFF_B:OFF_B + 2 * DN_HEADS], dwp[:, OFF_GS:OFF_B]], axis=1)
    cw = D_IN // N_CHIPS
    gxs = [d_w_in.reshape(D_MODEL, N_CHIPS, cw).transpose(1, 0, 2).reshape(N_CHIPS, 2, D_MODEL // 2, cw),
           g["s5_w_glu"].reshape(N_CHIPS, 2, D_S5 // 8, D_S5),
           g["s5_w_up"].reshape(N_CHIPS, 2, D_S5 // 2, D_MODEL // N_CHIPS),
           g["dn_w_up"].reshape(N_CHIPS, 2, D_DN // 2, D_MODEL // N_CHIPS),
           g["w_out"].reshape(N_CHIPS, 2, D_MODEL // 8, D_MODEL)]
    r1s = _swap_halves(gxs)
    csbs = [_chip_sums(gx, r1, where) for gx, r1 in zip(gxs, r1s)]
    r2s = _to_owners(csbs)
    gfs = [_owner_sum(gx, r1, r2x, where) for gx, r1, r2x in zip(gxs, r1s, r2s)]
    gfs = _share_halves(gfs)
    grads, deltas, new_m, new_v = {}, {}, {}, {}
    for n, gf in zip(_BIG, gfs):
        shp = w[n].shape
        g2 = gf.reshape(shp[1:])
        d_, m_, v_ = _adamw(w[n][0], g2, m[n][0], v[n][0], "adamw_" + n)
        grads[n], deltas[n], new_m[n], new_v[n] = g2.reshape(shp), d_.reshape(shp), m_.reshape(shp), v_.reshape(shp)

    gp = _pack([g[n] for n in _SMALL] + [g["dn_conv_w"]])
    zc = jnp.zeros((CONV_K, 3 * D_DN), F32)
    go, dl, mo, vo = _small_allreduce_adamw(gp, _pack([w[n] for n in _SMALL] + [zc]), _pack([m[n] for n in _SMALL] + [zc]),
                                            _pack([v[n] for n in _SMALL] + [zc]))
    shapes = [w[n].shape for n in _SMALL] + [(CONV_K, 3 * D_DN)]
    for dst, src in ((grads, go), (deltas, dl), (new_m, mo), (new_v, vo)):
        for n, a in zip(_SMALL, _unpack(src, shapes)):
            dst[n] = a
    cc = 3 * D_DN // N_CHIPS
    g_conv_mine = lax.dynamic_slice(_unpack(go, shapes)[-1], (0, chip * cc), (CONV_K, cc))
    d_, m_, v_ = _adamw(dn_conv_w[0], g_conv_mine, m_dn_conv_w[0], v_dn_conv_w[0], "adamw_dn_conv_w")
    grads["dn_conv_w"], deltas["dn_conv_w"], new_m["dn_conv_w"], new_v["dn_conv_w"] = (
        g_conv_mine[None], d_[None], m_[None], v_[None])

    return (loss, grad_x[None], *[grads[n] for n in _ORDER], *[deltas[n] for n in _ORDER], *[new_m[n] for n in _ORDER],
            *[new_v[n] for n in _ORDER])
```

```python
import functools
import math

import jax
import jax.numpy as jnp
from jax import lax
from jax.experimental import pallas as pl
from jax.experimental.pallas import tpu as pltpu

F32 = jnp.float32
BF16 = jnp.bfloat16
HI = lax.Precision.HIGHEST
MESH = pl.DeviceIdType.MESH
ANY = pl.BlockSpec(memory_space=pl.ANY)

EPS = 1e-6
D_MODEL = 2048
D_S5 = 1024
S5_GROUP = 16
S5_GROUPS = 64
S5_STATE = 64
S5_BLOCKS = 8
S5_SEG = 8
DN_HEADS = 8
DN_HEAD_DIM = 128
D_DN = 1024
CONV_K = 4
CHUNK = 64
D_IN = 10256
D_IN_PAD = 10368
OFF_US, OFF_ZS, OFF_Q, OFF_K, OFF_V, OFF_ZD, OFF_GS, OFF_GD, OFF_B = 0, 1024, 2048, 3072, 4096, 5120, 6144, 8192, 10240
N_CHIPS = 4
N_DEV = 8
VMEM_LIMIT = 56 * 1024 * 1024

ADAM_LR = 0.001
ADAM_B1 = 0.9
ADAM_B2 = 0.999
ADAM_EPS = 1e-08
ADAM_WD = 0.01
ADAM_STEP = 10


def _cp(sem=None):
    return pltpu.CompilerParams(dimension_semantics=sem, vmem_limit_bytes=VMEM_LIMIT)


def _sds(shape, dtype=F32):
    return jax.ShapeDtypeStruct(tuple(shape), dtype)


def _sigmoid(x):
    return 1.0 / (1.0 + jnp.exp(-x))


def _silu(x):
    return x * _sigmoid(x)


def _dsilu(x):
    s = _sigmoid(x)
    return s * (1.0 + x * (1.0 - s))


def _mm(a, b, *, name, ta=False, tb=False, out_dtype=F32, tm=512, tn=512, tk=2048, shard_out=False):
    if ta:
        K, M = a.shape
    else:
        M, K = a.shape
    if tb:
        N, K2 = b.shape
    else:
        K2, N = b.shape
    assert K == K2, (a.shape, b.shape)
    tm, tn, tk = min(tm, M), min(tn, N), min(tk, K)
    assert M % tm == 0 and N % tn == 0 and K % tk == 0, (M, N, K, tm, tn, tk)
    nk = K // tk
    dims = (((0 if ta else 1,), (1 if tb else 0,)), ((), ()))

    def body(a_ref, b_ref, o_ref, acc_ref):
        k = pl.program_id(2)
        p = lax.dot_general(a_ref[...].astype(BF16), b_ref[...].astype(BF16), dims, preferred_element_type=F32)

        @pl.when(k == 0)
        def _():
            acc_ref[...] = p

        @pl.when(k > 0)
        def _():
            acc_ref[...] += p

        @pl.when(k == nk - 1)
        def _():
            o_ref[...] = acc_ref[...].astype(out_dtype).reshape(o_ref.shape)

    a_spec = pl.BlockSpec((tk, tm), lambda i, j, k: (k, i)) if ta else pl.BlockSpec((tm, tk), lambda i, j, k: (i, k))
    b_spec = pl.BlockSpec((tn, tk), lambda i, j, k: (j, k)) if tb else pl.BlockSpec((tk, tn), lambda i, j, k: (k, j))
    if shard_out:
        o_spec = pl.BlockSpec((1, tm, tn), lambda i, j, k: (j, i, 0))
        o_shape = _sds((N // tn, M, tn), out_dtype)
    else:
        o_spec = pl.BlockSpec((tm, tn), lambda i, j, k: (i, j))
        o_shape = _sds((M, N), out_dtype)
    return pl.pallas_call(
        body, name=name, grid=(M // tm, N // tn, nk), in_specs=[a_spec, b_spec], out_specs=o_spec, out_shape=o_shape,
        scratch_shapes=[pltpu.VMEM((tm, tn), F32)], compiler_params=_cp(("parallel", "parallel", "arbitrary")),
    )(a, b)


def _ln_fwd(x, w):
    L, D = x.shape
    tm = min(256, L)

    def body(x_ref, w_ref, h_ref, r_ref):
        xv = x_ref[...]
        r = lax.rsqrt(jnp.mean(xv * xv, axis=-1, keepdims=True) + EPS)
        h_ref[...] = (xv * r * w_ref[...]).astype(BF16)
        r_ref[...] = r

    return pl.pallas_call(
        body, name="ln_fwd", grid=(L // tm,),
        in_specs=[pl.BlockSpec((tm, D), lambda i: (i, 0)), pl.BlockSpec((1, D), lambda i: (0, 0))],
        out_specs=[pl.BlockSpec((tm, D), lambda i: (i, 0)), pl.BlockSpec((tm, 1), lambda i: (i, 0))],
        out_shape=[_sds((L, D), BF16), _sds((L, 1))], compiler_params=_cp(("parallel",)),
    )(x, w)


def _ln_bwd(x, r, w, dh, dx2):
    L, D = x.shape
    tm = min(256, L)

    def body(x_ref, r_ref, w_ref, dh_ref, dx2_ref, dx_ref, dw_ref):
        i = pl.program_id(0)
        xv, rv, dhv = x_ref[...], r_ref[...], dh_ref[...]
        t = dhv * w_ref[...]
        m = jnp.mean(t * xv, axis=-1, keepdims=True)
        dx_ref[...] = dx2_ref[...] + rv * t - xv * (rv * rv * rv) * m
        part = jnp.sum(dhv * xv * rv, axis=0, keepdims=True)

        @pl.when(i == 0)
        def _():
            dw_ref[...] = part

        @pl.when(i > 0)
        def _():
            dw_ref[...] += part

    row = pl.BlockSpec((tm, D), lambda i: (i, 0))
    return pl.pallas_call(
        body, name="ln_bwd", grid=(L // tm,),
        in_specs=[row, pl.BlockSpec((tm, 1), lambda i: (i, 0)), pl.BlockSpec((1, D), lambda i: (0, 0)), row, row],
        out_specs=[row, pl.BlockSpec((1, D), lambda i: (0, 0))],
        out_shape=[_sds((L, D)), _sds((1, D))], compiler_params=_cp(("arbitrary",)),
    )(x, r, w, dh, dx2)


def _s5_param_math(lam_re, lam_im, log_step, b_re, b_im, expand):
    step = jnp.exp(log_step)
    mag = jnp.exp(lam_re * step)
    abar_re = mag * jnp.cos(lam_im * step)
    abar_im = mag * jnp.sin(lam_im * step)
    den = lam_re * lam_re + lam_im * lam_im
    xr = abar_re - 1.0
    f_re = (xr * lam_re + abar_im * lam_im) / den
    f_im = (abar_im * lam_re - xr * lam_im) / den
    fe_re = jnp.dot(f_re, expand, precision=HI, preferred_element_type=F32)
    fe_im = jnp.dot(f_im, expand, precision=HI, preferred_element_type=F32)
    bb_re = fe_re * b_re - fe_im * b_im
    bb_im = fe_re * b_im + fe_im * b_re
    return abar_re, abar_im, bb_re, bb_im


def _s5_expand():
    p = lax.broadcasted_iota(jnp.int32, (S5_STATE, S5_STATE * S5_GROUP), 0)
    q = lax.broadcasted_iota(jnp.int32, (S5_STATE, S5_STATE * S5_GROUP), 1)
    return (q // S5_GROUP == p).astype(F32)


def _s5_param_fwd(lam_re, lam_im, log_step, b_re, b_im):
    G, P = lam_re.shape

    def body(lr, li, ls, br, bi, ar_o, ai_o, bbr_o, bbi_o):
        outs = _s5_param_math(lr[...], li[...], ls[...], br[...], bi[...], _s5_expand())
        for o, v in zip((ar_o, ai_o, bbr_o, bbi_o), outs):
            o[...] = v

    return pl.pallas_call(
        body, name="s5_param_fwd",
        out_shape=[_sds((G, P)), _sds((G, P)), _sds(b_re.shape), _sds(b_re.shape)], compiler_params=_cp(),
    )(lam_re, lam_im, log_step, b_re, b_im)


def _s5_param_bwd(lam_re, lam_im, log_step, b_re, b_im, dar, dai, dbbr, dbbi):
    G, P = lam_re.shape

    def body(lr, li, ls, br, bi, g0, g1, g2, g3, dlr, dli, dls, dbr, dbi):
        ex = _s5_expand()
        _, f = jax.vjp(lambda a, b, c, d, e: _s5_param_math(a, b, c, d, e, ex), lr[...], li[...], ls[...], br[...], bi[...])
        grads = f((g0[...], g1[...], g2[...], g3[...]))
        for o, v in zip((dlr, dli, dls, dbr, dbi), grads):
            o[...] = v

    return pl.pallas_call(
        body, name="s5_param_bwd",
        out_shape=[_sds((G, P)), _sds((G, P)), _sds((G, 1)), _sds(b_re.shape), _sds(b_re.shape)], compiler_params=_cp(),
    )(lam_re, lam_im, log_step, b_re, b_im, dar, dai, dbbr, dbbi)


def _seg_scan(ar, ai, re_ref, im_ref, pwr_ref, pwi_ref, end_r_ref, end_i_ref, L, row0, reverse):
    S = L // S5_SEG
    NB = re_ref.shape[0]
    LN = 128
    ars = [ar[:, b * LN:(b + 1) * LN] for b in range(NB)]
    ais = [ai[:, b * LN:(b + 1) * LN] for b in range(NB)]
    ar8 = [jnp.broadcast_to(v, (S5_SEG, LN)) for v in ars]
    ai8 = [jnp.broadcast_to(v, (S5_SEG, LN)) for v in ais]

    def step(j, carry):
        r = (S - 1 - j) if reverse else j
        rows = pl.ds(row0 + r, S5_SEG, stride=S)
        out = []
        for b in range(NB):
            sr, si, pr, pi = carry[b]
            nr = ar8[b] * sr - ai8[b] * si + re_ref.at[b][rows, :]
            ni = ar8[b] * si + ai8[b] * sr + im_ref.at[b][rows, :]
            re_ref.at[b][rows, :] = nr
            im_ref.at[b][rows, :] = ni
            qr = ars[b] * pr - ais[b] * pi
            qi = ars[b] * pi + ais[b] * pr
            pwr_ref[b, pl.ds(r, 1), :] = qr
            pwi_ref[b, pl.ds(r, 1), :] = qi
            out.append((nr, ni, qr, qi))
        return tuple(out)

    z8 = jnp.zeros((S5_SEG, LN), F32)
    init = tuple((z8, z8, jnp.ones((1, LN), F32), jnp.zeros((1, LN), F32)) for _ in range(NB))
    fin = lax.fori_loop(0, S, step, init)
    order = range(S5_SEG - 2, -1, -1) if reverse else range(1, S5_SEG)
    first = S5_SEG - 1 if reverse else 0
    for b in range(NB):
        sr, si, pr, pi = fin[b]
        end_r_ref[b] = sr
        end_i_ref[b] = si
        cr, ci = end_r_ref[b, pl.ds(first, 1), :], end_i_ref[b, pl.ds(first, 1), :]
        for i in order:
            rows = pl.ds(row0 + i * S, S)
            pwr, pwi = pwr_ref[b], pwi_ref[b]
            re_ref.at[b][rows, :] += pwr * cr - pwi * ci
            im_ref.at[b][rows, :] += pwr * ci + pwi * cr
            er, ei = end_r_ref[b, pl.ds(i, 1), :], end_i_ref[b, pl.ds(i, 1), :]
            cr, ci = er + pr * cr - pi * ci, ei + pr * ci + pi * cr


def _s5_scratch(L, cs, pad):
    S, NB = L // S5_SEG, cs // 128
    return [pltpu.VMEM((NB, L + pad, 128), F32), pltpu.VMEM((NB, L + pad, 128), F32), pltpu.VMEM((NB, S, 128), F32),
            pltpu.VMEM((NB, S, 128), F32), pltpu.VMEM((NB, S5_SEG, 128), F32), pltpu.VMEM((NB, S5_SEG, 128), F32)]


def _s5_core_fwd(proj, wbr, wbi, a_re, a_im, cbr, cbi):
    L = proj.shape[0]
    nb, ci, cs = wbr.shape
    NB = cs // 128

    def body(u_ref, wbr_ref, wbi_ref, ar_ref, ai_ref, cbr_ref, cbi_ref, y_ref, sr, si, pwr, pwi, er, ei):
        u = u_ref[...].astype(BF16)
        for b in range(NB):
            lanes = pl.ds(b * 128, 128)
            sr[b] = jnp.dot(u, wbr_ref[0, :, lanes], preferred_element_type=F32)
            si[b] = jnp.dot(u, wbi_ref[0, :, lanes], preferred_element_type=F32)
        _seg_scan(ar_ref[...], ai_ref[...], sr, si, pwr, pwi, er, ei, L, 0, False)
        y = jnp.zeros((L, ci), F32)
        for b in range(NB):
            lanes = pl.ds(b * 128, 128)
            y = y + (jnp.dot(sr[b].astype(BF16), cbr_ref[0, lanes, :], preferred_element_type=F32)
                     - jnp.dot(si[b].astype(BF16), cbi_ref[0, lanes, :], preferred_element_type=F32))
        y_ref[...] = y

    wspec = pl.BlockSpec((1, ci, cs), lambda j: (j, 0, 0))
    aspec = pl.BlockSpec((1, cs), lambda j: (0, j))
    cspec = pl.BlockSpec((1, cs, ci), lambda j: (j, 0, 0))
    return pl.pallas_call(
        body, name="s5_core_fwd", grid=(nb,),
        in_specs=[pl.BlockSpec((L, ci), lambda j: (0, OFF_US // ci + j)), wspec, wspec, aspec, aspec, cspec, cspec],
        out_specs=pl.BlockSpec((L, ci), lambda j: (0, j)), out_shape=_sds((L, nb * ci)),
        scratch_shapes=_s5_scratch(L, cs, 0), compiler_params=_cp(("arbitrary",)),
    )(proj, wbr, wbi, a_re, a_im, cbr, cbi)


def _s5_core_bwd(proj, wbr, wbi, a_re, a_im, cbr, cbi, dyc, du1):
    L = proj.shape[0]
    nb, ci, cs = wbr.shape
    NB = cs // 128
    PAD = 8
    RC = min(256, L)

    def body(u_ref, wbr_ref, wbi_ref, ar_ref, ai_ref, cbr_ref, cbi_ref, dy_ref, du1_ref,
             du_ref, dwbr_ref, dwbi_ref, dcbr_ref, dcbi_ref, dar_ref, dai_ref, sr, si, pwr, pwi, er, ei, lr, li):
        tn = (((0,), (0,)), ((), ()))
        nt = (((1,), (1,)), ((), ()))
        u = u_ref[...].astype(BF16)
        dy = dy_ref[...].astype(BF16)
        ar, ai = ar_ref[...], ai_ref[...]
        for b in range(NB):
            lanes = pl.ds(b * 128, 128)
            sr[b, pl.ds(0, PAD), :] = jnp.zeros((PAD, 128), F32)
            si[b, pl.ds(0, PAD), :] = jnp.zeros((PAD, 128), F32)
            sr[b, pl.ds(PAD, L), :] = jnp.dot(u, wbr_ref[0, :, lanes], preferred_element_type=F32)
            si[b, pl.ds(PAD, L), :] = jnp.dot(u, wbi_ref[0, :, lanes], preferred_element_type=F32)
        _seg_scan(ar, ai, sr, si, pwr, pwi, er, ei, L, PAD, False)
        for b in range(NB):
            lanes = pl.ds(b * 128, 128)
            lr[b] = lax.dot_general(dy, cbr_ref[0, lanes, :], nt, preferred_element_type=F32)
            li[b] = -lax.dot_general(dy, cbi_ref[0, lanes, :], nt, preferred_element_type=F32)
            dcbr_ref[0, lanes, :] = lax.dot_general(sr[b, pl.ds(PAD, L), :].astype(BF16), dy, tn, preferred_element_type=F32)
            dcbi_ref[0, lanes, :] = -lax.dot_general(si[b, pl.ds(PAD, L), :].astype(BF16), dy, tn, preferred_element_type=F32)
        _seg_scan(ar, -ai, lr, li, pwr, pwi, er, ei, L, 0, True)
        du = du1_ref[...]
        for b in range(NB):
            lanes = pl.ds(b * 128, 128)
            dar = jnp.zeros((1, 128), F32)
            dai = jnp.zeros((1, 128), F32)
            for i in range(L // RC):
                pr_, pi_ = sr[b, pl.ds(PAD - 1 + i * RC, RC), :], si[b, pl.ds(PAD - 1 + i * RC, RC), :]
                gr, gi = lr[b, pl.ds(i * RC, RC), :], li[b, pl.ds(i * RC, RC), :]
                dar = dar + jnp.sum(gr * pr_ + gi * pi_, axis=0, keepdims=True)
                dai = dai + jnp.sum(gi * pr_ - gr * pi_, axis=0, keepdims=True)
            dar_ref[:, lanes] = dar
            dai_ref[:, lanes] = dai
            gr, gi = lr[b].astype(BF16), li[b].astype(BF16)
            du = du + (lax.dot_general(gr, wbr_ref[0, :, lanes], nt, preferred_element_type=F32)
                       + lax.dot_general(gi, wbi_ref[0, :, lanes], nt, preferred_element_type=F32))
            dwbr_ref[0, :, lanes] = lax.dot_general(u, gr, tn, preferred_element_type=F32)
            dwbi_ref[0, :, lanes] = lax.dot_general(u, gi, tn, preferred_element_type=F32)
        du_ref[...] = du.astype(BF16)

    wspec = pl.BlockSpec((1, ci, cs), lambda j: (j, 0, 0))
    aspec = pl.BlockSpec((1, cs), lambda j: (0, j))
    cspec = pl.BlockSpec((1, cs, ci), lambda j: (j, 0, 0))
    col = pl.BlockSpec((L, ci), lambda j: (0, j))
    return pl.pallas_call(
        body, name="s5_core_bwd", grid=(nb,),
        in_specs=[pl.BlockSpec((L, ci), lambda j: (0, OFF_US // ci + j)), wspec, wspec, aspec, aspec, cspec, cspec, col, col],
        out_specs=[col, wspec, wspec, cspec, cspec, aspec, aspec],
        out_shape=[_sds((L, nb * ci), BF16), _sds(wbr.shape), _sds(wbr.shape), _sds(cbr.shape), _sds(cbr.shape),
                   _sds((1, nb * cs)), _sds((1, nb * cs))],
        scratch_shapes=_s5_scratch(L, cs, PAD) + [pltpu.VMEM((NB, L, 128), F32), pltpu.VMEM((NB, L, 128), F32)],
        compiler_params=_cp(("arbitrary",)),
    )(proj, wbr, wbi, a_re, a_im, cbr, cbi, dyc, du1)


def _s5_post_math(yc, u, z, d, wg):
    y = yc + d * u
    y1 = jax.nn.gelu(y)
    t = jnp.dot(y1.astype(BF16), wg, preferred_element_type=F32)
    sg = _sigmoid(t)
    return y, y1, sg


def _s5_post_fwd(yc, proj, d, wg):
    L, W = yc.shape
    tm = min(256, L)

    def body(yc_ref, u_ref, z_ref, d_ref, wg_ref, o_ref):
        _, y1, sg = _s5_post_math(yc_ref[...], u_ref[...], z_ref[...], d_ref[...], wg_ref[...])
        o_ref[...] = (y1 * sg * _silu(z_ref[...])).astype(BF16)

    row = pl.BlockSpec((tm, W), lambda i: (i, 0))
    return pl.pallas_call(
        body, name="s5_post_fwd", grid=(L // tm,),
        in_specs=[row, pl.BlockSpec((tm, W), lambda i: (i, OFF_US // W)), pl.BlockSpec((tm, W), lambda i: (i, OFF_ZS // W)),
                  pl.BlockSpec((1, W), lambda i: (0, 0)), pl.BlockSpec((W, W), lambda i: (0, 0))],
        out_specs=row, out_shape=_sds((L, W), BF16), compiler_params=_cp(("parallel",)),
    )(yc, proj, proj, d, wg)


def _s5_post_bwd(yc, proj, d, wg, dout):
    L, W = yc.shape
    tm = min(256, L)

    def body(yc_ref, u_ref, z_ref, d_ref, wg_ref, do_ref, dyc_ref, du_ref, dz_ref, dd_ref, dwg_ref):
        i = pl.program_id(0)
        u, z, d_, wgv = u_ref[...], z_ref[...], d_ref[...], wg_ref[...]
        y, y1, sg = _s5_post_math(yc_ref[...], u, z, d_, wgv)
        dout_ = do_ref[...]
        y2 = y1 * sg
        dy2 = dout_ * _silu(z)
        dz_ref[...] = (dout_ * y2 * _dsilu(z)).astype(BF16)
        dt = (dy2 * y1 * sg * (1.0 - sg)).astype(BF16)
        dy1 = dy2 * sg + lax.dot_general(dt, wgv, (((1,), (1,)), ((), ())), preferred_element_type=F32)
        _, gelu_vjp = jax.vjp(jax.nn.gelu, y)
        dy = gelu_vjp(dy1)[0]
        dyc_ref[...] = dy
        du_ref[...] = dy * d_
        dd_part = jnp.sum(dy * u, axis=0, keepdims=True)
        dwg_part = lax.dot_general(y1.astype(BF16), dt, (((0,), (0,)), ((), ())), preferred_element_type=F32)

        @pl.when(i == 0)
        def _():
            dd_ref[...] = dd_part
            dwg_ref[...] = dwg_part

        @pl.when(i > 0)
        def _():
            dd_ref[...] += dd_part
            dwg_ref[...] += dwg_part

    row = pl.BlockSpec((tm, W), lambda i: (i, 0))
    return pl.pallas_call(
        body, name="s5_post_bwd", grid=(L // tm,),
        in_specs=[row, pl.BlockSpec((tm, W), lambda i: (i, OFF_US // W)), pl.BlockSpec((tm, W), lambda i: (i, OFF_ZS // W)),
                  pl.BlockSpec((1, W), lambda i: (0, 0)), pl.BlockSpec((W, W), lambda i: (0, 0)), row],
        out_specs=[row, row, row, pl.BlockSpec((1, W), lambda i: (0, 0)), pl.BlockSpec((W, W), lambda i: (0, 0))],
        out_shape=[_sds((L, W)), _sds((L, W)), _sds((L, W), BF16), _sds((1, W)), _sds((W, W))],
        compiler_params=_cp(("arbitrary",)),
    )(yc, proj, proj, d, wg, dout)


def _shift_down(x, s):
    if s == 0:
        return x
    rows = lax.broadcasted_iota(jnp.int32, x.shape, 0)
    return jnp.where(rows >= s, pltpu.roll(x, s, 0), 0.0)


def _shift_up(x, s):
    if s == 0:
        return x
    L = x.shape[0]
    rows = lax.broadcasted_iota(jnp.int32, x.shape, 0)
    return jnp.where(rows < L - s, pltpu.roll(x, L - s, 0), 0.0)


def _conv_pre(x, w):
    acc = w[CONV_K - 1:CONV_K, :] * x
    for s in range(1, CONV_K):
        acc = acc + w[CONV_K - 1 - s:CONV_K - s, :] * _shift_down(x, s)
    return acc


def _dn_conv_fwd(proj, conv_w):
    L = proj.shape[0]
    W = DN_HEAD_DIM
    nq = 2 * DN_HEADS

    def body(x_ref, w_ref, o_ref):
        j = pl.program_id(0)
        act = _silu(_conv_pre(x_ref[...], w_ref[...]))
        r = lax.rsqrt(jnp.sum(act * act, axis=-1, keepdims=True) + EPS)
        scale = jnp.where(j < DN_HEADS, DN_HEAD_DIM ** -0.5, 1.0)
        o_ref[...] = jnp.where(j < nq, act * r * scale, act)

    return pl.pallas_call(
        body, name="dn_conv_fwd", grid=(3 * DN_HEADS,),
        in_specs=[pl.BlockSpec((L, W), lambda j: (0, OFF_Q // W + j)), pl.BlockSpec((CONV_K, W), lambda j: (0, j))],
        out_specs=pl.BlockSpec((L, W), lambda j: (0, j)), out_shape=_sds((L, 3 * D_DN)), compiler_params=_cp(("parallel",)),
    )(proj, conv_w)


def _dn_conv_bwd(proj, conv_w, dout):
    L = proj.shape[0]
    W = DN_HEAD_DIM
    nq = 2 * DN_HEADS

    def body(x_ref, w_ref, do_ref, dx_ref, dw_ref):
        j = pl.program_id(0)
        x, w, dout_ = x_ref[...], w_ref[...], do_ref[...]
        pre = _conv_pre(x, w)
        act = _silu(pre)
        r = lax.rsqrt(jnp.sum(act * act, axis=-1, keepdims=True) + EPS)
        scale = jnp.where(j < DN_HEADS, DN_HEAD_DIM ** -0.5, 1.0)
        g = dout_ * scale
        dact_n = r * g - act * (r * r * r) * jnp.sum(g * act, axis=-1, keepdims=True)
        dact = jnp.where(j < nq, dact_n, dout_)
        dpre = dact * _dsilu(pre)
        dx = w[CONV_K - 1:CONV_K, :] * dpre
        for s in range(1, CONV_K):
            dx = dx + w[CONV_K - 1 - s:CONV_K - s, :] * _shift_up(dpre, s)
        dx_ref[...] = dx.astype(BF16)
        for s in range(CONV_K):
            dw_ref[pl.ds(CONV_K - 1 - s, 1), :] = jnp.sum(dpre * _shift_down(x, s), axis=0, keepdims=True)

    col = pl.BlockSpec((L, W), lambda j: (0, j))
    wsp = pl.BlockSpec((CONV_K, W), lambda j: (0, j))
    return pl.pallas_call(
        body, name="dn_conv_bwd", grid=(3 * DN_HEADS,),
        in_specs=[pl.BlockSpec((L, W), lambda j: (0, OFF_Q // W + j)), wsp, col], out_specs=[col, wsp],
        out_shape=[_sds((L, 3 * D_DN), BF16), _sds((CONV_K, 3 * D_DN))], compiler_params=_cp(("parallel",)),
    )(proj, conv_w, dout)


def _softplus(x):
    return jnp.maximum(x, 0.0) + jnp.log(1.0 + jnp.exp(-jnp.abs(x)))


def _dn_gates_fwd(proj, alog, dtb):
    L = proj.shape[0]
    W = 128

    def body(p_ref, al_ref, db_ref, o_ref):
        p = p_ref[...]
        lane = lax.broadcasted_iota(jnp.int32, p.shape, 1)
        g = -jnp.exp(al_ref[...]) * _softplus(p + db_ref[...])
        o_ref[...] = jnp.where(lane < DN_HEADS, _sigmoid(p), jnp.where(lane < 2 * DN_HEADS, g, 0.0))

    return pl.pallas_call(
        body, name="dn_gates_fwd", grid=(1,),
        in_specs=[pl.BlockSpec((L, W), lambda i: (0, OFF_B // W)), pl.BlockSpec((1, W), lambda i: (0, 0)),
                  pl.BlockSpec((1, W), lambda i: (0, 0))],
        out_specs=pl.BlockSpec((L, W), lambda i: (0, 0)), out_shape=_sds((L, W)), compiler_params=_cp(("arbitrary",)),
    )(proj, alog, dtb)


def _dn_gates_bwd(proj, alog, dtb, dgates):
    L = proj.shape[0]
    W = 128

    def body(p_ref, al_ref, db_ref, dg_ref, dp_ref, dal_ref, ddb_ref):
        p, dg = p_ref[...], dg_ref[...]
        lane = lax.broadcasted_iota(jnp.int32, p.shape, 1)
        is_g = jnp.logical_and(lane >= DN_HEADS, lane < 2 * DN_HEADS)
        beta = _sigmoid(p)
        na = -jnp.exp(al_ref[...])
        xs = p + db_ref[...]
        dsp = dg * na * _sigmoid(xs)
        dp_ref[...] = jnp.where(lane < DN_HEADS, dg * beta * (1.0 - beta), jnp.where(is_g, dsp, 0.0)).astype(BF16)
        dal_ref[...] = jnp.sum(jnp.where(is_g, dg * na * _softplus(xs), 0.0), axis=0, keepdims=True)
        ddb_ref[...] = jnp.sum(jnp.where(is_g, dsp, 0.0), axis=0, keepdims=True)

    one = pl.BlockSpec((1, W), lambda i: (0, 0))
    full = pl.BlockSpec((L, W), lambda i: (0, 0))
    return pl.pallas_call(
        body, name="dn_gates_bwd", grid=(1,),
        in_specs=[pl.BlockSpec((L, W), lambda i: (0, OFF_B // W)), one, one, full], out_specs=[full, one, one],
        out_shape=[_sds((L, W), BF16), _sds((1, W)), _sds((1, W))], compiler_params=_cp(("arbitrary",)),
    )(proj, alog, dtb, dgates)


def _bdot(a, b, dims):
    return lax.dot_general(a.astype(BF16), b.astype(BF16), (dims, ((), ())), preferred_element_type=F32)


_NN, _NT, _TN = ((1,), (0,)), ((1,), (1,)), ((0,), (0,))


def _dot3(a, b, dims):
    ah, bh = a.astype(BF16), b.astype(BF16)
    al, bl = (a - ah.astype(F32)).astype(BF16), (b - bh.astype(F32)).astype(BF16)
    d = lambda x, y: lax.dot_general(x, y, (dims, ((), ())), preferred_element_type=F32)
    return d(ah, bh) + (d(ah, bl) + d(al, bh))


def _mm_family(raw):
    nn = jax.custom_vjp(lambda a, b: raw(a, b, _NN))
    nt = jax.custom_vjp(lambda a, b: raw(a, b, _NT))
    tn = jax.custom_vjp(lambda a, b: raw(a, b, _TN))
    nn.defvjp(lambda a, b: (raw(a, b, _NN), (a, b)), lambda r, g: (raw(g, r[1], _NT), raw(r[0], g, _TN)))
    nt.defvjp(lambda a, b: (raw(a, b, _NT), (a, b)), lambda r, g: (raw(g, r[1], _NN), raw(g, r[0], _TN)))
    tn.defvjp(lambda a, b: (raw(a, b, _TN), (a, b)), lambda r, g: (raw(r[1], g, _NT), raw(r[0], g, _NN)))
    return nn, nt, tn


_mm_nn, _mm_nt, _mm_tn = _mm_family(_bdot)
_m3_nn, _m3_nt, _m3_tn = _mm_family(_dot3)


def _tri_apply(x, upper):
    C = x.shape[0]
    ii = lax.broadcasted_iota(jnp.int32, (C, C), 0)
    jj = lax.broadcasted_iota(jnp.int32, (C, C), 1)
    mat = ((ii <= jj) if upper else (ii >= jj)).astype(BF16)
    hi = x.astype(BF16)
    r = x - hi.astype(F32)
    mid = r.astype(BF16)
    lo = (r - mid.astype(F32)).astype(BF16)
    d = lambda y: jnp.dot(mat, y, preferred_element_type=F32)
    return d(hi) + (d(mid) + d(lo))


_cumsum_rows = jax.custom_vjp(lambda x: _tri_apply(x, False))
_cumsum_rows.defvjp(lambda x: (_tri_apply(x, False), None), lambda _, g: (_tri_apply(g, True),))


def _uli(a):
    C = a.shape[0]
    ii = lax.broadcasted_iota(jnp.int32, (C, C), 0)
    jj = lax.broadcasted_iota(jnp.int32, (C, C), 1)
    t = jnp.where(ii == jj, 1.0, 0.0) - a
    m = a
    for _ in range(int(math.log2(C)) - 1):
        m = _dot3(m, m, _NN)
        t = t + _dot3(t, m, _NN)
    return t


_unit_lower_inverse = jax.custom_vjp(_uli)
_unit_lower_inverse.defvjp(lambda a: (lambda t: (t, t))(_uli(a)), lambda t, g: (-_dot3(_dot3(t, g, _TN), t, _NT),))


def _prep_math(q, k, v, gcol, bcol):
    C, dv = v.shape
    ii = lax.broadcasted_iota(jnp.int32, (C, C), 0)
    jj = lax.broadcasted_iota(jnp.int32, (C, C), 1)
    causal = ii >= jj
    strict = ii > jj
    ones = jnp.ones((C, dv), F32)
    dm = _cumsum_rows(gcol * strict.astype(F32))
    decay = jnp.where(causal, jnp.exp(jnp.where(causal, dm, 0.0)), 0.0)
    gcb = _cumsum_rows(gcol * ones)
    glast = jnp.sum(gcol * ones, axis=0, keepdims=True)
    eg = jnp.exp(gcb)
    a = jnp.where(strict, bcol * _mm_nt(k, k) * decay, 0.0)
    t = _unit_lower_inverse(a)
    u_c = _m3_nn(t, v * bcol)
    w_c = _m3_nn(t, k * bcol * eg)
    qk = _mm_nt(q, k) * decay
    return u_c, w_c, q * eg, k * jnp.exp(glast - gcb), qk, jnp.exp(glast)


def _gate_cols(gates, h):
    lane = lax.broadcasted_iota(jnp.int32, gates.shape, 1)
    bcol = jnp.sum(jnp.where(lane == h, gates, 0.0), axis=1, keepdims=True)
    gcol = jnp.sum(jnp.where(lane == h + DN_HEADS, gates, 0.0), axis=1, keepdims=True)
    return gcol, bcol


DN_HB = 4


def _dn_prep_fwd(qkv, gates):
    L = qkv.shape[0]
    N, H, d, HB = L // CHUNK, DN_HEADS, DN_HEAD_DIM, DN_HB

    def body(q_ref, k_ref, v_ref, g_ref, u_ref, w_ref, qd_ref, kd_ref, qk_ref, egl_ref):
        h0 = pl.program_id(1) * HB
        gates_ = g_ref[...]
        for i in range(HB):
            lanes = pl.ds(i * d, d)
            gcol, bcol = _gate_cols(gates_, h0 + i)
            u, w, qd, kd, qk, egl = _prep_math(q_ref[:, lanes], k_ref[:, lanes], v_ref[:, lanes], gcol, bcol)
            u_ref[:, lanes] = u
            w_ref[:, lanes] = w
            qd_ref[:, lanes] = qd
            kd_ref[:, lanes] = kd
            qk_ref[0, i] = qk
            egl_ref[0, i] = jnp.broadcast_to(egl, (8, d))

    blk = lambda off: pl.BlockSpec((CHUNK, HB * d), lambda n, j: (n, off // HB + j))
    cc = pl.BlockSpec((1, HB, CHUNK, CHUNK), lambda n, j: (n, j, 0, 0))
    ee = pl.BlockSpec((1, HB, 8, d), lambda n, j: (n, j, 0, 0))
    big = _sds((L, D_DN))
    return pl.pallas_call(
        body, name="dn_prep_fwd", grid=(N, H // HB),
        in_specs=[blk(0), blk(H), blk(2 * H), pl.BlockSpec((CHUNK, 128), lambda n, j: (n, 0))],
        out_specs=[blk(0), blk(0), blk(0), blk(0), cc, ee],
        out_shape=[big, big, big, big, _sds((N, H, CHUNK, CHUNK)), _sds((N, H, 8, d))],
        compiler_params=_cp(("parallel", "parallel")),
    )(qkv, qkv, qkv, gates)


def _dn_scan_fwd(u, w, qd, kd, qk, egl):
    L = u.shape[0]
    N, H, d, HB = L // CHUNK, DN_HEADS, DN_HEAD_DIM, DN_HB

    def body(u_ref, w_ref, qd_ref, kd_ref, qk_ref, egl_ref, o_ref, st_ref, s_ref):
        n, h0 = pl.program_id(0), pl.program_id(1) * HB

        @pl.when(n == 0)
        def _():
            for i in range(HB):
                s_ref[h0 + i] = jnp.zeros((d, d), F32)

        for i in range(HB):
            lanes = pl.ds(i * d, d)
            state = s_ref[h0 + i]
            st_ref[0, i] = state
            vn = u_ref[:, lanes] - _bdot(w_ref[:, lanes], state, _NN)
            o_ref[:, lanes] = _bdot(qd_ref[:, lanes], state, _NN) + _bdot(qk_ref[0, i], vn, _NN)
            s_ref[h0 + i] = state * egl_ref[0, i, pl.ds(0, 1), :] + _bdot(kd_ref[:, lanes], vn, _TN)

    blk = pl.BlockSpec((CHUNK, HB * d), lambda n, j: (n, j))
    cc = pl.BlockSpec((1, HB, CHUNK, CHUNK), lambda n, j: (n, j, 0, 0))
    ee = pl.BlockSpec((1, HB, 8, d), lambda n, j: (n, j, 0, 0))
    return pl.pallas_call(
        body, name="dn_scan_fwd", grid=(N, H // HB), in_specs=[blk, blk, blk, blk, cc, ee],
        out_specs=[blk, pl.BlockSpec((1, HB, d, d), lambda n, j: (n, j, 0, 0))],
        out_shape=[_sds((L, D_DN)), _sds((N, H, d, d))], scratch_shapes=[pltpu.VMEM((H, d, d), F32)],
        compiler_params=_cp(("arbitrary", "arbitrary")),
    )(u, w, qd, kd, qk, egl)


def _dn_scan_bwd(u, w, qd, kd, qk, egl, states, do):
    L = u.shape[0]
    N, H, d, HB = L // CHUNK, DN_HEADS, DN_HEAD_DIM, DN_HB

    def body(u_ref, w_ref, qd_ref, kd_ref, qk_ref, egl_ref, st_ref, do_ref,
             du_ref, dw_ref, dqd_ref, dkd_ref, dqk_ref, degl_ref, ds_ref):
        n, h0 = pl.program_id(0), pl.program_id(1) * HB

        @pl.when(n == 0)
        def _():
            for i in range(HB):
                ds_ref[h0 + i] = jnp.zeros((d, d), F32)

        for i in range(HB):
            lanes = pl.ds(i * d, d)
            state, dsn = st_ref[0, i], ds_ref[h0 + i]
            w_, kd_, qd_, do_ = w_ref[:, lanes], kd_ref[:, lanes], qd_ref[:, lanes], do_ref[:, lanes]
            vn = u_ref[:, lanes] - _bdot(w_, state, _NN)
            dvn = _bdot(qk_ref[0, i], do_, _TN) + _bdot(kd_, dsn, _NN)
            du_ref[:, lanes] = dvn
            dw_ref[:, lanes] = -_bdot(dvn, state, _NT)
            dqd_ref[:, lanes] = _bdot(do_, state, _NT)
            dkd_ref[:, lanes] = _bdot(vn, dsn, _NT)
            dqk_ref[0, i] = _bdot(do_, vn, _NT)
            degl_ref[0, i] = jnp.broadcast_to(jnp.sum(dsn * state, keepdims=True), (8, d))
            ds_ref[h0 + i] = (_bdot(qd_, do_, _TN) - _bdot(w_, dvn, _TN)) + dsn * egl_ref[0, i, pl.ds(0, 1), :]

    blk = pl.BlockSpec((CHUNK, HB * d), lambda n, j: (N - 1 - n, j))
    cc = pl.BlockSpec((1, HB, CHUNK, CHUNK), lambda n, j: (N - 1 - n, j, 0, 0))
    ee = pl.BlockSpec((1, HB, 8, d), lambda n, j: (N - 1 - n, j, 0, 0))
    ss = pl.BlockSpec((1, HB, d, d), lambda n, j: (N - 1 - n, j, 0, 0))
    big = _sds((L, D_DN))
    return pl.pallas_call(
        body, name="dn_scan_bwd", grid=(N, H // HB), in_specs=[blk, blk, blk, blk, cc, ee, ss, blk],
        out_specs=[blk, blk, blk, blk, cc, ee],
        out_shape=[big, big, big, big, _sds((N, H, CHUNK, CHUNK)), _sds((N, H, 8, d))],
        scratch_shapes=[pltpu.VMEM((H, d, d), F32)], compiler_params=_cp(("arbitrary", "arbitrary")),
    )(u, w, qd, kd, qk, egl, states, do)


def _dn_prep_bwd(qkv, gates, du, dw, dqd, dkd, dqk, degl):
    L = qkv.shape[0]
    N, H, d, HB = L // CHUNK, DN_HEADS, DN_HEAD_DIM, DN_HB

    def body(q_ref, k_ref, v_ref, g_ref, du_ref, dw_ref, dqd_ref, dkd_ref, dqk_ref, degl_ref, dq_ref, dk_ref, dv_ref, dg_ref):
        j = pl.program_id(1)
        h0 = j * HB
        gates_ = g_ref[...]
        lane = lax.broadcasted_iota(jnp.int32, gates_.shape, 1)
        lane1 = lax.broadcasted_iota(jnp.int32, (1, d), 1)
        part = jnp.zeros(gates_.shape, F32)
        for i in range(HB):
            lanes = pl.ds(i * d, d)
            gcol, bcol = _gate_cols(gates_, h0 + i)
            _, f = jax.vjp(_prep_math, q_ref[:, lanes], k_ref[:, lanes], v_ref[:, lanes], gcol, bcol)
            cot_egl = jnp.where(lane1 == 0, degl_ref[0, i, pl.ds(0, 1), :], 0.0)
            dq, dk, dv, dgc, dbc = f((du_ref[:, lanes], dw_ref[:, lanes], dqd_ref[:, lanes], dkd_ref[:, lanes], dqk_ref[0, i], cot_egl))
            dq_ref[:, lanes] = dq
            dk_ref[:, lanes] = dk
            dv_ref[:, lanes] = dv
            part = part + jnp.where(lane == h0 + i, dbc, 0.0) + jnp.where(lane == h0 + i + DN_HEADS, dgc, 0.0)

        @pl.when(j == 0)
        def _():
            dg_ref[...] = part

        @pl.when(j > 0)
        def _():
            dg_ref[...] += part

    blk = lambda off: pl.BlockSpec((CHUNK, HB * d), lambda n, j: (n, off // HB + j))
    gsp = pl.BlockSpec((CHUNK, 128), lambda n, j: (n, 0))
    cc = pl.BlockSpec((1, HB, CHUNK, CHUNK), lambda n, j: (n, j, 0, 0))
    ee = pl.BlockSpec((1, HB, 8, d), lambda n, j: (n, j, 0, 0))
    big = _sds((L, D_DN))
    return pl.pallas_call(
        body, name="dn_prep_bwd", grid=(N, H // HB),
        in_specs=[blk(0), blk(H), blk(2 * H), gsp, blk(0), blk(0), blk(0), blk(0), cc, ee],
        out_specs=[blk(0), blk(0), blk(0), gsp], out_shape=[big, big, big, _sds((L, 128))],
        compiler_params=_cp(("parallel", "arbitrary")),
    )(qkv, qkv, qkv, gates, du, dw, dqd, dkd, dqk, degl)


def _dn_post_fwd(o, proj, nw):
    L = o.shape[0]
    d = DN_HEAD_DIM
    tm = min(512, L)

    def body(o_ref, z_ref, w_ref, y_ref):
        ov = o_ref[...]
        r = lax.rsqrt(jnp.mean(ov * ov, axis=-1, keepdims=True) + EPS)
        y_ref[...] = (ov * r * w_ref[...] * _silu(z_ref[...])).astype(BF16)

    blk = pl.BlockSpec((tm, d), lambda i, h: (i, h))
    return pl.pallas_call(
        body, name="dn_post_fwd", grid=(L // tm, DN_HEADS),
        in_specs=[blk, pl.BlockSpec((tm, d), lambda i, h: (i, OFF_ZD // d + h)), pl.BlockSpec((1, d), lambda i, h: (0, 0))],
        out_specs=blk, out_shape=_sds((L, D_DN), BF16), compiler_params=_cp(("parallel", "parallel")),
    )(o, proj, nw)


def _dn_post_bwd(o, proj, nw, dy):
    L = o.shape[0]
    d = DN_HEAD_DIM
    tm = min(512, L)

    def body(o_ref, z_ref, w_ref, dy_ref, do_ref, dz_ref, dw_ref):
        first = jnp.logical_and(pl.program_id(0) == 0, pl.program_id(1) == 0)
        ov, z, w, dyv = o_ref[...], z_ref[...], w_ref[...], dy_ref[...]
        r = lax.rsqrt(jnp.mean(ov * ov, axis=-1, keepdims=True) + EPS)
        xn = ov * r
        dz_ref[...] = (dyv * xn * w * _dsilu(z)).astype(BF16)
        dn = dyv * _silu(z)
        t = dn * w
        do_ref[...] = r * t - ov * (r * r * r) * jnp.mean(t * ov, axis=-1, keepdims=True)
        part = jnp.sum(dn * xn, axis=0, keepdims=True)

        @pl.when(first)
        def _():
            dw_ref[...] = part

        @pl.when(jnp.logical_not(first))
        def _():
            dw_ref[...] += part

    blk = pl.BlockSpec((tm, d), lambda i, h: (i, h))
    one = pl.BlockSpec((1, d), lambda i, h: (0, 0))
    return pl.pallas_call(
        body, name="dn_post_bwd", grid=(L // tm, DN_HEADS),
        in_specs=[blk, pl.BlockSpec((tm, d), lambda i, h: (i, OFF_ZD // d + h)), one, blk], out_specs=[blk, blk, one],
        out_shape=[_sds((L, D_DN)), _sds((L, D_DN), BF16), _sds((1, d))], compiler_params=_cp(("arbitrary", "arbitrary")),
    )(o, proj, nw, dy)


def _mix_fwd(s5o, dno, w_su, w_du, proj):
    L, K = s5o.shape
    N = w_su.shape[1]
    tm, tn = min(512, L), 512

    def body(a1, a2, b1, b2, gs, gd, ys_ref, yd_ref, mx_ref):
        ys = jnp.dot(a1[...], b1[...], preferred_element_type=F32)
        yd = jnp.dot(a2[...], b2[...], preferred_element_type=F32)
        ys_ref[...] = ys
        yd_ref[...] = yd
        mx_ref[...] = (_sigmoid(gs[...]) * ys + _sigmoid(gd[...]) * yd).astype(BF16)

    a = pl.BlockSpec((tm, K), lambda i, j: (i, 0))
    b = pl.BlockSpec((K, tn), lambda i, j: (0, j))
    o = pl.BlockSpec((tm, tn), lambda i, j: (i, j))
    return pl.pallas_call(
        body, name="mix_fwd", grid=(L // tm, N // tn),
        in_specs=[a, a, b, b, pl.BlockSpec((tm, tn), lambda i, j: (i, OFF_GS // tn + j)),
                  pl.BlockSpec((tm, tn), lambda i, j: (i, OFF_GD // tn + j))],
        out_specs=[o, o, o], out_shape=[_sds((L, N)), _sds((L, N)), _sds((L, N), BF16)],
        compiler_params=_cp(("parallel", "parallel")),
    )(s5o, dno, w_su, w_du, proj, proj)


def _mix_bwd(dx2b, w_out, proj, ys, yd):
    L, K = dx2b.shape
    N = w_out.shape[0]
    tm, tn = min(512, L), 512

    def body(a, b, gs, gd, ys_ref, yd_ref, dgs_ref, dgd_ref, dys_ref, dyd_ref):
        dm = lax.dot_general(a[...], b[...], (((1,), (1,)), ((), ())), preferred_element_type=F32)
        ss, sd = _sigmoid(gs[...]), _sigmoid(gd[...])
        dys_ref[...] = (dm * ss).astype(BF16)
        dyd_ref[...] = (dm * sd).astype(BF16)
        dgs_ref[...] = (dm * ys_ref[...] * ss * (1.0 - ss)).astype(BF16)
        dgd_ref[...] = (dm * yd_ref[...] * sd * (1.0 - sd)).astype(BF16)

    o = pl.BlockSpec((tm, tn), lambda i, j: (i, j))
    return pl.pallas_call(
        body, name="mix_bwd", grid=(L // tm, N // tn),
        in_specs=[pl.BlockSpec((tm, K), lambda i, j: (i, 0)), pl.BlockSpec((tn, K), lambda i, j: (j, 0)),
                  pl.BlockSpec((tm, tn), lambda i, j: (i, OFF_GS // tn + j)),
                  pl.BlockSpec((tm, tn), lambda i, j: (i, OFF_GD // tn + j)), o, o],
        out_specs=[o, o, o, o], out_shape=[_sds((L, N), BF16)] * 4, compiler_params=_cp(("parallel", "parallel")),
    )(dx2b, w_out, proj, proj, ys, yd)


def _final(mixed, w_out, x, tgt, fw):
    L, D = x.shape
    tm = min(256, L)

    def body(a_ref, b_ref, x_ref, t_ref, w_ref, dx_ref, dxb_ref, loss_ref, dw_ref):
        i = pl.program_id(0)
        x2 = x_ref[...] + jnp.dot(a_ref[...], b_ref[...], preferred_element_type=F32)
        w = w_ref[...]
        r = lax.rsqrt(jnp.mean(x2 * x2, axis=-1, keepdims=True) + EPS)
        xn = x2 * r
        e = xn * w - t_ref[...]
        lpart = 0.5 * jnp.sum(jnp.mean(e * e, axis=-1, keepdims=True), axis=0, keepdims=True)
        dy = e * (1.0 / D)
        t = dy * w
        dx2 = r * t - x2 * (r * r * r) * jnp.mean(t * x2, axis=-1, keepdims=True)
        dx_ref[...] = dx2
        dxb_ref[...] = dx2.astype(BF16)
        dwp = jnp.sum(dy * xn, axis=0, keepdims=True)
        lrow = jnp.broadcast_to(lpart, loss_ref.shape)

        @pl.when(i == 0)
        def _():
            loss_ref[...] = lrow
            dw_ref[...] = dwp

        @pl.when(i > 0)
        def _():
            loss_ref[...] += lrow
            dw_ref[...] += dwp

    row = pl.BlockSpec((tm, D), lambda i: (i, 0))
    one = pl.BlockSpec((1, D), lambda i: (0, 0))
    return pl.pallas_call(
        body, name="final", grid=(L // tm,),
        in_specs=[row, pl.BlockSpec((D, D), lambda i: (0, 0)), row, row, one],
        out_specs=[row, row, pl.BlockSpec((1, 128), lambda i: (0, 0)), one],
        out_shape=[_sds((L, D)), _sds((L, D), BF16), _sds((1, 128)), _sds((1, D))], compiler_params=_cp(("arbitrary",)),
    )(mixed, w_out, x, tgt, fw)


def _block_diag(t):
    J, g, a, b = t.shape
    eye = jnp.eye(g, dtype=t.dtype)
    return (t[:, :, :, None, :] * eye[None, :, None, :, None]).reshape(J, g * a, g * b)


def _block_diag_take(m, g):
    J, ga, gb = m.shape
    a, b = ga // g, gb // g
    m5 = m.reshape(J, g, a, g, b)
    idx = jnp.arange(g)
    return m5[:, idx, :, idx, :].transpose(1, 0, 2, 3)


def _local_step(x, tgt, ln_w, w_perm, lam_re, lam_im, log_step, b_re, b_im, c_re, c_im, s5_d, w_glu, w_su,
                conv_w, a_log, dt_bias, norm_w, w_du, w_out, fw):
    G, P, gb = S5_GROUPS, S5_STATE, S5_GROUPS // S5_BLOCKS
    h, rstd = _ln_fwd(x, ln_w)
    proj = _mm(h, w_perm, name="in_proj", tn=1152)

    b_re2, b_im2 = b_re.reshape(G, P * S5_GROUP), b_im.reshape(G, P * S5_GROUP)
    ls2 = log_step.reshape(G, 1)
    abar_re, abar_im, bb_re, bb_im = _s5_param_fwd(lam_re, lam_im, ls2, b_re2, b_im2)

    def to_wb(bb):
        return _block_diag(bb.reshape(S5_BLOCKS, gb, P, S5_GROUP).transpose(0, 1, 3, 2)).astype(BF16)

    def to_cb(cc):
        return _block_diag(cc.reshape(S5_BLOCKS, gb, S5_GROUP, P).transpose(0, 1, 3, 2)).astype(BF16)

    wbr, wbi, cbr, cbi = to_wb(bb_re), to_wb(bb_im), to_cb(c_re), to_cb(c_im)
    a_re_row, a_im_row = abar_re.reshape(1, G * P), abar_im.reshape(1, G * P)
    yc = _s5_core_fwd(proj, wbr, wbi, a_re_row, a_im_row, cbr, cbi)
    s5o = _s5_post_fwd(yc, proj, s5_d, w_glu)

    pad = lambda v: jnp.pad(v, ((0, 0), (DN_HEADS, 128 - 2 * DN_HEADS)))
    alog_row, dtb_row = pad(a_log), pad(dt_bias)
    qkv = _dn_conv_fwd(proj, conv_w)
    gates = _dn_gates_fwd(proj, alog_row, dtb_row)
    prep = _dn_prep_fwd(qkv, gates)
    o_dn, states = _dn_scan_fwd(*prep)
    dno = _dn_post_fwd(o_dn, proj, norm_w)

    ys, yd, mixed = _mix_fwd(s5o, dno, w_su, w_du, proj)
    dx2, dx2b, loss_row, d_fw = _final(mixed, w_out, x, tgt, fw)
    d_w_out = _mm(mixed, dx2b, ta=True, name="d_w_out")
    dgs, dgd, dys, dyd = _mix_bwd(dx2b, w_out, proj, ys, yd)
    d_w_su = _mm(s5o, dys, ta=True, name="d_w_su", shard_out=True)
    d_w_du = _mm(dno, dyd, ta=True, name="d_w_du", shard_out=True)
    ds5o = _mm(dys, w_su, tb=True, name="d_s5o")
    ddno = _mm(dyd, w_du, tb=True, name="d_dno")

    dyc, du1, dz_s, d_s5d, d_w_glu = _s5_post_bwd(yc, proj, s5_d, w_glu, ds5o)
    du, dwbr, dwbi, dcbr, dcbi, dar, dai = _s5_core_bwd(proj, wbr, wbi, a_re_row, a_im_row, cbr, cbi, dyc, du1)

    def from_wb(dwb):
        return _block_diag_take(dwb, gb).transpose(0, 1, 3, 2).reshape(G, P * S5_GROUP)

    def from_cb(dcb):
        return _block_diag_take(dcb, gb).transpose(0, 1, 3, 2).reshape(G, S5_GROUP, P)

    d_lam_re, d_lam_im, d_ls, d_b_re, d_b_im = _s5_param_bwd(
        lam_re, lam_im, ls2, b_re2, b_im2, dar.reshape(G, P), dai.reshape(G, P), from_wb(dwbr), from_wb(dwbi))

    do_dn, dz_d, d_norm_w = _dn_post_bwd(o_dn, proj, norm_w, ddno)
    dq, dk, dv, dgates = _dn_prep_bwd(qkv, gates, *_dn_scan_bwd(*prep, states, do_dn))
    dqkv, d_conv = _dn_conv_bwd(proj, conv_w, jnp.concatenate([dq, dk, dv], axis=1))
    dpb, d_alog_row, d_dtb_row = _dn_gates_bwd(proj, alog_row, dtb_row, dgates)

    dproj = jnp.concatenate([du, dz_s, dqkv, dz_d, dgs, dgd, dpb], axis=1)
    d_w_perm = _mm(h, dproj, ta=True, name="d_w_in", tn=1152)
    dh = _mm(dproj, w_perm, tb=True, name="d_h", tk=1152)
    grad_x, d_ln_w = _ln_bwd(x, rstd, ln_w, dh, dx2)

    grads = dict(
        ln_w=d_ln_w, w_perm=d_w_perm, s5_lam_re=d_lam_re, s5_lam_im=d_lam_im, s5_log_step=d_ls.reshape(1, G),
        s5_b_re=d_b_re.reshape(G, P, S5_GROUP), s5_b_im=d_b_im.reshape(G, P, S5_GROUP),
        s5_c_re=from_cb(dcbr), s5_c_im=from_cb(dcbi), s5_d=d_s5d, s5_w_glu=d_w_glu, s5_w_up=d_w_su,
        dn_conv_w=d_conv, dn_a_log=d_alog_row[:, DN_HEADS:2 * DN_HEADS], dn_dt_bias=d_dtb_row[:, DN_HEADS:2 * DN_HEADS],
        dn_norm_w=d_norm_w, dn_w_up=d_w_du, w_out=d_w_out, final_norm_w=d_fw)
    return loss_row, grad_x, grads


def _place():
    x, y, c = lax.axis_index("x"), lax.axis_index("y"), lax.axis_index("c")
    return x, y, c


def _remote(src, dst, send_sem, recv_sem, to):
    return pltpu.make_async_remote_copy(src_ref=src, dst_ref=dst, send_sem=send_sem, recv_sem=recv_sem,
                                        device_id=to, device_id_type=MESH)


def _gather_weights(shards, conv_w):
    na = len(shards)

    def body(*refs):
        ins, conv_in = refs[:na], refs[na]
        outs, conv_out = refs[na + 1:2 * na + 1], refs[2 * na + 1]
        send_sems, recv_sems, local_sems = refs[2 * na + 2:]
        x, y, c = _place()
        me = 2 * x + y
        sibling = (x, y, 1 - c)
        chips = [(1 - x, y), (x, 1 - y), (1 - x, 1 - y)]

        def part(ref, chip, half, a):
            r2 = shards[a].shape[0] // 2
            return ref.at[chip, pl.ds(half * r2, r2)]

        own = [pltpu.make_async_copy(ins[a], outs[a].at[me], local_sems.at[a]) for a in range(na)]
        own.append(pltpu.make_async_copy(conv_in, conv_out.at[me], local_sems.at[na]))
        for cp in own:
            cp.start()
        sends = []
        for a in range(na):
            r2 = shards[a].shape[0] // 2
            for j, (px, py) in enumerate(chips):
                k = 6 * a + j
                sends.append(_remote(ins[a].at[pl.ds(c * r2, r2)], part(outs[a], me, c, a), send_sems.at[k], recv_sems.at[k],
                                     (px, py, c)))
        for j, (px, py) in enumerate(chips):
            k = 6 * na + j
            sends.append(_remote(conv_in, conv_out.at[me], send_sems.at[k], recv_sems.at[k], (px, py, c)))
        for cp in sends:
            cp.start()
        for a in range(na):
            for j, (px, py) in enumerate(chips):
                src = 2 * px + py
                k = 6 * a + j
                landed = part(outs[a], src, c, a)
                _remote(landed, landed, send_sems.at[k], recv_sems.at[k], (px, py, c)).wait_recv()
                fwd = _remote(landed, landed, send_sems.at[k + 3], recv_sems.at[k + 3], sibling)
                fwd.start()
                sends.append(fwd)
        for a in range(na):
            for j, (px, py) in enumerate(chips):
                other = part(outs[a], 2 * px + py, 1 - c, a)
                _remote(other, other, send_sems.at[6 * a + 3 + j], recv_sems.at[6 * a + 3 + j], sibling).wait_recv()
        for j, (px, py) in enumerate(chips):
            k = 6 * na + j
            _remote(conv_in, conv_out.at[2 * px + py], send_sems.at[k], recv_sems.at[k], (px, py, c)).wait_recv()
        for cp in sends:
            cp.wait_send()
        for cp in own:
            cp.wait()

    nsem = 6 * na + 3
    return pl.pallas_call(
        body, name="gather_weights", in_specs=[ANY] * (na + 1), out_specs=[ANY] * (na + 1),
        out_shape=[_sds((N_CHIPS,) + s.shape, s.dtype) for s in shards] + [_sds((N_CHIPS,) + conv_w.shape, conv_w.dtype)],
        scratch_shapes=[pltpu.SemaphoreType.DMA((nsem,)), pltpu.SemaphoreType.DMA((nsem,)), pltpu.SemaphoreType.DMA((na + 1,))],
    )(*shards, conv_w)


def _swap_halves(gxs):
    na = len(gxs)

    def body(*refs):
        ins, outs = refs[:na], refs[na:2 * na]
        send_sems, recv_sems = refs[2 * na:]
        x, y, c = _place()
        cps = [_remote(ins[a].at[pl.ds(0, N_CHIPS), pl.ds(1 - c, 1)], outs[a], send_sems.at[a], recv_sems.at[a], (x, y, 1 - c))
               for a in range(na)]
        for cp in cps:
            cp.start()
        for cp in cps:
            cp.wait()

    return pl.pallas_call(
        body, name="rs_swap_halves", in_specs=[ANY] * na, out_specs=[ANY] * na,
        out_shape=[_sds((N_CHIPS, 1) + g.shape[2:], g.dtype) for g in gxs],
        scratch_shapes=[pltpu.SemaphoreType.DMA((na,)), pltpu.SemaphoreType.DMA((na,))],
    )(*gxs)


def _to_owners(csbs):
    na = len(csbs)

    def body(*refs):
        ins, outs = refs[:na], refs[na:2 * na]
        send_sems, recv_sems = refs[2 * na:]
        x, y, c = _place()
        me = 2 * x + y
        cps = []
        for a in range(na):
            for k in range(N_CHIPS - 1):
                j = (me + 1 + k) % N_CHIPS
                cps.append(_remote(ins[a].at[k], outs[a].at[2 - k], send_sems.at[3 * a + k], recv_sems.at[3 * a + 2 - k],
                                   (j // 2, j % 2, c)))
        for cp in cps:
            cp.start()
        for a in range(na):
            for k in range(N_CHIPS - 1):
                _remote(ins[a].at[k], outs[a].at[k], send_sems.at[3 * a + k], recv_sems.at[3 * a + k], (x, y, c)).wait_recv()
        for cp in cps:
            cp.wait_send()

    return pl.pallas_call(
        body, name="rs_to_owners", in_specs=[ANY] * na, out_specs=[ANY] * na,
        out_shape=[_sds(g.shape, g.dtype) for g in csbs],
        scratch_shapes=[pltpu.SemaphoreType.DMA((3 * na,)), pltpu.SemaphoreType.DMA((3 * na,))],
    )(*csbs)


def _share_halves(gfs):
    na = len(gfs)

    def body(*refs):
        ins, outs = refs[:na], refs[na:2 * na]
        send_sems, recv_sems = refs[2 * na:]
        x, y, c = _place()
        cps = [_remote(ins[a].at[pl.ds(c, 1)], outs[a].at[pl.ds(c, 1)], send_sems.at[a], recv_sems.at[a], (x, y, 1 - c))
               for a in range(na)]
        for cp in cps:
            cp.start()
        for a in range(na):
            cps[a].wait_send()
            _remote(ins[a].at[pl.ds(1 - c, 1)], outs[a].at[pl.ds(1 - c, 1)], send_sems.at[a], recv_sems.at[a], (x, y, 1 - c)).wait_recv()

    return pl.pallas_call(
        body, name="rs_share_halves", in_specs=[ANY] * na, out_specs=[ANY] * na,
        out_shape=[_sds(g.shape, g.dtype) for g in gfs], input_output_aliases={a: a for a in range(na)},
        scratch_shapes=[pltpu.SemaphoreType.DMA((na,)), pltpu.SemaphoreType.DMA((na,))],
    )(*gfs)


def _row_tile(rows, cols, budget=1 << 20):
    t = rows
    while t % 2 == 0 and t > 16 and t * cols * 4 > budget:
        t //= 2
    return t


def _chip_sums(gx, r1, where):
    _, _, r2, cd = gx.shape
    tr = _row_tile(r2, cd)

    def body(w_ref, a_ref, b_ref, o_ref):
        o_ref[...] = (a_ref[0] + b_ref[0]).astype(BF16)

    other = lambda k, i, w: ((w[1] + 1 + k) % N_CHIPS, w[0], i, 0)
    other0 = lambda k, i, w: ((w[1] + 1 + k) % N_CHIPS, 0, i, 0)
    return pl.pallas_call(
        body, name="rs_chip_sums",
        grid_spec=pltpu.PrefetchScalarGridSpec(
            num_scalar_prefetch=1, grid=(N_CHIPS - 1, r2 // tr),
            in_specs=[pl.BlockSpec((1, 1, tr, cd), other), pl.BlockSpec((1, 1, tr, cd), other0)],
            out_specs=pl.BlockSpec((1, tr, cd), lambda k, i, w: (k, i, 0))),
        out_shape=_sds((N_CHIPS - 1, r2, cd), BF16), compiler_params=_cp(("parallel", "parallel")),
    )(where, gx, r1)


def _owner_sum(gx, r1, r2x, where):
    _, _, r2, cd = gx.shape
    tr = _row_tile(r2, cd)

    def body(w_ref, a_ref, b_ref, r_ref, o_ref):
        acc = a_ref[0, 0] + b_ref[0, 0]
        for k in range(N_CHIPS - 1):
            acc = acc + r_ref[k].astype(F32)
        o_ref[0] = acc

    return pl.pallas_call(
        body, name="rs_owner_sum",
        grid_spec=pltpu.PrefetchScalarGridSpec(
            num_scalar_prefetch=1, grid=(r2 // tr,),
            in_specs=[pl.BlockSpec((1, 1, tr, cd), lambda i, w: (w[1], w[0], i, 0)),
                      pl.BlockSpec((1, 1, tr, cd), lambda i, w: (w[1], 0, i, 0)),
                      pl.BlockSpec((N_CHIPS - 1, tr, cd), lambda i, w: (0, i, 0))],
            out_specs=pl.BlockSpec((1, tr, cd), lambda i, w: (w[0], i, 0))),
        out_shape=_sds((2, r2, cd)), compiler_params=_cp(("parallel",)),
    )(where, gx, r1, r2x)


def _adamw_math(w, g, m, v):
    m = ADAM_B1 * m + (1.0 - ADAM_B1) * g
    v = ADAM_B2 * v + (1.0 - ADAM_B2) * (g * g)
    m_hat = m / (1.0 - ADAM_B1 ** ADAM_STEP)
    v_hat = v / (1.0 - ADAM_B2 ** ADAM_STEP)
    delta = -ADAM_LR * (m_hat / (jnp.sqrt(v_hat) + ADAM_EPS) + ADAM_WD * w)
    return delta, m, v


def _adamw(w, g, m, v, name):
    rows, cd = w.shape
    tr = _row_tile(rows, cd, budget=3 << 19) if rows % 16 == 0 else rows

    def body(w_ref, g_ref, m_ref, v_ref, d_ref, mo_ref, vo_ref):
        d, mm, vv = _adamw_math(w_ref[...], g_ref[...], m_ref[...], v_ref[...])
        d_ref[...] = d
        mo_ref[...] = mm
        vo_ref[...] = vv

    blk = pl.BlockSpec((tr, cd), lambda i: (i, 0))
    return pl.pallas_call(
        body, name=name, grid=(rows // tr,), in_specs=[blk] * 4, out_specs=[blk] * 3, out_shape=[_sds(w.shape)] * 3,
        compiler_params=_cp(("parallel",)),
    )(w, g, m, v)


def _small_allreduce_adamw(gp, wp, mp, vp):
    R = gp.shape[0]

    def body(g_ref, w_ref, m_ref, v_ref, go_ref, d_ref, mo_ref, vo_ref, land, send_sems, recv_sems):
        x, y, c = _place()
        me = 4 * x + 2 * y + c
        land[me] = g_ref[...]
        peers = []
        for k in range(1, N_DEV):
            px = 1 - x if k & 4 else x
            py = 1 - y if k & 2 else y
            pc = 1 - c if k & 1 else c
            peers.append((px, py, pc))
        cps = [_remote(g_ref, land.at[me], send_sems.at[k], recv_sems.at[k], peers[k]) for k in range(N_DEV - 1)]
        for cp in cps:
            cp.start()
        for k, (px, py, pc) in enumerate(peers):
            _remote(g_ref, land.at[4 * px + 2 * py + pc], send_sems.at[k], recv_sems.at[k], (px, py, pc)).wait_recv()
        for cp in cps:
            cp.wait_send()
        acc = land[0]
        for i in range(1, N_DEV):
            acc = acc + land[i]
        go_ref[...] = acc
        d, mm, vv = _adamw_math(w_ref[...], acc, m_ref[...], v_ref[...])
        d_ref[...] = d
        mo_ref[...] = mm
        vo_ref[...] = vv

    vm = pl.BlockSpec(memory_space=pltpu.VMEM)
    return pl.pallas_call(
        body, name="small_allreduce_adamw", in_specs=[vm] * 4, out_specs=[vm] * 4, out_shape=[_sds((R, 128))] * 4,
        scratch_shapes=[pltpu.VMEM((N_DEV, R, 128), F32), pltpu.SemaphoreType.DMA((N_DEV - 1,)),
                        pltpu.SemaphoreType.DMA((N_DEV - 1,))],
        compiler_params=_cp(),
    )(gp, wp, mp, vp)


def _pack(arrs):
    rows = []
    for a in arrs:
        f = a.reshape(-1)
        f = jnp.pad(f, (0, (-f.shape[0]) % 128))
        rows.append(f.reshape(-1, 128))
    p = jnp.concatenate(rows, axis=0)
    return jnp.pad(p, ((0, (-p.shape[0]) % 8), (0, 0)))


def _unpack(p, shapes):
    out, r = [], 0
    for s in shapes:
        n = math.prod(s)
        nr = -(-n // 128)
        out.append(p[r:r + nr].reshape(-1)[:n].reshape(s))
        r += nr
    return out


_SMALL = ("ln_w", "s5_lam_re", "s5_lam_im", "s5_log_step", "s5_b_re", "s5_b_im", "s5_c_re", "s5_c_im", "s5_d",
          "dn_a_log", "dn_dt_bias", "dn_norm_w", "final_norm_w")
_BIG = ("w_in", "s5_w_glu", "s5_w_up", "dn_w_up", "w_out")
_ORDER = ("ln_w", "w_in", "s5_lam_re", "s5_lam_im", "s5_log_step", "s5_b_re", "s5_b_im", "s5_c_re", "s5_c_im", "s5_d",
          "s5_w_glu", "s5_w_up", "dn_conv_w", "dn_a_log", "dn_dt_bias", "dn_norm_w", "dn_w_up", "w_out", "final_norm_w")


def kernel(x, ln_w, w_in, s5_lam_re, s5_lam_im, s5_log_step, s5_b_re, s5_b_im, s5_c_re, s5_c_im, s5_d, s5_w_glu, s5_w_up, dn_conv_w, dn_a_log, dn_dt_bias, dn_norm_w, dn_w_up, w_out, final_norm_w, loss_target, m_ln_w, m_w_in, m_s5_lam_re, m_s5_lam_im, m_s5_log_step, m_s5_b_re, m_s5_b_im, m_s5_c_re, m_s5_c_im, m_s5_d, m_s5_w_glu, m_s5_w_up, m_dn_conv_w, m_dn_a_log, m_dn_dt_bias, m_dn_norm_w, m_dn_w_up, m_w_out, m_final_norm_w, v_ln_w, v_w_in, v_s5_lam_re, v_s5_lam_im, v_s5_log_step, v_s5_b_re, v_s5_b_im, v_s5_c_re, v_s5_c_im, v_s5_d, v_s5_w_glu, v_s5_w_up, v_dn_conv_w, v_dn_a_log, v_dn_dt_bias, v_dn_norm_w, v_dn_w_up, v_w_out, v_final_norm_w):
    w = dict(ln_w=ln_w, w_in=w_in, s5_lam_re=s5_lam_re, s5_lam_im=s5_lam_im, s5_log_step=s5_log_step, s5_b_re=s5_b_re,
             s5_b_im=s5_b_im, s5_c_re=s5_c_re, s5_c_im=s5_c_im, s5_d=s5_d, s5_w_glu=s5_w_glu, s5_w_up=s5_w_up,
             dn_conv_w=dn_conv_w, dn_a_log=dn_a_log, dn_dt_bias=dn_dt_bias, dn_norm_w=dn_norm_w, dn_w_up=dn_w_up, w_out=w_out,
             final_norm_w=final_norm_w)
    m = dict(ln_w=m_ln_w, w_in=m_w_in, s5_lam_re=m_s5_lam_re, s5_lam_im=m_s5_lam_im, s5_log_step=m_s5_log_step,
             s5_b_re=m_s5_b_re, s5_b_im=m_s5_b_im, s5_c_re=m_s5_c_re, s5_c_im=m_s5_c_im, s5_d=m_s5_d, s5_w_glu=m_s5_w_glu,
             s5_w_up=m_s5_w_up, dn_conv_w=m_dn_conv_w, dn_a_log=m_dn_a_log, dn_dt_bias=m_dn_dt_bias, dn_norm_w=m_dn_norm_w,
             dn_w_up=m_dn_w_up, w_out=m_w_out, final_norm_w=m_final_norm_w)
    v = dict(ln_w=v_ln_w, w_in=v_w_in, s5_lam_re=v_s5_lam_re, s5_lam_im=v_s5_lam_im, s5_log_step=v_s5_log_step,
             s5_b_re=v_s5_b_re, s5_b_im=v_s5_b_im, s5_c_re=v_s5_c_re, s5_c_im=v_s5_c_im, s5_d=v_s5_d, s5_w_glu=v_s5_w_glu,
             s5_w_up=v_s5_w_up, dn_conv_w=v_dn_conv_w, dn_a_log=v_dn_a_log, dn_dt_bias=v_dn_dt_bias, dn_norm_w=v_dn_norm_w,
             dn_w_up=v_dn_w_up, w_out=v_w_out, final_norm_w=v_final_norm_w)
    xi, yi, ci = _place()
    chip = 2 * xi + yi
    where = jnp.stack([ci, chip]).astype(jnp.int32)

    g_in, g_glu, g_su, g_du, g_out, g_conv = _gather_weights(
        [w[n][0].astype(BF16) for n in _BIG], dn_conv_w[0])
    cat = lambda g: jnp.concatenate([g[j] for j in range(N_CHIPS)], axis=1)
    w_full = cat(g_in)
    w_perm = jnp.concatenate([w_full[:, :OFF_GS], w_full[:, OFF_GS + 2 * DN_HEADS:], w_full[:, OFF_GS:OFF_GS + 2 * DN_HEADS],
                              jnp.zeros((D_MODEL, D_IN_PAD - D_IN), BF16)], axis=1)

    loss_row, grad_x, g = _local_step(
        x[0], loss_target[0], ln_w, w_perm, s5_lam_re[0], s5_lam_im[0], s5_log_step, s5_b_re[0], s5_b_im[0], s5_c_re[0],
        s5_c_im[0], s5_d, g_glu.reshape(D_S5, D_S5), cat(g_su), cat(g_conv), dn_a_log, dn_dt_bias, dn_norm_w, cat(g_du),
        g_out.reshape(D_MODEL, D_MODEL), final_norm_w[None])
    loss = lax.psum(loss_row[0, 0], ("x", "y", "c"))

    dwp = g["w_perm"]
    d_w_in = jnp.concatenate([dwp[:, :OFF_GS], dwp[:, OFF_B:OFF_B + 2 * DN_HEADS], dwp[:, OFF_GS:OFF_B]], axis=1)
    cw = D_IN // N_CHIPS
    gxs = [d_w_in.reshape(D_MODEL, N_CHIPS, cw).transpose(1, 0, 2).reshape(N_CHIPS, 2, D_MODEL // 2, cw),
           g["s5_w_glu"].reshape(N_CHIPS, 2, D_S5 // 8, D_S5),
           g["s5_w_up"].reshape(N_CHIPS, 2, D_S5 // 2, D_MODEL // N_CHIPS),
           g["dn_w_up"].reshape(N_CHIPS, 2, D_DN // 2, D_MODEL // N_CHIPS),
           g["w_out"].reshape(N_CHIPS, 2, D_MODEL // 8, D_MODEL)]
    r1s = _swap_halves(gxs)
    csbs = [_chip_sums(gx, r1, where) for gx, r1 in zip(gxs, r1s)]
    r2s = _to_owners(csbs)
    gfs = [_owner_sum(gx, r1, r2x, where) for gx, r1, r2x in zip(gxs, r1s, r2s)]
    gfs = _share_halves(gfs)
    grads, deltas, new_m, new_v = {}, {}, {}, {}
    for n, gf in zip(_BIG, gfs):
        shp = w[n].shape
        g2 = gf.reshape(shp[1:])
        d_, m_, v_ = _adamw(w[n][0], g2, m[n][0], v[n][0], "adamw_" + n)
        grads[n], deltas[n], new_m[n], new_v[n] = g2.reshape(shp), d_.reshape(shp), m_.reshape(shp), v_.reshape(shp)

    gp = _pack([g[n] for n in _SMALL] + [g["dn_conv_w"]])
    zc = jnp.zeros((CONV_K, 3 * D_DN), F32)
    go, dl, mo, vo = _small_allreduce_adamw(gp, _pack([w[n] for n in _SMALL] + [zc]), _pack([m[n] for n in _SMALL] + [zc]),
                                            _pack([v[n] for n in _SMALL] + [zc]))
    shapes = [w[n].shape for n in _SMALL] + [(CONV_K, 3 * D_DN)]
    for dst, src in ((grads, go), (deltas, dl), (new_m, mo), (new_v, vo)):
        for n, a in zip(_SMALL, _unpack(src, shapes)):
            dst[n] = a
    cc = 3 * D_DN // N_CHIPS
    g_conv_mine = lax.dynamic_slice(_unpack(go, shapes)[-1], (0, chip * cc), (CONV_K, cc))
    d_, m_, v_ = _adamw(dn_conv_w[0], g_conv_mine, m_dn_conv_w[0], v_dn_conv_w[0], "adamw_dn_conv_w")
    grads["dn_conv_w"], deltas["dn_conv_w"], new_m["dn_conv_w"], new_v["dn_conv_w"] = (
        g_conv_mine[None], d_[None], m_[None], v_[None])

    return (loss, grad_x[None], *[grads[n] for n in _ORDER], *[deltas[n] for n in _ORDER], *[new_m[n] for n in _ORDER],
            *[new_v[n] for n in _ORDER])
```

```python
import functools
import math

import jax
import jax.numpy as jnp
from jax import lax
from jax.experimental import pallas as pl
from jax.experimental.pallas import tpu as pltpu

F32 = jnp.float32
BF16 = jnp.bfloat16
HI = lax.Precision.HIGHEST
MESH = pl.DeviceIdType.MESH
ANY = pl.BlockSpec(memory_space=pl.ANY)

EPS = 1e-6
D_MODEL = 2048
D_S5 = 1024
S5_GROUP = 16
S5_GROUPS = 64
S5_STATE = 64
S5_BLOCKS = 8
S5_SEG = 8
DN_HEADS = 8
DN_HEAD_DIM = 128
D_DN = 1024
CONV_K = 4
CHUNK = 64
D_IN = 10256
D_IN_PAD = 10368
OFF_US, OFF_ZS, OFF_Q, OFF_K, OFF_V, OFF_ZD, OFF_GS, OFF_GD, OFF_B = 0, 1024, 2048, 3072, 4096, 5120, 6144, 8192, 10240
N_CHIPS = 4
N_DEV = 8
VMEM_LIMIT = 56 * 1024 * 1024

ADAM_LR = 0.001
ADAM_B1 = 0.9
ADAM_B2 = 0.999
ADAM_EPS = 1e-08
ADAM_WD = 0.01
ADAM_STEP = 10


def _cp(sem=None):
    return pltpu.CompilerParams(dimension_semantics=sem, vmem_limit_bytes=VMEM_LIMIT)


def _sds(shape, dtype=F32):
    return jax.ShapeDtypeStruct(tuple(shape), dtype)


def _sigmoid(x):
    return 1.0 / (1.0 + jnp.exp(-x))


def _silu(x):
    return x * _sigmoid(x)


def _dsilu(x):
    s = _sigmoid(x)
    return s * (1.0 + x * (1.0 - s))


def _mm(a, b, *, name, ta=False, tb=False, out_dtype=F32, tm=512, tn=512, tk=2048, shard_out=False):
    if ta:
        K, M = a.shape
    else:
        M, K = a.shape
    if tb:
        N, K2 = b.shape
    else:
        K2, N = b.shape
    assert K == K2, (a.shape, b.shape)
    tm, tn, tk = min(tm, M), min(tn, N), min(tk, K)
    assert M % tm == 0 and N % tn == 0 and K % tk == 0, (M, N, K, tm, tn, tk)
    nk = K // tk
    dims = (((0 if ta else 1,), (1 if tb else 0,)), ((), ()))

    def body(a_ref, b_ref, o_ref, *acc):
        k = pl.program_id(2)
        p = lax.dot_general(a_ref[...].astype(BF16), b_ref[...].astype(BF16), dims, preferred_element_type=F32)
        if nk == 1:
            o_ref[...] = p.astype(out_dtype).reshape(o_ref.shape)
            return
        acc_ref, = acc

        @pl.when(k == 0)
        def _():
            acc_ref[...] = p

        @pl.when(k > 0)
        def _():
            acc_ref[...] += p

        @pl.when(k == nk - 1)
        def _():
            o_ref[...] = acc_ref[...].astype(out_dtype).reshape(o_ref.shape)

    a_spec = pl.BlockSpec((tk, tm), lambda i, j, k: (k, i)) if ta else pl.BlockSpec((tm, tk), lambda i, j, k: (i, k))
    b_spec = pl.BlockSpec((tn, tk), lambda i, j, k: (j, k)) if tb else pl.BlockSpec((tk, tn), lambda i, j, k: (k, j))
    if shard_out:
        o_spec = pl.BlockSpec((1, tm, tn), lambda i, j, k: (j, i, 0))
        o_shape = _sds((N // tn, M, tn), out_dtype)
    else:
        o_spec = pl.BlockSpec((tm, tn), lambda i, j, k: (i, j))
        o_shape = _sds((M, N), out_dtype)
    return pl.pallas_call(
        body, name=name, grid=(M // tm, N // tn, nk), in_specs=[a_spec, b_spec], out_specs=o_spec, out_shape=o_shape,
        scratch_shapes=[pltpu.VMEM((tm, tn), F32)] if nk > 1 else [],
        compiler_params=_cp(("parallel", "parallel", "arbitrary")),
    )(a, b)


def _ln_fwd(x, w):
    L, D = x.shape
    tm = min(256, L)

    def body(x_ref, w_ref, h_ref, r_ref):
        xv = x_ref[...]
        r = lax.rsqrt(jnp.mean(xv * xv, axis=-1, keepdims=True) + EPS)
        h_ref[...] = (xv * r * w_ref[...]).astype(BF16)
        r_ref[...] = r

    return pl.pallas_call(
        body, name="ln_fwd", grid=(L // tm,),
        in_specs=[pl.BlockSpec((tm, D), lambda i: (i, 0)), pl.BlockSpec((1, D), lambda i: (0, 0))],
        out_specs=[pl.BlockSpec((tm, D), lambda i: (i, 0)), pl.BlockSpec((tm, 1), lambda i: (i, 0))],
        out_shape=[_sds((L, D), BF16), _sds((L, 1))], compiler_params=_cp(("parallel",)),
    )(x, w)


def _ln_bwd(x, r, w, dh, dx2):
    L, D = x.shape
    tm = min(256, L)

    def body(x_ref, r_ref, w_ref, dh_ref, dx2_ref, dx_ref, dw_ref):
        i = pl.program_id(0)
        xv, rv, dhv = x_ref[...], r_ref[...], dh_ref[...]
        t = dhv * w_ref[...]
        m = jnp.mean(t * xv, axis=-1, keepdims=True)
        dx_ref[...] = dx2_ref[...] + rv * t - xv * (rv * rv * rv) * m
        part = jnp.sum(dhv * xv * rv, axis=0, keepdims=True)

        @pl.when(i == 0)
        def _():
            dw_ref[...] = part

        @pl.when(i > 0)
        def _():
            dw_ref[...] += part

    row = pl.BlockSpec((tm, D), lambda i: (i, 0))
    return pl.pallas_call(
        body, name="ln_bwd", grid=(L // tm,),
        in_specs=[row, pl.BlockSpec((tm, 1), lambda i: (i, 0)), pl.BlockSpec((1, D), lambda i: (0, 0)), row, row],
        out_specs=[row, pl.BlockSpec((1, D), lambda i: (0, 0))],
        out_shape=[_sds((L, D)), _sds((1, D))], compiler_params=_cp(("arbitrary",)),
    )(x, r, w, dh, dx2)


def _s5_param_math(lam_re, lam_im, log_step, b_re, b_im, expand):
    step = jnp.exp(log_step)
    mag = jnp.exp(lam_re * step)
    abar_re = mag * jnp.cos(lam_im * step)
    abar_im = mag * jnp.sin(lam_im * step)
    den = lam_re * lam_re + lam_im * lam_im
    xr = abar_re - 1.0
    f_re = (xr * lam_re + abar_im * lam_im) / den
    f_im = (abar_im * lam_re - xr * lam_im) / den
    fe_re = jnp.dot(f_re, expand, precision=HI, preferred_element_type=F32)
    fe_im = jnp.dot(f_im, expand, precision=HI, preferred_element_type=F32)
    bb_re = fe_re * b_re - fe_im * b_im
    bb_im = fe_re * b_im + fe_im * b_re
    return abar_re, abar_im, bb_re, bb_im


def _s5_expand():
    p = lax.broadcasted_iota(jnp.int32, (S5_STATE, S5_STATE * S5_GROUP), 0)
    q = lax.broadcasted_iota(jnp.int32, (S5_STATE, S5_STATE * S5_GROUP), 1)
    return (q // S5_GROUP == p).astype(F32)


def _s5_param_fwd(lam_re, lam_im, log_step, b_re, b_im):
    G, P = lam_re.shape

    def body(lr, li, ls, br, bi, ar_o, ai_o, bbr_o, bbi_o):
        outs = _s5_param_math(lr[...], li[...], ls[...], br[...], bi[...], _s5_expand())
        for o, v in zip((ar_o, ai_o, bbr_o, bbi_o), outs):
            o[...] = v

    return pl.pallas_call(
        body, name="s5_param_fwd",
        out_shape=[_sds((G, P)), _sds((G, P)), _sds(b_re.shape), _sds(b_re.shape)], compiler_params=_cp(),
    )(lam_re, lam_im, log_step, b_re, b_im)


def _s5_param_bwd(lam_re, lam_im, log_step, b_re, b_im, dar, dai, dbbr, dbbi):
    G, P = lam_re.shape

    def body(lr, li, ls, br, bi, g0, g1, g2, g3, dlr, dli, dls, dbr, dbi):
        ex = _s5_expand()
        _, f = jax.vjp(lambda a, b, c, d, e: _s5_param_math(a, b, c, d, e, ex), lr[...], li[...], ls[...], br[...], bi[...])
        grads = f((g0[...], g1[...], g2[...], g3[...]))
        for o, v in zip((dlr, dli, dls, dbr, dbi), grads):
            o[...] = v

    return pl.pallas_call(
        body, name="s5_param_bwd",
        out_shape=[_sds((G, P)), _sds((G, P)), _sds((G, 1)), _sds(b_re.shape), _sds(b_re.shape)], compiler_params=_cp(),
    )(lam_re, lam_im, log_step, b_re, b_im, dar, dai, dbbr, dbbi)


def _to_segs(src_ref, dst_ref, L):
    S = L // S5_SEG

    def body(j, carry):
        dst_ref[pl.ds(pl.multiple_of(S5_SEG * j, S5_SEG), S5_SEG), :] = src_ref[pl.ds(j, S5_SEG, stride=S), :]
        return carry

    lax.fori_loop(0, S, body, 0, unroll=8)


def _from_segs(src_ref, L, write):
    S = L // S5_SEG
    for seg in range(S5_SEG):
        def body(jb, carry, seg=seg):
            j0 = 16 * jb
            write(pl.multiple_of(seg * S + j0, 16), src_ref[pl.ds(S5_SEG * j0 + seg, 16, stride=S5_SEG), :])
            return carry

        lax.fori_loop(0, S // 16, body, 0, unroll=4)


def _scan_segs(ar, ai, re_ref, im_ref, end_r_ref, end_i_ref, c_r_ref, c_i_ref, L, tile0, reverse):
    S = L // S5_SEG
    NB, LN = re_ref.shape[0], 128
    assert S & (S - 1) == 0
    tile = lambda j: pl.ds(pl.multiple_of(S5_SEG * (tile0 + j), S5_SEG), S5_SEG)
    ar8 = [jnp.broadcast_to(ar[:, b * LN:(b + 1) * LN], (S5_SEG, LN)) for b in range(NB)]
    ai8 = [jnp.broadcast_to(ai[:, b * LN:(b + 1) * LN], (S5_SEG, LN)) for b in range(NB)]

    def step(idx, carry):
        rows = tile(S - 1 - idx if reverse else idx)
        out = []
        for b in range(NB):
            sr, si = carry[b]
            nr = ar8[b] * sr - ai8[b] * si + re_ref[b, rows, :]
            ni = ar8[b] * si + ai8[b] * sr + im_ref[b, rows, :]
            re_ref[b, rows, :] = nr
            im_ref[b, rows, :] = ni
            out.append((nr, ni))
        return tuple(out)

    z8 = jnp.zeros((S5_SEG, LN), F32)
    fin = lax.fori_loop(0, S, step, tuple((z8, z8) for _ in range(NB)), unroll=4)
    order = range(S5_SEG - 2, -1, -1) if reverse else range(1, S5_SEG)
    for b in range(NB):
        end_r_ref[b], end_i_ref[b] = fin[b]
        pr, pi = ar8[b][:1], ai8[b][:1]
        for _ in range(int(math.log2(S))):
            pr, pi = pr * pr - pi * pi, 2.0 * pr * pi
        first = S5_SEG - 1 if reverse else 0
        c_r_ref[b, pl.ds(first, 1), :] = jnp.zeros((1, LN), F32)
        c_i_ref[b, pl.ds(first, 1), :] = jnp.zeros((1, LN), F32)
        cr, ci = end_r_ref[b, pl.ds(first, 1), :], end_i_ref[b, pl.ds(first, 1), :]
        for i in order:
            c_r_ref[b, pl.ds(i, 1), :] = cr
            c_i_ref[b, pl.ds(i, 1), :] = ci
            er, ei = end_r_ref[b, pl.ds(i, 1), :], end_i_ref[b, pl.ds(i, 1), :]
            cr, ci = er + pr * cr - pi * ci, ei + pr * ci + pi * cr

    def fix(idx, carry):
        rows = tile(S - 1 - idx if reverse else idx)
        out = []
        for b in range(NB):
            pr, pi = carry[b]
            cr, ci = c_r_ref[b], c_i_ref[b]
            re_ref[b, rows, :] += pr * cr - pi * ci
            im_ref[b, rows, :] += pr * ci + pi * cr
            out.append((pr * ar8[b] - pi * ai8[b], pr * ai8[b] + pi * ar8[b]))
        return tuple(out)

    lax.fori_loop(0, S, fix, tuple((ar8[b], ai8[b]) for b in range(NB)), unroll=4)


def _s5_seg_scratch(L, cs, pad):
    NB = cs // 128
    small = [pltpu.VMEM((NB, S5_SEG, 128), F32) for _ in range(4)]
    return [pltpu.VMEM((NB, L + pad, 128), F32), pltpu.VMEM((NB, L + pad, 128), F32)] + small


def _s5_core_fwd(proj, wbr, wbi, a_re, a_im, cbr, cbi):
    L = proj.shape[0]
    nb, ci, cs = wbr.shape
    NB = cs // 128

    def body(u_ref, wbr_ref, wbi_ref, ar_ref, ai_ref, cbr_ref, cbi_ref, y_ref, sr, si, er, ei, cr, cim, up, yp):
        _to_segs(u_ref, up, L)
        u = up[...].astype(BF16)
        for b in range(NB):
            lanes = pl.ds(b * 128, 128)
            sr[b] = jnp.dot(u, wbr_ref[0, :, lanes], preferred_element_type=F32)
            si[b] = jnp.dot(u, wbi_ref[0, :, lanes], preferred_element_type=F32)
        _scan_segs(ar_ref[...], ai_ref[...], sr, si, er, ei, cr, cim, L, 0, False)
        y = jnp.zeros((L, ci), F32)
        for b in range(NB):
            lanes = pl.ds(b * 128, 128)
            y = y + (jnp.dot(sr[b].astype(BF16), cbr_ref[0, lanes, :], preferred_element_type=F32)
                     - jnp.dot(si[b].astype(BF16), cbi_ref[0, lanes, :], preferred_element_type=F32))
        yp[...] = y

        def write(row, val):
            y_ref[pl.ds(row, 16), :] = val

        _from_segs(yp, L, write)

    wspec = pl.BlockSpec((1, ci, cs), lambda j: (j, 0, 0))
    aspec = pl.BlockSpec((1, cs), lambda j: (0, j))
    cspec = pl.BlockSpec((1, cs, ci), lambda j: (j, 0, 0))
    return pl.pallas_call(
        body, name="s5_core_fwd", grid=(nb,),
        in_specs=[pl.BlockSpec((L, ci), lambda j: (0, OFF_US // ci + j)), wspec, wspec, aspec, aspec, cspec, cspec],
        out_specs=pl.BlockSpec((L, ci), lambda j: (0, j)), out_shape=_sds((L, nb * ci)),
        scratch_shapes=_s5_seg_scratch(L, cs, 0) + [pltpu.VMEM((L, ci), F32), pltpu.VMEM((L, ci), F32)],
        compiler_params=_cp(("arbitrary",)),
    )(proj, wbr, wbi, a_re, a_im, cbr, cbi)


def _s5_core_bwd(proj, wbr, wbi, a_re, a_im, cbr, cbi, dyc, du1):
    L = proj.shape[0]
    nb, ci, cs = wbr.shape
    NB = cs // 128
    S = L // S5_SEG
    PAD = S5_SEG

    def body(u_ref, wbr_ref, wbi_ref, ar_ref, ai_ref, cbr_ref, cbi_ref, dy_ref, du1_ref,
             du_ref, dwbr_ref, dwbi_ref, dcbr_ref, dcbi_ref, dar_ref, dai_ref,
             sr, si, er, ei, cr, cim, lr, li, up, dyp, dup):
        tn = (((0,), (0,)), ((), ()))
        nt = (((1,), (1,)), ((), ()))
        _to_segs(u_ref, up, L)
        _to_segs(dy_ref, dyp, L)
        _to_segs(du1_ref, dup, L)
        u = up[...].astype(BF16)
        dy = dyp[...].astype(BF16)
        ar, ai = ar_ref[...], ai_ref[...]
        for b in range(NB):
            lanes = pl.ds(b * 128, 128)
            sr[b, pl.ds(PAD, L), :] = jnp.dot(u, wbr_ref[0, :, lanes], preferred_element_type=F32)
            si[b, pl.ds(PAD, L), :] = jnp.dot(u, wbi_ref[0, :, lanes], preferred_element_type=F32)
        _scan_segs(ar, ai, sr, si, er, ei, cr, cim, L, 1, False)
        for b in range(NB):
            lanes = pl.ds(b * 128, 128)
            sr[b, pl.ds(0, PAD), :] = cr[b]
            si[b, pl.ds(0, PAD), :] = cim[b]
            lr[b] = lax.dot_general(dy, cbr_ref[0, lanes, :], nt, preferred_element_type=F32)
            li[b] = -lax.dot_general(dy, cbi_ref[0, lanes, :], nt, preferred_element_type=F32)
            dcbr_ref[0, lanes, :] = lax.dot_general(sr[b, pl.ds(PAD, L), :].astype(BF16), dy, tn, preferred_element_type=F32)
            dcbi_ref[0, lanes, :] = -lax.dot_general(si[b, pl.ds(PAD, L), :].astype(BF16), dy, tn, preferred_element_type=F32)
        _scan_segs(ar, -ai, lr, li, er, ei, cr, cim, L, 0, True)

        def da_step(j, carry):
            rows = pl.ds(pl.multiple_of(S5_SEG * j, S5_SEG), S5_SEG)
            out = []
            for b in range(NB):
                dar, dai = carry[b]
                pr_, pi_ = sr[b, rows, :], si[b, rows, :]
                gr, gi = lr[b, rows, :], li[b, rows, :]
                out.append((dar + (gr * pr_ + gi * pi_), dai + (gi * pr_ - gr * pi_)))
            return tuple(out)

        z8 = jnp.zeros((S5_SEG, 128), F32)
        acc = lax.fori_loop(0, S, da_step, tuple((z8, z8) for _ in range(NB)), unroll=4)
        du = dup[...]
        for b in range(NB):
            lanes = pl.ds(b * 128, 128)
            dar_ref[:, lanes] = jnp.sum(acc[b][0], axis=0, keepdims=True)
            dai_ref[:, lanes] = jnp.sum(acc[b][1], axis=0, keepdims=True)
            gr, gi = lr[b].astype(BF16), li[b].astype(BF16)
            du = du + (lax.dot_general(gr, wbr_ref[0, :, lanes], nt, preferred_element_type=F32)
                       + lax.dot_general(gi, wbi_ref[0, :, lanes], nt, preferred_element_type=F32))
            dwbr_ref[0, :, lanes] = lax.dot_general(u, gr, tn, preferred_element_type=F32)
            dwbi_ref[0, :, lanes] = lax.dot_general(u, gi, tn, preferred_element_type=F32)
        dup[...] = du

        def write(row, val):
            du_ref[pl.ds(row, 16), :] = val.astype(BF16)

        _from_segs(dup, L, write)

    wspec = pl.BlockSpec((1, ci, cs), lambda j: (j, 0, 0))
    aspec = pl.BlockSpec((1, cs), lambda j: (0, j))
    cspec = pl.BlockSpec((1, cs, ci), lambda j: (j, 0, 0))
    col = pl.BlockSpec((L, ci), lambda j: (0, j))
    return pl.pallas_call(
        body, name="s5_core_bwd", grid=(nb,),
        in_specs=[pl.BlockSpec((L, ci), lambda j: (0, OFF_US // ci + j)), wspec, wspec, aspec, aspec, cspec, cspec, col, col],
        out_specs=[col, wspec, wspec, cspec, cspec, aspec, aspec],
        out_shape=[_sds((L, nb * ci), BF16), _sds(wbr.shape), _sds(wbr.shape), _sds(cbr.shape), _sds(cbr.shape),
                   _sds((1, nb * cs)), _sds((1, nb * cs))],
        scratch_shapes=(_s5_seg_scratch(L, cs, PAD) + [pltpu.VMEM((NB, L, 128), F32), pltpu.VMEM((NB, L, 128), F32)]
                        + [pltpu.VMEM((L, ci), F32) for _ in range(3)]),
        compiler_params=_cp(("arbitrary",)),
    )(proj, wbr, wbi, a_re, a_im, cbr, cbi, dyc, du1)


def _s5_post_math(yc, u, z, d, wg):
    y = yc + d * u
    y1 = jax.nn.gelu(y)
    t = jnp.dot(y1.astype(BF16), wg, preferred_element_type=F32)
    sg = _sigmoid(t)
    return y, y1, sg


def _s5_post_fwd(yc, proj, d, wg):
    L, W = yc.shape
    tm = min(256, L)

    def body(yc_ref, u_ref, z_ref, d_ref, wg_ref, o_ref):
        _, y1, sg = _s5_post_math(yc_ref[...], u_ref[...], z_ref[...], d_ref[...], wg_ref[...])
        o_ref[...] = (y1 * sg * _silu(z_ref[...])).astype(BF16)

    row = pl.BlockSpec((tm, W), lambda i: (i, 0))
    return pl.pallas_call(
        body, name="s5_post_fwd", grid=(L // tm,),
        in_specs=[row, pl.BlockSpec((tm, W), lambda i: (i, OFF_US // W)), pl.BlockSpec((tm, W), lambda i: (i, OFF_ZS // W)),
                  pl.BlockSpec((1, W), lambda i: (0, 0)), pl.BlockSpec((W, W), lambda i: (0, 0))],
        out_specs=row, out_shape=_sds((L, W), BF16), compiler_params=_cp(("parallel",)),
    )(yc, proj, proj, d, wg)


def _s5_post_bwd(yc, proj, d, wg, dout):
    L, W = yc.shape
    tm = min(256, L)

    def body(yc_ref, u_ref, z_ref, d_ref, wg_ref, do_ref, dyc_ref, du_ref, dz_ref, dd_ref, dwg_ref):
        i = pl.program_id(0)
        u, z, d_, wgv = u_ref[...], z_ref[...], d_ref[...], wg_ref[...]
        y, y1, sg = _s5_post_math(yc_ref[...], u, z, d_, wgv)
        dout_ = do_ref[...]
        y2 = y1 * sg
        dy2 = dout_ * _silu(z)
        dz_ref[...] = (dout_ * y2 * _dsilu(z)).astype(BF16)
        dt = (dy2 * y1 * sg * (1.0 - sg)).astype(BF16)
        dy1 = dy2 * sg + lax.dot_general(dt, wgv, (((1,), (1,)), ((), ())), preferred_element_type=F32)
        _, gelu_vjp = jax.vjp(jax.nn.gelu, y)
        dy = gelu_vjp(dy1)[0]
        dyc_ref[...] = dy
        du_ref[...] = dy * d_
        dd_part = jnp.sum(dy * u, axis=0, keepdims=True)
        dwg_part = lax.dot_general(y1.astype(BF16), dt, (((0,), (0,)), ((), ())), preferred_element_type=F32)

        @pl.when(i == 0)
        def _():
            dd_ref[...] = dd_part
            dwg_ref[...] = dwg_part

        @pl.when(i > 0)
        def _():
            dd_ref[...] += dd_part
            dwg_ref[...] += dwg_part

    row = pl.BlockSpec((tm, W), lambda i: (i, 0))
    return pl.pallas_call(
        body, name="s5_post_bwd", grid=(L // tm,),
        in_specs=[row, pl.BlockSpec((tm, W), lambda i: (i, OFF_US // W)), pl.BlockSpec((tm, W), lambda i: (i, OFF_ZS // W)),
                  pl.BlockSpec((1, W), lambda i: (0, 0)), pl.BlockSpec((W, W), lambda i: (0, 0)), row],
        out_specs=[row, row, row, pl.BlockSpec((1, W), lambda i: (0, 0)), pl.BlockSpec((W, W), lambda i: (0, 0))],
        out_shape=[_sds((L, W)), _sds((L, W)), _sds((L, W), BF16), _sds((1, W)), _sds((W, W))],
        compiler_params=_cp(("arbitrary",)),
    )(yc, proj, proj, d, wg, dout)


def _shift_down(x, s):
    if s == 0:
        return x
    rows = lax.broadcasted_iota(jnp.int32, x.shape, 0)
    return jnp.where(rows >= s, pltpu.roll(x, s, 0), 0.0)


def _shift_up(x, s):
    if s == 0:
        return x
    L = x.shape[0]
    rows = lax.broadcasted_iota(jnp.int32, x.shape, 0)
    return jnp.where(rows < L - s, pltpu.roll(x, L - s, 0), 0.0)


def _conv_pre(x, w):
    acc = w[CONV_K - 1:CONV_K, :] * x
    for s in range(1, CONV_K):
        acc = acc + w[CONV_K - 1 - s:CONV_K - s, :] * _shift_down(x, s)
    return acc


def _dn_conv_fwd(proj, conv_w):
    L = proj.shape[0]
    W = DN_HEAD_DIM
    nq = 2 * DN_HEADS

    def body(x_ref, w_ref, o_ref):
        j = pl.program_id(0)
        act = _silu(_conv_pre(x_ref[...], w_ref[...]))
        r = lax.rsqrt(jnp.sum(act * act, axis=-1, keepdims=True) + EPS)
        scale = jnp.where(j < DN_HEADS, DN_HEAD_DIM ** -0.5, 1.0)
        o_ref[...] = jnp.where(j < nq, act * r * scale, act)

    return pl.pallas_call(
        body, name="dn_conv_fwd", grid=(3 * DN_HEADS,),
        in_specs=[pl.BlockSpec((L, W), lambda j: (0, OFF_Q // W + j)), pl.BlockSpec((CONV_K, W), lambda j: (0, j))],
        out_specs=pl.BlockSpec((L, W), lambda j: (0, j)), out_shape=_sds((L, 3 * D_DN)), compiler_params=_cp(("parallel",)),
    )(proj, conv_w)


def _dn_conv_bwd(proj, conv_w, dout):
    L = proj.shape[0]
    W = DN_HEAD_DIM
    nq = 2 * DN_HEADS

    def body(x_ref, w_ref, do_ref, dx_ref, dw_ref):
        j = pl.program_id(0)
        x, w, dout_ = x_ref[...], w_ref[...], do_ref[...]
        pre = _conv_pre(x, w)
        act = _silu(pre)
        r = lax.rsqrt(jnp.sum(act * act, axis=-1, keepdims=True) + EPS)
        scale = jnp.where(j < DN_HEADS, DN_HEAD_DIM ** -0.5, 1.0)
        g = dout_ * scale
        dact_n = r * g - act * (r * r * r) * jnp.sum(g * act, axis=-1, keepdims=True)
        dact = jnp.where(j < nq, dact_n, dout_)
        dpre = dact * _dsilu(pre)
        dx = w[CONV_K - 1:CONV_K, :] * dpre
        for s in range(1, CONV_K):
            dx = dx + w[CONV_K - 1 - s:CONV_K - s, :] * _shift_up(dpre, s)
        dx_ref[...] = dx.astype(BF16)
        for s in range(CONV_K):
            dw_ref[pl.ds(CONV_K - 1 - s, 1), :] = jnp.sum(dpre * _shift_down(x, s), axis=0, keepdims=True)

    col = pl.BlockSpec((L, W), lambda j: (0, j))
    wsp = pl.BlockSpec((CONV_K, W), lambda j: (0, j))
    return pl.pallas_call(
        body, name="dn_conv_bwd", grid=(3 * DN_HEADS,),
        in_specs=[pl.BlockSpec((L, W), lambda j: (0, OFF_Q // W + j)), wsp, col], out_specs=[col, wsp],
        out_shape=[_sds((L, 3 * D_DN), BF16), _sds((CONV_K, 3 * D_DN))], compiler_params=_cp(("parallel",)),
    )(proj, conv_w, dout)


def _softplus(x):
    return jnp.maximum(x, 0.0) + jnp.log(1.0 + jnp.exp(-jnp.abs(x)))


def _dn_gates_fwd(proj, alog, dtb):
    L = proj.shape[0]
    W = 128

    def body(p_ref, al_ref, db_ref, o_ref):
        p = p_ref[...]
        lane = lax.broadcasted_iota(jnp.int32, p.shape, 1)
        g = -jnp.exp(al_ref[...]) * _softplus(p + db_ref[...])
        o_ref[...] = jnp.where(lane < DN_HEADS, _sigmoid(p), jnp.where(lane < 2 * DN_HEADS, g, 0.0))

    return pl.pallas_call(
        body, name="dn_gates_fwd", grid=(1,),
        in_specs=[pl.BlockSpec((L, W), lambda i: (0, OFF_B // W)), pl.BlockSpec((1, W), lambda i: (0, 0)),
                  pl.BlockSpec((1, W), lambda i: (0, 0))],
        out_specs=pl.BlockSpec((L, W), lambda i: (0, 0)), out_shape=_sds((L, W)), compiler_params=_cp(("arbitrary",)),
    )(proj, alog, dtb)


def _dn_gates_bwd(proj, alog, dtb, dgates):
    L = proj.shape[0]
    W = 128

    def body(p_ref, al_ref, db_ref, dg_ref, dp_ref, dal_ref, ddb_ref):
        p, dg = p_ref[...], dg_ref[...]
        lane = lax.broadcasted_iota(jnp.int32, p.shape, 1)
        is_g = jnp.logical_and(lane >= DN_HEADS, lane < 2 * DN_HEADS)
        beta = _sigmoid(p)
        na = -jnp.exp(al_ref[...])
        xs = p + db_ref[...]
        dsp = dg * na * _sigmoid(xs)
        dp_ref[...] = jnp.where(lane < DN_HEADS, dg * beta * (1.0 - beta), jnp.where(is_g, dsp, 0.0)).astype(BF16)
        dal_ref[...] = jnp.sum(jnp.where(is_g, dg * na * _softplus(xs), 0.0), axis=0, keepdims=True)
        ddb_ref[...] = jnp.sum(jnp.where(is_g, dsp, 0.0), axis=0, keepdims=True)

    one = pl.BlockSpec((1, W), lambda i: (0, 0))
    full = pl.BlockSpec((L, W), lambda i: (0, 0))
    return pl.pallas_call(
        body, name="dn_gates_bwd", grid=(1,),
        in_specs=[pl.BlockSpec((L, W), lambda i: (0, OFF_B // W)), one, one, full], out_specs=[full, one, one],
        out_shape=[_sds((L, W), BF16), _sds((1, W)), _sds((1, W))], compiler_params=_cp(("arbitrary",)),
    )(proj, alog, dtb, dgates)


def _bdot(a, b, dims):
    return lax.dot_general(a.astype(BF16), b.astype(BF16), (dims, ((), ())), preferred_element_type=F32)


_NN, _NT, _TN = ((1,), (0,)), ((1,), (1,)), ((0,), (0,))


def _dot3(a, b, dims):
    ah, bh = a.astype(BF16), b.astype(BF16)
    al, bl = (a - ah.astype(F32)).astype(BF16), (b - bh.astype(F32)).astype(BF16)
    d = lambda x, y: lax.dot_general(x, y, (dims, ((), ())), preferred_element_type=F32)
    return d(ah, bh) + (d(ah, bl) + d(al, bh))


def _mm_family(raw):
    nn = jax.custom_vjp(lambda a, b: raw(a, b, _NN))
    nt = jax.custom_vjp(lambda a, b: raw(a, b, _NT))
    tn = jax.custom_vjp(lambda a, b: raw(a, b, _TN))
    nn.defvjp(lambda a, b: (raw(a, b, _NN), (a, b)), lambda r, g: (raw(g, r[1], _NT), raw(r[0], g, _TN)))
    nt.defvjp(lambda a, b: (raw(a, b, _NT), (a, b)), lambda r, g: (raw(g, r[1], _NN), raw(g, r[0], _TN)))
    tn.defvjp(lambda a, b: (raw(a, b, _TN), (a, b)), lambda r, g: (raw(r[1], g, _NT), raw(r[0], g, _NN)))
    return nn, nt, tn


_mm_nn, _mm_nt, _mm_tn = _mm_family(_bdot)
_m3_nn, _m3_nt, _m3_tn = _mm_family(_dot3)


def _tri_apply(x, upper):
    C = x.shape[0]
    ii = lax.broadcasted_iota(jnp.int32, (C, C), 0)
    jj = lax.broadcasted_iota(jnp.int32, (C, C), 1)
    mat = ((ii <= jj) if upper else (ii >= jj)).astype(BF16)
    hi = x.astype(BF16)
    r = x - hi.astype(F32)
    mid = r.astype(BF16)
    lo = (r - mid.astype(F32)).astype(BF16)
    d = lambda y: jnp.dot(mat, y, preferred_element_type=F32)
    return d(hi) + (d(mid) + d(lo))


_cumsum_rows = jax.custom_vjp(lambda x: _tri_apply(x, False))
_cumsum_rows.defvjp(lambda x: (_tri_apply(x, False), None), lambda _, g: (_tri_apply(g, True),))


def _uli(a):
    C = a.shape[0]
    ii = lax.broadcasted_iota(jnp.int32, (C, C), 0)
    jj = lax.broadcasted_iota(jnp.int32, (C, C), 1)
    t = jnp.where(ii == jj, 1.0, 0.0) - a
    m = a
    for _ in range(int(math.log2(C)) - 1):
        m = _dot3(m, m, _NN)
        t = t + _dot3(t, m, _NN)
    return t


_unit_lower_inverse = jax.custom_vjp(_uli)
_unit_lower_inverse.defvjp(lambda a: (lambda t: (t, t))(_uli(a)), lambda t, g: (-_dot3(_dot3(t, g, _TN), t, _NT),))


def _prep_math(q, k, v, gcol, bcol):
    C, dv = v.shape
    ii = lax.broadcasted_iota(jnp.int32, (C, C), 0)
    jj = lax.broadcasted_iota(jnp.int32, (C, C), 1)
    causal = ii >= jj
    strict = ii > jj
    ones = jnp.ones((C, dv), F32)
    dm = _cumsum_rows(gcol * strict.astype(F32))
    decay = jnp.where(causal, jnp.exp(jnp.where(causal, dm, 0.0)), 0.0)
    gcb = _cumsum_rows(gcol * ones)
    glast = jnp.sum(gcol * ones, axis=0, keepdims=True)
    eg = jnp.exp(gcb)
    a = jnp.where(strict, bcol * _mm_nt(k, k) * decay, 0.0)
    t = _unit_lower_inverse(a)
    u_c = _m3_nn(t, v * bcol)
    w_c = _m3_nn(t, k * bcol * eg)
    qk = _mm_nt(q, k) * decay
    return u_c, w_c, q * eg, k * jnp.exp(glast - gcb), qk, jnp.exp(glast)


def _gate_cols(gates, h):
    lane = lax.broadcasted_iota(jnp.int32, gates.shape, 1)
    bcol = jnp.sum(jnp.where(lane == h, gates, 0.0), axis=1, keepdims=True)
    gcol = jnp.sum(jnp.where(lane == h + DN_HEADS, gates, 0.0), axis=1, keepdims=True)
    return gcol, bcol


DN_HB = 4


def _dn_prep_fwd(qkv, gates):
    L = qkv.shape[0]
    N, H, d, HB = L // CHUNK, DN_HEADS, DN_HEAD_DIM, DN_HB

    def body(q_ref, k_ref, v_ref, g_ref, u_ref, w_ref, qd_ref, kd_ref, qk_ref, egl_ref):
        h0 = pl.program_id(1) * HB
        gates_ = g_ref[...]
        for i in range(HB):
            lanes = pl.ds(i * d, d)
            gcol, bcol = _gate_cols(gates_, h0 + i)
            u, w, qd, kd, qk, egl = _prep_math(q_ref[:, lanes], k_ref[:, lanes], v_ref[:, lanes], gcol, bcol)
            u_ref[:, lanes] = u
            w_ref[:, lanes] = w
            qd_ref[:, lanes] = qd
            kd_ref[:, lanes] = kd
            qk_ref[0, i] = qk
            egl_ref[0, i] = jnp.broadcast_to(egl, (8, d))

    blk = lambda off: pl.BlockSpec((CHUNK, HB * d), lambda n, j: (n, off // HB + j))
    cc = pl.BlockSpec((1, HB, CHUNK, CHUNK), lambda n, j: (n, j, 0, 0))
    ee = pl.BlockSpec((1, HB, 8, d), lambda n, j: (n, j, 0, 0))
    big = _sds((L, D_DN))
    return pl.pallas_call(
        body, name="dn_prep_fwd", grid=(N, H // HB),
        in_specs=[blk(0), blk(H), blk(2 * H), pl.BlockSpec((CHUNK, 128), lambda n, j: (n, 0))],
        out_specs=[blk(0), blk(0), blk(0), blk(0), cc, ee],
        out_shape=[big, big, big, big, _sds((N, H, CHUNK, CHUNK)), _sds((N, H, 8, d))],
        compiler_params=_cp(("parallel", "parallel")),
    )(qkv, qkv, qkv, gates)


def _dn_scan_fwd(u, w, qd, kd, qk, egl):
    L = u.shape[0]
    N, H, d, HB = L // CHUNK, DN_HEADS, DN_HEAD_DIM, DN_HB

    def body(u_ref, w_ref, qd_ref, kd_ref, qk_ref, egl_ref, o_ref, st_ref, s_ref):
        n, h0 = pl.program_id(0), pl.program_id(1) * HB

        @pl.when(n == 0)
        def _():
            for i in range(HB):
                s_ref[h0 + i] = jnp.zeros((d, d), F32)

        for i in range(HB):
            lanes = pl.ds(i * d, d)
            state = s_ref[h0 + i]
            st_ref[0, i] = state
            vn = u_ref[:, lanes] - _bdot(w_ref[:, lanes], state, _NN)
            o_ref[:, lanes] = _bdot(qd_ref[:, lanes], state, _NN) + _bdot(qk_ref[0, i], vn, _NN)
            s_ref[h0 + i] = state * egl_ref[0, i, pl.ds(0, 1), :] + _bdot(kd_ref[:, lanes], vn, _TN)

    blk = pl.BlockSpec((CHUNK, HB * d), lambda n, j: (n, j))
    cc = pl.BlockSpec((1, HB, CHUNK, CHUNK), lambda n, j: (n, j, 0, 0))
    ee = pl.BlockSpec((1, HB, 8, d), lambda n, j: (n, j, 0, 0))
    return pl.pallas_call(
        body, name="dn_scan_fwd", grid=(N, H // HB), in_specs=[blk, blk, blk, blk, cc, ee],
        out_specs=[blk, pl.BlockSpec((1, HB, d, d), lambda n, j: (n, j, 0, 0))],
        out_shape=[_sds((L, D_DN)), _sds((N, H, d, d))], scratch_shapes=[pltpu.VMEM((H, d, d), F32)],
        compiler_params=_cp(("arbitrary", "arbitrary")),
    )(u, w, qd, kd, qk, egl)


def _dn_scan_bwd(u, w, qd, kd, qk, egl, states, do):
    L = u.shape[0]
    N, H, d, HB = L // CHUNK, DN_HEADS, DN_HEAD_DIM, DN_HB

    def body(u_ref, w_ref, qd_ref, kd_ref, qk_ref, egl_ref, st_ref, do_ref,
             du_ref, dw_ref, dqd_ref, dkd_ref, dqk_ref, degl_ref, ds_ref):
        n, h0 = pl.program_id(0), pl.program_id(1) * HB

        @pl.when(n == 0)
        def _():
            for i in range(HB):
                ds_ref[h0 + i] = jnp.zeros((d, d), F32)

        for i in range(HB):
            lanes = pl.ds(i * d, d)
            state, dsn = st_ref[0, i], ds_ref[h0 + i]
            w_, kd_, qd_, do_ = w_ref[:, lanes], kd_ref[:, lanes], qd_ref[:, lanes], do_ref[:, lanes]
            vn = u_ref[:, lanes] - _bdot(w_, state, _NN)
            dvn = _bdot(qk_ref[0, i], do_, _TN) + _bdot(kd_, dsn, _NN)
            du_ref[:, lanes] = dvn
            dw_ref[:, lanes] = -_bdot(dvn, state, _NT)
            dqd_ref[:, lanes] = _bdot(do_, state, _NT)
            dkd_ref[:, lanes] = _bdot(vn, dsn, _NT)
            dqk_ref[0, i] = _bdot(do_, vn, _NT)
            degl_ref[0, i] = jnp.broadcast_to(jnp.sum(dsn * state, keepdims=True), (8, d))
            ds_ref[h0 + i] = (_bdot(qd_, do_, _TN) - _bdot(w_, dvn, _TN)) + dsn * egl_ref[0, i, pl.ds(0, 1), :]

    blk = pl.BlockSpec((CHUNK, HB * d), lambda n, j: (N - 1 - n, j))
    cc = pl.BlockSpec((1, HB, CHUNK, CHUNK), lambda n, j: (N - 1 - n, j, 0, 0))
    ee = pl.BlockSpec((1, HB, 8, d), lambda n, j: (N - 1 - n, j, 0, 0))
    ss = pl.BlockSpec((1, HB, d, d), lambda n, j: (N - 1 - n, j, 0, 0))
    big = _sds((L, D_DN))
    return pl.pallas_call(
        body, name="dn_scan_bwd", grid=(N, H // HB), in_specs=[blk, blk, blk, blk, cc, ee, ss, blk],
        out_specs=[blk, blk, blk, blk, cc, ee],
        out_shape=[big, big, big, big, _sds((N, H, CHUNK, CHUNK)), _sds((N, H, 8, d))],
        scratch_shapes=[pltpu.VMEM((H, d, d), F32)], compiler_params=_cp(("arbitrary", "arbitrary")),
    )(u, w, qd, kd, qk, egl, states, do)


def _dn_prep_bwd(qkv, gates, du, dw, dqd, dkd, dqk, degl):
    L = qkv.shape[0]
    N, H, d, HB = L // CHUNK, DN_HEADS, DN_HEAD_DIM, DN_HB

    def body(q_ref, k_ref, v_ref, g_ref, du_ref, dw_ref, dqd_ref, dkd_ref, dqk_ref, degl_ref, dq_ref, dk_ref, dv_ref, dg_ref):
        j = pl.program_id(1)
        h0 = j * HB
        gates_ = g_ref[...]
        lane = lax.broadcasted_iota(jnp.int32, gates_.shape, 1)
        lane1 = lax.broadcasted_iota(jnp.int32, (1, d), 1)
        part = jnp.zeros(gates_.shape, F32)
        for i in range(HB):
            lanes = pl.ds(i * d, d)
            gcol, bcol = _gate_cols(gates_, h0 + i)
            _, f = jax.vjp(_prep_math, q_ref[:, lanes], k_ref[:, lanes], v_ref[:, lanes], gcol, bcol)
            cot_egl = jnp.where(lane1 == 0, degl_ref[0, i, pl.ds(0, 1), :], 0.0)
            dq, dk, dv, dgc, dbc = f((du_ref[:, lanes], dw_ref[:, lanes], dqd_ref[:, lanes], dkd_ref[:, lanes], dqk_ref[0, i], cot_egl))
            dq_ref[:, lanes] = dq
            dk_ref[:, lanes] = dk
            dv_ref[:, lanes] = dv
            part = part + jnp.where(lane == h0 + i, dbc, 0.0) + jnp.where(lane == h0 + i + DN_HEADS, dgc, 0.0)

        @pl.when(j == 0)
        def _():
            dg_ref[...] = part

        @pl.when(j > 0)
        def _():
            dg_ref[...] += part

    blk = lambda off: pl.BlockSpec((CHUNK, HB * d), lambda n, j: (n, off // HB + j))
    gsp = pl.BlockSpec((CHUNK, 128), lambda n, j: (n, 0))
    cc = pl.BlockSpec((1, HB, CHUNK, CHUNK), lambda n, j: (n, j, 0, 0))
    ee = pl.BlockSpec((1, HB, 8, d), lambda n, j: (n, j, 0, 0))
    big = _sds((L, D_DN))
    return pl.pallas_call(
        body, name="dn_prep_bwd", grid=(N, H // HB),
        in_specs=[blk(0), blk(H), blk(2 * H), gsp, blk(0), blk(0), blk(0), blk(0), cc, ee],
        out_specs=[blk(0), blk(0), blk(0), gsp], out_shape=[big, big, big, _sds((L, 128))],
        compiler_params=_cp(("parallel", "arbitrary")),
    )(qkv, qkv, qkv, gates, du, dw, dqd, dkd, dqk, degl)


def _dn_post_fwd(o, proj, nw):
    L = o.shape[0]
    d = DN_HEAD_DIM
    tm = min(512, L)

    def body(o_ref, z_ref, w_ref, y_ref):
        ov = o_ref[...]
        r = lax.rsqrt(jnp.mean(ov * ov, axis=-1, keepdims=True) + EPS)
        y_ref[...] = (ov * r * w_ref[...] * _silu(z_ref[...])).astype(BF16)

    blk = pl.BlockSpec((tm, d), lambda i, h: (i, h))
    return pl.pallas_call(
        body, name="dn_post_fwd", grid=(L // tm, DN_HEADS),
        in_specs=[blk, pl.BlockSpec((tm, d), lambda i, h: (i, OFF_ZD // d + h)), pl.BlockSpec((1, d), lambda i, h: (0, 0))],
        out_specs=blk, out_shape=_sds((L, D_DN), BF16), compiler_params=_cp(("parallel", "parallel")),
    )(o, proj, nw)


def _dn_post_bwd(o, proj, nw, dy):
    L = o.shape[0]
    d = DN_HEAD_DIM
    tm = min(512, L)

    def body(o_ref, z_ref, w_ref, dy_ref, do_ref, dz_ref, dw_ref):
        first = jnp.logical_and(pl.program_id(0) == 0, pl.program_id(1) == 0)
        ov, z, w, dyv = o_ref[...], z_ref[...], w_ref[...], dy_ref[...]
        r = lax.rsqrt(jnp.mean(ov * ov, axis=-1, keepdims=True) + EPS)
        xn = ov * r
        dz_ref[...] = (dyv * xn * w * _dsilu(z)).astype(BF16)
        dn = dyv * _silu(z)
        t = dn * w
        do_ref[...] = r * t - ov * (r * r * r) * jnp.mean(t * ov, axis=-1, keepdims=True)
        part = jnp.sum(dn * xn, axis=0, keepdims=True)

        @pl.when(first)
        def _():
            dw_ref[...] = part

        @pl.when(jnp.logical_not(first))
        def _():
            dw_ref[...] += part

    blk = pl.BlockSpec((tm, d), lambda i, h: (i, h))
    one = pl.BlockSpec((1, d), lambda i, h: (0, 0))
    return pl.pallas_call(
        body, name="dn_post_bwd", grid=(L // tm, DN_HEADS),
        in_specs=[blk, pl.BlockSpec((tm, d), lambda i, h: (i, OFF_ZD // d + h)), one, blk], out_specs=[blk, blk, one],
        out_shape=[_sds((L, D_DN)), _sds((L, D_DN), BF16), _sds((1, d))], compiler_params=_cp(("arbitrary", "arbitrary")),
    )(o, proj, nw, dy)


def _mix_fwd(s5o, dno, w_su, w_du, proj):
    L, K = s5o.shape
    N = w_su.shape[1]
    tm, tn = min(512, L), 512

    def body(a1, a2, b1, b2, gs, gd, ys_ref, yd_ref, mx_ref):
        ys = jnp.dot(a1[...], b1[...], preferred_element_type=F32)
        yd = jnp.dot(a2[...], b2[...], preferred_element_type=F32)
        ys_ref[...] = ys
        yd_ref[...] = yd
        mx_ref[...] = (_sigmoid(gs[...]) * ys + _sigmoid(gd[...]) * yd).astype(BF16)

    a = pl.BlockSpec((tm, K), lambda i, j: (i, 0))
    b = pl.BlockSpec((K, tn), lambda i, j: (0, j))
    o = pl.BlockSpec((tm, tn), lambda i, j: (i, j))
    return pl.pallas_call(
        body, name="mix_fwd", grid=(L // tm, N // tn),
        in_specs=[a, a, b, b, pl.BlockSpec((tm, tn), lambda i, j: (i, OFF_GS // tn + j)),
                  pl.BlockSpec((tm, tn), lambda i, j: (i, OFF_GD // tn + j))],
        out_specs=[o, o, o], out_shape=[_sds((L, N)), _sds((L, N)), _sds((L, N), BF16)],
        compiler_params=_cp(("parallel", "parallel")),
    )(s5o, dno, w_su, w_du, proj, proj)


def _mix_bwd(dx2b, w_out, proj, ys, yd):
    L, K = dx2b.shape
    N = w_out.shape[0]
    tm, tn = min(512, L), 512

    def body(a, b, gs, gd, ys_ref, yd_ref, dgs_ref, dgd_ref, dys_ref, dyd_ref):
        dm = lax.dot_general(a[...], b[...], (((1,), (1,)), ((), ())), preferred_element_type=F32)
        ss, sd = _sigmoid(gs[...]), _sigmoid(gd[...])
        dys_ref[...] = (dm * ss).astype(BF16)
        dyd_ref[...] = (dm * sd).astype(BF16)
        dgs_ref[...] = (dm * ys_ref[...] * ss * (1.0 - ss)).astype(BF16)
        dgd_ref[...] = (dm * yd_ref[...] * sd * (1.0 - sd)).astype(BF16)

    o = pl.BlockSpec((tm, tn), lambda i, j: (i, j))
    return pl.pallas_call(
        body, name="mix_bwd", grid=(L // tm, N // tn),
        in_specs=[pl.BlockSpec((tm, K), lambda i, j: (i, 0)), pl.BlockSpec((tn, K), lambda i, j: (j, 0)),
                  pl.BlockSpec((tm, tn), lambda i, j: (i, OFF_GS // tn + j)),
                  pl.BlockSpec((tm, tn), lambda i, j: (i, OFF_GD // tn + j)), o, o],
        out_specs=[o, o, o, o], out_shape=[_sds((L, N), BF16)] * 4, compiler_params=_cp(("parallel", "parallel")),
    )(dx2b, w_out, proj, proj, ys, yd)


def _final(mixed, w_out, x, tgt, fw):
    L, D = x.shape
    tm = min(256, L)

    def body(a_ref, b_ref, x_ref, t_ref, w_ref, dx_ref, dxb_ref, loss_ref, dw_ref):
        i = pl.program_id(0)
        x2 = x_ref[...] + jnp.dot(a_ref[...], b_ref[...], preferred_element_type=F32)
        w = w_ref[...]
        r = lax.rsqrt(jnp.mean(x2 * x2, axis=-1, keepdims=True) + EPS)
        xn = x2 * r
        e = xn * w - t_ref[...]
        lpart = 0.5 * jnp.sum(jnp.mean(e * e, axis=-1, keepdims=True), axis=0, keepdims=True)
        dy = e * (1.0 / D)
        t = dy * w
        dx2 = r * t - x2 * (r * r * r) * jnp.mean(t * x2, axis=-1, keepdims=True)
        dx_ref[...] = dx2
        dxb_ref[...] = dx2.astype(BF16)
        dwp = jnp.sum(dy * xn, axis=0, keepdims=True)
        lrow = jnp.broadcast_to(lpart, loss_ref.shape)

        @pl.when(i == 0)
        def _():
            loss_ref[...] = lrow
            dw_ref[...] = dwp

        @pl.when(i > 0)
        def _():
            loss_ref[...] += lrow
            dw_ref[...] += dwp

    row = pl.BlockSpec((tm, D), lambda i: (i, 0))
    one = pl.BlockSpec((1, D), lambda i: (0, 0))
    return pl.pallas_call(
        body, name="final", grid=(L // tm,),
        in_specs=[row, pl.BlockSpec((D, D), lambda i: (0, 0)), row, row, one],
        out_specs=[row, row, pl.BlockSpec((1, 128), lambda i: (0, 0)), one],
        out_shape=[_sds((L, D)), _sds((L, D), BF16), _sds((1, 128)), _sds((1, D))], compiler_params=_cp(("arbitrary",)),
    )(mixed, w_out, x, tgt, fw)


def _block_diag(t):
    J, g, a, b = t.shape
    eye = jnp.eye(g, dtype=t.dtype)
    return (t[:, :, :, None, :] * eye[None, :, None, :, None]).reshape(J, g * a, g * b)


def _block_diag_take(m, g):
    J, ga, gb = m.shape
    a, b = ga // g, gb // g
    m5 = m.reshape(J, g, a, g, b)
    idx = jnp.arange(g)
    return m5[:, idx, :, idx, :].transpose(1, 0, 2, 3)


def _local_step(x, tgt, ln_w, w_perm, lam_re, lam_im, log_step, b_re, b_im, c_re, c_im, s5_d, w_glu, w_su,
                conv_w, a_log, dt_bias, norm_w, w_du, w_out, fw):
    G, P, gb = S5_GROUPS, S5_STATE, S5_GROUPS // S5_BLOCKS
    h, rstd = _ln_fwd(x, ln_w)
    proj = _mm(h, w_perm, name="in_proj", tm=1024, tn=1152)

    b_re2, b_im2 = b_re.reshape(G, P * S5_GROUP), b_im.reshape(G, P * S5_GROUP)
    ls2 = log_step.reshape(G, 1)
    abar_re, abar_im, bb_re, bb_im = _s5_param_fwd(lam_re, lam_im, ls2, b_re2, b_im2)

    def to_wb(bb):
        return _block_diag(bb.reshape(S5_BLOCKS, gb, P, S5_GROUP).transpose(0, 1, 3, 2)).astype(BF16)

    def to_cb(cc):
        return _block_diag(cc.reshape(S5_BLOCKS, gb, S5_GROUP, P).transpose(0, 1, 3, 2)).astype(BF16)

    wbr, wbi, cbr, cbi = to_wb(bb_re), to_wb(bb_im), to_cb(c_re), to_cb(c_im)
    a_re_row, a_im_row = abar_re.reshape(1, G * P), abar_im.reshape(1, G * P)
    yc = _s5_core_fwd(proj, wbr, wbi, a_re_row, a_im_row, cbr, cbi)
    s5o = _s5_post_fwd(yc, proj, s5_d, w_glu)

    pad = lambda v: jnp.pad(v, ((0, 0), (DN_HEADS, 128 - 2 * DN_HEADS)))
    alog_row, dtb_row = pad(a_log), pad(dt_bias)
    qkv = _dn_conv_fwd(proj, conv_w)
    gates = _dn_gates_fwd(proj, alog_row, dtb_row)
    prep = _dn_prep_fwd(qkv, gates)
    o_dn, states = _dn_scan_fwd(*prep)
    dno = _dn_post_fwd(o_dn, proj, norm_w)

    ys, yd, mixed = _mix_fwd(s5o, dno, w_su, w_du, proj)
    dx2, dx2b, loss_row, d_fw = _final(mixed, w_out, x, tgt, fw)
    d_w_out = _mm(mixed, dx2b, ta=True, name="d_w_out")
    dgs, dgd, dys, dyd = _mix_bwd(dx2b, w_out, proj, ys, yd)
    d_w_su = _mm(s5o, dys, ta=True, name="d_w_su", shard_out=True)
    d_w_du = _mm(dno, dyd, ta=True, name="d_w_du", shard_out=True)
    ds5o = _mm(dys, w_su, tb=True, name="d_s5o")
    ddno = _mm(dyd, w_du, tb=True, name="d_dno")

    dyc, du1, dz_s, d_s5d, d_w_glu = _s5_post_bwd(yc, proj, s5_d, w_glu, ds5o)
    du, dwbr, dwbi, dcbr, dcbi, dar, dai = _s5_core_bwd(proj, wbr, wbi, a_re_row, a_im_row, cbr, cbi, dyc, du1)

    def from_wb(dwb):
        return _block_diag_take(dwb, gb).transpose(0, 1, 3, 2).reshape(G, P * S5_GROUP)

    def from_cb(dcb):
        return _block_diag_take(dcb, gb).transpose(0, 1, 3, 2).reshape(G, S5_GROUP, P)

    d_lam_re, d_lam_im, d_ls, d_b_re, d_b_im = _s5_param_bwd(
        lam_re, lam_im, ls2, b_re2, b_im2, dar.reshape(G, P), dai.reshape(G, P), from_wb(dwbr), from_wb(dwbi))

    do_dn, dz_d, d_norm_w = _dn_post_bwd(o_dn, proj, norm_w, ddno)
    dq, dk, dv, dgates = _dn_prep_bwd(qkv, gates, *_dn_scan_bwd(*prep, states, do_dn))
    dqkv, d_conv = _dn_conv_bwd(proj, conv_w, jnp.concatenate([dq, dk, dv], axis=1))
    dpb, d_alog_row, d_dtb_row = _dn_gates_bwd(proj, alog_row, dtb_row, dgates)

    dproj = jnp.concatenate([du, dz_s, dqkv, dz_d, dgs, dgd, dpb], axis=1)
    d_w_perm = _mm(h, dproj, ta=True, name="d_w_in", tm=1024, tn=1152)
    dh = _mm(dproj, w_perm, tb=True, name="d_h", tm=2048, tn=1024, tk=1152)
    grad_x, d_ln_w = _ln_bwd(x, rstd, ln_w, dh, dx2)

    grads = dict(
        ln_w=d_ln_w, w_perm=d_w_perm, s5_lam_re=d_lam_re, s5_lam_im=d_lam_im, s5_log_step=d_ls.reshape(1, G),
        s5_b_re=d_b_re.reshape(G, P, S5_GROUP), s5_b_im=d_b_im.reshape(G, P, S5_GROUP),
        s5_c_re=from_cb(dcbr), s5_c_im=from_cb(dcbi), s5_d=d_s5d, s5_w_glu=d_w_glu, s5_w_up=d_w_su,
        dn_conv_w=d_conv, dn_a_log=d_alog_row[:, DN_HEADS:2 * DN_HEADS], dn_dt_bias=d_dtb_row[:, DN_HEADS:2 * DN_HEADS],
        dn_norm_w=d_norm_w, dn_w_up=d_w_du, w_out=d_w_out, final_norm_w=d_fw)
    return loss_row, grad_x, grads


def _place():
    x, y, c = lax.axis_index("x"), lax.axis_index("y"), lax.axis_index("c")
    return x, y, c


def _remote(src, dst, send_sem, recv_sem, to):
    return pltpu.make_async_remote_copy(src_ref=src, dst_ref=dst, send_sem=send_sem, recv_sem=recv_sem,
                                        device_id=to, device_id_type=MESH)


def _gather_weights(shards, conv_w):
    na = len(shards)

    def body(*refs):
        ins, conv_in = refs[:na], refs[na]
        outs, conv_out = refs[na + 1:2 * na + 1], refs[2 * na + 1]
        send_sems, recv_sems, local_sems = refs[2 * na + 2:]
        x, y, c = _place()
        me = 2 * x + y
        sibling = (x, y, 1 - c)
        chips = [(1 - x, y), (x, 1 - y), (1 - x, 1 - y)]

        def part(ref, chip, half, a):
            r2 = shards[a].shape[0] // 2
            return ref.at[chip, pl.ds(half * r2, r2)]

        own = [pltpu.make_async_copy(ins[a], outs[a].at[me], local_sems.at[a]) for a in range(na)]
        own.append(pltpu.make_async_copy(conv_in, conv_out.at[me], local_sems.at[na]))
        for cp in own:
            cp.start()
        sends = []
        for a in range(na):
            r2 = shards[a].shape[0] // 2
            for j, (px, py) in enumerate(chips):
                k = 6 * a + j
                sends.append(_remote(ins[a].at[pl.ds(c * r2, r2)], part(outs[a], me, c, a), send_sems.at[k], recv_sems.at[k],
                                     (px, py, c)))
        for j, (px, py) in enumerate(chips):
            k = 6 * na + j
            sends.append(_remote(conv_in, conv_out.at[me], send_sems.at[k], recv_sems.at[k], (px, py, c)))
        for cp in sends:
            cp.start()
        for a in range(na):
            for j, (px, py) in enumerate(chips):
                src = 2 * px + py
                k = 6 * a + j
                landed = part(outs[a], src, c, a)
                _remote(landed, landed, send_sems.at[k], recv_sems.at[k], (px, py, c)).wait_recv()
                fwd = _remote(landed, landed, send_sems.at[k + 3], recv_sems.at[k + 3], sibling)
                fwd.start()
                sends.append(fwd)
        for a in range(na):
            for j, (px, py) in enumerate(chips):
                other = part(outs[a], 2 * px + py, 1 - c, a)
                _remote(other, other, send_sems.at[6 * a + 3 + j], recv_sems.at[6 * a + 3 + j], sibling).wait_recv()
        for j, (px, py) in enumerate(chips):
            k = 6 * na + j
            _remote(conv_in, conv_out.at[2 * px + py], send_sems.at[k], recv_sems.at[k], (px, py, c)).wait_recv()
        for cp in sends:
            cp.wait_send()
        for cp in own:
            cp.wait()

    nsem = 6 * na + 3
    return pl.pallas_call(
        body, name="gather_weights", in_specs=[ANY] * (na + 1), out_specs=[ANY] * (na + 1),
        out_shape=[_sds((N_CHIPS,) + s.shape, s.dtype) for s in shards] + [_sds((N_CHIPS,) + conv_w.shape, conv_w.dtype)],
        scratch_shapes=[pltpu.SemaphoreType.DMA((nsem,)), pltpu.SemaphoreType.DMA((nsem,)), pltpu.SemaphoreType.DMA((na + 1,))],
    )(*shards, conv_w)


def _swap_halves(gxs):
    na = len(gxs)

    def body(*refs):
        ins, outs = refs[:na], refs[na:2 * na]
        send_sems, recv_sems = refs[2 * na:]
        x, y, c = _place()
        cps = [_remote(ins[a].at[pl.ds(0, N_CHIPS), pl.ds(1 - c, 1)], outs[a], send_sems.at[a], recv_sems.at[a], (x, y, 1 - c))
               for a in range(na)]
        for cp in cps:
            cp.start()
        for cp in cps:
            cp.wait()

    return pl.pallas_call(
        body, name="rs_swap_halves", in_specs=[ANY] * na, out_specs=[ANY] * na,
        out_shape=[_sds((N_CHIPS, 1) + g.shape[2:], g.dtype) for g in gxs],
        scratch_shapes=[pltpu.SemaphoreType.DMA((na,)), pltpu.SemaphoreType.DMA((na,))],
    )(*gxs)


def _to_owners(csbs):
    na = len(csbs)

    def body(*refs):
        ins, outs = refs[:na], refs[na:2 * na]
        send_sems, recv_sems = refs[2 * na:]
        x, y, c = _place()
        me = 2 * x + y
        cps = []
        for a in range(na):
            for k in range(N_CHIPS - 1):
                j = (me + 1 + k) % N_CHIPS
                cps.append(_remote(ins[a].at[k], outs[a].at[2 - k], send_sems.at[3 * a + k], recv_sems.at[3 * a + 2 - k],
                                   (j // 2, j % 2, c)))
        for cp in cps:
            cp.start()
        for a in range(na):
            for k in range(N_CHIPS - 1):
                _remote(ins[a].at[k], outs[a].at[k], send_sems.at[3 * a + k], recv_sems.at[3 * a + k], (x, y, c)).wait_recv()
        for cp in cps:
            cp.wait_send()

    return pl.pallas_call(
        body, name="rs_to_owners", in_specs=[ANY] * na, out_specs=[ANY] * na,
        out_shape=[_sds(g.shape, g.dtype) for g in csbs],
        scratch_shapes=[pltpu.SemaphoreType.DMA((3 * na,)), pltpu.SemaphoreType.DMA((3 * na,))],
    )(*csbs)


def _share_halves(gfs):
    na = len(gfs)

    def body(*refs):
        ins, outs = refs[:na], refs[na:2 * na]
        send_sems, recv_sems = refs[2 * na:]
        x, y, c = _place()
        cps = [_remote(ins[a].at[pl.ds(c, 1)], outs[a].at[pl.ds(c, 1)], send_sems.at[a], recv_sems.at[a], (x, y, 1 - c))
               for a in range(na)]
        for cp in cps:
            cp.start()
        for a in range(na):
            cps[a].wait_send()
            _remote(ins[a].at[pl.ds(1 - c, 1)], outs[a].at[pl.ds(1 - c, 1)], send_sems.at[a], recv_sems.at[a], (x, y, 1 - c)).wait_recv()

    return pl.pallas_call(
        body, name="rs_share_halves", in_specs=[ANY] * na, out_specs=[ANY] * na,
        out_shape=[_sds(g.shape, g.dtype) for g in gfs], input_output_aliases={a: a for a in range(na)},
        scratch_shapes=[pltpu.SemaphoreType.DMA((na,)), pltpu.SemaphoreType.DMA((na,))],
    )(*gfs)


def _row_tile(rows, cols, budget=1 << 20):
    t = rows
    while t % 2 == 0 and t > 16 and t * cols * 4 > budget:
        t //= 2
    return t


def _chip_sums(gx, r1, where):
    _, _, r2, cd = gx.shape
    tr = _row_tile(r2, cd)

    def body(w_ref, a_ref, b_ref, o_ref):
        o_ref[...] = (a_ref[0] + b_ref[0]).astype(BF16)

    other = lambda k, i, w: ((w[1] + 1 + k) % N_CHIPS, w[0], i, 0)
    other0 = lambda k, i, w: ((w[1] + 1 + k) % N_CHIPS, 0, i, 0)
    return pl.pallas_call(
        body, name="rs_chip_sums",
        grid_spec=pltpu.PrefetchScalarGridSpec(
            num_scalar_prefetch=1, grid=(N_CHIPS - 1, r2 // tr),
            in_specs=[pl.BlockSpec((1, 1, tr, cd), other), pl.BlockSpec((1, 1, tr, cd), other0)],
            out_specs=pl.BlockSpec((1, tr, cd), lambda k, i, w: (k, i, 0))),
        out_shape=_sds((N_CHIPS - 1, r2, cd), BF16), compiler_params=_cp(("parallel", "parallel")),
    )(where, gx, r1)


def _owner_sum(gx, r1, r2x, where):
    _, _, r2, cd = gx.shape
    tr = _row_tile(r2, cd)

    def body(w_ref, a_ref, b_ref, r_ref, o_ref):
        acc = a_ref[0, 0] + b_ref[0, 0]
        for k in range(N_CHIPS - 1):
            acc = acc + r_ref[k].astype(F32)
        o_ref[0] = acc

    return pl.pallas_call(
        body, name="rs_owner_sum",
        grid_spec=pltpu.PrefetchScalarGridSpec(
            num_scalar_prefetch=1, grid=(r2 // tr,),
            in_specs=[pl.BlockSpec((1, 1, tr, cd), lambda i, w: (w[1], w[0], i, 0)),
                      pl.BlockSpec((1, 1, tr, cd), lambda i, w: (w[1], 0, i, 0)),
                      pl.BlockSpec((N_CHIPS - 1, tr, cd), lambda i, w: (0, i, 0))],
            out_specs=pl.BlockSpec((1, tr, cd), lambda i, w: (w[0], i, 0))),
        out_shape=_sds((2, r2, cd)), compiler_params=_cp(("parallel",)),
    )(where, gx, r1, r2x)


def _adamw_math(w, g, m, v):
    m = ADAM_B1 * m + (1.0 - ADAM_B1) * g
    v = ADAM_B2 * v + (1.0 - ADAM_B2) * (g * g)
    m_hat = m / (1.0 - ADAM_B1 ** ADAM_STEP)
    v_hat = v / (1.0 - ADAM_B2 ** ADAM_STEP)
    delta = -ADAM_LR * (m_hat / (jnp.sqrt(v_hat) + ADAM_EPS) + ADAM_WD * w)
    return delta, m, v


def _adamw(w, g, m, v, name):
    rows, cd = w.shape
    tr = _row_tile(rows, cd, budget=3 << 19) if rows % 16 == 0 else rows

    def body(w_ref, g_ref, m_ref, v_ref, d_ref, mo_ref, vo_ref):
        d, mm, vv = _adamw_math(w_ref[...], g_ref[...], m_ref[...], v_ref[...])
        d_ref[...] = d
        mo_ref[...] = mm
        vo_ref[...] = vv

    blk = pl.BlockSpec((tr, cd), lambda i: (i, 0))
    return pl.pallas_call(
        body, name=name, grid=(rows // tr,), in_specs=[blk] * 4, out_specs=[blk] * 3, out_shape=[_sds(w.shape)] * 3,
        compiler_params=_cp(("parallel",)),
    )(w, g, m, v)


def _small_allreduce_adamw(gp, wp, mp, vp):
    R = gp.shape[0]
    R2 = R // 2
    assert R2 % 8 == 0

    def body(g_ref, w_ref, m_ref, v_ref, go_ref, d_ref, mo_ref, vo_ref, sib, csum, land, send_sems, recv_sems):
        x, y, c = _place()
        me = 2 * x + y
        sibling = (x, y, 1 - c)
        chips = [(1 - x, y), (x, 1 - y), (1 - x, 1 - y)]
        swap = _remote(g_ref, sib, send_sems.at[0], recv_sems.at[0], sibling)
        swap.start()
        swap.wait()
        csum[...] = g_ref[...] + sib[...]
        half = csum.at[pl.ds(c * R2, R2)]
        land[me] = csum[pl.ds(c * R2, R2), :]
        cps = [_remote(half, land.at[me], send_sems.at[1 + j], recv_sems.at[1 + j], (px, py, c))
               for j, (px, py) in enumerate(chips)]
        for cp in cps:
            cp.start()
        for j, (px, py) in enumerate(chips):
            _remote(half, land.at[2 * px + py], send_sems.at[1 + j], recv_sems.at[1 + j], (px, py, c)).wait_recv()
        for cp in cps:
            cp.wait_send()
        mine = go_ref.at[pl.ds(c * R2, R2)]
        go_ref[pl.ds(c * R2, R2), :] = (land[0] + land[1]) + (land[2] + land[3])
        share = _remote(mine, mine, send_sems.at[4], recv_sems.at[4], sibling)
        share.start()
        share.wait_send()
        other = go_ref.at[pl.ds((1 - c) * R2, R2)]
        _remote(other, other, send_sems.at[4], recv_sems.at[4], sibling).wait_recv()
        d, mm, vv = _adamw_math(w_ref[...], go_ref[...], m_ref[...], v_ref[...])
        d_ref[...] = d
        mo_ref[...] = mm
        vo_ref[...] = vv

    vm = pl.BlockSpec(memory_space=pltpu.VMEM)
    return pl.pallas_call(
        body, name="small_allreduce_adamw", in_specs=[vm] * 4, out_specs=[vm] * 4, out_shape=[_sds((R, 128))] * 4,
        scratch_shapes=[pltpu.VMEM((R, 128), F32), pltpu.VMEM((R, 128), F32), pltpu.VMEM((N_CHIPS, R2, 128), F32),
                        pltpu.SemaphoreType.DMA((5,)), pltpu.SemaphoreType.DMA((5,))],
        compiler_params=_cp(),
    )(gp, wp, mp, vp)


def _pack(arrs):
    rows = []
    for a in arrs:
        f = a.reshape(-1)
        f = jnp.pad(f, (0, (-f.shape[0]) % 128))
        rows.append(f.reshape(-1, 128))
    p = jnp.concatenate(rows, axis=0)
    return jnp.pad(p, ((0, (-p.shape[0]) % 8), (0, 0)))


def _unpack(p, shapes):
    out, r = [], 0
    for s in shapes:
        n = math.prod(s)
        nr = -(-n // 128)
        out.append(p[r:r + nr].reshape(-1)[:n].reshape(s))
        r += nr
    return out


_SMALL = ("ln_w", "s5_lam_re", "s5_lam_im", "s5_log_step", "s5_b_re", "s5_b_im", "s5_c_re", "s5_c_im", "s5_d",
          "dn_a_log", "dn_dt_bias", "dn_norm_w", "final_norm_w")
_BIG = ("w_in", "s5_w_glu", "s5_w_up", "dn_w_up", "w_out")
_ORDER = ("ln_w", "w_in", "s5_lam_re", "s5_lam_im", "s5_log_step", "s5_b_re", "s5_b_im", "s5_c_re", "s5_c_im", "s5_d",
          "s5_w_glu", "s5_w_up", "dn_conv_w", "dn_a_log", "dn_dt_bias", "dn_norm_w", "dn_w_up", "w_out", "final_norm_w")


def kernel(x, ln_w, w_in, s5_lam_re, s5_lam_im, s5_log_step, s5_b_re, s5_b_im, s5_c_re, s5_c_im, s5_d, s5_w_glu, s5_w_up, dn_conv_w, dn_a_log, dn_dt_bias, dn_norm_w, dn_w_up, w_out, final_norm_w, loss_target, m_ln_w, m_w_in, m_s5_lam_re, m_s5_lam_im, m_s5_log_step, m_s5_b_re, m_s5_b_im, m_s5_c_re, m_s5_c_im, m_s5_d, m_s5_w_glu, m_s5_w_up, m_dn_conv_w, m_dn_a_log, m_dn_dt_bias, m_dn_norm_w, m_dn_w_up, m_w_out, m_final_norm_w, v_ln_w, v_w_in, v_s5_lam_re, v_s5_lam_im, v_s5_log_step, v_s5_b_re, v_s5_b_im, v_s5_c_re, v_s5_c_im, v_s5_d, v_s5_w_glu, v_s5_w_up, v_dn_conv_w, v_dn_a_log, v_dn_dt_bias, v_dn_norm_w, v_dn_w_up, v_w_out, v_final_norm_w):
    w = dict(ln_w=ln_w, w_in=w_in, s5_lam_re=s5_lam_re, s5_lam_im=s5_lam_im, s5_log_step=s5_log_step, s5_b_re=s5_b_re,
             s5_b_im=s5_b_im, s5_c_re=s5_c_re, s5_c_im=s5_c_im, s5_d=s5_d, s5_w_glu=s5_w_glu, s5_w_up=s5_w_up,
             dn_conv_w=dn_conv_w, dn_a_log=dn_a_log, dn_dt_bias=dn_dt_bias, dn_norm_w=dn_norm_w, dn_w_up=dn_w_up, w_out=w_out,
             final_norm_w=final_norm_w)
    m = dict(ln_w=m_ln_w, w_in=m_w_in, s5_lam_re=m_s5_lam_re, s5_lam_im=m_s5_lam_im, s5_log_step=m_s5_log_step,
             s5_b_re=m_s5_b_re, s5_b_im=m_s5_b_im, s5_c_re=m_s5_c_re, s5_c_im=m_s5_c_im, s5_d=m_s5_d, s5_w_glu=m_s5_w_glu,
             s5_w_up=m_s5_w_up, dn_conv_w=m_dn_conv_w, dn_a_log=m_dn_a_log, dn_dt_bias=m_dn_dt_bias, dn_norm_w=m_dn_norm_w,
             dn_w_up=m_dn_w_up, w_out=m_w_out, final_norm_w=m_final_norm_w)
    v = dict(ln_w=v_ln_w, w_in=v_w_in, s5_lam_re=v_s5_lam_re, s5_lam_im=v_s5_lam_im, s5_log_step=v_s5_log_step,
             s5_b_re=v_s5_b_re, s5_b_im=v_s5_b_im, s5_c_re=v_s5_c_re, s5_c_im=v_s5_c_im, s5_d=v_s5_d, s5_w_glu=v_s5_w_glu,
             s5_w_up=v_s5_w_up, dn_conv_w=v_dn_conv_w, dn_a_log=v_dn_a_log, dn_dt_bias=v_dn_dt_bias, dn_norm_w=v_dn_norm_w,
             dn_w_up=v_dn_w_up, w_out=v_w_out, final_norm_w=v_final_norm_w)
    xi, yi, ci = _place()
    chip = 2 * xi + yi
    where = jnp.stack([ci, chip]).astype(jnp.int32)

    g_in, g_glu, g_su, g_du, g_out, g_conv = _gather_weights(
        [w[n][0].astype(BF16) for n in _BIG], dn_conv_w[0])
    cat = lambda g: jnp.concatenate([g[j] for j in range(N_CHIPS)], axis=1)
    w_full = cat(g_in)
    w_perm = jnp.concatenate([w_full[:, :OFF_GS], w_full[:, OFF_GS + 2 * DN_HEADS:], w_full[:, OFF_GS:OFF_GS + 2 * DN_HEADS],
                              jnp.zeros((D_MODEL, D_IN_PAD - D_IN), BF16)], axis=1)

    loss_row, grad_x, g = _local_step(
        x[0], loss_target[0], ln_w, w_perm, s5_lam_re[0], s5_lam_im[0], s5_log_step, s5_b_re[0], s5_b_im[0], s5_c_re[0],
        s5_c_im[0], s5_d, g_glu.reshape(D_S5, D_S5), cat(g_su), cat(g_conv), dn_a_log, dn_dt_bias, dn_norm_w, cat(g_du),
        g_out.reshape(D_MODEL, D_MODEL), final_norm_w[None])
    loss = lax.psum(loss_row[0, 0], ("x", "y", "c"))

    dwp = g["w_perm"]
    d_w_in = jnp.concatenate([dwp[:, :OFF_GS], dwp[:, OFF_B:OFF_B + 2 * DN_HEADS], dwp[:, OFF_GS:OFF_B]], axis=1)
    cw = D_IN // N_CHIPS
    gxs = [d_w_in.reshape(D_MODEL, N_CHIPS, cw).transpose(1, 0, 2).reshape(N_CHIPS, 2, D_MODEL // 2, cw),
           g["s5_w_glu"].reshape(N_CHIPS, 2, D_S5 // 8, D_S5),
           g["s5_w_up"].reshape(N_CHIPS, 2, D_S5 // 2, D_MODEL // N_CHIPS),
           g["dn_w_up"].reshape(N_CHIPS, 2, D_DN // 2, D_MODEL // N_CHIPS),
           g["w_out"].reshape(N_CHIPS, 2, D_MODEL // 8, D_MODEL)]
    r1s = _swap_halves(gxs)
    csbs = [_chip_sums(gx, r1, where) for gx, r1 in zip(gxs, r1s)]
    r2s = _to_owners(csbs)
    gfs = [_owner_sum(gx, r1, r2x, where) for gx, r1, r2x in zip(gxs, r1s, r2s)]
    gfs = _share_halves(gfs)
    grads, deltas, new_m, new_v = {}, {}, {}, {}
    for n, gf in zip(_BIG, gfs):
        shp = w[n].shape
        g2 = gf.reshape(shp[1:])
        d_, m_, v_ = _adamw(w[n][0], g2, m[n][0], v[n][0], "adamw_" + n)
        grads[n], deltas[n], new_m[n], new_v[n] = g2.reshape(shp), d_.reshape(shp), m_.reshape(shp), v_.reshape(shp)

    gp = _pack([g[n] for n in _SMALL] + [g["dn_conv_w"]])
    zc = jnp.zeros((CONV_K, 3 * D_DN), F32)
    go, dl, mo, vo = _small_allreduce_adamw(gp, _pack([w[n] for n in _SMALL] + [zc]), _pack([m[n] for n in _SMALL] + [zc]),
                                            _pack([v[n] for n in _SMALL] + [zc]))
    shapes = [w[n].shape for n in _SMALL] + [(CONV_K, 3 * D_DN)]
    for dst, src in ((grads, go), (deltas, dl), (new_m, mo), (new_v, vo)):
        for n, a in zip(_SMALL, _unpack(src, shapes)):
            dst[n] = a
    cc = 3 * D_DN // N_CHIPS
    g_conv_mine = lax.dynamic_slice(_unpack(go, shapes)[-1], (0, chip * cc), (CONV_K, cc))
    d_, m_, v_ = _adamw(dn_conv_w[0], g_conv_mine, m_dn_conv_w[0], v_dn_conv_w[0], "adamw_dn_conv_w")
    grads["dn_conv_w"], deltas["dn_conv_w"], new_m["dn_conv_w"], new_v["dn_conv_w"] = (
        g_conv_mine[None], d_[None], m_[None], v_[None])

    return (loss, grad_x[None], *[grads[n] for n in _ORDER], *[deltas[n] for n in _ORDER], *[new_m[n] for n in _ORDER],
            *[new_v[n] for n in _ORDER])
```

```python
import functools
import math

import jax
import jax.numpy as jnp
from jax import lax
from jax.experimental import pallas as pl
from jax.experimental.pallas import tpu as pltpu

F32 = jnp.float32
BF16 = jnp.bfloat16
HI = lax.Precision.HIGHEST
MESH = pl.DeviceIdType.MESH
ANY = pl.BlockSpec(memory_space=pl.ANY)

EPS = 1e-6
D_MODEL = 2048
D_S5 = 1024
S5_GROUP = 16
S5_GROUPS = 64
S5_STATE = 64
S5_BLOCKS = 8
S5_SEG = 8
DN_HEADS = 8
DN_HEAD_DIM = 128
D_DN = 1024
CONV_K = 4
CHUNK = 64
D_IN = 10256
D_IN_PAD = 10368
OFF_US, OFF_ZS, OFF_Q, OFF_K, OFF_V, OFF_ZD, OFF_GS, OFF_GD, OFF_B = 0, 1024, 2048, 3072, 4096, 5120, 6144, 8192, 10240
N_CHIPS = 4
N_DEV = 8
VMEM_LIMIT = 56 * 1024 * 1024

ADAM_LR = 0.001
ADAM_B1 = 0.9
ADAM_B2 = 0.999
ADAM_EPS = 1e-08
ADAM_WD = 0.01
ADAM_STEP = 10


def _cp(sem=None):
    return pltpu.CompilerParams(dimension_semantics=sem, vmem_limit_bytes=VMEM_LIMIT)


def _sds(shape, dtype=F32):
    return jax.ShapeDtypeStruct(tuple(shape), dtype)


def _sigmoid(x):
    return 1.0 / (1.0 + jnp.exp(-x))


def _silu(x):
    return x * _sigmoid(x)


def _dsilu(x):
    s = _sigmoid(x)
    return s * (1.0 + x * (1.0 - s))


def _mm(a, b, *, name, ta=False, tb=False, out_dtype=F32, tm=512, tn=512, tk=2048, shard_out=False):
    if ta:
        K, M = a.shape
    else:
        M, K = a.shape
    if tb:
        N, K2 = b.shape
    else:
        K2, N = b.shape
    assert K == K2, (a.shape, b.shape)
    tm, tn, tk = min(tm, M), min(tn, N), min(tk, K)
    assert M % tm == 0 and N % tn == 0 and K % tk == 0, (M, N, K, tm, tn, tk)
    nk = K // tk
    dims = (((0 if ta else 1,), (1 if tb else 0,)), ((), ()))

    def body(a_ref, b_ref, o_ref, *acc):
        k = pl.program_id(2)
        p = lax.dot_general(a_ref[...].astype(BF16), b_ref[...].astype(BF16), dims, preferred_element_type=F32)
        if nk == 1:
            o_ref[...] = p.astype(out_dtype).reshape(o_ref.shape)
            return
        acc_ref, = acc

        @pl.when(k == 0)
        def _():
            acc_ref[...] = p

        @pl.when(k > 0)
        def _():
            acc_ref[...] += p

        @pl.when(k == nk - 1)
        def _():
            o_ref[...] = acc_ref[...].astype(out_dtype).reshape(o_ref.shape)

    a_spec = pl.BlockSpec((tk, tm), lambda i, j, k: (k, i)) if ta else pl.BlockSpec((tm, tk), lambda i, j, k: (i, k))
    b_spec = pl.BlockSpec((tn, tk), lambda i, j, k: (j, k)) if tb else pl.BlockSpec((tk, tn), lambda i, j, k: (k, j))
    if shard_out:
        o_spec = pl.BlockSpec((1, tm, tn), lambda i, j, k: (j, i, 0))
        o_shape = _sds((N // tn, M, tn), out_dtype)
    else:
        o_spec = pl.BlockSpec((tm, tn), lambda i, j, k: (i, j))
        o_shape = _sds((M, N), out_dtype)
    return pl.pallas_call(
        body, name=name, grid=(M // tm, N // tn, nk), in_specs=[a_spec, b_spec], out_specs=o_spec, out_shape=o_shape,
        scratch_shapes=[pltpu.VMEM((tm, tn), F32)] if nk > 1 else [],
        compiler_params=_cp(("parallel", "parallel", "arbitrary")),
    )(a, b)


def _ln_fwd(x, w):
    L, D = x.shape
    tm = min(256, L)

    def body(x_ref, w_ref, h_ref, r_ref):
        xv = x_ref[...]
        r = lax.rsqrt(jnp.mean(xv * xv, axis=-1, keepdims=True) + EPS)
        h_ref[...] = (xv * r * w_ref[...]).astype(BF16)
        r_ref[...] = r

    return pl.pallas_call(
        body, name="ln_fwd", grid=(L // tm,),
        in_specs=[pl.BlockSpec((tm, D), lambda i: (i, 0)), pl.BlockSpec((1, D), lambda i: (0, 0))],
        out_specs=[pl.BlockSpec((tm, D), lambda i: (i, 0)), pl.BlockSpec((tm, 1), lambda i: (i, 0))],
        out_shape=[_sds((L, D), BF16), _sds((L, 1))], compiler_params=_cp(("parallel",)),
    )(x, w)


def _ln_bwd(x, r, w, dh, dx2):
    L, D = x.shape
    tm = min(256, L)

    def body(x_ref, r_ref, w_ref, dh_ref, dx2_ref, dx_ref, dw_ref):
        i = pl.program_id(0)
        xv, rv, dhv = x_ref[...], r_ref[...], dh_ref[...]
        t = dhv * w_ref[...]
        m = jnp.mean(t * xv, axis=-1, keepdims=True)
        dx_ref[...] = dx2_ref[...] + rv * t - xv * (rv * rv * rv) * m
        part = jnp.sum(dhv * xv * rv, axis=0, keepdims=True)

        @pl.when(i == 0)
        def _():
            dw_ref[...] = part

        @pl.when(i > 0)
        def _():
            dw_ref[...] += part

    row = pl.BlockSpec((tm, D), lambda i: (i, 0))
    return pl.pallas_call(
        body, name="ln_bwd", grid=(L // tm,),
        in_specs=[row, pl.BlockSpec((tm, 1), lambda i: (i, 0)), pl.BlockSpec((1, D), lambda i: (0, 0)), row, row],
        out_specs=[row, pl.BlockSpec((1, D), lambda i: (0, 0))],
        out_shape=[_sds((L, D)), _sds((1, D))], compiler_params=_cp(("arbitrary",)),
    )(x, r, w, dh, dx2)


def _s5_param_math(lam_re, lam_im, log_step, b_re, b_im, expand):
    step = jnp.exp(log_step)
    mag = jnp.exp(lam_re * step)
    abar_re = mag * jnp.cos(lam_im * step)
    abar_im = mag * jnp.sin(lam_im * step)
    den = lam_re * lam_re + lam_im * lam_im
    xr = abar_re - 1.0
    f_re = (xr * lam_re + abar_im * lam_im) / den
    f_im = (abar_im * lam_re - xr * lam_im) / den
    fe_re = jnp.dot(f_re, expand, precision=HI, preferred_element_type=F32)
    fe_im = jnp.dot(f_im, expand, precision=HI, preferred_element_type=F32)
    bb_re = fe_re * b_re - fe_im * b_im
    bb_im = fe_re * b_im + fe_im * b_re
    return abar_re, abar_im, bb_re, bb_im


def _s5_expand():
    p = lax.broadcasted_iota(jnp.int32, (S5_STATE, S5_STATE * S5_GROUP), 0)
    q = lax.broadcasted_iota(jnp.int32, (S5_STATE, S5_STATE * S5_GROUP), 1)
    return (q // S5_GROUP == p).astype(F32)


def _s5_param_fwd(lam_re, lam_im, log_step, b_re, b_im):
    G, P = lam_re.shape

    def body(lr, li, ls, br, bi, ar_o, ai_o, bbr_o, bbi_o):
        outs = _s5_param_math(lr[...], li[...], ls[...], br[...], bi[...], _s5_expand())
        for o, v in zip((ar_o, ai_o, bbr_o, bbi_o), outs):
            o[...] = v

    return pl.pallas_call(
        body, name="s5_param_fwd",
        out_shape=[_sds((G, P)), _sds((G, P)), _sds(b_re.shape), _sds(b_re.shape)], compiler_params=_cp(),
    )(lam_re, lam_im, log_step, b_re, b_im)


def _s5_param_bwd(lam_re, lam_im, log_step, b_re, b_im, dar, dai, dbbr, dbbi):
    G, P = lam_re.shape

    def body(lr, li, ls, br, bi, g0, g1, g2, g3, dlr, dli, dls, dbr, dbi):
        ex = _s5_expand()
        _, f = jax.vjp(lambda a, b, c, d, e: _s5_param_math(a, b, c, d, e, ex), lr[...], li[...], ls[...], br[...], bi[...])
        grads = f((g0[...], g1[...], g2[...], g3[...]))
        for o, v in zip((dlr, dli, dls, dbr, dbi), grads):
            o[...] = v

    return pl.pallas_call(
        body, name="s5_param_bwd",
        out_shape=[_sds((G, P)), _sds((G, P)), _sds((G, 1)), _sds(b_re.shape), _sds(b_re.shape)], compiler_params=_cp(),
    )(lam_re, lam_im, log_step, b_re, b_im, dar, dai, dbbr, dbbi)


def _to_segs(src_ref, dst_ref, L):
    S = L // S5_SEG

    def body(j, carry):
        dst_ref[pl.ds(pl.multiple_of(S5_SEG * j, S5_SEG), S5_SEG), :] = src_ref[pl.ds(j, S5_SEG, stride=S), :]
        return carry

    lax.fori_loop(0, S, body, 0, unroll=8)


def _from_segs(src_ref, L, write):
    S = L // S5_SEG
    for seg in range(S5_SEG):
        def body(jb, carry, seg=seg):
            j0 = 16 * jb
            write(pl.multiple_of(seg * S + j0, 16), src_ref[pl.ds(S5_SEG * j0 + seg, 16, stride=S5_SEG), :])
            return carry

        lax.fori_loop(0, S // 16, body, 0, unroll=4)


def _scan_segs(ar, ai, re_ref, im_ref, end_r_ref, end_i_ref, c_r_ref, c_i_ref, L, tile0, reverse):
    S = L // S5_SEG
    NB, LN = re_ref.shape[0], 128
    assert S & (S - 1) == 0
    tile = lambda j: pl.ds(pl.multiple_of(S5_SEG * (tile0 + j), S5_SEG), S5_SEG)
    ar8 = [jnp.broadcast_to(ar[:, b * LN:(b + 1) * LN], (S5_SEG, LN)) for b in range(NB)]
    ai8 = [jnp.broadcast_to(ai[:, b * LN:(b + 1) * LN], (S5_SEG, LN)) for b in range(NB)]

    def step(idx, carry):
        rows = tile(S - 1 - idx if reverse else idx)
        out = []
        for b in range(NB):
            sr, si = carry[b]
            nr = ar8[b] * sr - ai8[b] * si + re_ref[b, rows, :]
            ni = ar8[b] * si + ai8[b] * sr + im_ref[b, rows, :]
            re_ref[b, rows, :] = nr
            im_ref[b, rows, :] = ni
            out.append((nr, ni))
        return tuple(out)

    z8 = jnp.zeros((S5_SEG, LN), F32)
    fin = lax.fori_loop(0, S, step, tuple((z8, z8) for _ in range(NB)), unroll=4)
    order = range(S5_SEG - 2, -1, -1) if reverse else range(1, S5_SEG)
    for b in range(NB):
        end_r_ref[b], end_i_ref[b] = fin[b]
        pr, pi = ar8[b][:1], ai8[b][:1]
        for _ in range(int(math.log2(S))):
            pr, pi = pr * pr - pi * pi, 2.0 * pr * pi
        first = S5_SEG - 1 if reverse else 0
        c_r_ref[b, pl.ds(first, 1), :] = jnp.zeros((1, LN), F32)
        c_i_ref[b, pl.ds(first, 1), :] = jnp.zeros((1, LN), F32)
        cr, ci = end_r_ref[b, pl.ds(first, 1), :], end_i_ref[b, pl.ds(first, 1), :]
        for i in order:
            c_r_ref[b, pl.ds(i, 1), :] = cr
            c_i_ref[b, pl.ds(i, 1), :] = ci
            er, ei = end_r_ref[b, pl.ds(i, 1), :], end_i_ref[b, pl.ds(i, 1), :]
            cr, ci = er + pr * cr - pi * ci, ei + pr * ci + pi * cr

    def fix(idx, carry):
        rows = tile(S - 1 - idx if reverse else idx)
        out = []
        for b in range(NB):
            pr, pi = carry[b]
            cr, ci = c_r_ref[b], c_i_ref[b]
            re_ref[b, rows, :] += pr * cr - pi * ci
            im_ref[b, rows, :] += pr * ci + pi * cr
            out.append((pr * ar8[b] - pi * ai8[b], pr * ai8[b] + pi * ar8[b]))
        return tuple(out)

    lax.fori_loop(0, S, fix, tuple((ar8[b], ai8[b]) for b in range(NB)), unroll=4)


def _s5_seg_scratch(L, cs, pad):
    NB = cs // 128
    small = [pltpu.VMEM((NB, S5_SEG, 128), F32) for _ in range(4)]
    return [pltpu.VMEM((NB, L + pad, 128), F32), pltpu.VMEM((NB, L + pad, 128), F32)] + small


def _s5_core_fwd(proj, wbr, wbi, a_re, a_im, cbr, cbi):
    L = proj.shape[0]
    nb, ci, cs = wbr.shape
    NB = cs // 128

    def body(u_ref, wbr_ref, wbi_ref, ar_ref, ai_ref, cbr_ref, cbi_ref, y_ref, sr, si, er, ei, cr, cim, up, yp):
        _to_segs(u_ref, up, L)
        u = up[...].astype(BF16)
        for b in range(NB):
            lanes = pl.ds(b * 128, 128)
            sr[b] = jnp.dot(u, wbr_ref[0, :, lanes], preferred_element_type=F32)
            si[b] = jnp.dot(u, wbi_ref[0, :, lanes], preferred_element_type=F32)
        _scan_segs(ar_ref[...], ai_ref[...], sr, si, er, ei, cr, cim, L, 0, False)
        y = jnp.zeros((L, ci), F32)
        for b in range(NB):
            lanes = pl.ds(b * 128, 128)
            y = y + (jnp.dot(sr[b].astype(BF16), cbr_ref[0, lanes, :], preferred_element_type=F32)
                     - jnp.dot(si[b].astype(BF16), cbi_ref[0, lanes, :], preferred_element_type=F32))
        yp[...] = y

        def write(row, val):
            y_ref[pl.ds(row, 16), :] = val

        _from_segs(yp, L, write)

    wspec = pl.BlockSpec((1, ci, cs), lambda j: (j, 0, 0))
    aspec = pl.BlockSpec((1, cs), lambda j: (0, j))
    cspec = pl.BlockSpec((1, cs, ci), lambda j: (j, 0, 0))
    return pl.pallas_call(
        body, name="s5_core_fwd", grid=(nb,),
        in_specs=[pl.BlockSpec((L, ci), lambda j: (0, OFF_US // ci + j)), wspec, wspec, aspec, aspec, cspec, cspec],
        out_specs=pl.BlockSpec((L, ci), lambda j: (0, j)), out_shape=_sds((L, nb * ci)),
        scratch_shapes=_s5_seg_scratch(L, cs, 0) + [pltpu.VMEM((L, ci), F32), pltpu.VMEM((L, ci), F32)],
        compiler_params=_cp(("arbitrary",)),
    )(proj, wbr, wbi, a_re, a_im, cbr, cbi)


def _s5_core_bwd(proj, wbr, wbi, a_re, a_im, cbr, cbi, dyc, du1):
    L = proj.shape[0]
    nb, ci, cs = wbr.shape
    NB = cs // 128
    S = L // S5_SEG
    PAD = S5_SEG

    def body(u_ref, wbr_ref, wbi_ref, ar_ref, ai_ref, cbr_ref, cbi_ref, dy_ref, du1_ref,
             du_ref, dwbr_ref, dwbi_ref, dcbr_ref, dcbi_ref, dar_ref, dai_ref,
             sr, si, er, ei, cr, cim, lr, li, up, dyp, dup):
        tn = (((0,), (0,)), ((), ()))
        nt = (((1,), (1,)), ((), ()))
        _to_segs(u_ref, up, L)
        _to_segs(dy_ref, dyp, L)
        _to_segs(du1_ref, dup, L)
        u = up[...].astype(BF16)
        dy = dyp[...].astype(BF16)
        ar, ai = ar_ref[...], ai_ref[...]
        for b in range(NB):
            lanes = pl.ds(b * 128, 128)
            sr[b, pl.ds(PAD, L), :] = jnp.dot(u, wbr_ref[0, :, lanes], preferred_element_type=F32)
            si[b, pl.ds(PAD, L), :] = jnp.dot(u, wbi_ref[0, :, lanes], preferred_element_type=F32)
        _scan_segs(ar, ai, sr, si, er, ei, cr, cim, L, 1, False)
        for b in range(NB):
            lanes = pl.ds(b * 128, 128)
            sr[b, pl.ds(0, PAD), :] = cr[b]
            si[b, pl.ds(0, PAD), :] = cim[b]
            lr[b] = lax.dot_general(dy, cbr_ref[0, lanes, :], nt, preferred_element_type=F32)
            li[b] = -lax.dot_general(dy, cbi_ref[0, lanes, :], nt, preferred_element_type=F32)
            dcbr_ref[0, lanes, :] = lax.dot_general(sr[b, pl.ds(PAD, L), :].astype(BF16), dy, tn, preferred_element_type=F32)
            dcbi_ref[0, lanes, :] = -lax.dot_general(si[b, pl.ds(PAD, L), :].astype(BF16), dy, tn, preferred_element_type=F32)
        _scan_segs(ar, -ai, lr, li, er, ei, cr, cim, L, 0, True)

        def da_step(j, carry):
            rows = pl.ds(pl.multiple_of(S5_SEG * j, S5_SEG), S5_SEG)
            out = []
            for b in range(NB):
                dar, dai = carry[b]
                pr_, pi_ = sr[b, rows, :], si[b, rows, :]
                gr, gi = lr[b, rows, :], li[b, rows, :]
                out.append((dar + (gr * pr_ + gi * pi_), dai + (gi * pr_ - gr * pi_)))
            return tuple(out)

        z8 = jnp.zeros((S5_SEG, 128), F32)
        acc = lax.fori_loop(0, S, da_step, tuple((z8, z8) for _ in range(NB)), unroll=4)
        du = dup[...]
        for b in range(NB):
            lanes = pl.ds(b * 128, 128)
            dar_ref[:, lanes] = jnp.sum(acc[b][0], axis=0, keepdims=True)
            dai_ref[:, lanes] = jnp.sum(acc[b][1], axis=0, keepdims=True)
            gr, gi = lr[b].astype(BF16), li[b].astype(BF16)
            du = du + (lax.dot_general(gr, wbr_ref[0, :, lanes], nt, preferred_element_type=F32)
                       + lax.dot_general(gi, wbi_ref[0, :, lanes], nt, preferred_element_type=F32))
            dwbr_ref[0, :, lanes] = lax.dot_general(u, gr, tn, preferred_element_type=F32)
            dwbi_ref[0, :, lanes] = lax.dot_general(u, gi, tn, preferred_element_type=F32)
        dup[...] = du

        def write(row, val):
            du_ref[pl.ds(row, 16), :] = val.astype(BF16)

        _from_segs(dup, L, write)

    wspec = pl.BlockSpec((1, ci, cs), lambda j: (j, 0, 0))
    aspec = pl.BlockSpec((1, cs), lambda j: (0, j))
    cspec = pl.BlockSpec((1, cs, ci), lambda j: (j, 0, 0))
    col = pl.BlockSpec((L, ci), lambda j: (0, j))
    return pl.pallas_call(
        body, name="s5_core_bwd", grid=(nb,),
        in_specs=[pl.BlockSpec((L, ci), lambda j: (0, OFF_US // ci + j)), wspec, wspec, aspec, aspec, cspec, cspec, col, col],
        out_specs=[col, wspec, wspec, cspec, cspec, aspec, aspec],
        out_shape=[_sds((L, nb * ci), BF16), _sds(wbr.shape), _sds(wbr.shape), _sds(cbr.shape), _sds(cbr.shape),
                   _sds((1, nb * cs)), _sds((1, nb * cs))],
        scratch_shapes=(_s5_seg_scratch(L, cs, PAD) + [pltpu.VMEM((NB, L, 128), F32), pltpu.VMEM((NB, L, 128), F32)]
                        + [pltpu.VMEM((L, ci), F32) for _ in range(3)]),
        compiler_params=_cp(("arbitrary",)),
    )(proj, wbr, wbi, a_re, a_im, cbr, cbi, dyc, du1)


def _s5_post_math(yc, u, z, d, wg):
    y = yc + d * u
    y1 = jax.nn.gelu(y)
    t = jnp.dot(y1.astype(BF16), wg, preferred_element_type=F32)
    sg = _sigmoid(t)
    return y, y1, sg


def _s5_post_fwd(yc, proj, d, wg):
    L, W = yc.shape
    tm = min(256, L)

    def body(yc_ref, u_ref, z_ref, d_ref, wg_ref, o_ref):
        _, y1, sg = _s5_post_math(yc_ref[...], u_ref[...], z_ref[...], d_ref[...], wg_ref[...])
        o_ref[...] = (y1 * sg * _silu(z_ref[...])).astype(BF16)

    row = pl.BlockSpec((tm, W), lambda i: (i, 0))
    return pl.pallas_call(
        body, name="s5_post_fwd", grid=(L // tm,),
        in_specs=[row, pl.BlockSpec((tm, W), lambda i: (i, OFF_US // W)), pl.BlockSpec((tm, W), lambda i: (i, OFF_ZS // W)),
                  pl.BlockSpec((1, W), lambda i: (0, 0)), pl.BlockSpec((W, W), lambda i: (0, 0))],
        out_specs=row, out_shape=_sds((L, W), BF16), compiler_params=_cp(("parallel",)),
    )(yc, proj, proj, d, wg)


def _s5_post_bwd(yc, proj, d, wg, dout):
    L, W = yc.shape
    tm = min(256, L)

    def body(yc_ref, u_ref, z_ref, d_ref, wg_ref, do_ref, dyc_ref, du_ref, dz_ref, dd_ref, dwg_ref):
        i = pl.program_id(0)
        u, z, d_, wgv = u_ref[...], z_ref[...], d_ref[...], wg_ref[...]
        y, y1, sg = _s5_post_math(yc_ref[...], u, z, d_, wgv)
        dout_ = do_ref[...]
        y2 = y1 * sg
        dy2 = dout_ * _silu(z)
        dz_ref[...] = (dout_ * y2 * _dsilu(z)).astype(BF16)
        dt = (dy2 * y1 * sg * (1.0 - sg)).astype(BF16)
        dy1 = dy2 * sg + lax.dot_general(dt, wgv, (((1,), (1,)), ((), ())), preferred_element_type=F32)
        _, gelu_vjp = jax.vjp(jax.nn.gelu, y)
        dy = gelu_vjp(dy1)[0]
        dyc_ref[...] = dy
        du_ref[...] = dy * d_
        dd_part = jnp.sum(dy * u, axis=0, keepdims=True)
        dwg_part = lax.dot_general(y1.astype(BF16), dt, (((0,), (0,)), ((), ())), preferred_element_type=F32)

        @pl.when(i == 0)
        def _():
            dd_ref[...] = dd_part
            dwg_ref[...] = dwg_part

        @pl.when(i > 0)
        def _():
            dd_ref[...] += dd_part
            dwg_ref[...] += dwg_part

    row = pl.BlockSpec((tm, W), lambda i: (i, 0))
    return pl.pallas_call(
        body, name="s5_post_bwd", grid=(L // tm,),
        in_specs=[row, pl.BlockSpec((tm, W), lambda i: (i, OFF_US // W)), pl.BlockSpec((tm, W), lambda i: (i, OFF_ZS // W)),
                  pl.BlockSpec((1, W), lambda i: (0, 0)), pl.BlockSpec((W, W), lambda i: (0, 0)), row],
        out_specs=[row, row, row, pl.BlockSpec((1, W), lambda i: (0, 0)), pl.BlockSpec((W, W), lambda i: (0, 0))],
        out_shape=[_sds((L, W)), _sds((L, W)), _sds((L, W), BF16), _sds((1, W)), _sds((W, W))],
        compiler_params=_cp(("arbitrary",)),
    )(yc, proj, proj, d, wg, dout)


def _shift_down(x, s):
    if s == 0:
        return x
    rows = lax.broadcasted_iota(jnp.int32, x.shape, 0)
    return jnp.where(rows >= s, pltpu.roll(x, s, 0), 0.0)


def _shift_up(x, s):
    if s == 0:
        return x
    L = x.shape[0]
    rows = lax.broadcasted_iota(jnp.int32, x.shape, 0)
    return jnp.where(rows < L - s, pltpu.roll(x, L - s, 0), 0.0)


def _conv_pre(x, w):
    acc = w[CONV_K - 1:CONV_K, :] * x
    for s in range(1, CONV_K):
        acc = acc + w[CONV_K - 1 - s:CONV_K - s, :] * _shift_down(x, s)
    return acc


def _dn_conv_fwd(proj, conv_w):
    L = proj.shape[0]
    W = DN_HEAD_DIM
    nq = 2 * DN_HEADS

    def body(x_ref, w_ref, o_ref):
        j = pl.program_id(0)
        act = _silu(_conv_pre(x_ref[...], w_ref[...]))
        r = lax.rsqrt(jnp.sum(act * act, axis=-1, keepdims=True) + EPS)
        scale = jnp.where(j < DN_HEADS, DN_HEAD_DIM ** -0.5, 1.0)
        o_ref[...] = jnp.where(j < nq, act * r * scale, act)

    return pl.pallas_call(
        body, name="dn_conv_fwd", grid=(3 * DN_HEADS,),
        in_specs=[pl.BlockSpec((L, W), lambda j: (0, OFF_Q // W + j)), pl.BlockSpec((CONV_K, W), lambda j: (0, j))],
        out_specs=pl.BlockSpec((L, W), lambda j: (0, j)), out_shape=_sds((L, 3 * D_DN)), compiler_params=_cp(("parallel",)),
    )(proj, conv_w)


def _dn_conv_bwd(proj, conv_w, dout):
    L = proj.shape[0]
    W = DN_HEAD_DIM
    nq = 2 * DN_HEADS

    def body(x_ref, w_ref, do_ref, dx_ref, dw_ref):
        j = pl.program_id(0)
        x, w, dout_ = x_ref[...], w_ref[...], do_ref[...]
        pre = _conv_pre(x, w)
        act = _silu(pre)
        r = lax.rsqrt(jnp.sum(act * act, axis=-1, keepdims=True) + EPS)
        scale = jnp.where(j < DN_HEADS, DN_HEAD_DIM ** -0.5, 1.0)
        g = dout_ * scale
        dact_n = r * g - act * (r * r * r) * jnp.sum(g * act, axis=-1, keepdims=True)
        dact = jnp.where(j < nq, dact_n, dout_)
        dpre = dact * _dsilu(pre)
        dx = w[CONV_K - 1:CONV_K, :] * dpre
        for s in range(1, CONV_K):
            dx = dx + w[CONV_K - 1 - s:CONV_K - s, :] * _shift_up(dpre, s)
        dx_ref[...] = dx.astype(BF16)
        for s in range(CONV_K):
            dw_ref[pl.ds(CONV_K - 1 - s, 1), :] = jnp.sum(dpre * _shift_down(x, s), axis=0, keepdims=True)

    col = pl.BlockSpec((L, W), lambda j: (0, j))
    wsp = pl.BlockSpec((CONV_K, W), lambda j: (0, j))
    return pl.pallas_call(
        body, name="dn_conv_bwd", grid=(3 * DN_HEADS,),
        in_specs=[pl.BlockSpec((L, W), lambda j: (0, OFF_Q // W + j)), wsp, col], out_specs=[col, wsp],
        out_shape=[_sds((L, 3 * D_DN), BF16), _sds((CONV_K, 3 * D_DN))], compiler_params=_cp(("parallel",)),
    )(proj, conv_w, dout)


def _softplus(x):
    return jnp.maximum(x, 0.0) + jnp.log(1.0 + jnp.exp(-jnp.abs(x)))


def _dn_gates_fwd(proj, alog, dtb):
    L = proj.shape[0]
    W = 128

    def body(p_ref, al_ref, db_ref, o_ref):
        p = p_ref[...]
        lane = lax.broadcasted_iota(jnp.int32, p.shape, 1)
        g = -jnp.exp(al_ref[...]) * _softplus(p + db_ref[...])
        o_ref[...] = jnp.where(lane < DN_HEADS, _sigmoid(p), jnp.where(lane < 2 * DN_HEADS, g, 0.0))

    return pl.pallas_call(
        body, name="dn_gates_fwd", grid=(1,),
        in_specs=[pl.BlockSpec((L, W), lambda i: (0, OFF_B // W)), pl.BlockSpec((1, W), lambda i: (0, 0)),
                  pl.BlockSpec((1, W), lambda i: (0, 0))],
        out_specs=pl.BlockSpec((L, W), lambda i: (0, 0)), out_shape=_sds((L, W)), compiler_params=_cp(("arbitrary",)),
    )(proj, alog, dtb)


def _dn_gates_bwd(proj, alog, dtb, dgates):
    L = proj.shape[0]
    W = 128

    def body(p_ref, al_ref, db_ref, dg_ref, dp_ref, dal_ref, ddb_ref):
        p, dg = p_ref[...], dg_ref[...]
        lane = lax.broadcasted_iota(jnp.int32, p.shape, 1)
        is_g = jnp.logical_and(lane >= DN_HEADS, lane < 2 * DN_HEADS)
        beta = _sigmoid(p)
        na = -jnp.exp(al_ref[...])
        xs = p + db_ref[...]
        dsp = dg * na * _sigmoid(xs)
        dp_ref[...] = jnp.where(lane < DN_HEADS, dg * beta * (1.0 - beta), jnp.where(is_g, dsp, 0.0)).astype(BF16)
        dal_ref[...] = jnp.sum(jnp.where(is_g, dg * na * _softplus(xs), 0.0), axis=0, keepdims=True)
        ddb_ref[...] = jnp.sum(jnp.where(is_g, dsp, 0.0), axis=0, keepdims=True)

    one = pl.BlockSpec((1, W), lambda i: (0, 0))
    full = pl.BlockSpec((L, W), lambda i: (0, 0))
    return pl.pallas_call(
        body, name="dn_gates_bwd", grid=(1,),
        in_specs=[pl.BlockSpec((L, W), lambda i: (0, OFF_B // W)), one, one, full], out_specs=[full, one, one],
        out_shape=[_sds((L, W), BF16), _sds((1, W)), _sds((1, W))], compiler_params=_cp(("arbitrary",)),
    )(proj, alog, dtb, dgates)


def _bdot(a, b, dims):
    return lax.dot_general(a.astype(BF16), b.astype(BF16), (dims, ((), ())), preferred_element_type=F32)


_NN, _NT, _TN = ((1,), (0,)), ((1,), (1,)), ((0,), (0,))


def _dot3(a, b, dims):
    ah, bh = a.astype(BF16), b.astype(BF16)
    al, bl = (a - ah.astype(F32)).astype(BF16), (b - bh.astype(F32)).astype(BF16)
    (ca,), (cb,) = dims
    a3 = jnp.concatenate([ah, ah, al], axis=ca)
    b3 = jnp.concatenate([bh, bl, bh], axis=cb)
    return lax.dot_general(a3, b3, (dims, ((), ())), preferred_element_type=F32)


def _mm_family(raw):
    nn = jax.custom_vjp(lambda a, b: raw(a, b, _NN))
    nt = jax.custom_vjp(lambda a, b: raw(a, b, _NT))
    tn = jax.custom_vjp(lambda a, b: raw(a, b, _TN))
    nn.defvjp(lambda a, b: (raw(a, b, _NN), (a, b)), lambda r, g: (raw(g, r[1], _NT), raw(r[0], g, _TN)))
    nt.defvjp(lambda a, b: (raw(a, b, _NT), (a, b)), lambda r, g: (raw(g, r[1], _NN), raw(g, r[0], _TN)))
    tn.defvjp(lambda a, b: (raw(a, b, _TN), (a, b)), lambda r, g: (raw(r[1], g, _NT), raw(r[0], g, _NN)))
    return nn, nt, tn


_mm_nn, _mm_nt, _mm_tn = _mm_family(_bdot)
_m3_nn, _m3_nt, _m3_tn = _mm_family(_dot3)


def _tri_apply(x, upper):
    C = x.shape[0]
    ii = lax.broadcasted_iota(jnp.int32, (C, 3 * C), 0)
    jj = lax.broadcasted_iota(jnp.int32, (C, 3 * C), 1) % C
    mat = ((ii <= jj) if upper else (ii >= jj)).astype(BF16)
    hi = x.astype(BF16)
    r = x - hi.astype(F32)
    mid = r.astype(BF16)
    lo = (r - mid.astype(F32)).astype(BF16)
    return jnp.dot(mat, jnp.concatenate([hi, mid, lo], axis=0), preferred_element_type=F32)


_cumsum_rows = jax.custom_vjp(lambda x: _tri_apply(x, False))
_cumsum_rows.defvjp(lambda x: (_tri_apply(x, False), None), lambda _, g: (_tri_apply(g, True),))


def _uli(a_s):
    C = a_s[0].shape[0]
    ii = lax.broadcasted_iota(jnp.int32, (C, C), 0)
    jj = lax.broadcasted_iota(jnp.int32, (C, C), 1)
    eye = jnp.where(ii == jj, 1.0, 0.0)
    ts = [eye - a for a in a_s]
    ms = list(a_s)
    for _ in range(int(math.log2(C)) - 1):
        ms = [_dot3(m, m, _NN) for m in ms]
        ts = [t + _dot3(t, m, _NN) for t, m in zip(ts, ms)]
    return tuple(ts)


def _uli_bwd(ts, gs):
    xs = [_dot3(t, g, _TN) for t, g in zip(ts, gs)]
    return (tuple(-_dot3(x, t, _NT) for x, t in zip(xs, ts)),)


_unit_lower_inverse = jax.custom_vjp(_uli)
_unit_lower_inverse.defvjp(lambda a_s: (lambda ts: (ts, ts))(_uli(a_s)), _uli_bwd)


def _prep_math(qs, ks, vs, gcols, bcols):
    n = len(qs)
    C, dv = vs[0].shape
    ii = lax.broadcasted_iota(jnp.int32, (C, C), 0)
    jj = lax.broadcasted_iota(jnp.int32, (C, C), 1)
    causal = ii >= jj
    strict = ii > jj
    sf = strict.astype(F32)
    ones = jnp.ones((C, dv), F32)
    dms = [_cumsum_rows(g * sf) for g in gcols]
    gcbs = [_cumsum_rows(g * ones) for g in gcols]
    kks = [_mm_nt(k, k) for k in ks]
    qks = [_mm_nt(q, k) for q, k in zip(qs, ks)]
    decays = [jnp.where(causal, jnp.exp(jnp.where(causal, dm, 0.0)), 0.0) for dm in dms]
    glasts = [jnp.sum(g * ones, axis=0, keepdims=True) for g in gcols]
    egs = [jnp.exp(gcb) for gcb in gcbs]
    ts = _unit_lower_inverse(tuple(jnp.where(strict, b * kk * dc, 0.0) for b, kk, dc in zip(bcols, kks, decays)))
    us = [_m3_nn(t, v * b) for t, v, b in zip(ts, vs, bcols)]
    ws = [_m3_nn(t, k * b * eg) for t, k, b, eg in zip(ts, ks, bcols, egs)]
    return tuple((us[i], ws[i], qs[i] * egs[i], ks[i] * jnp.exp(glasts[i] - gcbs[i]), qks[i] * decays[i],
                  jnp.exp(glasts[i])) for i in range(n))


def _gate_cols(gates, h):
    lane = lax.broadcasted_iota(jnp.int32, gates.shape, 1)
    bcol = jnp.sum(jnp.where(lane == h, gates, 0.0), axis=1, keepdims=True)
    gcol = jnp.sum(jnp.where(lane == h + DN_HEADS, gates, 0.0), axis=1, keepdims=True)
    return gcol, bcol


DN_HB = 8


def _dn_prep_fwd(qkv, gates):
    L = qkv.shape[0]
    N, H, d, HB = L // CHUNK, DN_HEADS, DN_HEAD_DIM, DN_HB

    def body(q_ref, k_ref, v_ref, g_ref, u_ref, w_ref, qd_ref, kd_ref, qk_ref, egl_ref):
        h0 = pl.program_id(1) * HB
        gates_ = g_ref[...]
        lanes_of = [pl.ds(i * d, d) for i in range(HB)]
        cols = [_gate_cols(gates_, h0 + i) for i in range(HB)]
        outs = _prep_math([q_ref[:, l] for l in lanes_of], [k_ref[:, l] for l in lanes_of], [v_ref[:, l] for l in lanes_of],
                          [c[0] for c in cols], [c[1] for c in cols])
        for i in range(HB):
            lanes = lanes_of[i]
            u, w, qd, kd, qk, egl = outs[i]
            u_ref[:, lanes] = u
            w_ref[:, lanes] = w
            qd_ref[:, lanes] = qd
            kd_ref[:, lanes] = kd
            qk_ref[0, i] = qk
            egl_ref[0, i] = jnp.broadcast_to(egl, (8, d))

    blk = lambda off: pl.BlockSpec((CHUNK, HB * d), lambda n, j: (n, off // HB + j))
    cc = pl.BlockSpec((1, HB, CHUNK, CHUNK), lambda n, j: (n, j, 0, 0))
    ee = pl.BlockSpec((1, HB, 8, d), lambda n, j: (n, j, 0, 0))
    big = _sds((L, D_DN))
    return pl.pallas_call(
        body, name="dn_prep_fwd", grid=(N, H // HB),
        in_specs=[blk(0), blk(H), blk(2 * H), pl.BlockSpec((CHUNK, 128), lambda n, j: (n, 0))],
        out_specs=[blk(0), blk(0), blk(0), blk(0), cc, ee],
        out_shape=[big, big, big, big, _sds((N, H, CHUNK, CHUNK)), _sds((N, H, 8, d))],
        compiler_params=_cp(("parallel", "parallel")),
    )(qkv, qkv, qkv, gates)


def _dn_scan_fwd(u, w, qd, kd, qk, egl):
    L = u.shape[0]
    N, H, d, HB = L // CHUNK, DN_HEADS, DN_HEAD_DIM, DN_HB

    def body(u_ref, w_ref, qd_ref, kd_ref, qk_ref, egl_ref, o_ref, st_ref, s_ref):
        n, h0 = pl.program_id(0), pl.program_id(1) * HB

        @pl.when(n == 0)
        def _():
            for i in range(HB):
                s_ref[h0 + i] = jnp.zeros((d, d), F32)

        hs = range(HB)
        ln = [pl.ds(i * d, d) for i in hs]
        st = [s_ref[h0 + i] for i in hs]
        ws = [_bdot(w_ref[:, ln[i]], st[i], _NN) for i in hs]
        qs = [_bdot(qd_ref[:, ln[i]], st[i], _NN) for i in hs]
        vn = [u_ref[:, ln[i]] - ws[i] for i in hs]
        qv = [_bdot(qk_ref[0, i], vn[i], _NN) for i in hs]
        kv = [_bdot(kd_ref[:, ln[i]], vn[i], _TN) for i in hs]
        for i in hs:
            st_ref[0, i] = st[i]
            o_ref[:, ln[i]] = qs[i] + qv[i]
            s_ref[h0 + i] = st[i] * egl_ref[0, i, pl.ds(0, 1), :] + kv[i]

    blk = pl.BlockSpec((CHUNK, HB * d), lambda n, j: (n, j))
    cc = pl.BlockSpec((1, HB, CHUNK, CHUNK), lambda n, j: (n, j, 0, 0))
    ee = pl.BlockSpec((1, HB, 8, d), lambda n, j: (n, j, 0, 0))
    return pl.pallas_call(
        body, name="dn_scan_fwd", grid=(N, H // HB), in_specs=[blk, blk, blk, blk, cc, ee],
        out_specs=[blk, pl.BlockSpec((1, HB, d, d), lambda n, j: (n, j, 0, 0))],
        out_shape=[_sds((L, D_DN)), _sds((N, H, d, d))], scratch_shapes=[pltpu.VMEM((H, d, d), F32)],
        compiler_params=_cp(("arbitrary", "arbitrary")),
    )(u, w, qd, kd, qk, egl)


def _dn_scan_bwd(u, w, qd, kd, qk, egl, states, do):
    L = u.shape[0]
    N, H, d, HB = L // CHUNK, DN_HEADS, DN_HEAD_DIM, DN_HB

    def body(u_ref, w_ref, qd_ref, kd_ref, qk_ref, egl_ref, st_ref, do_ref,
             du_ref, dw_ref, dqd_ref, dkd_ref, dqk_ref, degl_ref, ds_ref):
        n, h0 = pl.program_id(0), pl.program_id(1) * HB

        @pl.when(n == 0)
        def _():
            for i in range(HB):
                ds_ref[h0 + i] = jnp.zeros((d, d), F32)

        hs = range(HB)
        ln = [pl.ds(i * d, d) for i in hs]
        st = [st_ref[0, i] for i in hs]
        dsn = [ds_ref[h0 + i] for i in hs]
        do_ = [do_ref[:, ln[i]] for i in hs]
        ws = [_bdot(w_ref[:, ln[i]], st[i], _NN) for i in hs]
        d1 = [_bdot(qk_ref[0, i], do_[i], _TN) for i in hs]
        d2 = [_bdot(kd_ref[:, ln[i]], dsn[i], _NN) for i in hs]
        dqd = [_bdot(do_[i], st[i], _NT) for i in hs]
        qdo = [_bdot(qd_ref[:, ln[i]], do_[i], _TN) for i in hs]
        vn = [u_ref[:, ln[i]] - ws[i] for i in hs]
        dvn = [d1[i] + d2[i] for i in hs]
        dw = [_bdot(dvn[i], st[i], _NT) for i in hs]
        dkd = [_bdot(vn[i], dsn[i], _NT) for i in hs]
        dqk = [_bdot(do_[i], vn[i], _NT) for i in hs]
        wdv = [_bdot(w_ref[:, ln[i]], dvn[i], _TN) for i in hs]
        for i in hs:
            du_ref[:, ln[i]] = dvn[i]
            dw_ref[:, ln[i]] = -dw[i]
            dqd_ref[:, ln[i]] = dqd[i]
            dkd_ref[:, ln[i]] = dkd[i]
            dqk_ref[0, i] = dqk[i]
            degl_ref[0, i] = jnp.broadcast_to(jnp.sum(dsn[i] * st[i], keepdims=True), (8, d))
            ds_ref[h0 + i] = (qdo[i] - wdv[i]) + dsn[i] * egl_ref[0, i, pl.ds(0, 1), :]

    blk = pl.BlockSpec((CHUNK, HB * d), lambda n, j: (N - 1 - n, j))
    cc = pl.BlockSpec((1, HB, CHUNK, CHUNK), lambda n, j: (N - 1 - n, j, 0, 0))
    ee = pl.BlockSpec((1, HB, 8, d), lambda n, j: (N - 1 - n, j, 0, 0))
    ss = pl.BlockSpec((1, HB, d, d), lambda n, j: (N - 1 - n, j, 0, 0))
    big = _sds((L, D_DN))
    return pl.pallas_call(
        body, name="dn_scan_bwd", grid=(N, H // HB), in_specs=[blk, blk, blk, blk, cc, ee, ss, blk],
        out_specs=[blk, blk, blk, blk, cc, ee],
        out_shape=[big, big, big, big, _sds((N, H, CHUNK, CHUNK)), _sds((N, H, 8, d))],
        scratch_shapes=[pltpu.VMEM((H, d, d), F32)], compiler_params=_cp(("arbitrary", "arbitrary")),
    )(u, w, qd, kd, qk, egl, states, do)


def _dn_prep_bwd(qkv, gates, du, dw, dqd, dkd, dqk, degl):
    L = qkv.shape[0]
    N, H, d, HB = L // CHUNK, DN_HEADS, DN_HEAD_DIM, DN_HB

    def body(q_ref, k_ref, v_ref, g_ref, du_ref, dw_ref, dqd_ref, dkd_ref, dqk_ref, degl_ref, dq_ref, dk_ref, dv_ref, dg_ref):
        j = pl.program_id(1)
        h0 = j * HB
        gates_ = g_ref[...]
        lane = lax.broadcasted_iota(jnp.int32, gates_.shape, 1)
        lane1 = lax.broadcasted_iota(jnp.int32, (1, d), 1)
        part = jnp.zeros(gates_.shape, F32)
        lanes_of = [pl.ds(i * d, d) for i in range(HB)]
        cols = [_gate_cols(gates_, h0 + i) for i in range(HB)]
        _, f = jax.vjp(_prep_math, [q_ref[:, l] for l in lanes_of], [k_ref[:, l] for l in lanes_of],
                       [v_ref[:, l] for l in lanes_of], [c[0] for c in cols], [c[1] for c in cols])
        cots = tuple((du_ref[:, l], dw_ref[:, l], dqd_ref[:, l], dkd_ref[:, l], dqk_ref[0, i],
                      jnp.where(lane1 == 0, degl_ref[0, i, pl.ds(0, 1), :], 0.0)) for i, l in enumerate(lanes_of))
        dqs, dks, dvs, dgcs, dbcs = f(cots)
        for i in range(HB):
            lanes = lanes_of[i]
            dq_ref[:, lanes] = dqs[i]
            dk_ref[:, lanes] = dks[i]
            dv_ref[:, lanes] = dvs[i]
            part = part + jnp.where(lane == h0 + i, dbcs[i], 0.0) + jnp.where(lane == h0 + i + DN_HEADS, dgcs[i], 0.0)

        @pl.when(j == 0)
        def _():
            dg_ref[...] = part

        @pl.when(j > 0)
        def _():
            dg_ref[...] += part

    blk = lambda off: pl.BlockSpec((CHUNK, HB * d), lambda n, j: (n, off // HB + j))
    gsp = pl.BlockSpec((CHUNK, 128), lambda n, j: (n, 0))
    cc = pl.BlockSpec((1, HB, CHUNK, CHUNK), lambda n, j: (n, j, 0, 0))
    ee = pl.BlockSpec((1, HB, 8, d), lambda n, j: (n, j, 0, 0))
    big = _sds((L, D_DN))
    return pl.pallas_call(
        body, name="dn_prep_bwd", grid=(N, H // HB),
        in_specs=[blk(0), blk(H), blk(2 * H), gsp, blk(0), blk(0), blk(0), blk(0), cc, ee],
        out_specs=[blk(0), blk(0), blk(0), gsp], out_shape=[big, big, big, _sds((L, 128))],
        compiler_params=_cp(("parallel", "arbitrary")),
    )(qkv, qkv, qkv, gates, du, dw, dqd, dkd, dqk, degl)


def _dn_post_fwd(o, proj, nw):
    L = o.shape[0]
    d = DN_HEAD_DIM
    tm = min(512, L)

    def body(o_ref, z_ref, w_ref, y_ref):
        ov = o_ref[...]
        r = lax.rsqrt(jnp.mean(ov * ov, axis=-1, keepdims=True) + EPS)
        y_ref[...] = (ov * r * w_ref[...] * _silu(z_ref[...])).astype(BF16)

    blk = pl.BlockSpec((tm, d), lambda i, h: (i, h))
    return pl.pallas_call(
        body, name="dn_post_fwd", grid=(L // tm, DN_HEADS),
        in_specs=[blk, pl.BlockSpec((tm, d), lambda i, h: (i, OFF_ZD // d + h)), pl.BlockSpec((1, d), lambda i, h: (0, 0))],
        out_specs=blk, out_shape=_sds((L, D_DN), BF16), compiler_params=_cp(("parallel", "parallel")),
    )(o, proj, nw)


def _dn_post_bwd(o, proj, nw, dy):
    L = o.shape[0]
    d = DN_HEAD_DIM
    tm = min(512, L)

    def body(o_ref, z_ref, w_ref, dy_ref, do_ref, dz_ref, dw_ref):
        first = jnp.logical_and(pl.program_id(0) == 0, pl.program_id(1) == 0)
        ov, z, w, dyv = o_ref[...], z_ref[...], w_ref[...], dy_ref[...]
        r = lax.rsqrt(jnp.mean(ov * ov, axis=-1, keepdims=True) + EPS)
        xn = ov * r
        dz_ref[...] = (dyv * xn * w * _dsilu(z)).astype(BF16)
        dn = dyv * _silu(z)
        t = dn * w
        do_ref[...] = r * t - ov * (r * r * r) * jnp.mean(t * ov, axis=-1, keepdims=True)
        part = jnp.sum(dn * xn, axis=0, keepdims=True)

        @pl.when(first)
        def _():
            dw_ref[...] = part

        @pl.when(jnp.logical_not(first))
        def _():
            dw_ref[...] += part

    blk = pl.BlockSpec((tm, d), lambda i, h: (i, h))
    one = pl.BlockSpec((1, d), lambda i, h: (0, 0))
    return pl.pallas_call(
        body, name="dn_post_bwd", grid=(L // tm, DN_HEADS),
        in_specs=[blk, pl.BlockSpec((tm, d), lambda i, h: (i, OFF_ZD // d + h)), one, blk], out_specs=[blk, blk, one],
        out_shape=[_sds((L, D_DN)), _sds((L, D_DN), BF16), _sds((1, d))], compiler_params=_cp(("arbitrary", "arbitrary")),
    )(o, proj, nw, dy)


def _mix_fwd(s5o, dno, w_su, w_du, proj):
    L, K = s5o.shape
    N = w_su.shape[1]
    tm, tn = min(512, L), 512

    def body(a1, a2, b1, b2, gs, gd, ys_ref, yd_ref, mx_ref):
        ys = jnp.dot(a1[...], b1[...], preferred_element_type=F32)
        yd = jnp.dot(a2[...], b2[...], preferred_element_type=F32)
        ys_ref[...] = ys
        yd_ref[...] = yd
        mx_ref[...] = (_sigmoid(gs[...]) * ys + _sigmoid(gd[...]) * yd).astype(BF16)

    a = pl.BlockSpec((tm, K), lambda i, j: (i, 0))
    b = pl.BlockSpec((K, tn), lambda i, j: (0, j))
    o = pl.BlockSpec((tm, tn), lambda i, j: (i, j))
    return pl.pallas_call(
        body, name="mix_fwd", grid=(L // tm, N // tn),
        in_specs=[a, a, b, b, pl.BlockSpec((tm, tn), lambda i, j: (i, OFF_GS // tn + j)),
                  pl.BlockSpec((tm, tn), lambda i, j: (i, OFF_GD // tn + j))],
        out_specs=[o, o, o], out_shape=[_sds((L, N)), _sds((L, N)), _sds((L, N), BF16)],
        compiler_params=_cp(("parallel", "parallel")),
    )(s5o, dno, w_su, w_du, proj, proj)


def _mix_bwd(dx2b, w_out, proj, ys, yd):
    L, K = dx2b.shape
    N = w_out.shape[0]
    tm, tn = min(512, L), 512

    def body(a, b, gs, gd, ys_ref, yd_ref, dgs_ref, dgd_ref, dys_ref, dyd_ref):
        dm = lax.dot_general(a[...], b[...], (((1,), (1,)), ((), ())), preferred_element_type=F32)
        ss, sd = _sigmoid(gs[...]), _sigmoid(gd[...])
        dys_ref[...] = (dm * ss).astype(BF16)
        dyd_ref[...] = (dm * sd).astype(BF16)
        dgs_ref[...] = (dm * ys_ref[...] * ss * (1.0 - ss)).astype(BF16)
        dgd_ref[...] = (dm * yd_ref[...] * sd * (1.0 - sd)).astype(BF16)

    o = pl.BlockSpec((tm, tn), lambda i, j: (i, j))
    return pl.pallas_call(
        body, name="mix_bwd", grid=(L // tm, N // tn),
        in_specs=[pl.BlockSpec((tm, K), lambda i, j: (i, 0)), pl.BlockSpec((tn, K), lambda i, j: (j, 0)),
                  pl.BlockSpec((tm, tn), lambda i, j: (i, OFF_GS // tn + j)),
                  pl.BlockSpec((tm, tn), lambda i, j: (i, OFF_GD // tn + j)), o, o],
        out_specs=[o, o, o, o], out_shape=[_sds((L, N), BF16)] * 4, compiler_params=_cp(("parallel", "parallel")),
    )(dx2b, w_out, proj, proj, ys, yd)


def _final(mixed, w_out, x, tgt, fw):
    L, D = x.shape
    tm = min(256, L)

    def body(a_ref, b_ref, x_ref, t_ref, w_ref, dx_ref, dxb_ref, loss_ref, dw_ref):
        i = pl.program_id(0)
        x2 = x_ref[...] + jnp.dot(a_ref[...], b_ref[...], preferred_element_type=F32)
        w = w_ref[...]
        r = lax.rsqrt(jnp.mean(x2 * x2, axis=-1, keepdims=True) + EPS)
        xn = x2 * r
        e = xn * w - t_ref[...]
        lpart = 0.5 * jnp.sum(jnp.mean(e * e, axis=-1, keepdims=True), axis=0, keepdims=True)
        dy = e * (1.0 / D)
        t = dy * w
        dx2 = r * t - x2 * (r * r * r) * jnp.mean(t * x2, axis=-1, keepdims=True)
        dx_ref[...] = dx2
        dxb_ref[...] = dx2.astype(BF16)
        dwp = jnp.sum(dy * xn, axis=0, keepdims=True)
        lrow = jnp.broadcast_to(lpart, loss_ref.shape)

        @pl.when(i == 0)
        def _():
            loss_ref[...] = lrow
            dw_ref[...] = dwp

        @pl.when(i > 0)
        def _():
            loss_ref[...] += lrow
            dw_ref[...] += dwp

    row = pl.BlockSpec((tm, D), lambda i: (i, 0))
    one = pl.BlockSpec((1, D), lambda i: (0, 0))
    return pl.pallas_call(
        body, name="final", grid=(L // tm,),
        in_specs=[row, pl.BlockSpec((D, D), lambda i: (0, 0)), row, row, one],
        out_specs=[row, row, pl.BlockSpec((1, 128), lambda i: (0, 0)), one],
        out_shape=[_sds((L, D)), _sds((L, D), BF16), _sds((1, 128)), _sds((1, D))], compiler_params=_cp(("arbitrary",)),
    )(mixed, w_out, x, tgt, fw)


def _block_diag(t):
    J, g, a, b = t.shape
    eye = jnp.eye(g, dtype=t.dtype)
    return (t[:, :, :, None, :] * eye[None, :, None, :, None]).reshape(J, g * a, g * b)


def _block_diag_take(m, g):
    J, ga, gb = m.shape
    a, b = ga // g, gb // g
    m5 = m.reshape(J, g, a, g, b)
    idx = jnp.arange(g)
    return m5[:, idx, :, idx, :].transpose(1, 0, 2, 3)


def _local_step(x, tgt, ln_w, w_perm, lam_re, lam_im, log_step, b_re, b_im, c_re, c_im, s5_d, w_glu, w_su,
                conv_w, a_log, dt_bias, norm_w, w_du, w_out, fw):
    G, P, gb = S5_GROUPS, S5_STATE, S5_GROUPS // S5_BLOCKS
    h, rstd = _ln_fwd(x, ln_w)
    proj = _mm(h, w_perm, name="in_proj", tm=1024, tn=1152)

    b_re2, b_im2 = b_re.reshape(G, P * S5_GROUP), b_im.reshape(G, P * S5_GROUP)
    ls2 = log_step.reshape(G, 1)
    abar_re, abar_im, bb_re, bb_im = _s5_param_fwd(lam_re, lam_im, ls2, b_re2, b_im2)

    def to_wb(bb):
        return _block_diag(bb.reshape(S5_BLOCKS, gb, P, S5_GROUP).transpose(0, 1, 3, 2)).astype(BF16)

    def to_cb(cc):
        return _block_diag(cc.reshape(S5_BLOCKS, gb, S5_GROUP, P).transpose(0, 1, 3, 2)).astype(BF16)

    wbr, wbi, cbr, cbi = to_wb(bb_re), to_wb(bb_im), to_cb(c_re), to_cb(c_im)
    a_re_row, a_im_row = abar_re.reshape(1, G * P), abar_im.reshape(1, G * P)
    yc = _s5_core_fwd(proj, wbr, wbi, a_re_row, a_im_row, cbr, cbi)
    s5o = _s5_post_fwd(yc, proj, s5_d, w_glu)

    pad = lambda v: jnp.pad(v, ((0, 0), (DN_HEADS, 128 - 2 * DN_HEADS)))
    alog_row, dtb_row = pad(a_log), pad(dt_bias)
    qkv = _dn_conv_fwd(proj, conv_w)
    gates = _dn_gates_fwd(proj, alog_row, dtb_row)
    prep = _dn_prep_fwd(qkv, gates)
    o_dn, states = _dn_scan_fwd(*prep)
    dno = _dn_post_fwd(o_dn, proj, norm_w)

    ys, yd, mixed = _mix_fwd(s5o, dno, w_su, w_du, proj)
    dx2, dx2b, loss_row, d_fw = _final(mixed, w_out, x, tgt, fw)
    d_w_out = _mm(mixed, dx2b, ta=True, name="d_w_out")
    dgs, dgd, dys, dyd = _mix_bwd(dx2b, w_out, proj, ys, yd)
    d_w_su = _mm(s5o, dys, ta=True, name="d_w_su", shard_out=True)
    d_w_du = _mm(dno, dyd, ta=True, name="d_w_du", shard_out=True)
    ds5o = _mm(dys, w_su, tb=True, name="d_s5o")
    ddno = _mm(dyd, w_du, tb=True, name="d_dno")

    dyc, du1, dz_s, d_s5d, d_w_glu = _s5_post_bwd(yc, proj, s5_d, w_glu, ds5o)
    du, dwbr, dwbi, dcbr, dcbi, dar, dai = _s5_core_bwd(proj, wbr, wbi, a_re_row, a_im_row, cbr, cbi, dyc, du1)

    def from_wb(dwb):
        return _block_diag_take(dwb, gb).transpose(0, 1, 3, 2).reshape(G, P * S5_GROUP)

    def from_cb(dcb):
        return _block_diag_take(dcb, gb).transpose(0, 1, 3, 2).reshape(G, S5_GROUP, P)

    d_lam_re, d_lam_im, d_ls, d_b_re, d_b_im = _s5_param_bwd(
        lam_re, lam_im, ls2, b_re2, b_im2, dar.reshape(G, P), dai.reshape(G, P), from_wb(dwbr), from_wb(dwbi))

    do_dn, dz_d, d_norm_w = _dn_post_bwd(o_dn, proj, norm_w, ddno)
    dq, dk, dv, dgates = _dn_prep_bwd(qkv, gates, *_dn_scan_bwd(*prep, states, do_dn))
    dqkv, d_conv = _dn_conv_bwd(proj, conv_w, jnp.concatenate([dq, dk, dv], axis=1))
    dpb, d_alog_row, d_dtb_row = _dn_gates_bwd(proj, alog_row, dtb_row, dgates)

    dproj = jnp.concatenate([du, dz_s, dqkv, dz_d, dgs, dgd, dpb], axis=1)
    d_w_perm = _mm(h, dproj, ta=True, name="d_w_in", tm=1024, tn=1152)
    dh = _mm(dproj, w_perm, tb=True, name="d_h", tm=2048, tn=1024, tk=1152)
    grad_x, d_ln_w = _ln_bwd(x, rstd, ln_w, dh, dx2)

    grads = dict(
        ln_w=d_ln_w, w_perm=d_w_perm, s5_lam_re=d_lam_re, s5_lam_im=d_lam_im, s5_log_step=d_ls.reshape(1, G),
        s5_b_re=d_b_re.reshape(G, P, S5_GROUP), s5_b_im=d_b_im.reshape(G, P, S5_GROUP),
        s5_c_re=from_cb(dcbr), s5_c_im=from_cb(dcbi), s5_d=d_s5d, s5_w_glu=d_w_glu, s5_w_up=d_w_su,
        dn_conv_w=d_conv, dn_a_log=d_alog_row[:, DN_HEADS:2 * DN_HEADS], dn_dt_bias=d_dtb_row[:, DN_HEADS:2 * DN_HEADS],
        dn_norm_w=d_norm_w, dn_w_up=d_w_du, w_out=d_w_out, final_norm_w=d_fw)
    return loss_row, grad_x, grads


def _place():
    x, y, c = lax.axis_index("x"), lax.axis_index("y"), lax.axis_index("c")
    return x, y, c


def _remote(src, dst, send_sem, recv_sem, to):
    return pltpu.make_async_remote_copy(src_ref=src, dst_ref=dst, send_sem=send_sem, recv_sem=recv_sem,
                                        device_id=to, device_id_type=MESH)


def _gather_weights(shards, conv_w):
    na = len(shards)

    def body(*refs):
        ins, conv_in = refs[:na], refs[na]
        outs, conv_out = refs[na + 1:2 * na + 1], refs[2 * na + 1]
        send_sems, recv_sems, local_sems = refs[2 * na + 2:]
        x, y, c = _place()
        me = 2 * x + y
        sibling = (x, y, 1 - c)
        chips = [(1 - x, y), (x, 1 - y), (1 - x, 1 - y)]

        def part(ref, chip, half, a):
            r2 = shards[a].shape[0] // 2
            return ref.at[chip, pl.ds(half * r2, r2)]

        own = [pltpu.make_async_copy(ins[a], outs[a].at[me], local_sems.at[a]) for a in range(na)]
        own.append(pltpu.make_async_copy(conv_in, conv_out.at[me], local_sems.at[na]))
        for cp in own:
            cp.start()
        sends = []
        for a in range(na):
            r2 = shards[a].shape[0] // 2
            for j, (px, py) in enumerate(chips):
                k = 6 * a + j
                sends.append(_remote(ins[a].at[pl.ds(c * r2, r2)], part(outs[a], me, c, a), send_sems.at[k], recv_sems.at[k],
                                     (px, py, c)))
        for j, (px, py) in enumerate(chips):
            k = 6 * na + j
            sends.append(_remote(conv_in, conv_out.at[me], send_sems.at[k], recv_sems.at[k], (px, py, c)))
        for cp in sends:
            cp.start()
        for a in range(na):
            for j, (px, py) in enumerate(chips):
                src = 2 * px + py
                k = 6 * a + j
                landed = part(outs[a], src, c, a)
                _remote(landed, landed, send_sems.at[k], recv_sems.at[k], (px, py, c)).wait_recv()
                fwd = _remote(landed, landed, send_sems.at[k + 3], recv_sems.at[k + 3], sibling)
                fwd.start()
                sends.append(fwd)
        for a in range(na):
            for j, (px, py) in enumerate(chips):
                other = part(outs[a], 2 * px + py, 1 - c, a)
                _remote(other, other, send_sems.at[6 * a + 3 + j], recv_sems.at[6 * a + 3 + j], sibling).wait_recv()
        for j, (px, py) in enumerate(chips):
            k = 6 * na + j
            _remote(conv_in, conv_out.at[2 * px + py], send_sems.at[k], recv_sems.at[k], (px, py, c)).wait_recv()
        for cp in sends:
            cp.wait_send()
        for cp in own:
            cp.wait()

    nsem = 6 * na + 3
    return pl.pallas_call(
        body, name="gather_weights", in_specs=[ANY] * (na + 1), out_specs=[ANY] * (na + 1),
        out_shape=[_sds((N_CHIPS,) + s.shape, s.dtype) for s in shards] + [_sds((N_CHIPS,) + conv_w.shape, conv_w.dtype)],
        scratch_shapes=[pltpu.SemaphoreType.DMA((nsem,)), pltpu.SemaphoreType.DMA((nsem,)), pltpu.SemaphoreType.DMA((na + 1,))],
    )(*shards, conv_w)


def _swap_halves(gxs):
    na = len(gxs)

    def body(*refs):
        ins, outs = refs[:na], refs[na:2 * na]
        send_sems, recv_sems = refs[2 * na:]
        x, y, c = _place()
        cps = [_remote(ins[a].at[pl.ds(0, N_CHIPS), pl.ds(1 - c, 1)], outs[a], send_sems.at[a], recv_sems.at[a], (x, y, 1 - c))
               for a in range(na)]
        for cp in cps:
            cp.start()
        for cp in cps:
            cp.wait()

    return pl.pallas_call(
        body, name="rs_swap_halves", in_specs=[ANY] * na, out_specs=[ANY] * na,
        out_shape=[_sds((N_CHIPS, 1) + g.shape[2:], g.dtype) for g in gxs],
        scratch_shapes=[pltpu.SemaphoreType.DMA((na,)), pltpu.SemaphoreType.DMA((na,))],
    )(*gxs)


def _to_owners(csbs):
    na = len(csbs)

    def body(*refs):
        ins, outs = refs[:na], refs[na:2 * na]
        send_sems, recv_sems = refs[2 * na:]
        x, y, c = _place()
        me = 2 * x + y
        cps = []
        for a in range(na):
            for k in range(N_CHIPS - 1):
                j = (me + 1 + k) % N_CHIPS
                cps.append(_remote(ins[a].at[k], outs[a].at[2 - k], send_sems.at[3 * a + k], recv_sems.at[3 * a + 2 - k],
                                   (j // 2, j % 2, c)))
        for cp in cps:
            cp.start()
        for a in range(na):
            for k in range(N_CHIPS - 1):
                _remote(ins[a].at[k], outs[a].at[k], send_sems.at[3 * a + k], recv_sems.at[3 * a + k], (x, y, c)).wait_recv()
        for cp in cps:
            cp.wait_send()

    return pl.pallas_call(
        body, name="rs_to_owners", in_specs=[ANY] * na, out_specs=[ANY] * na,
        out_shape=[_sds(g.shape, g.dtype) for g in csbs],
        scratch_shapes=[pltpu.SemaphoreType.DMA((3 * na,)), pltpu.SemaphoreType.DMA((3 * na,))],
    )(*csbs)


def _share_halves(gfs):
    na = len(gfs)

    def body(*refs):
        ins, outs = refs[:na], refs[na:2 * na]
        send_sems, recv_sems = refs[2 * na:]
        x, y, c = _place()
        cps = [_remote(ins[a].at[pl.ds(c, 1)], outs[a].at[pl.ds(c, 1)], send_sems.at[a], recv_sems.at[a], (x, y, 1 - c))
               for a in range(na)]
        for cp in cps:
            cp.start()
        for a in range(na):
            cps[a].wait_send()
            _remote(ins[a].at[pl.ds(1 - c, 1)], outs[a].at[pl.ds(1 - c, 1)], send_sems.at[a], recv_sems.at[a], (x, y, 1 - c)).wait_recv()

    return pl.pallas_call(
        body, name="rs_share_halves", in_specs=[ANY] * na, out_specs=[ANY] * na,
        out_shape=[_sds(g.shape, g.dtype) for g in gfs], input_output_aliases={a: a for a in range(na)},
        scratch_shapes=[pltpu.SemaphoreType.DMA((na,)), pltpu.SemaphoreType.DMA((na,))],
    )(*gfs)


def _row_tile(rows, cols, budget=1 << 20):
    t = rows
    while t % 2 == 0 and t > 16 and t * cols * 4 > budget:
        t //= 2
    return t


def _chip_sums(gx, r1, where):
    _, _, r2, cd = gx.shape
    tr = _row_tile(r2, cd)

    def body(w_ref, a_ref, b_ref, o_ref):
        o_ref[...] = (a_ref[0] + b_ref[0]).astype(BF16)

    other = lambda k, i, w: ((w[1] + 1 + k) % N_CHIPS, w[0], i, 0)
    other0 = lambda k, i, w: ((w[1] + 1 + k) % N_CHIPS, 0, i, 0)
    return pl.pallas_call(
        body, name="rs_chip_sums",
        grid_spec=pltpu.PrefetchScalarGridSpec(
            num_scalar_prefetch=1, grid=(N_CHIPS - 1, r2 // tr),
            in_specs=[pl.BlockSpec((1, 1, tr, cd), other), pl.BlockSpec((1, 1, tr, cd), other0)],
            out_specs=pl.BlockSpec((1, tr, cd), lambda k, i, w: (k, i, 0))),
        out_shape=_sds((N_CHIPS - 1, r2, cd), BF16), compiler_params=_cp(("parallel", "parallel")),
    )(where, gx, r1)


def _owner_sum(gx, r1, r2x, where):
    _, _, r2, cd = gx.shape
    tr = _row_tile(r2, cd)

    def body(w_ref, a_ref, b_ref, r_ref, o_ref):
        acc = a_ref[0, 0] + b_ref[0, 0]
        for k in range(N_CHIPS - 1):
            acc = acc + r_ref[k].astype(F32)
        o_ref[0] = acc

    return pl.pallas_call(
        body, name="rs_owner_sum",
        grid_spec=pltpu.PrefetchScalarGridSpec(
            num_scalar_prefetch=1, grid=(r2 // tr,),
            in_specs=[pl.BlockSpec((1, 1, tr, cd), lambda i, w: (w[1], w[0], i, 0)),
                      pl.BlockSpec((1, 1, tr, cd), lambda i, w: (w[1], 0, i, 0)),
                      pl.BlockSpec((N_CHIPS - 1, tr, cd), lambda i, w: (0, i, 0))],
            out_specs=pl.BlockSpec((1, tr, cd), lambda i, w: (w[0], i, 0))),
        out_shape=_sds((2, r2, cd)), compiler_params=_cp(("parallel",)),
    )(where, gx, r1, r2x)


def _adamw_math(w, g, m, v):
    m = ADAM_B1 * m + (1.0 - ADAM_B1) * g
    v = ADAM_B2 * v + (1.0 - ADAM_B2) * (g * g)
    m_hat = m / (1.0 - ADAM_B1 ** ADAM_STEP)
    v_hat = v / (1.0 - ADAM_B2 ** ADAM_STEP)
    delta = -ADAM_LR * (m_hat / (jnp.sqrt(v_hat) + ADAM_EPS) + ADAM_WD * w)
    return delta, m, v


def _adamw(w, g, m, v, name):
    rows, cd = w.shape
    tr = _row_tile(rows, cd, budget=3 << 19) if rows % 16 == 0 else rows

    def body(w_ref, g_ref, m_ref, v_ref, d_ref, mo_ref, vo_ref):
        d, mm, vv = _adamw_math(w_ref[...], g_ref[...], m_ref[...], v_ref[...])
        d_ref[...] = d
        mo_ref[...] = mm
        vo_ref[...] = vv

    blk = pl.BlockSpec((tr, cd), lambda i: (i, 0))
    return pl.pallas_call(
        body, name=name, grid=(rows // tr,), in_specs=[blk] * 4, out_specs=[blk] * 3, out_shape=[_sds(w.shape)] * 3,
        compiler_params=_cp(("parallel",)),
    )(w, g, m, v)


def _small_allreduce_adamw(gp, wp, mp, vp):
    R = gp.shape[0]
    R2 = R // 2
    assert R2 % 8 == 0

    def body(g_ref, w_ref, m_ref, v_ref, go_ref, d_ref, mo_ref, vo_ref, sib, csum, land, send_sems, recv_sems):
        x, y, c = _place()
        me = 2 * x + y
        sibling = (x, y, 1 - c)
        chips = [(1 - x, y), (x, 1 - y), (1 - x, 1 - y)]
        swap = _remote(g_ref, sib, send_sems.at[0], recv_sems.at[0], sibling)
        swap.start()
        swap.wait()
        csum[...] = g_ref[...] + sib[...]
        half = csum.at[pl.ds(c * R2, R2)]
        land[me] = csum[pl.ds(c * R2, R2), :]
        cps = [_remote(half, land.at[me], send_sems.at[1 + j], recv_sems.at[1 + j], (px, py, c))
               for j, (px, py) in enumerate(chips)]
        for cp in cps:
            cp.start()
        for j, (px, py) in enumerate(chips):
            _remote(half, land.at[2 * px + py], send_sems.at[1 + j], recv_sems.at[1 + j], (px, py, c)).wait_recv()
        for cp in cps:
            cp.wait_send()
        mine = go_ref.at[pl.ds(c * R2, R2)]
        go_ref[pl.ds(c * R2, R2), :] = (land[0] + land[1]) + (land[2] + land[3])
        share = _remote(mine, mine, send_sems.at[4], recv_sems.at[4], sibling)
        share.start()
        share.wait_send()
        other = go_ref.at[pl.ds((1 - c) * R2, R2)]
        _remote(other, other, send_sems.at[4], recv_sems.at[4], sibling).wait_recv()
        d, mm, vv = _adamw_math(w_ref[...], go_ref[...], m_ref[...], v_ref[...])
        d_ref[...] = d
        mo_ref[...] = mm
        vo_ref[...] = vv

    vm = pl.BlockSpec(memory_space=pltpu.VMEM)
    return pl.pallas_call(
        body, name="small_allreduce_adamw", in_specs=[vm] * 4, out_specs=[vm] * 4, out_shape=[_sds((R, 128))] * 4,
        scratch_shapes=[pltpu.VMEM((R, 128), F32), pltpu.VMEM((R, 128), F32), pltpu.VMEM((N_CHIPS, R2, 128), F32),
                        pltpu.SemaphoreType.DMA((5,)), pltpu.SemaphoreType.DMA((5,))],
        compiler_params=_cp(),
    )(gp, wp, mp, vp)


def _pack(arrs):
    rows = []
    for a in arrs:
        f = a.reshape(-1)
        f = jnp.pad(f, (0, (-f.shape[0]) % 128))
        rows.append(f.reshape(-1, 128))
    p = jnp.concatenate(rows, axis=0)
    return jnp.pad(p, ((0, (-p.shape[0]) % 8), (0, 0)))


def _unpack(p, shapes):
    out, r = [], 0
    for s in shapes:
        n = math.prod(s)
        nr = -(-n // 128)
        out.append(p[r:r + nr].reshape(-1)[:n].reshape(s))
        r += nr
    return out


_SMALL = ("ln_w", "s5_lam_re", "s5_lam_im", "s5_log_step", "s5_b_re", "s5_b_im", "s5_c_re", "s5_c_im", "s5_d",
          "dn_a_log", "dn_dt_bias", "dn_norm_w", "final_norm_w")
_BIG = ("w_in", "s5_w_glu", "s5_w_up", "dn_w_up", "w_out")
_ORDER = ("ln_w", "w_in", "s5_lam_re", "s5_lam_im", "s5_log_step", "s5_b_re", "s5_b_im", "s5_c_re", "s5_c_im", "s5_d",
          "s5_w_glu", "s5_w_up", "dn_conv_w", "dn_a_log", "dn_dt_bias", "dn_norm_w", "dn_w_up", "w_out", "final_norm_w")


def kernel(x, ln_w, w_in, s5_lam_re, s5_lam_im, s5_log_step, s5_b_re, s5_b_im, s5_c_re, s5_c_im, s5_d, s5_w_glu, s5_w_up, dn_conv_w, dn_a_log, dn_dt_bias, dn_norm_w, dn_w_up, w_out, final_norm_w, loss_target, m_ln_w, m_w_in, m_s5_lam_re, m_s5_lam_im, m_s5_log_step, m_s5_b_re, m_s5_b_im, m_s5_c_re, m_s5_c_im, m_s5_d, m_s5_w_glu, m_s5_w_up, m_dn_conv_w, m_dn_a_log, m_dn_dt_bias, m_dn_norm_w, m_dn_w_up, m_w_out, m_final_norm_w, v_ln_w, v_w_in, v_s5_lam_re, v_s5_lam_im, v_s5_log_step, v_s5_b_re, v_s5_b_im, v_s5_c_re, v_s5_c_im, v_s5_d, v_s5_w_glu, v_s5_w_up, v_dn_conv_w, v_dn_a_log, v_dn_dt_bias, v_dn_norm_w, v_dn_w_up, v_w_out, v_final_norm_w):
    w = dict(ln_w=ln_w, w_in=w_in, s5_lam_re=s5_lam_re, s5_lam_im=s5_lam_im, s5_log_step=s5_log_step, s5_b_re=s5_b_re,
             s5_b_im=s5_b_im, s5_c_re=s5_c_re, s5_c_im=s5_c_im, s5_d=s5_d, s5_w_glu=s5_w_glu, s5_w_up=s5_w_up,
             dn_conv_w=dn_conv_w, dn_a_log=dn_a_log, dn_dt_bias=dn_dt_bias, dn_norm_w=dn_norm_w, dn_w_up=dn_w_up, w_out=w_out,
             final_norm_w=final_norm_w)
    m = dict(ln_w=m_ln_w, w_in=m_w_in, s5_lam_re=m_s5_lam_re, s5_lam_im=m_s5_lam_im, s5_log_step=m_s5_log_step,
             s5_b_re=m_s5_b_re, s5_b_im=m_s5_b_im, s5_c_re=m_s5_c_re, s5_c_im=m_s5_c_im, s5_d=m_s5_d, s5_w_glu=m_s5_w_glu,
             s5_w_up=m_s5_w_up, dn_conv_w=m_dn_conv_w, dn_a_log=m_dn_a_log, dn_dt_bias=m_dn_dt_bias, dn_norm_w=m_dn_norm_w,
             dn_w_up=m_dn_w_up, w_out=m_w_out, final_norm_w=m_final_norm_w)
    v = dict(ln_w=v_ln_w, w_in=v_w_in, s5_lam_re=v_s5_lam_re, s5_lam_im=v_s5_lam_im, s5_log_step=v_s5_log_step,
             s5_b_re=v_s5_b_re, s5_b_im=v_s5_b_im, s5_c_re=v_s5_c_re, s5_c_im=v_s5_c_im, s5_d=v_s5_d, s5_w_glu=v_s5_w_glu,
             s5_w_up=v_s5_w_up, dn_conv_w=v_dn_conv_w, dn_a_log=v_dn_a_log, dn_dt_bias=v_dn_dt_bias, dn_norm_w=v_dn_norm_w,
             dn_w_up=v_dn_w_up, w_out=v_w_out, final_norm_w=v_final_norm_w)
    xi, yi, ci = _place()
    chip = 2 * xi + yi
    where = jnp.stack([ci, chip]).astype(jnp.int32)

    g_in, g_glu, g_su, g_du, g_out, g_conv = _gather_weights(
        [w[n][0].astype(BF16) for n in _BIG], dn_conv_w[0])
    cat = lambda g: jnp.concatenate([g[j] for j in range(N_CHIPS)], axis=1)
    w_full = cat(g_in)
    w_perm = jnp.concatenate([w_full[:, :OFF_GS], w_full[:, OFF_GS + 2 * DN_HEADS:], w_full[:, OFF_GS:OFF_GS + 2 * DN_HEADS],
                              jnp.zeros((D_MODEL, D_IN_PAD - D_IN), BF16)], axis=1)

    loss_row, grad_x, g = _local_step(
        x[0], loss_target[0], ln_w, w_perm, s5_lam_re[0], s5_lam_im[0], s5_log_step, s5_b_re[0], s5_b_im[0], s5_c_re[0],
        s5_c_im[0], s5_d, g_glu.reshape(D_S5, D_S5), cat(g_su), cat(g_conv), dn_a_log, dn_dt_bias, dn_norm_w, cat(g_du),
        g_out.reshape(D_MODEL, D_MODEL), final_norm_w[None])
    loss = lax.psum(loss_row[0, 0], ("x", "y", "c"))

    dwp = g["w_perm"]
    d_w_in = jnp.concatenate([dwp[:, :OFF_GS], dwp[:, OFF_B:OFF_B + 2 * DN_HEADS], dwp[:, OFF_GS:OFF_B]], axis=1)
    cw = D_IN // N_CHIPS
    gxs = [d_w_in.reshape(D_MODEL, N_CHIPS, cw).transpose(1, 0, 2).reshape(N_CHIPS, 2, D_MODEL // 2, cw),
           g["s5_w_glu"].reshape(N_CHIPS, 2, D_S5 // 8, D_S5),
           g["s5_w_up"].reshape(N_CHIPS, 2, D_S5 // 2, D_MODEL // N_CHIPS),
           g["dn_w_up"].reshape(N_CHIPS, 2, D_DN // 2, D_MODEL // N_CHIPS),
           g["w_out"].reshape(N_CHIPS, 2, D_MODEL // 8, D_MODEL)]
    r1s = _swap_halves(gxs)
    csbs = [_chip_sums(gx, r1, where) for gx, r1 in zip(gxs, r1s)]
    r2s = _to_owners(csbs)
    gfs = [_owner_sum(gx, r1, r2x, where) for gx, r1, r2x in zip(gxs, r1s, r2s)]
    gfs = _share_halves(gfs)
    grads, deltas, new_m, new_v = {}, {}, {}, {}
    for n, gf in zip(_BIG, gfs):
        shp = w[n].shape
        g2 = gf.reshape(shp[1:])
        d_, m_, v_ = _adamw(w[n][0], g2, m[n][0], v[n][0], "adamw_" + n)
        grads[n], deltas[n], new_m[n], new_v[n] = g2.reshape(shp), d_.reshape(shp), m_.reshape(shp), v_.reshape(shp)

    gp = _pack([g[n] for n in _SMALL] + [g["dn_conv_w"]])
    zc = jnp.zeros((CONV_K, 3 * D_DN), F32)
    go, dl, mo, vo = _small_allreduce_adamw(gp, _pack([w[n] for n in _SMALL] + [zc]), _pack([m[n] for n in _SMALL] + [zc]),
                                            _pack([v[n] for n in _SMALL] + [zc]))
    shapes = [w[n].shape for n in _SMALL] + [(CONV_K, 3 * D_DN)]
    for dst, src in ((grads, go), (deltas, dl), (new_m, mo), (new_v, vo)):
        for n, a in zip(_SMALL, _unpack(src, shapes)):
            dst[n] = a
    cc = 3 * D_DN // N_CHIPS
    g_conv_mine = lax.dynamic_slice(_unpack(go, shapes)[-1], (0, chip * cc), (CONV_K, cc))
    d_, m_, v_ = _adamw(dn_conv_w[0], g_conv_mine, m_dn_conv_w[0], v_dn_conv_w[0], "adamw_dn_conv_w")
    grads["dn_conv_w"], deltas["dn_conv_w"], new_m["dn_conv_w"], new_v["dn_conv_w"] = (
        g_conv_mine[None], d_[None], m_[None], v_[None])

    return (loss, grad_x[None], *[grads[n] for n in _ORDER], *[deltas[n] for n in _ORDER], *[new_m[n] for n in _ORDER],
            *[new_v[n] for n in _ORDER])
```

```python
import functools
import math

import jax
import jax.numpy as jnp
from jax import lax
from jax.experimental import pallas as pl
from jax.experimental.pallas import tpu as pltpu

F32 = jnp.float32
BF16 = jnp.bfloat16
HI = lax.Precision.HIGHEST
MESH = pl.DeviceIdType.MESH
ANY = pl.BlockSpec(memory_space=pl.ANY)

EPS = 1e-6
D_MODEL = 2048
D_S5 = 1024
S5_GROUP = 16
S5_GROUPS = 64
S5_STATE = 64
S5_BLOCKS = 8
S5_SEG = 8
DN_HEADS = 8
DN_HEAD_DIM = 128
D_DN = 1024
CONV_K = 4
CHUNK = 64
D_IN = 10256
D_IN_PAD = 10368
OFF_US, OFF_ZS, OFF_Q, OFF_K, OFF_V, OFF_ZD, OFF_GS, OFF_GD, OFF_B = 0, 1024, 2048, 3072, 4096, 5120, 6144, 8192, 10240
N_CHIPS = 4
N_DEV = 8
VMEM_LIMIT = 56 * 1024 * 1024

ADAM_LR = 0.001
ADAM_B1 = 0.9
ADAM_B2 = 0.999
ADAM_EPS = 1e-08
ADAM_WD = 0.01
ADAM_STEP = 10


def _cp(sem=None):
    return pltpu.CompilerParams(dimension_semantics=sem, vmem_limit_bytes=VMEM_LIMIT)


def _sds(shape, dtype=F32):
    return jax.ShapeDtypeStruct(tuple(shape), dtype)


def _sigmoid(x):
    return 1.0 / (1.0 + jnp.exp(-x))


def _silu(x):
    return x * _sigmoid(x)


def _dsilu(x):
    s = _sigmoid(x)
    return s * (1.0 + x * (1.0 - s))


class _Exchange:
    def __init__(self, ins, out_shapes, n_sems, start, finish):
        self.ins, self.out_shapes, self.n_sems, self.start, self.finish = list(ins), list(out_shapes), n_sems, start, finish


def _mm(a, b, *, name, ta=False, tb=False, out_dtype=F32, tm=512, tn=512, tk=2048, shard_out=False, exchange=None):
    if ta:
        K, M = a.shape
    else:
        M, K = a.shape
    if tb:
        N, K2 = b.shape
    else:
        K2, N = b.shape
    assert K == K2, (a.shape, b.shape)
    tm, tn, tk = min(tm, M), min(tn, N), min(tk, K)
    assert M % tm == 0 and N % tn == 0 and K % tk == 0, (M, N, K, tm, tn, tk)
    nk = K // tk
    dims = (((0 if ta else 1,), (1 if tb else 0,)), ((), ()))

    gm, gn = M // tm, N // tn
    n_in = len(exchange.ins) if exchange else 0
    n_out = len(exchange.out_shapes) if exchange else 0

    def body(*refs):
        a_ref, b_ref, xin, o_ref = refs[0], refs[1], refs[2:2 + n_in], refs[2 + n_in]
        xout, rest = refs[3 + n_in:3 + n_in + n_out], refs[3 + n_in + n_out:]
        i, j, k = pl.program_id(0), pl.program_id(1), pl.program_id(2)
        if exchange:
            sems = rest[-3:]

            @pl.when(jnp.logical_and(jnp.logical_and(i == 0, j == 0), k == 0))
            def _():
                exchange.start(xin, xout, *sems)

        p = lax.dot_general(a_ref[...].astype(BF16), b_ref[...].astype(BF16), dims, preferred_element_type=F32)
        if nk == 1:
            o_ref[...] = p.astype(out_dtype).reshape(o_ref.shape)
        else:
            acc_ref = rest[0]

            @pl.when(k == 0)
            def _():
                acc_ref[...] = p

            @pl.when(k > 0)
            def _():
                acc_ref[...] += p

            @pl.when(k == nk - 1)
            def _():
                o_ref[...] = acc_ref[...].astype(out_dtype).reshape(o_ref.shape)

        if exchange:
            @pl.when(jnp.logical_and(jnp.logical_and(i == gm - 1, j == gn - 1), k == nk - 1))
            def _():
                exchange.finish(xin, xout, *sems)

    a_spec = pl.BlockSpec((tk, tm), lambda i, j, k: (k, i)) if ta else pl.BlockSpec((tm, tk), lambda i, j, k: (i, k))
    b_spec = pl.BlockSpec((tn, tk), lambda i, j, k: (j, k)) if tb else pl.BlockSpec((tk, tn), lambda i, j, k: (k, j))
    if shard_out:
        o_spec = pl.BlockSpec((1, tm, tn), lambda i, j, k: (j, i, 0))
        o_shape = _sds((N // tn, M, tn), out_dtype)
    else:
        o_spec = pl.BlockSpec((tm, tn), lambda i, j, k: (i, j))
        o_shape = _sds((M, N), out_dtype)
    scratch = [pltpu.VMEM((tm, tn), F32)] if nk > 1 else []
    if not exchange:
        return pl.pallas_call(
            body, name=name, grid=(gm, gn, nk), in_specs=[a_spec, b_spec], out_specs=o_spec, out_shape=o_shape,
            scratch_shapes=scratch, compiler_params=_cp(("parallel", "parallel", "arbitrary")),
        )(a, b)
    scratch += [pltpu.SemaphoreType.DMA((exchange.n_sems,)) for _ in range(3)]
    return pl.pallas_call(
        body, name=name, grid=(gm, gn, nk), in_specs=[a_spec, b_spec] + [ANY] * n_in, out_specs=[o_spec] + [ANY] * n_out,
        out_shape=[o_shape] + exchange.out_shapes, scratch_shapes=scratch,
        compiler_params=_cp(("arbitrary", "arbitrary", "arbitrary")),
    )(a, b, *exchange.ins)


def _ln_fwd(x, w):
    L, D = x.shape
    tm = min(256, L)

    def body(x_ref, w_ref, h_ref, r_ref):
        xv = x_ref[...]
        r = lax.rsqrt(jnp.mean(xv * xv, axis=-1, keepdims=True) + EPS)
        h_ref[...] = (xv * r * w_ref[...]).astype(BF16)
        r_ref[...] = r

    return pl.pallas_call(
        body, name="ln_fwd", grid=(L // tm,),
        in_specs=[pl.BlockSpec((tm, D), lambda i: (i, 0)), pl.BlockSpec((1, D), lambda i: (0, 0))],
        out_specs=[pl.BlockSpec((tm, D), lambda i: (i, 0)), pl.BlockSpec((tm, 1), lambda i: (i, 0))],
        out_shape=[_sds((L, D), BF16), _sds((L, 1))], compiler_params=_cp(("parallel",)),
    )(x, w)


def _ln_bwd(x, r, w, dh, dx2):
    L, D = x.shape
    tm = min(256, L)

    def body(x_ref, r_ref, w_ref, dh_ref, dx2_ref, dx_ref, dw_ref):
        i = pl.program_id(0)
        xv, rv, dhv = x_ref[...], r_ref[...], dh_ref[...]
        t = dhv * w_ref[...]
        m = jnp.mean(t * xv, axis=-1, keepdims=True)
        dx_ref[...] = dx2_ref[...] + rv * t - xv * (rv * rv * rv) * m
        part = jnp.sum(dhv * xv * rv, axis=0, keepdims=True)

        @pl.when(i == 0)
        def _():
            dw_ref[...] = part

        @pl.when(i > 0)
        def _():
            dw_ref[...] += part

    row = pl.BlockSpec((tm, D), lambda i: (i, 0))
    return pl.pallas_call(
        body, name="ln_bwd", grid=(L // tm,),
        in_specs=[row, pl.BlockSpec((tm, 1), lambda i: (i, 0)), pl.BlockSpec((1, D), lambda i: (0, 0)), row, row],
        out_specs=[row, pl.BlockSpec((1, D), lambda i: (0, 0))],
        out_shape=[_sds((L, D)), _sds((1, D))], compiler_params=_cp(("arbitrary",)),
    )(x, r, w, dh, dx2)


def _s5_param_math(lam_re, lam_im, log_step, b_re, b_im, expand):
    step = jnp.exp(log_step)
    mag = jnp.exp(lam_re * step)
    abar_re = mag * jnp.cos(lam_im * step)
    abar_im = mag * jnp.sin(lam_im * step)
    den = lam_re * lam_re + lam_im * lam_im
    xr = abar_re - 1.0
    f_re = (xr * lam_re + abar_im * lam_im) / den
    f_im = (abar_im * lam_re - xr * lam_im) / den
    fe_re = jnp.dot(f_re, expand, precision=HI, preferred_element_type=F32)
    fe_im = jnp.dot(f_im, expand, precision=HI, preferred_element_type=F32)
    bb_re = fe_re * b_re - fe_im * b_im
    bb_im = fe_re * b_im + fe_im * b_re
    return abar_re, abar_im, bb_re, bb_im


def _s5_expand():
    p = lax.broadcasted_iota(jnp.int32, (S5_STATE, S5_STATE * S5_GROUP), 0)
    q = lax.broadcasted_iota(jnp.int32, (S5_STATE, S5_STATE * S5_GROUP), 1)
    return (q // S5_GROUP == p).astype(F32)


def _s5_param_fwd(lam_re, lam_im, log_step, b_re, b_im):
    G, P = lam_re.shape

    def body(lr, li, ls, br, bi, ar_o, ai_o, bbr_o, bbi_o):
        outs = _s5_param_math(lr[...], li[...], ls[...], br[...], bi[...], _s5_expand())
        for o, v in zip((ar_o, ai_o, bbr_o, bbi_o), outs):
            o[...] = v

    return pl.pallas_call(
        body, name="s5_param_fwd",
        out_shape=[_sds((G, P)), _sds((G, P)), _sds(b_re.shape), _sds(b_re.shape)], compiler_params=_cp(),
    )(lam_re, lam_im, log_step, b_re, b_im)


def _s5_param_bwd(lam_re, lam_im, log_step, b_re, b_im, dar, dai, dbbr, dbbi):
    G, P = lam_re.shape

    def body(lr, li, ls, br, bi, g0, g1, g2, g3, dlr, dli, dls, dbr, dbi):
        ex = _s5_expand()
        _, f = jax.vjp(lambda a, b, c, d, e: _s5_param_math(a, b, c, d, e, ex), lr[...], li[...], ls[...], br[...], bi[...])
        grads = f((g0[...], g1[...], g2[...], g3[...]))
        for o, v in zip((dlr, dli, dls, dbr, dbi), grads):
            o[...] = v

    return pl.pallas_call(
        body, name="s5_param_bwd",
        out_shape=[_sds((G, P)), _sds((G, P)), _sds((G, 1)), _sds(b_re.shape), _sds(b_re.shape)], compiler_params=_cp(),
    )(lam_re, lam_im, log_step, b_re, b_im, dar, dai, dbbr, dbbi)


def _to_segs(src_ref, dst_ref, L):
    S = L // S5_SEG

    def body(j, carry):
        dst_ref[pl.ds(pl.multiple_of(S5_SEG * j, S5_SEG), S5_SEG), :] = src_ref[pl.ds(j, S5_SEG, stride=S), :]
        return carry

    lax.fori_loop(0, S, body, 0, unroll=8)


def _from_segs(src_ref, L, write):
    S = L // S5_SEG
    for seg in range(S5_SEG):
        def body(jb, carry, seg=seg):
            j0 = 16 * jb
            write(pl.multiple_of(seg * S + j0, 16), src_ref[pl.ds(S5_SEG * j0 + seg, 16, stride=S5_SEG), :])
            return carry

        lax.fori_loop(0, S // 16, body, 0, unroll=4)


def _scan_segs(ar, ai, re_ref, im_ref, end_r_ref, end_i_ref, c_r_ref, c_i_ref, L, tile0, reverse):
    S = L // S5_SEG
    NB, LN = re_ref.shape[0], 128
    assert S & (S - 1) == 0
    tile = lambda j: pl.ds(pl.multiple_of(S5_SEG * (tile0 + j), S5_SEG), S5_SEG)
    ar8 = [jnp.broadcast_to(ar[:, b * LN:(b + 1) * LN], (S5_SEG, LN)) for b in range(NB)]
    ai8 = [jnp.broadcast_to(ai[:, b * LN:(b + 1) * LN], (S5_SEG, LN)) for b in range(NB)]

    def step(idx, carry):
        rows = tile(S - 1 - idx if reverse else idx)
        out = []
        for b in range(NB):
            sr, si = carry[b]
            nr = ar8[b] * sr - ai8[b] * si + re_ref[b, rows, :]
            ni = ar8[b] * si + ai8[b] * sr + im_ref[b, rows, :]
            re_ref[b, rows, :] = nr
            im_ref[b, rows, :] = ni
            out.append((nr, ni))
        return tuple(out)

    z8 = jnp.zeros((S5_SEG, LN), F32)
    fin = lax.fori_loop(0, S, step, tuple((z8, z8) for _ in range(NB)), unroll=4)
    order = range(S5_SEG - 2, -1, -1) if reverse else range(1, S5_SEG)
    for b in range(NB):
        end_r_ref[b], end_i_ref[b] = fin[b]
        pr, pi = ar8[b][:1], ai8[b][:1]
        for _ in range(int(math.log2(S))):
            pr, pi = pr * pr - pi * pi, 2.0 * pr * pi
        first = S5_SEG - 1 if reverse else 0
        c_r_ref[b, pl.ds(first, 1), :] = jnp.zeros((1, LN), F32)
        c_i_ref[b, pl.ds(first, 1), :] = jnp.zeros((1, LN), F32)
        cr, ci = end_r_ref[b, pl.ds(first, 1), :], end_i_ref[b, pl.ds(first, 1), :]
        for i in order:
            c_r_ref[b, pl.ds(i, 1), :] = cr
            c_i_ref[b, pl.ds(i, 1), :] = ci
            er, ei = end_r_ref[b, pl.ds(i, 1), :], end_i_ref[b, pl.ds(i, 1), :]
            cr, ci = er + pr * cr - pi * ci, ei + pr * ci + pi * cr

    def fix(idx, carry):
        rows = tile(S - 1 - idx if reverse else idx)
        out = []
        for b in range(NB):
            pr, pi = carry[b]
            cr, ci = c_r_ref[b], c_i_ref[b]
            re_ref[b, rows, :] += pr * cr - pi * ci
            im_ref[b, rows, :] += pr * ci + pi * cr
            out.append((pr * ar8[b] - pi * ai8[b], pr * ai8[b] + pi * ar8[b]))
        return tuple(out)

    lax.fori_loop(0, S, fix, tuple((ar8[b], ai8[b]) for b in range(NB)), unroll=4)


def _s5_seg_scratch(L, cs, pad):
    NB = cs // 128
    small = [pltpu.VMEM((NB, S5_SEG, 128), F32) for _ in range(4)]
    return [pltpu.VMEM((NB, L + pad, 128), F32), pltpu.VMEM((NB, L + pad, 128), F32)] + small


def _s5_core_fwd(proj, wbr, wbi, a_re, a_im, cbr, cbi):
    L = proj.shape[0]
    nb, ci, cs = wbr.shape
    NB = cs // 128

    def body(u_ref, wbr_ref, wbi_ref, ar_ref, ai_ref, cbr_ref, cbi_ref, y_ref, sr, si, er, ei, cr, cim, up, yp):
        _to_segs(u_ref, up, L)
        u = up[...].astype(BF16)
        for b in range(NB):
            lanes = pl.ds(b * 128, 128)
            sr[b] = jnp.dot(u, wbr_ref[0, :, lanes], preferred_element_type=F32)
            si[b] = jnp.dot(u, wbi_ref[0, :, lanes], preferred_element_type=F32)
        _scan_segs(ar_ref[...], ai_ref[...], sr, si, er, ei, cr, cim, L, 0, False)
        y = jnp.zeros((L, ci), F32)
        for b in range(NB):
            lanes = pl.ds(b * 128, 128)
            y = y + (jnp.dot(sr[b].astype(BF16), cbr_ref[0, lanes, :], preferred_element_type=F32)
                     - jnp.dot(si[b].astype(BF16), cbi_ref[0, lanes, :], preferred_element_type=F32))
        yp[...] = y

        def write(row, val):
            y_ref[pl.ds(row, 16), :] = val

        _from_segs(yp, L, write)

    wspec = pl.BlockSpec((1, ci, cs), lambda j: (j, 0, 0))
    aspec = pl.BlockSpec((1, cs), lambda j: (0, j))
    cspec = pl.BlockSpec((1, cs, ci), lambda j: (j, 0, 0))
    return pl.pallas_call(
        body, name="s5_core_fwd", grid=(nb,),
        in_specs=[pl.BlockSpec((L, ci), lambda j: (0, OFF_US // ci + j)), wspec, wspec, aspec, aspec, cspec, cspec],
        out_specs=pl.BlockSpec((L, ci), lambda j: (0, j)), out_shape=_sds((L, nb * ci)),
        scratch_shapes=_s5_seg_scratch(L, cs, 0) + [pltpu.VMEM((L, ci), F32), pltpu.VMEM((L, ci), F32)],
        compiler_params=_cp(("arbitrary",)),
    )(proj, wbr, wbi, a_re, a_im, cbr, cbi)


def _s5_core_bwd(proj, wbr, wbi, a_re, a_im, cbr, cbi, dyc, du1):
    L = proj.shape[0]
    nb, ci, cs = wbr.shape
    NB = cs // 128
    S = L // S5_SEG
    PAD = S5_SEG

    def body(u_ref, wbr_ref, wbi_ref, ar_ref, ai_ref, cbr_ref, cbi_ref, dy_ref, du1_ref,
             du_ref, dwbr_ref, dwbi_ref, dcbr_ref, dcbi_ref, dar_ref, dai_ref,
             sr, si, er, ei, cr, cim, lr, li, up, dyp, dup):
        tn = (((0,), (0,)), ((), ()))
        nt = (((1,), (1,)), ((), ()))
        _to_segs(u_ref, up, L)
        _to_segs(dy_ref, dyp, L)
        _to_segs(du1_ref, dup, L)
        u = up[...].astype(BF16)
        dy = dyp[...].astype(BF16)
        ar, ai = ar_ref[...], ai_ref[...]
        for b in range(NB):
            lanes = pl.ds(b * 128, 128)
            sr[b, pl.ds(PAD, L), :] = jnp.dot(u, wbr_ref[0, :, lanes], preferred_element_type=F32)
            si[b, pl.ds(PAD, L), :] = jnp.dot(u, wbi_ref[0, :, lanes], preferred_element_type=F32)
        _scan_segs(ar, ai, sr, si, er, ei, cr, cim, L, 1, False)
        for b in range(NB):
            lanes = pl.ds(b * 128, 128)
            sr[b, pl.ds(0, PAD), :] = cr[b]
            si[b, pl.ds(0, PAD), :] = cim[b]
            lr[b] = lax.dot_general(dy, cbr_ref[0, lanes, :], nt, preferred_element_type=F32)
            li[b] = -lax.dot_general(dy, cbi_ref[0, lanes, :], nt, preferred_element_type=F32)
            dcbr_ref[0, lanes, :] = lax.dot_general(sr[b, pl.ds(PAD, L), :].astype(BF16), dy, tn, preferred_element_type=F32)
            dcbi_ref[0, lanes, :] = -lax.dot_general(si[b, pl.ds(PAD, L), :].astype(BF16), dy, tn, preferred_element_type=F32)
        _scan_segs(ar, -ai, lr, li, er, ei, cr, cim, L, 0, True)

        def da_step(j, carry):
            rows = pl.ds(pl.multiple_of(S5_SEG * j, S5_SEG), S5_SEG)
            out = []
            for b in range(NB):
                dar, dai = carry[b]
                pr_, pi_ = sr[b, rows, :], si[b, rows, :]
                gr, gi = lr[b, rows, :], li[b, rows, :]
                out.append((dar + (gr * pr_ + gi * pi_), dai + (gi * pr_ - gr * pi_)))
            return tuple(out)

        z8 = jnp.zeros((S5_SEG, 128), F32)
        acc = lax.fori_loop(0, S, da_step, tuple((z8, z8) for _ in range(NB)), unroll=4)
        du = dup[...]
        for b in range(NB):
            lanes = pl.ds(b * 128, 128)
            dar_ref[:, lanes] = jnp.sum(acc[b][0], axis=0, keepdims=True)
            dai_ref[:, lanes] = jnp.sum(acc[b][1], axis=0, keepdims=True)
            gr, gi = lr[b].astype(BF16), li[b].astype(BF16)
            du = du + (lax.dot_general(gr, wbr_ref[0, :, lanes], nt, preferred_element_type=F32)
                       + lax.dot_general(gi, wbi_ref[0, :, lanes], nt, preferred_element_type=F32))
            dwbr_ref[0, :, lanes] = lax.dot_general(u, gr, tn, preferred_element_type=F32)
            dwbi_ref[0, :, lanes] = lax.dot_general(u, gi, tn, preferred_element_type=F32)
        dup[...] = du

        def write(row, val):
            du_ref[pl.ds(row, 16), :] = val.astype(BF16)

        _from_segs(dup, L, write)

    wspec = pl.BlockSpec((1, ci, cs), lambda j: (j, 0, 0))
    aspec = pl.BlockSpec((1, cs), lambda j: (0, j))
    cspec = pl.BlockSpec((1, cs, ci), lambda j: (j, 0, 0))
    col = pl.BlockSpec((L, ci), lambda j: (0, j))
    return pl.pallas_call(
        body, name="s5_core_bwd", grid=(nb,),
        in_specs=[pl.BlockSpec((L, ci), lambda j: (0, OFF_US // ci + j)), wspec, wspec, aspec, aspec, cspec, cspec, col, col],
        out_specs=[col, wspec, wspec, cspec, cspec, aspec, aspec],
        out_shape=[_sds((L, nb * ci), BF16), _sds(wbr.shape), _sds(wbr.shape), _sds(cbr.shape), _sds(cbr.shape),
                   _sds((1, nb * cs)), _sds((1, nb * cs))],
        scratch_shapes=(_s5_seg_scratch(L, cs, PAD) + [pltpu.VMEM((NB, L, 128), F32), pltpu.VMEM((NB, L, 128), F32)]
                        + [pltpu.VMEM((L, ci), F32) for _ in range(3)]),
        compiler_params=_cp(("arbitrary",)),
    )(proj, wbr, wbi, a_re, a_im, cbr, cbi, dyc, du1)


def _s5_post_math(yc, u, z, d, wg):
    y = yc + d * u
    y1 = jax.nn.gelu(y)
    t = jnp.dot(y1.astype(BF16), wg, preferred_element_type=F32)
    sg = _sigmoid(t)
    return y, y1, sg


def _s5_post_fwd(yc, proj, d, wg):
    L, W = yc.shape
    tm = min(256, L)

    def body(yc_ref, u_ref, z_ref, d_ref, wg_ref, o_ref):
        _, y1, sg = _s5_post_math(yc_ref[...], u_ref[...], z_ref[...], d_ref[...], wg_ref[...])
        o_ref[...] = (y1 * sg * _silu(z_ref[...])).astype(BF16)

    row = pl.BlockSpec((tm, W), lambda i: (i, 0))
    return pl.pallas_call(
        body, name="s5_post_fwd", grid=(L // tm,),
        in_specs=[row, pl.BlockSpec((tm, W), lambda i: (i, OFF_US // W)), pl.BlockSpec((tm, W), lambda i: (i, OFF_ZS // W)),
                  pl.BlockSpec((1, W), lambda i: (0, 0)), pl.BlockSpec((W, W), lambda i: (0, 0))],
        out_specs=row, out_shape=_sds((L, W), BF16), compiler_params=_cp(("parallel",)),
    )(yc, proj, proj, d, wg)


def _s5_post_bwd(yc, proj, d, wg, dout):
    L, W = yc.shape
    tm = min(256, L)

    def body(yc_ref, u_ref, z_ref, d_ref, wg_ref, do_ref, dyc_ref, du_ref, dz_ref, dd_ref, dwg_ref):
        i = pl.program_id(0)
        u, z, d_, wgv = u_ref[...], z_ref[...], d_ref[...], wg_ref[...]
        y, y1, sg = _s5_post_math(yc_ref[...], u, z, d_, wgv)
        dout_ = do_ref[...]
        y2 = y1 * sg
        dy2 = dout_ * _silu(z)
        dz_ref[...] = (dout_ * y2 * _dsilu(z)).astype(BF16)
        dt = (dy2 * y1 * sg * (1.0 - sg)).astype(BF16)
        dy1 = dy2 * sg + lax.dot_general(dt, wgv, (((1,), (1,)), ((), ())), preferred_element_type=F32)
        _, gelu_vjp = jax.vjp(jax.nn.gelu, y)
        dy = gelu_vjp(dy1)[0]
        dyc_ref[...] = dy
        du_ref[...] = dy * d_
        dd_part = jnp.sum(dy * u, axis=0, keepdims=True)
        dwg_part = lax.dot_general(y1.astype(BF16), dt, (((0,), (0,)), ((), ())), preferred_element_type=F32)

        @pl.when(i == 0)
        def _():
            dd_ref[...] = dd_part
            dwg_ref[...] = dwg_part

        @pl.when(i > 0)
        def _():
            dd_ref[...] += dd_part
            dwg_ref[...] += dwg_part

    row = pl.BlockSpec((tm, W), lambda i: (i, 0))
    return pl.pallas_call(
        body, name="s5_post_bwd", grid=(L // tm,),
        in_specs=[row, pl.BlockSpec((tm, W), lambda i: (i, OFF_US // W)), pl.BlockSpec((tm, W), lambda i: (i, OFF_ZS // W)),
                  pl.BlockSpec((1, W), lambda i: (0, 0)), pl.BlockSpec((W, W), lambda i: (0, 0)), row],
        out_specs=[row, row, row, pl.BlockSpec((1, W), lambda i: (0, 0)), pl.BlockSpec((W, W), lambda i: (0, 0))],
        out_shape=[_sds((L, W)), _sds((L, W)), _sds((L, W), BF16), _sds((1, W)), _sds((W, W))],
        compiler_params=_cp(("arbitrary",)),
    )(yc, proj, proj, d, wg, dout)


def _shift_down(x, s):
    if s == 0:
        return x
    rows = lax.broadcasted_iota(jnp.int32, x.shape, 0)
    return jnp.where(rows >= s, pltpu.roll(x, s, 0), 0.0)


def _shift_up(x, s):
    if s == 0:
        return x
    L = x.shape[0]
    rows = lax.broadcasted_iota(jnp.int32, x.shape, 0)
    return jnp.where(rows < L - s, pltpu.roll(x, L - s, 0), 0.0)


def _conv_pre(x, w):
    acc = w[CONV_K - 1:CONV_K, :] * x
    for s in range(1, CONV_K):
        acc = acc + w[CONV_K - 1 - s:CONV_K - s, :] * _shift_down(x, s)
    return acc


def _dn_conv_fwd(proj, conv_w):
    L = proj.shape[0]
    W = DN_HEAD_DIM
    nq = 2 * DN_HEADS

    def body(x_ref, w_ref, o_ref):
        j = pl.program_id(0)
        act = _silu(_conv_pre(x_ref[...], w_ref[...]))
        r = lax.rsqrt(jnp.sum(act * act, axis=-1, keepdims=True) + EPS)
        scale = jnp.where(j < DN_HEADS, DN_HEAD_DIM ** -0.5, 1.0)
        o_ref[...] = jnp.where(j < nq, act * r * scale, act)

    return pl.pallas_call(
        body, name="dn_conv_fwd", grid=(3 * DN_HEADS,),
        in_specs=[pl.BlockSpec((L, W), lambda j: (0, OFF_Q // W + j)), pl.BlockSpec((CONV_K, W), lambda j: (0, j))],
        out_specs=pl.BlockSpec((L, W), lambda j: (0, j)), out_shape=_sds((L, 3 * D_DN)), compiler_params=_cp(("parallel",)),
    )(proj, conv_w)


def _dn_conv_bwd(proj, conv_w, dout):
    L = proj.shape[0]
    W = DN_HEAD_DIM
    nq = 2 * DN_HEADS

    def body(x_ref, w_ref, do_ref, dx_ref, dw_ref):
        j = pl.program_id(0)
        x, w, dout_ = x_ref[...], w_ref[...], do_ref[...]
        pre = _conv_pre(x, w)
        act = _silu(pre)
        r = lax.rsqrt(jnp.sum(act * act, axis=-1, keepdims=True) + EPS)
        scale = jnp.where(j < DN_HEADS, DN_HEAD_DIM ** -0.5, 1.0)
        g = dout_ * scale
        dact_n = r * g - act * (r * r * r) * jnp.sum(g * act, axis=-1, keepdims=True)
        dact = jnp.where(j < nq, dact_n, dout_)
        dpre = dact * _dsilu(pre)
        dx = w[CONV_K - 1:CONV_K, :] * dpre
        for s in range(1, CONV_K):
            dx = dx + w[CONV_K - 1 - s:CONV_K - s, :] * _shift_up(dpre, s)
        dx_ref[...] = dx.astype(BF16)
        for s in range(CONV_K):
            dw_ref[pl.ds(CONV_K - 1 - s, 1), :] = jnp.sum(dpre * _shift_down(x, s), axis=0, keepdims=True)

    col = pl.BlockSpec((L, W), lambda j: (0, j))
    wsp = pl.BlockSpec((CONV_K, W), lambda j: (0, j))
    return pl.pallas_call(
        body, name="dn_conv_bwd", grid=(3 * DN_HEADS,),
        in_specs=[pl.BlockSpec((L, W), lambda j: (0, OFF_Q // W + j)), wsp, col], out_specs=[col, wsp],
        out_shape=[_sds((L, 3 * D_DN), BF16), _sds((CONV_K, 3 * D_DN))], compiler_params=_cp(("parallel",)),
    )(proj, conv_w, dout)


def _softplus(x):
    return jnp.maximum(x, 0.0) + jnp.log(1.0 + jnp.exp(-jnp.abs(x)))


def _dn_gates_fwd(proj, alog, dtb):
    L = proj.shape[0]
    W = 128

    def body(p_ref, al_ref, db_ref, o_ref):
        p = p_ref[...]
        lane = lax.broadcasted_iota(jnp.int32, p.shape, 1)
        g = -jnp.exp(al_ref[...]) * _softplus(p + db_ref[...])
        o_ref[...] = jnp.where(lane < DN_HEADS, _sigmoid(p), jnp.where(lane < 2 * DN_HEADS, g, 0.0))

    return pl.pallas_call(
        body, name="dn_gates_fwd", grid=(1,),
        in_specs=[pl.BlockSpec((L, W), lambda i: (0, OFF_B // W)), pl.BlockSpec((1, W), lambda i: (0, 0)),
                  pl.BlockSpec((1, W), lambda i: (0, 0))],
        out_specs=pl.BlockSpec((L, W), lambda i: (0, 0)), out_shape=_sds((L, W)), compiler_params=_cp(("arbitrary",)),
    )(proj, alog, dtb)


def _dn_gates_bwd(proj, alog, dtb, dgates):
    L = proj.shape[0]
    W = 128

    def body(p_ref, al_ref, db_ref, dg_ref, dp_ref, dal_ref, ddb_ref):
        p, dg = p_ref[...], dg_ref[...]
        lane = lax.broadcasted_iota(jnp.int32, p.shape, 1)
        is_g = jnp.logical_and(lane >= DN_HEADS, lane < 2 * DN_HEADS)
        beta = _sigmoid(p)
        na = -jnp.exp(al_ref[...])
        xs = p + db_ref[...]
        dsp = dg * na * _sigmoid(xs)
        dp_ref[...] = jnp.where(lane < DN_HEADS, dg * beta * (1.0 - beta), jnp.where(is_g, dsp, 0.0)).astype(BF16)
        dal_ref[...] = jnp.sum(jnp.where(is_g, dg * na * _softplus(xs), 0.0), axis=0, keepdims=True)
        ddb_ref[...] = jnp.sum(jnp.where(is_g, dsp, 0.0), axis=0, keepdims=True)

    one = pl.BlockSpec((1, W), lambda i: (0, 0))
    full = pl.BlockSpec((L, W), lambda i: (0, 0))
    return pl.pallas_call(
        body, name="dn_gates_bwd", grid=(1,),
        in_specs=[pl.BlockSpec((L, W), lambda i: (0, OFF_B // W)), one, one, full], out_specs=[full, one, one],
        out_shape=[_sds((L, W), BF16), _sds((1, W)), _sds((1, W))], compiler_params=_cp(("arbitrary",)),
    )(proj, alog, dtb, dgates)


def _bdot(a, b, dims):
    return lax.dot_general(a.astype(BF16), b.astype(BF16), (dims, ((), ())), preferred_element_type=F32)


_NN, _NT, _TN = ((1,), (0,)), ((1,), (1,)), ((0,), (0,))


def _dot3(a, b, dims):
    ah, bh = a.astype(BF16), b.astype(BF16)
    al, bl = (a - ah.astype(F32)).astype(BF16), (b - bh.astype(F32)).astype(BF16)
    (ca,), (cb,) = dims
    a3 = jnp.concatenate([ah, ah, al], axis=ca)
    b3 = jnp.concatenate([bh, bl, bh], axis=cb)
    return lax.dot_general(a3, b3, (dims, ((), ())), preferred_element_type=F32)


def _mm_family(raw):
    nn = jax.custom_vjp(lambda a, b: raw(a, b, _NN))
    nt = jax.custom_vjp(lambda a, b: raw(a, b, _NT))
    tn = jax.custom_vjp(lambda a, b: raw(a, b, _TN))
    nn.defvjp(lambda a, b: (raw(a, b, _NN), (a, b)), lambda r, g: (raw(g, r[1], _NT), raw(r[0], g, _TN)))
    nt.defvjp(lambda a, b: (raw(a, b, _NT), (a, b)), lambda r, g: (raw(g, r[1], _NN), raw(g, r[0], _TN)))
    tn.defvjp(lambda a, b: (raw(a, b, _TN), (a, b)), lambda r, g: (raw(r[1], g, _NT), raw(r[0], g, _NN)))
    return nn, nt, tn


_mm_nn, _mm_nt, _mm_tn = _mm_family(_bdot)
_m3_nn, _m3_nt, _m3_tn = _mm_family(_dot3)


def _tri_apply(x, upper):
    C = x.shape[0]
    ii = lax.broadcasted_iota(jnp.int32, (C, 3 * C), 0)
    jj = lax.broadcasted_iota(jnp.int32, (C, 3 * C), 1) % C
    mat = ((ii <= jj) if upper else (ii >= jj)).astype(BF16)
    hi = x.astype(BF16)
    r = x - hi.astype(F32)
    mid = r.astype(BF16)
    lo = (r - mid.astype(F32)).astype(BF16)
    return jnp.dot(mat, jnp.concatenate([hi, mid, lo], axis=0), preferred_element_type=F32)


_cumsum_rows = jax.custom_vjp(lambda x: _tri_apply(x, False))
_cumsum_rows.defvjp(lambda x: (_tri_apply(x, False), None), lambda _, g: (_tri_apply(g, True),))


def _uli(a_s):
    C = a_s[0].shape[0]
    ii = lax.broadcasted_iota(jnp.int32, (C, C), 0)
    jj = lax.broadcasted_iota(jnp.int32, (C, C), 1)
    eye = jnp.where(ii == jj, 1.0, 0.0)
    ts = [eye - a for a in a_s]
    ms = list(a_s)
    for _ in range(int(math.log2(C)) - 1):
        ms = [_dot3(m, m, _NN) for m in ms]
        ts = [t + _dot3(t, m, _NN) for t, m in zip(ts, ms)]
    return tuple(ts)


def _uli_bwd(ts, gs):
    xs = [_dot3(t, g, _TN) for t, g in zip(ts, gs)]
    return (tuple(-_dot3(x, t, _NT) for x, t in zip(xs, ts)),)


_unit_lower_inverse = jax.custom_vjp(_uli)
_unit_lower_inverse.defvjp(lambda a_s: (lambda ts: (ts, ts))(_uli(a_s)), _uli_bwd)


def _prep_math(qs, ks, vs, gcols, bcols):
    n = len(qs)
    C, dv = vs[0].shape
    ii = lax.broadcasted_iota(jnp.int32, (C, C), 0)
    jj = lax.broadcasted_iota(jnp.int32, (C, C), 1)
    causal = ii >= jj
    strict = ii > jj
    sf = strict.astype(F32)
    ones = jnp.ones((C, dv), F32)
    dms = [_cumsum_rows(g * sf) for g in gcols]
    gcbs = [_cumsum_rows(g * ones) for g in gcols]
    kks = [_mm_nt(k, k) for k in ks]
    qks = [_mm_nt(q, k) for q, k in zip(qs, ks)]
    decays = [jnp.where(causal, jnp.exp(jnp.where(causal, dm, 0.0)), 0.0) for dm in dms]
    glasts = [jnp.sum(g * ones, axis=0, keepdims=True) for g in gcols]
    egs = [jnp.exp(gcb) for gcb in gcbs]
    ts = _unit_lower_inverse(tuple(jnp.where(strict, b * kk * dc, 0.0) for b, kk, dc in zip(bcols, kks, decays)))
    us = [_m3_nn(t, v * b) for t, v, b in zip(ts, vs, bcols)]
    ws = [_m3_nn(t, k * b * eg) for t, k, b, eg in zip(ts, ks, bcols, egs)]
    return tuple((us[i], ws[i], qs[i] * egs[i], ks[i] * jnp.exp(glasts[i] - gcbs[i]), qks[i] * decays[i],
                  jnp.exp(glasts[i])) for i in range(n))


def _gate_cols(gates, h):
    lane = lax.broadcasted_iota(jnp.int32, gates.shape, 1)
    bcol = jnp.sum(jnp.where(lane == h, gates, 0.0), axis=1, keepdims=True)
    gcol = jnp.sum(jnp.where(lane == h + DN_HEADS, gates, 0.0), axis=1, keepdims=True)
    return gcol, bcol


DN_HB = 8


def _dn_prep_fwd(qkv, gates):
    L = qkv.shape[0]
    N, H, d, HB = L // CHUNK, DN_HEADS, DN_HEAD_DIM, DN_HB

    def body(q_ref, k_ref, v_ref, g_ref, u_ref, w_ref, qd_ref, kd_ref, qk_ref, egl_ref):
        h0 = pl.program_id(1) * HB
        gates_ = g_ref[...]
        lanes_of = [pl.ds(i * d, d) for i in range(HB)]
        cols = [_gate_cols(gates_, h0 + i) for i in range(HB)]
        outs = _prep_math([q_ref[:, l] for l in lanes_of], [k_ref[:, l] for l in lanes_of], [v_ref[:, l] for l in lanes_of],
                          [c[0] for c in cols], [c[1] for c in cols])
        for i in range(HB):
            lanes = lanes_of[i]
            u, w, qd, kd, qk, egl = outs[i]
            u_ref[:, lanes] = u
            w_ref[:, lanes] = w
            qd_ref[:, lanes] = qd
            kd_ref[:, lanes] = kd
            qk_ref[0, i] = qk
            egl_ref[0, i] = jnp.broadcast_to(egl, (8, d))

    blk = lambda off: pl.BlockSpec((CHUNK, HB * d), lambda n, j: (n, off // HB + j))
    cc = pl.BlockSpec((1, HB, CHUNK, CHUNK), lambda n, j: (n, j, 0, 0))
    ee = pl.BlockSpec((1, HB, 8, d), lambda n, j: (n, j, 0, 0))
    big = _sds((L, D_DN))
    return pl.pallas_call(
        body, name="dn_prep_fwd", grid=(N, H // HB),
        in_specs=[blk(0), blk(H), blk(2 * H), pl.BlockSpec((CHUNK, 128), lambda n, j: (n, 0))],
        out_specs=[blk(0), blk(0), blk(0), blk(0), cc, ee],
        out_shape=[big, big, big, big, _sds((N, H, CHUNK, CHUNK)), _sds((N, H, 8, d))],
        compiler_params=_cp(("parallel", "parallel")),
    )(qkv, qkv, qkv, gates)


def _dn_scan_fwd(u, w, qd, kd, qk, egl):
    L = u.shape[0]
    N, H, d, HB = L // CHUNK, DN_HEADS, DN_HEAD_DIM, DN_HB

    def body(u_ref, w_ref, qd_ref, kd_ref, qk_ref, egl_ref, o_ref, st_ref, s_ref):
        n, h0 = pl.program_id(0), pl.program_id(1) * HB

        @pl.when(n == 0)
        def _():
            for i in range(HB):
                s_ref[h0 + i] = jnp.zeros((d, d), F32)

        hs = range(HB)
        ln = [pl.ds(i * d, d) for i in hs]
        st = [s_ref[h0 + i] for i in hs]
        ws = [_bdot(w_ref[:, ln[i]], st[i], _NN) for i in hs]
        qs = [_bdot(qd_ref[:, ln[i]], st[i], _NN) for i in hs]
        vn = [u_ref[:, ln[i]] - ws[i] for i in hs]
        qv = [_bdot(qk_ref[0, i], vn[i], _NN) for i in hs]
        kv = [_bdot(kd_ref[:, ln[i]], vn[i], _TN) for i in hs]
        for i in hs:
            st_ref[0, i] = st[i]
            o_ref[:, ln[i]] = qs[i] + qv[i]
            s_ref[h0 + i] = st[i] * egl_ref[0, i, pl.ds(0, 1), :] + kv[i]

    blk = pl.BlockSpec((CHUNK, HB * d), lambda n, j: (n, j))
    cc = pl.BlockSpec((1, HB, CHUNK, CHUNK), lambda n, j: (n, j, 0, 0))
    ee = pl.BlockSpec((1, HB, 8, d), lambda n, j: (n, j, 0, 0))
    return pl.pallas_call(
        body, name="dn_scan_fwd", grid=(N, H // HB), in_specs=[blk, blk, blk, blk, cc, ee],
        out_specs=[blk, pl.BlockSpec((1, HB, d, d), lambda n, j: (n, j, 0, 0))],
        out_shape=[_sds((L, D_DN)), _sds((N, H, d, d))], scratch_shapes=[pltpu.VMEM((H, d, d), F32)],
        compiler_params=_cp(("arbitrary", "arbitrary")),
    )(u, w, qd, kd, qk, egl)


def _dn_scan_bwd(u, w, qd, kd, qk, egl, states, do):
    L = u.shape[0]
    N, H, d, HB = L // CHUNK, DN_HEADS, DN_HEAD_DIM, DN_HB

    def body(u_ref, w_ref, qd_ref, kd_ref, qk_ref, egl_ref, st_ref, do_ref,
             du_ref, dw_ref, dqd_ref, dkd_ref, dqk_ref, degl_ref, ds_ref):
        n, h0 = pl.program_id(0), pl.program_id(1) * HB

        @pl.when(n == 0)
        def _():
            for i in range(HB):
                ds_ref[h0 + i] = jnp.zeros((d, d), F32)

        hs = range(HB)
        ln = [pl.ds(i * d, d) for i in hs]
        st = [st_ref[0, i] for i in hs]
        dsn = [ds_ref[h0 + i] for i in hs]
        do_ = [do_ref[:, ln[i]] for i in hs]
        ws = [_bdot(w_ref[:, ln[i]], st[i], _NN) for i in hs]
        d1 = [_bdot(qk_ref[0, i], do_[i], _TN) for i in hs]
        d2 = [_bdot(kd_ref[:, ln[i]], dsn[i], _NN) for i in hs]
        dqd = [_bdot(do_[i], st[i], _NT) for i in hs]
        qdo = [_bdot(qd_ref[:, ln[i]], do_[i], _TN) for i in hs]
        vn = [u_ref[:, ln[i]] - ws[i] for i in hs]
        dvn = [d1[i] + d2[i] for i in hs]
        dw = [_bdot(dvn[i], st[i], _NT) for i in hs]
        dkd = [_bdot(vn[i], dsn[i], _NT) for i in hs]
        dqk = [_bdot(do_[i], vn[i], _NT) for i in hs]
        wdv = [_bdot(w_ref[:, ln[i]], dvn[i], _TN) for i in hs]
        for i in hs:
            du_ref[:, ln[i]] = dvn[i]
            dw_ref[:, ln[i]] = -dw[i]
            dqd_ref[:, ln[i]] = dqd[i]
            dkd_ref[:, ln[i]] = dkd[i]
            dqk_ref[0, i] = dqk[i]
            degl_ref[0, i] = jnp.broadcast_to(jnp.sum(dsn[i] * st[i], keepdims=True), (8, d))
            ds_ref[h0 + i] = (qdo[i] - wdv[i]) + dsn[i] * egl_ref[0, i, pl.ds(0, 1), :]

    blk = pl.BlockSpec((CHUNK, HB * d), lambda n, j: (N - 1 - n, j))
    cc = pl.BlockSpec((1, HB, CHUNK, CHUNK), lambda n, j: (N - 1 - n, j, 0, 0))
    ee = pl.BlockSpec((1, HB, 8, d), lambda n, j: (N - 1 - n, j, 0, 0))
    ss = pl.BlockSpec((1, HB, d, d), lambda n, j: (N - 1 - n, j, 0, 0))
    big = _sds((L, D_DN))
    return pl.pallas_call(
        body, name="dn_scan_bwd", grid=(N, H // HB), in_specs=[blk, blk, blk, blk, cc, ee, ss, blk],
        out_specs=[blk, blk, blk, blk, cc, ee],
        out_shape=[big, big, big, big, _sds((N, H, CHUNK, CHUNK)), _sds((N, H, 8, d))],
        scratch_shapes=[pltpu.VMEM((H, d, d), F32)], compiler_params=_cp(("arbitrary", "arbitrary")),
    )(u, w, qd, kd, qk, egl, states, do)


def _dn_prep_bwd(qkv, gates, du, dw, dqd, dkd, dqk, degl):
    L = qkv.shape[0]
    N, H, d, HB = L // CHUNK, DN_HEADS, DN_HEAD_DIM, DN_HB

    def body(q_ref, k_ref, v_ref, g_ref, du_ref, dw_ref, dqd_ref, dkd_ref, dqk_ref, degl_ref, dq_ref, dk_ref, dv_ref, dg_ref):
        j = pl.program_id(1)
        h0 = j * HB
        gates_ = g_ref[...]
        lane = lax.broadcasted_iota(jnp.int32, gates_.shape, 1)
        lane1 = lax.broadcasted_iota(jnp.int32, (1, d), 1)
        part = jnp.zeros(gates_.shape, F32)
        lanes_of = [pl.ds(i * d, d) for i in range(HB)]
        cols = [_gate_cols(gates_, h0 + i) for i in range(HB)]
        _, f = jax.vjp(_prep_math, [q_ref[:, l] for l in lanes_of], [k_ref[:, l] for l in lanes_of],
                       [v_ref[:, l] for l in lanes_of], [c[0] for c in cols], [c[1] for c in cols])
        cots = tuple((du_ref[:, l], dw_ref[:, l], dqd_ref[:, l], dkd_ref[:, l], dqk_ref[0, i],
                      jnp.where(lane1 == 0, degl_ref[0, i, pl.ds(0, 1), :], 0.0)) for i, l in enumerate(lanes_of))
        dqs, dks, dvs, dgcs, dbcs = f(cots)
        for i in range(HB):
            lanes = lanes_of[i]
            dq_ref[:, lanes] = dqs[i]
            dk_ref[:, lanes] = dks[i]
            dv_ref[:, lanes] = dvs[i]
            part = part + jnp.where(lane == h0 + i, dbcs[i], 0.0) + jnp.where(lane == h0 + i + DN_HEADS, dgcs[i], 0.0)

        @pl.when(j == 0)
        def _():
            dg_ref[...] = part

        @pl.when(j > 0)
        def _():
            dg_ref[...] += part

    blk = lambda off: pl.BlockSpec((CHUNK, HB * d), lambda n, j: (n, off // HB + j))
    gsp = pl.BlockSpec((CHUNK, 128), lambda n, j: (n, 0))
    cc = pl.BlockSpec((1, HB, CHUNK, CHUNK), lambda n, j: (n, j, 0, 0))
    ee = pl.BlockSpec((1, HB, 8, d), lambda n, j: (n, j, 0, 0))
    big = _sds((L, D_DN))
    return pl.pallas_call(
        body, name="dn_prep_bwd", grid=(N, H // HB),
        in_specs=[blk(0), blk(H), blk(2 * H), gsp, blk(0), blk(0), blk(0), blk(0), cc, ee],
        out_specs=[blk(0), blk(0), blk(0), gsp], out_shape=[big, big, big, _sds((L, 128))],
        compiler_params=_cp(("parallel", "arbitrary")),
    )(qkv, qkv, qkv, gates, du, dw, dqd, dkd, dqk, degl)


def _dn_post_fwd(o, proj, nw):
    L = o.shape[0]
    d = DN_HEAD_DIM
    tm = min(512, L)

    def body(o_ref, z_ref, w_ref, y_ref):
        ov = o_ref[...]
        r = lax.rsqrt(jnp.mean(ov * ov, axis=-1, keepdims=True) + EPS)
        y_ref[...] = (ov * r * w_ref[...] * _silu(z_ref[...])).astype(BF16)

    blk = pl.BlockSpec((tm, d), lambda i, h: (i, h))
    return pl.pallas_call(
        body, name="dn_post_fwd", grid=(L // tm, DN_HEADS),
        in_specs=[blk, pl.BlockSpec((tm, d), lambda i, h: (i, OFF_ZD // d + h)), pl.BlockSpec((1, d), lambda i, h: (0, 0))],
        out_specs=blk, out_shape=_sds((L, D_DN), BF16), compiler_params=_cp(("parallel", "parallel")),
    )(o, proj, nw)


def _dn_post_bwd(o, proj, nw, dy):
    L = o.shape[0]
    d = DN_HEAD_DIM
    tm = min(512, L)

    def body(o_ref, z_ref, w_ref, dy_ref, do_ref, dz_ref, dw_ref):
        first = jnp.logical_and(pl.program_id(0) == 0, pl.program_id(1) == 0)
        ov, z, w, dyv = o_ref[...], z_ref[...], w_ref[...], dy_ref[...]
        r = lax.rsqrt(jnp.mean(ov * ov, axis=-1, keepdims=True) + EPS)
        xn = ov * r
        dz_ref[...] = (dyv * xn * w * _dsilu(z)).astype(BF16)
        dn = dyv * _silu(z)
        t = dn * w
        do_ref[...] = r * t - ov * (r * r * r) * jnp.mean(t * ov, axis=-1, keepdims=True)
        part = jnp.sum(dn * xn, axis=0, keepdims=True)

        @pl.when(first)
        def _():
            dw_ref[...] = part

        @pl.when(jnp.logical_not(first))
        def _():
            dw_ref[...] += part

    blk = pl.BlockSpec((tm, d), lambda i, h: (i, h))
    one = pl.BlockSpec((1, d), lambda i, h: (0, 0))
    return pl.pallas_call(
        body, name="dn_post_bwd", grid=(L // tm, DN_HEADS),
        in_specs=[blk, pl.BlockSpec((tm, d), lambda i, h: (i, OFF_ZD // d + h)), one, blk], out_specs=[blk, blk, one],
        out_shape=[_sds((L, D_DN)), _sds((L, D_DN), BF16), _sds((1, d))], compiler_params=_cp(("arbitrary", "arbitrary")),
    )(o, proj, nw, dy)


def _mix_fwd(s5o, dno, w_su, w_du, proj):
    L, K = s5o.shape
    N = w_su.shape[1]
    tm, tn = min(512, L), 512

    def body(a1, a2, b1, b2, gs, gd, ys_ref, yd_ref, mx_ref):
        ys = jnp.dot(a1[...], b1[...], preferred_element_type=F32)
        yd = jnp.dot(a2[...], b2[...], preferred_element_type=F32)
        ys_ref[...] = ys
        yd_ref[...] = yd
        mx_ref[...] = (_sigmoid(gs[...]) * ys + _sigmoid(gd[...]) * yd).astype(BF16)

    a = pl.BlockSpec((tm, K), lambda i, j: (i, 0))
    b = pl.BlockSpec((K, tn), lambda i, j: (0, j))
    o = pl.BlockSpec((tm, tn), lambda i, j: (i, j))
    return pl.pallas_call(
        body, name="mix_fwd", grid=(L // tm, N // tn),
        in_specs=[a, a, b, b, pl.BlockSpec((tm, tn), lambda i, j: (i, OFF_GS // tn + j)),
                  pl.BlockSpec((tm, tn), lambda i, j: (i, OFF_GD // tn + j))],
        out_specs=[o, o, o], out_shape=[_sds((L, N)), _sds((L, N)), _sds((L, N), BF16)],
        compiler_params=_cp(("parallel", "parallel")),
    )(s5o, dno, w_su, w_du, proj, proj)


def _mix_bwd(dx2b, w_out, proj, ys, yd):
    L, K = dx2b.shape
    N = w_out.shape[0]
    tm, tn = min(512, L), 512

    def body(a, b, gs, gd, ys_ref, yd_ref, dgs_ref, dgd_ref, dys_ref, dyd_ref):
        dm = lax.dot_general(a[...], b[...], (((1,), (1,)), ((), ())), preferred_element_type=F32)
        ss, sd = _sigmoid(gs[...]), _sigmoid(gd[...])
        dys_ref[...] = (dm * ss).astype(BF16)
        dyd_ref[...] = (dm * sd).astype(BF16)
        dgs_ref[...] = (dm * ys_ref[...] * ss * (1.0 - ss)).astype(BF16)
        dgd_ref[...] = (dm * yd_ref[...] * sd * (1.0 - sd)).astype(BF16)

    o = pl.BlockSpec((tm, tn), lambda i, j: (i, j))
    return pl.pallas_call(
        body, name="mix_bwd", grid=(L // tm, N // tn),
        in_specs=[pl.BlockSpec((tm, K), lambda i, j: (i, 0)), pl.BlockSpec((tn, K), lambda i, j: (j, 0)),
                  pl.BlockSpec((tm, tn), lambda i, j: (i, OFF_GS // tn + j)),
                  pl.BlockSpec((tm, tn), lambda i, j: (i, OFF_GD // tn + j)), o, o],
        out_specs=[o, o, o, o], out_shape=[_sds((L, N), BF16)] * 4, compiler_params=_cp(("parallel", "parallel")),
    )(dx2b, w_out, proj, proj, ys, yd)


def _final(mixed, w_out, x, tgt, fw):
    L, D = x.shape
    tm = min(256, L)

    def body(a_ref, b_ref, x_ref, t_ref, w_ref, dx_ref, dxb_ref, loss_ref, dw_ref):
        i = pl.program_id(0)
        x2 = x_ref[...] + jnp.dot(a_ref[...], b_ref[...], preferred_element_type=F32)
        w = w_ref[...]
        r = lax.rsqrt(jnp.mean(x2 * x2, axis=-1, keepdims=True) + EPS)
        xn = x2 * r
        e = xn * w - t_ref[...]
        lpart = 0.5 * jnp.sum(jnp.mean(e * e, axis=-1, keepdims=True), axis=0, keepdims=True)
        dy = e * (1.0 / D)
        t = dy * w
        dx2 = r * t - x2 * (r * r * r) * jnp.mean(t * x2, axis=-1, keepdims=True)
        dx_ref[...] = dx2
        dxb_ref[...] = dx2.astype(BF16)
        dwp = jnp.sum(dy * xn, axis=0, keepdims=True)
        lrow = jnp.broadcast_to(lpart, loss_ref.shape)

        @pl.when(i == 0)
        def _():
            loss_ref[...] = lrow
            dw_ref[...] = dwp

        @pl.when(i > 0)
        def _():
            loss_ref[...] += lrow
            dw_ref[...] += dwp

    row = pl.BlockSpec((tm, D), lambda i: (i, 0))
    one = pl.BlockSpec((1, D), lambda i: (0, 0))
    return pl.pallas_call(
        body, name="final", grid=(L // tm,),
        in_specs=[row, pl.BlockSpec((D, D), lambda i: (0, 0)), row, row, one],
        out_specs=[row, row, pl.BlockSpec((1, 128), lambda i: (0, 0)), one],
        out_shape=[_sds((L, D)), _sds((L, D), BF16), _sds((1, 128)), _sds((1, D))], compiler_params=_cp(("arbitrary",)),
    )(mixed, w_out, x, tgt, fw)


def _block_diag(t):
    J, g, a, b = t.shape
    eye = jnp.eye(g, dtype=t.dtype)
    return (t[:, :, :, None, :] * eye[None, :, None, :, None]).reshape(J, g * a, g * b)


def _block_diag_take(m, g):
    J, ga, gb = m.shape
    a, b = ga // g, gb // g
    m5 = m.reshape(J, g, a, g, b)
    idx = jnp.arange(g)
    return m5[:, idx, :, idx, :].transpose(1, 0, 2, 3)


class _PlainOps:
    def __init__(self, w_rest):
        self.w_rest = w_rest

    def in_proj(self, h, w_perm):
        return _mm(h, w_perm, name="in_proj", tm=1024, tn=1152), self.w_rest

    def rest_grads(self, d_w_glu, d_w_su, d_w_du, d_w_out):
        pass

    def d_w_in(self, h, dproj):
        return _mm(h, dproj, ta=True, name="d_w_in", tm=1024, tn=1152)

    def d_h(self, dproj, w_perm, d_w_perm):
        return _mm(dproj, w_perm, tb=True, name="d_h", tm=2048, tn=1024, tk=1152)


def _local_step(x, tgt, ln_w, w_perm, lam_re, lam_im, log_step, b_re, b_im, c_re, c_im, s5_d,
                conv_w, a_log, dt_bias, norm_w, fw, ops):
    G, P, gb = S5_GROUPS, S5_STATE, S5_GROUPS // S5_BLOCKS
    h, rstd = _ln_fwd(x, ln_w)
    proj, (w_glu, w_su, w_du, w_out) = ops.in_proj(h, w_perm)

    b_re2, b_im2 = b_re.reshape(G, P * S5_GROUP), b_im.reshape(G, P * S5_GROUP)
    ls2 = log_step.reshape(G, 1)
    abar_re, abar_im, bb_re, bb_im = _s5_param_fwd(lam_re, lam_im, ls2, b_re2, b_im2)

    def to_wb(bb):
        return _block_diag(bb.reshape(S5_BLOCKS, gb, P, S5_GROUP).transpose(0, 1, 3, 2)).astype(BF16)

    def to_cb(cc):
        return _block_diag(cc.reshape(S5_BLOCKS, gb, S5_GROUP, P).transpose(0, 1, 3, 2)).astype(BF16)

    wbr, wbi, cbr, cbi = to_wb(bb_re), to_wb(bb_im), to_cb(c_re), to_cb(c_im)
    a_re_row, a_im_row = abar_re.reshape(1, G * P), abar_im.reshape(1, G * P)
    yc = _s5_core_fwd(proj, wbr, wbi, a_re_row, a_im_row, cbr, cbi)
    s5o = _s5_post_fwd(yc, proj, s5_d, w_glu)

    pad = lambda v: jnp.pad(v, ((0, 0), (DN_HEADS, 128 - 2 * DN_HEADS)))
    alog_row, dtb_row = pad(a_log), pad(dt_bias)
    qkv = _dn_conv_fwd(proj, conv_w)
    gates = _dn_gates_fwd(proj, alog_row, dtb_row)
    prep = _dn_prep_fwd(qkv, gates)
    o_dn, states = _dn_scan_fwd(*prep)
    dno = _dn_post_fwd(o_dn, proj, norm_w)

    ys, yd, mixed = _mix_fwd(s5o, dno, w_su, w_du, proj)
    dx2, dx2b, loss_row, d_fw = _final(mixed, w_out, x, tgt, fw)
    d_w_out = _mm(mixed, dx2b, ta=True, name="d_w_out")
    dgs, dgd, dys, dyd = _mix_bwd(dx2b, w_out, proj, ys, yd)
    d_w_su = _mm(s5o, dys, ta=True, name="d_w_su", shard_out=True)
    d_w_du = _mm(dno, dyd, ta=True, name="d_w_du", shard_out=True)
    ds5o = _mm(dys, w_su, tb=True, name="d_s5o")
    ddno = _mm(dyd, w_du, tb=True, name="d_dno")

    dyc, du1, dz_s, d_s5d, d_w_glu = _s5_post_bwd(yc, proj, s5_d, w_glu, ds5o)
    ops.rest_grads(d_w_glu, d_w_su, d_w_du, d_w_out)
    du, dwbr, dwbi, dcbr, dcbi, dar, dai = _s5_core_bwd(proj, wbr, wbi, a_re_row, a_im_row, cbr, cbi, dyc, du1)

    def from_wb(dwb):
        return _block_diag_take(dwb, gb).transpose(0, 1, 3, 2).reshape(G, P * S5_GROUP)

    def from_cb(dcb):
        return _block_diag_take(dcb, gb).transpose(0, 1, 3, 2).reshape(G, S5_GROUP, P)

    d_lam_re, d_lam_im, d_ls, d_b_re, d_b_im = _s5_param_bwd(
        lam_re, lam_im, ls2, b_re2, b_im2, dar.reshape(G, P), dai.reshape(G, P), from_wb(dwbr), from_wb(dwbi))

    do_dn, dz_d, d_norm_w = _dn_post_bwd(o_dn, proj, norm_w, ddno)
    dq, dk, dv, dgates = _dn_prep_bwd(qkv, gates, *_dn_scan_bwd(*prep, states, do_dn))
    dqkv, d_conv = _dn_conv_bwd(proj, conv_w, jnp.concatenate([dq, dk, dv], axis=1))
    dpb, d_alog_row, d_dtb_row = _dn_gates_bwd(proj, alog_row, dtb_row, dgates)

    dproj = jnp.concatenate([du, dz_s, dqkv, dz_d, dgs, dgd, dpb], axis=1)
    d_w_perm = ops.d_w_in(h, dproj)
    dh = ops.d_h(dproj, w_perm, d_w_perm)
    grad_x, d_ln_w = _ln_bwd(x, rstd, ln_w, dh, dx2)

    grads = dict(
        ln_w=d_ln_w, w_perm=d_w_perm, s5_lam_re=d_lam_re, s5_lam_im=d_lam_im, s5_log_step=d_ls.reshape(1, G),
        s5_b_re=d_b_re.reshape(G, P, S5_GROUP), s5_b_im=d_b_im.reshape(G, P, S5_GROUP),
        s5_c_re=from_cb(dcbr), s5_c_im=from_cb(dcbi), s5_d=d_s5d, s5_w_glu=d_w_glu, s5_w_up=d_w_su,
        dn_conv_w=d_conv, dn_a_log=d_alog_row[:, DN_HEADS:2 * DN_HEADS], dn_dt_bias=d_dtb_row[:, DN_HEADS:2 * DN_HEADS],
        dn_norm_w=d_norm_w, dn_w_up=d_w_du, w_out=d_w_out, final_norm_w=d_fw)
    return loss_row, grad_x, grads


def _place():
    x, y, c = lax.axis_index("x"), lax.axis_index("y"), lax.axis_index("c")
    return x, y, c


def _remote(src, dst, send_sem, recv_sem, to):
    return pltpu.make_async_remote_copy(src_ref=src, dst_ref=dst, send_sem=send_sem, recv_sem=recv_sem,
                                        device_id=to, device_id_type=MESH)


def _gather_exchange(shards, whole=()):
    na, nw = len(shards), len(whole)

    def plan(ins, outs, send_sems, recv_sems, local_sems, receiving):
        x, y, c = _place()
        me = 2 * x + y
        sibling = (x, y, 1 - c)
        chips = [(1 - x, y), (x, 1 - y), (1 - x, 1 - y)]

        def part(a, chip, half):
            r2 = shards[a].shape[0] // 2
            return outs[a].at[chip, pl.ds(half * r2, r2)]

        own = [pltpu.make_async_copy(ins[a], outs[a].at[me], local_sems.at[a]) for a in range(na + nw)]
        sends, landed, passed, arrivals = [], [], [], []
        for a in range(na):
            r2 = shards[a].shape[0] // 2
            for j, (px, py) in enumerate(chips):
                k = 6 * a + j
                sends.append(_remote(ins[a].at[pl.ds(c * r2, r2)], part(a, me, c), send_sems.at[k], recv_sems.at[k], (px, py, c)))
                if receiving:
                    got, other = part(a, 2 * px + py, c), part(a, 2 * px + py, 1 - c)
                    landed.append(_remote(got, got, send_sems.at[k], recv_sems.at[k], (px, py, c)))
                    passed.append(_remote(got, got, send_sems.at[k + 3], recv_sems.at[k + 3], sibling))
                    arrivals.append(_remote(other, other, send_sems.at[k + 3], recv_sems.at[k + 3], sibling))
        for a in range(na, na + nw):
            for j, (px, py) in enumerate(chips):
                k = 6 * na + 3 * (a - na) + j
                sends.append(_remote(ins[a], outs[a].at[me], send_sems.at[k], recv_sems.at[k], (px, py, c)))
                if receiving:
                    arrivals.append(_remote(ins[a], outs[a].at[2 * px + py], send_sems.at[k], recv_sems.at[k], (px, py, c)))
        return own, sends, landed, passed, arrivals

    def start(ins, outs, *sems):
        own, sends, _, _, _ = plan(ins, outs, *sems, False)
        for cp in own + sends:
            cp.start()

    def finish(ins, outs, *sems):
        own, sends, landed, passed, arrivals = plan(ins, outs, *sems, True)
        for got, fwd in zip(landed, passed):
            got.wait_recv()
            fwd.start()
        for cp in arrivals:
            cp.wait_recv()
        for cp in sends + passed:
            cp.wait_send()
        for cp in own:
            cp.wait()

    arrays = list(shards) + list(whole)
    return _Exchange(arrays, [_sds((N_CHIPS,) + s.shape, s.dtype) for s in arrays], 6 * na + 3 * nw, start, finish)


def _owners_exchange(csbs):
    na = len(csbs)

    def plan(ins, outs, send_sems, recv_sems, local_sems, receiving):
        x, y, c = _place()
        me = 2 * x + y
        sends, arrivals = [], []
        for a in range(na):
            for k in range(N_CHIPS - 1):
                j = (me + 1 + k) % N_CHIPS
                sends.append(_remote(ins[a].at[k], outs[a].at[2 - k], send_sems.at[3 * a + k], recv_sems.at[3 * a + 2 - k],
                                     (j // 2, j % 2, c)))
                if receiving:
                    arrivals.append(_remote(ins[a].at[k], outs[a].at[k], send_sems.at[3 * a + k], recv_sems.at[3 * a + k], (x, y, c)))
        return sends, arrivals

    def start(ins, outs, *sems):
        for cp in plan(ins, outs, *sems, False)[0]:
            cp.start()

    def finish(ins, outs, *sems):
        sends, arrivals = plan(ins, outs, *sems, True)
        for cp in arrivals:
            cp.wait_recv()
        for cp in sends:
            cp.wait_send()

    return _Exchange(csbs, [_sds(g.shape, g.dtype) for g in csbs], 3 * na, start, finish)


def _run_exchange(ex, name):
    n_in, n_out = len(ex.ins), len(ex.out_shapes)

    def body(*refs):
        ins, outs, sems = refs[:n_in], refs[n_in:n_in + n_out], refs[n_in + n_out:]
        ex.start(ins, outs, *sems)
        ex.finish(ins, outs, *sems)

    return pl.pallas_call(
        body, name=name, in_specs=[ANY] * n_in, out_specs=[ANY] * n_out, out_shape=ex.out_shapes,
        scratch_shapes=[pltpu.SemaphoreType.DMA((ex.n_sems,)) for _ in range(3)],
    )(*ex.ins)


def _swap_halves(gxs, name):
    na = len(gxs)

    def body(*refs):
        ins, outs = refs[:na], refs[na:2 * na]
        send_sems, recv_sems = refs[2 * na:]
        x, y, c = _place()
        cps = [_remote(ins[a].at[pl.ds(0, N_CHIPS), pl.ds(1 - c, 1)], outs[a], send_sems.at[a], recv_sems.at[a], (x, y, 1 - c))
               for a in range(na)]
        for cp in cps:
            cp.start()
        for cp in cps:
            cp.wait()

    return pl.pallas_call(
        body, name=name, in_specs=[ANY] * na, out_specs=[ANY] * na,
        out_shape=[_sds((N_CHIPS, 1) + g.shape[2:], g.dtype) for g in gxs],
        scratch_shapes=[pltpu.SemaphoreType.DMA((na,)), pltpu.SemaphoreType.DMA((na,))],
    )(*gxs)


def _share_halves(gfs):
    na = len(gfs)

    def body(*refs):
        ins, outs = refs[:na], refs[na:2 * na]
        send_sems, recv_sems = refs[2 * na:]
        x, y, c = _place()
        cps = [_remote(ins[a].at[pl.ds(c, 1)], outs[a].at[pl.ds(c, 1)], send_sems.at[a], recv_sems.at[a], (x, y, 1 - c))
               for a in range(na)]
        for cp in cps:
            cp.start()
        for a in range(na):
            cps[a].wait_send()
            _remote(ins[a].at[pl.ds(1 - c, 1)], outs[a].at[pl.ds(1 - c, 1)], send_sems.at[a], recv_sems.at[a], (x, y, 1 - c)).wait_recv()

    return pl.pallas_call(
        body, name="rs_share_halves", in_specs=[ANY] * na, out_specs=[ANY] * na,
        out_shape=[_sds(g.shape, g.dtype) for g in gfs], input_output_aliases={a: a for a in range(na)},
        scratch_shapes=[pltpu.SemaphoreType.DMA((na,)), pltpu.SemaphoreType.DMA((na,))],
    )(*gfs)


def _row_tile(rows, cols, budget=1 << 20):
    t = rows
    while t % 2 == 0 and t > 16 and t * cols * 4 > budget:
        t //= 2
    return t


def _chip_sums(gx, r1, where):
    _, _, r2, cd = gx.shape
    tr = _row_tile(r2, cd)

    def body(w_ref, a_ref, b_ref, o_ref):
        o_ref[...] = (a_ref[0] + b_ref[0]).astype(BF16)

    other = lambda k, i, w: ((w[1] + 1 + k) % N_CHIPS, w[0], i, 0)
    other0 = lambda k, i, w: ((w[1] + 1 + k) % N_CHIPS, 0, i, 0)
    return pl.pallas_call(
        body, name="rs_chip_sums",
        grid_spec=pltpu.PrefetchScalarGridSpec(
            num_scalar_prefetch=1, grid=(N_CHIPS - 1, r2 // tr),
            in_specs=[pl.BlockSpec((1, 1, tr, cd), other), pl.BlockSpec((1, 1, tr, cd), other0)],
            out_specs=pl.BlockSpec((1, tr, cd), lambda k, i, w: (k, i, 0))),
        out_shape=_sds((N_CHIPS - 1, r2, cd), BF16), compiler_params=_cp(("parallel", "parallel")),
    )(where, gx, r1)


def _owner_sum(gx, r1, r2x, where):
    _, _, r2, cd = gx.shape
    tr = _row_tile(r2, cd)

    def body(w_ref, a_ref, b_ref, r_ref, o_ref):
        acc = a_ref[0, 0] + b_ref[0, 0]
        for k in range(N_CHIPS - 1):
            acc = acc + r_ref[k].astype(F32)
        o_ref[0] = acc

    return pl.pallas_call(
        body, name="rs_owner_sum",
        grid_spec=pltpu.PrefetchScalarGridSpec(
            num_scalar_prefetch=1, grid=(r2 // tr,),
            in_specs=[pl.BlockSpec((1, 1, tr, cd), lambda i, w: (w[1], w[0], i, 0)),
                      pl.BlockSpec((1, 1, tr, cd), lambda i, w: (w[1], 0, i, 0)),
                      pl.BlockSpec((N_CHIPS - 1, tr, cd), lambda i, w: (0, i, 0))],
            out_specs=pl.BlockSpec((1, tr, cd), lambda i, w: (w[0], i, 0))),
        out_shape=_sds((2, r2, cd)), compiler_params=_cp(("parallel",)),
    )(where, gx, r1, r2x)


def _adamw_math(w, g, m, v):
    m = ADAM_B1 * m + (1.0 - ADAM_B1) * g
    v = ADAM_B2 * v + (1.0 - ADAM_B2) * (g * g)
    m_hat = m / (1.0 - ADAM_B1 ** ADAM_STEP)
    v_hat = v / (1.0 - ADAM_B2 ** ADAM_STEP)
    delta = -ADAM_LR * (m_hat / (jnp.sqrt(v_hat) + ADAM_EPS) + ADAM_WD * w)
    return delta, m, v


def _adamw(w, g, m, v, name):
    rows, cd = w.shape
    tr = _row_tile(rows, cd, budget=3 << 19) if rows % 16 == 0 else rows

    def body(w_ref, g_ref, m_ref, v_ref, d_ref, mo_ref, vo_ref):
        d, mm, vv = _adamw_math(w_ref[...], g_ref[...], m_ref[...], v_ref[...])
        d_ref[...] = d
        mo_ref[...] = mm
        vo_ref[...] = vv

    blk = pl.BlockSpec((tr, cd), lambda i: (i, 0))
    return pl.pallas_call(
        body, name=name, grid=(rows // tr,), in_specs=[blk] * 4, out_specs=[blk] * 3, out_shape=[_sds(w.shape)] * 3,
        compiler_params=_cp(("parallel",)),
    )(w, g, m, v)


def _small_allreduce_adamw(gp, wp, mp, vp):
    R = gp.shape[0]
    R2 = R // 2
    assert R2 % 8 == 0

    def body(g_ref, w_ref, m_ref, v_ref, go_ref, d_ref, mo_ref, vo_ref, sib, csum, land, send_sems, recv_sems):
        x, y, c = _place()
        me = 2 * x + y
        sibling = (x, y, 1 - c)
        chips = [(1 - x, y), (x, 1 - y), (1 - x, 1 - y)]
        swap = _remote(g_ref, sib, send_sems.at[0], recv_sems.at[0], sibling)
        swap.start()
        swap.wait()
        csum[...] = g_ref[...] + sib[...]
        half = csum.at[pl.ds(c * R2, R2)]
        land[me] = csum[pl.ds(c * R2, R2), :]
        cps = [_remote(half, land.at[me], send_sems.at[1 + j], recv_sems.at[1 + j], (px, py, c))
               for j, (px, py) in enumerate(chips)]
        for cp in cps:
            cp.start()
        for j, (px, py) in enumerate(chips):
            _remote(half, land.at[2 * px + py], send_sems.at[1 + j], recv_sems.at[1 + j], (px, py, c)).wait_recv()
        for cp in cps:
            cp.wait_send()
        mine = go_ref.at[pl.ds(c * R2, R2)]
        go_ref[pl.ds(c * R2, R2), :] = (land[0] + land[1]) + (land[2] + land[3])
        share = _remote(mine, mine, send_sems.at[4], recv_sems.at[4], sibling)
        share.start()
        share.wait_send()
        other = go_ref.at[pl.ds((1 - c) * R2, R2)]
        _remote(other, other, send_sems.at[4], recv_sems.at[4], sibling).wait_recv()
        d, mm, vv = _adamw_math(w_ref[...], go_ref[...], m_ref[...], v_ref[...])
        d_ref[...] = d
        mo_ref[...] = mm
        vo_ref[...] = vv

    vm = pl.BlockSpec(memory_space=pltpu.VMEM)
    return pl.pallas_call(
        body, name="small_allreduce_adamw", in_specs=[vm] * 4, out_specs=[vm] * 4, out_shape=[_sds((R, 128))] * 4,
        scratch_shapes=[pltpu.VMEM((R, 128), F32), pltpu.VMEM((R, 128), F32), pltpu.VMEM((N_CHIPS, R2, 128), F32),
                        pltpu.SemaphoreType.DMA((5,)), pltpu.SemaphoreType.DMA((5,))],
        compiler_params=_cp(),
    )(gp, wp, mp, vp)


def _pack(arrs):
    rows = []
    for a in arrs:
        f = a.reshape(-1)
        f = jnp.pad(f, (0, (-f.shape[0]) % 128))
        rows.append(f.reshape(-1, 128))
    p = jnp.concatenate(rows, axis=0)
    return jnp.pad(p, ((0, (-p.shape[0]) % 8), (0, 0)))


def _unpack(p, shapes):
    out, r = [], 0
    for s in shapes:
        n = math.prod(s)
        nr = -(-n // 128)
        out.append(p[r:r + nr].reshape(-1)[:n].reshape(s))
        r += nr
    return out


class _ExchangeOps(_PlainOps):
    def __init__(self, rest_shards, where):
        self.rest_shards, self.where = rest_shards, where
        self.reduced = []

    def in_proj(self, h, w_perm):
        proj, g_glu, g_su, g_du, g_out = _mm(h, w_perm, name="in_proj", tm=1024, tn=1152,
                                             exchange=_gather_exchange(self.rest_shards))
        cat = lambda g: jnp.concatenate([g[j] for j in range(N_CHIPS)], axis=1)
        return proj, (g_glu.reshape(D_S5, D_S5), cat(g_su), cat(g_du), g_out.reshape(D_MODEL, D_MODEL))

    def _chip_sums(self, gxs, name):
        r1s = _swap_halves(gxs, name)
        return r1s, [_chip_sums(gx, r1, self.where) for gx, r1 in zip(gxs, r1s)]

    def rest_grads(self, d_w_glu, d_w_su, d_w_du, d_w_out):
        gxs = [d_w_glu.reshape(N_CHIPS, 2, D_S5 // 8, D_S5), d_w_su.reshape(N_CHIPS, 2, D_S5 // 2, D_MODEL // N_CHIPS),
               d_w_du.reshape(N_CHIPS, 2, D_DN // 2, D_MODEL // N_CHIPS), d_w_out.reshape(N_CHIPS, 2, D_MODEL // 8, D_MODEL)]
        r1s, csbs = self._chip_sums(gxs, "rs_swap_rest")
        self.rest = (gxs, r1s, csbs)

    def d_w_in(self, h, dproj):
        gxs, r1s, csbs = self.rest
        d_w_perm, *r2s = _mm(h, dproj, ta=True, name="d_w_in", tm=1024, tn=1152, exchange=_owners_exchange(csbs))
        self.reduced = list(zip(gxs, r1s, r2s))
        return d_w_perm

    def d_h(self, dproj, w_perm, d_w_perm):
        d_w = jnp.concatenate([d_w_perm[:, :OFF_GS], d_w_perm[:, OFF_B:OFF_B + 2 * DN_HEADS], d_w_perm[:, OFF_GS:OFF_B]], axis=1)
        cw = D_IN // N_CHIPS
        gx = d_w.reshape(D_MODEL, N_CHIPS, cw).transpose(1, 0, 2).reshape(N_CHIPS, 2, D_MODEL // 2, cw)
        (r1,), (csb,) = self._chip_sums([gx], "rs_swap_w_in")
        dh, r2 = _mm(dproj, w_perm, tb=True, name="d_h", tm=2048, tn=1024, tk=1152, exchange=_owners_exchange([csb]))
        self.reduced = [(gx, r1, r2)] + self.reduced
        return dh


_SMALL = ("ln_w", "s5_lam_re", "s5_lam_im", "s5_log_step", "s5_b_re", "s5_b_im", "s5_c_re", "s5_c_im", "s5_d",
          "dn_a_log", "dn_dt_bias", "dn_norm_w", "final_norm_w")
_BIG = ("w_in", "s5_w_glu", "s5_w_up", "dn_w_up", "w_out")
_ORDER = ("ln_w", "w_in", "s5_lam_re", "s5_lam_im", "s5_log_step", "s5_b_re", "s5_b_im", "s5_c_re", "s5_c_im", "s5_d",
          "s5_w_glu", "s5_w_up", "dn_conv_w", "dn_a_log", "dn_dt_bias", "dn_norm_w", "dn_w_up", "w_out", "final_norm_w")


def kernel(x, ln_w, w_in, s5_lam_re, s5_lam_im, s5_log_step, s5_b_re, s5_b_im, s5_c_re, s5_c_im, s5_d, s5_w_glu, s5_w_up, dn_conv_w, dn_a_log, dn_dt_bias, dn_norm_w, dn_w_up, w_out, final_norm_w, loss_target, m_ln_w, m_w_in, m_s5_lam_re, m_s5_lam_im, m_s5_log_step, m_s5_b_re, m_s5_b_im, m_s5_c_re, m_s5_c_im, m_s5_d, m_s5_w_glu, m_s5_w_up, m_dn_conv_w, m_dn_a_log, m_dn_dt_bias, m_dn_norm_w, m_dn_w_up, m_w_out, m_final_norm_w, v_ln_w, v_w_in, v_s5_lam_re, v_s5_lam_im, v_s5_log_step, v_s5_b_re, v_s5_b_im, v_s5_c_re, v_s5_c_im, v_s5_d, v_s5_w_glu, v_s5_w_up, v_dn_conv_w, v_dn_a_log, v_dn_dt_bias, v_dn_norm_w, v_dn_w_up, v_w_out, v_final_norm_w):
    w = dict(ln_w=ln_w, w_in=w_in, s5_lam_re=s5_lam_re, s5_lam_im=s5_lam_im, s5_log_step=s5_log_step, s5_b_re=s5_b_re,
             s5_b_im=s5_b_im, s5_c_re=s5_c_re, s5_c_im=s5_c_im, s5_d=s5_d, s5_w_glu=s5_w_glu, s5_w_up=s5_w_up,
             dn_conv_w=dn_conv_w, dn_a_log=dn_a_log, dn_dt_bias=dn_dt_bias, dn_norm_w=dn_norm_w, dn_w_up=dn_w_up, w_out=w_out,
             final_norm_w=final_norm_w)
    m = dict(ln_w=m_ln_w, w_in=m_w_in, s5_lam_re=m_s5_lam_re, s5_lam_im=m_s5_lam_im, s5_log_step=m_s5_log_step,
             s5_b_re=m_s5_b_re, s5_b_im=m_s5_b_im, s5_c_re=m_s5_c_re, s5_c_im=m_s5_c_im, s5_d=m_s5_d, s5_w_glu=m_s5_w_glu,
             s5_w_up=m_s5_w_up, dn_conv_w=m_dn_conv_w, dn_a_log=m_dn_a_log, dn_dt_bias=m_dn_dt_bias, dn_norm_w=m_dn_norm_w,
             dn_w_up=m_dn_w_up, w_out=m_w_out, final_norm_w=m_final_norm_w)
    v = dict(ln_w=v_ln_w, w_in=v_w_in, s5_lam_re=v_s5_lam_re, s5_lam_im=v_s5_lam_im, s5_log_step=v_s5_log_step,
             s5_b_re=v_s5_b_re, s5_b_im=v_s5_b_im, s5_c_re=v_s5_c_re, s5_c_im=v_s5_c_im, s5_d=v_s5_d, s5_w_glu=v_s5_w_glu,
             s5_w_up=v_s5_w_up, dn_conv_w=v_dn_conv_w, dn_a_log=v_dn_a_log, dn_dt_bias=v_dn_dt_bias, dn_norm_w=v_dn_norm_w,
             dn_w_up=v_dn_w_up, w_out=v_w_out, final_norm_w=v_final_norm_w)
    xi, yi, ci = _place()
    chip = 2 * xi + yi
    where = jnp.stack([ci, chip]).astype(jnp.int32)

    g_in, g_conv = _run_exchange(_gather_exchange([w_in[0].astype(BF16)], [dn_conv_w[0]]), "gather_w_in")
    cat = lambda g: jnp.concatenate([g[j] for j in range(N_CHIPS)], axis=1)
    w_full = cat(g_in)
    w_perm = jnp.concatenate([w_full[:, :OFF_GS], w_full[:, OFF_GS + 2 * DN_HEADS:], w_full[:, OFF_GS:OFF_GS + 2 * DN_HEADS],
                              jnp.zeros((D_MODEL, D_IN_PAD - D_IN), BF16)], axis=1)

    ops = _ExchangeOps([w[n][0].astype(BF16) for n in _BIG[1:]], where)
    loss_row, grad_x, g = _local_step(
        x[0], loss_target[0], ln_w, w_perm, s5_lam_re[0], s5_lam_im[0], s5_log_step, s5_b_re[0], s5_b_im[0], s5_c_re[0],
        s5_c_im[0], s5_d, cat(g_conv), dn_a_log, dn_dt_bias, dn_norm_w, final_norm_w[None], ops)
    loss = lax.psum(loss_row[0, 0], ("x", "y", "c"))

    gfs = [_owner_sum(gx, r1, r2x, where) for gx, r1, r2x in ops.reduced]
    gfs = _share_halves(gfs)
    grads, deltas, new_m, new_v = {}, {}, {}, {}
    for n, gf in zip(_BIG, gfs):
        shp = w[n].shape
        g2 = gf.reshape(shp[1:])
        d_, m_, v_ = _adamw(w[n][0], g2, m[n][0], v[n][0], "adamw_" + n)
        grads[n], deltas[n], new_m[n], new_v[n] = g2.reshape(shp), d_.reshape(shp), m_.reshape(shp), v_.reshape(shp)

    gp = _pack([g[n] for n in _SMALL] + [g["dn_conv_w"]])
    zc = jnp.zeros((CONV_K, 3 * D_DN), F32)
    go, dl, mo, vo = _small_allreduce_adamw(gp, _pack([w[n] for n in _SMALL] + [zc]), _pack([m[n] for n in _SMALL] + [zc]),
                                            _pack([v[n] for n in _SMALL] + [zc]))
    shapes = [w[n].shape for n in _SMALL] + [(CONV_K, 3 * D_DN)]
    for dst, src in ((grads, go), (deltas, dl), (new_m, mo), (new_v, vo)):
        for n, a in zip(_SMALL, _unpack(src, shapes)):
            dst[n] = a
    cc = 3 * D_DN // N_CHIPS
    g_conv_mine = lax.dynamic_slice(_unpack(go, shapes)[-1], (0, chip * cc), (CONV_K, cc))
    d_, m_, v_ = _adamw(dn_conv_w[0], g_conv_mine, m_dn_conv_w[0], v_dn_conv_w[0], "adamw_dn_conv_w")
    grads["dn_conv_w"], deltas["dn_conv_w"], new_m["dn_conv_w"], new_v["dn_conv_w"] = (
        g_conv_mine[None], d_[None], m_[None], v_[None])

    return (loss, grad_x[None], *[grads[n] for n in _ORDER], *[deltas[n] for n in _ORDER], *[new_m[n] for n in _ORDER],
            *[new_v[n] for n in _ORDER])
```

```python
import functools
import math

import jax
import jax.numpy as jnp
from jax import lax
from jax.experimental import pallas as pl
from jax.experimental.pallas import tpu as pltpu

F32 = jnp.float32
BF16 = jnp.bfloat16
HI = lax.Precision.HIGHEST
MESH = pl.DeviceIdType.MESH
ANY = pl.BlockSpec(memory_space=pl.ANY)

EPS = 1e-6
D_MODEL = 2048
D_S5 = 1024
S5_GROUP = 16
S5_GROUPS = 64
S5_STATE = 64
S5_BLOCKS = 8
S5_SEG = 8
DN_HEADS = 8
DN_HEAD_DIM = 128
D_DN = 1024
CONV_K = 4
CHUNK = 64
D_IN = 10256
D_IN_PAD = 10368
OFF_US, OFF_ZS, OFF_Q, OFF_K, OFF_V, OFF_ZD, OFF_GS, OFF_GD, OFF_B = 0, 1024, 2048, 3072, 4096, 5120, 6144, 8192, 10240
N_CHIPS = 4
N_DEV = 8
VMEM_LIMIT = 56 * 1024 * 1024

ADAM_LR = 0.001
ADAM_B1 = 0.9
ADAM_B2 = 0.999
ADAM_EPS = 1e-08
ADAM_WD = 0.01
ADAM_STEP = 10


def _cp(sem=None):
    return pltpu.CompilerParams(dimension_semantics=sem, vmem_limit_bytes=VMEM_LIMIT)


def _sds(shape, dtype=F32):
    return jax.ShapeDtypeStruct(tuple(shape), dtype)


def _sigmoid(x):
    return 1.0 / (1.0 + jnp.exp(-x))


def _silu(x):
    return x * _sigmoid(x)


def _dsilu(x):
    s = _sigmoid(x)
    return s * (1.0 + x * (1.0 - s))


class _Exchange:
    def __init__(self, ins, out_shapes, n_sems, start, finish):
        self.ins, self.out_shapes, self.n_sems, self.start, self.finish = list(ins), list(out_shapes), n_sems, start, finish


def _mm(a, b, *, name, ta=False, tb=False, out_dtype=F32, tm=512, tn=512, tk=2048, shard_out=False, exchange=None):
    if ta:
        K, M = a.shape
    else:
        M, K = a.shape
    if tb:
        N, K2 = b.shape
    else:
        K2, N = b.shape
    assert K == K2, (a.shape, b.shape)
    tm, tn, tk = min(tm, M), min(tn, N), min(tk, K)
    assert M % tm == 0 and N % tn == 0 and K % tk == 0, (M, N, K, tm, tn, tk)
    nk = K // tk
    dims = (((0 if ta else 1,), (1 if tb else 0,)), ((), ()))

    gm, gn = M // tm, N // tn
    n_in = len(exchange.ins) if exchange else 0
    n_out = len(exchange.out_shapes) if exchange else 0

    def body(*refs):
        a_ref, b_ref, xin, o_ref = refs[0], refs[1], refs[2:2 + n_in], refs[2 + n_in]
        xout, rest = refs[3 + n_in:3 + n_in + n_out], refs[3 + n_in + n_out:]
        i, j, k = pl.program_id(0), pl.program_id(1), pl.program_id(2)
        if exchange:
            sems = rest[-3:]

            @pl.when(jnp.logical_and(jnp.logical_and(i == 0, j == 0), k == 0))
            def _():
                exchange.start(xin, xout, *sems)

        p = lax.dot_general(a_ref[...].astype(BF16), b_ref[...].astype(BF16), dims, preferred_element_type=F32)
        if nk == 1:
            o_ref[...] = p.astype(out_dtype).reshape(o_ref.shape)
        else:
            acc_ref = rest[0]

            @pl.when(k == 0)
            def _():
                acc_ref[...] = p

            @pl.when(k > 0)
            def _():
                acc_ref[...] += p

            @pl.when(k == nk - 1)
            def _():
                o_ref[...] = acc_ref[...].astype(out_dtype).reshape(o_ref.shape)

        if exchange:
            @pl.when(jnp.logical_and(jnp.logical_and(i == gm - 1, j == gn - 1), k == nk - 1))
            def _():
                exchange.finish(xin, xout, *sems)

    a_spec = pl.BlockSpec((tk, tm), lambda i, j, k: (k, i)) if ta else pl.BlockSpec((tm, tk), lambda i, j, k: (i, k))
    b_spec = pl.BlockSpec((tn, tk), lambda i, j, k: (j, k)) if tb else pl.BlockSpec((tk, tn), lambda i, j, k: (k, j))
    if shard_out:
        o_spec = pl.BlockSpec((1, tm, tn), lambda i, j, k: (j, i, 0))
        o_shape = _sds((N // tn, M, tn), out_dtype)
    else:
        o_spec = pl.BlockSpec((tm, tn), lambda i, j, k: (i, j))
        o_shape = _sds((M, N), out_dtype)
    scratch = [pltpu.VMEM((tm, tn), F32)] if nk > 1 else []
    if not exchange:
        return pl.pallas_call(
            body, name=name, grid=(gm, gn, nk), in_specs=[a_spec, b_spec], out_specs=o_spec, out_shape=o_shape,
            scratch_shapes=scratch, compiler_params=_cp(("parallel", "parallel", "arbitrary")),
        )(a, b)
    scratch += [pltpu.SemaphoreType.DMA((exchange.n_sems,)) for _ in range(3)]
    return pl.pallas_call(
        body, name=name, grid=(gm, gn, nk), in_specs=[a_spec, b_spec] + [ANY] * n_in, out_specs=[o_spec] + [ANY] * n_out,
        out_shape=[o_shape] + exchange.out_shapes, scratch_shapes=scratch,
        compiler_params=_cp(("arbitrary", "arbitrary", "arbitrary")),
    )(a, b, *exchange.ins)


def _ln_fwd(x, w):
    L, D = x.shape
    tm = min(256, L)

    def body(x_ref, w_ref, h_ref, r_ref):
        xv = x_ref[...]
        r = lax.rsqrt(jnp.mean(xv * xv, axis=-1, keepdims=True) + EPS)
        h_ref[...] = (xv * r * w_ref[...]).astype(BF16)
        r_ref[...] = r

    return pl.pallas_call(
        body, name="ln_fwd", grid=(L // tm,),
        in_specs=[pl.BlockSpec((tm, D), lambda i: (i, 0)), pl.BlockSpec((1, D), lambda i: (0, 0))],
        out_specs=[pl.BlockSpec((tm, D), lambda i: (i, 0)), pl.BlockSpec((tm, 1), lambda i: (i, 0))],
        out_shape=[_sds((L, D), BF16), _sds((L, 1))], compiler_params=_cp(("parallel",)),
    )(x, w)


def _ln_bwd(x, r, w, dh, dx2):
    L, D = x.shape
    tm = min(256, L)

    def body(x_ref, r_ref, w_ref, dh_ref, dx2_ref, dx_ref, dw_ref):
        i = pl.program_id(0)
        xv, rv, dhv = x_ref[...], r_ref[...], dh_ref[...]
        t = dhv * w_ref[...]
        m = jnp.mean(t * xv, axis=-1, keepdims=True)
        dx_ref[...] = dx2_ref[...] + rv * t - xv * (rv * rv * rv) * m
        part = jnp.sum(dhv * xv * rv, axis=0, keepdims=True)

        @pl.when(i == 0)
        def _():
            dw_ref[...] = part

        @pl.when(i > 0)
        def _():
            dw_ref[...] += part

    row = pl.BlockSpec((tm, D), lambda i: (i, 0))
    return pl.pallas_call(
        body, name="ln_bwd", grid=(L // tm,),
        in_specs=[row, pl.BlockSpec((tm, 1), lambda i: (i, 0)), pl.BlockSpec((1, D), lambda i: (0, 0)), row, row],
        out_specs=[row, pl.BlockSpec((1, D), lambda i: (0, 0))],
        out_shape=[_sds((L, D)), _sds((1, D))], compiler_params=_cp(("arbitrary",)),
    )(x, r, w, dh, dx2)


def _s5_param_math(lam_re, lam_im, log_step, b_re, b_im, expand):
    step = jnp.exp(log_step)
    mag = jnp.exp(lam_re * step)
    abar_re = mag * jnp.cos(lam_im * step)
    abar_im = mag * jnp.sin(lam_im * step)
    den = lam_re * lam_re + lam_im * lam_im
    xr = abar_re - 1.0
    f_re = (xr * lam_re + abar_im * lam_im) / den
    f_im = (abar_im * lam_re - xr * lam_im) / den
    fe_re = jnp.dot(f_re, expand, precision=HI, preferred_element_type=F32)
    fe_im = jnp.dot(f_im, expand, precision=HI, preferred_element_type=F32)
    bb_re = fe_re * b_re - fe_im * b_im
    bb_im = fe_re * b_im + fe_im * b_re
    return abar_re, abar_im, bb_re, bb_im


def _s5_expand():
    p = lax.broadcasted_iota(jnp.int32, (S5_STATE, S5_STATE * S5_GROUP), 0)
    q = lax.broadcasted_iota(jnp.int32, (S5_STATE, S5_STATE * S5_GROUP), 1)
    return (q // S5_GROUP == p).astype(F32)


def _s5_param_fwd(lam_re, lam_im, log_step, b_re, b_im):
    G, P = lam_re.shape

    def body(lr, li, ls, br, bi, ar_o, ai_o, bbr_o, bbi_o):
        outs = _s5_param_math(lr[...], li[...], ls[...], br[...], bi[...], _s5_expand())
        for o, v in zip((ar_o, ai_o, bbr_o, bbi_o), outs):
            o[...] = v

    return pl.pallas_call(
        body, name="s5_param_fwd",
        out_shape=[_sds((G, P)), _sds((G, P)), _sds(b_re.shape), _sds(b_re.shape)], compiler_params=_cp(),
    )(lam_re, lam_im, log_step, b_re, b_im)


def _s5_param_bwd(lam_re, lam_im, log_step, b_re, b_im, dar, dai, dbbr, dbbi):
    G, P = lam_re.shape

    def body(lr, li, ls, br, bi, g0, g1, g2, g3, dlr, dli, dls, dbr, dbi):
        ex = _s5_expand()
        _, f = jax.vjp(lambda a, b, c, d, e: _s5_param_math(a, b, c, d, e, ex), lr[...], li[...], ls[...], br[...], bi[...])
        grads = f((g0[...], g1[...], g2[...], g3[...]))
        for o, v in zip((dlr, dli, dls, dbr, dbi), grads):
            o[...] = v

    return pl.pallas_call(
        body, name="s5_param_bwd",
        out_shape=[_sds((G, P)), _sds((G, P)), _sds((G, 1)), _sds(b_re.shape), _sds(b_re.shape)], compiler_params=_cp(),
    )(lam_re, lam_im, log_step, b_re, b_im, dar, dai, dbbr, dbbi)


def _to_segs(src_ref, dst_ref, L):
    S = L // S5_SEG

    def body(j, carry):
        dst_ref[pl.ds(pl.multiple_of(S5_SEG * j, S5_SEG), S5_SEG), :] = src_ref[pl.ds(j, S5_SEG, stride=S), :]
        return carry

    lax.fori_loop(0, S, body, 0, unroll=8)


def _from_segs(src_ref, L, write):
    S = L // S5_SEG
    for seg in range(S5_SEG):
        def body(jb, carry, seg=seg):
            j0 = 16 * jb
            write(pl.multiple_of(seg * S + j0, 16), src_ref[pl.ds(S5_SEG * j0 + seg, 16, stride=S5_SEG), :])
            return carry

        lax.fori_loop(0, S // 16, body, 0, unroll=4)


def _scan_segs(ar, ai, re_ref, im_ref, end_r_ref, end_i_ref, c_r_ref, c_i_ref, L, tile0, reverse):
    S = L // S5_SEG
    NB, LN = re_ref.shape[0], 128
    assert S & (S - 1) == 0
    tile = lambda j: pl.ds(pl.multiple_of(S5_SEG * (tile0 + j), S5_SEG), S5_SEG)
    ar8 = [jnp.broadcast_to(ar[:, b * LN:(b + 1) * LN], (S5_SEG, LN)) for b in range(NB)]
    ai8 = [jnp.broadcast_to(ai[:, b * LN:(b + 1) * LN], (S5_SEG, LN)) for b in range(NB)]

    def step(idx, carry):
        rows = tile(S - 1 - idx if reverse else idx)
        out = []
        for b in range(NB):
            sr, si = carry[b]
            nr = ar8[b] * sr - ai8[b] * si + re_ref[b, rows, :]
            ni = ar8[b] * si + ai8[b] * sr + im_ref[b, rows, :]
            re_ref[b, rows, :] = nr
            im_ref[b, rows, :] = ni
            out.append((nr, ni))
        return tuple(out)

    z8 = jnp.zeros((S5_SEG, LN), F32)
    fin = lax.fori_loop(0, S, step, tuple((z8, z8) for _ in range(NB)), unroll=4)
    order = range(S5_SEG - 2, -1, -1) if reverse else range(1, S5_SEG)
    for b in range(NB):
        end_r_ref[b], end_i_ref[b] = fin[b]
        pr, pi = ar8[b][:1], ai8[b][:1]
        for _ in range(int(math.log2(S))):
            pr, pi = pr * pr - pi * pi, 2.0 * pr * pi
        first = S5_SEG - 1 if reverse else 0
        c_r_ref[b, pl.ds(first, 1), :] = jnp.zeros((1, LN), F32)
        c_i_ref[b, pl.ds(first, 1), :] = jnp.zeros((1, LN), F32)
        cr, ci = end_r_ref[b, pl.ds(first, 1), :], end_i_ref[b, pl.ds(first, 1), :]
        for i in order:
            c_r_ref[b, pl.ds(i, 1), :] = cr
            c_i_ref[b, pl.ds(i, 1), :] = ci
            er, ei = end_r_ref[b, pl.ds(i, 1), :], end_i_ref[b, pl.ds(i, 1), :]
            cr, ci = er + pr * cr - pi * ci, ei + pr * ci + pi * cr

    def fix(idx, carry):
        rows = tile(S - 1 - idx if reverse else idx)
        out = []
        for b in range(NB):
            pr, pi = carry[b]
            cr, ci = c_r_ref[b], c_i_ref[b]
            re_ref[b, rows, :] += pr * cr - pi * ci
            im_ref[b, rows, :] += pr * ci + pi * cr
            out.append((pr * ar8[b] - pi * ai8[b], pr * ai8[b] + pi * ar8[b]))
        return tuple(out)

    lax.fori_loop(0, S, fix, tuple((ar8[b], ai8[b]) for b in range(NB)), unroll=4)


def _s5_seg_scratch(L, cs, pad):
    NB = cs // 128
    small = [pltpu.VMEM((NB, S5_SEG, 128), F32) for _ in range(4)]
    return [pltpu.VMEM((NB, L + pad, 128), F32), pltpu.VMEM((NB, L + pad, 128), F32)] + small


def _s5_core_fwd(proj, wbr, wbi, a_re, a_im, cbr, cbi):
    L = proj.shape[0]
    nb, ci, cs = wbr.shape
    NB = cs // 128

    def body(u_ref, wbr_ref, wbi_ref, ar_ref, ai_ref, cbr_ref, cbi_ref, y_ref, sr, si, er, ei, cr, cim, up, yp):
        _to_segs(u_ref, up, L)
        u = up[...].astype(BF16)
        for b in range(NB):
            lanes = pl.ds(b * 128, 128)
            sr[b] = jnp.dot(u, wbr_ref[0, :, lanes], preferred_element_type=F32)
            si[b] = jnp.dot(u, wbi_ref[0, :, lanes], preferred_element_type=F32)
        _scan_segs(ar_ref[...], ai_ref[...], sr, si, er, ei, cr, cim, L, 0, False)
        y = jnp.zeros((L, ci), F32)
        for b in range(NB):
            lanes = pl.ds(b * 128, 128)
            y = y + (jnp.dot(sr[b].astype(BF16), cbr_ref[0, lanes, :], preferred_element_type=F32)
                     - jnp.dot(si[b].astype(BF16), cbi_ref[0, lanes, :], preferred_element_type=F32))
        yp[...] = y

        def write(row, val):
            y_ref[pl.ds(row, 16), :] = val

        _from_segs(yp, L, write)

    wspec = pl.BlockSpec((1, ci, cs), lambda j: (j, 0, 0))
    aspec = pl.BlockSpec((1, cs), lambda j: (0, j))
    cspec = pl.BlockSpec((1, cs, ci), lambda j: (j, 0, 0))
    return pl.pallas_call(
        body, name="s5_core_fwd", grid=(nb,),
        in_specs=[pl.BlockSpec((L, ci), lambda j: (0, OFF_US // ci + j)), wspec, wspec, aspec, aspec, cspec, cspec],
        out_specs=pl.BlockSpec((L, ci), lambda j: (0, j)), out_shape=_sds((L, nb * ci)),
        scratch_shapes=_s5_seg_scratch(L, cs, 0) + [pltpu.VMEM((L, ci), F32), pltpu.VMEM((L, ci), F32)],
        compiler_params=_cp(("arbitrary",)),
    )(proj, wbr, wbi, a_re, a_im, cbr, cbi)


def _s5_core_bwd(proj, wbr, wbi, a_re, a_im, cbr, cbi, dyc, du1):
    L = proj.shape[0]
    nb, ci, cs = wbr.shape
    NB = cs // 128
    S = L // S5_SEG
    PAD = S5_SEG

    def body(u_ref, wbr_ref, wbi_ref, ar_ref, ai_ref, cbr_ref, cbi_ref, dy_ref, du1_ref,
             du_ref, dwbr_ref, dwbi_ref, dcbr_ref, dcbi_ref, dar_ref, dai_ref,
             sr, si, er, ei, cr, cim, lr, li, up, dyp, dup):
        tn = (((0,), (0,)), ((), ()))
        nt = (((1,), (1,)), ((), ()))
        _to_segs(u_ref, up, L)
        _to_segs(dy_ref, dyp, L)
        _to_segs(du1_ref, dup, L)
        u = up[...].astype(BF16)
        dy = dyp[...].astype(BF16)
        ar, ai = ar_ref[...], ai_ref[...]
        for b in range(NB):
            lanes = pl.ds(b * 128, 128)
            sr[b, pl.ds(PAD, L), :] = jnp.dot(u, wbr_ref[0, :, lanes], preferred_element_type=F32)
            si[b, pl.ds(PAD, L), :] = jnp.dot(u, wbi_ref[0, :, lanes], preferred_element_type=F32)
        _scan_segs(ar, ai, sr, si, er, ei, cr, cim, L, 1, False)
        for b in range(NB):
            lanes = pl.ds(b * 128, 128)
            sr[b, pl.ds(0, PAD), :] = cr[b]
            si[b, pl.ds(0, PAD), :] = cim[b]
            lr[b] = lax.dot_general(dy, cbr_ref[0, lanes, :], nt, preferred_element_type=F32)
            li[b] = -lax.dot_general(dy, cbi_ref[0, lanes, :], nt, preferred_element_type=F32)
            dcbr_ref[0, lanes, :] = lax.dot_general(sr[b, pl.ds(PAD, L), :].astype(BF16), dy, tn, preferred_element_type=F32)
            dcbi_ref[0, lanes, :] = -lax.dot_general(si[b, pl.ds(PAD, L), :].astype(BF16), dy, tn, preferred_element_type=F32)
        _scan_segs(ar, -ai, lr, li, er, ei, cr, cim, L, 0, True)

        def da_step(j, carry):
            rows = pl.ds(pl.multiple_of(S5_SEG * j, S5_SEG), S5_SEG)
            out = []
            for b in range(NB):
                dar, dai = carry[b]
                pr_, pi_ = sr[b, rows, :], si[b, rows, :]
                gr, gi = lr[b, rows, :], li[b, rows, :]
                out.append((dar + (gr * pr_ + gi * pi_), dai + (gi * pr_ - gr * pi_)))
            return tuple(out)

        z8 = jnp.zeros((S5_SEG, 128), F32)
        acc = lax.fori_loop(0, S, da_step, tuple((z8, z8) for _ in range(NB)), unroll=4)
        du = dup[...]
        for b in range(NB):
            lanes = pl.ds(b * 128, 128)
            dar_ref[:, lanes] = jnp.sum(acc[b][0], axis=0, keepdims=True)
            dai_ref[:, lanes] = jnp.sum(acc[b][1], axis=0, keepdims=True)
            gr, gi = lr[b].astype(BF16), li[b].astype(BF16)
            du = du + (lax.dot_general(gr, wbr_ref[0, :, lanes], nt, preferred_element_type=F32)
                       + lax.dot_general(gi, wbi_ref[0, :, lanes], nt, preferred_element_type=F32))
            dwbr_ref[0, :, lanes] = lax.dot_general(u, gr, tn, preferred_element_type=F32)
            dwbi_ref[0, :, lanes] = lax.dot_general(u, gi, tn, preferred_element_type=F32)
        dup[...] = du

        def write(row, val):
            du_ref[pl.ds(row, 16), :] = val.astype(BF16)

        _from_segs(dup, L, write)

    wspec = pl.BlockSpec((1, ci, cs), lambda j: (j, 0, 0))
    aspec = pl.BlockSpec((1, cs), lambda j: (0, j))
    cspec = pl.BlockSpec((1, cs, ci), lambda j: (j, 0, 0))
    col = pl.BlockSpec((L, ci), lambda j: (0, j))
    return pl.pallas_call(
        body, name="s5_core_bwd", grid=(nb,),
        in_specs=[pl.BlockSpec((L, ci), lambda j: (0, OFF_US // ci + j)), wspec, wspec, aspec, aspec, cspec, cspec, col, col],
        out_specs=[col, wspec, wspec, cspec, cspec, aspec, aspec],
        out_shape=[_sds((L, nb * ci), BF16), _sds(wbr.shape), _sds(wbr.shape), _sds(cbr.shape), _sds(cbr.shape),
                   _sds((1, nb * cs)), _sds((1, nb * cs))],
        scratch_shapes=(_s5_seg_scratch(L, cs, PAD) + [pltpu.VMEM((NB, L, 128), F32), pltpu.VMEM((NB, L, 128), F32)]
                        + [pltpu.VMEM((L, ci), F32) for _ in range(3)]),
        compiler_params=_cp(("arbitrary",)),
    )(proj, wbr, wbi, a_re, a_im, cbr, cbi, dyc, du1)


def _s5_post_math(yc, u, z, d, wg):
    y = yc + d * u
    y1 = jax.nn.gelu(y)
    t = jnp.dot(y1.astype(BF16), wg, preferred_element_type=F32)
    sg = _sigmoid(t)
    return y, y1, sg


def _s5_post_fwd(yc, proj, d, wg):
    L, W = yc.shape
    tm = min(256, L)

    def body(yc_ref, u_ref, z_ref, d_ref, wg_ref, o_ref):
        _, y1, sg = _s5_post_math(yc_ref[...], u_ref[...], z_ref[...], d_ref[...], wg_ref[...])
        o_ref[...] = (y1 * sg * _silu(z_ref[...])).astype(BF16)

    row = pl.BlockSpec((tm, W), lambda i: (i, 0))
    return pl.pallas_call(
        body, name="s5_post_fwd", grid=(L // tm,),
        in_specs=[row, pl.BlockSpec((tm, W), lambda i: (i, OFF_US // W)), pl.BlockSpec((tm, W), lambda i: (i, OFF_ZS // W)),
                  pl.BlockSpec((1, W), lambda i: (0, 0)), pl.BlockSpec((W, W), lambda i: (0, 0))],
        out_specs=row, out_shape=_sds((L, W), BF16), compiler_params=_cp(("parallel",)),
    )(yc, proj, proj, d, wg)


def _s5_post_bwd(yc, proj, d, wg, dout):
    L, W = yc.shape
    tm = min(256, L)

    def body(yc_ref, u_ref, z_ref, d_ref, wg_ref, do_ref, dyc_ref, du_ref, dz_ref, dd_ref, dwg_ref):
        i = pl.program_id(0)
        u, z, d_, wgv = u_ref[...], z_ref[...], d_ref[...], wg_ref[...]
        y, y1, sg = _s5_post_math(yc_ref[...], u, z, d_, wgv)
        dout_ = do_ref[...]
        y2 = y1 * sg
        dy2 = dout_ * _silu(z)
        dz_ref[...] = (dout_ * y2 * _dsilu(z)).astype(BF16)
        dt = (dy2 * y1 * sg * (1.0 - sg)).astype(BF16)
        dy1 = dy2 * sg + lax.dot_general(dt, wgv, (((1,), (1,)), ((), ())), preferred_element_type=F32)
        _, gelu_vjp = jax.vjp(jax.nn.gelu, y)
        dy = gelu_vjp(dy1)[0]
        dyc_ref[...] = dy
        du_ref[...] = dy * d_
        dd_part = jnp.sum(dy * u, axis=0, keepdims=True)
        dwg_part = lax.dot_general(y1.astype(BF16), dt, (((0,), (0,)), ((), ())), preferred_element_type=F32)

        @pl.when(i == 0)
        def _():
            dd_ref[...] = dd_part
            dwg_ref[...] = dwg_part

        @pl.when(i > 0)
        def _():
            dd_ref[...] += dd_part
            dwg_ref[...] += dwg_part

    row = pl.BlockSpec((tm, W), lambda i: (i, 0))
    return pl.pallas_call(
        body, name="s5_post_bwd", grid=(L // tm,),
        in_specs=[row, pl.BlockSpec((tm, W), lambda i: (i, OFF_US // W)), pl.BlockSpec((tm, W), lambda i: (i, OFF_ZS // W)),
                  pl.BlockSpec((1, W), lambda i: (0, 0)), pl.BlockSpec((W, W), lambda i: (0, 0)), row],
        out_specs=[row, row, row, pl.BlockSpec((1, W), lambda i: (0, 0)), pl.BlockSpec((W, W), lambda i: (0, 0))],
        out_shape=[_sds((L, W)), _sds((L, W)), _sds((L, W), BF16), _sds((1, W)), _sds((W, W))],
        compiler_params=_cp(("arbitrary",)),
    )(yc, proj, proj, d, wg, dout)


def _shift_down(x, s):
    if s == 0:
        return x
    rows = lax.broadcasted_iota(jnp.int32, x.shape, 0)
    return jnp.where(rows >= s, pltpu.roll(x, s, 0), 0.0)


def _shift_up(x, s):
    if s == 0:
        return x
    L = x.shape[0]
    rows = lax.broadcasted_iota(jnp.int32, x.shape, 0)
    return jnp.where(rows < L - s, pltpu.roll(x, L - s, 0), 0.0)


def _conv_pre(x, w):
    acc = w[CONV_K - 1:CONV_K, :] * x
    for s in range(1, CONV_K):
        acc = acc + w[CONV_K - 1 - s:CONV_K - s, :] * _shift_down(x, s)
    return acc


def _dn_conv_fwd(proj, conv_w):
    L = proj.shape[0]
    W = DN_HEAD_DIM
    nq = 2 * DN_HEADS

    def body(x_ref, w_ref, o_ref):
        j = pl.program_id(0)
        act = _silu(_conv_pre(x_ref[...], w_ref[...]))
        r = lax.rsqrt(jnp.sum(act * act, axis=-1, keepdims=True) + EPS)
        scale = jnp.where(j < DN_HEADS, DN_HEAD_DIM ** -0.5, 1.0)
        o_ref[...] = jnp.where(j < nq, act * r * scale, act)

    return pl.pallas_call(
        body, name="dn_conv_fwd", grid=(3 * DN_HEADS,),
        in_specs=[pl.BlockSpec((L, W), lambda j: (0, OFF_Q // W + j)), pl.BlockSpec((CONV_K, W), lambda j: (0, j))],
        out_specs=pl.BlockSpec((L, W), lambda j: (0, j)), out_shape=_sds((L, 3 * D_DN)), compiler_params=_cp(("parallel",)),
    )(proj, conv_w)


def _dn_conv_bwd(proj, conv_w, dout):
    L = proj.shape[0]
    W = DN_HEAD_DIM
    nq = 2 * DN_HEADS

    def body(x_ref, w_ref, do_ref, dx_ref, dw_ref):
        j = pl.program_id(0)
        x, w, dout_ = x_ref[...], w_ref[...], do_ref[...]
        pre = _conv_pre(x, w)
        act = _silu(pre)
        r = lax.rsqrt(jnp.sum(act * act, axis=-1, keepdims=True) + EPS)
        scale = jnp.where(j < DN_HEADS, DN_HEAD_DIM ** -0.5, 1.0)
        g = dout_ * scale
        dact_n = r * g - act * (r * r * r) * jnp.sum(g * act, axis=-1, keepdims=True)
        dact = jnp.where(j < nq, dact_n, dout_)
        dpre = dact * _dsilu(pre)
        dx = w[CONV_K - 1:CONV_K, :] * dpre
        for s in range(1, CONV_K):
            dx = dx + w[CONV_K - 1 - s:CONV_K - s, :] * _shift_up(dpre, s)
        dx_ref[...] = dx.astype(BF16)
        for s in range(CONV_K):
            dw_ref[pl.ds(CONV_K - 1 - s, 1), :] = jnp.sum(dpre * _shift_down(x, s), axis=0, keepdims=True)

    col = pl.BlockSpec((L, W), lambda j: (0, j))
    wsp = pl.BlockSpec((CONV_K, W), lambda j: (0, j))
    return pl.pallas_call(
        body, name="dn_conv_bwd", grid=(3 * DN_HEADS,),
        in_specs=[pl.BlockSpec((L, W), lambda j: (0, OFF_Q // W + j)), wsp, col], out_specs=[col, wsp],
        out_shape=[_sds((L, 3 * D_DN), BF16), _sds((CONV_K, 3 * D_DN))], compiler_params=_cp(("parallel",)),
    )(proj, conv_w, dout)


def _softplus(x):
    return jnp.maximum(x, 0.0) + jnp.log(1.0 + jnp.exp(-jnp.abs(x)))


def _dn_gates_fwd(proj, alog, dtb):
    L = proj.shape[0]
    W = 128

    def body(p_ref, al_ref, db_ref, o_ref):
        p = p_ref[...]
        lane = lax.broadcasted_iota(jnp.int32, p.shape, 1)
        g = -jnp.exp(al_ref[...]) * _softplus(p + db_ref[...])
        o_ref[...] = jnp.where(lane < DN_HEADS, _sigmoid(p), jnp.where(lane < 2 * DN_HEADS, g, 0.0))

    return pl.pallas_call(
        body, name="dn_gates_fwd", grid=(1,),
        in_specs=[pl.BlockSpec((L, W), lambda i: (0, OFF_B // W)), pl.BlockSpec((1, W), lambda i: (0, 0)),
                  pl.BlockSpec((1, W), lambda i: (0, 0))],
        out_specs=pl.BlockSpec((L, W), lambda i: (0, 0)), out_shape=_sds((L, W)), compiler_params=_cp(("arbitrary",)),
    )(proj, alog, dtb)


def _dn_gates_bwd(proj, alog, dtb, dgates):
    L = proj.shape[0]
    W = 128

    def body(p_ref, al_ref, db_ref, dg_ref, dp_ref, dal_ref, ddb_ref):
        p, dg = p_ref[...], dg_ref[...]
        lane = lax.broadcasted_iota(jnp.int32, p.shape, 1)
        is_g = jnp.logical_and(lane >= DN_HEADS, lane < 2 * DN_HEADS)
        beta = _sigmoid(p)
        na = -jnp.exp(al_ref[...])
        xs = p + db_ref[...]
        dsp = dg * na * _sigmoid(xs)
        dp_ref[...] = jnp.where(lane < DN_HEADS, dg * beta * (1.0 - beta), jnp.where(is_g, dsp, 0.0)).astype(BF16)
        dal_ref[...] = jnp.sum(jnp.where(is_g, dg * na * _softplus(xs), 0.0), axis=0, keepdims=True)
        ddb_ref[...] = jnp.sum(jnp.where(is_g, dsp, 0.0), axis=0, keepdims=True)

    one = pl.BlockSpec((1, W), lambda i: (0, 0))
    full = pl.BlockSpec((L, W), lambda i: (0, 0))
    return pl.pallas_call(
        body, name="dn_gates_bwd", grid=(1,),
        in_specs=[pl.BlockSpec((L, W), lambda i: (0, OFF_B // W)), one, one, full], out_specs=[full, one, one],
        out_shape=[_sds((L, W), BF16), _sds((1, W)), _sds((1, W))], compiler_params=_cp(("arbitrary",)),
    )(proj, alog, dtb, dgates)


def _bdot(a, b, dims):
    return lax.dot_general(a.astype(BF16), b.astype(BF16), (dims, ((), ())), preferred_element_type=F32)


_NN, _NT, _TN = ((1,), (0,)), ((1,), (1,)), ((0,), (0,))


def _dot3(a, b, dims):
    ah, bh = a.astype(BF16), b.astype(BF16)
    al, bl = (a - ah.astype(F32)).astype(BF16), (b - bh.astype(F32)).astype(BF16)
    (ca,), (cb,) = dims
    a3 = jnp.concatenate([ah, ah, al], axis=ca)
    b3 = jnp.concatenate([bh, bl, bh], axis=cb)
    return lax.dot_general(a3, b3, (dims, ((), ())), preferred_element_type=F32)


def _mm_family(raw):
    nn = jax.custom_vjp(lambda a, b: raw(a, b, _NN))
    nt = jax.custom_vjp(lambda a, b: raw(a, b, _NT))
    tn = jax.custom_vjp(lambda a, b: raw(a, b, _TN))
    nn.defvjp(lambda a, b: (raw(a, b, _NN), (a, b)), lambda r, g: (raw(g, r[1], _NT), raw(r[0], g, _TN)))
    nt.defvjp(lambda a, b: (raw(a, b, _NT), (a, b)), lambda r, g: (raw(g, r[1], _NN), raw(g, r[0], _TN)))
    tn.defvjp(lambda a, b: (raw(a, b, _TN), (a, b)), lambda r, g: (raw(r[1], g, _NT), raw(r[0], g, _NN)))
    return nn, nt, tn


_mm_nn, _mm_nt, _mm_tn = _mm_family(_bdot)
_m3_nn, _m3_nt, _m3_tn = _mm_family(_dot3)


def _tri_apply(x, upper):
    C = x.shape[0]
    ii = lax.broadcasted_iota(jnp.int32, (C, 3 * C), 0)
    jj = lax.broadcasted_iota(jnp.int32, (C, 3 * C), 1) % C
    mat = ((ii <= jj) if upper else (ii >= jj)).astype(BF16)
    hi = x.astype(BF16)
    r = x - hi.astype(F32)
    mid = r.astype(BF16)
    lo = (r - mid.astype(F32)).astype(BF16)
    return jnp.dot(mat, jnp.concatenate([hi, mid, lo], axis=0), preferred_element_type=F32)


_cumsum_rows = jax.custom_vjp(lambda x: _tri_apply(x, False))
_cumsum_rows.defvjp(lambda x: (_tri_apply(x, False), None), lambda _, g: (_tri_apply(g, True),))


def _uli(a_s):
    C = a_s[0].shape[0]
    ii = lax.broadcasted_iota(jnp.int32, (C, C), 0)
    jj = lax.broadcasted_iota(jnp.int32, (C, C), 1)
    eye = jnp.where(ii == jj, 1.0, 0.0)
    ts = [eye - a for a in a_s]
    ms = list(a_s)
    for _ in range(int(math.log2(C)) - 1):
        ms = [_dot3(m, m, _NN) for m in ms]
        ts = [t + _dot3(t, m, _NN) for t, m in zip(ts, ms)]
    return tuple(ts)


def _uli_bwd(ts, gs):
    xs = [_dot3(t, g, _TN) for t, g in zip(ts, gs)]
    return (tuple(-_dot3(x, t, _NT) for x, t in zip(xs, ts)),)


_unit_lower_inverse = jax.custom_vjp(_uli)
_unit_lower_inverse.defvjp(lambda a_s: (lambda ts: (ts, ts))(_uli(a_s)), _uli_bwd)


def _prep_math(qs, ks, vs, gcols, bcols):
    n = len(qs)
    C, dv = vs[0].shape
    ii = lax.broadcasted_iota(jnp.int32, (C, C), 0)
    jj = lax.broadcasted_iota(jnp.int32, (C, C), 1)
    causal = ii >= jj
    strict = ii > jj
    sf = strict.astype(F32)
    ones = jnp.ones((C, dv), F32)
    dms = [_cumsum_rows(g * sf) for g in gcols]
    gcbs = [_cumsum_rows(g * ones) for g in gcols]
    kks = [_mm_nt(k, k) for k in ks]
    qks = [_mm_nt(q, k) for q, k in zip(qs, ks)]
    decays = [jnp.where(causal, jnp.exp(jnp.where(causal, dm, 0.0)), 0.0) for dm in dms]
    glasts = [jnp.sum(g * ones, axis=0, keepdims=True) for g in gcols]
    egs = [jnp.exp(gcb) for gcb in gcbs]
    ts = _unit_lower_inverse(tuple(jnp.where(strict, b * kk * dc, 0.0) for b, kk, dc in zip(bcols, kks, decays)))
    us = [_m3_nn(t, v * b) for t, v, b in zip(ts, vs, bcols)]
    ws = [_m3_nn(t, k * b * eg) for t, k, b, eg in zip(ts, ks, bcols, egs)]
    return tuple((us[i], ws[i], qs[i] * egs[i], ks[i] * jnp.exp(glasts[i] - gcbs[i]), qks[i] * decays[i],
                  jnp.exp(glasts[i])) for i in range(n))


def _gate_cols(gates, h):
    lane = lax.broadcasted_iota(jnp.int32, gates.shape, 1)
    bcol = jnp.sum(jnp.where(lane == h, gates, 0.0), axis=1, keepdims=True)
    gcol = jnp.sum(jnp.where(lane == h + DN_HEADS, gates, 0.0), axis=1, keepdims=True)
    return gcol, bcol


DN_HB = 8


def _dn_prep_fwd(qkv, gates):
    L = qkv.shape[0]
    N, H, d, HB = L // CHUNK, DN_HEADS, DN_HEAD_DIM, DN_HB

    def body(q_ref, k_ref, v_ref, g_ref, u_ref, w_ref, qd_ref, kd_ref, qk_ref, egl_ref):
        h0 = pl.program_id(1) * HB
        gates_ = g_ref[...]
        lanes_of = [pl.ds(i * d, d) for i in range(HB)]
        cols = [_gate_cols(gates_, h0 + i) for i in range(HB)]
        outs = _prep_math([q_ref[:, l] for l in lanes_of], [k_ref[:, l] for l in lanes_of], [v_ref[:, l] for l in lanes_of],
                          [c[0] for c in cols], [c[1] for c in cols])
        for i in range(HB):
            lanes = lanes_of[i]
            u, w, qd, kd, qk, egl = outs[i]
            u_ref[:, lanes] = u
            w_ref[:, lanes] = w
            qd_ref[:, lanes] = qd
            kd_ref[:, lanes] = kd
            qk_ref[0, i] = qk
            egl_ref[0, i] = jnp.broadcast_to(egl, (8, d))

    blk = lambda off: pl.BlockSpec((CHUNK, HB * d), lambda n, j: (n, off // HB + j))
    cc = pl.BlockSpec((1, HB, CHUNK, CHUNK), lambda n, j: (n, j, 0, 0))
    ee = pl.BlockSpec((1, HB, 8, d), lambda n, j: (n, j, 0, 0))
    big = _sds((L, D_DN))
    return pl.pallas_call(
        body, name="dn_prep_fwd", grid=(N, H // HB),
        in_specs=[blk(0), blk(H), blk(2 * H), pl.BlockSpec((CHUNK, 128), lambda n, j: (n, 0))],
        out_specs=[blk(0), blk(0), blk(0), blk(0), cc, ee],
        out_shape=[big, big, big, big, _sds((N, H, CHUNK, CHUNK)), _sds((N, H, 8, d))],
        compiler_params=_cp(("parallel", "parallel")),
    )(qkv, qkv, qkv, gates)


def _dn_scan_fwd(u, w, qd, kd, qk, egl):
    L = u.shape[0]
    N, H, d, HB = L // CHUNK, DN_HEADS, DN_HEAD_DIM, DN_HB

    def body(u_ref, w_ref, qd_ref, kd_ref, qk_ref, egl_ref, o_ref, st_ref, s_ref):
        n, h0 = pl.program_id(0), pl.program_id(1) * HB

        @pl.when(n == 0)
        def _():
            for i in range(HB):
                s_ref[h0 + i] = jnp.zeros((d, d), F32)

        hs = range(HB)
        ln = [pl.ds(i * d, d) for i in hs]
        st = [s_ref[h0 + i] for i in hs]
        ws = [_bdot(w_ref[:, ln[i]], st[i], _NN) for i in hs]
        qs = [_bdot(qd_ref[:, ln[i]], st[i], _NN) for i in hs]
        vn = [u_ref[:, ln[i]] - ws[i] for i in hs]
        qv = [_bdot(qk_ref[0, i], vn[i], _NN) for i in hs]
        kv = [_bdot(kd_ref[:, ln[i]], vn[i], _TN) for i in hs]
        for i in hs:
            st_ref[0, i] = st[i]
            o_ref[:, ln[i]] = qs[i] + qv[i]
            s_ref[h0 + i] = st[i] * egl_ref[0, i, pl.ds(0, 1), :] + kv[i]

    blk = pl.BlockSpec((CHUNK, HB * d), lambda n, j: (n, j))
    cc = pl.BlockSpec((1, HB, CHUNK, CHUNK), lambda n, j: (n, j, 0, 0))
    ee = pl.BlockSpec((1, HB, 8, d), lambda n, j: (n, j, 0, 0))
    return pl.pallas_call(
        body, name="dn_scan_fwd", grid=(N, H // HB), in_specs=[blk, blk, blk, blk, cc, ee],
        out_specs=[blk, pl.BlockSpec((1, HB, d, d), lambda n, j: (n, j, 0, 0))],
        out_shape=[_sds((L, D_DN)), _sds((N, H, d, d))], scratch_shapes=[pltpu.VMEM((H, d, d), F32)],
        compiler_params=_cp(("arbitrary", "arbitrary")),
    )(u, w, qd, kd, qk, egl)


def _dn_scan_bwd(u, w, qd, kd, qk, egl, states, do):
    L = u.shape[0]
    N, H, d, HB = L // CHUNK, DN_HEADS, DN_HEAD_DIM, DN_HB

    def body(u_ref, w_ref, qd_ref, kd_ref, qk_ref, egl_ref, st_ref, do_ref,
             du_ref, dw_ref, dqd_ref, dkd_ref, dqk_ref, degl_ref, ds_ref):
        n, h0 = pl.program_id(0), pl.program_id(1) * HB

        @pl.when(n == 0)
        def _():
            for i in range(HB):
                ds_ref[h0 + i] = jnp.zeros((d, d), F32)

        hs = range(HB)
        ln = [pl.ds(i * d, d) for i in hs]
        st = [st_ref[0, i] for i in hs]
        dsn = [ds_ref[h0 + i] for i in hs]
        do_ = [do_ref[:, ln[i]] for i in hs]
        ws = [_bdot(w_ref[:, ln[i]], st[i], _NN) for i in hs]
        d1 = [_bdot(qk_ref[0, i], do_[i], _TN) for i in hs]
        d2 = [_bdot(kd_ref[:, ln[i]], dsn[i], _NN) for i in hs]
        dqd = [_bdot(do_[i], st[i], _NT) for i in hs]
        qdo = [_bdot(qd_ref[:, ln[i]], do_[i], _TN) for i in hs]
        vn = [u_ref[:, ln[i]] - ws[i] for i in hs]
        dvn = [d1[i] + d2[i] for i in hs]
        dw = [_bdot(dvn[i], st[i], _NT) for i in hs]
        dkd = [_bdot(vn[i], dsn[i], _NT) for i in hs]
        dqk = [_bdot(do_[i], vn[i], _NT) for i in hs]
        wdv = [_bdot(w_ref[:, ln[i]], dvn[i], _TN) for i in hs]
        for i in hs:
            du_ref[:, ln[i]] = dvn[i]
            dw_ref[:, ln[i]] = -dw[i]
            dqd_ref[:, ln[i]] = dqd[i]
            dkd_ref[:, ln[i]] = dkd[i]
            dqk_ref[0, i] = dqk[i]
            degl_ref[0, i] = jnp.broadcast_to(jnp.sum(dsn[i] * st[i], keepdims=True), (8, d))
            ds_ref[h0 + i] = (qdo[i] - wdv[i]) + dsn[i] * egl_ref[0, i, pl.ds(0, 1), :]

    blk = pl.BlockSpec((CHUNK, HB * d), lambda n, j: (N - 1 - n, j))
    cc = pl.BlockSpec((1, HB, CHUNK, CHUNK), lambda n, j: (N - 1 - n, j, 0, 0))
    ee = pl.BlockSpec((1, HB, 8, d), lambda n, j: (N - 1 - n, j, 0, 0))
    ss = pl.BlockSpec((1, HB, d, d), lambda n, j: (N - 1 - n, j, 0, 0))
    big = _sds((L, D_DN))
    return pl.pallas_call(
        body, name="dn_scan_bwd", grid=(N, H // HB), in_specs=[blk, blk, blk, blk, cc, ee, ss, blk],
        out_specs=[blk, blk, blk, blk, cc, ee],
        out_shape=[big, big, big, big, _sds((N, H, CHUNK, CHUNK)), _sds((N, H, 8, d))],
        scratch_shapes=[pltpu.VMEM((H, d, d), F32)], compiler_params=_cp(("arbitrary", "arbitrary")),
    )(u, w, qd, kd, qk, egl, states, do)


def _dn_prep_bwd(qkv, gates, du, dw, dqd, dkd, dqk, degl):
    L = qkv.shape[0]
    N, H, d, HB = L // CHUNK, DN_HEADS, DN_HEAD_DIM, DN_HB

    def body(q_ref, k_ref, v_ref, g_ref, du_ref, dw_ref, dqd_ref, dkd_ref, dqk_ref, degl_ref, dq_ref, dk_ref, dv_ref, dg_ref):
        j = pl.program_id(1)
        h0 = j * HB
        gates_ = g_ref[...]
        lane = lax.broadcasted_iota(jnp.int32, gates_.shape, 1)
        lane1 = lax.broadcasted_iota(jnp.int32, (1, d), 1)
        part = jnp.zeros(gates_.shape, F32)
        lanes_of = [pl.ds(i * d, d) for i in range(HB)]
        cols = [_gate_cols(gates_, h0 + i) for i in range(HB)]
        _, f = jax.vjp(_prep_math, [q_ref[:, l] for l in lanes_of], [k_ref[:, l] for l in lanes_of],
                       [v_ref[:, l] for l in lanes_of], [c[0] for c in cols], [c[1] for c in cols])
        cots = tuple((du_ref[:, l], dw_ref[:, l], dqd_ref[:, l], dkd_ref[:, l], dqk_ref[0, i],
                      jnp.where(lane1 == 0, degl_ref[0, i, pl.ds(0, 1), :], 0.0)) for i, l in enumerate(lanes_of))
        dqs, dks, dvs, dgcs, dbcs = f(cots)
        for i in range(HB):
            lanes = lanes_of[i]
            dq_ref[:, lanes] = dqs[i]
            dk_ref[:, lanes] = dks[i]
            dv_ref[:, lanes] = dvs[i]
            part = part + jnp.where(lane == h0 + i, dbcs[i], 0.0) + jnp.where(lane == h0 + i + DN_HEADS, dgcs[i], 0.0)

        @pl.when(j == 0)
        def _():
            dg_ref[...] = part

        @pl.when(j > 0)
        def _():
            dg_ref[...] += part

    blk = lambda off: pl.BlockSpec((CHUNK, HB * d), lambda n, j: (n, off // HB + j))
    gsp = pl.BlockSpec((CHUNK, 128), lambda n, j: (n, 0))
    cc = pl.BlockSpec((1, HB, CHUNK, CHUNK), lambda n, j: (n, j, 0, 0))
    ee = pl.BlockSpec((1, HB, 8, d), lambda n, j: (n, j, 0, 0))
    big = _sds((L, D_DN))
    return pl.pallas_call(
        body, name="dn_prep_bwd", grid=(N, H // HB),
        in_specs=[blk(0), blk(H), blk(2 * H), gsp, blk(0), blk(0), blk(0), blk(0), cc, ee],
        out_specs=[blk(0), blk(0), blk(0), gsp], out_shape=[big, big, big, _sds((L, 128))],
        compiler_params=_cp(("parallel", "arbitrary")),
    )(qkv, qkv, qkv, gates, du, dw, dqd, dkd, dqk, degl)


def _dn_post_fwd(o, proj, nw):
    L = o.shape[0]
    d = DN_HEAD_DIM
    tm = min(512, L)

    def body(o_ref, z_ref, w_ref, y_ref):
        ov = o_ref[...]
        r = lax.rsqrt(jnp.mean(ov * ov, axis=-1, keepdims=True) + EPS)
        y_ref[...] = (ov * r * w_ref[...] * _silu(z_ref[...])).astype(BF16)

    blk = pl.BlockSpec((tm, d), lambda i, h: (i, h))
    return pl.pallas_call(
        body, name="dn_post_fwd", grid=(L // tm, DN_HEADS),
        in_specs=[blk, pl.BlockSpec((tm, d), lambda i, h: (i, OFF_ZD // d + h)), pl.BlockSpec((1, d), lambda i, h: (0, 0))],
        out_specs=blk, out_shape=_sds((L, D_DN), BF16), compiler_params=_cp(("parallel", "parallel")),
    )(o, proj, nw)


def _dn_post_bwd(o, proj, nw, dy):
    L = o.shape[0]
    d = DN_HEAD_DIM
    tm = min(512, L)

    def body(o_ref, z_ref, w_ref, dy_ref, do_ref, dz_ref, dw_ref):
        first = jnp.logical_and(pl.program_id(0) == 0, pl.program_id(1) == 0)
        ov, z, w, dyv = o_ref[...], z_ref[...], w_ref[...], dy_ref[...]
        r = lax.rsqrt(jnp.mean(ov * ov, axis=-1, keepdims=True) + EPS)
        xn = ov * r
        dz_ref[...] = (dyv * xn * w * _dsilu(z)).astype(BF16)
        dn = dyv * _silu(z)
        t = dn * w
        do_ref[...] = r * t - ov * (r * r * r) * jnp.mean(t * ov, axis=-1, keepdims=True)
        part = jnp.sum(dn * xn, axis=0, keepdims=True)

        @pl.when(first)
        def _():
            dw_ref[...] = part

        @pl.when(jnp.logical_not(first))
        def _():
            dw_ref[...] += part

    blk = pl.BlockSpec((tm, d), lambda i, h: (i, h))
    one = pl.BlockSpec((1, d), lambda i, h: (0, 0))
    return pl.pallas_call(
        body, name="dn_post_bwd", grid=(L // tm, DN_HEADS),
        in_specs=[blk, pl.BlockSpec((tm, d), lambda i, h: (i, OFF_ZD // d + h)), one, blk], out_specs=[blk, blk, one],
        out_shape=[_sds((L, D_DN)), _sds((L, D_DN), BF16), _sds((1, d))], compiler_params=_cp(("arbitrary", "arbitrary")),
    )(o, proj, nw, dy)


def _mix_fwd(s5o, dno, w_su, w_du, proj):
    L, K = s5o.shape
    N = w_su.shape[1]
    tm, tn = min(512, L), 512

    def body(a1, a2, b1, b2, gs, gd, ys_ref, yd_ref, mx_ref):
        ys = jnp.dot(a1[...], b1[...], preferred_element_type=F32)
        yd = jnp.dot(a2[...], b2[...], preferred_element_type=F32)
        ys_ref[...] = ys
        yd_ref[...] = yd
        mx_ref[...] = (_sigmoid(gs[...]) * ys + _sigmoid(gd[...]) * yd).astype(BF16)

    a = pl.BlockSpec((tm, K), lambda i, j: (i, 0))
    b = pl.BlockSpec((K, tn), lambda i, j: (0, j))
    o = pl.BlockSpec((tm, tn), lambda i, j: (i, j))
    return pl.pallas_call(
        body, name="mix_fwd", grid=(L // tm, N // tn),
        in_specs=[a, a, b, b, pl.BlockSpec((tm, tn), lambda i, j: (i, OFF_GS // tn + j)),
                  pl.BlockSpec((tm, tn), lambda i, j: (i, OFF_GD // tn + j))],
        out_specs=[o, o, o], out_shape=[_sds((L, N)), _sds((L, N)), _sds((L, N), BF16)],
        compiler_params=_cp(("parallel", "parallel")),
    )(s5o, dno, w_su, w_du, proj, proj)


def _mix_bwd(dx2b, w_out, proj, ys, yd):
    L, K = dx2b.shape
    N = w_out.shape[0]
    tm, tn = min(512, L), 512

    def body(a, b, gs, gd, ys_ref, yd_ref, dgs_ref, dgd_ref, dys_ref, dyd_ref):
        dm = lax.dot_general(a[...], b[...], (((1,), (1,)), ((), ())), preferred_element_type=F32)
        ss, sd = _sigmoid(gs[...]), _sigmoid(gd[...])
        dys_ref[...] = (dm * ss).astype(BF16)
        dyd_ref[...] = (dm * sd).astype(BF16)
        dgs_ref[...] = (dm * ys_ref[...] * ss * (1.0 - ss)).astype(BF16)
        dgd_ref[...] = (dm * yd_ref[...] * sd * (1.0 - sd)).astype(BF16)

    o = pl.BlockSpec((tm, tn), lambda i, j: (i, j))
    return pl.pallas_call(
        body, name="mix_bwd", grid=(L // tm, N // tn),
        in_specs=[pl.BlockSpec((tm, K), lambda i, j: (i, 0)), pl.BlockSpec((tn, K), lambda i, j: (j, 0)),
                  pl.BlockSpec((tm, tn), lambda i, j: (i, OFF_GS // tn + j)),
                  pl.BlockSpec((tm, tn), lambda i, j: (i, OFF_GD // tn + j)), o, o],
        out_specs=[o, o, o, o], out_shape=[_sds((L, N), BF16)] * 4, compiler_params=_cp(("parallel", "parallel")),
    )(dx2b, w_out, proj, proj, ys, yd)


def _final(mixed, w_out, x, tgt, fw):
    L, D = x.shape
    tm = min(256, L)

    def body(a_ref, b_ref, x_ref, t_ref, w_ref, dx_ref, dxb_ref, loss_ref, dw_ref):
        i = pl.program_id(0)
        x2 = x_ref[...] + jnp.dot(a_ref[...], b_ref[...], preferred_element_type=F32)
        w = w_ref[...]
        r = lax.rsqrt(jnp.mean(x2 * x2, axis=-1, keepdims=True) + EPS)
        xn = x2 * r
        e = xn * w - t_ref[...]
        lpart = 0.5 * jnp.sum(jnp.mean(e * e, axis=-1, keepdims=True), axis=0, keepdims=True)
        dy = e * (1.0 / D)
        t = dy * w
        dx2 = r * t - x2 * (r * r * r) * jnp.mean(t * x2, axis=-1, keepdims=True)
        dx_ref[...] = dx2
        dxb_ref[...] = dx2.astype(BF16)
        dwp = jnp.sum(dy * xn, axis=0, keepdims=True)
        lrow = jnp.broadcast_to(lpart, loss_ref.shape)

        @pl.when(i == 0)
        def _():
            loss_ref[...] = lrow
            dw_ref[...] = dwp

        @pl.when(i > 0)
        def _():
            loss_ref[...] += lrow
            dw_ref[...] += dwp

    row = pl.BlockSpec((tm, D), lambda i: (i, 0))
    one = pl.BlockSpec((1, D), lambda i: (0, 0))
    return pl.pallas_call(
        body, name="final", grid=(L // tm,),
        in_specs=[row, pl.BlockSpec((D, D), lambda i: (0, 0)), row, row, one],
        out_specs=[row, row, pl.BlockSpec((1, 128), lambda i: (0, 0)), one],
        out_shape=[_sds((L, D)), _sds((L, D), BF16), _sds((1, 128)), _sds((1, D))], compiler_params=_cp(("arbitrary",)),
    )(mixed, w_out, x, tgt, fw)


def _block_diag(t):
    J, g, a, b = t.shape
    eye = jnp.eye(g, dtype=t.dtype)
    return (t[:, :, :, None, :] * eye[None, :, None, :, None]).reshape(J, g * a, g * b)


def _block_diag_take(m, g):
    J, ga, gb = m.shape
    a, b = ga // g, gb // g
    m5 = m.reshape(J, g, a, g, b)
    idx = jnp.arange(g)
    return m5[:, idx, :, idx, :].transpose(1, 0, 2, 3)


class _PlainOps:
    def __init__(self, w_rest):
        self.w_rest = w_rest

    def in_proj(self, h, wt_perm):
        return _mm(h, wt_perm, tb=True, name="in_proj", tm=1024, tn=1152), self.w_rest

    def rest_grads(self, d_w_glu, d_w_su, d_w_du, d_w_out):
        pass

    def d_w_in(self, h, dproj):
        return _mm(dproj, h, ta=True, name="d_w_in", tm=1152, tn=1024)

    def d_h(self, dproj, wt_perm, d_wt_perm):
        return _mm(dproj, wt_perm, name="d_h", tm=2048, tn=1024, tk=1152)


def _local_step(x, tgt, ln_w, w_perm, lam_re, lam_im, log_step, b_re, b_im, c_re, c_im, s5_d,
                conv_w, a_log, dt_bias, norm_w, fw, ops):
    G, P, gb = S5_GROUPS, S5_STATE, S5_GROUPS // S5_BLOCKS
    h, rstd = _ln_fwd(x, ln_w)
    proj, (w_glu, w_su, w_du, w_out) = ops.in_proj(h, w_perm)

    b_re2, b_im2 = b_re.reshape(G, P * S5_GROUP), b_im.reshape(G, P * S5_GROUP)
    ls2 = log_step.reshape(G, 1)
    abar_re, abar_im, bb_re, bb_im = _s5_param_fwd(lam_re, lam_im, ls2, b_re2, b_im2)

    def to_wb(bb):
        return _block_diag(bb.reshape(S5_BLOCKS, gb, P, S5_GROUP).transpose(0, 1, 3, 2)).astype(BF16)

    def to_cb(cc):
        return _block_diag(cc.reshape(S5_BLOCKS, gb, S5_GROUP, P).transpose(0, 1, 3, 2)).astype(BF16)

    wbr, wbi, cbr, cbi = to_wb(bb_re), to_wb(bb_im), to_cb(c_re), to_cb(c_im)
    a_re_row, a_im_row = abar_re.reshape(1, G * P), abar_im.reshape(1, G * P)
    yc = _s5_core_fwd(proj, wbr, wbi, a_re_row, a_im_row, cbr, cbi)
    s5o = _s5_post_fwd(yc, proj, s5_d, w_glu)

    pad = lambda v: jnp.pad(v, ((0, 0), (DN_HEADS, 128 - 2 * DN_HEADS)))
    alog_row, dtb_row = pad(a_log), pad(dt_bias)
    qkv = _dn_conv_fwd(proj, conv_w)
    gates = _dn_gates_fwd(proj, alog_row, dtb_row)
    prep = _dn_prep_fwd(qkv, gates)
    o_dn, states = _dn_scan_fwd(*prep)
    dno = _dn_post_fwd(o_dn, proj, norm_w)

    ys, yd, mixed = _mix_fwd(s5o, dno, w_su, w_du, proj)
    dx2, dx2b, loss_row, d_fw = _final(mixed, w_out, x, tgt, fw)
    d_w_out = _mm(mixed, dx2b, ta=True, name="d_w_out")
    dgs, dgd, dys, dyd = _mix_bwd(dx2b, w_out, proj, ys, yd)
    d_w_su = _mm(s5o, dys, ta=True, name="d_w_su", shard_out=True)
    d_w_du = _mm(dno, dyd, ta=True, name="d_w_du", shard_out=True)
    ds5o = _mm(dys, w_su, tb=True, name="d_s5o")
    ddno = _mm(dyd, w_du, tb=True, name="d_dno")

    dyc, du1, dz_s, d_s5d, d_w_glu = _s5_post_bwd(yc, proj, s5_d, w_glu, ds5o)
    ops.rest_grads(d_w_glu, d_w_su, d_w_du, d_w_out)
    du, dwbr, dwbi, dcbr, dcbi, dar, dai = _s5_core_bwd(proj, wbr, wbi, a_re_row, a_im_row, cbr, cbi, dyc, du1)

    def from_wb(dwb):
        return _block_diag_take(dwb, gb).transpose(0, 1, 3, 2).reshape(G, P * S5_GROUP)

    def from_cb(dcb):
        return _block_diag_take(dcb, gb).transpose(0, 1, 3, 2).reshape(G, S5_GROUP, P)

    d_lam_re, d_lam_im, d_ls, d_b_re, d_b_im = _s5_param_bwd(
        lam_re, lam_im, ls2, b_re2, b_im2, dar.reshape(G, P), dai.reshape(G, P), from_wb(dwbr), from_wb(dwbi))

    do_dn, dz_d, d_norm_w = _dn_post_bwd(o_dn, proj, norm_w, ddno)
    dq, dk, dv, dgates = _dn_prep_bwd(qkv, gates, *_dn_scan_bwd(*prep, states, do_dn))
    dqkv, d_conv = _dn_conv_bwd(proj, conv_w, jnp.concatenate([dq, dk, dv], axis=1))
    dpb, d_alog_row, d_dtb_row = _dn_gates_bwd(proj, alog_row, dtb_row, dgates)

    dproj = jnp.concatenate([du, dz_s, dqkv, dz_d, dgs, dgd, dpb], axis=1)
    d_w_perm = ops.d_w_in(h, dproj)
    dh = ops.d_h(dproj, w_perm, d_w_perm)
    grad_x, d_ln_w = _ln_bwd(x, rstd, ln_w, dh, dx2)

    grads = dict(
        ln_w=d_ln_w, w_perm=d_w_perm, s5_lam_re=d_lam_re, s5_lam_im=d_lam_im, s5_log_step=d_ls.reshape(1, G),
        s5_b_re=d_b_re.reshape(G, P, S5_GROUP), s5_b_im=d_b_im.reshape(G, P, S5_GROUP),
        s5_c_re=from_cb(dcbr), s5_c_im=from_cb(dcbi), s5_d=d_s5d, s5_w_glu=d_w_glu, s5_w_up=d_w_su,
        dn_conv_w=d_conv, dn_a_log=d_alog_row[:, DN_HEADS:2 * DN_HEADS], dn_dt_bias=d_dtb_row[:, DN_HEADS:2 * DN_HEADS],
        dn_norm_w=d_norm_w, dn_w_up=d_w_du, w_out=d_w_out, final_norm_w=d_fw)
    return loss_row, grad_x, grads


def _place():
    x, y, c = lax.axis_index("x"), lax.axis_index("y"), lax.axis_index("c")
    return x, y, c


def _remote(src, dst, send_sem, recv_sem, to):
    return pltpu.make_async_remote_copy(src_ref=src, dst_ref=dst, send_sem=send_sem, recv_sem=recv_sem,
                                        device_id=to, device_id_type=MESH)


def _gather_exchange(shards, whole=(), by_columns=False):
    na, nw = len(shards), len(whole)

    def half_of(ref, a, half):
        rows, cols = shards[a].shape
        if by_columns:
            return ref.at[pl.ds(0, rows), pl.ds(half * (cols // 2), cols // 2)]
        return ref.at[pl.ds(half * (rows // 2), rows // 2)]

    def plan(ins, outs, send_sems, recv_sems, local_sems, receiving):
        x, y, c = _place()
        me = 2 * x + y
        sibling = (x, y, 1 - c)
        chips = [(1 - x, y), (x, 1 - y), (1 - x, 1 - y)]

        def part(a, chip, half):
            return half_of(outs[a].at[chip], a, half)

        own = [pltpu.make_async_copy(ins[a], outs[a].at[me], local_sems.at[a]) for a in range(na + nw)]
        sends, landed, passed, arrivals = [], [], [], []
        for a in range(na):
            for j, (px, py) in enumerate(chips):
                k = 6 * a + j
                sends.append(_remote(half_of(ins[a], a, c), part(a, me, c), send_sems.at[k], recv_sems.at[k], (px, py, c)))
                if receiving:
                    got, other = part(a, 2 * px + py, c), part(a, 2 * px + py, 1 - c)
                    landed.append(_remote(got, got, send_sems.at[k], recv_sems.at[k], (px, py, c)))
                    passed.append(_remote(got, got, send_sems.at[k + 3], recv_sems.at[k + 3], sibling))
                    arrivals.append(_remote(other, other, send_sems.at[k + 3], recv_sems.at[k + 3], sibling))
        for a in range(na, na + nw):
            for j, (px, py) in enumerate(chips):
                k = 6 * na + 3 * (a - na) + j
                sends.append(_remote(ins[a], outs[a].at[me], send_sems.at[k], recv_sems.at[k], (px, py, c)))
                if receiving:
                    arrivals.append(_remote(ins[a], outs[a].at[2 * px + py], send_sems.at[k], recv_sems.at[k], (px, py, c)))
        return own, sends, landed, passed, arrivals

    def start(ins, outs, *sems):
        own, sends, _, _, _ = plan(ins, outs, *sems, False)
        for cp in own + sends:
            cp.start()

    def finish(ins, outs, *sems):
        own, sends, landed, passed, arrivals = plan(ins, outs, *sems, True)
        for got, fwd in zip(landed, passed):
            got.wait_recv()
            fwd.start()
        for cp in arrivals:
            cp.wait_recv()
        for cp in sends + passed:
            cp.wait_send()
        for cp in own:
            cp.wait()

    arrays = list(shards) + list(whole)
    return _Exchange(arrays, [_sds((N_CHIPS,) + s.shape, s.dtype) for s in arrays], 6 * na + 3 * nw, start, finish)


def _owners_exchange(csbs):
    na = len(csbs)

    def plan(ins, outs, send_sems, recv_sems, local_sems, receiving):
        x, y, c = _place()
        me = 2 * x + y
        sends, arrivals = [], []
        for a in range(na):
            for k in range(N_CHIPS - 1):
                j = (me + 1 + k) % N_CHIPS
                sends.append(_remote(ins[a].at[k], outs[a].at[2 - k], send_sems.at[3 * a + k], recv_sems.at[3 * a + 2 - k],
                                     (j // 2, j % 2, c)))
                if receiving:
                    arrivals.append(_remote(ins[a].at[k], outs[a].at[k], send_sems.at[3 * a + k], recv_sems.at[3 * a + k], (x, y, c)))
        return sends, arrivals

    def start(ins, outs, *sems):
        for cp in plan(ins, outs, *sems, False)[0]:
            cp.start()

    def finish(ins, outs, *sems):
        sends, arrivals = plan(ins, outs, *sems, True)
        for cp in arrivals:
            cp.wait_recv()
        for cp in sends:
            cp.wait_send()

    return _Exchange(csbs, [_sds(g.shape, g.dtype) for g in csbs], 3 * na, start, finish)


def _run_exchange(ex, name):
    n_in, n_out = len(ex.ins), len(ex.out_shapes)

    def body(*refs):
        ins, outs, sems = refs[:n_in], refs[n_in:n_in + n_out], refs[n_in + n_out:]
        ex.start(ins, outs, *sems)
        ex.finish(ins, outs, *sems)

    return pl.pallas_call(
        body, name=name, in_specs=[ANY] * n_in, out_specs=[ANY] * n_out, out_shape=ex.out_shapes,
        scratch_shapes=[pltpu.SemaphoreType.DMA((ex.n_sems,)) for _ in range(3)],
    )(*ex.ins)


def _swap_halves(gxs, name):
    na = len(gxs)

    def body(*refs):
        ins, outs = refs[:na], refs[na:2 * na]
        send_sems, recv_sems = refs[2 * na:]
        x, y, c = _place()
        cps = [_remote(ins[a].at[pl.ds(0, N_CHIPS), pl.ds(1 - c, 1)], outs[a], send_sems.at[a], recv_sems.at[a], (x, y, 1 - c))
               for a in range(na)]
        for cp in cps:
            cp.start()
        for cp in cps:
            cp.wait()

    return pl.pallas_call(
        body, name=name, in_specs=[ANY] * na, out_specs=[ANY] * na,
        out_shape=[_sds((N_CHIPS, 1) + g.shape[2:], g.dtype) for g in gxs],
        scratch_shapes=[pltpu.SemaphoreType.DMA((na,)), pltpu.SemaphoreType.DMA((na,))],
    )(*gxs)


def _share_halves(gfs):
    na = len(gfs)

    def body(*refs):
        ins, outs = refs[:na], refs[na:2 * na]
        send_sems, recv_sems = refs[2 * na:]
        x, y, c = _place()
        cps = [_remote(ins[a].at[pl.ds(c, 1)], outs[a].at[pl.ds(c, 1)], send_sems.at[a], recv_sems.at[a], (x, y, 1 - c))
               for a in range(na)]
        for cp in cps:
            cp.start()
        for a in range(na):
            cps[a].wait_send()
            _remote(ins[a].at[pl.ds(1 - c, 1)], outs[a].at[pl.ds(1 - c, 1)], send_sems.at[a], recv_sems.at[a], (x, y, 1 - c)).wait_recv()

    return pl.pallas_call(
        body, name="rs_share_halves", in_specs=[ANY] * na, out_specs=[ANY] * na,
        out_shape=[_sds(g.shape, g.dtype) for g in gfs], input_output_aliases={a: a for a in range(na)},
        scratch_shapes=[pltpu.SemaphoreType.DMA((na,)), pltpu.SemaphoreType.DMA((na,))],
    )(*gfs)


def _row_tile(rows, cols, budget=5 << 18):
    fits = [t for t in range(16, rows + 1, 16) if rows % t == 0 and t * cols * 4 <= budget]
    return max(fits) if fits else rows


def _chip_sums(gx, r1, where):
    _, _, r2, cd = gx.shape
    tr = _row_tile(r2, cd)

    def body(w_ref, a_ref, b_ref, o_ref):
        o_ref[...] = (a_ref[0] + b_ref[0]).astype(BF16)

    other = lambda k, i, w: ((w[1] + 1 + k) % N_CHIPS, w[0], i, 0)
    other0 = lambda k, i, w: ((w[1] + 1 + k) % N_CHIPS, 0, i, 0)
    return pl.pallas_call(
        body, name="rs_chip_sums",
        grid_spec=pltpu.PrefetchScalarGridSpec(
            num_scalar_prefetch=1, grid=(N_CHIPS - 1, r2 // tr),
            in_specs=[pl.BlockSpec((1, 1, tr, cd), other), pl.BlockSpec((1, 1, tr, cd), other0)],
            out_specs=pl.BlockSpec((1, tr, cd), lambda k, i, w: (k, i, 0))),
        out_shape=_sds((N_CHIPS - 1, r2, cd), BF16), compiler_params=_cp(("parallel", "parallel")),
    )(where, gx, r1)


def _owner_sum(gx, r1, r2x, where):
    _, _, r2, cd = gx.shape
    tr = _row_tile(r2, cd)

    def body(w_ref, a_ref, b_ref, r_ref, o_ref):
        acc = a_ref[0, 0] + b_ref[0, 0]
        for k in range(N_CHIPS - 1):
            acc = acc + r_ref[k].astype(F32)
        o_ref[0] = acc

    return pl.pallas_call(
        body, name="rs_owner_sum",
        grid_spec=pltpu.PrefetchScalarGridSpec(
            num_scalar_prefetch=1, grid=(r2 // tr,),
            in_specs=[pl.BlockSpec((1, 1, tr, cd), lambda i, w: (w[1], w[0], i, 0)),
                      pl.BlockSpec((1, 1, tr, cd), lambda i, w: (w[1], 0, i, 0)),
                      pl.BlockSpec((N_CHIPS - 1, tr, cd), lambda i, w: (0, i, 0))],
            out_specs=pl.BlockSpec((1, tr, cd), lambda i, w: (w[0], i, 0))),
        out_shape=_sds((2, r2, cd)), compiler_params=_cp(("parallel",)),
    )(where, gx, r1, r2x)


def _adamw_math(w, g, m, v):
    m = ADAM_B1 * m + (1.0 - ADAM_B1) * g
    v = ADAM_B2 * v + (1.0 - ADAM_B2) * (g * g)
    m_hat = m / (1.0 - ADAM_B1 ** ADAM_STEP)
    v_hat = v / (1.0 - ADAM_B2 ** ADAM_STEP)
    delta = -ADAM_LR * (m_hat / (jnp.sqrt(v_hat) + ADAM_EPS) + ADAM_WD * w)
    return delta, m, v


def _adamw(w, g, m, v, name):
    rows, cd = w.shape
    if rows % 16 == 0:
        tr, tc = _row_tile(rows, cd, budget=3 << 19), cd
    else:
        tr, tc = rows, (128 if rows * cd * 4 > (3 << 19) else cd)
    assert rows % tr == 0 and cd % tc == 0

    def body(w_ref, g_ref, m_ref, v_ref, d_ref, mo_ref, vo_ref):
        d, mm, vv = _adamw_math(w_ref[...], g_ref[...], m_ref[...], v_ref[...])
        d_ref[...] = d
        mo_ref[...] = mm
        vo_ref[...] = vv

    blk = pl.BlockSpec((tr, tc), lambda i, j: (i, j))
    return pl.pallas_call(
        body, name=name, grid=(rows // tr, cd // tc), in_specs=[blk] * 4, out_specs=[blk] * 3, out_shape=[_sds(w.shape)] * 3,
        compiler_params=_cp(("parallel", "parallel")),
    )(w, g, m, v)


def _small_allreduce_adamw(gp, wp, mp, vp):
    R = gp.shape[0]
    R2 = R // 2
    assert R2 % 8 == 0

    def body(g_ref, w_ref, m_ref, v_ref, go_ref, d_ref, mo_ref, vo_ref, sib, csum, land, send_sems, recv_sems):
        x, y, c = _place()
        me = 2 * x + y
        sibling = (x, y, 1 - c)
        chips = [(1 - x, y), (x, 1 - y), (1 - x, 1 - y)]
        swap = _remote(g_ref, sib, send_sems.at[0], recv_sems.at[0], sibling)
        swap.start()
        swap.wait()
        csum[...] = g_ref[...] + sib[...]
        half = csum.at[pl.ds(c * R2, R2)]
        land[me] = csum[pl.ds(c * R2, R2), :]
        cps = [_remote(half, land.at[me], send_sems.at[1 + j], recv_sems.at[1 + j], (px, py, c))
               for j, (px, py) in enumerate(chips)]
        for cp in cps:
            cp.start()
        for j, (px, py) in enumerate(chips):
            _remote(half, land.at[2 * px + py], send_sems.at[1 + j], recv_sems.at[1 + j], (px, py, c)).wait_recv()
        for cp in cps:
            cp.wait_send()
        mine = go_ref.at[pl.ds(c * R2, R2)]
        go_ref[pl.ds(c * R2, R2), :] = (land[0] + land[1]) + (land[2] + land[3])
        share = _remote(mine, mine, send_sems.at[4], recv_sems.at[4], sibling)
        share.start()
        share.wait_send()
        other = go_ref.at[pl.ds((1 - c) * R2, R2)]
        _remote(other, other, send_sems.at[4], recv_sems.at[4], sibling).wait_recv()
        d, mm, vv = _adamw_math(w_ref[...], go_ref[...], m_ref[...], v_ref[...])
        d_ref[...] = d
        mo_ref[...] = mm
        vo_ref[...] = vv

    vm = pl.BlockSpec(memory_space=pltpu.VMEM)
    return pl.pallas_call(
        body, name="small_allreduce_adamw", in_specs=[vm] * 4, out_specs=[vm] * 4, out_shape=[_sds((R, 128))] * 4,
        scratch_shapes=[pltpu.VMEM((R, 128), F32), pltpu.VMEM((R, 128), F32), pltpu.VMEM((N_CHIPS, R2, 128), F32),
                        pltpu.SemaphoreType.DMA((5,)), pltpu.SemaphoreType.DMA((5,))],
        compiler_params=_cp(),
    )(gp, wp, mp, vp)


def _pack(arrs):
    rows = []
    for a in arrs:
        f = a.reshape(-1)
        f = jnp.pad(f, (0, (-f.shape[0]) % 128))
        rows.append(f.reshape(-1, 128))
    p = jnp.concatenate(rows, axis=0)
    return jnp.pad(p, ((0, (-p.shape[0]) % 8), (0, 0)))


def _unpack(p, shapes):
    out, r = [], 0
    for s in shapes:
        n = math.prod(s)
        nr = -(-n // 128)
        out.append(p[r:r + nr].reshape(-1)[:n].reshape(s))
        r += nr
    return out


class _ExchangeOps(_PlainOps):
    def __init__(self, rest_shards, where):
        self.rest_shards, self.where = rest_shards, where
        self.reduced = []

    def in_proj(self, h, wt_perm):
        proj, g_glu, g_su, g_du, g_out = _mm(h, wt_perm, tb=True, name="in_proj", tm=1024, tn=1152,
                                             exchange=_gather_exchange(self.rest_shards))
        cat = lambda g: jnp.concatenate([g[j] for j in range(N_CHIPS)], axis=1)
        return proj, (g_glu.reshape(D_S5, D_S5), cat(g_su), cat(g_du), g_out.reshape(D_MODEL, D_MODEL))

    def _chip_sums(self, gxs, name):
        r1s = _swap_halves(gxs, name)
        return r1s, [_chip_sums(gx, r1, self.where) for gx, r1 in zip(gxs, r1s)]

    def rest_grads(self, d_w_glu, d_w_su, d_w_du, d_w_out):
        gxs = [d_w_glu.reshape(N_CHIPS, 2, D_S5 // 8, D_S5), d_w_su.reshape(N_CHIPS, 2, D_S5 // 2, D_MODEL // N_CHIPS),
               d_w_du.reshape(N_CHIPS, 2, D_DN // 2, D_MODEL // N_CHIPS), d_w_out.reshape(N_CHIPS, 2, D_MODEL // 8, D_MODEL)]
        r1s, csbs = self._chip_sums(gxs, "rs_swap_rest")
        self.rest = (gxs, r1s, csbs)

    def d_w_in(self, h, dproj):
        gxs, r1s, csbs = self.rest
        d_wt_perm, *r2s = _mm(dproj, h, ta=True, name="d_w_in", tm=1152, tn=1024, exchange=_owners_exchange(csbs))
        self.reduced = list(zip(gxs, r1s, r2s))
        return d_wt_perm

    def d_h(self, dproj, wt_perm, d_wt_perm):
        gx = _wt_shards(d_wt_perm).reshape(N_CHIPS, 2, WT_ROWS // 2, D_MODEL)
        (r1,), (csb,) = self._chip_sums([gx], "rs_swap_w_in")
        dh, r2 = _mm(dproj, wt_perm, name="d_h", tm=2048, tn=1024, tk=1152, exchange=_owners_exchange([csb]))
        self.reduced = [(gx, r1, r2)] + self.reduced
        return dh


WT_SHARD = D_IN // N_CHIPS
WT_ROWS = 2592


def _wt_perm(orig):
    pad = jnp.zeros((D_IN_PAD - D_IN, orig.shape[1]), orig.dtype)
    return jnp.concatenate([orig[:OFF_GS], orig[OFF_GS + 2 * DN_HEADS:], orig[OFF_GS:OFF_GS + 2 * DN_HEADS], pad], axis=0)


def _wt_shards(perm):
    orig_rows = lambda lo, hi: ([perm[lo:hi]] if hi <= OFF_GS else [perm[lo - 2 * DN_HEADS:hi - 2 * DN_HEADS]]
                                if lo >= OFF_GS + 2 * DN_HEADS else
                                [perm[lo:OFF_GS], perm[OFF_B:OFF_B + 2 * DN_HEADS], perm[OFF_GS:hi - 2 * DN_HEADS]])
    pad = jnp.zeros((WT_ROWS - WT_SHARD, perm.shape[1]), perm.dtype)
    return jnp.stack([jnp.concatenate(orig_rows(j * WT_SHARD, (j + 1) * WT_SHARD) + [pad], axis=0) for j in range(N_CHIPS)])


_SMALL = ("ln_w", "s5_lam_re", "s5_lam_im", "s5_log_step", "s5_b_re", "s5_b_im", "s5_c_re", "s5_c_im", "s5_d",
          "dn_a_log", "dn_dt_bias", "dn_norm_w", "final_norm_w")
_BIG = ("w_in", "s5_w_glu", "s5_w_up", "dn_w_up", "w_out")
_ORDER = ("ln_w", "w_in", "s5_lam_re", "s5_lam_im", "s5_log_step", "s5_b_re", "s5_b_im", "s5_c_re", "s5_c_im", "s5_d",
          "s5_w_glu", "s5_w_up", "dn_conv_w", "dn_a_log", "dn_dt_bias", "dn_norm_w", "dn_w_up", "w_out", "final_norm_w")


def kernel(x, ln_w, w_in, s5_lam_re, s5_lam_im, s5_log_step, s5_b_re, s5_b_im, s5_c_re, s5_c_im, s5_d, s5_w_glu, s5_w_up, dn_conv_w, dn_a_log, dn_dt_bias, dn_norm_w, dn_w_up, w_out, final_norm_w, loss_target, m_ln_w, m_w_in, m_s5_lam_re, m_s5_lam_im, m_s5_log_step, m_s5_b_re, m_s5_b_im, m_s5_c_re, m_s5_c_im, m_s5_d, m_s5_w_glu, m_s5_w_up, m_dn_conv_w, m_dn_a_log, m_dn_dt_bias, m_dn_norm_w, m_dn_w_up, m_w_out, m_final_norm_w, v_ln_w, v_w_in, v_s5_lam_re, v_s5_lam_im, v_s5_log_step, v_s5_b_re, v_s5_b_im, v_s5_c_re, v_s5_c_im, v_s5_d, v_s5_w_glu, v_s5_w_up, v_dn_conv_w, v_dn_a_log, v_dn_dt_bias, v_dn_norm_w, v_dn_w_up, v_w_out, v_final_norm_w):
    w = dict(ln_w=ln_w, w_in=w_in, s5_lam_re=s5_lam_re, s5_lam_im=s5_lam_im, s5_log_step=s5_log_step, s5_b_re=s5_b_re,
             s5_b_im=s5_b_im, s5_c_re=s5_c_re, s5_c_im=s5_c_im, s5_d=s5_d, s5_w_glu=s5_w_glu, s5_w_up=s5_w_up,
             dn_conv_w=dn_conv_w, dn_a_log=dn_a_log, dn_dt_bias=dn_dt_bias, dn_norm_w=dn_norm_w, dn_w_up=dn_w_up, w_out=w_out,
             final_norm_w=final_norm_w)
    m = dict(ln_w=m_ln_w, w_in=m_w_in, s5_lam_re=m_s5_lam_re, s5_lam_im=m_s5_lam_im, s5_log_step=m_s5_log_step,
             s5_b_re=m_s5_b_re, s5_b_im=m_s5_b_im, s5_c_re=m_s5_c_re, s5_c_im=m_s5_c_im, s5_d=m_s5_d, s5_w_glu=m_s5_w_glu,
             s5_w_up=m_s5_w_up, dn_conv_w=m_dn_conv_w, dn_a_log=m_dn_a_log, dn_dt_bias=m_dn_dt_bias, dn_norm_w=m_dn_norm_w,
             dn_w_up=m_dn_w_up, w_out=m_w_out, final_norm_w=m_final_norm_w)
    v = dict(ln_w=v_ln_w, w_in=v_w_in, s5_lam_re=v_s5_lam_re, s5_lam_im=v_s5_lam_im, s5_log_step=v_s5_log_step,
             s5_b_re=v_s5_b_re, s5_b_im=v_s5_b_im, s5_c_re=v_s5_c_re, s5_c_im=v_s5_c_im, s5_d=v_s5_d, s5_w_glu=v_s5_w_glu,
             s5_w_up=v_s5_w_up, dn_conv_w=v_dn_conv_w, dn_a_log=v_dn_a_log, dn_dt_bias=v_dn_dt_bias, dn_norm_w=v_dn_norm_w,
             dn_w_up=v_dn_w_up, w_out=v_w_out, final_norm_w=v_final_norm_w)
    xi, yi, ci = _place()
    chip = 2 * xi + yi
    where = jnp.stack([ci, chip]).astype(jnp.int32)

    tr = lambda a: jnp.swapaxes(a[0], 0, 1)
    g_in, g_conv = _run_exchange(_gather_exchange([tr(w_in).astype(BF16)], [dn_conv_w[0]], by_columns=True), "gather_w_in")
    cat = lambda g: jnp.concatenate([g[j] for j in range(N_CHIPS)], axis=1)
    w_perm = _wt_perm(g_in.reshape(D_IN, D_MODEL))

    ops = _ExchangeOps([w[n][0].astype(BF16) for n in _BIG[1:]], where)
    loss_row, grad_x, g = _local_step(
        x[0], loss_target[0], ln_w, w_perm, s5_lam_re[0], s5_lam_im[0], s5_log_step, s5_b_re[0], s5_b_im[0], s5_c_re[0],
        s5_c_im[0], s5_d, cat(g_conv), dn_a_log, dn_dt_bias, dn_norm_w, final_norm_w[None], ops)
    loss = lax.psum(loss_row[0, 0], ("x", "y", "c"))

    gfs = [_owner_sum(gx, r1, r2x, where) for gx, r1, r2x in ops.reduced]
    gfs = _share_halves(gfs)
    grads, deltas, new_m, new_v = {}, {}, {}, {}
    gt = gfs[0].reshape(WT_ROWS, D_MODEL)[:WT_SHARD]
    d_, m_, v_ = _adamw(tr(w_in), gt, tr(m_w_in), tr(v_w_in), "adamw_w_in")
    grads["w_in"], deltas["w_in"], new_m["w_in"], new_v["w_in"] = (jnp.swapaxes(a, 0, 1)[None] for a in (gt, d_, m_, v_))
    for n, gf in zip(_BIG[1:], gfs[1:]):
        shp = w[n].shape
        g2 = gf.reshape(shp[1:])
        d_, m_, v_ = _adamw(w[n][0], g2, m[n][0], v[n][0], "adamw_" + n)
        grads[n], deltas[n], new_m[n], new_v[n] = g2.reshape(shp), d_.reshape(shp), m_.reshape(shp), v_.reshape(shp)

    gp = _pack([g[n] for n in _SMALL] + [g["dn_conv_w"]])
    zc = jnp.zeros((CONV_K, 3 * D_DN), F32)
    go, dl, mo, vo = _small_allreduce_adamw(gp, _pack([w[n] for n in _SMALL] + [zc]), _pack([m[n] for n in _SMALL] + [zc]),
                                            _pack([v[n] for n in _SMALL] + [zc]))
    shapes = [w[n].shape for n in _SMALL] + [(CONV_K, 3 * D_DN)]
    for dst, src in ((grads, go), (deltas, dl), (new_m, mo), (new_v, vo)):
        for n, a in zip(_SMALL, _unpack(src, shapes)):
            dst[n] = a
    cc = 3 * D_DN // N_CHIPS
    g_conv_mine = lax.dynamic_slice(_unpack(go, shapes)[-1], (0, chip * cc), (CONV_K, cc))
    d_, m_, v_ = _adamw(dn_conv_w[0], g_conv_mine, m_dn_conv_w[0], v_dn_conv_w[0], "adamw_dn_conv_w")
    grads["dn_conv_w"], deltas["dn_conv_w"], new_m["dn_conv_w"], new_v["dn_conv_w"] = (
        g_conv_mine[None], d_[None], m_[None], v_[None])

    return (loss, grad_x[None], *[grads[n] for n in _ORDER], *[deltas[n] for n in _ORDER], *[new_m[n] for n in _ORDER],
            *[new_v[n] for n in _ORDER])
```

```python
import functools
import math

import jax
import jax.numpy as jnp
from jax import lax
from jax.experimental import pallas as pl
from jax.experimental.pallas import tpu as pltpu

F32 = jnp.float32
BF16 = jnp.bfloat16
HI = lax.Precision.HIGHEST
MESH = pl.DeviceIdType.MESH
ANY = pl.BlockSpec(memory_space=pl.ANY)

EPS = 1e-6
D_MODEL = 2048
D_S5 = 1024
S5_GROUP = 16
S5_GROUPS = 64
S5_STATE = 64
S5_BLOCKS = 8
S5_SEG = 8
DN_HEADS = 8
DN_HEAD_DIM = 128
D_DN = 1024
CONV_K = 4
CHUNK = 64
D_IN = 10256
D_IN_PAD = 10368
OFF_US, OFF_ZS, OFF_Q, OFF_K, OFF_V, OFF_ZD, OFF_GS, OFF_GD, OFF_B = 0, 1024, 2048, 3072, 4096, 5120, 6144, 8192, 10240
N_CHIPS = 4
N_DEV = 8
VMEM_LIMIT = 56 * 1024 * 1024

ADAM_LR = 0.001
ADAM_B1 = 0.9
ADAM_B2 = 0.999
ADAM_EPS = 1e-08
ADAM_WD = 0.01
ADAM_STEP = 10


def _cp(sem=None):
    return pltpu.CompilerParams(dimension_semantics=sem, vmem_limit_bytes=VMEM_LIMIT)


def _sds(shape, dtype=F32):
    return jax.ShapeDtypeStruct(tuple(shape), dtype)


def _sigmoid(x):
    return 1.0 / (1.0 + jnp.exp(-x))


def _silu(x):
    return x * _sigmoid(x)


def _dsilu(x):
    s = _sigmoid(x)
    return s * (1.0 + x * (1.0 - s))


class _Exchange:
    def __init__(self, ins, out_shapes, n_sems, start, finish):
        self.ins, self.out_shapes, self.n_sems, self.start, self.finish = list(ins), list(out_shapes), n_sems, start, finish


def _mm(a, b, *, name, ta=False, tb=False, out_dtype=F32, tm=512, tn=512, tk=2048, shard_out=False, exchange=None):
    if ta:
        K, M = a.shape
    else:
        M, K = a.shape
    if tb:
        N, K2 = b.shape
    else:
        K2, N = b.shape
    assert K == K2, (a.shape, b.shape)
    tm, tn, tk = min(tm, M), min(tn, N), min(tk, K)
    assert M % tm == 0 and N % tn == 0 and K % tk == 0, (M, N, K, tm, tn, tk)
    nk = K // tk
    dims = (((0 if ta else 1,), (1 if tb else 0,)), ((), ()))

    gm, gn = M // tm, N // tn
    n_in = len(exchange.ins) if exchange else 0
    n_out = len(exchange.out_shapes) if exchange else 0

    def body(*refs):
        a_ref, b_ref, xin, o_ref = refs[0], refs[1], refs[2:2 + n_in], refs[2 + n_in]
        xout, rest = refs[3 + n_in:3 + n_in + n_out], refs[3 + n_in + n_out:]
        i, j, k = pl.program_id(0), pl.program_id(1), pl.program_id(2)
        if exchange:
            sems = rest[-3:]

            @pl.when(jnp.logical_and(jnp.logical_and(i == 0, j == 0), k == 0))
            def _():
                exchange.start(xin, xout, *sems)

        p = lax.dot_general(a_ref[...].astype(BF16), b_ref[...].astype(BF16), dims, preferred_element_type=F32)
        if nk == 1:
            o_ref[...] = p.astype(out_dtype).reshape(o_ref.shape)
        else:
            acc_ref = rest[0]

            @pl.when(k == 0)
            def _():
                acc_ref[...] = p

            @pl.when(k > 0)
            def _():
                acc_ref[...] += p

            @pl.when(k == nk - 1)
            def _():
                o_ref[...] = acc_ref[...].astype(out_dtype).reshape(o_ref.shape)

        if exchange:
            @pl.when(jnp.logical_and(jnp.logical_and(i == gm - 1, j == gn - 1), k == nk - 1))
            def _():
                exchange.finish(xin, xout, *sems)

    a_spec = pl.BlockSpec((tk, tm), lambda i, j, k: (k, i)) if ta else pl.BlockSpec((tm, tk), lambda i, j, k: (i, k))
    b_spec = pl.BlockSpec((tn, tk), lambda i, j, k: (j, k)) if tb else pl.BlockSpec((tk, tn), lambda i, j, k: (k, j))
    if shard_out:
        o_spec = pl.BlockSpec((1, tm, tn), lambda i, j, k: (j, i, 0))
        o_shape = _sds((N // tn, M, tn), out_dtype)
    else:
        o_spec = pl.BlockSpec((tm, tn), lambda i, j, k: (i, j))
        o_shape = _sds((M, N), out_dtype)
    scratch = [pltpu.VMEM((tm, tn), F32)] if nk > 1 else []
    if not exchange:
        return pl.pallas_call(
            body, name=name, grid=(gm, gn, nk), in_specs=[a_spec, b_spec], out_specs=o_spec, out_shape=o_shape,
            scratch_shapes=scratch, compiler_params=_cp(("parallel", "parallel", "arbitrary")),
        )(a, b)
    scratch += [pltpu.SemaphoreType.DMA((exchange.n_sems,)) for _ in range(3)]
    return pl.pallas_call(
        body, name=name, grid=(gm, gn, nk), in_specs=[a_spec, b_spec] + [ANY] * n_in, out_specs=[o_spec] + [ANY] * n_out,
        out_shape=[o_shape] + exchange.out_shapes, scratch_shapes=scratch,
        compiler_params=_cp(("arbitrary", "arbitrary", "arbitrary")),
    )(a, b, *exchange.ins)


def _ln_fwd(x, w):
    L, D = x.shape
    tm = min(256, L)

    def body(x_ref, w_ref, h_ref, r_ref):
        xv = x_ref[...]
        r = lax.rsqrt(jnp.mean(xv * xv, axis=-1, keepdims=True) + EPS)
        h_ref[...] = (xv * r * w_ref[...]).astype(BF16)
        r_ref[...] = r

    return pl.pallas_call(
        body, name="ln_fwd", grid=(L // tm,),
        in_specs=[pl.BlockSpec((tm, D), lambda i: (i, 0)), pl.BlockSpec((1, D), lambda i: (0, 0))],
        out_specs=[pl.BlockSpec((tm, D), lambda i: (i, 0)), pl.BlockSpec((tm, 1), lambda i: (i, 0))],
        out_shape=[_sds((L, D), BF16), _sds((L, 1))], compiler_params=_cp(("parallel",)),
    )(x, w)


def _ln_bwd(x, r, w, dh, dx2):
    L, D = x.shape
    tm = min(256, L)

    def body(x_ref, r_ref, w_ref, dh_ref, dx2_ref, dx_ref, dw_ref):
        i = pl.program_id(0)
        xv, rv, dhv = x_ref[...], r_ref[...], dh_ref[...]
        t = dhv * w_ref[...]
        m = jnp.mean(t * xv, axis=-1, keepdims=True)
        dx_ref[...] = dx2_ref[...] + rv * t - xv * (rv * rv * rv) * m
        part = jnp.sum(dhv * xv * rv, axis=0, keepdims=True)

        @pl.when(i == 0)
        def _():
            dw_ref[...] = part

        @pl.when(i > 0)
        def _():
            dw_ref[...] += part

    row = pl.BlockSpec((tm, D), lambda i: (i, 0))
    return pl.pallas_call(
        body, name="ln_bwd", grid=(L // tm,),
        in_specs=[row, pl.BlockSpec((tm, 1), lambda i: (i, 0)), pl.BlockSpec((1, D), lambda i: (0, 0)), row, row],
        out_specs=[row, pl.BlockSpec((1, D), lambda i: (0, 0))],
        out_shape=[_sds((L, D)), _sds((1, D))], compiler_params=_cp(("arbitrary",)),
    )(x, r, w, dh, dx2)


def _s5_param_math(lam_re, lam_im, log_step, b_re, b_im, expand):
    step = jnp.exp(log_step)
    mag = jnp.exp(lam_re * step)
    abar_re = mag * jnp.cos(lam_im * step)
    abar_im = mag * jnp.sin(lam_im * step)
    den = lam_re * lam_re + lam_im * lam_im
    xr = abar_re - 1.0
    f_re = (xr * lam_re + abar_im * lam_im) / den
    f_im = (abar_im * lam_re - xr * lam_im) / den
    fe_re = jnp.dot(f_re, expand, precision=HI, preferred_element_type=F32)
    fe_im = jnp.dot(f_im, expand, precision=HI, preferred_element_type=F32)
    bb_re = fe_re * b_re - fe_im * b_im
    bb_im = fe_re * b_im + fe_im * b_re
    return abar_re, abar_im, bb_re, bb_im


def _s5_expand():
    p = lax.broadcasted_iota(jnp.int32, (S5_STATE, S5_STATE * S5_GROUP), 0)
    q = lax.broadcasted_iota(jnp.int32, (S5_STATE, S5_STATE * S5_GROUP), 1)
    return (q // S5_GROUP == p).astype(F32)


def _s5_param_fwd(lam_re, lam_im, log_step, b_re, b_im):
    G, P = lam_re.shape

    def body(lr, li, ls, br, bi, ar_o, ai_o, bbr_o, bbi_o):
        outs = _s5_param_math(lr[...], li[...], ls[...], br[...], bi[...], _s5_expand())
        for o, v in zip((ar_o, ai_o, bbr_o, bbi_o), outs):
            o[...] = v

    return pl.pallas_call(
        body, name="s5_param_fwd",
        out_shape=[_sds((G, P)), _sds((G, P)), _sds(b_re.shape), _sds(b_re.shape)], compiler_params=_cp(),
    )(lam_re, lam_im, log_step, b_re, b_im)


def _s5_param_bwd(lam_re, lam_im, log_step, b_re, b_im, dar, dai, dbbr, dbbi):
    G, P = lam_re.shape

    def body(lr, li, ls, br, bi, g0, g1, g2, g3, dlr, dli, dls, dbr, dbi):
        ex = _s5_expand()
        _, f = jax.vjp(lambda a, b, c, d, e: _s5_param_math(a, b, c, d, e, ex), lr[...], li[...], ls[...], br[...], bi[...])
        grads = f((g0[...], g1[...], g2[...], g3[...]))
        for o, v in zip((dlr, dli, dls, dbr, dbi), grads):
            o[...] = v

    return pl.pallas_call(
        body, name="s5_param_bwd",
        out_shape=[_sds((G, P)), _sds((G, P)), _sds((G, 1)), _sds(b_re.shape), _sds(b_re.shape)], compiler_params=_cp(),
    )(lam_re, lam_im, log_step, b_re, b_im, dar, dai, dbbr, dbbi)


def _to_segs(src_ref, dst_ref, L):
    S = L // S5_SEG

    def body(j, carry):
        dst_ref[pl.ds(pl.multiple_of(S5_SEG * j, S5_SEG), S5_SEG), :] = src_ref[pl.ds(j, S5_SEG, stride=S), :]
        return carry

    lax.fori_loop(0, S, body, 0, unroll=8)


def _from_segs(src_ref, L, write):
    S = L // S5_SEG
    for seg in range(S5_SEG):
        def body(jb, carry, seg=seg):
            j0 = 16 * jb
            write(pl.multiple_of(seg * S + j0, 16), src_ref[pl.ds(S5_SEG * j0 + seg, 16, stride=S5_SEG), :])
            return carry

        lax.fori_loop(0, S // 16, body, 0, unroll=4)


def _scan_segs(ar, ai, re_ref, im_ref, end_r_ref, end_i_ref, c_r_ref, c_i_ref, L, tile0, reverse):
    S = L // S5_SEG
    NB, LN = re_ref.shape[0], 128
    assert S & (S - 1) == 0
    tile = lambda j: pl.ds(pl.multiple_of(S5_SEG * (tile0 + j), S5_SEG), S5_SEG)
    ar8 = [jnp.broadcast_to(ar[:, b * LN:(b + 1) * LN], (S5_SEG, LN)) for b in range(NB)]
    ai8 = [jnp.broadcast_to(ai[:, b * LN:(b + 1) * LN], (S5_SEG, LN)) for b in range(NB)]

    def step(idx, carry):
        rows = tile(S - 1 - idx if reverse else idx)
        out = []
        for b in range(NB):
            sr, si = carry[b]
            nr = ar8[b] * sr - ai8[b] * si + re_ref[b, rows, :]
            ni = ar8[b] * si + ai8[b] * sr + im_ref[b, rows, :]
            re_ref[b, rows, :] = nr
            im_ref[b, rows, :] = ni
            out.append((nr, ni))
        return tuple(out)

    z8 = jnp.zeros((S5_SEG, LN), F32)
    fin = lax.fori_loop(0, S, step, tuple((z8, z8) for _ in range(NB)), unroll=4)
    order = range(S5_SEG - 2, -1, -1) if reverse else range(1, S5_SEG)
    for b in range(NB):
        end_r_ref[b], end_i_ref[b] = fin[b]
        pr, pi = ar8[b][:1], ai8[b][:1]
        for _ in range(int(math.log2(S))):
            pr, pi = pr * pr - pi * pi, 2.0 * pr * pi
        first = S5_SEG - 1 if reverse else 0
        c_r_ref[b, pl.ds(first, 1), :] = jnp.zeros((1, LN), F32)
        c_i_ref[b, pl.ds(first, 1), :] = jnp.zeros((1, LN), F32)
        cr, ci = end_r_ref[b, pl.ds(first, 1), :], end_i_ref[b, pl.ds(first, 1), :]
        for i in order:
            c_r_ref[b, pl.ds(i, 1), :] = cr
            c_i_ref[b, pl.ds(i, 1), :] = ci
            er, ei = end_r_ref[b, pl.ds(i, 1), :], end_i_ref[b, pl.ds(i, 1), :]
            cr, ci = er + pr * cr - pi * ci, ei + pr * ci + pi * cr

    entering = [(c_r_ref[b], c_i_ref[b]) for b in range(NB)]

    def fix(idx, carry):
        rows = tile(S - 1 - idx if reverse else idx)
        out = []
        for b in range(NB):
            pr, pi = carry[b]
            cr, ci = entering[b]
            re_ref[b, rows, :] += pr * cr - pi * ci
            im_ref[b, rows, :] += pr * ci + pi * cr
            out.append((pr * ar8[b] - pi * ai8[b], pr * ai8[b] + pi * ar8[b]))
        return tuple(out)

    lax.fori_loop(0, S, fix, tuple((ar8[b], ai8[b]) for b in range(NB)), unroll=4)


def _s5_seg_scratch(L, cs, pad):
    NB = cs // 128
    small = [pltpu.VMEM((NB, S5_SEG, 128), F32) for _ in range(4)]
    return [pltpu.VMEM((NB, L + pad, 128), F32), pltpu.VMEM((NB, L + pad, 128), F32)] + small


def _s5_core_fwd(proj, wbr, wbi, a_re, a_im, cbr, cbi):
    L = proj.shape[0]
    nb, ci, cs = wbr.shape
    NB = cs // 128

    def body(u_ref, wbr_ref, wbi_ref, ar_ref, ai_ref, cbr_ref, cbi_ref, y_ref, sr, si, er, ei, cr, cim, up, yp):
        _to_segs(u_ref, up, L)
        u = up[...].astype(BF16)
        for b in range(NB):
            lanes = pl.ds(b * 128, 128)
            sr[b] = jnp.dot(u, wbr_ref[0, :, lanes], preferred_element_type=F32)
            si[b] = jnp.dot(u, wbi_ref[0, :, lanes], preferred_element_type=F32)
        _scan_segs(ar_ref[...], ai_ref[...], sr, si, er, ei, cr, cim, L, 0, False)
        y = jnp.zeros((L, ci), F32)
        for b in range(NB):
            lanes = pl.ds(b * 128, 128)
            y = y + (jnp.dot(sr[b].astype(BF16), cbr_ref[0, lanes, :], preferred_element_type=F32)
                     - jnp.dot(si[b].astype(BF16), cbi_ref[0, lanes, :], preferred_element_type=F32))
        yp[...] = y

        def write(row, val):
            y_ref[pl.ds(row, 16), :] = val

        _from_segs(yp, L, write)

    wspec = pl.BlockSpec((1, ci, cs), lambda j: (j, 0, 0))
    aspec = pl.BlockSpec((1, cs), lambda j: (0, j))
    cspec = pl.BlockSpec((1, cs, ci), lambda j: (j, 0, 0))
    return pl.pallas_call(
        body, name="s5_core_fwd", grid=(nb,),
        in_specs=[pl.BlockSpec((L, ci), lambda j: (0, OFF_US // ci + j)), wspec, wspec, aspec, aspec, cspec, cspec],
        out_specs=pl.BlockSpec((L, ci), lambda j: (0, j)), out_shape=_sds((L, nb * ci)),
        scratch_shapes=_s5_seg_scratch(L, cs, 0) + [pltpu.VMEM((L, ci), F32), pltpu.VMEM((L, ci), F32)],
        compiler_params=_cp(("arbitrary",)),
    )(proj, wbr, wbi, a_re, a_im, cbr, cbi)


def _s5_core_bwd(proj, wbr, wbi, a_re, a_im, cbr, cbi, dyc, du1):
    L = proj.shape[0]
    nb, ci, cs = wbr.shape
    NB = cs // 128
    S = L // S5_SEG
    PAD = S5_SEG

    def body(u_ref, wbr_ref, wbi_ref, ar_ref, ai_ref, cbr_ref, cbi_ref, dy_ref, du1_ref,
             du_ref, dwbr_ref, dwbi_ref, dcbr_ref, dcbi_ref, dar_ref, dai_ref,
             sr, si, er, ei, cr, cim, lr, li, up, dyp, dup):
        tn = (((0,), (0,)), ((), ()))
        nt = (((1,), (1,)), ((), ()))
        _to_segs(u_ref, up, L)
        _to_segs(dy_ref, dyp, L)
        _to_segs(du1_ref, dup, L)
        u = up[...].astype(BF16)
        dy = dyp[...].astype(BF16)
        ar, ai = ar_ref[...], ai_ref[...]
        for b in range(NB):
            lanes = pl.ds(b * 128, 128)
            sr[b, pl.ds(PAD, L), :] = jnp.dot(u, wbr_ref[0, :, lanes], preferred_element_type=F32)
            si[b, pl.ds(PAD, L), :] = jnp.dot(u, wbi_ref[0, :, lanes], preferred_element_type=F32)
        _scan_segs(ar, ai, sr, si, er, ei, cr, cim, L, 1, False)
        for b in range(NB):
            lanes = pl.ds(b * 128, 128)
            sr[b, pl.ds(0, PAD), :] = cr[b]
            si[b, pl.ds(0, PAD), :] = cim[b]
            lr[b] = lax.dot_general(dy, cbr_ref[0, lanes, :], nt, preferred_element_type=F32)
            li[b] = -lax.dot_general(dy, cbi_ref[0, lanes, :], nt, preferred_element_type=F32)
            dcbr_ref[0, lanes, :] = lax.dot_general(sr[b, pl.ds(PAD, L), :].astype(BF16), dy, tn, preferred_element_type=F32)
            dcbi_ref[0, lanes, :] = -lax.dot_general(si[b, pl.ds(PAD, L), :].astype(BF16), dy, tn, preferred_element_type=F32)
        _scan_segs(ar, -ai, lr, li, er, ei, cr, cim, L, 0, True)

        def da_step(j, carry):
            rows = pl.ds(pl.multiple_of(S5_SEG * j, S5_SEG), S5_SEG)
            out = []
            for b in range(NB):
                dar, dai = carry[b]
                pr_, pi_ = sr[b, rows, :], si[b, rows, :]
                gr, gi = lr[b, rows, :], li[b, rows, :]
                out.append((dar + (gr * pr_ + gi * pi_), dai + (gi * pr_ - gr * pi_)))
            return tuple(out)

        z8 = jnp.zeros((S5_SEG, 128), F32)
        acc = lax.fori_loop(0, S, da_step, tuple((z8, z8) for _ in range(NB)), unroll=4)
        du = dup[...]
        for b in range(NB):
            lanes = pl.ds(b * 128, 128)
            dar_ref[:, lanes] = jnp.sum(acc[b][0], axis=0, keepdims=True)
            dai_ref[:, lanes] = jnp.sum(acc[b][1], axis=0, keepdims=True)
            gr, gi = lr[b].astype(BF16), li[b].astype(BF16)
            du = du + (lax.dot_general(gr, wbr_ref[0, :, lanes], nt, preferred_element_type=F32)
                       + lax.dot_general(gi, wbi_ref[0, :, lanes], nt, preferred_element_type=F32))
            dwbr_ref[0, :, lanes] = lax.dot_general(u, gr, tn, preferred_element_type=F32)
            dwbi_ref[0, :, lanes] = lax.dot_general(u, gi, tn, preferred_element_type=F32)
        dup[...] = du

        def write(row, val):
            du_ref[pl.ds(row, 16), :] = val.astype(BF16)

        _from_segs(dup, L, write)

    wspec = pl.BlockSpec((1, ci, cs), lambda j: (j, 0, 0))
    aspec = pl.BlockSpec((1, cs), lambda j: (0, j))
    cspec = pl.BlockSpec((1, cs, ci), lambda j: (j, 0, 0))
    col = pl.BlockSpec((L, ci), lambda j: (0, j))
    return pl.pallas_call(
        body, name="s5_core_bwd", grid=(nb,),
        in_specs=[pl.BlockSpec((L, ci), lambda j: (0, OFF_US // ci + j)), wspec, wspec, aspec, aspec, cspec, cspec, col, col],
        out_specs=[col, wspec, wspec, cspec, cspec, aspec, aspec],
        out_shape=[_sds((L, nb * ci), BF16), _sds(wbr.shape), _sds(wbr.shape), _sds(cbr.shape), _sds(cbr.shape),
                   _sds((1, nb * cs)), _sds((1, nb * cs))],
        scratch_shapes=(_s5_seg_scratch(L, cs, PAD) + [pltpu.VMEM((NB, L, 128), F32), pltpu.VMEM((NB, L, 128), F32)]
                        + [pltpu.VMEM((L, ci), F32) for _ in range(3)]),
        compiler_params=_cp(("arbitrary",)),
    )(proj, wbr, wbi, a_re, a_im, cbr, cbi, dyc, du1)


def _s5_post_math(yc, u, z, d, wg):
    y = yc + d * u
    y1 = jax.nn.gelu(y)
    t = jnp.dot(y1.astype(BF16), wg, preferred_element_type=F32)
    sg = _sigmoid(t)
    return y, y1, sg


def _s5_post_fwd(yc, proj, d, wg):
    L, W = yc.shape
    tm = min(256, L)

    def body(yc_ref, u_ref, z_ref, d_ref, wg_ref, o_ref):
        _, y1, sg = _s5_post_math(yc_ref[...], u_ref[...], z_ref[...], d_ref[...], wg_ref[...])
        o_ref[...] = (y1 * sg * _silu(z_ref[...])).astype(BF16)

    row = pl.BlockSpec((tm, W), lambda i: (i, 0))
    return pl.pallas_call(
        body, name="s5_post_fwd", grid=(L // tm,),
        in_specs=[row, pl.BlockSpec((tm, W), lambda i: (i, OFF_US // W)), pl.BlockSpec((tm, W), lambda i: (i, OFF_ZS // W)),
                  pl.BlockSpec((1, W), lambda i: (0, 0)), pl.BlockSpec((W, W), lambda i: (0, 0))],
        out_specs=row, out_shape=_sds((L, W), BF16), compiler_params=_cp(("parallel",)),
    )(yc, proj, proj, d, wg)


def _s5_post_bwd(yc, proj, d, wg, dout):
    L, W = yc.shape
    tm = min(256, L)

    def body(yc_ref, u_ref, z_ref, d_ref, wg_ref, do_ref, dyc_ref, du_ref, dz_ref, dd_ref, dwg_ref):
        i = pl.program_id(0)
        u, z, d_, wgv = u_ref[...], z_ref[...], d_ref[...], wg_ref[...]
        y, y1, sg = _s5_post_math(yc_ref[...], u, z, d_, wgv)
        dout_ = do_ref[...]
        y2 = y1 * sg
        dy2 = dout_ * _silu(z)
        dz_ref[...] = (dout_ * y2 * _dsilu(z)).astype(BF16)
        dt = (dy2 * y1 * sg * (1.0 - sg)).astype(BF16)
        dy1 = dy2 * sg + lax.dot_general(dt, wgv, (((1,), (1,)), ((), ())), preferred_element_type=F32)
        _, gelu_vjp = jax.vjp(jax.nn.gelu, y)
        dy = gelu_vjp(dy1)[0]
        dyc_ref[...] = dy
        du_ref[...] = dy * d_
        dd_part = jnp.sum(dy * u, axis=0, keepdims=True)
        dwg_part = lax.dot_general(y1.astype(BF16), dt, (((0,), (0,)), ((), ())), preferred_element_type=F32)

        @pl.when(i == 0)
        def _():
            dd_ref[...] = dd_part
            dwg_ref[...] = dwg_part

        @pl.when(i > 0)
        def _():
            dd_ref[...] += dd_part
            dwg_ref[...] += dwg_part

    row = pl.BlockSpec((tm, W), lambda i: (i, 0))
    return pl.pallas_call(
        body, name="s5_post_bwd", grid=(L // tm,),
        in_specs=[row, pl.BlockSpec((tm, W), lambda i: (i, OFF_US // W)), pl.BlockSpec((tm, W), lambda i: (i, OFF_ZS // W)),
                  pl.BlockSpec((1, W), lambda i: (0, 0)), pl.BlockSpec((W, W), lambda i: (0, 0)), row],
        out_specs=[row, row, row, pl.BlockSpec((1, W), lambda i: (0, 0)), pl.BlockSpec((W, W), lambda i: (0, 0))],
        out_shape=[_sds((L, W)), _sds((L, W)), _sds((L, W), BF16), _sds((1, W)), _sds((W, W))],
        compiler_params=_cp(("arbitrary",)),
    )(yc, proj, proj, d, wg, dout)


def _shift_down(x, s):
    if s == 0:
        return x
    rows = lax.broadcasted_iota(jnp.int32, x.shape, 0)
    return jnp.where(rows >= s, pltpu.roll(x, s, 0), 0.0)


def _shift_up(x, s):
    if s == 0:
        return x
    L = x.shape[0]
    rows = lax.broadcasted_iota(jnp.int32, x.shape, 0)
    return jnp.where(rows < L - s, pltpu.roll(x, L - s, 0), 0.0)


def _conv_pre(x, w):
    acc = w[CONV_K - 1:CONV_K, :] * x
    for s in range(1, CONV_K):
        acc = acc + w[CONV_K - 1 - s:CONV_K - s, :] * _shift_down(x, s)
    return acc


def _dn_conv_fwd(proj, conv_w):
    L = proj.shape[0]
    W = DN_HEAD_DIM
    nq = 2 * DN_HEADS

    def body(x_ref, w_ref, o_ref):
        j = pl.program_id(0)
        act = _silu(_conv_pre(x_ref[...], w_ref[...]))
        r = lax.rsqrt(jnp.sum(act * act, axis=-1, keepdims=True) + EPS)
        scale = jnp.where(j < DN_HEADS, DN_HEAD_DIM ** -0.5, 1.0)
        o_ref[...] = jnp.where(j < nq, act * r * scale, act)

    return pl.pallas_call(
        body, name="dn_conv_fwd", grid=(3 * DN_HEADS,),
        in_specs=[pl.BlockSpec((L, W), lambda j: (0, OFF_Q // W + j)), pl.BlockSpec((CONV_K, W), lambda j: (0, j))],
        out_specs=pl.BlockSpec((L, W), lambda j: (0, j)), out_shape=_sds((L, 3 * D_DN)), compiler_params=_cp(("parallel",)),
    )(proj, conv_w)


def _dn_conv_bwd(proj, conv_w, dout):
    L = proj.shape[0]
    W = DN_HEAD_DIM
    nq = 2 * DN_HEADS

    def body(x_ref, w_ref, do_ref, dx_ref, dw_ref):
        j = pl.program_id(0)
        x, w, dout_ = x_ref[...], w_ref[...], do_ref[...]
        pre = _conv_pre(x, w)
        act = _silu(pre)
        r = lax.rsqrt(jnp.sum(act * act, axis=-1, keepdims=True) + EPS)
        scale = jnp.where(j < DN_HEADS, DN_HEAD_DIM ** -0.5, 1.0)
        g = dout_ * scale
        dact_n = r * g - act * (r * r * r) * jnp.sum(g * act, axis=-1, keepdims=True)
        dact = jnp.where(j < nq, dact_n, dout_)
        dpre = dact * _dsilu(pre)
        dx = w[CONV_K - 1:CONV_K, :] * dpre
        for s in range(1, CONV_K):
            dx = dx + w[CONV_K - 1 - s:CONV_K - s, :] * _shift_up(dpre, s)
        dx_ref[...] = dx.astype(BF16)
        for s in range(CONV_K):
            dw_ref[pl.ds(CONV_K - 1 - s, 1), :] = jnp.sum(dpre * _shift_down(x, s), axis=0, keepdims=True)

    col = pl.BlockSpec((L, W), lambda j: (0, j))
    wsp = pl.BlockSpec((CONV_K, W), lambda j: (0, j))
    return pl.pallas_call(
        body, name="dn_conv_bwd", grid=(3 * DN_HEADS,),
        in_specs=[pl.BlockSpec((L, W), lambda j: (0, OFF_Q // W + j)), wsp, col], out_specs=[col, wsp],
        out_shape=[_sds((L, 3 * D_DN), BF16), _sds((CONV_K, 3 * D_DN))], compiler_params=_cp(("parallel",)),
    )(proj, conv_w, dout)


def _softplus(x):
    return jnp.maximum(x, 0.0) + jnp.log(1.0 + jnp.exp(-jnp.abs(x)))


def _dn_gates_fwd(proj, alog, dtb):
    L = proj.shape[0]
    W = 128

    def body(p_ref, al_ref, db_ref, o_ref):
        p = p_ref[...]
        lane = lax.broadcasted_iota(jnp.int32, p.shape, 1)
        g = -jnp.exp(al_ref[...]) * _softplus(p + db_ref[...])
        o_ref[...] = jnp.where(lane < DN_HEADS, _sigmoid(p), jnp.where(lane < 2 * DN_HEADS, g, 0.0))

    return pl.pallas_call(
        body, name="dn_gates_fwd", grid=(1,),
        in_specs=[pl.BlockSpec((L, W), lambda i: (0, OFF_B // W)), pl.BlockSpec((1, W), lambda i: (0, 0)),
                  pl.BlockSpec((1, W), lambda i: (0, 0))],
        out_specs=pl.BlockSpec((L, W), lambda i: (0, 0)), out_shape=_sds((L, W)), compiler_params=_cp(("arbitrary",)),
    )(proj, alog, dtb)


def _dn_gates_bwd(proj, alog, dtb, dgates):
    L = proj.shape[0]
    W = 128

    def body(p_ref, al_ref, db_ref, dg_ref, dp_ref, dal_ref, ddb_ref):
        p, dg = p_ref[...], dg_ref[...]
        lane = lax.broadcasted_iota(jnp.int32, p.shape, 1)
        is_g = jnp.logical_and(lane >= DN_HEADS, lane < 2 * DN_HEADS)
        beta = _sigmoid(p)
        na = -jnp.exp(al_ref[...])
        xs = p + db_ref[...]
        dsp = dg * na * _sigmoid(xs)
        dp_ref[...] = jnp.where(lane < DN_HEADS, dg * beta * (1.0 - beta), jnp.where(is_g, dsp, 0.0)).astype(BF16)
        dal_ref[...] = jnp.sum(jnp.where(is_g, dg * na * _softplus(xs), 0.0), axis=0, keepdims=True)
        ddb_ref[...] = jnp.sum(jnp.where(is_g, dsp, 0.0), axis=0, keepdims=True)

    one = pl.BlockSpec((1, W), lambda i: (0, 0))
    full = pl.BlockSpec((L, W), lambda i: (0, 0))
    return pl.pallas_call(
        body, name="dn_gates_bwd", grid=(1,),
        in_specs=[pl.BlockSpec((L, W), lambda i: (0, OFF_B // W)), one, one, full], out_specs=[full, one, one],
        out_shape=[_sds((L, W), BF16), _sds((1, W)), _sds((1, W))], compiler_params=_cp(("arbitrary",)),
    )(proj, alog, dtb, dgates)


def _bdot(a, b, dims):
    return lax.dot_general(a.astype(BF16), b.astype(BF16), (dims, ((), ())), preferred_element_type=F32)


_NN, _NT, _TN = ((1,), (0,)), ((1,), (1,)), ((0,), (0,))


def _dot3(a, b, dims):
    ah, bh = a.astype(BF16), b.astype(BF16)
    al, bl = (a - ah.astype(F32)).astype(BF16), (b - bh.astype(F32)).astype(BF16)
    (ca,), (cb,) = dims
    a3 = jnp.concatenate([ah, ah, al], axis=ca)
    b3 = jnp.concatenate([bh, bl, bh], axis=cb)
    return lax.dot_general(a3, b3, (dims, ((), ())), preferred_element_type=F32)


def _mm_family(raw):
    nn = jax.custom_vjp(lambda a, b: raw(a, b, _NN))
    nt = jax.custom_vjp(lambda a, b: raw(a, b, _NT))
    tn = jax.custom_vjp(lambda a, b: raw(a, b, _TN))
    nn.defvjp(lambda a, b: (raw(a, b, _NN), (a, b)), lambda r, g: (raw(g, r[1], _NT), raw(r[0], g, _TN)))
    nt.defvjp(lambda a, b: (raw(a, b, _NT), (a, b)), lambda r, g: (raw(g, r[1], _NN), raw(g, r[0], _TN)))
    tn.defvjp(lambda a, b: (raw(a, b, _TN), (a, b)), lambda r, g: (raw(r[1], g, _NT), raw(r[0], g, _NN)))
    return nn, nt, tn


_mm_nn, _mm_nt, _mm_tn = _mm_family(_bdot)
_m3_nn, _m3_nt, _m3_tn = _mm_family(_dot3)


def _tri_apply(x, upper):
    C = x.shape[0]
    ii = lax.broadcasted_iota(jnp.int32, (C, 3 * C), 0)
    jj = lax.broadcasted_iota(jnp.int32, (C, 3 * C), 1) % C
    mat = ((ii <= jj) if upper else (ii >= jj)).astype(BF16)
    hi = x.astype(BF16)
    r = x - hi.astype(F32)
    mid = r.astype(BF16)
    lo = (r - mid.astype(F32)).astype(BF16)
    return jnp.dot(mat, jnp.concatenate([hi, mid, lo], axis=0), preferred_element_type=F32)


_cumsum_rows = jax.custom_vjp(lambda x: _tri_apply(x, False))
_cumsum_rows.defvjp(lambda x: (_tri_apply(x, False), None), lambda _, g: (_tri_apply(g, True),))


def _uli(a_s):
    C = a_s[0].shape[0]
    ii = lax.broadcasted_iota(jnp.int32, (C, C), 0)
    jj = lax.broadcasted_iota(jnp.int32, (C, C), 1)
    eye = jnp.where(ii == jj, 1.0, 0.0)
    ts = [eye - a for a in a_s]
    ms = list(a_s)
    for _ in range(int(math.log2(C)) - 1):
        ms = [_dot3(m, m, _NN) for m in ms]
        ts = [t + _dot3(t, m, _NN) for t, m in zip(ts, ms)]
    return tuple(ts)


def _uli_bwd(ts, gs):
    xs = [_dot3(t, g, _TN) for t, g in zip(ts, gs)]
    return (tuple(-_dot3(x, t, _NT) for x, t in zip(xs, ts)),)


_unit_lower_inverse = jax.custom_vjp(_uli)
_unit_lower_inverse.defvjp(lambda a_s: (lambda ts: (ts, ts))(_uli(a_s)), _uli_bwd)


def _prep_math(qs, ks, vs, gcols, bcols):
    n = len(qs)
    C, dv = vs[0].shape
    ii = lax.broadcasted_iota(jnp.int32, (C, C), 0)
    jj = lax.broadcasted_iota(jnp.int32, (C, C), 1)
    causal = ii >= jj
    strict = ii > jj
    sf = strict.astype(F32)
    ones = jnp.ones((C, dv), F32)
    dms = [_cumsum_rows(g * sf) for g in gcols]
    gcbs = [_cumsum_rows(g * ones) for g in gcols]
    kks = [_mm_nt(k, k) for k in ks]
    qks = [_mm_nt(q, k) for q, k in zip(qs, ks)]
    decays = [jnp.where(causal, jnp.exp(jnp.where(causal, dm, 0.0)), 0.0) for dm in dms]
    glasts = [jnp.sum(g * ones, axis=0, keepdims=True) for g in gcols]
    egs = [jnp.exp(gcb) for gcb in gcbs]
    ts = _unit_lower_inverse(tuple(jnp.where(strict, b * kk * dc, 0.0) for b, kk, dc in zip(bcols, kks, decays)))
    us = [_m3_nn(t, v * b) for t, v, b in zip(ts, vs, bcols)]
    ws = [_m3_nn(t, k * b * eg) for t, k, b, eg in zip(ts, ks, bcols, egs)]
    return tuple((us[i], ws[i], qs[i] * egs[i], ks[i] * jnp.exp(glasts[i] - gcbs[i]), qks[i] * decays[i],
                  jnp.exp(glasts[i])) for i in range(n))


def _gate_cols(gates, h):
    lane = lax.broadcasted_iota(jnp.int32, gates.shape, 1)
    bcol = jnp.sum(jnp.where(lane == h, gates, 0.0), axis=1, keepdims=True)
    gcol = jnp.sum(jnp.where(lane == h + DN_HEADS, gates, 0.0), axis=1, keepdims=True)
    return gcol, bcol


DN_HB = 8


def _dn_prep_fwd(qkv, gates):
    L = qkv.shape[0]
    N, H, d, HB = L // CHUNK, DN_HEADS, DN_HEAD_DIM, DN_HB

    def body(q_ref, k_ref, v_ref, g_ref, u_ref, w_ref, qd_ref, kd_ref, qk_ref, egl_ref):
        h0 = pl.program_id(1) * HB
        gates_ = g_ref[...]
        lanes_of = [pl.ds(i * d, d) for i in range(HB)]
        cols = [_gate_cols(gates_, h0 + i) for i in range(HB)]
        outs = _prep_math([q_ref[:, l] for l in lanes_of], [k_ref[:, l] for l in lanes_of], [v_ref[:, l] for l in lanes_of],
                          [c[0] for c in cols], [c[1] for c in cols])
        for i in range(HB):
            lanes = lanes_of[i]
            u, w, qd, kd, qk, egl = outs[i]
            u_ref[:, lanes] = u
            w_ref[:, lanes] = w
            qd_ref[:, lanes] = qd
            kd_ref[:, lanes] = kd
            qk_ref[0, i] = qk
            egl_ref[0, i] = jnp.broadcast_to(egl, (8, d))

    blk = lambda off: pl.BlockSpec((CHUNK, HB * d), lambda n, j: (n, off // HB + j))
    cc = pl.BlockSpec((1, HB, CHUNK, CHUNK), lambda n, j: (n, j, 0, 0))
    ee = pl.BlockSpec((1, HB, 8, d), lambda n, j: (n, j, 0, 0))
    big = _sds((L, D_DN))
    return pl.pallas_call(
        body, name="dn_prep_fwd", grid=(N, H // HB),
        in_specs=[blk(0), blk(H), blk(2 * H), pl.BlockSpec((CHUNK, 128), lambda n, j: (n, 0))],
        out_specs=[blk(0), blk(0), blk(0), blk(0), cc, ee],
        out_shape=[big, big, big, big, _sds((N, H, CHUNK, CHUNK)), _sds((N, H, 8, d))],
        compiler_params=_cp(("parallel", "parallel")),
    )(qkv, qkv, qkv, gates)


def _dn_scan_fwd(u, w, qd, kd, qk, egl):
    L = u.shape[0]
    N, H, d, HB = L // CHUNK, DN_HEADS, DN_HEAD_DIM, DN_HB

    def body(u_ref, w_ref, qd_ref, kd_ref, qk_ref, egl_ref, o_ref, st_ref, s_ref):
        n, h0 = pl.program_id(0), pl.program_id(1) * HB

        @pl.when(n == 0)
        def _():
            for i in range(HB):
                s_ref[h0 + i] = jnp.zeros((d, d), F32)

        hs = range(HB)
        ln = [pl.ds(i * d, d) for i in hs]
        st = [s_ref[h0 + i] for i in hs]
        ws = [_bdot(w_ref[:, ln[i]], st[i], _NN) for i in hs]
        qs = [_bdot(qd_ref[:, ln[i]], st[i], _NN) for i in hs]
        vn = [u_ref[:, ln[i]] - ws[i] for i in hs]
        qv = [_bdot(qk_ref[0, i], vn[i], _NN) for i in hs]
        kv = [_bdot(kd_ref[:, ln[i]], vn[i], _TN) for i in hs]
        for i in hs:
            st_ref[0, i] = st[i]
            o_ref[:, ln[i]] = qs[i] + qv[i]
            s_ref[h0 + i] = st[i] * egl_ref[0, i, pl.ds(0, 1), :] + kv[i]

    blk = pl.BlockSpec((CHUNK, HB * d), lambda n, j: (n, j))
    cc = pl.BlockSpec((1, HB, CHUNK, CHUNK), lambda n, j: (n, j, 0, 0))
    ee = pl.BlockSpec((1, HB, 8, d), lambda n, j: (n, j, 0, 0))
    return pl.pallas_call(
        body, name="dn_scan_fwd", grid=(N, H // HB), in_specs=[blk, blk, blk, blk, cc, ee],
        out_specs=[blk, pl.BlockSpec((1, HB, d, d), lambda n, j: (n, j, 0, 0))],
        out_shape=[_sds((L, D_DN)), _sds((N, H, d, d))], scratch_shapes=[pltpu.VMEM((H, d, d), F32)],
        compiler_params=_cp(("arbitrary", "arbitrary")),
    )(u, w, qd, kd, qk, egl)


def _dn_scan_bwd(u, w, qd, kd, qk, egl, states, do):
    L = u.shape[0]
    N, H, d, HB = L // CHUNK, DN_HEADS, DN_HEAD_DIM, DN_HB

    def body(u_ref, w_ref, qd_ref, kd_ref, qk_ref, egl_ref, st_ref, do_ref,
             du_ref, dw_ref, dqd_ref, dkd_ref, dqk_ref, degl_ref, ds_ref):
        n, h0 = pl.program_id(0), pl.program_id(1) * HB

        @pl.when(n == 0)
        def _():
            for i in range(HB):
                ds_ref[h0 + i] = jnp.zeros((d, d), F32)

        hs = range(HB)
        ln = [pl.ds(i * d, d) for i in hs]
        st = [st_ref[0, i] for i in hs]
        dsn = [ds_ref[h0 + i] for i in hs]
        do_ = [do_ref[:, ln[i]] for i in hs]
        ws = [_bdot(w_ref[:, ln[i]], st[i], _NN) for i in hs]
        d1 = [_bdot(qk_ref[0, i], do_[i], _TN) for i in hs]
        d2 = [_bdot(kd_ref[:, ln[i]], dsn[i], _NN) for i in hs]
        dqd = [_bdot(do_[i], st[i], _NT) for i in hs]
        qdo = [_bdot(qd_ref[:, ln[i]], do_[i], _TN) for i in hs]
        vn = [u_ref[:, ln[i]] - ws[i] for i in hs]
        dvn = [d1[i] + d2[i] for i in hs]
        dw = [_bdot(dvn[i], st[i], _NT) for i in hs]
        dkd = [_bdot(vn[i], dsn[i], _NT) for i in hs]
        dqk = [_bdot(do_[i], vn[i], _NT) for i in hs]
        wdv = [_bdot(w_ref[:, ln[i]], dvn[i], _TN) for i in hs]
        for i in hs:
            du_ref[:, ln[i]] = dvn[i]
            dw_ref[:, ln[i]] = -dw[i]
            dqd_ref[:, ln[i]] = dqd[i]
            dkd_ref[:, ln[i]] = dkd[i]
            dqk_ref[0, i] = dqk[i]
            degl_ref[0, i] = jnp.broadcast_to(jnp.sum(dsn[i] * st[i], keepdims=True), (8, d))
            ds_ref[h0 + i] = (qdo[i] - wdv[i]) + dsn[i] * egl_ref[0, i, pl.ds(0, 1), :]

    blk = pl.BlockSpec((CHUNK, HB * d), lambda n, j: (N - 1 - n, j))
    cc = pl.BlockSpec((1, HB, CHUNK, CHUNK), lambda n, j: (N - 1 - n, j, 0, 0))
    ee = pl.BlockSpec((1, HB, 8, d), lambda n, j: (N - 1 - n, j, 0, 0))
    ss = pl.BlockSpec((1, HB, d, d), lambda n, j: (N - 1 - n, j, 0, 0))
    big = _sds((L, D_DN))
    return pl.pallas_call(
        body, name="dn_scan_bwd", grid=(N, H // HB), in_specs=[blk, blk, blk, blk, cc, ee, ss, blk],
        out_specs=[blk, blk, blk, blk, cc, ee],
        out_shape=[big, big, big, big, _sds((N, H, CHUNK, CHUNK)), _sds((N, H, 8, d))],
        scratch_shapes=[pltpu.VMEM((H, d, d), F32)], compiler_params=_cp(("arbitrary", "arbitrary")),
    )(u, w, qd, kd, qk, egl, states, do)


def _dn_prep_bwd(qkv, gates, du, dw, dqd, dkd, dqk, degl):
    L = qkv.shape[0]
    N, H, d, HB = L // CHUNK, DN_HEADS, DN_HEAD_DIM, DN_HB

    def body(q_ref, k_ref, v_ref, g_ref, du_ref, dw_ref, dqd_ref, dkd_ref, dqk_ref, degl_ref, dq_ref, dk_ref, dv_ref, dg_ref):
        j = pl.program_id(1)
        h0 = j * HB
        gates_ = g_ref[...]
        lane = lax.broadcasted_iota(jnp.int32, gates_.shape, 1)
        lane1 = lax.broadcasted_iota(jnp.int32, (1, d), 1)
        part = jnp.zeros(gates_.shape, F32)
        lanes_of = [pl.ds(i * d, d) for i in range(HB)]
        cols = [_gate_cols(gates_, h0 + i) for i in range(HB)]
        _, f = jax.vjp(_prep_math, [q_ref[:, l] for l in lanes_of], [k_ref[:, l] for l in lanes_of],
                       [v_ref[:, l] for l in lanes_of], [c[0] for c in cols], [c[1] for c in cols])
        cots = tuple((du_ref[:, l], dw_ref[:, l], dqd_ref[:, l], dkd_ref[:, l], dqk_ref[0, i],
                      jnp.where(lane1 == 0, degl_ref[0, i, pl.ds(0, 1), :], 0.0)) for i, l in enumerate(lanes_of))
        dqs, dks, dvs, dgcs, dbcs = f(cots)
        for i in range(HB):
            lanes = lanes_of[i]
            dq_ref[:, lanes] = dqs[i]
            dk_ref[:, lanes] = dks[i]
            dv_ref[:, lanes] = dvs[i]
            part = part + jnp.where(lane == h0 + i, dbcs[i], 0.0) + jnp.where(lane == h0 + i + DN_HEADS, dgcs[i], 0.0)

        @pl.when(j == 0)
        def _():
            dg_ref[...] = part

        @pl.when(j > 0)
        def _():
            dg_ref[...] += part

    blk = lambda off: pl.BlockSpec((CHUNK, HB * d), lambda n, j: (n, off // HB + j))
    gsp = pl.BlockSpec((CHUNK, 128), lambda n, j: (n, 0))
    cc = pl.BlockSpec((1, HB, CHUNK, CHUNK), lambda n, j: (n, j, 0, 0))
    ee = pl.BlockSpec((1, HB, 8, d), lambda n, j: (n, j, 0, 0))
    big = _sds((L, D_DN))
    return pl.pallas_call(
        body, name="dn_prep_bwd", grid=(N, H // HB),
        in_specs=[blk(0), blk(H), blk(2 * H), gsp, blk(0), blk(0), blk(0), blk(0), cc, ee],
        out_specs=[blk(0), blk(0), blk(0), gsp], out_shape=[big, big, big, _sds((L, 128))],
        compiler_params=_cp(("parallel", "arbitrary")),
    )(qkv, qkv, qkv, gates, du, dw, dqd, dkd, dqk, degl)


def _dn_post_fwd(o, proj, nw):
    L = o.shape[0]
    d = DN_HEAD_DIM
    tm = min(512, L)

    def body(o_ref, z_ref, w_ref, y_ref):
        ov = o_ref[...]
        r = lax.rsqrt(jnp.mean(ov * ov, axis=-1, keepdims=True) + EPS)
        y_ref[...] = (ov * r * w_ref[...] * _silu(z_ref[...])).astype(BF16)

    blk = pl.BlockSpec((tm, d), lambda i, h: (i, h))
    return pl.pallas_call(
        body, name="dn_post_fwd", grid=(L // tm, DN_HEADS),
        in_specs=[blk, pl.BlockSpec((tm, d), lambda i, h: (i, OFF_ZD // d + h)), pl.BlockSpec((1, d), lambda i, h: (0, 0))],
        out_specs=blk, out_shape=_sds((L, D_DN), BF16), compiler_params=_cp(("parallel", "parallel")),
    )(o, proj, nw)


def _dn_post_bwd(o, proj, nw, dy):
    L = o.shape[0]
    d = DN_HEAD_DIM
    tm = min(512, L)

    def body(o_ref, z_ref, w_ref, dy_ref, do_ref, dz_ref, dw_ref):
        first = jnp.logical_and(pl.program_id(0) == 0, pl.program_id(1) == 0)
        ov, z, w, dyv = o_ref[...], z_ref[...], w_ref[...], dy_ref[...]
        r = lax.rsqrt(jnp.mean(ov * ov, axis=-1, keepdims=True) + EPS)
        xn = ov * r
        dz_ref[...] = (dyv * xn * w * _dsilu(z)).astype(BF16)
        dn = dyv * _silu(z)
        t = dn * w
        do_ref[...] = r * t - ov * (r * r * r) * jnp.mean(t * ov, axis=-1, keepdims=True)
        part = jnp.sum(dn * xn, axis=0, keepdims=True)

        @pl.when(first)
        def _():
            dw_ref[...] = part

        @pl.when(jnp.logical_not(first))
        def _():
            dw_ref[...] += part

    blk = pl.BlockSpec((tm, d), lambda i, h: (i, h))
    one = pl.BlockSpec((1, d), lambda i, h: (0, 0))
    return pl.pallas_call(
        body, name="dn_post_bwd", grid=(L // tm, DN_HEADS),
        in_specs=[blk, pl.BlockSpec((tm, d), lambda i, h: (i, OFF_ZD // d + h)), one, blk], out_specs=[blk, blk, one],
        out_shape=[_sds((L, D_DN)), _sds((L, D_DN), BF16), _sds((1, d))], compiler_params=_cp(("arbitrary", "arbitrary")),
    )(o, proj, nw, dy)


def _mix_fwd(s5o, dno, w_su, w_du, proj):
    L, K = s5o.shape
    N = w_su.shape[1]
    tm, tn = min(512, L), 512

    def body(a1, a2, b1, b2, gs, gd, ys_ref, yd_ref, mx_ref):
        ys = jnp.dot(a1[...], b1[...], preferred_element_type=F32)
        yd = jnp.dot(a2[...], b2[...], preferred_element_type=F32)
        ys_ref[...] = ys
        yd_ref[...] = yd
        mx_ref[...] = (_sigmoid(gs[...]) * ys + _sigmoid(gd[...]) * yd).astype(BF16)

    a = pl.BlockSpec((tm, K), lambda i, j: (i, 0))
    b = pl.BlockSpec((K, tn), lambda i, j: (0, j))
    o = pl.BlockSpec((tm, tn), lambda i, j: (i, j))
    return pl.pallas_call(
        body, name="mix_fwd", grid=(L // tm, N // tn),
        in_specs=[a, a, b, b, pl.BlockSpec((tm, tn), lambda i, j: (i, OFF_GS // tn + j)),
                  pl.BlockSpec((tm, tn), lambda i, j: (i, OFF_GD // tn + j))],
        out_specs=[o, o, o], out_shape=[_sds((L, N)), _sds((L, N)), _sds((L, N), BF16)],
        compiler_params=_cp(("parallel", "parallel")),
    )(s5o, dno, w_su, w_du, proj, proj)


def _mix_bwd(dx2b, w_out, proj, ys, yd):
    L, K = dx2b.shape
    N = w_out.shape[0]
    tm, tn = min(512, L), 512

    def body(a, b, gs, gd, ys_ref, yd_ref, dgs_ref, dgd_ref, dys_ref, dyd_ref):
        dm = lax.dot_general(a[...], b[...], (((1,), (1,)), ((), ())), preferred_element_type=F32)
        ss, sd = _sigmoid(gs[...]), _sigmoid(gd[...])
        dys_ref[...] = (dm * ss).astype(BF16)
        dyd_ref[...] = (dm * sd).astype(BF16)
        dgs_ref[...] = (dm * ys_ref[...] * ss * (1.0 - ss)).astype(BF16)
        dgd_ref[...] = (dm * yd_ref[...] * sd * (1.0 - sd)).astype(BF16)

    o = pl.BlockSpec((tm, tn), lambda i, j: (i, j))
    return pl.pallas_call(
        body, name="mix_bwd", grid=(L // tm, N // tn),
        in_specs=[pl.BlockSpec((tm, K), lambda i, j: (i, 0)), pl.BlockSpec((tn, K), lambda i, j: (j, 0)),
                  pl.BlockSpec((tm, tn), lambda i, j: (i, OFF_GS // tn + j)),
                  pl.BlockSpec((tm, tn), lambda i, j: (i, OFF_GD // tn + j)), o, o],
        out_specs=[o, o, o, o], out_shape=[_sds((L, N), BF16)] * 4, compiler_params=_cp(("parallel", "parallel")),
    )(dx2b, w_out, proj, proj, ys, yd)


def _final(mixed, w_out, x, tgt, fw):
    L, D = x.shape
    tm = min(256, L)

    def body(a_ref, b_ref, x_ref, t_ref, w_ref, dx_ref, dxb_ref, loss_ref, dw_ref):
        i = pl.program_id(0)
        x2 = x_ref[...] + jnp.dot(a_ref[...], b_ref[...], preferred_element_type=F32)
        w = w_ref[...]
        r = lax.rsqrt(jnp.mean(x2 * x2, axis=-1, keepdims=True) + EPS)
        xn = x2 * r
        e = xn * w - t_ref[...]
        lpart = 0.5 * jnp.sum(jnp.mean(e * e, axis=-1, keepdims=True), axis=0, keepdims=True)
        dy = e * (1.0 / D)
        t = dy * w
        dx2 = r * t - x2 * (r * r * r) * jnp.mean(t * x2, axis=-1, keepdims=True)
        dx_ref[...] = dx2
        dxb_ref[...] = dx2.astype(BF16)
        dwp = jnp.sum(dy * xn, axis=0, keepdims=True)
        lrow = jnp.broadcast_to(lpart, loss_ref.shape)

        @pl.when(i == 0)
        def _():
            loss_ref[...] = lrow
            dw_ref[...] = dwp

        @pl.when(i > 0)
        def _():
            loss_ref[...] += lrow
            dw_ref[...] += dwp

    row = pl.BlockSpec((tm, D), lambda i: (i, 0))
    one = pl.BlockSpec((1, D), lambda i: (0, 0))
    return pl.pallas_call(
        body, name="final", grid=(L // tm,),
        in_specs=[row, pl.BlockSpec((D, D), lambda i: (0, 0)), row, row, one],
        out_specs=[row, row, pl.BlockSpec((1, 128), lambda i: (0, 0)), one],
        out_shape=[_sds((L, D)), _sds((L, D), BF16), _sds((1, 128)), _sds((1, D))], compiler_params=_cp(("arbitrary",)),
    )(mixed, w_out, x, tgt, fw)


def _block_diag(t):
    J, g, a, b = t.shape
    eye = jnp.eye(g, dtype=t.dtype)
    return (t[:, :, :, None, :] * eye[None, :, None, :, None]).reshape(J, g * a, g * b)


def _block_diag_take(m, g):
    J, ga, gb = m.shape
    a, b = ga // g, gb // g
    m5 = m.reshape(J, g, a, g, b)
    idx = jnp.arange(g)
    return m5[:, idx, :, idx, :].transpose(1, 0, 2, 3)


class _PlainOps:
    def __init__(self, w_rest):
        self.w_rest = w_rest

    def in_proj(self, h, wt_perm):
        return _mm(h, wt_perm, tb=True, name="in_proj", tm=1024, tn=1152), self.w_rest

    def rest_grads(self, d_w_glu, d_w_su, d_w_du, d_w_out):
        pass

    def d_w_in(self, h, dproj):
        return _mm(dproj, h, ta=True, name="d_w_in", tm=1152, tn=1024)

    def d_h(self, dproj, wt_perm, d_wt_perm):
        return _mm(dproj, wt_perm, name="d_h", tm=2048, tn=1024, tk=1152)


def _local_step(x, tgt, ln_w, w_perm, lam_re, lam_im, log_step, b_re, b_im, c_re, c_im, s5_d,
                conv_w, a_log, dt_bias, norm_w, fw, ops):
    G, P, gb = S5_GROUPS, S5_STATE, S5_GROUPS // S5_BLOCKS
    h, rstd = _ln_fwd(x, ln_w)
    proj, (w_glu, w_su, w_du, w_out) = ops.in_proj(h, w_perm)

    b_re2, b_im2 = b_re.reshape(G, P * S5_GROUP), b_im.reshape(G, P * S5_GROUP)
    ls2 = log_step.reshape(G, 1)
    abar_re, abar_im, bb_re, bb_im = _s5_param_fwd(lam_re, lam_im, ls2, b_re2, b_im2)

    def to_wb(bb):
        return _block_diag(bb.reshape(S5_BLOCKS, gb, P, S5_GROUP).transpose(0, 1, 3, 2)).astype(BF16)

    def to_cb(cc):
        return _block_diag(cc.reshape(S5_BLOCKS, gb, S5_GROUP, P).transpose(0, 1, 3, 2)).astype(BF16)

    wbr, wbi, cbr, cbi = to_wb(bb_re), to_wb(bb_im), to_cb(c_re), to_cb(c_im)
    a_re_row, a_im_row = abar_re.reshape(1, G * P), abar_im.reshape(1, G * P)
    yc = _s5_core_fwd(proj, wbr, wbi, a_re_row, a_im_row, cbr, cbi)
    s5o = _s5_post_fwd(yc, proj, s5_d, w_glu)

    pad = lambda v: jnp.pad(v, ((0, 0), (DN_HEADS, 128 - 2 * DN_HEADS)))
    alog_row, dtb_row = pad(a_log), pad(dt_bias)
    qkv = _dn_conv_fwd(proj, conv_w)
    gates = _dn_gates_fwd(proj, alog_row, dtb_row)
    prep = _dn_prep_fwd(qkv, gates)
    o_dn, states = _dn_scan_fwd(*prep)
    dno = _dn_post_fwd(o_dn, proj, norm_w)

    ys, yd, mixed = _mix_fwd(s5o, dno, w_su, w_du, proj)
    dx2, dx2b, loss_row, d_fw = _final(mixed, w_out, x, tgt, fw)
    d_w_out = _mm(mixed, dx2b, ta=True, name="d_w_out")
    dgs, dgd, dys, dyd = _mix_bwd(dx2b, w_out, proj, ys, yd)
    d_w_su = _mm(s5o, dys, ta=True, name="d_w_su", shard_out=True)
    d_w_du = _mm(dno, dyd, ta=True, name="d_w_du", shard_out=True)
    ds5o = _mm(dys, w_su, tb=True, name="d_s5o")
    ddno = _mm(dyd, w_du, tb=True, name="d_dno")

    dyc, du1, dz_s, d_s5d, d_w_glu = _s5_post_bwd(yc, proj, s5_d, w_glu, ds5o)
    ops.rest_grads(d_w_glu, d_w_su, d_w_du, d_w_out)
    du, dwbr, dwbi, dcbr, dcbi, dar, dai = _s5_core_bwd(proj, wbr, wbi, a_re_row, a_im_row, cbr, cbi, dyc, du1)

    def from_wb(dwb):
        return _block_diag_take(dwb, gb).transpose(0, 1, 3, 2).reshape(G, P * S5_GROUP)

    def from_cb(dcb):
        return _block_diag_take(dcb, gb).transpose(0, 1, 3, 2).reshape(G, S5_GROUP, P)

    d_lam_re, d_lam_im, d_ls, d_b_re, d_b_im = _s5_param_bwd(
        lam_re, lam_im, ls2, b_re2, b_im2, dar.reshape(G, P), dai.reshape(G, P), from_wb(dwbr), from_wb(dwbi))

    do_dn, dz_d, d_norm_w = _dn_post_bwd(o_dn, proj, norm_w, ddno)
    dq, dk, dv, dgates = _dn_prep_bwd(qkv, gates, *_dn_scan_bwd(*prep, states, do_dn))
    dqkv, d_conv = _dn_conv_bwd(proj, conv_w, jnp.concatenate([dq, dk, dv], axis=1))
    dpb, d_alog_row, d_dtb_row = _dn_gates_bwd(proj, alog_row, dtb_row, dgates)

    dproj = jnp.concatenate([du, dz_s, dqkv, dz_d, dgs, dgd, dpb], axis=1)
    d_w_perm = ops.d_w_in(h, dproj)
    dh = ops.d_h(dproj, w_perm, d_w_perm)
    grad_x, d_ln_w = _ln_bwd(x, rstd, ln_w, dh, dx2)

    grads = dict(
        ln_w=d_ln_w, w_perm=d_w_perm, s5_lam_re=d_lam_re, s5_lam_im=d_lam_im, s5_log_step=d_ls.reshape(1, G),
        s5_b_re=d_b_re.reshape(G, P, S5_GROUP), s5_b_im=d_b_im.reshape(G, P, S5_GROUP),
        s5_c_re=from_cb(dcbr), s5_c_im=from_cb(dcbi), s5_d=d_s5d, s5_w_glu=d_w_glu, s5_w_up=d_w_su,
        dn_conv_w=d_conv, dn_a_log=d_alog_row[:, DN_HEADS:2 * DN_HEADS], dn_dt_bias=d_dtb_row[:, DN_HEADS:2 * DN_HEADS],
        dn_norm_w=d_norm_w, dn_w_up=d_w_du, w_out=d_w_out, final_norm_w=d_fw)
    return loss_row, grad_x, grads


def _place():
    x, y, c = lax.axis_index("x"), lax.axis_index("y"), lax.axis_index("c")
    return x, y, c


def _remote(src, dst, send_sem, recv_sem, to):
    return pltpu.make_async_remote_copy(src_ref=src, dst_ref=dst, send_sem=send_sem, recv_sem=recv_sem,
                                        device_id=to, device_id_type=MESH)


def _gather_exchange(shards, whole=(), by_columns=False):
    na, nw = len(shards), len(whole)

    def half_of(ref, a, half):
        rows, cols = shards[a].shape
        if by_columns:
            return ref.at[pl.ds(0, rows), pl.ds(half * (cols // 2), cols // 2)]
        return ref.at[pl.ds(half * (rows // 2), rows // 2)]

    def plan(ins, outs, send_sems, recv_sems, local_sems, receiving):
        x, y, c = _place()
        me = 2 * x + y
        sibling = (x, y, 1 - c)
        chips = [(1 - x, y), (x, 1 - y), (1 - x, 1 - y)]

        def part(a, chip, half):
            return half_of(outs[a].at[chip], a, half)

        own = [pltpu.make_async_copy(ins[a], outs[a].at[me], local_sems.at[a]) for a in range(na + nw)]
        sends, landed, passed, arrivals = [], [], [], []
        for a in range(na):
            for j, (px, py) in enumerate(chips):
                k = 6 * a + j
                sends.append(_remote(half_of(ins[a], a, c), part(a, me, c), send_sems.at[k], recv_sems.at[k], (px, py, c)))
                if receiving:
                    got, other = part(a, 2 * px + py, c), part(a, 2 * px + py, 1 - c)
                    landed.append(_remote(got, got, send_sems.at[k], recv_sems.at[k], (px, py, c)))
                    passed.append(_remote(got, got, send_sems.at[k + 3], recv_sems.at[k + 3], sibling))
                    arrivals.append(_remote(other, other, send_sems.at[k + 3], recv_sems.at[k + 3], sibling))
        for a in range(na, na + nw):
            for j, (px, py) in enumerate(chips):
                k = 6 * na + 3 * (a - na) + j
                sends.append(_remote(ins[a], outs[a].at[me], send_sems.at[k], recv_sems.at[k], (px, py, c)))
                if receiving:
                    arrivals.append(_remote(ins[a], outs[a].at[2 * px + py], send_sems.at[k], recv_sems.at[k], (px, py, c)))
        return own, sends, landed, passed, arrivals

    def start(ins, outs, *sems):
        own, sends, _, _, _ = plan(ins, outs, *sems, False)
        for cp in own + sends:
            cp.start()

    def finish(ins, outs, *sems):
        own, sends, landed, passed, arrivals = plan(ins, outs, *sems, True)
        for got, fwd in zip(landed, passed):
            got.wait_recv()
            fwd.start()
        for cp in arrivals:
            cp.wait_recv()
        for cp in sends + passed:
            cp.wait_send()
        for cp in own:
            cp.wait()

    arrays = list(shards) + list(whole)
    return _Exchange(arrays, [_sds((N_CHIPS,) + s.shape, s.dtype) for s in arrays], 6 * na + 3 * nw, start, finish)


def _owners_exchange(csbs):
    na = len(csbs)

    def plan(ins, outs, send_sems, recv_sems, local_sems, receiving):
        x, y, c = _place()
        me = 2 * x + y
        sends, arrivals = [], []
        for a in range(na):
            for k in range(N_CHIPS - 1):
                j = (me + 1 + k) % N_CHIPS
                sends.append(_remote(ins[a].at[k], outs[a].at[2 - k], send_sems.at[3 * a + k], recv_sems.at[3 * a + 2 - k],
                                     (j // 2, j % 2, c)))
                if receiving:
                    arrivals.append(_remote(ins[a].at[k], outs[a].at[k], send_sems.at[3 * a + k], recv_sems.at[3 * a + k], (x, y, c)))
        return sends, arrivals

    def start(ins, outs, *sems):
        for cp in plan(ins, outs, *sems, False)[0]:
            cp.start()

    def finish(ins, outs, *sems):
        sends, arrivals = plan(ins, outs, *sems, True)
        for cp in arrivals:
            cp.wait_recv()
        for cp in sends:
            cp.wait_send()

    return _Exchange(csbs, [_sds(g.shape, g.dtype) for g in csbs], 3 * na, start, finish)


def _run_exchange(ex, name):
    n_in, n_out = len(ex.ins), len(ex.out_shapes)

    def body(*refs):
        ins, outs, sems = refs[:n_in], refs[n_in:n_in + n_out], refs[n_in + n_out:]
        ex.start(ins, outs, *sems)
        ex.finish(ins, outs, *sems)

    return pl.pallas_call(
        body, name=name, in_specs=[ANY] * n_in, out_specs=[ANY] * n_out, out_shape=ex.out_shapes,
        scratch_shapes=[pltpu.SemaphoreType.DMA((ex.n_sems,)) for _ in range(3)],
    )(*ex.ins)


def _swap_halves(gxs, name):
    na = len(gxs)

    def body(*refs):
        ins, outs = refs[:na], refs[na:2 * na]
        send_sems, recv_sems = refs[2 * na:]
        x, y, c = _place()
        cps = [_remote(ins[a].at[pl.ds(0, N_CHIPS), pl.ds(1 - c, 1)], outs[a], send_sems.at[a], recv_sems.at[a], (x, y, 1 - c))
               for a in range(na)]
        for cp in cps:
            cp.start()
        for cp in cps:
            cp.wait()

    return pl.pallas_call(
        body, name=name, in_specs=[ANY] * na, out_specs=[ANY] * na,
        out_shape=[_sds((N_CHIPS, 1) + g.shape[2:], g.dtype) for g in gxs],
        scratch_shapes=[pltpu.SemaphoreType.DMA((na,)), pltpu.SemaphoreType.DMA((na,))],
    )(*gxs)


def _share_halves(gfs):
    na = len(gfs)

    def body(*refs):
        ins, outs = refs[:na], refs[na:2 * na]
        send_sems, recv_sems = refs[2 * na:]
        x, y, c = _place()
        cps = [_remote(ins[a].at[pl.ds(c, 1)], outs[a].at[pl.ds(c, 1)], send_sems.at[a], recv_sems.at[a], (x, y, 1 - c))
               for a in range(na)]
        for cp in cps:
            cp.start()
        for a in range(na):
            cps[a].wait_send()
            _remote(ins[a].at[pl.ds(1 - c, 1)], outs[a].at[pl.ds(1 - c, 1)], send_sems.at[a], recv_sems.at[a], (x, y, 1 - c)).wait_recv()

    return pl.pallas_call(
        body, name="rs_share_halves", in_specs=[ANY] * na, out_specs=[ANY] * na,
        out_shape=[_sds(g.shape, g.dtype) for g in gfs], input_output_aliases={a: a for a in range(na)},
        scratch_shapes=[pltpu.SemaphoreType.DMA((na,)), pltpu.SemaphoreType.DMA((na,))],
    )(*gfs)


def _row_tile(rows, cols, budget=5 << 18):
    fits = [t for t in range(16, rows + 1, 16) if rows % t == 0 and t * cols * 4 <= budget]
    return max(fits) if fits else rows


def _chip_sums(gx, r1, where):
    _, _, r2, cd = gx.shape
    tr = _row_tile(r2, cd)

    def body(w_ref, a_ref, b_ref, o_ref):
        o_ref[...] = (a_ref[0] + b_ref[0]).astype(BF16)

    other = lambda k, i, w: ((w[1] + 1 + k) % N_CHIPS, w[0], i, 0)
    other0 = lambda k, i, w: ((w[1] + 1 + k) % N_CHIPS, 0, i, 0)
    return pl.pallas_call(
        body, name="rs_chip_sums",
        grid_spec=pltpu.PrefetchScalarGridSpec(
            num_scalar_prefetch=1, grid=(N_CHIPS - 1, r2 // tr),
            in_specs=[pl.BlockSpec((1, 1, tr, cd), other), pl.BlockSpec((1, 1, tr, cd), other0)],
            out_specs=pl.BlockSpec((1, tr, cd), lambda k, i, w: (k, i, 0))),
        out_shape=_sds((N_CHIPS - 1, r2, cd), BF16), compiler_params=_cp(("parallel", "parallel")),
    )(where, gx, r1)


def _owner_sum(gx, r1, r2x, where):
    _, _, r2, cd = gx.shape
    tr = _row_tile(r2, cd)

    def body(w_ref, a_ref, b_ref, r_ref, o_ref):
        acc = a_ref[0, 0] + b_ref[0, 0]
        for k in range(N_CHIPS - 1):
            acc = acc + r_ref[k].astype(F32)
        o_ref[0] = acc

    return pl.pallas_call(
        body, name="rs_owner_sum",
        grid_spec=pltpu.PrefetchScalarGridSpec(
            num_scalar_prefetch=1, grid=(r2 // tr,),
            in_specs=[pl.BlockSpec((1, 1, tr, cd), lambda i, w: (w[1], w[0], i, 0)),
                      pl.BlockSpec((1, 1, tr, cd), lambda i, w: (w[1], 0, i, 0)),
                      pl.BlockSpec((N_CHIPS - 1, tr, cd), lambda i, w: (0, i, 0))],
            out_specs=pl.BlockSpec((1, tr, cd), lambda i, w: (w[0], i, 0))),
        out_shape=_sds((2, r2, cd)), compiler_params=_cp(("parallel",)),
    )(where, gx, r1, r2x)


def _adamw_math(w, g, m, v):
    m = ADAM_B1 * m + (1.0 - ADAM_B1) * g
    v = ADAM_B2 * v + (1.0 - ADAM_B2) * (g * g)
    m_hat = m / (1.0 - ADAM_B1 ** ADAM_STEP)
    v_hat = v / (1.0 - ADAM_B2 ** ADAM_STEP)
    delta = -ADAM_LR * (m_hat / (jnp.sqrt(v_hat) + ADAM_EPS) + ADAM_WD * w)
    return delta, m, v


def _adamw(w, g, m, v, name):
    rows, cd = w.shape
    if rows % 16 == 0:
        tr, tc = _row_tile(rows, cd, budget=3 << 19), cd
    else:
        tr, tc = rows, (128 if rows * cd * 4 > (3 << 19) else cd)
    assert rows % tr == 0 and cd % tc == 0

    def body(w_ref, g_ref, m_ref, v_ref, d_ref, mo_ref, vo_ref):
        d, mm, vv = _adamw_math(w_ref[...], g_ref[...], m_ref[...], v_ref[...])
        d_ref[...] = d
        mo_ref[...] = mm
        vo_ref[...] = vv

    blk = pl.BlockSpec((tr, tc), lambda i, j: (i, j))
    return pl.pallas_call(
        body, name=name, grid=(rows // tr, cd // tc), in_specs=[blk] * 4, out_specs=[blk] * 3, out_shape=[_sds(w.shape)] * 3,
        compiler_params=_cp(("parallel", "parallel")),
    )(w, g, m, v)


def _small_allreduce(gp):
    R = gp.shape[0]
    R2 = R // 2
    assert R2 % 8 == 0

    def body(g_ref, go_ref, sib, csum, land, send_sems, recv_sems):
        x, y, c = _place()
        me = 2 * x + y
        sibling = (x, y, 1 - c)
        chips = [(1 - x, y), (x, 1 - y), (1 - x, 1 - y)]
        swap = _remote(g_ref, sib, send_sems.at[0], recv_sems.at[0], sibling)
        swap.start()
        swap.wait()
        csum[...] = g_ref[...] + sib[...]
        half = csum.at[pl.ds(c * R2, R2)]
        land[me] = csum[pl.ds(c * R2, R2), :]
        cps = [_remote(half, land.at[me], send_sems.at[1 + j], recv_sems.at[1 + j], (px, py, c))
               for j, (px, py) in enumerate(chips)]
        for cp in cps:
            cp.start()
        for j, (px, py) in enumerate(chips):
            _remote(half, land.at[2 * px + py], send_sems.at[1 + j], recv_sems.at[1 + j], (px, py, c)).wait_recv()
        for cp in cps:
            cp.wait_send()
        mine = go_ref.at[pl.ds(c * R2, R2)]
        go_ref[pl.ds(c * R2, R2), :] = (land[0] + land[1]) + (land[2] + land[3])
        share = _remote(mine, mine, send_sems.at[4], recv_sems.at[4], sibling)
        share.start()
        share.wait_send()
        other = go_ref.at[pl.ds((1 - c) * R2, R2)]
        _remote(other, other, send_sems.at[4], recv_sems.at[4], sibling).wait_recv()

    vm = pl.BlockSpec(memory_space=pltpu.VMEM)
    return pl.pallas_call(
        body, name="small_allreduce", in_specs=[vm], out_specs=vm, out_shape=_sds((R, 128)),
        scratch_shapes=[pltpu.VMEM((R, 128), F32), pltpu.VMEM((R, 128), F32), pltpu.VMEM((N_CHIPS, R2, 128), F32),
                        pltpu.SemaphoreType.DMA((5,)), pltpu.SemaphoreType.DMA((5,))],
        compiler_params=_cp(),
    )(gp)


def _adamw_many(ws, gs, ms, vs):
    n = len(ws)

    def body(*refs):
        w_r, g_r, m_r, v_r = refs[:n], refs[n:2 * n], refs[2 * n:3 * n], refs[3 * n:4 * n]
        d_r, mo_r, vo_r = refs[4 * n:5 * n], refs[5 * n:6 * n], refs[6 * n:]
        for i in range(n):
            d_r[i][...], mo_r[i][...], vo_r[i][...] = _adamw_math(w_r[i][...], g_r[i][...], m_r[i][...], v_r[i][...])

    vm = pl.BlockSpec(memory_space=pltpu.VMEM)
    shapes = [_sds(a.shape) for a in ws]
    outs = pl.pallas_call(
        body, name="adamw_small", in_specs=[vm] * (4 * n), out_specs=[vm] * (3 * n), out_shape=shapes * 3, compiler_params=_cp(),
    )(*ws, *gs, *ms, *vs)
    return outs[:n], outs[n:2 * n], outs[2 * n:]


def _pack(arrs):
    rows = []
    for a in arrs:
        f = a.reshape(-1)
        f = jnp.pad(f, (0, (-f.shape[0]) % 128))
        rows.append(f.reshape(-1, 128))
    p = jnp.concatenate(rows, axis=0)
    return jnp.pad(p, ((0, (-p.shape[0]) % 8), (0, 0)))


def _unpack(p, shapes):
    out, r = [], 0
    for s in shapes:
        n = math.prod(s)
        nr = -(-n // 128)
        out.append(p[r:r + nr].reshape(-1)[:n].reshape(s))
        r += nr
    return out


class _ExchangeOps(_PlainOps):
    def __init__(self, rest_shards, where):
        self.rest_shards, self.where = rest_shards, where
        self.reduced = []

    def in_proj(self, h, wt_perm):
        proj, g_glu, g_su, g_du, g_out = _mm(h, wt_perm, tb=True, name="in_proj", tm=1024, tn=1152,
                                             exchange=_gather_exchange(self.rest_shards))
        cat = lambda g: jnp.concatenate([g[j] for j in range(N_CHIPS)], axis=1)
        return proj, (g_glu.reshape(D_S5, D_S5), cat(g_su), cat(g_du), g_out.reshape(D_MODEL, D_MODEL))

    def _chip_sums(self, gxs, name):
        r1s = _swap_halves(gxs, name)
        return r1s, [_chip_sums(gx, r1, self.where) for gx, r1 in zip(gxs, r1s)]

    def rest_grads(self, d_w_glu, d_w_su, d_w_du, d_w_out):
        gxs = [d_w_glu.reshape(N_CHIPS, 2, D_S5 // 8, D_S5), d_w_su.reshape(N_CHIPS, 2, D_S5 // 2, D_MODEL // N_CHIPS),
               d_w_du.reshape(N_CHIPS, 2, D_DN // 2, D_MODEL // N_CHIPS), d_w_out.reshape(N_CHIPS, 2, D_MODEL // 8, D_MODEL)]
        r1s, csbs = self._chip_sums(gxs, "rs_swap_rest")
        self.rest = (gxs, r1s, csbs)

    def d_w_in(self, h, dproj):
        gxs, r1s, csbs = self.rest
        d_wt_perm, *r2s = _mm(dproj, h, ta=True, name="d_w_in", tm=1152, tn=1024, exchange=_owners_exchange(csbs))
        self.reduced = list(zip(gxs, r1s, r2s))
        return d_wt_perm

    def d_h(self, dproj, wt_perm, d_wt_perm):
        gx = _wt_shards(d_wt_perm).reshape(N_CHIPS, 2, WT_ROWS // 2, D_MODEL)
        (r1,), (csb,) = self._chip_sums([gx], "rs_swap_w_in")
        dh, r2 = _mm(dproj, wt_perm, name="d_h", tm=2048, tn=1024, tk=1152, exchange=_owners_exchange([csb]))
        self.reduced = [(gx, r1, r2)] + self.reduced
        return dh


WT_SHARD = D_IN // N_CHIPS
WT_ROWS = 2592


def _wt_perm(shards):
    j, lo = divmod(OFF_GS, WT_SHARD)
    hi = lo + 2 * DN_HEADS
    assert hi <= WT_SHARD
    pad = jnp.zeros((D_IN_PAD - D_IN, shards.shape[2]), shards.dtype)
    parts = [shards[i] for i in range(j)] + [shards[j, :lo], shards[j, hi:]] + [shards[i] for i in range(j + 1, N_CHIPS)]
    return jnp.concatenate(parts + [shards[j, lo:hi], pad], axis=0)


def _wt_shards(perm):
    j, nb = OFF_GS // WT_SHARD, 2 * DN_HEADS

    def rows(i):
        if i < j:
            return [perm[i * WT_SHARD:(i + 1) * WT_SHARD]]
        if i > j:
            return [perm[i * WT_SHARD - nb:(i + 1) * WT_SHARD - nb]]
        return [perm[j * WT_SHARD:OFF_GS], perm[OFF_B:OFF_B + nb], perm[OFF_GS:(j + 1) * WT_SHARD - nb]]

    pad = jnp.zeros((WT_ROWS - WT_SHARD, perm.shape[1]), perm.dtype)
    return jnp.stack([jnp.concatenate(rows(i) + [pad], axis=0) for i in range(N_CHIPS)])


_SMALL = ("ln_w", "s5_lam_re", "s5_lam_im", "s5_log_step", "s5_b_re", "s5_b_im", "s5_c_re", "s5_c_im", "s5_d",
          "dn_a_log", "dn_dt_bias", "dn_norm_w", "final_norm_w")
_BIG = ("w_in", "s5_w_glu", "s5_w_up", "dn_w_up", "w_out")
_ORDER = ("ln_w", "w_in", "s5_lam_re", "s5_lam_im", "s5_log_step", "s5_b_re", "s5_b_im", "s5_c_re", "s5_c_im", "s5_d",
          "s5_w_glu", "s5_w_up", "dn_conv_w", "dn_a_log", "dn_dt_bias", "dn_norm_w", "dn_w_up", "w_out", "final_norm_w")


def kernel(x, ln_w, w_in, s5_lam_re, s5_lam_im, s5_log_step, s5_b_re, s5_b_im, s5_c_re, s5_c_im, s5_d, s5_w_glu, s5_w_up, dn_conv_w, dn_a_log, dn_dt_bias, dn_norm_w, dn_w_up, w_out, final_norm_w, loss_target, m_ln_w, m_w_in, m_s5_lam_re, m_s5_lam_im, m_s5_log_step, m_s5_b_re, m_s5_b_im, m_s5_c_re, m_s5_c_im, m_s5_d, m_s5_w_glu, m_s5_w_up, m_dn_conv_w, m_dn_a_log, m_dn_dt_bias, m_dn_norm_w, m_dn_w_up, m_w_out, m_final_norm_w, v_ln_w, v_w_in, v_s5_lam_re, v_s5_lam_im, v_s5_log_step, v_s5_b_re, v_s5_b_im, v_s5_c_re, v_s5_c_im, v_s5_d, v_s5_w_glu, v_s5_w_up, v_dn_conv_w, v_dn_a_log, v_dn_dt_bias, v_dn_norm_w, v_dn_w_up, v_w_out, v_final_norm_w):
    w = dict(ln_w=ln_w, w_in=w_in, s5_lam_re=s5_lam_re, s5_lam_im=s5_lam_im, s5_log_step=s5_log_step, s5_b_re=s5_b_re,
             s5_b_im=s5_b_im, s5_c_re=s5_c_re, s5_c_im=s5_c_im, s5_d=s5_d, s5_w_glu=s5_w_glu, s5_w_up=s5_w_up,
             dn_conv_w=dn_conv_w, dn_a_log=dn_a_log, dn_dt_bias=dn_dt_bias, dn_norm_w=dn_norm_w, dn_w_up=dn_w_up, w_out=w_out,
             final_norm_w=final_norm_w)
    m = dict(ln_w=m_ln_w, w_in=m_w_in, s5_lam_re=m_s5_lam_re, s5_lam_im=m_s5_lam_im, s5_log_step=m_s5_log_step,
             s5_b_re=m_s5_b_re, s5_b_im=m_s5_b_im, s5_c_re=m_s5_c_re, s5_c_im=m_s5_c_im, s5_d=m_s5_d, s5_w_glu=m_s5_w_glu,
             s5_w_up=m_s5_w_up, dn_conv_w=m_dn_conv_w, dn_a_log=m_dn_a_log, dn_dt_bias=m_dn_dt_bias, dn_norm_w=m_dn_norm_w,
             dn_w_up=m_dn_w_up, w_out=m_w_out, final_norm_w=m_final_norm_w)
    v = dict(ln_w=v_ln_w, w_in=v_w_in, s5_lam_re=v_s5_lam_re, s5_lam_im=v_s5_lam_im, s5_log_step=v_s5_log_step,
             s5_b_re=v_s5_b_re, s5_b_im=v_s5_b_im, s5_c_re=v_s5_c_re, s5_c_im=v_s5_c_im, s5_d=v_s5_d, s5_w_glu=v_s5_w_glu,
             s5_w_up=v_s5_w_up, dn_conv_w=v_dn_conv_w, dn_a_log=v_dn_a_log, dn_dt_bias=v_dn_dt_bias, dn_norm_w=v_dn_norm_w,
             dn_w_up=v_dn_w_up, w_out=v_w_out, final_norm_w=v_final_norm_w)
    xi, yi, ci = _place()
    chip = 2 * xi + yi
    where = jnp.stack([ci, chip]).astype(jnp.int32)

    tr = lambda a: jnp.swapaxes(a[0], 0, 1)
    g_in, g_conv = _run_exchange(_gather_exchange([tr(w_in).astype(BF16)], [dn_conv_w[0]], by_columns=True), "gather_w_in")
    cat = lambda g: jnp.concatenate([g[j] for j in range(N_CHIPS)], axis=1)
    w_perm = _wt_perm(g_in)

    ops = _ExchangeOps([w[n][0].astype(BF16) for n in _BIG[1:]], where)
    loss_row, grad_x, g = _local_step(
        x[0], loss_target[0], ln_w, w_perm, s5_lam_re[0], s5_lam_im[0], s5_log_step, s5_b_re[0], s5_b_im[0], s5_c_re[0],
        s5_c_im[0], s5_d, cat(g_conv), dn_a_log, dn_dt_bias, dn_norm_w, final_norm_w[None], ops)
    loss = lax.psum(loss_row[0, 0], ("x", "y", "c"))

    gfs = [_owner_sum(gx, r1, r2x, where) for gx, r1, r2x in ops.reduced]
    gfs = _share_halves(gfs)
    grads, deltas, new_m, new_v = {}, {}, {}, {}
    gt = gfs[0].reshape(WT_ROWS, D_MODEL)[:WT_SHARD]
    d_, m_, v_ = _adamw(tr(w_in), gt, tr(m_w_in), tr(v_w_in), "adamw_w_in")
    grads["w_in"], deltas["w_in"], new_m["w_in"], new_v["w_in"] = (jnp.swapaxes(a, 0, 1)[None] for a in (gt, d_, m_, v_))
    for n, gf in zip(_BIG[1:], gfs[1:]):
        shp = w[n].shape
        g2 = gf.reshape(shp[1:])
        d_, m_, v_ = _adamw(w[n][0], g2, m[n][0], v[n][0], "adamw_" + n)
        grads[n], deltas[n], new_m[n], new_v[n] = g2.reshape(shp), d_.reshape(shp), m_.reshape(shp), v_.reshape(shp)

    go = _small_allreduce(_pack([g[n] for n in _SMALL] + [g["dn_conv_w"]]))
    lanes = {"s5_b_re": (S5_GROUPS, S5_STATE * S5_GROUP), "s5_b_im": (S5_GROUPS, S5_STATE * S5_GROUP)}
    flat = [lanes.get(n, (math.prod(w[n].shape[:-1]), w[n].shape[-1])) for n in _SMALL]
    *gs, g_conv = _unpack(go, flat + [(CONV_K, 3 * D_DN)])
    as2d = lambda t: [t[n].reshape(s) for n, s in zip(_SMALL, flat)]
    for dst, src in zip((grads, deltas, new_m, new_v), (gs, *_adamw_many(as2d(w), gs, as2d(m), as2d(v)))):
        for n, a in zip(_SMALL, src):
            dst[n] = a.reshape(w[n].shape)
    cc = 3 * D_DN // N_CHIPS
    g_conv_mine = lax.dynamic_slice(g_conv, (0, chip * cc), (CONV_K, cc))
    d_, m_, v_ = _adamw(dn_conv_w[0], g_conv_mine, m_dn_conv_w[0], v_dn_conv_w[0], "adamw_dn_conv_w")
    grads["dn_conv_w"], deltas["dn_conv_w"], new_m["dn_conv_w"], new_v["dn_conv_w"] = (
        g_conv_mine[None], d_[None], m_[None], v_[None])

    return (loss, grad_x[None], *[grads[n] for n in _ORDER], *[deltas[n] for n in _ORDER], *[new_m[n] for n in _ORDER],
            *[new_v[n] for n in _ORDER])
```

```python
import functools
import math

import jax
import jax.numpy as jnp
from jax import lax
from jax.experimental import pallas as pl
from jax.experimental.pallas import tpu as pltpu

F32 = jnp.float32
BF16 = jnp.bfloat16
HI = lax.Precision.HIGHEST
MESH = pl.DeviceIdType.MESH
ANY = pl.BlockSpec(memory_space=pl.ANY)

EPS = 1e-6
D_MODEL = 2048
D_S5 = 1024
S5_GROUP = 16
S5_GROUPS = 64
S5_STATE = 64
S5_BLOCKS = 8
S5_SEG = 8
DN_HEADS = 8
DN_HEAD_DIM = 128
D_DN = 1024
CONV_K = 4
CHUNK = 64
D_IN = 10256
D_IN_PAD = 10368
OFF_US, OFF_ZS, OFF_Q, OFF_K, OFF_V, OFF_ZD, OFF_GS, OFF_GD, OFF_B = 0, 1024, 2048, 3072, 4096, 5120, 6144, 8192, 10240
N_CHIPS = 4
N_DEV = 8
VMEM_LIMIT = 56 * 1024 * 1024

ADAM_LR = 0.001
ADAM_B1 = 0.9
ADAM_B2 = 0.999
ADAM_EPS = 1e-08
ADAM_WD = 0.01
ADAM_STEP = 10


def _cp(sem=None):
    return pltpu.CompilerParams(dimension_semantics=sem, vmem_limit_bytes=VMEM_LIMIT)


def _sds(shape, dtype=F32):
    return jax.ShapeDtypeStruct(tuple(shape), dtype)


def _sigmoid(x):
    return 1.0 / (1.0 + jnp.exp(-x))


def _silu(x):
    return x * _sigmoid(x)


def _dsilu(x):
    s = _sigmoid(x)
    return s * (1.0 + x * (1.0 - s))


class _Exchange:
    def __init__(self, ins, out_shapes, n_sems, start, finish):
        self.ins, self.out_shapes, self.n_sems, self.start, self.finish = list(ins), list(out_shapes), n_sems, start, finish


def _mm(a, b, *, name, ta=False, tb=False, out_dtype=F32, tm=512, tn=512, tk=2048, shard_out=False, exchange=None):
    if ta:
        K, M = a.shape
    else:
        M, K = a.shape
    if tb:
        N, K2 = b.shape
    else:
        K2, N = b.shape
    assert K == K2, (a.shape, b.shape)
    tm, tn, tk = min(tm, M), min(tn, N), min(tk, K)
    assert M % tm == 0 and N % tn == 0 and K % tk == 0, (M, N, K, tm, tn, tk)
    nk = K // tk
    dims = (((0 if ta else 1,), (1 if tb else 0,)), ((), ()))

    gm, gn = M // tm, N // tn
    n_in = len(exchange.ins) if exchange else 0
    n_out = len(exchange.out_shapes) if exchange else 0

    def body(*refs):
        a_ref, b_ref, xin, o_ref = refs[0], refs[1], refs[2:2 + n_in], refs[2 + n_in]
        xout, rest = refs[3 + n_in:3 + n_in + n_out], refs[3 + n_in + n_out:]
        i, j, k = pl.program_id(0), pl.program_id(1), pl.program_id(2)
        if exchange:
            sems = rest[-3:]

            @pl.when(jnp.logical_and(jnp.logical_and(i == 0, j == 0), k == 0))
            def _():
                exchange.start(xin, xout, *sems)

        p = lax.dot_general(a_ref[...].astype(BF16), b_ref[...].astype(BF16), dims, preferred_element_type=F32)
        if nk == 1:
            o_ref[...] = p.astype(out_dtype).reshape(o_ref.shape)
        else:
            acc_ref = rest[0]

            @pl.when(k == 0)
            def _():
                acc_ref[...] = p

            @pl.when(k > 0)
            def _():
                acc_ref[...] += p

            @pl.when(k == nk - 1)
            def _():
                o_ref[...] = acc_ref[...].astype(out_dtype).reshape(o_ref.shape)

        if exchange:
            @pl.when(jnp.logical_and(jnp.logical_and(i == gm - 1, j == gn - 1), k == nk - 1))
            def _():
                exchange.finish(xin, xout, *sems)

    a_spec = pl.BlockSpec((tk, tm), lambda i, j, k: (k, i)) if ta else pl.BlockSpec((tm, tk), lambda i, j, k: (i, k))
    b_spec = pl.BlockSpec((tn, tk), lambda i, j, k: (j, k)) if tb else pl.BlockSpec((tk, tn), lambda i, j, k: (k, j))
    if shard_out:
        o_spec = pl.BlockSpec((1, tm, tn), lambda i, j, k: (j, i, 0))
        o_shape = _sds((N // tn, M, tn), out_dtype)
    else:
        o_spec = pl.BlockSpec((tm, tn), lambda i, j, k: (i, j))
        o_shape = _sds((M, N), out_dtype)
    scratch = [pltpu.VMEM((tm, tn), F32)] if nk > 1 else []
    if not exchange:
        return pl.pallas_call(
            body, name=name, grid=(gm, gn, nk), in_specs=[a_spec, b_spec], out_specs=o_spec, out_shape=o_shape,
            scratch_shapes=scratch, compiler_params=_cp(("parallel", "parallel", "arbitrary")),
        )(a, b)
    scratch += [pltpu.SemaphoreType.DMA((exchange.n_sems,)) for _ in range(3)]
    return pl.pallas_call(
        body, name=name, grid=(gm, gn, nk), in_specs=[a_spec, b_spec] + [ANY] * n_in, out_specs=[o_spec] + [ANY] * n_out,
        out_shape=[o_shape] + exchange.out_shapes, scratch_shapes=scratch,
        compiler_params=_cp(("arbitrary", "arbitrary", "arbitrary")),
    )(a, b, *exchange.ins)


def _ln_fwd(x, w):
    L, D = x.shape
    tm = min(256, L)

    def body(x_ref, w_ref, h_ref, r_ref):
        xv = x_ref[...]
        r = lax.rsqrt(jnp.mean(xv * xv, axis=-1, keepdims=True) + EPS)
        h_ref[...] = (xv * r * w_ref[...]).astype(BF16)
        r_ref[...] = r

    return pl.pallas_call(
        body, name="ln_fwd", grid=(L // tm,),
        in_specs=[pl.BlockSpec((tm, D), lambda i: (i, 0)), pl.BlockSpec((1, D), lambda i: (0, 0))],
        out_specs=[pl.BlockSpec((tm, D), lambda i: (i, 0)), pl.BlockSpec((tm, 1), lambda i: (i, 0))],
        out_shape=[_sds((L, D), BF16), _sds((L, 1))], compiler_params=_cp(("parallel",)),
    )(x, w)


def _ln_bwd(x, r, w, dh, dx2):
    L, D = x.shape
    tm = min(256, L)

    def body(x_ref, r_ref, w_ref, dh_ref, dx2_ref, dx_ref, dw_ref):
        i = pl.program_id(0)
        xv, rv, dhv = x_ref[...], r_ref[...], dh_ref[...]
        t = dhv * w_ref[...]
        m = jnp.mean(t * xv, axis=-1, keepdims=True)
        dx_ref[...] = dx2_ref[...] + rv * t - xv * (rv * rv * rv) * m
        part = jnp.sum(dhv * xv * rv, axis=0, keepdims=True)

        @pl.when(i == 0)
        def _():
            dw_ref[...] = part

        @pl.when(i > 0)
        def _():
            dw_ref[...] += part

    row = pl.BlockSpec((tm, D), lambda i: (i, 0))
    return pl.pallas_call(
        body, name="ln_bwd", grid=(L // tm,),
        in_specs=[row, pl.BlockSpec((tm, 1), lambda i: (i, 0)), pl.BlockSpec((1, D), lambda i: (0, 0)), row, row],
        out_specs=[row, pl.BlockSpec((1, D), lambda i: (0, 0))],
        out_shape=[_sds((L, D)), _sds((1, D))], compiler_params=_cp(("arbitrary",)),
    )(x, r, w, dh, dx2)


def _s5_param_math(lam_re, lam_im, log_step, b_re, b_im, expand):
    step = jnp.exp(log_step)
    mag = jnp.exp(lam_re * step)
    abar_re = mag * jnp.cos(lam_im * step)
    abar_im = mag * jnp.sin(lam_im * step)
    den = lam_re * lam_re + lam_im * lam_im
    xr = abar_re - 1.0
    f_re = (xr * lam_re + abar_im * lam_im) / den
    f_im = (abar_im * lam_re - xr * lam_im) / den
    fe_re = jnp.dot(f_re, expand, precision=HI, preferred_element_type=F32)
    fe_im = jnp.dot(f_im, expand, precision=HI, preferred_element_type=F32)
    bb_re = fe_re * b_re - fe_im * b_im
    bb_im = fe_re * b_im + fe_im * b_re
    return abar_re, abar_im, bb_re, bb_im


def _s5_expand():
    p = lax.broadcasted_iota(jnp.int32, (S5_STATE, S5_STATE * S5_GROUP), 0)
    q = lax.broadcasted_iota(jnp.int32, (S5_STATE, S5_STATE * S5_GROUP), 1)
    return (q // S5_GROUP == p).astype(F32)


def _s5_param_fwd(lam_re, lam_im, log_step, b_re, b_im):
    G, P = lam_re.shape

    def body(lr, li, ls, br, bi, ar_o, ai_o, bbr_o, bbi_o):
        outs = _s5_param_math(lr[...], li[...], ls[...], br[...], bi[...], _s5_expand())
        for o, v in zip((ar_o, ai_o, bbr_o, bbi_o), outs):
            o[...] = v

    return pl.pallas_call(
        body, name="s5_param_fwd",
        out_shape=[_sds((G, P)), _sds((G, P)), _sds(b_re.shape), _sds(b_re.shape)], compiler_params=_cp(),
    )(lam_re, lam_im, log_step, b_re, b_im)


def _s5_param_bwd(lam_re, lam_im, log_step, b_re, b_im, dar, dai, dbbr, dbbi):
    G, P = lam_re.shape

    def body(lr, li, ls, br, bi, g0, g1, g2, g3, dlr, dli, dls, dbr, dbi):
        ex = _s5_expand()
        _, f = jax.vjp(lambda a, b, c, d, e: _s5_param_math(a, b, c, d, e, ex), lr[...], li[...], ls[...], br[...], bi[...])
        grads = f((g0[...], g1[...], g2[...], g3[...]))
        for o, v in zip((dlr, dli, dls, dbr, dbi), grads):
            o[...] = v

    return pl.pallas_call(
        body, name="s5_param_bwd",
        out_shape=[_sds((G, P)), _sds((G, P)), _sds((G, 1)), _sds(b_re.shape), _sds(b_re.shape)], compiler_params=_cp(),
    )(lam_re, lam_im, log_step, b_re, b_im, dar, dai, dbbr, dbbi)


def _to_segs(src_ref, dst_ref, L):
    S = L // S5_SEG

    def body(j, carry):
        dst_ref[pl.ds(pl.multiple_of(S5_SEG * j, S5_SEG), S5_SEG), :] = src_ref[pl.ds(j, S5_SEG, stride=S), :]
        return carry

    lax.fori_loop(0, S, body, 0, unroll=8)


def _from_segs(src_ref, L, write):
    S = L // S5_SEG
    for seg in range(S5_SEG):
        def body(jb, carry, seg=seg):
            j0 = 16 * jb
            write(pl.multiple_of(seg * S + j0, 16), src_ref[pl.ds(S5_SEG * j0 + seg, 16, stride=S5_SEG), :])
            return carry

        lax.fori_loop(0, S // 16, body, 0, unroll=4)


def _scan_segs(ar, ai, re_ref, im_ref, end_r_ref, end_i_ref, c_r_ref, c_i_ref, L, tile0, reverse):
    S = L // S5_SEG
    NB, LN = re_ref.shape[0], 128
    assert S & (S - 1) == 0
    tile = lambda j: pl.ds(pl.multiple_of(S5_SEG * (tile0 + j), S5_SEG), S5_SEG)
    ar8 = [jnp.broadcast_to(ar[:, b * LN:(b + 1) * LN], (S5_SEG, LN)) for b in range(NB)]
    ai8 = [jnp.broadcast_to(ai[:, b * LN:(b + 1) * LN], (S5_SEG, LN)) for b in range(NB)]

    def step(idx, carry):
        rows = tile(S - 1 - idx if reverse else idx)
        out = []
        for b in range(NB):
            sr, si = carry[b]
            nr = ar8[b] * sr - ai8[b] * si + re_ref[b, rows, :]
            ni = ar8[b] * si + ai8[b] * sr + im_ref[b, rows, :]
            re_ref[b, rows, :] = nr
            im_ref[b, rows, :] = ni
            out.append((nr, ni))
        return tuple(out)

    z8 = jnp.zeros((S5_SEG, LN), F32)
    fin = lax.fori_loop(0, S, step, tuple((z8, z8) for _ in range(NB)), unroll=4)
    order = range(S5_SEG - 2, -1, -1) if reverse else range(1, S5_SEG)
    for b in range(NB):
        end_r_ref[b], end_i_ref[b] = fin[b]
        pr, pi = ar8[b][:1], ai8[b][:1]
        for _ in range(int(math.log2(S))):
            pr, pi = pr * pr - pi * pi, 2.0 * pr * pi
        first = S5_SEG - 1 if reverse else 0
        c_r_ref[b, pl.ds(first, 1), :] = jnp.zeros((1, LN), F32)
        c_i_ref[b, pl.ds(first, 1), :] = jnp.zeros((1, LN), F32)
        cr, ci = end_r_ref[b, pl.ds(first, 1), :], end_i_ref[b, pl.ds(first, 1), :]
        for i in order:
            c_r_ref[b, pl.ds(i, 1), :] = cr
            c_i_ref[b, pl.ds(i, 1), :] = ci
            er, ei = end_r_ref[b, pl.ds(i, 1), :], end_i_ref[b, pl.ds(i, 1), :]
            cr, ci = er + pr * cr - pi * ci, ei + pr * ci + pi * cr

    entering = [(c_r_ref[b], c_i_ref[b]) for b in range(NB)]

    def fix(idx, carry):
        rows = tile(S - 1 - idx if reverse else idx)
        out = []
        for b in range(NB):
            pr, pi = carry[b]
            cr, ci = entering[b]
            re_ref[b, rows, :] += pr * cr - pi * ci
            im_ref[b, rows, :] += pr * ci + pi * cr
            out.append((pr * ar8[b] - pi * ai8[b], pr * ai8[b] + pi * ar8[b]))
        return tuple(out)

    lax.fori_loop(0, S, fix, tuple((ar8[b], ai8[b]) for b in range(NB)), unroll=4)


def _s5_seg_scratch(L, cs, pad):
    NB = cs // 128
    small = [pltpu.VMEM((NB, S5_SEG, 128), F32) for _ in range(4)]
    return [pltpu.VMEM((NB, L + pad, 128), F32), pltpu.VMEM((NB, L + pad, 128), F32)] + small


def _s5_core_fwd(proj, wbr, wbi, a_re, a_im, cbr, cbi):
    L = proj.shape[0]
    nb, ci, cs = wbr.shape
    NB = cs // 128

    def body(u_ref, wbr_ref, wbi_ref, ar_ref, ai_ref, cbr_ref, cbi_ref, y_ref, sr, si, er, ei, cr, cim, up, yp):
        _to_segs(u_ref, up, L)
        u = up[...].astype(BF16)
        for b in range(NB):
            lanes = pl.ds(b * 128, 128)
            sr[b] = jnp.dot(u, wbr_ref[0, :, lanes], preferred_element_type=F32)
            si[b] = jnp.dot(u, wbi_ref[0, :, lanes], preferred_element_type=F32)
        _scan_segs(ar_ref[...], ai_ref[...], sr, si, er, ei, cr, cim, L, 0, False)
        y = jnp.zeros((L, ci), F32)
        for b in range(NB):
            lanes = pl.ds(b * 128, 128)
            y = y + (jnp.dot(sr[b].astype(BF16), cbr_ref[0, lanes, :], preferred_element_type=F32)
                     - jnp.dot(si[b].astype(BF16), cbi_ref[0, lanes, :], preferred_element_type=F32))
        yp[...] = y

        def write(row, val):
            y_ref[pl.ds(row, 16), :] = val

        _from_segs(yp, L, write)

    wspec = pl.BlockSpec((1, ci, cs), lambda j: (j, 0, 0))
    aspec = pl.BlockSpec((1, cs), lambda j: (0, j))
    cspec = pl.BlockSpec((1, cs, ci), lambda j: (j, 0, 0))
    return pl.pallas_call(
        body, name="s5_core_fwd", grid=(nb,),
        in_specs=[pl.BlockSpec((L, ci), lambda j: (0, OFF_US // ci + j)), wspec, wspec, aspec, aspec, cspec, cspec],
        out_specs=pl.BlockSpec((L, ci), lambda j: (0, j)), out_shape=_sds((L, nb * ci)),
        scratch_shapes=_s5_seg_scratch(L, cs, 0) + [pltpu.VMEM((L, ci), F32), pltpu.VMEM((L, ci), F32)],
        compiler_params=_cp(("arbitrary",)),
    )(proj, wbr, wbi, a_re, a_im, cbr, cbi)


def _s5_core_bwd(proj, wbr, wbi, a_re, a_im, cbr, cbi, dyc, du1):
    L = proj.shape[0]
    nb, ci, cs = wbr.shape
    NB = cs // 128
    S = L // S5_SEG
    PAD = S5_SEG

    def body(u_ref, wbr_ref, wbi_ref, ar_ref, ai_ref, cbr_ref, cbi_ref, dy_ref, du1_ref,
             du_ref, dwbr_ref, dwbi_ref, dcbr_ref, dcbi_ref, dar_ref, dai_ref,
             sr, si, er, ei, cr, cim, lr, li, up, dyp, dup):
        tn = (((0,), (0,)), ((), ()))
        nt = (((1,), (1,)), ((), ()))
        _to_segs(u_ref, up, L)
        _to_segs(dy_ref, dyp, L)
        _to_segs(du1_ref, dup, L)
        u = up[...].astype(BF16)
        dy = dyp[...].astype(BF16)
        ar, ai = ar_ref[...], ai_ref[...]
        for b in range(NB):
            lanes = pl.ds(b * 128, 128)
            sr[b, pl.ds(PAD, L), :] = jnp.dot(u, wbr_ref[0, :, lanes], preferred_element_type=F32)
            si[b, pl.ds(PAD, L), :] = jnp.dot(u, wbi_ref[0, :, lanes], preferred_element_type=F32)
        _scan_segs(ar, ai, sr, si, er, ei, cr, cim, L, 1, False)
        for b in range(NB):
            lanes = pl.ds(b * 128, 128)
            sr[b, pl.ds(0, PAD), :] = cr[b]
            si[b, pl.ds(0, PAD), :] = cim[b]
            lr[b] = lax.dot_general(dy, cbr_ref[0, lanes, :], nt, preferred_element_type=F32)
            li[b] = -lax.dot_general(dy, cbi_ref[0, lanes, :], nt, preferred_element_type=F32)
            dcbr_ref[0, lanes, :] = lax.dot_general(sr[b, pl.ds(PAD, L), :].astype(BF16), dy, tn, preferred_element_type=F32)
            dcbi_ref[0, lanes, :] = -lax.dot_general(si[b, pl.ds(PAD, L), :].astype(BF16), dy, tn, preferred_element_type=F32)
        _scan_segs(ar, -ai, lr, li, er, ei, cr, cim, L, 0, True)

        def da_step(j, carry):
            rows = pl.ds(pl.multiple_of(S5_SEG * j, S5_SEG), S5_SEG)
            out = []
            for b in range(NB):
                dar, dai = carry[b]
                pr_, pi_ = sr[b, rows, :], si[b, rows, :]
                gr, gi = lr[b, rows, :], li[b, rows, :]
                out.append((dar + (gr * pr_ + gi * pi_), dai + (gi * pr_ - gr * pi_)))
            return tuple(out)

        z8 = jnp.zeros((S5_SEG, 128), F32)
        acc = lax.fori_loop(0, S, da_step, tuple((z8, z8) for _ in range(NB)), unroll=4)
        du = dup[...]
        for b in range(NB):
            lanes = pl.ds(b * 128, 128)
            dar_ref[:, lanes] = jnp.sum(acc[b][0], axis=0, keepdims=True)
            dai_ref[:, lanes] = jnp.sum(acc[b][1], axis=0, keepdims=True)
            gr, gi = lr[b].astype(BF16), li[b].astype(BF16)
            du = du + (lax.dot_general(gr, wbr_ref[0, :, lanes], nt, preferred_element_type=F32)
                       + lax.dot_general(gi, wbi_ref[0, :, lanes], nt, preferred_element_type=F32))
            dwbr_ref[0, :, lanes] = lax.dot_general(u, gr, tn, preferred_element_type=F32)
            dwbi_ref[0, :, lanes] = lax.dot_general(u, gi, tn, preferred_element_type=F32)
        dup[...] = du

        def write(row, val):
            du_ref[pl.ds(row, 16), :] = val.astype(BF16)

        _from_segs(dup, L, write)

    wspec = pl.BlockSpec((1, ci, cs), lambda j: (j, 0, 0))
    aspec = pl.BlockSpec((1, cs), lambda j: (0, j))
    cspec = pl.BlockSpec((1, cs, ci), lambda j: (j, 0, 0))
    col = pl.BlockSpec((L, ci), lambda j: (0, j))
    return pl.pallas_call(
        body, name="s5_core_bwd", grid=(nb,),
        in_specs=[pl.BlockSpec((L, ci), lambda j: (0, OFF_US // ci + j)), wspec, wspec, aspec, aspec, cspec, cspec, col, col],
        out_specs=[col, wspec, wspec, cspec, cspec, aspec, aspec],
        out_shape=[_sds((L, nb * ci), BF16), _sds(wbr.shape), _sds(wbr.shape), _sds(cbr.shape), _sds(cbr.shape),
                   _sds((1, nb * cs)), _sds((1, nb * cs))],
        scratch_shapes=(_s5_seg_scratch(L, cs, PAD) + [pltpu.VMEM((NB, L, 128), F32), pltpu.VMEM((NB, L, 128), F32)]
                        + [pltpu.VMEM((L, ci), F32) for _ in range(3)]),
        compiler_params=_cp(("arbitrary",)),
    )(proj, wbr, wbi, a_re, a_im, cbr, cbi, dyc, du1)


def _s5_post_math(yc, u, z, d, wg):
    y = yc + d * u
    y1 = jax.nn.gelu(y)
    t = jnp.dot(y1.astype(BF16), wg, preferred_element_type=F32)
    sg = _sigmoid(t)
    return y, y1, sg


def _s5_post_fwd(yc, proj, d, wg):
    L, W = yc.shape
    tm = min(256, L)

    def body(yc_ref, u_ref, z_ref, d_ref, wg_ref, o_ref):
        _, y1, sg = _s5_post_math(yc_ref[...], u_ref[...], z_ref[...], d_ref[...], wg_ref[...])
        o_ref[...] = (y1 * sg * _silu(z_ref[...])).astype(BF16)

    row = pl.BlockSpec((tm, W), lambda i: (i, 0))
    return pl.pallas_call(
        body, name="s5_post_fwd", grid=(L // tm,),
        in_specs=[row, pl.BlockSpec((tm, W), lambda i: (i, OFF_US // W)), pl.BlockSpec((tm, W), lambda i: (i, OFF_ZS // W)),
                  pl.BlockSpec((1, W), lambda i: (0, 0)), pl.BlockSpec((W, W), lambda i: (0, 0))],
        out_specs=row, out_shape=_sds((L, W), BF16), compiler_params=_cp(("parallel",)),
    )(yc, proj, proj, d, wg)


def _s5_post_bwd(yc, proj, d, wg, dout):
    L, W = yc.shape
    tm = min(256, L)

    def body(yc_ref, u_ref, z_ref, d_ref, wg_ref, do_ref, dyc_ref, du_ref, dz_ref, dd_ref, dwg_ref):
        i = pl.program_id(0)
        u, z, d_, wgv = u_ref[...], z_ref[...], d_ref[...], wg_ref[...]
        y, y1, sg = _s5_post_math(yc_ref[...], u, z, d_, wgv)
        dout_ = do_ref[...]
        y2 = y1 * sg
        dy2 = dout_ * _silu(z)
        dz_ref[...] = (dout_ * y2 * _dsilu(z)).astype(BF16)
        dt = (dy2 * y1 * sg * (1.0 - sg)).astype(BF16)
        dy1 = dy2 * sg + lax.dot_general(dt, wgv, (((1,), (1,)), ((), ())), preferred_element_type=F32)
        _, gelu_vjp = jax.vjp(jax.nn.gelu, y)
        dy = gelu_vjp(dy1)[0]
        dyc_ref[...] = dy
        du_ref[...] = dy * d_
        dd_part = jnp.sum(dy * u, axis=0, keepdims=True)
        dwg_part = lax.dot_general(y1.astype(BF16), dt, (((0,), (0,)), ((), ())), preferred_element_type=F32)

        @pl.when(i == 0)
        def _():
            dd_ref[...] = dd_part
            dwg_ref[...] = dwg_part

        @pl.when(i > 0)
        def _():
            dd_ref[...] += dd_part
            dwg_ref[...] += dwg_part

    row = pl.BlockSpec((tm, W), lambda i: (i, 0))
    return pl.pallas_call(
        body, name="s5_post_bwd", grid=(L // tm,),
        in_specs=[row, pl.BlockSpec((tm, W), lambda i: (i, OFF_US // W)), pl.BlockSpec((tm, W), lambda i: (i, OFF_ZS // W)),
                  pl.BlockSpec((1, W), lambda i: (0, 0)), pl.BlockSpec((W, W), lambda i: (0, 0)), row],
        out_specs=[row, row, row, pl.BlockSpec((1, W), lambda i: (0, 0)), pl.BlockSpec((W, W), lambda i: (0, 0))],
        out_shape=[_sds((L, W)), _sds((L, W)), _sds((L, W), BF16), _sds((1, W)), _sds((W, W))],
        compiler_params=_cp(("arbitrary",)),
    )(yc, proj, proj, d, wg, dout)


def _shift_down(x, s):
    if s == 0:
        return x
    rows = lax.broadcasted_iota(jnp.int32, x.shape, 0)
    return jnp.where(rows >= s, pltpu.roll(x, s, 0), 0.0)


def _shift_up(x, s):
    if s == 0:
        return x
    L = x.shape[0]
    rows = lax.broadcasted_iota(jnp.int32, x.shape, 0)
    return jnp.where(rows < L - s, pltpu.roll(x, L - s, 0), 0.0)


def _conv_pre(x, w):
    acc = w[CONV_K - 1:CONV_K, :] * x
    for s in range(1, CONV_K):
        acc = acc + w[CONV_K - 1 - s:CONV_K - s, :] * _shift_down(x, s)
    return acc


def _dn_conv_fwd(proj, conv_w):
    L = proj.shape[0]
    W = DN_HEAD_DIM
    nq = 2 * DN_HEADS

    def body(x_ref, w_ref, o_ref):
        j = pl.program_id(0)
        act = _silu(_conv_pre(x_ref[...], w_ref[...]))
        r = lax.rsqrt(jnp.sum(act * act, axis=-1, keepdims=True) + EPS)
        scale = jnp.where(j < DN_HEADS, DN_HEAD_DIM ** -0.5, 1.0)
        o_ref[...] = jnp.where(j < nq, act * r * scale, act)

    return pl.pallas_call(
        body, name="dn_conv_fwd", grid=(3 * DN_HEADS,),
        in_specs=[pl.BlockSpec((L, W), lambda j: (0, OFF_Q // W + j)), pl.BlockSpec((CONV_K, W), lambda j: (0, j))],
        out_specs=pl.BlockSpec((L, W), lambda j: (0, j)), out_shape=_sds((L, 3 * D_DN)), compiler_params=_cp(("parallel",)),
    )(proj, conv_w)


def _dn_conv_bwd(proj, conv_w, dout):
    L = proj.shape[0]
    W = DN_HEAD_DIM
    nq = 2 * DN_HEADS

    def body(x_ref, w_ref, do_ref, dx_ref, dw_ref):
        j = pl.program_id(0)
        x, w, dout_ = x_ref[...], w_ref[...], do_ref[...]
        pre = _conv_pre(x, w)
        act = _silu(pre)
        r = lax.rsqrt(jnp.sum(act * act, axis=-1, keepdims=True) + EPS)
        scale = jnp.where(j < DN_HEADS, DN_HEAD_DIM ** -0.5, 1.0)
        g = dout_ * scale
        dact_n = r * g - act * (r * r * r) * jnp.sum(g * act, axis=-1, keepdims=True)
        dact = jnp.where(j < nq, dact_n, dout_)
        dpre = dact * _dsilu(pre)
        dx = w[CONV_K - 1:CONV_K, :] * dpre
        for s in range(1, CONV_K):
            dx = dx + w[CONV_K - 1 - s:CONV_K - s, :] * _shift_up(dpre, s)
        dx_ref[...] = dx.astype(BF16)
        for s in range(CONV_K):
            dw_ref[pl.ds(CONV_K - 1 - s, 1), :] = jnp.sum(dpre * _shift_down(x, s), axis=0, keepdims=True)

    col = pl.BlockSpec((L, W), lambda j: (0, j))
    wsp = pl.BlockSpec((CONV_K, W), lambda j: (0, j))
    return pl.pallas_call(
        body, name="dn_conv_bwd", grid=(3 * DN_HEADS,),
        in_specs=[pl.BlockSpec((L, W), lambda j: (0, OFF_Q // W + j)), wsp, col], out_specs=[col, wsp],
        out_shape=[_sds((L, 3 * D_DN), BF16), _sds((CONV_K, 3 * D_DN))], compiler_params=_cp(("parallel",)),
    )(proj, conv_w, dout)


def _softplus(x):
    return jnp.maximum(x, 0.0) + jnp.log(1.0 + jnp.exp(-jnp.abs(x)))


def _dn_gates_fwd(proj, alog, dtb):
    L = proj.shape[0]
    W = 128

    def body(p_ref, al_ref, db_ref, o_ref):
        p = p_ref[...]
        lane = lax.broadcasted_iota(jnp.int32, p.shape, 1)
        g = -jnp.exp(al_ref[...]) * _softplus(p + db_ref[...])
        o_ref[...] = jnp.where(lane < DN_HEADS, _sigmoid(p), jnp.where(lane < 2 * DN_HEADS, g, 0.0))

    return pl.pallas_call(
        body, name="dn_gates_fwd", grid=(1,),
        in_specs=[pl.BlockSpec((L, W), lambda i: (0, OFF_B // W)), pl.BlockSpec((1, W), lambda i: (0, 0)),
                  pl.BlockSpec((1, W), lambda i: (0, 0))],
        out_specs=pl.BlockSpec((L, W), lambda i: (0, 0)), out_shape=_sds((L, W)), compiler_params=_cp(("arbitrary",)),
    )(proj, alog, dtb)


def _dn_gates_bwd(proj, alog, dtb, dgates):
    L = proj.shape[0]
    W = 128

    def body(p_ref, al_ref, db_ref, dg_ref, dp_ref, dal_ref, ddb_ref):
        p, dg = p_ref[...], dg_ref[...]
        lane = lax.broadcasted_iota(jnp.int32, p.shape, 1)
        is_g = jnp.logical_and(lane >= DN_HEADS, lane < 2 * DN_HEADS)
        beta = _sigmoid(p)
        na = -jnp.exp(al_ref[...])
        xs = p + db_ref[...]
        dsp = dg * na * _sigmoid(xs)
        dp_ref[...] = jnp.where(lane < DN_HEADS, dg * beta * (1.0 - beta), jnp.where(is_g, dsp, 0.0)).astype(BF16)
        dal_ref[...] = jnp.sum(jnp.where(is_g, dg * na * _softplus(xs), 0.0), axis=0, keepdims=True)
        ddb_ref[...] = jnp.sum(jnp.where(is_g, dsp, 0.0), axis=0, keepdims=True)

    one = pl.BlockSpec((1, W), lambda i: (0, 0))
    full = pl.BlockSpec((L, W), lambda i: (0, 0))
    return pl.pallas_call(
        body, name="dn_gates_bwd", grid=(1,),
        in_specs=[pl.BlockSpec((L, W), lambda i: (0, OFF_B // W)), one, one, full], out_specs=[full, one, one],
        out_shape=[_sds((L, W), BF16), _sds((1, W)), _sds((1, W))], compiler_params=_cp(("arbitrary",)),
    )(proj, alog, dtb, dgates)


def _bdot(a, b, dims):
    return lax.dot_general(a.astype(BF16), b.astype(BF16), (dims, ((), ())), preferred_element_type=F32)


_NN, _NT, _TN = ((1,), (0,)), ((1,), (1,)), ((0,), (0,))


def _dot3(a, b, dims):
    ah, bh = a.astype(BF16), b.astype(BF16)
    al, bl = (a - ah.astype(F32)).astype(BF16), (b - bh.astype(F32)).astype(BF16)
    (ca,), (cb,) = dims
    a3 = jnp.concatenate([ah, ah, al], axis=ca)
    b3 = jnp.concatenate([bh, bl, bh], axis=cb)
    return lax.dot_general(a3, b3, (dims, ((), ())), preferred_element_type=F32)


def _mm_family(raw):
    nn = jax.custom_vjp(lambda a, b: raw(a, b, _NN))
    nt = jax.custom_vjp(lambda a, b: raw(a, b, _NT))
    tn = jax.custom_vjp(lambda a, b: raw(a, b, _TN))
    nn.defvjp(lambda a, b: (raw(a, b, _NN), (a, b)), lambda r, g: (raw(g, r[1], _NT), raw(r[0], g, _TN)))
    nt.defvjp(lambda a, b: (raw(a, b, _NT), (a, b)), lambda r, g: (raw(g, r[1], _NN), raw(g, r[0], _TN)))
    tn.defvjp(lambda a, b: (raw(a, b, _TN), (a, b)), lambda r, g: (raw(r[1], g, _NT), raw(r[0], g, _NN)))
    return nn, nt, tn


_mm_nn, _mm_nt, _mm_tn = _mm_family(_bdot)
_m3_nn, _m3_nt, _m3_tn = _mm_family(_dot3)


def _tri_apply(x, upper):
    C = x.shape[0]
    ii = lax.broadcasted_iota(jnp.int32, (C, 3 * C), 0)
    jj = lax.broadcasted_iota(jnp.int32, (C, 3 * C), 1) % C
    mat = ((ii <= jj) if upper else (ii >= jj)).astype(BF16)
    hi = x.astype(BF16)
    r = x - hi.astype(F32)
    mid = r.astype(BF16)
    lo = (r - mid.astype(F32)).astype(BF16)
    return jnp.dot(mat, jnp.concatenate([hi, mid, lo], axis=0), preferred_element_type=F32)


_cumsum_rows = jax.custom_vjp(lambda x: _tri_apply(x, False))
_cumsum_rows.defvjp(lambda x: (_tri_apply(x, False), None), lambda _, g: (_tri_apply(g, True),))


def _uli(a_s):
    C = a_s[0].shape[0]
    ii = lax.broadcasted_iota(jnp.int32, (C, C), 0)
    jj = lax.broadcasted_iota(jnp.int32, (C, C), 1)
    eye = jnp.where(ii == jj, 1.0, 0.0)
    ts = [eye - a for a in a_s]
    ms = list(a_s)
    for _ in range(int(math.log2(C)) - 1):
        ms = [_dot3(m, m, _NN) for m in ms]
        ts = [t + _dot3(t, m, _NN) for t, m in zip(ts, ms)]
    return tuple(ts)


def _uli_bwd(ts, gs):
    xs = [_dot3(t, g, _TN) for t, g in zip(ts, gs)]
    return (tuple(-_dot3(x, t, _NT) for x, t in zip(xs, ts)),)


_unit_lower_inverse = jax.custom_vjp(_uli)
_unit_lower_inverse.defvjp(lambda a_s: (lambda ts: (ts, ts))(_uli(a_s)), _uli_bwd)


def _prep_math(qs, ks, vs, gcols, bcols):
    n = len(qs)
    C, dv = vs[0].shape
    ii = lax.broadcasted_iota(jnp.int32, (C, C), 0)
    jj = lax.broadcasted_iota(jnp.int32, (C, C), 1)
    causal = ii >= jj
    strict = ii > jj
    sf = strict.astype(F32)
    ones = jnp.ones((C, dv), F32)
    dms = [_cumsum_rows(g * sf) for g in gcols]
    gcbs = [_cumsum_rows(g * ones) for g in gcols]
    kks = [_mm_nt(k, k) for k in ks]
    qks = [_mm_nt(q, k) for q, k in zip(qs, ks)]
    decays = [jnp.where(causal, jnp.exp(jnp.where(causal, dm, 0.0)), 0.0) for dm in dms]
    glasts = [jnp.sum(g * ones, axis=0, keepdims=True) for g in gcols]
    egs = [jnp.exp(gcb) for gcb in gcbs]
    ts = _unit_lower_inverse(tuple(jnp.where(strict, b * kk * dc, 0.0) for b, kk, dc in zip(bcols, kks, decays)))
    us = [_m3_nn(t, v * b) for t, v, b in zip(ts, vs, bcols)]
    ws = [_m3_nn(t, k * b * eg) for t, k, b, eg in zip(ts, ks, bcols, egs)]
    return tuple((us[i], ws[i], qs[i] * egs[i], ks[i] * jnp.exp(glasts[i] - gcbs[i]), qks[i] * decays[i],
                  jnp.exp(glasts[i])) for i in range(n))


def _gate_cols(gates, h):
    lane = lax.broadcasted_iota(jnp.int32, gates.shape, 1)
    bcol = jnp.sum(jnp.where(lane == h, gates, 0.0), axis=1, keepdims=True)
    gcol = jnp.sum(jnp.where(lane == h + DN_HEADS, gates, 0.0), axis=1, keepdims=True)
    return gcol, bcol


DN_HB = 8


def _dn_prep_fwd(qkv, gates):
    L = qkv.shape[0]
    N, H, d, HB = L // CHUNK, DN_HEADS, DN_HEAD_DIM, DN_HB

    def body(q_ref, k_ref, v_ref, g_ref, u_ref, w_ref, qd_ref, kd_ref, qk_ref, egl_ref):
        h0 = pl.program_id(1) * HB
        gates_ = g_ref[...]
        lanes_of = [pl.ds(i * d, d) for i in range(HB)]
        cols = [_gate_cols(gates_, h0 + i) for i in range(HB)]
        outs = _prep_math([q_ref[:, l] for l in lanes_of], [k_ref[:, l] for l in lanes_of], [v_ref[:, l] for l in lanes_of],
                          [c[0] for c in cols], [c[1] for c in cols])
        for i in range(HB):
            lanes = lanes_of[i]
            u, w, qd, kd, qk, egl = outs[i]
            u_ref[:, lanes] = u
            w_ref[:, lanes] = w
            qd_ref[:, lanes] = qd
            kd_ref[:, lanes] = kd
            qk_ref[0, i] = qk
            egl_ref[0, i] = jnp.broadcast_to(egl, (8, d))

    blk = lambda off: pl.BlockSpec((CHUNK, HB * d), lambda n, j: (n, off // HB + j))
    cc = pl.BlockSpec((1, HB, CHUNK, CHUNK), lambda n, j: (n, j, 0, 0))
    ee = pl.BlockSpec((1, HB, 8, d), lambda n, j: (n, j, 0, 0))
    big = _sds((L, D_DN))
    return pl.pallas_call(
        body, name="dn_prep_fwd", grid=(N, H // HB),
        in_specs=[blk(0), blk(H), blk(2 * H), pl.BlockSpec((CHUNK, 128), lambda n, j: (n, 0))],
        out_specs=[blk(0), blk(0), blk(0), blk(0), cc, ee],
        out_shape=[big, big, big, big, _sds((N, H, CHUNK, CHUNK)), _sds((N, H, 8, d))],
        compiler_params=_cp(("parallel", "parallel")),
    )(qkv, qkv, qkv, gates)


def _dn_scan_fwd(u, w, qd, kd, qk, egl):
    L = u.shape[0]
    N, H, d, HB = L // CHUNK, DN_HEADS, DN_HEAD_DIM, DN_HB

    def body(u_ref, w_ref, qd_ref, kd_ref, qk_ref, egl_ref, o_ref, st_ref, s_ref):
        n, h0 = pl.program_id(0), pl.program_id(1) * HB

        @pl.when(n == 0)
        def _():
            for i in range(HB):
                s_ref[h0 + i] = jnp.zeros((d, d), F32)

        hs = range(HB)
        ln = [pl.ds(i * d, d) for i in hs]
        st = [s_ref[h0 + i] for i in hs]
        ws = [_bdot(w_ref[:, ln[i]], st[i], _NN) for i in hs]
        qs = [_bdot(qd_ref[:, ln[i]], st[i], _NN) for i in hs]
        vn = [u_ref[:, ln[i]] - ws[i] for i in hs]
        qv = [_bdot(qk_ref[0, i], vn[i], _NN) for i in hs]
        kv = [_bdot(kd_ref[:, ln[i]], vn[i], _TN) for i in hs]
        for i in hs:
            st_ref[0, i] = st[i]
            o_ref[:, ln[i]] = qs[i] + qv[i]
            s_ref[h0 + i] = st[i] * egl_ref[0, i, pl.ds(0, 1), :] + kv[i]

    blk = pl.BlockSpec((CHUNK, HB * d), lambda n, j: (n, j))
    cc = pl.BlockSpec((1, HB, CHUNK, CHUNK), lambda n, j: (n, j, 0, 0))
    ee = pl.BlockSpec((1, HB, 8, d), lambda n, j: (n, j, 0, 0))
    return pl.pallas_call(
        body, name="dn_scan_fwd", grid=(N, H // HB), in_specs=[blk, blk, blk, blk, cc, ee],
        out_specs=[blk, pl.BlockSpec((1, HB, d, d), lambda n, j: (n, j, 0, 0))],
        out_shape=[_sds((L, D_DN)), _sds((N, H, d, d))], scratch_shapes=[pltpu.VMEM((H, d, d), F32)],
        compiler_params=_cp(("arbitrary", "arbitrary")),
    )(u, w, qd, kd, qk, egl)


def _dn_scan_bwd(u, w, qd, kd, qk, egl, states, do):
    L = u.shape[0]
    N, H, d, HB = L // CHUNK, DN_HEADS, DN_HEAD_DIM, DN_HB

    def body(u_ref, w_ref, qd_ref, kd_ref, qk_ref, egl_ref, st_ref, do_ref,
             du_ref, dw_ref, dqd_ref, dkd_ref, dqk_ref, degl_ref, ds_ref):
        n, h0 = pl.program_id(0), pl.program_id(1) * HB

        @pl.when(n == 0)
        def _():
            for i in range(HB):
                ds_ref[h0 + i] = jnp.zeros((d, d), F32)

        hs = range(HB)
        ln = [pl.ds(i * d, d) for i in hs]
        st = [st_ref[0, i] for i in hs]
        dsn = [ds_ref[h0 + i] for i in hs]
        do_ = [do_ref[:, ln[i]] for i in hs]
        ws = [_bdot(w_ref[:, ln[i]], st[i], _NN) for i in hs]
        d1 = [_bdot(qk_ref[0, i], do_[i], _TN) for i in hs]
        d2 = [_bdot(kd_ref[:, ln[i]], dsn[i], _NN) for i in hs]
        dqd = [_bdot(do_[i], st[i], _NT) for i in hs]
        qdo = [_bdot(qd_ref[:, ln[i]], do_[i], _TN) for i in hs]
        vn = [u_ref[:, ln[i]] - ws[i] for i in hs]
        dvn = [d1[i] + d2[i] for i in hs]
        dw = [_bdot(dvn[i], st[i], _NT) for i in hs]
        dkd = [_bdot(vn[i], dsn[i], _NT) for i in hs]
        dqk = [_bdot(do_[i], vn[i], _NT) for i in hs]
        wdv = [_bdot(w_ref[:, ln[i]], dvn[i], _TN) for i in hs]
        for i in hs:
            du_ref[:, ln[i]] = dvn[i]
            dw_ref[:, ln[i]] = -dw[i]
            dqd_ref[:, ln[i]] = dqd[i]
            dkd_ref[:, ln[i]] = dkd[i]
            dqk_ref[0, i] = dqk[i]
            degl_ref[0, i] = jnp.broadcast_to(jnp.sum(dsn[i] * st[i], keepdims=True), (8, d))
            ds_ref[h0 + i] = (qdo[i] - wdv[i]) + dsn[i] * egl_ref[0, i, pl.ds(0, 1), :]

    blk = pl.BlockSpec((CHUNK, HB * d), lambda n, j: (N - 1 - n, j))
    cc = pl.BlockSpec((1, HB, CHUNK, CHUNK), lambda n, j: (N - 1 - n, j, 0, 0))
    ee = pl.BlockSpec((1, HB, 8, d), lambda n, j: (N - 1 - n, j, 0, 0))
    ss = pl.BlockSpec((1, HB, d, d), lambda n, j: (N - 1 - n, j, 0, 0))
    big = _sds((L, D_DN))
    return pl.pallas_call(
        body, name="dn_scan_bwd", grid=(N, H // HB), in_specs=[blk, blk, blk, blk, cc, ee, ss, blk],
        out_specs=[blk, blk, blk, blk, cc, ee],
        out_shape=[big, big, big, big, _sds((N, H, CHUNK, CHUNK)), _sds((N, H, 8, d))],
        scratch_shapes=[pltpu.VMEM((H, d, d), F32)], compiler_params=_cp(("arbitrary", "arbitrary")),
    )(u, w, qd, kd, qk, egl, states, do)


def _dn_prep_bwd(qkv, gates, du, dw, dqd, dkd, dqk, degl):
    L = qkv.shape[0]
    N, H, d, HB = L // CHUNK, DN_HEADS, DN_HEAD_DIM, DN_HB

    def body(q_ref, k_ref, v_ref, g_ref, du_ref, dw_ref, dqd_ref, dkd_ref, dqk_ref, degl_ref, dq_ref, dk_ref, dv_ref, dg_ref):
        j = pl.program_id(1)
        h0 = j * HB
        gates_ = g_ref[...]
        lane = lax.broadcasted_iota(jnp.int32, gates_.shape, 1)
        lane1 = lax.broadcasted_iota(jnp.int32, (1, d), 1)
        part = jnp.zeros(gates_.shape, F32)
        lanes_of = [pl.ds(i * d, d) for i in range(HB)]
        cols = [_gate_cols(gates_, h0 + i) for i in range(HB)]
        _, f = jax.vjp(_prep_math, [q_ref[:, l] for l in lanes_of], [k_ref[:, l] for l in lanes_of],
                       [v_ref[:, l] for l in lanes_of], [c[0] for c in cols], [c[1] for c in cols])
        cots = tuple((du_ref[:, l], dw_ref[:, l], dqd_ref[:, l], dkd_ref[:, l], dqk_ref[0, i],
                      jnp.where(lane1 == 0, degl_ref[0, i, pl.ds(0, 1), :], 0.0)) for i, l in enumerate(lanes_of))
        dqs, dks, dvs, dgcs, dbcs = f(cots)
        for i in range(HB):
            lanes = lanes_of[i]
            dq_ref[:, lanes] = dqs[i]
            dk_ref[:, lanes] = dks[i]
            dv_ref[:, lanes] = dvs[i]
            part = part + jnp.where(lane == h0 + i, dbcs[i], 0.0) + jnp.where(lane == h0 + i + DN_HEADS, dgcs[i], 0.0)

        @pl.when(j == 0)
        def _():
            dg_ref[...] = part

        @pl.when(j > 0)
        def _():
            dg_ref[...] += part

    blk = lambda off: pl.BlockSpec((CHUNK, HB * d), lambda n, j: (n, off // HB + j))
    gsp = pl.BlockSpec((CHUNK, 128), lambda n, j: (n, 0))
    cc = pl.BlockSpec((1, HB, CHUNK, CHUNK), lambda n, j: (n, j, 0, 0))
    ee = pl.BlockSpec((1, HB, 8, d), lambda n, j: (n, j, 0, 0))
    big = _sds((L, D_DN))
    return pl.pallas_call(
        body, name="dn_prep_bwd", grid=(N, H // HB),
        in_specs=[blk(0), blk(H), blk(2 * H), gsp, blk(0), blk(0), blk(0), blk(0), cc, ee],
        out_specs=[blk(0), blk(0), blk(0), gsp], out_shape=[big, big, big, _sds((L, 128))],
        compiler_params=_cp(("parallel", "arbitrary")),
    )(qkv, qkv, qkv, gates, du, dw, dqd, dkd, dqk, degl)


def _dn_post_fwd(o, proj, nw):
    L = o.shape[0]
    d = DN_HEAD_DIM
    tm = min(512, L)

    def body(o_ref, z_ref, w_ref, y_ref):
        ov = o_ref[...]
        r = lax.rsqrt(jnp.mean(ov * ov, axis=-1, keepdims=True) + EPS)
        y_ref[...] = (ov * r * w_ref[...] * _silu(z_ref[...])).astype(BF16)

    blk = pl.BlockSpec((tm, d), lambda i, h: (i, h))
    return pl.pallas_call(
        body, name="dn_post_fwd", grid=(L // tm, DN_HEADS),
        in_specs=[blk, pl.BlockSpec((tm, d), lambda i, h: (i, OFF_ZD // d + h)), pl.BlockSpec((1, d), lambda i, h: (0, 0))],
        out_specs=blk, out_shape=_sds((L, D_DN), BF16), compiler_params=_cp(("parallel", "parallel")),
    )(o, proj, nw)


def _dn_post_bwd(o, proj, nw, dy):
    L = o.shape[0]
    d = DN_HEAD_DIM
    tm = min(512, L)

    def body(o_ref, z_ref, w_ref, dy_ref, do_ref, dz_ref, dw_ref):
        first = jnp.logical_and(pl.program_id(0) == 0, pl.program_id(1) == 0)
        ov, z, w, dyv = o_ref[...], z_ref[...], w_ref[...], dy_ref[...]
        r = lax.rsqrt(jnp.mean(ov * ov, axis=-1, keepdims=True) + EPS)
        xn = ov * r
        dz_ref[...] = (dyv * xn * w * _dsilu(z)).astype(BF16)
        dn = dyv * _silu(z)
        t = dn * w
        do_ref[...] = r * t - ov * (r * r * r) * jnp.mean(t * ov, axis=-1, keepdims=True)
        part = jnp.sum(dn * xn, axis=0, keepdims=True)

        @pl.when(first)
        def _():
            dw_ref[...] = part

        @pl.when(jnp.logical_not(first))
        def _():
            dw_ref[...] += part

    blk = pl.BlockSpec((tm, d), lambda i, h: (i, h))
    one = pl.BlockSpec((1, d), lambda i, h: (0, 0))
    return pl.pallas_call(
        body, name="dn_post_bwd", grid=(L // tm, DN_HEADS),
        in_specs=[blk, pl.BlockSpec((tm, d), lambda i, h: (i, OFF_ZD // d + h)), one, blk], out_specs=[blk, blk, one],
        out_shape=[_sds((L, D_DN)), _sds((L, D_DN), BF16), _sds((1, d))], compiler_params=_cp(("arbitrary", "arbitrary")),
    )(o, proj, nw, dy)


def _mix_fwd(s5o, dno, w_su, w_du, proj):
    L, K = s5o.shape
    N = w_su.shape[1]
    tm, tn = min(512, L), 512

    def body(a1, a2, b1, b2, gs, gd, ys_ref, yd_ref, mx_ref):
        ys = jnp.dot(a1[...], b1[...], preferred_element_type=F32)
        yd = jnp.dot(a2[...], b2[...], preferred_element_type=F32)
        ys_ref[...] = ys
        yd_ref[...] = yd
        mx_ref[...] = (_sigmoid(gs[...]) * ys + _sigmoid(gd[...]) * yd).astype(BF16)

    a = pl.BlockSpec((tm, K), lambda i, j: (i, 0))
    b = pl.BlockSpec((K, tn), lambda i, j: (0, j))
    o = pl.BlockSpec((tm, tn), lambda i, j: (i, j))
    return pl.pallas_call(
        body, name="mix_fwd", grid=(L // tm, N // tn),
        in_specs=[a, a, b, b, pl.BlockSpec((tm, tn), lambda i, j: (i, OFF_GS // tn + j)),
                  pl.BlockSpec((tm, tn), lambda i, j: (i, OFF_GD // tn + j))],
        out_specs=[o, o, o], out_shape=[_sds((L, N)), _sds((L, N)), _sds((L, N), BF16)],
        compiler_params=_cp(("parallel", "parallel")),
    )(s5o, dno, w_su, w_du, proj, proj)


def _mix_bwd(dx2b, w_out, proj, ys, yd):
    L, K = dx2b.shape
    N = w_out.shape[0]
    tm, tn = min(512, L), 512

    def body(a, b, gs, gd, ys_ref, yd_ref, dgs_ref, dgd_ref, dys_ref, dyd_ref):
        dm = lax.dot_general(a[...], b[...], (((1,), (1,)), ((), ())), preferred_element_type=F32)
        ss, sd = _sigmoid(gs[...]), _sigmoid(gd[...])
        dys_ref[...] = (dm * ss).astype(BF16)
        dyd_ref[...] = (dm * sd).astype(BF16)
        dgs_ref[...] = (dm * ys_ref[...] * ss * (1.0 - ss)).astype(BF16)
        dgd_ref[...] = (dm * yd_ref[...] * sd * (1.0 - sd)).astype(BF16)

    o = pl.BlockSpec((tm, tn), lambda i, j: (i, j))
    return pl.pallas_call(
        body, name="mix_bwd", grid=(L // tm, N // tn),
        in_specs=[pl.BlockSpec((tm, K), lambda i, j: (i, 0)), pl.BlockSpec((tn, K), lambda i, j: (j, 0)),
                  pl.BlockSpec((tm, tn), lambda i, j: (i, OFF_GS // tn + j)),
                  pl.BlockSpec((tm, tn), lambda i, j: (i, OFF_GD // tn + j)), o, o],
        out_specs=[o, o, o, o], out_shape=[_sds((L, N), BF16)] * 4, compiler_params=_cp(("parallel", "parallel")),
    )(dx2b, w_out, proj, proj, ys, yd)


def _final(mixed, w_out, x, tgt, fw):
    L, D = x.shape
    tm = min(256, L)

    def body(a_ref, b_ref, x_ref, t_ref, w_ref, dx_ref, dxb_ref, loss_ref, dw_ref):
        i = pl.program_id(0)
        x2 = x_ref[...] + jnp.dot(a_ref[...], b_ref[...], preferred_element_type=F32)
        w = w_ref[...]
        r = lax.rsqrt(jnp.mean(x2 * x2, axis=-1, keepdims=True) + EPS)
        xn = x2 * r
        e = xn * w - t_ref[...]
        lpart = 0.5 * jnp.sum(jnp.mean(e * e, axis=-1, keepdims=True), axis=0, keepdims=True)
        dy = e * (1.0 / D)
        t = dy * w
        dx2 = r * t - x2 * (r * r * r) * jnp.mean(t * x2, axis=-1, keepdims=True)
        dx_ref[...] = dx2
        dxb_ref[...] = dx2.astype(BF16)
        dwp = jnp.sum(dy * xn, axis=0, keepdims=True)
        lrow = jnp.broadcast_to(lpart, loss_ref.shape)

        @pl.when(i == 0)
        def _():
            loss_ref[...] = lrow
            dw_ref[...] = dwp

        @pl.when(i > 0)
        def _():
            loss_ref[...] += lrow
            dw_ref[...] += dwp

    row = pl.BlockSpec((tm, D), lambda i: (i, 0))
    one = pl.BlockSpec((1, D), lambda i: (0, 0))
    return pl.pallas_call(
        body, name="final", grid=(L // tm,),
        in_specs=[row, pl.BlockSpec((D, D), lambda i: (0, 0)), row, row, one],
        out_specs=[row, row, pl.BlockSpec((1, 128), lambda i: (0, 0)), one],
        out_shape=[_sds((L, D)), _sds((L, D), BF16), _sds((1, 128)), _sds((1, D))], compiler_params=_cp(("arbitrary",)),
    )(mixed, w_out, x, tgt, fw)


def _block_diag(t):
    J, g, a, b = t.shape
    eye = jnp.eye(g, dtype=t.dtype)
    return (t[:, :, :, None, :] * eye[None, :, None, :, None]).reshape(J, g * a, g * b)


def _block_diag_take(m, g):
    J, ga, gb = m.shape
    a, b = ga // g, gb // g
    m5 = m.reshape(J, g, a, g, b)
    idx = jnp.arange(g)
    return m5[:, idx, :, idx, :].transpose(1, 0, 2, 3)


class _PlainOps:
    def __init__(self, w_rest):
        self.w_rest = w_rest

    def in_proj(self, h, wt_perm):
        return _mm(h, wt_perm, tb=True, name="in_proj", tm=1024, tn=1152), self.w_rest

    def rest_grads(self, d_w_glu, d_w_su, d_w_du, d_w_out):
        pass

    def d_w_in(self, h, dproj):
        return _mm(dproj, h, ta=True, name="d_w_in", tm=1152, tn=1024)

    def d_h(self, dproj, wt_perm, d_wt_perm):
        return _mm(dproj, wt_perm, name="d_h", tm=2048, tn=1024, tk=1152)


def _local_step(x, tgt, ln_w, w_perm, lam_re, lam_im, log_step, b_re, b_im, c_re, c_im, s5_d,
                conv_w, a_log, dt_bias, norm_w, fw, ops):
    G, P, gb = S5_GROUPS, S5_STATE, S5_GROUPS // S5_BLOCKS
    h, rstd = _ln_fwd(x, ln_w)
    proj, (w_glu, w_su, w_du, w_out) = ops.in_proj(h, w_perm)

    b_re2, b_im2 = b_re.reshape(G, P * S5_GROUP), b_im.reshape(G, P * S5_GROUP)
    ls2 = log_step.reshape(G, 1)
    abar_re, abar_im, bb_re, bb_im = _s5_param_fwd(lam_re, lam_im, ls2, b_re2, b_im2)

    def to_wb(bb):
        return _block_diag(bb.reshape(S5_BLOCKS, gb, P, S5_GROUP).transpose(0, 1, 3, 2)).astype(BF16)

    def to_cb(cc):
        return _block_diag(cc.reshape(S5_BLOCKS, gb, S5_GROUP, P).transpose(0, 1, 3, 2)).astype(BF16)

    wbr, wbi, cbr, cbi = to_wb(bb_re), to_wb(bb_im), to_cb(c_re), to_cb(c_im)
    a_re_row, a_im_row = abar_re.reshape(1, G * P), abar_im.reshape(1, G * P)
    yc = _s5_core_fwd(proj, wbr, wbi, a_re_row, a_im_row, cbr, cbi)
    s5o = _s5_post_fwd(yc, proj, s5_d, w_glu)

    pad = lambda v: jnp.pad(v, ((0, 0), (DN_HEADS, 128 - 2 * DN_HEADS)))
    alog_row, dtb_row = pad(a_log), pad(dt_bias)
    qkv = _dn_conv_fwd(proj, conv_w)
    gates = _dn_gates_fwd(proj, alog_row, dtb_row)
    prep = _dn_prep_fwd(qkv, gates)
    o_dn, states = _dn_scan_fwd(*prep)
    dno = _dn_post_fwd(o_dn, proj, norm_w)

    ys, yd, mixed = _mix_fwd(s5o, dno, w_su, w_du, proj)
    dx2, dx2b, loss_row, d_fw = _final(mixed, w_out, x, tgt, fw)
    d_w_out = _mm(mixed, dx2b, ta=True, name="d_w_out")
    dgs, dgd, dys, dyd = _mix_bwd(dx2b, w_out, proj, ys, yd)
    d_w_su = _mm(s5o, dys, ta=True, name="d_w_su", shard_out=True)
    d_w_du = _mm(dno, dyd, ta=True, name="d_w_du", shard_out=True)
    ds5o = _mm(dys, w_su, tb=True, name="d_s5o")
    ddno = _mm(dyd, w_du, tb=True, name="d_dno")

    dyc, du1, dz_s, d_s5d, d_w_glu = _s5_post_bwd(yc, proj, s5_d, w_glu, ds5o)
    ops.rest_grads(d_w_glu, d_w_su, d_w_du, d_w_out)
    du, dwbr, dwbi, dcbr, dcbi, dar, dai = _s5_core_bwd(proj, wbr, wbi, a_re_row, a_im_row, cbr, cbi, dyc, du1)

    def from_wb(dwb):
        return _block_diag_take(dwb, gb).transpose(0, 1, 3, 2).reshape(G, P * S5_GROUP)

    def from_cb(dcb):
        return _block_diag_take(dcb, gb).transpose(0, 1, 3, 2).reshape(G, S5_GROUP, P)

    d_lam_re, d_lam_im, d_ls, d_b_re, d_b_im = _s5_param_bwd(
        lam_re, lam_im, ls2, b_re2, b_im2, dar.reshape(G, P), dai.reshape(G, P), from_wb(dwbr), from_wb(dwbi))

    do_dn, dz_d, d_norm_w = _dn_post_bwd(o_dn, proj, norm_w, ddno)
    dq, dk, dv, dgates = _dn_prep_bwd(qkv, gates, *_dn_scan_bwd(*prep, states, do_dn))
    dqkv, d_conv = _dn_conv_bwd(proj, conv_w, jnp.concatenate([dq, dk, dv], axis=1))
    dpb, d_alog_row, d_dtb_row = _dn_gates_bwd(proj, alog_row, dtb_row, dgates)

    dproj = jnp.concatenate([du, dz_s, dqkv, dz_d, dgs, dgd, dpb], axis=1)
    d_w_perm = ops.d_w_in(h, dproj)
    dh = ops.d_h(dproj, w_perm, d_w_perm)
    grad_x, d_ln_w = _ln_bwd(x, rstd, ln_w, dh, dx2)

    grads = dict(
        ln_w=d_ln_w, w_perm=d_w_perm, s5_lam_re=d_lam_re, s5_lam_im=d_lam_im, s5_log_step=d_ls.reshape(1, G),
        s5_b_re=d_b_re.reshape(G, P, S5_GROUP), s5_b_im=d_b_im.reshape(G, P, S5_GROUP),
        s5_c_re=from_cb(dcbr), s5_c_im=from_cb(dcbi), s5_d=d_s5d, s5_w_glu=d_w_glu, s5_w_up=d_w_su,
        dn_conv_w=d_conv, dn_a_log=d_alog_row[:, DN_HEADS:2 * DN_HEADS], dn_dt_bias=d_dtb_row[:, DN_HEADS:2 * DN_HEADS],
        dn_norm_w=d_norm_w, dn_w_up=d_w_du, w_out=d_w_out, final_norm_w=d_fw)
    return loss_row, grad_x, grads


def _place():
    x, y, c = lax.axis_index("x"), lax.axis_index("y"), lax.axis_index("c")
    return x, y, c


def _remote(src, dst, send_sem, recv_sem, to):
    return pltpu.make_async_remote_copy(src_ref=src, dst_ref=dst, send_sem=send_sem, recv_sem=recv_sem,
                                        device_id=to, device_id_type=MESH)


def _gather_exchange(shards, whole=(), by_columns=False):
    na, nw = len(shards), len(whole)

    def half_of(ref, a, half):
        rows, cols = shards[a].shape
        if by_columns:
            return ref.at[pl.ds(0, rows), pl.ds(half * (cols // 2), cols // 2)]
        return ref.at[pl.ds(half * (rows // 2), rows // 2)]

    def plan(ins, outs, send_sems, recv_sems, local_sems, receiving):
        x, y, c = _place()
        me = 2 * x + y
        sibling = (x, y, 1 - c)
        chips = [(1 - x, y), (x, 1 - y), (1 - x, 1 - y)]

        def part(a, chip, half):
            return half_of(outs[a].at[chip], a, half)

        own = [pltpu.make_async_copy(ins[a], outs[a].at[me], local_sems.at[a]) for a in range(na + nw)]
        sends, landed, passed, arrivals = [], [], [], []
        for a in range(na):
            for j, (px, py) in enumerate(chips):
                k = 6 * a + j
                sends.append(_remote(half_of(ins[a], a, c), part(a, me, c), send_sems.at[k], recv_sems.at[k], (px, py, c)))
                if receiving:
                    got, other = part(a, 2 * px + py, c), part(a, 2 * px + py, 1 - c)
                    landed.append(_remote(got, got, send_sems.at[k], recv_sems.at[k], (px, py, c)))
                    passed.append(_remote(got, got, send_sems.at[k + 3], recv_sems.at[k + 3], sibling))
                    arrivals.append(_remote(other, other, send_sems.at[k + 3], recv_sems.at[k + 3], sibling))
        for a in range(na, na + nw):
            for j, (px, py) in enumerate(chips):
                k = 6 * na + 3 * (a - na) + j
                sends.append(_remote(ins[a], outs[a].at[me], send_sems.at[k], recv_sems.at[k], (px, py, c)))
                if receiving:
                    arrivals.append(_remote(ins[a], outs[a].at[2 * px + py], send_sems.at[k], recv_sems.at[k], (px, py, c)))
        return own, sends, landed, passed, arrivals

    def start(ins, outs, *sems):
        own, sends, _, _, _ = plan(ins, outs, *sems, False)
        for cp in own + sends:
            cp.start()

    def finish(ins, outs, *sems):
        own, sends, landed, passed, arrivals = plan(ins, outs, *sems, True)
        for got, fwd in zip(landed, passed):
            got.wait_recv()
            fwd.start()
        for cp in arrivals:
            cp.wait_recv()
        for cp in sends + passed:
            cp.wait_send()
        for cp in own:
            cp.wait()

    arrays = list(shards) + list(whole)
    return _Exchange(arrays, [_sds((N_CHIPS,) + s.shape, s.dtype) for s in arrays], 6 * na + 3 * nw, start, finish)


def _owners_exchange(csbs):
    na = len(csbs)

    def plan(ins, outs, send_sems, recv_sems, local_sems, receiving):
        x, y, c = _place()
        me = 2 * x + y
        sends, arrivals = [], []
        for a in range(na):
            for k in range(N_CHIPS - 1):
                j = (me + 1 + k) % N_CHIPS
                sends.append(_remote(ins[a].at[k], outs[a].at[2 - k], send_sems.at[3 * a + k], recv_sems.at[3 * a + 2 - k],
                                     (j // 2, j % 2, c)))
                if receiving:
                    arrivals.append(_remote(ins[a].at[k], outs[a].at[k], send_sems.at[3 * a + k], recv_sems.at[3 * a + k], (x, y, c)))
        return sends, arrivals

    def start(ins, outs, *sems):
        for cp in plan(ins, outs, *sems, False)[0]:
            cp.start()

    def finish(ins, outs, *sems):
        sends, arrivals = plan(ins, outs, *sems, True)
        for cp in arrivals:
            cp.wait_recv()
        for cp in sends:
            cp.wait_send()

    return _Exchange(csbs, [_sds(g.shape, g.dtype) for g in csbs], 3 * na, start, finish)


def _run_exchange(ex, name):
    n_in, n_out = len(ex.ins), len(ex.out_shapes)

    def body(*refs):
        ins, outs, sems = refs[:n_in], refs[n_in:n_in + n_out], refs[n_in + n_out:]
        ex.start(ins, outs, *sems)
        ex.finish(ins, outs, *sems)

    return pl.pallas_call(
        body, name=name, in_specs=[ANY] * n_in, out_specs=[ANY] * n_out, out_shape=ex.out_shapes,
        scratch_shapes=[pltpu.SemaphoreType.DMA((ex.n_sems,)) for _ in range(3)],
    )(*ex.ins)


def _swap_halves(gxs, name):
    na = len(gxs)

    def body(*refs):
        ins, outs = refs[:na], refs[na:2 * na]
        send_sems, recv_sems = refs[2 * na:]
        x, y, c = _place()
        cps = [_remote(ins[a].at[pl.ds(0, N_CHIPS), pl.ds(1 - c, 1)], outs[a], send_sems.at[a], recv_sems.at[a], (x, y, 1 - c))
               for a in range(na)]
        for cp in cps:
            cp.start()
        for cp in cps:
            cp.wait()

    return pl.pallas_call(
        body, name=name, in_specs=[ANY] * na, out_specs=[ANY] * na,
        out_shape=[_sds((N_CHIPS, 1) + g.shape[2:], g.dtype) for g in gxs],
        scratch_shapes=[pltpu.SemaphoreType.DMA((na,)), pltpu.SemaphoreType.DMA((na,))],
    )(*gxs)


def _share_halves(gfs):
    na = len(gfs)

    def body(*refs):
        ins, outs = refs[:na], refs[na:2 * na]
        send_sems, recv_sems = refs[2 * na:]
        x, y, c = _place()
        cps = [_remote(ins[a].at[pl.ds(c, 1)], outs[a].at[pl.ds(c, 1)], send_sems.at[a], recv_sems.at[a], (x, y, 1 - c))
               for a in range(na)]
        for cp in cps:
            cp.start()
        for a in range(na):
            cps[a].wait_send()
            _remote(ins[a].at[pl.ds(1 - c, 1)], outs[a].at[pl.ds(1 - c, 1)], send_sems.at[a], recv_sems.at[a], (x, y, 1 - c)).wait_recv()

    return pl.pallas_call(
        body, name="rs_share_halves", in_specs=[ANY] * na, out_specs=[ANY] * na,
        out_shape=[_sds(g.shape, g.dtype) for g in gfs], input_output_aliases={a: a for a in range(na)},
        scratch_shapes=[pltpu.SemaphoreType.DMA((na,)), pltpu.SemaphoreType.DMA((na,))],
    )(*gfs)


def _row_tile(rows, cols, budget=5 << 18):
    fits = [t for t in range(16, rows + 1, 16) if rows % t == 0 and t * cols * 4 <= budget]
    return max(fits) if fits else rows


def _chip_sums(gx, r1, where):
    _, _, r2, cd = gx.shape
    tr = _row_tile(r2, cd)

    def body(w_ref, a_ref, b_ref, o_ref):
        o_ref[...] = (a_ref[0] + b_ref[0]).astype(BF16)

    other = lambda k, i, w: ((w[1] + 1 + k) % N_CHIPS, w[0], i, 0)
    other0 = lambda k, i, w: ((w[1] + 1 + k) % N_CHIPS, 0, i, 0)
    return pl.pallas_call(
        body, name="rs_chip_sums",
        grid_spec=pltpu.PrefetchScalarGridSpec(
            num_scalar_prefetch=1, grid=(N_CHIPS - 1, r2 // tr),
            in_specs=[pl.BlockSpec((1, 1, tr, cd), other), pl.BlockSpec((1, 1, tr, cd), other0)],
            out_specs=pl.BlockSpec((1, tr, cd), lambda k, i, w: (k, i, 0))),
        out_shape=_sds((N_CHIPS - 1, r2, cd), BF16), compiler_params=_cp(("parallel", "parallel")),
    )(where, gx, r1)


def _owner_sum(gx, r1, r2x, where):
    _, _, r2, cd = gx.shape
    tr = _row_tile(r2, cd)

    def body(w_ref, a_ref, b_ref, r_ref, o_ref):
        acc = a_ref[0, 0] + b_ref[0, 0]
        for k in range(N_CHIPS - 1):
            acc = acc + r_ref[k].astype(F32)
        o_ref[0] = acc

    return pl.pallas_call(
        body, name="rs_owner_sum",
        grid_spec=pltpu.PrefetchScalarGridSpec(
            num_scalar_prefetch=1, grid=(r2 // tr,),
            in_specs=[pl.BlockSpec((1, 1, tr, cd), lambda i, w: (w[1], w[0], i, 0)),
                      pl.BlockSpec((1, 1, tr, cd), lambda i, w: (w[1], 0, i, 0)),
                      pl.BlockSpec((N_CHIPS - 1, tr, cd), lambda i, w: (0, i, 0))],
            out_specs=pl.BlockSpec((1, tr, cd), lambda i, w: (w[0], i, 0))),
        out_shape=_sds((2, r2, cd)), compiler_params=_cp(("parallel",)),
    )(where, gx, r1, r2x)


def _adamw_math(w, g, m, v):
    m = ADAM_B1 * m + (1.0 - ADAM_B1) * g
    v = ADAM_B2 * v + (1.0 - ADAM_B2) * (g * g)
    m_hat = m / (1.0 - ADAM_B1 ** ADAM_STEP)
    v_hat = v / (1.0 - ADAM_B2 ** ADAM_STEP)
    delta = -ADAM_LR * (m_hat / (jnp.sqrt(v_hat) + ADAM_EPS) + ADAM_WD * w)
    return delta, m, v


def _adamw(w, g, m, v, name):
    rows, cd = w.shape
    if rows % 16 == 0:
        tr, tc = _row_tile(rows, cd, budget=3 << 19), cd
    else:
        tr, tc = rows, (128 if rows * cd * 4 > (3 << 19) else cd)
    assert rows % tr == 0 and cd % tc == 0

    def body(w_ref, g_ref, m_ref, v_ref, d_ref, mo_ref, vo_ref):
        d, mm, vv = _adamw_math(w_ref[...], g_ref[...], m_ref[...], v_ref[...])
        d_ref[...] = d
        mo_ref[...] = mm
        vo_ref[...] = vv

    blk = pl.BlockSpec((tr, tc), lambda i, j: (i, j))
    return pl.pallas_call(
        body, name=name, grid=(rows // tr, cd // tc), in_specs=[blk] * 4, out_specs=[blk] * 3, out_shape=[_sds(w.shape)] * 3,
        compiler_params=_cp(("parallel", "parallel")),
    )(w, g, m, v)


def _small_allreduce(gp):
    R = gp.shape[0]
    R2 = R // 2
    assert R2 % 8 == 0

    def body(g_ref, go_ref, sib, csum, land, send_sems, recv_sems):
        x, y, c = _place()
        me = 2 * x + y
        sibling = (x, y, 1 - c)
        chips = [(1 - x, y), (x, 1 - y), (1 - x, 1 - y)]
        swap = _remote(g_ref, sib, send_sems.at[0], recv_sems.at[0], sibling)
        swap.start()
        swap.wait()
        csum[...] = g_ref[...] + sib[...]
        half = csum.at[pl.ds(c * R2, R2)]
        land[me] = csum[pl.ds(c * R2, R2), :]
        cps = [_remote(half, land.at[me], send_sems.at[1 + j], recv_sems.at[1 + j], (px, py, c))
               for j, (px, py) in enumerate(chips)]
        for cp in cps:
            cp.start()
        for j, (px, py) in enumerate(chips):
            _remote(half, land.at[2 * px + py], send_sems.at[1 + j], recv_sems.at[1 + j], (px, py, c)).wait_recv()
        for cp in cps:
            cp.wait_send()
        mine = go_ref.at[pl.ds(c * R2, R2)]
        go_ref[pl.ds(c * R2, R2), :] = (land[0] + land[1]) + (land[2] + land[3])
        share = _remote(mine, mine, send_sems.at[4], recv_sems.at[4], sibling)
        share.start()
        share.wait_send()
        other = go_ref.at[pl.ds((1 - c) * R2, R2)]
        _remote(other, other, send_sems.at[4], recv_sems.at[4], sibling).wait_recv()

    vm = pl.BlockSpec(memory_space=pltpu.VMEM)
    return pl.pallas_call(
        body, name="small_allreduce", in_specs=[vm], out_specs=vm, out_shape=_sds((R, 128)),
        scratch_shapes=[pltpu.VMEM((R, 128), F32), pltpu.VMEM((R, 128), F32), pltpu.VMEM((N_CHIPS, R2, 128), F32),
                        pltpu.SemaphoreType.DMA((5,)), pltpu.SemaphoreType.DMA((5,))],
        compiler_params=_cp(),
    )(gp)


def _adamw_many(ws, gs, ms, vs):
    n = len(ws)

    def body(*refs):
        w_r, g_r, m_r, v_r = refs[:n], refs[n:2 * n], refs[2 * n:3 * n], refs[3 * n:4 * n]
        d_r, mo_r, vo_r = refs[4 * n:5 * n], refs[5 * n:6 * n], refs[6 * n:]
        for i in range(n):
            d_r[i][...], mo_r[i][...], vo_r[i][...] = _adamw_math(w_r[i][...], g_r[i][...], m_r[i][...], v_r[i][...])

    vm = pl.BlockSpec(memory_space=pltpu.VMEM)
    shapes = [_sds(a.shape) for a in ws]
    outs = pl.pallas_call(
        body, name="adamw_small", in_specs=[vm] * (4 * n), out_specs=[vm] * (3 * n), out_shape=shapes * 3, compiler_params=_cp(),
    )(*ws, *gs, *ms, *vs)
    return outs[:n], outs[n:2 * n], outs[2 * n:]


def _pack(arrs):
    rows = []
    for a in arrs:
        f = a.reshape(-1)
        f = jnp.pad(f, (0, (-f.shape[0]) % 128))
        rows.append(f.reshape(-1, 128))
    p = jnp.concatenate(rows, axis=0)
    return jnp.pad(p, ((0, (-p.shape[0]) % 8), (0, 0)))


def _unpack(p, shapes):
    out, r = [], 0
    for s in shapes:
        n = math.prod(s)
        nr = -(-n // 128)
        out.append(p[r:r + nr].reshape(-1)[:n].reshape(s))
        r += nr
    return out


class _ExchangeOps(_PlainOps):
    def __init__(self, rest_shards, where):
        self.rest_shards, self.where = rest_shards, where
        self.reduced = []

    def in_proj(self, h, wt_perm):
        proj, g_glu, g_su, g_du, g_out = _mm(h, wt_perm, tb=True, name="in_proj", tm=1024, tn=1152,
                                             exchange=_gather_exchange(self.rest_shards))
        cat = lambda g: jnp.concatenate([g[j] for j in range(N_CHIPS)], axis=1)
        return proj, (g_glu.reshape(D_S5, D_S5), cat(g_su), cat(g_du), g_out.reshape(D_MODEL, D_MODEL))

    def _chip_sums(self, gxs, name):
        r1s = _swap_halves(gxs, name)
        return r1s, [_chip_sums(gx, r1, self.where) for gx, r1 in zip(gxs, r1s)]

    def rest_grads(self, d_w_glu, d_w_su, d_w_du, d_w_out):
        gxs = [d_w_glu.reshape(N_CHIPS, 2, D_S5 // 8, D_S5), d_w_su.reshape(N_CHIPS, 2, D_S5 // 2, D_MODEL // N_CHIPS),
               d_w_du.reshape(N_CHIPS, 2, D_DN // 2, D_MODEL // N_CHIPS), d_w_out.reshape(N_CHIPS, 2, D_MODEL // 8, D_MODEL)]
        r1s, csbs = self._chip_sums(gxs, "rs_swap_rest")
        self.rest = (gxs, r1s, csbs)

    def d_w_in(self, h, dproj):
        gxs, r1s, csbs = self.rest
        d_wt_perm, *r2s = _mm(dproj, h, ta=True, name="d_w_in", tm=1152, tn=1024, exchange=_owners_exchange(csbs))
        self.reduced = list(zip(gxs, r1s, r2s))
        return d_wt_perm

    def d_h(self, dproj, wt_perm, d_wt_perm):
        gx = _wt_windows(d_wt_perm).reshape(N_CHIPS, 2, WT_ROWS // 2, D_MODEL)
        self.beta_a = d_wt_perm[OFF_B:OFF_B + WT_NB]
        (r1,), (csb,) = self._chip_sums([gx], "rs_swap_w_in")
        dh, r2 = _mm(dproj, wt_perm, name="d_h", tm=2048, tn=1024, tk=1152, exchange=_owners_exchange([csb]))
        self.reduced = [(gx, r1, r2)] + self.reduced
        return dh


WT_SHARD = D_IN // N_CHIPS
WT_NB = 2 * DN_HEADS
WT_B, WT_LO = divmod(OFF_GS, WT_SHARD)
WT_FIRST = [i * WT_SHARD - (WT_NB if i > WT_B else 0) for i in range(N_CHIPS)]
WT_WIN = [16 * (r // 16) for r in WT_FIRST]
WT_SHIFT = [r - s for r, s in zip(WT_FIRST, WT_WIN)]
WT_ROWS = 2592
assert (WT_LO + WT_SHIFT[WT_B]) % 16 == 0 and max(WT_SHIFT) + WT_SHARD <= WT_ROWS and WT_WIN[-1] + WT_ROWS <= D_IN_PAD


def _wt_to_window(shard, chip):
    d = jnp.asarray(WT_SHIFT, jnp.int32)[chip]
    gap = jnp.where(chip == WT_B, 0, WT_NB)
    win = jnp.zeros((WT_ROWS, shard.shape[1]), shard.dtype)
    win = lax.dynamic_update_slice(win, shard[:WT_LO], (d, 0))
    win = lax.dynamic_update_slice(win, shard[WT_LO:WT_LO + WT_NB], (d + WT_LO, 0))
    win = lax.dynamic_update_slice(win, shard[WT_LO + WT_NB:], (d + WT_LO + gap, 0))
    return win, shard[WT_LO:WT_LO + WT_NB]


def _wt_from_window(win, beta_a, chip):
    d = jnp.asarray(WT_SHIFT, jnp.int32)[chip]
    gap = jnp.where(chip == WT_B, 0, WT_NB)
    cols = win.shape[1]
    head = lax.dynamic_slice(win, (d, 0), (WT_LO, cols))
    mid = jnp.where(chip == WT_B, beta_a, lax.dynamic_slice(win, (d + WT_LO, 0), (WT_NB, cols)))
    tail = lax.dynamic_slice(win, (d + WT_LO + gap, 0), (WT_SHARD - WT_LO - WT_NB, cols))
    return jnp.concatenate([head, mid, tail], axis=0)


def _wt_regroup(wins, beta_a):
    parts, at = [], 0
    for i in range(N_CHIPS):
        end = WT_WIN[i + 1] if i + 1 < N_CHIPS else OFF_B
        lo = at - WT_WIN[i]
        over = WT_WIN[i] + WT_ROWS - end if i + 1 < N_CHIPS else 0
        parts.append(wins[i, lo:end - WT_WIN[i]])
        if over:
            parts.append(wins[i, end - WT_WIN[i]:] + wins[i + 1, :over])
        at = end + over
    pad = jnp.zeros((D_IN_PAD - OFF_B - WT_NB, wins.shape[2]), wins.dtype)
    return jnp.concatenate(parts + [beta_a, pad], axis=0)


def _wt_windows(regrouped):
    return jnp.stack([regrouped[s:s + WT_ROWS] for s in WT_WIN])


_SMALL = ("ln_w", "s5_lam_re", "s5_lam_im", "s5_log_step", "s5_b_re", "s5_b_im", "s5_c_re", "s5_c_im", "s5_d",
          "dn_a_log", "dn_dt_bias", "dn_norm_w", "final_norm_w")
_BIG = ("w_in", "s5_w_glu", "s5_w_up", "dn_w_up", "w_out")
_ORDER = ("ln_w", "w_in", "s5_lam_re", "s5_lam_im", "s5_log_step", "s5_b_re", "s5_b_im", "s5_c_re", "s5_c_im", "s5_d",
          "s5_w_glu", "s5_w_up", "dn_conv_w", "dn_a_log", "dn_dt_bias", "dn_norm_w", "dn_w_up", "w_out", "final_norm_w")


def kernel(x, ln_w, w_in, s5_lam_re, s5_lam_im, s5_log_step, s5_b_re, s5_b_im, s5_c_re, s5_c_im, s5_d, s5_w_glu, s5_w_up, dn_conv_w, dn_a_log, dn_dt_bias, dn_norm_w, dn_w_up, w_out, final_norm_w, loss_target, m_ln_w, m_w_in, m_s5_lam_re, m_s5_lam_im, m_s5_log_step, m_s5_b_re, m_s5_b_im, m_s5_c_re, m_s5_c_im, m_s5_d, m_s5_w_glu, m_s5_w_up, m_dn_conv_w, m_dn_a_log, m_dn_dt_bias, m_dn_norm_w, m_dn_w_up, m_w_out, m_final_norm_w, v_ln_w, v_w_in, v_s5_lam_re, v_s5_lam_im, v_s5_log_step, v_s5_b_re, v_s5_b_im, v_s5_c_re, v_s5_c_im, v_s5_d, v_s5_w_glu, v_s5_w_up, v_dn_conv_w, v_dn_a_log, v_dn_dt_bias, v_dn_norm_w, v_dn_w_up, v_w_out, v_final_norm_w):
    w = dict(ln_w=ln_w, w_in=w_in, s5_lam_re=s5_lam_re, s5_lam_im=s5_lam_im, s5_log_step=s5_log_step, s5_b_re=s5_b_re,
             s5_b_im=s5_b_im, s5_c_re=s5_c_re, s5_c_im=s5_c_im, s5_d=s5_d, s5_w_glu=s5_w_glu, s5_w_up=s5_w_up,
             dn_conv_w=dn_conv_w, dn_a_log=dn_a_log, dn_dt_bias=dn_dt_bias, dn_norm_w=dn_norm_w, dn_w_up=dn_w_up, w_out=w_out,
             final_norm_w=final_norm_w)
    m = dict(ln_w=m_ln_w, w_in=m_w_in, s5_lam_re=m_s5_lam_re, s5_lam_im=m_s5_lam_im, s5_log_step=m_s5_log_step,
             s5_b_re=m_s5_b_re, s5_b_im=m_s5_b_im, s5_c_re=m_s5_c_re, s5_c_im=m_s5_c_im, s5_d=m_s5_d, s5_w_glu=m_s5_w_glu,
             s5_w_up=m_s5_w_up, dn_conv_w=m_dn_conv_w, dn_a_log=m_dn_a_log, dn_dt_bias=m_dn_dt_bias, dn_norm_w=m_dn_norm_w,
             dn_w_up=m_dn_w_up, w_out=m_w_out, final_norm_w=m_final_norm_w)
    v = dict(ln_w=v_ln_w, w_in=v_w_in, s5_lam_re=v_s5_lam_re, s5_lam_im=v_s5_lam_im, s5_log_step=v_s5_log_step,
             s5_b_re=v_s5_b_re, s5_b_im=v_s5_b_im, s5_c_re=v_s5_c_re, s5_c_im=v_s5_c_im, s5_d=v_s5_d, s5_w_glu=v_s5_w_glu,
             s5_w_up=v_s5_w_up, dn_conv_w=v_dn_conv_w, dn_a_log=v_dn_a_log, dn_dt_bias=v_dn_dt_bias, dn_norm_w=v_dn_norm_w,
             dn_w_up=v_dn_w_up, w_out=v_w_out, final_norm_w=v_final_norm_w)
    xi, yi, ci = _place()
    chip = 2 * xi + yi
    where = jnp.stack([ci, chip]).astype(jnp.int32)

    tr = lambda a: jnp.swapaxes(a[0], 0, 1)
    win, beta_a = _wt_to_window(tr(w_in).astype(BF16), chip)
    g_win, g_ba, g_conv = _run_exchange(_gather_exchange([win], [beta_a, dn_conv_w[0]], by_columns=True), "gather_w_in")
    cat = lambda g: jnp.concatenate([g[j] for j in range(N_CHIPS)], axis=1)
    w_perm = _wt_regroup(g_win, g_ba[WT_B])

    ops = _ExchangeOps([w[n][0].astype(BF16) for n in _BIG[1:]], where)
    loss_row, grad_x, g = _local_step(
        x[0], loss_target[0], ln_w, w_perm, s5_lam_re[0], s5_lam_im[0], s5_log_step, s5_b_re[0], s5_b_im[0], s5_c_re[0],
        s5_c_im[0], s5_d, cat(g_conv), dn_a_log, dn_dt_bias, dn_norm_w, final_norm_w[None], ops)
    loss = lax.psum(loss_row[0, 0], ("x", "y", "c"))

    gfs = [_owner_sum(gx, r1, r2x, where) for gx, r1, r2x in ops.reduced]
    gfs = _share_halves(gfs)
    grads, deltas, new_m, new_v = {}, {}, {}, {}
    for n, gf in zip(_BIG[1:], gfs[1:]):
        shp = w[n].shape
        g2 = gf.reshape(shp[1:])
        d_, m_, v_ = _adamw(w[n][0], g2, m[n][0], v[n][0], "adamw_" + n)
        grads[n], deltas[n], new_m[n], new_v[n] = g2.reshape(shp), d_.reshape(shp), m_.reshape(shp), v_.reshape(shp)

    go = _small_allreduce(_pack([g[n] for n in _SMALL] + [g["dn_conv_w"], ops.beta_a]))
    lanes = {"s5_b_re": (S5_GROUPS, S5_STATE * S5_GROUP), "s5_b_im": (S5_GROUPS, S5_STATE * S5_GROUP)}
    flat = [lanes.get(n, (math.prod(w[n].shape[:-1]), w[n].shape[-1])) for n in _SMALL]
    *gs, g_conv, g_beta_a = _unpack(go, flat + [(CONV_K, 3 * D_DN), (WT_NB, D_MODEL)])
    gt = _wt_from_window(gfs[0].reshape(WT_ROWS, D_MODEL), g_beta_a, chip)
    d_, m_, v_ = _adamw(tr(w_in), gt, tr(m_w_in), tr(v_w_in), "adamw_w_in")
    grads["w_in"], deltas["w_in"], new_m["w_in"], new_v["w_in"] = (jnp.swapaxes(a, 0, 1)[None] for a in (gt, d_, m_, v_))
    as2d = lambda t: [t[n].reshape(s) for n, s in zip(_SMALL, flat)]
    for dst, src in zip((grads, deltas, new_m, new_v), (gs, *_adamw_many(as2d(w), gs, as2d(m), as2d(v)))):
        for n, a in zip(_SMALL, src):
            dst[n] = a.reshape(w[n].shape)
    cc = 3 * D_DN // N_CHIPS
    g_conv_mine = lax.dynamic_slice(g_conv, (0, chip * cc), (CONV_K, cc))
    d_, m_, v_ = _adamw(dn_conv_w[0], g_conv_mine, m_dn_conv_w[0], v_dn_conv_w[0], "adamw_dn_conv_w")
    grads["dn_conv_w"], deltas["dn_conv_w"], new_m["dn_conv_w"], new_v["dn_conv_w"] = (
        g_conv_mine[None], d_[None], m_[None], v_[None])

    return (loss, grad_x[None], *[grads[n] for n in _ORDER], *[deltas[n] for n in _ORDER], *[new_m[n] for n in _ORDER],
            *[new_v[n] for n in _ORDER])
```

```python
import functools
import math

import jax
import jax.numpy as jnp
from jax import lax
from jax.experimental import pallas as pl
from jax.experimental.pallas import tpu as pltpu

F32 = jnp.float32
BF16 = jnp.bfloat16
HI = lax.Precision.HIGHEST
MESH = pl.DeviceIdType.MESH
ANY = pl.BlockSpec(memory_space=pl.ANY)

EPS = 1e-6
D_MODEL = 2048
D_S5 = 1024
S5_GROUP = 16
S5_GROUPS = 64
S5_STATE = 64
S5_BLOCKS = 8
S5_SEG = 8
DN_HEADS = 8
DN_HEAD_DIM = 128
D_DN = 1024
CONV_K = 4
CHUNK = 64
D_IN = 10256
D_IN_PAD = 10368
OFF_US, OFF_ZS, OFF_Q, OFF_K, OFF_V, OFF_ZD, OFF_GS, OFF_GD, OFF_B = 0, 1024, 2048, 3072, 4096, 5120, 6144, 8192, 10240
N_CHIPS = 4
N_DEV = 8
VMEM_LIMIT = 56 * 1024 * 1024

ADAM_LR = 0.001
ADAM_B1 = 0.9
ADAM_B2 = 0.999
ADAM_EPS = 1e-08
ADAM_WD = 0.01
ADAM_STEP = 10


def _cp(sem=None):
    return pltpu.CompilerParams(dimension_semantics=sem, vmem_limit_bytes=VMEM_LIMIT)


def _sds(shape, dtype=F32):
    return jax.ShapeDtypeStruct(tuple(shape), dtype)


def _sigmoid(x):
    return 1.0 / (1.0 + jnp.exp(-x))


def _silu(x):
    return x * _sigmoid(x)


def _dsilu(x):
    s = _sigmoid(x)
    return s * (1.0 + x * (1.0 - s))


class _Exchange:
    def __init__(self, ins, out_shapes, n_sems, start, finish):
        self.ins, self.out_shapes, self.n_sems, self.start, self.finish = list(ins), list(out_shapes), n_sems, start, finish


def _mm(a, b, *, name, ta=False, tb=False, out_dtype=F32, tm=512, tn=512, tk=2048, shard_out=False, exchange=None):
    if ta:
        K, M = a.shape
    else:
        M, K = a.shape
    if tb:
        N, K2 = b.shape
    else:
        K2, N = b.shape
    assert K == K2, (a.shape, b.shape)
    tm, tn, tk = min(tm, M), min(tn, N), min(tk, K)
    assert M % tm == 0 and N % tn == 0 and K % tk == 0, (M, N, K, tm, tn, tk)
    nk = K // tk
    dims = (((0 if ta else 1,), (1 if tb else 0,)), ((), ()))

    gm, gn = M // tm, N // tn
    n_in = len(exchange.ins) if exchange else 0
    n_out = len(exchange.out_shapes) if exchange else 0

    def body(*refs):
        a_ref, b_ref, xin, o_ref = refs[0], refs[1], refs[2:2 + n_in], refs[2 + n_in]
        xout, rest = refs[3 + n_in:3 + n_in + n_out], refs[3 + n_in + n_out:]
        i, j, k = pl.program_id(0), pl.program_id(1), pl.program_id(2)
        if exchange:
            sems = rest[-3:]

            @pl.when(jnp.logical_and(jnp.logical_and(i == 0, j == 0), k == 0))
            def _():
                exchange.start(xin, xout, *sems)

        p = lax.dot_general(a_ref[...].astype(BF16), b_ref[...].astype(BF16), dims, preferred_element_type=F32)
        if nk == 1:
            o_ref[...] = p.astype(out_dtype).reshape(o_ref.shape)
        else:
            acc_ref = rest[0]

            @pl.when(k == 0)
            def _():
                acc_ref[...] = p

            @pl.when(k > 0)
            def _():
                acc_ref[...] += p

            @pl.when(k == nk - 1)
            def _():
                o_ref[...] = acc_ref[...].astype(out_dtype).reshape(o_ref.shape)

        if exchange:
            @pl.when(jnp.logical_and(jnp.logical_and(i == gm - 1, j == gn - 1), k == nk - 1))
            def _():
                exchange.finish(xin, xout, *sems)

    a_spec = pl.BlockSpec((tk, tm), lambda i, j, k: (k, i)) if ta else pl.BlockSpec((tm, tk), lambda i, j, k: (i, k))
    b_spec = pl.BlockSpec((tn, tk), lambda i, j, k: (j, k)) if tb else pl.BlockSpec((tk, tn), lambda i, j, k: (k, j))
    if shard_out:
        o_spec = pl.BlockSpec((1, tm, tn), lambda i, j, k: (j, i, 0))
        o_shape = _sds((N // tn, M, tn), out_dtype)
    else:
        o_spec = pl.BlockSpec((tm, tn), lambda i, j, k: (i, j))
        o_shape = _sds((M, N), out_dtype)
    scratch = [pltpu.VMEM((tm, tn), F32)] if nk > 1 else []
    if not exchange:
        return pl.pallas_call(
            body, name=name, grid=(gm, gn, nk), in_specs=[a_spec, b_spec], out_specs=o_spec, out_shape=o_shape,
            scratch_shapes=scratch, compiler_params=_cp(("parallel", "parallel", "arbitrary")),
        )(a, b)
    scratch += [pltpu.SemaphoreType.DMA((exchange.n_sems,)) for _ in range(3)]
    return pl.pallas_call(
        body, name=name, grid=(gm, gn, nk), in_specs=[a_spec, b_spec] + [ANY] * n_in, out_specs=[o_spec] + [ANY] * n_out,
        out_shape=[o_shape] + exchange.out_shapes, scratch_shapes=scratch,
        compiler_params=_cp(("arbitrary", "arbitrary", "arbitrary")),
    )(a, b, *exchange.ins)


def _ln_fwd(x, w):
    L, D = x.shape
    tm = min(256, L)

    def body(x_ref, w_ref, h_ref, r_ref):
        xv = x_ref[...]
        r = lax.rsqrt(jnp.mean(xv * xv, axis=-1, keepdims=True) + EPS)
        h_ref[...] = (xv * r * w_ref[...]).astype(BF16)
        r_ref[...] = r

    return pl.pallas_call(
        body, name="ln_fwd", grid=(L // tm,),
        in_specs=[pl.BlockSpec((tm, D), lambda i: (i, 0)), pl.BlockSpec((1, D), lambda i: (0, 0))],
        out_specs=[pl.BlockSpec((tm, D), lambda i: (i, 0)), pl.BlockSpec((tm, 1), lambda i: (i, 0))],
        out_shape=[_sds((L, D), BF16), _sds((L, 1))], compiler_params=_cp(("parallel",)),
    )(x, w)


def _ln_bwd(x, r, w, dh, dx2):
    L, D = x.shape
    tm = min(256, L)

    def body(x_ref, r_ref, w_ref, dh_ref, dx2_ref, dx_ref, dw_ref):
        i = pl.program_id(0)
        xv, rv, dhv = x_ref[...], r_ref[...], dh_ref[...]
        t = dhv * w_ref[...]
        m = jnp.mean(t * xv, axis=-1, keepdims=True)
        dx_ref[...] = dx2_ref[...] + rv * t - xv * (rv * rv * rv) * m
        part = jnp.sum(dhv * xv * rv, axis=0, keepdims=True)

        @pl.when(i == 0)
        def _():
            dw_ref[...] = part

        @pl.when(i > 0)
        def _():
            dw_ref[...] += part

    row = pl.BlockSpec((tm, D), lambda i: (i, 0))
    return pl.pallas_call(
        body, name="ln_bwd", grid=(L // tm,),
        in_specs=[row, pl.BlockSpec((tm, 1), lambda i: (i, 0)), pl.BlockSpec((1, D), lambda i: (0, 0)), row, row],
        out_specs=[row, pl.BlockSpec((1, D), lambda i: (0, 0))],
        out_shape=[_sds((L, D)), _sds((1, D))], compiler_params=_cp(("arbitrary",)),
    )(x, r, w, dh, dx2)


def _s5_param_math(lam_re, lam_im, log_step, b_re, b_im, expand):
    step = jnp.exp(log_step)
    mag = jnp.exp(lam_re * step)
    abar_re = mag * jnp.cos(lam_im * step)
    abar_im = mag * jnp.sin(lam_im * step)
    den = lam_re * lam_re + lam_im * lam_im
    xr = abar_re - 1.0
    f_re = (xr * lam_re + abar_im * lam_im) / den
    f_im = (abar_im * lam_re - xr * lam_im) / den
    fe_re = jnp.dot(f_re, expand, precision=HI, preferred_element_type=F32)
    fe_im = jnp.dot(f_im, expand, precision=HI, preferred_element_type=F32)
    bb_re = fe_re * b_re - fe_im * b_im
    bb_im = fe_re * b_im + fe_im * b_re
    return abar_re, abar_im, bb_re, bb_im


def _s5_expand():
    p = lax.broadcasted_iota(jnp.int32, (S5_STATE, S5_STATE * S5_GROUP), 0)
    q = lax.broadcasted_iota(jnp.int32, (S5_STATE, S5_STATE * S5_GROUP), 1)
    return (q // S5_GROUP == p).astype(F32)


def _s5_param_fwd(lam_re, lam_im, log_step, b_re, b_im):
    G, P = lam_re.shape

    def body(lr, li, ls, br, bi, ar_o, ai_o, bbr_o, bbi_o):
        outs = _s5_param_math(lr[...], li[...], ls[...], br[...], bi[...], _s5_expand())
        for o, v in zip((ar_o, ai_o, bbr_o, bbi_o), outs):
            o[...] = v

    return pl.pallas_call(
        body, name="s5_param_fwd",
        out_shape=[_sds((G, P)), _sds((G, P)), _sds(b_re.shape), _sds(b_re.shape)], compiler_params=_cp(),
    )(lam_re, lam_im, log_step, b_re, b_im)


def _s5_param_bwd(lam_re, lam_im, log_step, b_re, b_im, dar, dai, dbbr, dbbi):
    G, P = lam_re.shape

    def body(lr, li, ls, br, bi, g0, g1, g2, g3, dlr, dli, dls, dbr, dbi):
        ex = _s5_expand()
        _, f = jax.vjp(lambda a, b, c, d, e: _s5_param_math(a, b, c, d, e, ex), lr[...], li[...], ls[...], br[...], bi[...])
        grads = f((g0[...], g1[...], g2[...], g3[...]))
        for o, v in zip((dlr, dli, dls, dbr, dbi), grads):
            o[...] = v

    return pl.pallas_call(
        body, name="s5_param_bwd",
        out_shape=[_sds((G, P)), _sds((G, P)), _sds((G, 1)), _sds(b_re.shape), _sds(b_re.shape)], compiler_params=_cp(),
    )(lam_re, lam_im, log_step, b_re, b_im, dar, dai, dbbr, dbbi)


def _to_segs(src_ref, dst_ref, L):
    S = L // S5_SEG

    def body(j, carry):
        dst_ref[pl.ds(pl.multiple_of(S5_SEG * j, S5_SEG), S5_SEG), :] = src_ref[pl.ds(j, S5_SEG, stride=S), :]
        return carry

    lax.fori_loop(0, S, body, 0, unroll=8)


def _from_segs(src_ref, L, write):
    S = L // S5_SEG
    for seg in range(S5_SEG):
        def body(jb, carry, seg=seg):
            j0 = 16 * jb
            write(pl.multiple_of(seg * S + j0, 16), src_ref[pl.ds(S5_SEG * j0 + seg, 16, stride=S5_SEG), :])
            return carry

        lax.fori_loop(0, S // 16, body, 0, unroll=4)


def _scan_segs(ar, ai, re_ref, im_ref, end_r_ref, end_i_ref, c_r_ref, c_i_ref, L, tile0, reverse):
    S = L // S5_SEG
    NB, LN = re_ref.shape[0], 128
    assert S & (S - 1) == 0
    tile = lambda j: pl.ds(pl.multiple_of(S5_SEG * (tile0 + j), S5_SEG), S5_SEG)
    ar8 = [jnp.broadcast_to(ar[:, b * LN:(b + 1) * LN], (S5_SEG, LN)) for b in range(NB)]
    ai8 = [jnp.broadcast_to(ai[:, b * LN:(b + 1) * LN], (S5_SEG, LN)) for b in range(NB)]

    def step(idx, carry):
        rows = tile(S - 1 - idx if reverse else idx)
        out = []
        for b in range(NB):
            sr, si = carry[b]
            nr = ar8[b] * sr - ai8[b] * si + re_ref[b, rows, :]
            ni = ar8[b] * si + ai8[b] * sr + im_ref[b, rows, :]
            re_ref[b, rows, :] = nr
            im_ref[b, rows, :] = ni
            out.append((nr, ni))
        return tuple(out)

    z8 = jnp.zeros((S5_SEG, LN), F32)
    fin = lax.fori_loop(0, S, step, tuple((z8, z8) for _ in range(NB)), unroll=4)
    order = range(S5_SEG - 2, -1, -1) if reverse else range(1, S5_SEG)
    for b in range(NB):
        end_r_ref[b], end_i_ref[b] = fin[b]
        pr, pi = ar8[b][:1], ai8[b][:1]
        for _ in range(int(math.log2(S))):
            pr, pi = pr * pr - pi * pi, 2.0 * pr * pi
        first = S5_SEG - 1 if reverse else 0
        c_r_ref[b, pl.ds(first, 1), :] = jnp.zeros((1, LN), F32)
        c_i_ref[b, pl.ds(first, 1), :] = jnp.zeros((1, LN), F32)
        cr, ci = end_r_ref[b, pl.ds(first, 1), :], end_i_ref[b, pl.ds(first, 1), :]
        for i in order:
            c_r_ref[b, pl.ds(i, 1), :] = cr
            c_i_ref[b, pl.ds(i, 1), :] = ci
            er, ei = end_r_ref[b, pl.ds(i, 1), :], end_i_ref[b, pl.ds(i, 1), :]
            cr, ci = er + pr * cr - pi * ci, ei + pr * ci + pi * cr

    entering = [(c_r_ref[b], c_i_ref[b]) for b in range(NB)]

    def fix(idx, carry):
        rows = tile(S - 1 - idx if reverse else idx)
        out = []
        for b in range(NB):
            pr, pi = carry[b]
            cr, ci = entering[b]
            re_ref[b, rows, :] += pr * cr - pi * ci
            im_ref[b, rows, :] += pr * ci + pi * cr
            out.append((pr * ar8[b] - pi * ai8[b], pr * ai8[b] + pi * ar8[b]))
        return tuple(out)

    lax.fori_loop(0, S, fix, tuple((ar8[b], ai8[b]) for b in range(NB)), unroll=4)


def _s5_seg_scratch(L, cs, pad):
    NB = cs // 128
    small = [pltpu.VMEM((NB, S5_SEG, 128), F32) for _ in range(4)]
    return [pltpu.VMEM((NB, L + pad, 128), F32), pltpu.VMEM((NB, L + pad, 128), F32)] + small


def _s5_core_fwd(proj, wbr, wbi, a_re, a_im, cbr, cbi):
    L = proj.shape[0]
    nb, ci, cs = wbr.shape
    NB = cs // 128

    def body(u_ref, wbr_ref, wbi_ref, ar_ref, ai_ref, cbr_ref, cbi_ref, y_ref, sr, si, er, ei, cr, cim, up, yp):
        _to_segs(u_ref, up, L)
        u = up[...].astype(BF16)
        for b in range(NB):
            lanes = pl.ds(b * 128, 128)
            sr[b] = jnp.dot(u, wbr_ref[0, :, lanes], preferred_element_type=F32)
            si[b] = jnp.dot(u, wbi_ref[0, :, lanes], preferred_element_type=F32)
        _scan_segs(ar_ref[...], ai_ref[...], sr, si, er, ei, cr, cim, L, 0, False)
        y = jnp.zeros((L, ci), F32)
        for b in range(NB):
            lanes = pl.ds(b * 128, 128)
            y = y + (jnp.dot(sr[b].astype(BF16), cbr_ref[0, lanes, :], preferred_element_type=F32)
                     - jnp.dot(si[b].astype(BF16), cbi_ref[0, lanes, :], preferred_element_type=F32))
        yp[...] = y

        def write(row, val):
            y_ref[pl.ds(row, 16), :] = val

        _from_segs(yp, L, write)

    wspec = pl.BlockSpec((1, ci, cs), lambda j: (j, 0, 0))
    aspec = pl.BlockSpec((1, cs), lambda j: (0, j))
    cspec = pl.BlockSpec((1, cs, ci), lambda j: (j, 0, 0))
    return pl.pallas_call(
        body, name="s5_core_fwd", grid=(nb,),
        in_specs=[pl.BlockSpec((L, ci), lambda j: (0, OFF_US // ci + j)), wspec, wspec, aspec, aspec, cspec, cspec],
        out_specs=pl.BlockSpec((L, ci), lambda j: (0, j)), out_shape=_sds((L, nb * ci)),
        scratch_shapes=_s5_seg_scratch(L, cs, 0) + [pltpu.VMEM((L, ci), F32), pltpu.VMEM((L, ci), F32)],
        compiler_params=_cp(("arbitrary",)),
    )(proj, wbr, wbi, a_re, a_im, cbr, cbi)


def _s5_core_bwd(proj, wbr, wbi, a_re, a_im, cbr, cbi, dyc, du1):
    L = proj.shape[0]
    nb, ci, cs = wbr.shape
    NB = cs // 128
    S = L // S5_SEG
    PAD = S5_SEG

    def body(u_ref, wbr_ref, wbi_ref, ar_ref, ai_ref, cbr_ref, cbi_ref, dy_ref, du1_ref,
             du_ref, dwbr_ref, dwbi_ref, dcbr_ref, dcbi_ref, dar_ref, dai_ref,
             sr, si, er, ei, cr, cim, lr, li, up, dyp, dup):
        tn = (((0,), (0,)), ((), ()))
        nt = (((1,), (1,)), ((), ()))
        _to_segs(u_ref, up, L)
        _to_segs(dy_ref, dyp, L)
        _to_segs(du1_ref, dup, L)
        u = up[...].astype(BF16)
        dy = dyp[...].astype(BF16)
        ar, ai = ar_ref[...], ai_ref[...]
        for b in range(NB):
            lanes = pl.ds(b * 128, 128)
            sr[b, pl.ds(PAD, L), :] = jnp.dot(u, wbr_ref[0, :, lanes], preferred_element_type=F32)
            si[b, pl.ds(PAD, L), :] = jnp.dot(u, wbi_ref[0, :, lanes], preferred_element_type=F32)
        _scan_segs(ar, ai, sr, si, er, ei, cr, cim, L, 1, False)
        for b in range(NB):
            lanes = pl.ds(b * 128, 128)
            sr[b, pl.ds(0, PAD), :] = cr[b]
            si[b, pl.ds(0, PAD), :] = cim[b]
            lr[b] = lax.dot_general(dy, cbr_ref[0, lanes, :], nt, preferred_element_type=F32)
            li[b] = -lax.dot_general(dy, cbi_ref[0, lanes, :], nt, preferred_element_type=F32)
            dcbr_ref[0, lanes, :] = lax.dot_general(sr[b, pl.ds(PAD, L), :].astype(BF16), dy, tn, preferred_element_type=F32)
            dcbi_ref[0, lanes, :] = -lax.dot_general(si[b, pl.ds(PAD, L), :].astype(BF16), dy, tn, preferred_element_type=F32)
        _scan_segs(ar, -ai, lr, li, er, ei, cr, cim, L, 0, True)

        def da_step(j, carry):
            rows = pl.ds(pl.multiple_of(S5_SEG * j, S5_SEG), S5_SEG)
            out = []
            for b in range(NB):
                dar, dai = carry[b]
                pr_, pi_ = sr[b, rows, :], si[b, rows, :]
                gr, gi = lr[b, rows, :], li[b, rows, :]
                out.append((dar + (gr * pr_ + gi * pi_), dai + (gi * pr_ - gr * pi_)))
            return tuple(out)

        z8 = jnp.zeros((S5_SEG, 128), F32)
        acc = lax.fori_loop(0, S, da_step, tuple((z8, z8) for _ in range(NB)), unroll=4)
        du = dup[...]
        for b in range(NB):
            lanes = pl.ds(b * 128, 128)
            dar_ref[:, lanes] = jnp.sum(acc[b][0], axis=0, keepdims=True)
            dai_ref[:, lanes] = jnp.sum(acc[b][1], axis=0, keepdims=True)
            gr, gi = lr[b].astype(BF16), li[b].astype(BF16)
            du = du + (lax.dot_general(gr, wbr_ref[0, :, lanes], nt, preferred_element_type=F32)
                       + lax.dot_general(gi, wbi_ref[0, :, lanes], nt, preferred_element_type=F32))
            dwbr_ref[0, :, lanes] = lax.dot_general(u, gr, tn, preferred_element_type=F32)
            dwbi_ref[0, :, lanes] = lax.dot_general(u, gi, tn, preferred_element_type=F32)
        dup[...] = du

        def write(row, val):
            du_ref[pl.ds(row, 16), :] = val.astype(BF16)

        _from_segs(dup, L, write)

    wspec = pl.BlockSpec((1, ci, cs), lambda j: (j, 0, 0))
    aspec = pl.BlockSpec((1, cs), lambda j: (0, j))
    cspec = pl.BlockSpec((1, cs, ci), lambda j: (j, 0, 0))
    col = pl.BlockSpec((L, ci), lambda j: (0, j))
    return pl.pallas_call(
        body, name="s5_core_bwd", grid=(nb,),
        in_specs=[pl.BlockSpec((L, ci), lambda j: (0, OFF_US // ci + j)), wspec, wspec, aspec, aspec, cspec, cspec, col, col],
        out_specs=[col, wspec, wspec, cspec, cspec, aspec, aspec],
        out_shape=[_sds((L, nb * ci), BF16), _sds(wbr.shape), _sds(wbr.shape), _sds(cbr.shape), _sds(cbr.shape),
                   _sds((1, nb * cs)), _sds((1, nb * cs))],
        scratch_shapes=(_s5_seg_scratch(L, cs, PAD) + [pltpu.VMEM((NB, L, 128), F32), pltpu.VMEM((NB, L, 128), F32)]
                        + [pltpu.VMEM((L, ci), F32) for _ in range(3)]),
        compiler_params=_cp(("arbitrary",)),
    )(proj, wbr, wbi, a_re, a_im, cbr, cbi, dyc, du1)


def _s5_post_math(yc, u, z, d, wg):
    y = yc + d * u
    y1 = jax.nn.gelu(y)
    t = jnp.dot(y1.astype(BF16), wg, preferred_element_type=F32)
    sg = _sigmoid(t)
    return y, y1, sg


def _s5_post_fwd(yc, proj, d, wg):
    L, W = yc.shape
    tm = min(256, L)

    def body(yc_ref, u_ref, z_ref, d_ref, wg_ref, o_ref):
        _, y1, sg = _s5_post_math(yc_ref[...], u_ref[...], z_ref[...], d_ref[...], wg_ref[...])
        o_ref[...] = (y1 * sg * _silu(z_ref[...])).astype(BF16)

    row = pl.BlockSpec((tm, W), lambda i: (i, 0))
    return pl.pallas_call(
        body, name="s5_post_fwd", grid=(L // tm,),
        in_specs=[row, pl.BlockSpec((tm, W), lambda i: (i, OFF_US // W)), pl.BlockSpec((tm, W), lambda i: (i, OFF_ZS // W)),
                  pl.BlockSpec((1, W), lambda i: (0, 0)), pl.BlockSpec((W, W), lambda i: (0, 0))],
        out_specs=row, out_shape=_sds((L, W), BF16), compiler_params=_cp(("parallel",)),
    )(yc, proj, proj, d, wg)


def _s5_post_bwd(yc, proj, d, wg, dout):
    L, W = yc.shape
    tm = min(256, L)

    def body(yc_ref, u_ref, z_ref, d_ref, wg_ref, do_ref, dyc_ref, du_ref, dz_ref, dd_ref, dwg_ref):
        i = pl.program_id(0)
        u, z, d_, wgv = u_ref[...], z_ref[...], d_ref[...], wg_ref[...]
        y, y1, sg = _s5_post_math(yc_ref[...], u, z, d_, wgv)
        dout_ = do_ref[...]
        y2 = y1 * sg
        dy2 = dout_ * _silu(z)
        dz_ref[...] = (dout_ * y2 * _dsilu(z)).astype(BF16)
        dt = (dy2 * y1 * sg * (1.0 - sg)).astype(BF16)
        dy1 = dy2 * sg + lax.dot_general(dt, wgv, (((1,), (1,)), ((), ())), preferred_element_type=F32)
        _, gelu_vjp = jax.vjp(jax.nn.gelu, y)
        dy = gelu_vjp(dy1)[0]
        dyc_ref[...] = dy
        du_ref[...] = dy * d_
        dd_part = jnp.sum(dy * u, axis=0, keepdims=True)
        dwg_part = lax.dot_general(y1.astype(BF16), dt, (((0,), (0,)), ((), ())), preferred_element_type=F32)

        @pl.when(i == 0)
        def _():
            dd_ref[...] = dd_part
            dwg_ref[...] = dwg_part

        @pl.when(i > 0)
        def _():
            dd_ref[...] += dd_part
            dwg_ref[...] += dwg_part

    row = pl.BlockSpec((tm, W), lambda i: (i, 0))
    return pl.pallas_call(
        body, name="s5_post_bwd", grid=(L // tm,),
        in_specs=[row, pl.BlockSpec((tm, W), lambda i: (i, OFF_US // W)), pl.BlockSpec((tm, W), lambda i: (i, OFF_ZS // W)),
                  pl.BlockSpec((1, W), lambda i: (0, 0)), pl.BlockSpec((W, W), lambda i: (0, 0)), row],
        out_specs=[row, row, row, pl.BlockSpec((1, W), lambda i: (0, 0)), pl.BlockSpec((W, W), lambda i: (0, 0))],
        out_shape=[_sds((L, W)), _sds((L, W)), _sds((L, W), BF16), _sds((1, W)), _sds((W, W))],
        compiler_params=_cp(("arbitrary",)),
    )(yc, proj, proj, d, wg, dout)


def _shift_down(x, s):
    if s == 0:
        return x
    rows = lax.broadcasted_iota(jnp.int32, x.shape, 0)
    return jnp.where(rows >= s, pltpu.roll(x, s, 0), 0.0)


def _shift_up(x, s):
    if s == 0:
        return x
    L = x.shape[0]
    rows = lax.broadcasted_iota(jnp.int32, x.shape, 0)
    return jnp.where(rows < L - s, pltpu.roll(x, L - s, 0), 0.0)


def _conv_pre(x, w):
    acc = w[CONV_K - 1:CONV_K, :] * x
    for s in range(1, CONV_K):
        acc = acc + w[CONV_K - 1 - s:CONV_K - s, :] * _shift_down(x, s)
    return acc


def _dn_conv_fwd(proj, conv_w):
    L = proj.shape[0]
    W = DN_HEAD_DIM
    nq = 2 * DN_HEADS

    def body(x_ref, w_ref, o_ref):
        j = pl.program_id(0)
        act = _silu(_conv_pre(x_ref[...], w_ref[...]))
        r = lax.rsqrt(jnp.sum(act * act, axis=-1, keepdims=True) + EPS)
        scale = jnp.where(j < DN_HEADS, DN_HEAD_DIM ** -0.5, 1.0)
        o_ref[...] = jnp.where(j < nq, act * r * scale, act)

    return pl.pallas_call(
        body, name="dn_conv_fwd", grid=(3 * DN_HEADS,),
        in_specs=[pl.BlockSpec((L, W), lambda j: (0, OFF_Q // W + j)), pl.BlockSpec((CONV_K, W), lambda j: (0, j))],
        out_specs=pl.BlockSpec((L, W), lambda j: (0, j)), out_shape=_sds((L, 3 * D_DN)), compiler_params=_cp(("parallel",)),
    )(proj, conv_w)


def _dn_conv_bwd(proj, conv_w, dout):
    L = proj.shape[0]
    W = DN_HEAD_DIM
    nq = 2 * DN_HEADS

    def body(x_ref, w_ref, do_ref, dx_ref, dw_ref):
        j = pl.program_id(0)
        x, w, dout_ = x_ref[...], w_ref[...], do_ref[...]
        pre = _conv_pre(x, w)
        act = _silu(pre)
        r = lax.rsqrt(jnp.sum(act * act, axis=-1, keepdims=True) + EPS)
        scale = jnp.where(j < DN_HEADS, DN_HEAD_DIM ** -0.5, 1.0)
        g = dout_ * scale
        dact_n = r * g - act * (r * r * r) * jnp.sum(g * act, axis=-1, keepdims=True)
        dact = jnp.where(j < nq, dact_n, dout_)
        dpre = dact * _dsilu(pre)
        dx = w[CONV_K - 1:CONV_K, :] * dpre
        for s in range(1, CONV_K):
            dx = dx + w[CONV_K - 1 - s:CONV_K - s, :] * _shift_up(dpre, s)
        dx_ref[...] = dx.astype(BF16)
        for s in range(CONV_K):
            dw_ref[pl.ds(CONV_K - 1 - s, 1), :] = jnp.sum(dpre * _shift_down(x, s), axis=0, keepdims=True)

    col = pl.BlockSpec((L, W), lambda j: (0, j))
    wsp = pl.BlockSpec((CONV_K, W), lambda j: (0, j))
    return pl.pallas_call(
        body, name="dn_conv_bwd", grid=(3 * DN_HEADS,),
        in_specs=[pl.BlockSpec((L, W), lambda j: (0, OFF_Q // W + j)), wsp, col], out_specs=[col, wsp],
        out_shape=[_sds((L, 3 * D_DN), BF16), _sds((CONV_K, 3 * D_DN))], compiler_params=_cp(("parallel",)),
    )(proj, conv_w, dout)


def _softplus(x):
    return jnp.maximum(x, 0.0) + jnp.log(1.0 + jnp.exp(-jnp.abs(x)))


def _dn_gates_fwd(proj, alog, dtb):
    L = proj.shape[0]
    W = 128

    def body(p_ref, al_ref, db_ref, o_ref):
        p = p_ref[...]
        lane = lax.broadcasted_iota(jnp.int32, p.shape, 1)
        g = -jnp.exp(al_ref[...]) * _softplus(p + db_ref[...])
        o_ref[...] = jnp.where(lane < DN_HEADS, _sigmoid(p), jnp.where(lane < 2 * DN_HEADS, g, 0.0))

    return pl.pallas_call(
        body, name="dn_gates_fwd", grid=(1,),
        in_specs=[pl.BlockSpec((L, W), lambda i: (0, OFF_B // W)), pl.BlockSpec((1, W), lambda i: (0, 0)),
                  pl.BlockSpec((1, W), lambda i: (0, 0))],
        out_specs=pl.BlockSpec((L, W), lambda i: (0, 0)), out_shape=_sds((L, W)), compiler_params=_cp(("arbitrary",)),
    )(proj, alog, dtb)


def _dn_gates_bwd(proj, alog, dtb, dgates):
    L = proj.shape[0]
    W = 128

    def body(p_ref, al_ref, db_ref, dg_ref, dp_ref, dal_ref, ddb_ref):
        p, dg = p_ref[...], dg_ref[...]
        lane = lax.broadcasted_iota(jnp.int32, p.shape, 1)
        is_g = jnp.logical_and(lane >= DN_HEADS, lane < 2 * DN_HEADS)
        beta = _sigmoid(p)
        na = -jnp.exp(al_ref[...])
        xs = p + db_ref[...]
        dsp = dg * na * _sigmoid(xs)
        dp_ref[...] = jnp.where(lane < DN_HEADS, dg * beta * (1.0 - beta), jnp.where(is_g, dsp, 0.0)).astype(BF16)
        dal_ref[...] = jnp.sum(jnp.where(is_g, dg * na * _softplus(xs), 0.0), axis=0, keepdims=True)
        ddb_ref[...] = jnp.sum(jnp.where(is_g, dsp, 0.0), axis=0, keepdims=True)

    one = pl.BlockSpec((1, W), lambda i: (0, 0))
    full = pl.BlockSpec((L, W), lambda i: (0, 0))
    return pl.pallas_call(
        body, name="dn_gates_bwd", grid=(1,),
        in_specs=[pl.BlockSpec((L, W), lambda i: (0, OFF_B // W)), one, one, full], out_specs=[full, one, one],
        out_shape=[_sds((L, W), BF16), _sds((1, W)), _sds((1, W))], compiler_params=_cp(("arbitrary",)),
    )(proj, alog, dtb, dgates)


def _bdot(a, b, dims):
    return lax.dot_general(a.astype(BF16), b.astype(BF16), (dims, ((), ())), preferred_element_type=F32)


_NN, _NT, _TN = ((1,), (0,)), ((1,), (1,)), ((0,), (0,))


def _dot3(a, b, dims):
    ah, bh = a.astype(BF16), b.astype(BF16)
    al, bl = (a - ah.astype(F32)).astype(BF16), (b - bh.astype(F32)).astype(BF16)
    (ca,), (cb,) = dims
    a3 = jnp.concatenate([ah, ah, al], axis=ca)
    b3 = jnp.concatenate([bh, bl, bh], axis=cb)
    return lax.dot_general(a3, b3, (dims, ((), ())), preferred_element_type=F32)


def _mm_family(raw):
    nn = jax.custom_vjp(lambda a, b: raw(a, b, _NN))
    nt = jax.custom_vjp(lambda a, b: raw(a, b, _NT))
    tn = jax.custom_vjp(lambda a, b: raw(a, b, _TN))
    nn.defvjp(lambda a, b: (raw(a, b, _NN), (a, b)), lambda r, g: (raw(g, r[1], _NT), raw(r[0], g, _TN)))
    nt.defvjp(lambda a, b: (raw(a, b, _NT), (a, b)), lambda r, g: (raw(g, r[1], _NN), raw(g, r[0], _TN)))
    tn.defvjp(lambda a, b: (raw(a, b, _TN), (a, b)), lambda r, g: (raw(r[1], g, _NT), raw(r[0], g, _NN)))
    return nn, nt, tn


_mm_nn, _mm_nt, _mm_tn = _mm_family(_bdot)
_m3_nn, _m3_nt, _m3_tn = _mm_family(_dot3)


def _tri_apply(x, upper):
    C = x.shape[0]
    ii = lax.broadcasted_iota(jnp.int32, (C, 3 * C), 0)
    jj = lax.broadcasted_iota(jnp.int32, (C, 3 * C), 1) % C
    mat = ((ii <= jj) if upper else (ii >= jj)).astype(BF16)
    hi = x.astype(BF16)
    r = x - hi.astype(F32)
    mid = r.astype(BF16)
    lo = (r - mid.astype(F32)).astype(BF16)
    return jnp.dot(mat, jnp.concatenate([hi, mid, lo], axis=0), preferred_element_type=F32)


_cumsum_rows = jax.custom_vjp(lambda x: _tri_apply(x, False))
_cumsum_rows.defvjp(lambda x: (_tri_apply(x, False), None), lambda _, g: (_tri_apply(g, True),))


def _uli(a_s):
    C = a_s[0].shape[0]
    ii = lax.broadcasted_iota(jnp.int32, (C, C), 0)
    jj = lax.broadcasted_iota(jnp.int32, (C, C), 1)
    eye = jnp.where(ii == jj, 1.0, 0.0)
    ts = [eye - a for a in a_s]
    ms = list(a_s)
    for _ in range(int(math.log2(C)) - 1):
        ms = [_dot3(m, m, _NN) for m in ms]
        ts = [t + _dot3(t, m, _NN) for t, m in zip(ts, ms)]
    return tuple(ts)


def _uli_bwd(ts, gs):
    xs = [_dot3(t, g, _TN) for t, g in zip(ts, gs)]
    return (tuple(-_dot3(x, t, _NT) for x, t in zip(xs, ts)),)


_unit_lower_inverse = jax.custom_vjp(_uli)
_unit_lower_inverse.defvjp(lambda a_s: (lambda ts: (ts, ts))(_uli(a_s)), _uli_bwd)


def _prep_math(qs, ks, vs, gcols, bcols):
    n = len(qs)
    C, dv = vs[0].shape
    ii = lax.broadcasted_iota(jnp.int32, (C, C), 0)
    jj = lax.broadcasted_iota(jnp.int32, (C, C), 1)
    causal = ii >= jj
    strict = ii > jj
    sf = strict.astype(F32)
    ones = jnp.ones((C, dv), F32)
    dms = [_cumsum_rows(g * sf) for g in gcols]
    gcbs = [_cumsum_rows(g * ones) for g in gcols]
    kks = [_mm_nt(k, k) for k in ks]
    qks = [_mm_nt(q, k) for q, k in zip(qs, ks)]
    decays = [jnp.where(causal, jnp.exp(jnp.where(causal, dm, 0.0)), 0.0) for dm in dms]
    glasts = [jnp.sum(g * ones, axis=0, keepdims=True) for g in gcols]
    egs = [jnp.exp(gcb) for gcb in gcbs]
    ts = _unit_lower_inverse(tuple(jnp.where(strict, b * kk * dc, 0.0) for b, kk, dc in zip(bcols, kks, decays)))
    us = [_m3_nn(t, v * b) for t, v, b in zip(ts, vs, bcols)]
    ws = [_m3_nn(t, k * b * eg) for t, k, b, eg in zip(ts, ks, bcols, egs)]
    return tuple((us[i], ws[i], qs[i] * egs[i], ks[i] * jnp.exp(glasts[i] - gcbs[i]), qks[i] * decays[i],
                  jnp.exp(glasts[i])) for i in range(n))


def _gate_cols(gates, h):
    lane = lax.broadcasted_iota(jnp.int32, gates.shape, 1)
    bcol = jnp.sum(jnp.where(lane == h, gates, 0.0), axis=1, keepdims=True)
    gcol = jnp.sum(jnp.where(lane == h + DN_HEADS, gates, 0.0), axis=1, keepdims=True)
    return gcol, bcol


DN_HB = 8


def _dn_prep_fwd(qkv, gates):
    L = qkv.shape[0]
    N, H, d, HB = L // CHUNK, DN_HEADS, DN_HEAD_DIM, DN_HB

    def body(q_ref, k_ref, v_ref, g_ref, u_ref, w_ref, qd_ref, kd_ref, qk_ref, egl_ref):
        h0 = pl.program_id(1) * HB
        gates_ = g_ref[...]
        lanes_of = [pl.ds(i * d, d) for i in range(HB)]
        cols = [_gate_cols(gates_, h0 + i) for i in range(HB)]
        outs = _prep_math([q_ref[:, l] for l in lanes_of], [k_ref[:, l] for l in lanes_of], [v_ref[:, l] for l in lanes_of],
                          [c[0] for c in cols], [c[1] for c in cols])
        for i in range(HB):
            lanes = lanes_of[i]
            u, w, qd, kd, qk, egl = outs[i]
            u_ref[:, lanes] = u
            w_ref[:, lanes] = w
            qd_ref[:, lanes] = qd
            kd_ref[:, lanes] = kd
            qk_ref[0, i] = qk
            egl_ref[0, i] = jnp.broadcast_to(egl, (8, d))

    blk = lambda off: pl.BlockSpec((CHUNK, HB * d), lambda n, j: (n, off // HB + j))
    cc = pl.BlockSpec((1, HB, CHUNK, CHUNK), lambda n, j: (n, j, 0, 0))
    ee = pl.BlockSpec((1, HB, 8, d), lambda n, j: (n, j, 0, 0))
    big = _sds((L, D_DN))
    return pl.pallas_call(
        body, name="dn_prep_fwd", grid=(N, H // HB),
        in_specs=[blk(0), blk(H), blk(2 * H), pl.BlockSpec((CHUNK, 128), lambda n, j: (n, 0))],
        out_specs=[blk(0), blk(0), blk(0), blk(0), cc, ee],
        out_shape=[big, big, big, big, _sds((N, H, CHUNK, CHUNK)), _sds((N, H, 8, d))],
        compiler_params=_cp(("parallel", "parallel")),
    )(qkv, qkv, qkv, gates)


def _dn_scan_fwd(u, w, qd, kd, qk, egl):
    L = u.shape[0]
    N, H, d, HB = L // CHUNK, DN_HEADS, DN_HEAD_DIM, DN_HB

    def body(u_ref, w_ref, qd_ref, kd_ref, qk_ref, egl_ref, o_ref, st_ref, s_ref):
        n, h0 = pl.program_id(0), pl.program_id(1) * HB

        @pl.when(n == 0)
        def _():
            for i in range(HB):
                s_ref[h0 + i] = jnp.zeros((d, d), F32)

        hs = range(HB)
        ln = [pl.ds(i * d, d) for i in hs]
        st = [s_ref[h0 + i] for i in hs]
        ws = [_bdot(w_ref[:, ln[i]], st[i], _NN) for i in hs]
        qs = [_bdot(qd_ref[:, ln[i]], st[i], _NN) for i in hs]
        vn = [u_ref[:, ln[i]] - ws[i] for i in hs]
        qv = [_bdot(qk_ref[0, i], vn[i], _NN) for i in hs]
        kv = [_bdot(kd_ref[:, ln[i]], vn[i], _TN) for i in hs]
        for i in hs:
            st_ref[0, i] = st[i]
            o_ref[:, ln[i]] = qs[i] + qv[i]
            s_ref[h0 + i] = st[i] * egl_ref[0, i, pl.ds(0, 1), :] + kv[i]

    blk = pl.BlockSpec((CHUNK, HB * d), lambda n, j: (n, j))
    cc = pl.BlockSpec((1, HB, CHUNK, CHUNK), lambda n, j: (n, j, 0, 0))
    ee = pl.BlockSpec((1, HB, 8, d), lambda n, j: (n, j, 0, 0))
    return pl.pallas_call(
        body, name="dn_scan_fwd", grid=(N, H // HB), in_specs=[blk, blk, blk, blk, cc, ee],
        out_specs=[blk, pl.BlockSpec((1, HB, d, d), lambda n, j: (n, j, 0, 0))],
        out_shape=[_sds((L, D_DN)), _sds((N, H, d, d))], scratch_shapes=[pltpu.VMEM((H, d, d), F32)],
        compiler_params=_cp(("arbitrary", "arbitrary")),
    )(u, w, qd, kd, qk, egl)


def _dn_scan_bwd(u, w, qd, kd, qk, egl, states, do):
    L = u.shape[0]
    N, H, d, HB = L // CHUNK, DN_HEADS, DN_HEAD_DIM, DN_HB

    def body(u_ref, w_ref, qd_ref, kd_ref, qk_ref, egl_ref, st_ref, do_ref,
             du_ref, dw_ref, dqd_ref, dkd_ref, dqk_ref, degl_ref, ds_ref):
        n, h0 = pl.program_id(0), pl.program_id(1) * HB

        @pl.when(n == 0)
        def _():
            for i in range(HB):
                ds_ref[h0 + i] = jnp.zeros((d, d), F32)

        hs = range(HB)
        ln = [pl.ds(i * d, d) for i in hs]
        st = [st_ref[0, i] for i in hs]
        dsn = [ds_ref[h0 + i] for i in hs]
        do_ = [do_ref[:, ln[i]] for i in hs]
        ws = [_bdot(w_ref[:, ln[i]], st[i], _NN) for i in hs]
        d1 = [_bdot(qk_ref[0, i], do_[i], _TN) for i in hs]
        d2 = [_bdot(kd_ref[:, ln[i]], dsn[i], _NN) for i in hs]
        dqd = [_bdot(do_[i], st[i], _NT) for i in hs]
        qdo = [_bdot(qd_ref[:, ln[i]], do_[i], _TN) for i in hs]
        vn = [u_ref[:, ln[i]] - ws[i] for i in hs]
        dvn = [d1[i] + d2[i] for i in hs]
        dw = [_bdot(dvn[i], st[i], _NT) for i in hs]
        dkd = [_bdot(vn[i], dsn[i], _NT) for i in hs]
        dqk = [_bdot(do_[i], vn[i], _NT) for i in hs]
        wdv = [_bdot(w_ref[:, ln[i]], dvn[i], _TN) for i in hs]
        for i in hs:
            du_ref[:, ln[i]] = dvn[i]
            dw_ref[:, ln[i]] = -dw[i]
            dqd_ref[:, ln[i]] = dqd[i]
            dkd_ref[:, ln[i]] = dkd[i]
            dqk_ref[0, i] = dqk[i]
            degl_ref[0, i] = jnp.broadcast_to(jnp.sum(dsn[i] * st[i], keepdims=True), (8, d))
            ds_ref[h0 + i] = (qdo[i] - wdv[i]) + dsn[i] * egl_ref[0, i, pl.ds(0, 1), :]

    blk = pl.BlockSpec((CHUNK, HB * d), lambda n, j: (N - 1 - n, j))
    cc = pl.BlockSpec((1, HB, CHUNK, CHUNK), lambda n, j: (N - 1 - n, j, 0, 0))
    ee = pl.BlockSpec((1, HB, 8, d), lambda n, j: (N - 1 - n, j, 0, 0))
    ss = pl.BlockSpec((1, HB, d, d), lambda n, j: (N - 1 - n, j, 0, 0))
    big = _sds((L, D_DN))
    return pl.pallas_call(
        body, name="dn_scan_bwd", grid=(N, H // HB), in_specs=[blk, blk, blk, blk, cc, ee, ss, blk],
        out_specs=[blk, blk, blk, blk, cc, ee],
        out_shape=[big, big, big, big, _sds((N, H, CHUNK, CHUNK)), _sds((N, H, 8, d))],
        scratch_shapes=[pltpu.VMEM((H, d, d), F32)], compiler_params=_cp(("arbitrary", "arbitrary")),
    )(u, w, qd, kd, qk, egl, states, do)


def _dn_prep_bwd(qkv, gates, du, dw, dqd, dkd, dqk, degl):
    L = qkv.shape[0]
    N, H, d, HB = L // CHUNK, DN_HEADS, DN_HEAD_DIM, DN_HB

    def body(q_ref, k_ref, v_ref, g_ref, du_ref, dw_ref, dqd_ref, dkd_ref, dqk_ref, degl_ref, dq_ref, dk_ref, dv_ref, dg_ref):
        j = pl.program_id(1)
        h0 = j * HB
        gates_ = g_ref[...]
        lane = lax.broadcasted_iota(jnp.int32, gates_.shape, 1)
        lane1 = lax.broadcasted_iota(jnp.int32, (1, d), 1)
        part = jnp.zeros(gates_.shape, F32)
        lanes_of = [pl.ds(i * d, d) for i in range(HB)]
        cols = [_gate_cols(gates_, h0 + i) for i in range(HB)]
        _, f = jax.vjp(_prep_math, [q_ref[:, l] for l in lanes_of], [k_ref[:, l] for l in lanes_of],
                       [v_ref[:, l] for l in lanes_of], [c[0] for c in cols], [c[1] for c in cols])
        cots = tuple((du_ref[:, l], dw_ref[:, l], dqd_ref[:, l], dkd_ref[:, l], dqk_ref[0, i],
                      jnp.where(lane1 == 0, degl_ref[0, i, pl.ds(0, 1), :], 0.0)) for i, l in enumerate(lanes_of))
        dqs, dks, dvs, dgcs, dbcs = f(cots)
        for i in range(HB):
            lanes = lanes_of[i]
            dq_ref[:, lanes] = dqs[i]
            dk_ref[:, lanes] = dks[i]
            dv_ref[:, lanes] = dvs[i]
            part = part + jnp.where(lane == h0 + i, dbcs[i], 0.0) + jnp.where(lane == h0 + i + DN_HEADS, dgcs[i], 0.0)

        @pl.when(j == 0)
        def _():
            dg_ref[...] = part

        @pl.when(j > 0)
        def _():
            dg_ref[...] += part

    blk = lambda off: pl.BlockSpec((CHUNK, HB * d), lambda n, j: (n, off // HB + j))
    gsp = pl.BlockSpec((CHUNK, 128), lambda n, j: (n, 0))
    cc = pl.BlockSpec((1, HB, CHUNK, CHUNK), lambda n, j: (n, j, 0, 0))
    ee = pl.BlockSpec((1, HB, 8, d), lambda n, j: (n, j, 0, 0))
    big = _sds((L, D_DN))
    return pl.pallas_call(
        body, name="dn_prep_bwd", grid=(N, H // HB),
        in_specs=[blk(0), blk(H), blk(2 * H), gsp, blk(0), blk(0), blk(0), blk(0), cc, ee],
        out_specs=[blk(0), blk(0), blk(0), gsp], out_shape=[big, big, big, _sds((L, 128))],
        compiler_params=_cp(("parallel", "arbitrary")),
    )(qkv, qkv, qkv, gates, du, dw, dqd, dkd, dqk, degl)


def _dn_post_fwd(o, proj, nw):
    L = o.shape[0]
    d = DN_HEAD_DIM
    tm = min(512, L)

    def body(o_ref, z_ref, w_ref, y_ref):
        ov = o_ref[...]
        r = lax.rsqrt(jnp.mean(ov * ov, axis=-1, keepdims=True) + EPS)
        y_ref[...] = (ov * r * w_ref[...] * _silu(z_ref[...])).astype(BF16)

    blk = pl.BlockSpec((tm, d), lambda i, h: (i, h))
    return pl.pallas_call(
        body, name="dn_post_fwd", grid=(L // tm, DN_HEADS),
        in_specs=[blk, pl.BlockSpec((tm, d), lambda i, h: (i, OFF_ZD // d + h)), pl.BlockSpec((1, d), lambda i, h: (0, 0))],
        out_specs=blk, out_shape=_sds((L, D_DN), BF16), compiler_params=_cp(("parallel", "parallel")),
    )(o, proj, nw)


def _dn_post_bwd(o, proj, nw, dy):
    L = o.shape[0]
    d = DN_HEAD_DIM
    tm = min(512, L)

    def body(o_ref, z_ref, w_ref, dy_ref, do_ref, dz_ref, dw_ref):
        first = jnp.logical_and(pl.program_id(0) == 0, pl.program_id(1) == 0)
        ov, z, w, dyv = o_ref[...], z_ref[...], w_ref[...], dy_ref[...]
        r = lax.rsqrt(jnp.mean(ov * ov, axis=-1, keepdims=True) + EPS)
        xn = ov * r
        dz_ref[...] = (dyv * xn * w * _dsilu(z)).astype(BF16)
        dn = dyv * _silu(z)
        t = dn * w
        do_ref[...] = r * t - ov * (r * r * r) * jnp.mean(t * ov, axis=-1, keepdims=True)
        part = jnp.sum(dn * xn, axis=0, keepdims=True)

        @pl.when(first)
        def _():
            dw_ref[...] = part

        @pl.when(jnp.logical_not(first))
        def _():
            dw_ref[...] += part

    blk = pl.BlockSpec((tm, d), lambda i, h: (i, h))
    one = pl.BlockSpec((1, d), lambda i, h: (0, 0))
    return pl.pallas_call(
        body, name="dn_post_bwd", grid=(L // tm, DN_HEADS),
        in_specs=[blk, pl.BlockSpec((tm, d), lambda i, h: (i, OFF_ZD // d + h)), one, blk], out_specs=[blk, blk, one],
        out_shape=[_sds((L, D_DN)), _sds((L, D_DN), BF16), _sds((1, d))], compiler_params=_cp(("arbitrary", "arbitrary")),
    )(o, proj, nw, dy)


def _mix_fwd(s5o, dno, w_su, w_du, proj):
    L, K = s5o.shape
    N = w_su.shape[1]
    tm, tn = min(512, L), 512

    def body(a1, a2, b1, b2, gs, gd, ys_ref, yd_ref, mx_ref):
        ys = jnp.dot(a1[...], b1[...], preferred_element_type=F32)
        yd = jnp.dot(a2[...], b2[...], preferred_element_type=F32)
        ys_ref[...] = ys
        yd_ref[...] = yd
        mx_ref[...] = (_sigmoid(gs[...]) * ys + _sigmoid(gd[...]) * yd).astype(BF16)

    a = pl.BlockSpec((tm, K), lambda i, j: (i, 0))
    b = pl.BlockSpec((K, tn), lambda i, j: (0, j))
    o = pl.BlockSpec((tm, tn), lambda i, j: (i, j))
    return pl.pallas_call(
        body, name="mix_fwd", grid=(L // tm, N // tn),
        in_specs=[a, a, b, b, pl.BlockSpec((tm, tn), lambda i, j: (i, OFF_GS // tn + j)),
                  pl.BlockSpec((tm, tn), lambda i, j: (i, OFF_GD // tn + j))],
        out_specs=[o, o, o], out_shape=[_sds((L, N)), _sds((L, N)), _sds((L, N), BF16)],
        compiler_params=_cp(("parallel", "parallel")),
    )(s5o, dno, w_su, w_du, proj, proj)


def _mix_bwd(dx2b, w_out, proj, ys, yd):
    L, K = dx2b.shape
    N = w_out.shape[0]
    tm, tn = min(512, L), 512

    def body(a, b, gs, gd, ys_ref, yd_ref, dgs_ref, dgd_ref, dys_ref, dyd_ref):
        dm = lax.dot_general(a[...], b[...], (((1,), (1,)), ((), ())), preferred_element_type=F32)
        ss, sd = _sigmoid(gs[...]), _sigmoid(gd[...])
        dys_ref[...] = (dm * ss).astype(BF16)
        dyd_ref[...] = (dm * sd).astype(BF16)
        dgs_ref[...] = (dm * ys_ref[...] * ss * (1.0 - ss)).astype(BF16)
        dgd_ref[...] = (dm * yd_ref[...] * sd * (1.0 - sd)).astype(BF16)

    o = pl.BlockSpec((tm, tn), lambda i, j: (i, j))
    return pl.pallas_call(
        body, name="mix_bwd", grid=(L // tm, N // tn),
        in_specs=[pl.BlockSpec((tm, K), lambda i, j: (i, 0)), pl.BlockSpec((tn, K), lambda i, j: (j, 0)),
                  pl.BlockSpec((tm, tn), lambda i, j: (i, OFF_GS // tn + j)),
                  pl.BlockSpec((tm, tn), lambda i, j: (i, OFF_GD // tn + j)), o, o],
        out_specs=[o, o, o, o], out_shape=[_sds((L, N), BF16)] * 4, compiler_params=_cp(("parallel", "parallel")),
    )(dx2b, w_out, proj, proj, ys, yd)


def _final(mixed, w_out, x, tgt, fw):
    L, D = x.shape
    tm = min(256, L)

    def body(a_ref, b_ref, x_ref, t_ref, w_ref, dx_ref, dxb_ref, loss_ref, dw_ref):
        i = pl.program_id(0)
        x2 = x_ref[...] + jnp.dot(a_ref[...], b_ref[...], preferred_element_type=F32)
        w = w_ref[...]
        r = lax.rsqrt(jnp.mean(x2 * x2, axis=-1, keepdims=True) + EPS)
        xn = x2 * r
        e = xn * w - t_ref[...]
        lpart = 0.5 * jnp.sum(jnp.mean(e * e, axis=-1, keepdims=True), axis=0, keepdims=True)
        dy = e * (1.0 / D)
        t = dy * w
        dx2 = r * t - x2 * (r * r * r) * jnp.mean(t * x2, axis=-1, keepdims=True)
        dx_ref[...] = dx2
        dxb_ref[...] = dx2.astype(BF16)
        dwp = jnp.sum(dy * xn, axis=0, keepdims=True)
        lrow = jnp.broadcast_to(lpart, loss_ref.shape)

        @pl.when(i == 0)
        def _():
            loss_ref[...] = lrow
            dw_ref[...] = dwp

        @pl.when(i > 0)
        def _():
            loss_ref[...] += lrow
            dw_ref[...] += dwp

    row = pl.BlockSpec((tm, D), lambda i: (i, 0))
    one = pl.BlockSpec((1, D), lambda i: (0, 0))
    return pl.pallas_call(
        body, name="final", grid=(L // tm,),
        in_specs=[row, pl.BlockSpec((D, D), lambda i: (0, 0)), row, row, one],
        out_specs=[row, row, pl.BlockSpec((1, 128), lambda i: (0, 0)), one],
        out_shape=[_sds((L, D)), _sds((L, D), BF16), _sds((1, 128)), _sds((1, D))], compiler_params=_cp(("arbitrary",)),
    )(mixed, w_out, x, tgt, fw)


def _block_diag(t):
    J, g, a, b = t.shape
    eye = jnp.eye(g, dtype=t.dtype)
    return (t[:, :, :, None, :] * eye[None, :, None, :, None]).reshape(J, g * a, g * b)


def _block_diag_take(m, g):
    J, ga, gb = m.shape
    a, b = ga // g, gb // g
    m5 = m.reshape(J, g, a, g, b)
    idx = jnp.arange(g)
    return m5[:, idx, :, idx, :].transpose(1, 0, 2, 3)


class _PlainOps:
    def __init__(self, w_rest):
        self.w_rest = w_rest

    def in_proj(self, h, wt_perm):
        return _mm(h, wt_perm, tb=True, name="in_proj", tm=1024, tn=1152), self.w_rest

    def rest_grads(self, d_w_glu, d_w_su, d_w_du, d_w_out):
        pass

    def d_w_in(self, h, dproj):
        return _mm(dproj, h, ta=True, name="d_w_in", tm=1152, tn=1024)

    def d_h(self, dproj, wt_perm, d_wt_perm):
        return _mm(dproj, wt_perm, name="d_h", tm=2048, tn=1024, tk=1152)


def _local_step(x, tgt, ln_w, w_perm, lam_re, lam_im, log_step, b_re, b_im, c_re, c_im, s5_d,
                conv_w, a_log, dt_bias, norm_w, fw, ops):
    G, P, gb = S5_GROUPS, S5_STATE, S5_GROUPS // S5_BLOCKS
    h, rstd = _ln_fwd(x, ln_w)
    proj, (w_glu, w_su, w_du, w_out) = ops.in_proj(h, w_perm)

    b_re2, b_im2 = b_re.reshape(G, P * S5_GROUP), b_im.reshape(G, P * S5_GROUP)
    ls2 = log_step.reshape(G, 1)
    abar_re, abar_im, bb_re, bb_im = _s5_param_fwd(lam_re, lam_im, ls2, b_re2, b_im2)

    def to_wb(bb):
        return _block_diag(bb.reshape(S5_BLOCKS, gb, P, S5_GROUP).transpose(0, 1, 3, 2)).astype(BF16)

    def to_cb(cc):
        return _block_diag(cc.reshape(S5_BLOCKS, gb, S5_GROUP, P).transpose(0, 1, 3, 2)).astype(BF16)

    wbr, wbi, cbr, cbi = to_wb(bb_re), to_wb(bb_im), to_cb(c_re), to_cb(c_im)
    a_re_row, a_im_row = abar_re.reshape(1, G * P), abar_im.reshape(1, G * P)
    yc = _s5_core_fwd(proj, wbr, wbi, a_re_row, a_im_row, cbr, cbi)
    s5o = _s5_post_fwd(yc, proj, s5_d, w_glu)

    pad = lambda v: jnp.pad(v, ((0, 0), (DN_HEADS, 128 - 2 * DN_HEADS)))
    alog_row, dtb_row = pad(a_log), pad(dt_bias)
    qkv = _dn_conv_fwd(proj, conv_w)
    gates = _dn_gates_fwd(proj, alog_row, dtb_row)
    prep = _dn_prep_fwd(qkv, gates)
    o_dn, states = _dn_scan_fwd(*prep)
    dno = _dn_post_fwd(o_dn, proj, norm_w)

    ys, yd, mixed = _mix_fwd(s5o, dno, w_su, w_du, proj)
    dx2, dx2b, loss_row, d_fw = _final(mixed, w_out, x, tgt, fw)
    d_w_out = _mm(mixed, dx2b, ta=True, name="d_w_out")
    dgs, dgd, dys, dyd = _mix_bwd(dx2b, w_out, proj, ys, yd)
    d_w_su = _mm(s5o, dys, ta=True, name="d_w_su", shard_out=True)
    d_w_du = _mm(dno, dyd, ta=True, name="d_w_du", shard_out=True)
    ds5o = _mm(dys, w_su, tb=True, name="d_s5o")
    ddno = _mm(dyd, w_du, tb=True, name="d_dno")

    dyc, du1, dz_s, d_s5d, d_w_glu = _s5_post_bwd(yc, proj, s5_d, w_glu, ds5o)
    ops.rest_grads(d_w_glu, d_w_su, d_w_du, d_w_out)
    du, dwbr, dwbi, dcbr, dcbi, dar, dai = _s5_core_bwd(proj, wbr, wbi, a_re_row, a_im_row, cbr, cbi, dyc, du1)

    def from_wb(dwb):
        return _block_diag_take(dwb, gb).transpose(0, 1, 3, 2).reshape(G, P * S5_GROUP)

    def from_cb(dcb):
        return _block_diag_take(dcb, gb).transpose(0, 1, 3, 2).reshape(G, S5_GROUP, P)

    d_lam_re, d_lam_im, d_ls, d_b_re, d_b_im = _s5_param_bwd(
        lam_re, lam_im, ls2, b_re2, b_im2, dar.reshape(G, P), dai.reshape(G, P), from_wb(dwbr), from_wb(dwbi))

    do_dn, dz_d, d_norm_w = _dn_post_bwd(o_dn, proj, norm_w, ddno)
    dq, dk, dv, dgates = _dn_prep_bwd(qkv, gates, *_dn_scan_bwd(*prep, states, do_dn))
    dqkv, d_conv = _dn_conv_bwd(proj, conv_w, jnp.concatenate([dq, dk, dv], axis=1))
    dpb, d_alog_row, d_dtb_row = _dn_gates_bwd(proj, alog_row, dtb_row, dgates)

    dproj = jnp.concatenate([du, dz_s, dqkv, dz_d, dgs, dgd, dpb], axis=1)
    d_w_perm = ops.d_w_in(h, dproj)
    dh = ops.d_h(dproj, w_perm, d_w_perm)
    grad_x, d_ln_w = _ln_bwd(x, rstd, ln_w, dh, dx2)

    grads = dict(
        ln_w=d_ln_w, w_perm=d_w_perm, s5_lam_re=d_lam_re, s5_lam_im=d_lam_im, s5_log_step=d_ls.reshape(1, G),
        s5_b_re=d_b_re.reshape(G, P, S5_GROUP), s5_b_im=d_b_im.reshape(G, P, S5_GROUP),
        s5_c_re=from_cb(dcbr), s5_c_im=from_cb(dcbi), s5_d=d_s5d, s5_w_glu=d_w_glu, s5_w_up=d_w_su,
        dn_conv_w=d_conv, dn_a_log=d_alog_row[:, DN_HEADS:2 * DN_HEADS], dn_dt_bias=d_dtb_row[:, DN_HEADS:2 * DN_HEADS],
        dn_norm_w=d_norm_w, dn_w_up=d_w_du, w_out=d_w_out, final_norm_w=d_fw)
    return loss_row, grad_x, grads


def _place():
    x, y, c = lax.axis_index("x"), lax.axis_index("y"), lax.axis_index("c")
    return x, y, c


def _remote(src, dst, send_sem, recv_sem, to):
    return pltpu.make_async_remote_copy(src_ref=src, dst_ref=dst, send_sem=send_sem, recv_sem=recv_sem,
                                        device_id=to, device_id_type=MESH)


def _gather_exchange(shards, whole=()):
    na, nw = len(shards), len(whole)

    def half_of(ref, a, half):
        rows = shards[a].shape[0]
        return ref.at[pl.ds(half * (rows // 2), rows // 2)]

    def plan(ins, outs, send_sems, recv_sems, local_sems, receiving):
        x, y, c = _place()
        me = 2 * x + y
        sibling = (x, y, 1 - c)
        chips = [(1 - x, y), (x, 1 - y), (1 - x, 1 - y)]

        def part(a, chip, half):
            return half_of(outs[a].at[chip], a, half)

        own = [pltpu.make_async_copy(ins[a], outs[a].at[me], local_sems.at[a]) for a in range(na + nw)]
        sends, landed, passed, arrivals = [], [], [], []
        for a in range(na):
            for j, (px, py) in enumerate(chips):
                k = 6 * a + j
                sends.append(_remote(half_of(ins[a], a, c), part(a, me, c), send_sems.at[k], recv_sems.at[k], (px, py, c)))
                if receiving:
                    got, other = part(a, 2 * px + py, c), part(a, 2 * px + py, 1 - c)
                    landed.append(_remote(got, got, send_sems.at[k], recv_sems.at[k], (px, py, c)))
                    passed.append(_remote(got, got, send_sems.at[k + 3], recv_sems.at[k + 3], sibling))
                    arrivals.append(_remote(other, other, send_sems.at[k + 3], recv_sems.at[k + 3], sibling))
        for a in range(na, na + nw):
            for j, (px, py) in enumerate(chips):
                k = 6 * na + 3 * (a - na) + j
                sends.append(_remote(ins[a], outs[a].at[me], send_sems.at[k], recv_sems.at[k], (px, py, c)))
                if receiving:
                    arrivals.append(_remote(ins[a], outs[a].at[2 * px + py], send_sems.at[k], recv_sems.at[k], (px, py, c)))
        return own, sends, landed, passed, arrivals

    def start(ins, outs, *sems):
        own, sends, _, _, _ = plan(ins, outs, *sems, False)
        for cp in own + sends:
            cp.start()

    def finish(ins, outs, *sems):
        own, sends, landed, passed, arrivals = plan(ins, outs, *sems, True)
        for got, fwd in zip(landed, passed):
            got.wait_recv()
            fwd.start()
        for cp in arrivals:
            cp.wait_recv()
        for cp in sends + passed:
            cp.wait_send()
        for cp in own:
            cp.wait()

    arrays = list(shards) + list(whole)
    return _Exchange(arrays, [_sds((N_CHIPS,) + s.shape, s.dtype) for s in arrays], 6 * na + 3 * nw, start, finish)


def _gather_relayed(shard, whole, name):
    rows, cols = shard.shape
    nw = len(whole)

    def body(*refs):
        in_ref, w_in = refs[0], refs[1:1 + nw]
        out_ref, w_out = refs[1 + nw], refs[2 + nw:2 + 2 * nw]
        send_sems, recv_sems, local_sems = refs[2 + 2 * nw:]
        x, y, c = _place()
        me = 2 * x + y
        first = (jnp.where(c == 0, 1 - x, x), jnp.where(c == 0, y, 1 - y))
        second = (jnp.where(c == 0, x, 1 - x), jnp.where(c == 0, 1 - y, y))
        diag = (1 - x, 1 - y)
        sibling = (x, y, 1 - c)
        chip_of = lambda p: 2 * p[0] + p[1]

        def half(ref, h):
            return ref.at[pl.ds(0, rows), pl.ds(h * (cols // 2), cols // 2)]

        def slot(p, h):
            return half(out_ref.at[chip_of(p)], h)

        own = [pltpu.make_async_copy(in_ref, out_ref.at[me], local_sems.at[0])]
        own += [pltpu.make_async_copy(w_in[a], w_out[a].at[me], local_sems.at[1 + a]) for a in range(nw)]
        mine = half(out_ref.at[me], c)
        sends = [_remote(half(in_ref, c), mine, send_sems.at[0], recv_sems.at[0], (*first, c)),
                 _remote(half(in_ref, c), mine, send_sems.at[1], recv_sems.at[1], (*second, c))]
        others = [(1 - x, y), (x, 1 - y), (1 - x, 1 - y)]
        small = [_remote(w_in[a], w_out[a].at[me], send_sems.at[6 + 3 * a + j], recv_sems.at[6 + 3 * a + j], (*p, c))
                 for a in range(nw) for j, p in enumerate(others)]
        for cp in own + sends + small:
            cp.start()
        relay = None
        for k, (via, of) in enumerate([(first, first), (second, second), (second, diag)]):
            got = slot(of, c)
            _remote(got, got, send_sems.at[k], recv_sems.at[k], (*via, c)).wait_recv()
            if k == 0:
                relay = _remote(got, got, send_sems.at[2], recv_sems.at[2], (*second, c))
                relay.start()
            fwd = _remote(got, got, send_sems.at[3 + k], recv_sems.at[3 + k], sibling)
            fwd.start()
            sends.append(fwd)
        sends.append(relay)
        for k, of in enumerate([second, first, diag]):
            got = slot(of, 1 - c)
            _remote(got, got, send_sems.at[3 + k], recv_sems.at[3 + k], sibling).wait_recv()
        for a in range(nw):
            for j, p in enumerate(others):
                _remote(w_in[a], w_out[a].at[chip_of(p)], send_sems.at[6 + 3 * a + j], recv_sems.at[6 + 3 * a + j], (*p, c)).wait_recv()
        for cp in sends + small:
            cp.wait_send()
        for cp in own:
            cp.wait()

    arrays = [shard] + list(whole)
    n_sems = 6 + 3 * nw
    return pl.pallas_call(
        body, name=name, in_specs=[ANY] * (1 + nw), out_specs=[ANY] * (1 + nw),
        out_shape=[_sds((N_CHIPS,) + a.shape, a.dtype) for a in arrays],
        scratch_shapes=[pltpu.SemaphoreType.DMA((n_sems,)) for _ in range(3)],
    )(*arrays)


def _owners_exchange(csbs):
    na = len(csbs)

    def plan(ins, outs, send_sems, recv_sems, local_sems, receiving):
        x, y, c = _place()
        me = 2 * x + y
        sends, arrivals = [], []
        for a in range(na):
            for k in range(N_CHIPS - 1):
                j = (me + 1 + k) % N_CHIPS
                sends.append(_remote(ins[a].at[k], outs[a].at[2 - k], send_sems.at[3 * a + k], recv_sems.at[3 * a + 2 - k],
                                     (j // 2, j % 2, c)))
                if receiving:
                    arrivals.append(_remote(ins[a].at[k], outs[a].at[k], send_sems.at[3 * a + k], recv_sems.at[3 * a + k], (x, y, c)))
        return sends, arrivals

    def start(ins, outs, *sems):
        for cp in plan(ins, outs, *sems, False)[0]:
            cp.start()

    def finish(ins, outs, *sems):
        sends, arrivals = plan(ins, outs, *sems, True)
        for cp in arrivals:
            cp.wait_recv()
        for cp in sends:
            cp.wait_send()

    return _Exchange(csbs, [_sds(g.shape, g.dtype) for g in csbs], 3 * na, start, finish)


def _run_exchange(ex, name):
    n_in, n_out = len(ex.ins), len(ex.out_shapes)

    def body(*refs):
        ins, outs, sems = refs[:n_in], refs[n_in:n_in + n_out], refs[n_in + n_out:]
        ex.start(ins, outs, *sems)
        ex.finish(ins, outs, *sems)

    return pl.pallas_call(
        body, name=name, in_specs=[ANY] * n_in, out_specs=[ANY] * n_out, out_shape=ex.out_shapes,
        scratch_shapes=[pltpu.SemaphoreType.DMA((ex.n_sems,)) for _ in range(3)],
    )(*ex.ins)


def _swap_halves(gxs, name):
    na = len(gxs)

    def body(*refs):
        ins, outs = refs[:na], refs[na:2 * na]
        send_sems, recv_sems = refs[2 * na:]
        x, y, c = _place()
        cps = [_remote(ins[a].at[pl.ds(0, N_CHIPS), pl.ds(1 - c, 1)], outs[a], send_sems.at[a], recv_sems.at[a], (x, y, 1 - c))
               for a in range(na)]
        for cp in cps:
            cp.start()
        for cp in cps:
            cp.wait()

    return pl.pallas_call(
        body, name=name, in_specs=[ANY] * na, out_specs=[ANY] * na,
        out_shape=[_sds((N_CHIPS, 1) + g.shape[2:], g.dtype) for g in gxs],
        scratch_shapes=[pltpu.SemaphoreType.DMA((na,)), pltpu.SemaphoreType.DMA((na,))],
    )(*gxs)


def _share_halves(gfs):
    na = len(gfs)

    def body(*refs):
        ins, outs = refs[:na], refs[na:2 * na]
        send_sems, recv_sems = refs[2 * na:]
        x, y, c = _place()
        cps = [_remote(ins[a].at[pl.ds(c, 1)], outs[a].at[pl.ds(c, 1)], send_sems.at[a], recv_sems.at[a], (x, y, 1 - c))
               for a in range(na)]
        for cp in cps:
            cp.start()
        for a in range(na):
            cps[a].wait_send()
            _remote(ins[a].at[pl.ds(1 - c, 1)], outs[a].at[pl.ds(1 - c, 1)], send_sems.at[a], recv_sems.at[a], (x, y, 1 - c)).wait_recv()

    return pl.pallas_call(
        body, name="rs_share_halves", in_specs=[ANY] * na, out_specs=[ANY] * na,
        out_shape=[_sds(g.shape, g.dtype) for g in gfs], input_output_aliases={a: a for a in range(na)},
        scratch_shapes=[pltpu.SemaphoreType.DMA((na,)), pltpu.SemaphoreType.DMA((na,))],
    )(*gfs)


def _row_tile(rows, cols, budget=5 << 18):
    fits = [t for t in range(16, rows + 1, 16) if rows % t == 0 and t * cols * 4 <= budget]
    return max(fits) if fits else rows


def _chip_sums(gx, r1, where):
    _, _, r2, cd = gx.shape
    tr = _row_tile(r2, cd)

    def body(w_ref, a_ref, b_ref, o_ref):
        o_ref[...] = (a_ref[0] + b_ref[0]).astype(BF16)

    other = lambda k, i, w: ((w[1] + 1 + k) % N_CHIPS, w[0], i, 0)
    other0 = lambda k, i, w: ((w[1] + 1 + k) % N_CHIPS, 0, i, 0)
    return pl.pallas_call(
        body, name="rs_chip_sums",
        grid_spec=pltpu.PrefetchScalarGridSpec(
            num_scalar_prefetch=1, grid=(N_CHIPS - 1, r2 // tr),
            in_specs=[pl.BlockSpec((1, 1, tr, cd), other), pl.BlockSpec((1, 1, tr, cd), other0)],
            out_specs=pl.BlockSpec((1, tr, cd), lambda k, i, w: (k, i, 0))),
        out_shape=_sds((N_CHIPS - 1, r2, cd), BF16), compiler_params=_cp(("parallel", "parallel")),
    )(where, gx, r1)


def _owner_sum(gx, r1, r2x, where):
    _, _, r2, cd = gx.shape
    tr = _row_tile(r2, cd)

    def body(w_ref, a_ref, b_ref, r_ref, o_ref):
        acc = a_ref[0, 0] + b_ref[0, 0]
        for k in range(N_CHIPS - 1):
            acc = acc + r_ref[k].astype(F32)
        o_ref[0] = acc

    return pl.pallas_call(
        body, name="rs_owner_sum",
        grid_spec=pltpu.PrefetchScalarGridSpec(
            num_scalar_prefetch=1, grid=(r2 // tr,),
            in_specs=[pl.BlockSpec((1, 1, tr, cd), lambda i, w: (w[1], w[0], i, 0)),
                      pl.BlockSpec((1, 1, tr, cd), lambda i, w: (w[1], 0, i, 0)),
                      pl.BlockSpec((N_CHIPS - 1, tr, cd), lambda i, w: (0, i, 0))],
            out_specs=pl.BlockSpec((1, tr, cd), lambda i, w: (w[0], i, 0))),
        out_shape=_sds((2, r2, cd)), compiler_params=_cp(("parallel",)),
    )(where, gx, r1, r2x)


def _adamw_math(w, g, m, v):
    m = ADAM_B1 * m + (1.0 - ADAM_B1) * g
    v = ADAM_B2 * v + (1.0 - ADAM_B2) * (g * g)
    m_hat = m / (1.0 - ADAM_B1 ** ADAM_STEP)
    v_hat = v / (1.0 - ADAM_B2 ** ADAM_STEP)
    delta = -ADAM_LR * (m_hat / (jnp.sqrt(v_hat) + ADAM_EPS) + ADAM_WD * w)
    return delta, m, v


def _adamw(w, g, m, v, name):
    rows, cd = w.shape
    if rows % 16 == 0:
        tr, tc = _row_tile(rows, cd, budget=3 << 19), cd
    else:
        tr, tc = rows, (128 if rows * cd * 4 > (3 << 19) else cd)
    assert rows % tr == 0 and cd % tc == 0

    def body(w_ref, g_ref, m_ref, v_ref, d_ref, mo_ref, vo_ref):
        d, mm, vv = _adamw_math(w_ref[...], g_ref[...], m_ref[...], v_ref[...])
        d_ref[...] = d
        mo_ref[...] = mm
        vo_ref[...] = vv

    blk = pl.BlockSpec((tr, tc), lambda i, j: (i, j))
    return pl.pallas_call(
        body, name=name, grid=(rows // tr, cd // tc), in_specs=[blk] * 4, out_specs=[blk] * 3, out_shape=[_sds(w.shape)] * 3,
        compiler_params=_cp(("parallel", "parallel")),
    )(w, g, m, v)


def _small_allreduce(gp):
    R = gp.shape[0]
    R2 = R // 2
    assert R2 % 8 == 0

    def body(g_ref, go_ref, sib, csum, land, send_sems, recv_sems):
        x, y, c = _place()
        me = 2 * x + y
        sibling = (x, y, 1 - c)
        chips = [(1 - x, y), (x, 1 - y), (1 - x, 1 - y)]
        swap = _remote(g_ref, sib, send_sems.at[0], recv_sems.at[0], sibling)
        swap.start()
        swap.wait()
        csum[...] = g_ref[...] + sib[...]
        half = csum.at[pl.ds(c * R2, R2)]
        land[me] = csum[pl.ds(c * R2, R2), :]
        cps = [_remote(half, land.at[me], send_sems.at[1 + j], recv_sems.at[1 + j], (px, py, c))
               for j, (px, py) in enumerate(chips)]
        for cp in cps:
            cp.start()
        for j, (px, py) in enumerate(chips):
            _remote(half, land.at[2 * px + py], send_sems.at[1 + j], recv_sems.at[1 + j], (px, py, c)).wait_recv()
        for cp in cps:
            cp.wait_send()
        mine = go_ref.at[pl.ds(c * R2, R2)]
        go_ref[pl.ds(c * R2, R2), :] = (land[0] + land[1]) + (land[2] + land[3])
        share = _remote(mine, mine, send_sems.at[4], recv_sems.at[4], sibling)
        share.start()
        share.wait_send()
        other = go_ref.at[pl.ds((1 - c) * R2, R2)]
        _remote(other, other, send_sems.at[4], recv_sems.at[4], sibling).wait_recv()

    vm = pl.BlockSpec(memory_space=pltpu.VMEM)
    return pl.pallas_call(
        body, name="small_allreduce", in_specs=[vm], out_specs=vm, out_shape=_sds((R, 128)),
        scratch_shapes=[pltpu.VMEM((R, 128), F32), pltpu.VMEM((R, 128), F32), pltpu.VMEM((N_CHIPS, R2, 128), F32),
                        pltpu.SemaphoreType.DMA((5,)), pltpu.SemaphoreType.DMA((5,))],
        compiler_params=_cp(),
    )(gp)


def _adamw_many(ws, gs, ms, vs):
    n = len(ws)

    def body(*refs):
        w_r, g_r, m_r, v_r = refs[:n], refs[n:2 * n], refs[2 * n:3 * n], refs[3 * n:4 * n]
        d_r, mo_r, vo_r = refs[4 * n:5 * n], refs[5 * n:6 * n], refs[6 * n:]
        for i in range(n):
            d_r[i][...], mo_r[i][...], vo_r[i][...] = _adamw_math(w_r[i][...], g_r[i][...], m_r[i][...], v_r[i][...])

    vm = pl.BlockSpec(memory_space=pltpu.VMEM)
    shapes = [_sds(a.shape) for a in ws]
    outs = pl.pallas_call(
        body, name="adamw_small", in_specs=[vm] * (4 * n), out_specs=[vm] * (3 * n), out_shape=shapes * 3, compiler_params=_cp(),
    )(*ws, *gs, *ms, *vs)
    return outs[:n], outs[n:2 * n], outs[2 * n:]


def _pack(arrs):
    rows = []
    for a in arrs:
        f = a.reshape(-1)
        f = jnp.pad(f, (0, (-f.shape[0]) % 128))
        rows.append(f.reshape(-1, 128))
    p = jnp.concatenate(rows, axis=0)
    return jnp.pad(p, ((0, (-p.shape[0]) % 8), (0, 0)))


def _unpack(p, shapes):
    out, r = [], 0
    for s in shapes:
        n = math.prod(s)
        nr = -(-n // 128)
        out.append(p[r:r + nr].reshape(-1)[:n].reshape(s))
        r += nr
    return out


class _ExchangeOps(_PlainOps):
    def __init__(self, rest_shards, where):
        self.rest_shards, self.where = rest_shards, where
        self.reduced = []

    def in_proj(self, h, wt_perm):
        proj, g_glu, g_su, g_du, g_out = _mm(h, wt_perm, tb=True, name="in_proj", tm=1024, tn=1152,
                                             exchange=_gather_exchange(self.rest_shards))
        cat = lambda g: jnp.concatenate([g[j] for j in range(N_CHIPS)], axis=1)
        return proj, (g_glu.reshape(D_S5, D_S5), cat(g_su), cat(g_du), g_out.reshape(D_MODEL, D_MODEL))

    def _chip_sums(self, gxs, name):
        r1s = _swap_halves(gxs, name)
        return r1s, [_chip_sums(gx, r1, self.where) for gx, r1 in zip(gxs, r1s)]

    def rest_grads(self, d_w_glu, d_w_su, d_w_du, d_w_out):
        gxs = [d_w_glu.reshape(N_CHIPS, 2, D_S5 // 8, D_S5), d_w_su.reshape(N_CHIPS, 2, D_S5 // 2, D_MODEL // N_CHIPS),
               d_w_du.reshape(N_CHIPS, 2, D_DN // 2, D_MODEL // N_CHIPS), d_w_out.reshape(N_CHIPS, 2, D_MODEL // 8, D_MODEL)]
        r1s, csbs = self._chip_sums(gxs, "rs_swap_rest")
        self.rest = (gxs, r1s, csbs)

    def d_w_in(self, h, dproj):
        gxs, r1s, csbs = self.rest
        d_wt_perm, *r2s = _mm(dproj, h, ta=True, name="d_w_in", tm=1152, tn=1024, exchange=_owners_exchange(csbs))
        self.reduced = list(zip(gxs, r1s, r2s))
        return d_wt_perm

    def d_h(self, dproj, wt_perm, d_wt_perm):
        gx = _wt_windows(d_wt_perm).reshape(N_CHIPS, 2, WT_ROWS // 2, D_MODEL)
        self.beta_a = d_wt_perm[OFF_B:OFF_B + WT_NB]
        (r1,), (csb,) = self._chip_sums([gx], "rs_swap_w_in")
        dh, r2 = _mm(dproj, wt_perm, name="d_h", tm=2048, tn=1024, tk=1152, exchange=_owners_exchange([csb]))
        self.reduced = [(gx, r1, r2)] + self.reduced
        return dh


WT_SHARD = D_IN // N_CHIPS
WT_NB = 2 * DN_HEADS
WT_B, WT_LO = divmod(OFF_GS, WT_SHARD)
WT_FIRST = [i * WT_SHARD - (WT_NB if i > WT_B else 0) for i in range(N_CHIPS)]
WT_WIN = [16 * (r // 16) for r in WT_FIRST]
WT_SHIFT = [r - s for r, s in zip(WT_FIRST, WT_WIN)]
WT_ROWS = 2592
assert (WT_LO + WT_SHIFT[WT_B]) % 16 == 0 and max(WT_SHIFT) + WT_SHARD <= WT_ROWS and WT_WIN[-1] + WT_ROWS <= D_IN_PAD


def _wt_to_window(shard, chip):
    d = jnp.asarray(WT_SHIFT, jnp.int32)[chip]
    gap = jnp.where(chip == WT_B, 0, WT_NB)
    win = jnp.zeros((WT_ROWS, shard.shape[1]), shard.dtype)
    win = lax.dynamic_update_slice(win, shard[:WT_LO], (d, 0))
    win = lax.dynamic_update_slice(win, shard[WT_LO:WT_LO + WT_NB], (d + WT_LO, 0))
    win = lax.dynamic_update_slice(win, shard[WT_LO + WT_NB:], (d + WT_LO + gap, 0))
    return win, shard[WT_LO:WT_LO + WT_NB]


def _wt_from_window(win, beta_a, chip):
    d = jnp.asarray(WT_SHIFT, jnp.int32)[chip]
    gap = jnp.where(chip == WT_B, 0, WT_NB)
    cols = win.shape[1]
    head = lax.dynamic_slice(win, (d, 0), (WT_LO, cols))
    mid = jnp.where(chip == WT_B, beta_a, lax.dynamic_slice(win, (d + WT_LO, 0), (WT_NB, cols)))
    tail = lax.dynamic_slice(win, (d + WT_LO + gap, 0), (WT_SHARD - WT_LO - WT_NB, cols))
    return jnp.concatenate([head, mid, tail], axis=0)


def _wt_regroup(wins, beta_a):
    parts, at = [], 0
    for i in range(N_CHIPS):
        end = WT_WIN[i + 1] if i + 1 < N_CHIPS else OFF_B
        lo = at - WT_WIN[i]
        over = WT_WIN[i] + WT_ROWS - end if i + 1 < N_CHIPS else 0
        parts.append(wins[i, lo:end - WT_WIN[i]])
        if over:
            parts.append(wins[i, end - WT_WIN[i]:] + wins[i + 1, :over])
        at = end + over
    pad = jnp.zeros((D_IN_PAD - OFF_B - WT_NB, wins.shape[2]), wins.dtype)
    return jnp.concatenate(parts + [beta_a, pad], axis=0)


def _wt_windows(regrouped):
    return jnp.stack([regrouped[s:s + WT_ROWS] for s in WT_WIN])


_SMALL = ("ln_w", "s5_lam_re", "s5_lam_im", "s5_log_step", "s5_b_re", "s5_b_im", "s5_c_re", "s5_c_im", "s5_d",
          "dn_a_log", "dn_dt_bias", "dn_norm_w", "final_norm_w")
_BIG = ("w_in", "s5_w_glu", "s5_w_up", "dn_w_up", "w_out")
_ORDER = ("ln_w", "w_in", "s5_lam_re", "s5_lam_im", "s5_log_step", "s5_b_re", "s5_b_im", "s5_c_re", "s5_c_im", "s5_d",
          "s5_w_glu", "s5_w_up", "dn_conv_w", "dn_a_log", "dn_dt_bias", "dn_norm_w", "dn_w_up", "w_out", "final_norm_w")


def kernel(x, ln_w, w_in, s5_lam_re, s5_lam_im, s5_log_step, s5_b_re, s5_b_im, s5_c_re, s5_c_im, s5_d, s5_w_glu, s5_w_up, dn_conv_w, dn_a_log, dn_dt_bias, dn_norm_w, dn_w_up, w_out, final_norm_w, loss_target, m_ln_w, m_w_in, m_s5_lam_re, m_s5_lam_im, m_s5_log_step, m_s5_b_re, m_s5_b_im, m_s5_c_re, m_s5_c_im, m_s5_d, m_s5_w_glu, m_s5_w_up, m_dn_conv_w, m_dn_a_log, m_dn_dt_bias, m_dn_norm_w, m_dn_w_up, m_w_out, m_final_norm_w, v_ln_w, v_w_in, v_s5_lam_re, v_s5_lam_im, v_s5_log_step, v_s5_b_re, v_s5_b_im, v_s5_c_re, v_s5_c_im, v_s5_d, v_s5_w_glu, v_s5_w_up, v_dn_conv_w, v_dn_a_log, v_dn_dt_bias, v_dn_norm_w, v_dn_w_up, v_w_out, v_final_norm_w):
    w = dict(ln_w=ln_w, w_in=w_in, s5_lam_re=s5_lam_re, s5_lam_im=s5_lam_im, s5_log_step=s5_log_step, s5_b_re=s5_b_re,
             s5_b_im=s5_b_im, s5_c_re=s5_c_re, s5_c_im=s5_c_im, s5_d=s5_d, s5_w_glu=s5_w_glu, s5_w_up=s5_w_up,
             dn_conv_w=dn_conv_w, dn_a_log=dn_a_log, dn_dt_bias=dn_dt_bias, dn_norm_w=dn_norm_w, dn_w_up=dn_w_up, w_out=w_out,
             final_norm_w=final_norm_w)
    m = dict(ln_w=m_ln_w, w_in=m_w_in, s5_lam_re=m_s5_lam_re, s5_lam_im=m_s5_lam_im, s5_log_step=m_s5_log_step,
             s5_b_re=m_s5_b_re, s5_b_im=m_s5_b_im, s5_c_re=m_s5_c_re, s5_c_im=m_s5_c_im, s5_d=m_s5_d, s5_w_glu=m_s5_w_glu,
             s5_w_up=m_s5_w_up, dn_conv_w=m_dn_conv_w, dn_a_log=m_dn_a_log, dn_dt_bias=m_dn_dt_bias, dn_norm_w=m_dn_norm_w,
             dn_w_up=m_dn_w_up, w_out=m_w_out, final_norm_w=m_final_norm_w)
    v = dict(ln_w=v_ln_w, w_in=v_w_in, s5_lam_re=v_s5_lam_re, s5_lam_im=v_s5_lam_im, s5_log_step=v_s5_log_step,
             s5_b_re=v_s5_b_re, s5_b_im=v_s5_b_im, s5_c_re=v_s5_c_re, s5_c_im=v_s5_c_im, s5_d=v_s5_d, s5_w_glu=v_s5_w_glu,
             s5_w_up=v_s5_w_up, dn_conv_w=v_dn_conv_w, dn_a_log=v_dn_a_log, dn_dt_bias=v_dn_dt_bias, dn_norm_w=v_dn_norm_w,
             dn_w_up=v_dn_w_up, w_out=v_w_out, final_norm_w=v_final_norm_w)
    xi, yi, ci = _place()
    chip = 2 * xi + yi
    where = jnp.stack([ci, chip]).astype(jnp.int32)

    tr = lambda a: jnp.swapaxes(a[0], 0, 1)
    win, beta_a = _wt_to_window(tr(w_in).astype(BF16), chip)
    g_win, g_ba, g_conv = _gather_relayed(win, [beta_a, dn_conv_w[0]], "gather_w_in")
    cat = lambda g: jnp.concatenate([g[j] for j in range(N_CHIPS)], axis=1)
    w_perm = _wt_regroup(g_win, g_ba[WT_B])

    ops = _ExchangeOps([w[n][0].astype(BF16) for n in _BIG[1:]], where)
    loss_row, grad_x, g = _local_step(
        x[0], loss_target[0], ln_w, w_perm, s5_lam_re[0], s5_lam_im[0], s5_log_step, s5_b_re[0], s5_b_im[0], s5_c_re[0],
        s5_c_im[0], s5_d, cat(g_conv), dn_a_log, dn_dt_bias, dn_norm_w, final_norm_w[None], ops)
    loss = lax.psum(loss_row[0, 0], ("x", "y", "c"))

    gfs = [_owner_sum(gx, r1, r2x, where) for gx, r1, r2x in ops.reduced]
    gfs = _share_halves(gfs)
    grads, deltas, new_m, new_v = {}, {}, {}, {}
    for n, gf in zip(_BIG[1:], gfs[1:]):
        shp = w[n].shape
        g2 = gf.reshape(shp[1:])
        d_, m_, v_ = _adamw(w[n][0], g2, m[n][0], v[n][0], "adamw_" + n)
        grads[n], deltas[n], new_m[n], new_v[n] = g2.reshape(shp), d_.reshape(shp), m_.reshape(shp), v_.reshape(shp)

    go = _small_allreduce(_pack([g[n] for n in _SMALL] + [g["dn_conv_w"], ops.beta_a]))
    lanes = {"s5_b_re": (S5_GROUPS, S5_STATE * S5_GROUP), "s5_b_im": (S5_GROUPS, S5_STATE * S5_GROUP)}
    flat = [lanes.get(n, (math.prod(w[n].shape[:-1]), w[n].shape[-1])) for n in _SMALL]
    *gs, g_conv, g_beta_a = _unpack(go, flat + [(CONV_K, 3 * D_DN), (WT_NB, D_MODEL)])
    gt = _wt_from_window(gfs[0].reshape(WT_ROWS, D_MODEL), g_beta_a, chip)
    d_, m_, v_ = _adamw(tr(w_in), gt, tr(m_w_in), tr(v_w_in), "adamw_w_in")
    grads["w_in"], deltas["w_in"], new_m["w_in"], new_v["w_in"] = (jnp.swapaxes(a, 0, 1)[None] for a in (gt, d_, m_, v_))
    as2d = lambda t: [t[n].reshape(s) for n, s in zip(_SMALL, flat)]
    for dst, src in zip((grads, deltas, new_m, new_v), (gs, *_adamw_many(as2d(w), gs, as2d(m), as2d(v)))):
        for n, a in zip(_SMALL, src):
            dst[n] = a.reshape(w[n].shape)
    cc = 3 * D_DN // N_CHIPS
    g_conv_mine = lax.dynamic_slice(g_conv, (0, chip * cc), (CONV_K, cc))
    d_, m_, v_ = _adamw(dn_conv_w[0], g_conv_mine, m_dn_conv_w[0], v_dn_conv_w[0], "adamw_dn_conv_w")
    grads["dn_conv_w"], deltas["dn_conv_w"], new_m["dn_conv_w"], new_v["dn_conv_w"] = (
        g_conv_mine[None], d_[None], m_[None], v_[None])

    return (loss, grad_x[None], *[grads[n] for n in _ORDER], *[deltas[n] for n in _ORDER], *[new_m[n] for n in _ORDER],
            *[new_v[n] for n in _ORDER])
```

```python
import functools
import math

import jax
import jax.numpy as jnp
from jax import lax
from jax.experimental import pallas as pl
from jax.experimental.pallas import tpu as pltpu

F32 = jnp.float32
BF16 = jnp.bfloat16
HI = lax.Precision.HIGHEST
MESH = pl.DeviceIdType.MESH
ANY = pl.BlockSpec(memory_space=pl.ANY)

EPS = 1e-6
D_MODEL = 2048
D_S5 = 1024
S5_GROUP = 16
S5_GROUPS = 64
S5_STATE = 64
S5_BLOCKS = 8
S5_SEG = 8
DN_HEADS = 8
DN_HEAD_DIM = 128
D_DN = 1024
CONV_K = 4
CHUNK = 64
D_IN = 10256
D_IN_PAD = 10368
OFF_US, OFF_ZS, OFF_Q, OFF_K, OFF_V, OFF_ZD, OFF_GS, OFF_GD, OFF_B = 0, 1024, 2048, 3072, 4096, 5120, 6144, 8192, 10240
N_CHIPS = 4
N_DEV = 8
VMEM_LIMIT = 56 * 1024 * 1024

ADAM_LR = 0.001
ADAM_B1 = 0.9
ADAM_B2 = 0.999
ADAM_EPS = 1e-08
ADAM_WD = 0.01
ADAM_STEP = 10


def _cp(sem=None):
    return pltpu.CompilerParams(dimension_semantics=sem, vmem_limit_bytes=VMEM_LIMIT)


def _sds(shape, dtype=F32):
    return jax.ShapeDtypeStruct(tuple(shape), dtype)


def _sigmoid(x):
    return 1.0 / (1.0 + jnp.exp(-x))


def _silu(x):
    return x * _sigmoid(x)


def _dsilu(x):
    s = _sigmoid(x)
    return s * (1.0 + x * (1.0 - s))


class _Exchange:
    def __init__(self, ins, out_shapes, n_sems, start, finish):
        self.ins, self.out_shapes, self.n_sems, self.start, self.finish = list(ins), list(out_shapes), n_sems, start, finish


def _mm(a, b, *, name, ta=False, tb=False, out_dtype=F32, tm=512, tn=512, tk=2048, shard_out=False, exchange=None):
    if ta:
        K, M = a.shape
    else:
        M, K = a.shape
    if tb:
        N, K2 = b.shape
    else:
        K2, N = b.shape
    assert K == K2, (a.shape, b.shape)
    tm, tn, tk = min(tm, M), min(tn, N), min(tk, K)
    assert M % tm == 0 and N % tn == 0 and K % tk == 0, (M, N, K, tm, tn, tk)
    nk = K // tk
    dims = (((0 if ta else 1,), (1 if tb else 0,)), ((), ()))

    gm, gn = M // tm, N // tn
    n_in = len(exchange.ins) if exchange else 0
    n_out = len(exchange.out_shapes) if exchange else 0

    def body(*refs):
        a_ref, b_ref, xin, o_ref = refs[0], refs[1], refs[2:2 + n_in], refs[2 + n_in]
        xout, rest = refs[3 + n_in:3 + n_in + n_out], refs[3 + n_in + n_out:]
        i, j, k = pl.program_id(0), pl.program_id(1), pl.program_id(2)
        if exchange:
            sems = rest[-3:]

            @pl.when(jnp.logical_and(jnp.logical_and(i == 0, j == 0), k == 0))
            def _():
                exchange.start(xin, xout, *sems)

        p = lax.dot_general(a_ref[...].astype(BF16), b_ref[...].astype(BF16), dims, preferred_element_type=F32)
        if nk == 1:
            o_ref[...] = p.astype(out_dtype).reshape(o_ref.shape)
        else:
            acc_ref = rest[0]

            @pl.when(k == 0)
            def _():
                acc_ref[...] = p

            @pl.when(k > 0)
            def _():
                acc_ref[...] += p

            @pl.when(k == nk - 1)
            def _():
                o_ref[...] = acc_ref[...].astype(out_dtype).reshape(o_ref.shape)

        if exchange:
            @pl.when(jnp.logical_and(jnp.logical_and(i == gm - 1, j == gn - 1), k == nk - 1))
            def _():
                exchange.finish(xin, xout, *sems)

    a_spec = pl.BlockSpec((tk, tm), lambda i, j, k: (k, i)) if ta else pl.BlockSpec((tm, tk), lambda i, j, k: (i, k))
    b_spec = pl.BlockSpec((tn, tk), lambda i, j, k: (j, k)) if tb else pl.BlockSpec((tk, tn), lambda i, j, k: (k, j))
    if shard_out:
        o_spec = pl.BlockSpec((1, tm, tn), lambda i, j, k: (j, i, 0))
        o_shape = _sds((N // tn, M, tn), out_dtype)
    else:
        o_spec = pl.BlockSpec((tm, tn), lambda i, j, k: (i, j))
        o_shape = _sds((M, N), out_dtype)
    scratch = [pltpu.VMEM((tm, tn), F32)] if nk > 1 else []
    if not exchange:
        return pl.pallas_call(
            body, name=name, grid=(gm, gn, nk), in_specs=[a_spec, b_spec], out_specs=o_spec, out_shape=o_shape,
            scratch_shapes=scratch, compiler_params=_cp(("parallel", "parallel", "arbitrary")),
        )(a, b)
    scratch += [pltpu.SemaphoreType.DMA((exchange.n_sems,)) for _ in range(3)]
    return pl.pallas_call(
        body, name=name, grid=(gm, gn, nk), in_specs=[a_spec, b_spec] + [ANY] * n_in, out_specs=[o_spec] + [ANY] * n_out,
        out_shape=[o_shape] + exchange.out_shapes, scratch_shapes=scratch,
        compiler_params=_cp(("arbitrary", "arbitrary", "arbitrary")),
    )(a, b, *exchange.ins)


def _ln_fwd(x, w):
    L, D = x.shape
    tm = min(256, L)

    def body(x_ref, w_ref, h_ref, r_ref):
        xv = x_ref[...]
        r = lax.rsqrt(jnp.mean(xv * xv, axis=-1, keepdims=True) + EPS)
        h_ref[...] = (xv * r * w_ref[...]).astype(BF16)
        r_ref[...] = r

    return pl.pallas_call(
        body, name="ln_fwd", grid=(L // tm,),
        in_specs=[pl.BlockSpec((tm, D), lambda i: (i, 0)), pl.BlockSpec((1, D), lambda i: (0, 0))],
        out_specs=[pl.BlockSpec((tm, D), lambda i: (i, 0)), pl.BlockSpec((tm, 1), lambda i: (i, 0))],
        out_shape=[_sds((L, D), BF16), _sds((L, 1))], compiler_params=_cp(("parallel",)),
    )(x, w)


def _ln_bwd(x, r, w, dh, dx2):
    L, D = x.shape
    tm = min(256, L)

    def body(x_ref, r_ref, w_ref, dh_ref, dx2_ref, dx_ref, dw_ref):
        i = pl.program_id(0)
        xv, rv, dhv = x_ref[...], r_ref[...], dh_ref[...]
        t = dhv * w_ref[...]
        m = jnp.mean(t * xv, axis=-1, keepdims=True)
        dx_ref[...] = dx2_ref[...] + rv * t - xv * (rv * rv * rv) * m
        part = jnp.sum(dhv * xv * rv, axis=0, keepdims=True)

        @pl.when(i == 0)
        def _():
            dw_ref[...] = part

        @pl.when(i > 0)
        def _():
            dw_ref[...] += part

    row = pl.BlockSpec((tm, D), lambda i: (i, 0))
    return pl.pallas_call(
        body, name="ln_bwd", grid=(L // tm,),
        in_specs=[row, pl.BlockSpec((tm, 1), lambda i: (i, 0)), pl.BlockSpec((1, D), lambda i: (0, 0)), row, row],
        out_specs=[row, pl.BlockSpec((1, D), lambda i: (0, 0))],
        out_shape=[_sds((L, D)), _sds((1, D))], compiler_params=_cp(("arbitrary",)),
    )(x, r, w, dh, dx2)


def _s5_param_math(lam_re, lam_im, log_step, b_re, b_im, expand):
    step = jnp.exp(log_step)
    mag = jnp.exp(lam_re * step)
    abar_re = mag * jnp.cos(lam_im * step)
    abar_im = mag * jnp.sin(lam_im * step)
    den = lam_re * lam_re + lam_im * lam_im
    xr = abar_re - 1.0
    f_re = (xr * lam_re + abar_im * lam_im) / den
    f_im = (abar_im * lam_re - xr * lam_im) / den
    fe_re = jnp.dot(f_re, expand, precision=HI, preferred_element_type=F32)
    fe_im = jnp.dot(f_im, expand, precision=HI, preferred_element_type=F32)
    bb_re = fe_re * b_re - fe_im * b_im
    bb_im = fe_re * b_im + fe_im * b_re
    return abar_re, abar_im, bb_re, bb_im


def _s5_expand():
    p = lax.broadcasted_iota(jnp.int32, (S5_STATE, S5_STATE * S5_GROUP), 0)
    q = lax.broadcasted_iota(jnp.int32, (S5_STATE, S5_STATE * S5_GROUP), 1)
    return (q // S5_GROUP == p).astype(F32)


def _s5_param_fwd(lam_re, lam_im, log_step, b_re, b_im):
    G, P = lam_re.shape

    def body(lr, li, ls, br, bi, ar_o, ai_o, bbr_o, bbi_o):
        outs = _s5_param_math(lr[...], li[...], ls[...], br[...], bi[...], _s5_expand())
        for o, v in zip((ar_o, ai_o, bbr_o, bbi_o), outs):
            o[...] = v

    return pl.pallas_call(
        body, name="s5_param_fwd",
        out_shape=[_sds((G, P)), _sds((G, P)), _sds(b_re.shape), _sds(b_re.shape)], compiler_params=_cp(),
    )(lam_re, lam_im, log_step, b_re, b_im)


def _s5_param_bwd(lam_re, lam_im, log_step, b_re, b_im, dar, dai, dbbr, dbbi):
    G, P = lam_re.shape

    def body(lr, li, ls, br, bi, g0, g1, g2, g3, dlr, dli, dls, dbr, dbi):
        ex = _s5_expand()
        _, f = jax.vjp(lambda a, b, c, d, e: _s5_param_math(a, b, c, d, e, ex), lr[...], li[...], ls[...], br[...], bi[...])
        grads = f((g0[...], g1[...], g2[...], g3[...]))
        for o, v in zip((dlr, dli, dls, dbr, dbi), grads):
            o[...] = v

    return pl.pallas_call(
        body, name="s5_param_bwd",
        out_shape=[_sds((G, P)), _sds((G, P)), _sds((G, 1)), _sds(b_re.shape), _sds(b_re.shape)], compiler_params=_cp(),
    )(lam_re, lam_im, log_step, b_re, b_im, dar, dai, dbbr, dbbi)


def _to_segs(src_ref, dst_ref, L):
    S = L // S5_SEG

    def body(j, carry):
        dst_ref[pl.ds(pl.multiple_of(S5_SEG * j, S5_SEG), S5_SEG), :] = src_ref[pl.ds(j, S5_SEG, stride=S), :]
        return carry

    lax.fori_loop(0, S, body, 0, unroll=8)


def _from_segs(src_ref, L, write):
    S = L // S5_SEG
    for seg in range(S5_SEG):
        def body(jb, carry, seg=seg):
            j0 = 16 * jb
            write(pl.multiple_of(seg * S + j0, 16), src_ref[pl.ds(S5_SEG * j0 + seg, 16, stride=S5_SEG), :])
            return carry

        lax.fori_loop(0, S // 16, body, 0, unroll=4)


def _scan_segs(ar, ai, re_ref, im_ref, end_r_ref, end_i_ref, c_r_ref, c_i_ref, L, tile0, reverse):
    S = L // S5_SEG
    NB, LN = re_ref.shape[0], 128
    assert S & (S - 1) == 0
    tile = lambda j: pl.ds(pl.multiple_of(S5_SEG * (tile0 + j), S5_SEG), S5_SEG)
    ar8 = [jnp.broadcast_to(ar[:, b * LN:(b + 1) * LN], (S5_SEG, LN)) for b in range(NB)]
    ai8 = [jnp.broadcast_to(ai[:, b * LN:(b + 1) * LN], (S5_SEG, LN)) for b in range(NB)]

    def step(idx, carry):
        rows = tile(S - 1 - idx if reverse else idx)
        out = []
        for b in range(NB):
            sr, si = carry[b]
            nr = ar8[b] * sr - ai8[b] * si + re_ref[b, rows, :]
            ni = ar8[b] * si + ai8[b] * sr + im_ref[b, rows, :]
            re_ref[b, rows, :] = nr
            im_ref[b, rows, :] = ni
            out.append((nr, ni))
        return tuple(out)

    z8 = jnp.zeros((S5_SEG, LN), F32)
    fin = lax.fori_loop(0, S, step, tuple((z8, z8) for _ in range(NB)), unroll=4)
    order = range(S5_SEG - 2, -1, -1) if reverse else range(1, S5_SEG)
    for b in range(NB):
        end_r_ref[b], end_i_ref[b] = fin[b]
        pr, pi = ar8[b][:1], ai8[b][:1]
        for _ in range(int(math.log2(S))):
            pr, pi = pr * pr - pi * pi, 2.0 * pr * pi
        first = S5_SEG - 1 if reverse else 0
        c_r_ref[b, pl.ds(first, 1), :] = jnp.zeros((1, LN), F32)
        c_i_ref[b, pl.ds(first, 1), :] = jnp.zeros((1, LN), F32)
        cr, ci = end_r_ref[b, pl.ds(first, 1), :], end_i_ref[b, pl.ds(first, 1), :]
        for i in order:
            c_r_ref[b, pl.ds(i, 1), :] = cr
            c_i_ref[b, pl.ds(i, 1), :] = ci
            er, ei = end_r_ref[b, pl.ds(i, 1), :], end_i_ref[b, pl.ds(i, 1), :]
            cr, ci = er + pr * cr - pi * ci, ei + pr * ci + pi * cr

    entering = [(c_r_ref[b], c_i_ref[b]) for b in range(NB)]

    def fix(idx, carry):
        rows = tile(S - 1 - idx if reverse else idx)
        out = []
        for b in range(NB):
            pr, pi = carry[b]
            cr, ci = entering[b]
            re_ref[b, rows, :] += pr * cr - pi * ci
            im_ref[b, rows, :] += pr * ci + pi * cr
            out.append((pr * ar8[b] - pi * ai8[b], pr * ai8[b] + pi * ar8[b]))
        return tuple(out)

    lax.fori_loop(0, S, fix, tuple((ar8[b], ai8[b]) for b in range(NB)), unroll=4)


def _s5_seg_scratch(L, cs, pad):
    NB = cs // 128
    small = [pltpu.VMEM((NB, S5_SEG, 128), F32) for _ in range(4)]
    return [pltpu.VMEM((NB, L + pad, 128), F32), pltpu.VMEM((NB, L + pad, 128), F32)] + small


def _s5_core_fwd(proj, wbr, wbi, a_re, a_im, cbr, cbi):
    L = proj.shape[0]
    nb, ci, cs = wbr.shape
    NB = cs // 128

    def body(u_ref, wbr_ref, wbi_ref, ar_ref, ai_ref, cbr_ref, cbi_ref, y_ref, sr, si, er, ei, cr, cim, up, yp):
        _to_segs(u_ref, up, L)
        u = up[...].astype(BF16)
        for b in range(NB):
            lanes = pl.ds(b * 128, 128)
            sr[b] = jnp.dot(u, wbr_ref[0, :, lanes], preferred_element_type=F32)
            si[b] = jnp.dot(u, wbi_ref[0, :, lanes], preferred_element_type=F32)
        _scan_segs(ar_ref[...], ai_ref[...], sr, si, er, ei, cr, cim, L, 0, False)
        y = jnp.zeros((L, ci), F32)
        for b in range(NB):
            lanes = pl.ds(b * 128, 128)
            y = y + (jnp.dot(sr[b].astype(BF16), cbr_ref[0, lanes, :], preferred_element_type=F32)
                     - jnp.dot(si[b].astype(BF16), cbi_ref[0, lanes, :], preferred_element_type=F32))
        yp[...] = y

        def write(row, val):
            y_ref[pl.ds(row, 16), :] = val

        _from_segs(yp, L, write)

    wspec = pl.BlockSpec((1, ci, cs), lambda j: (j, 0, 0))
    aspec = pl.BlockSpec((1, cs), lambda j: (0, j))
    cspec = pl.BlockSpec((1, cs, ci), lambda j: (j, 0, 0))
    return pl.pallas_call(
        body, name="s5_core_fwd", grid=(nb,),
        in_specs=[pl.BlockSpec((L, ci), lambda j: (0, OFF_US // ci + j)), wspec, wspec, aspec, aspec, cspec, cspec],
        out_specs=pl.BlockSpec((L, ci), lambda j: (0, j)), out_shape=_sds((L, nb * ci)),
        scratch_shapes=_s5_seg_scratch(L, cs, 0) + [pltpu.VMEM((L, ci), F32), pltpu.VMEM((L, ci), F32)],
        compiler_params=_cp(("arbitrary",)),
    )(proj, wbr, wbi, a_re, a_im, cbr, cbi)


def _s5_core_bwd(proj, wbr, wbi, a_re, a_im, cbr, cbi, dyc, du1):
    L = proj.shape[0]
    nb, ci, cs = wbr.shape
    NB = cs // 128
    S = L // S5_SEG
    PAD = S5_SEG

    def body(u_ref, wbr_ref, wbi_ref, ar_ref, ai_ref, cbr_ref, cbi_ref, dy_ref, du1_ref,
             du_ref, dwbr_ref, dwbi_ref, dcbr_ref, dcbi_ref, dar_ref, dai_ref,
             sr, si, er, ei, cr, cim, lr, li, up, dyp, dup):
        tn = (((0,), (0,)), ((), ()))
        nt = (((1,), (1,)), ((), ()))
        _to_segs(u_ref, up, L)
        _to_segs(dy_ref, dyp, L)
        _to_segs(du1_ref, dup, L)
        u = up[...].astype(BF16)
        dy = dyp[...].astype(BF16)
        ar, ai = ar_ref[...], ai_ref[...]
        for b in range(NB):
            lanes = pl.ds(b * 128, 128)
            sr[b, pl.ds(PAD, L), :] = jnp.dot(u, wbr_ref[0, :, lanes], preferred_element_type=F32)
            si[b, pl.ds(PAD, L), :] = jnp.dot(u, wbi_ref[0, :, lanes], preferred_element_type=F32)
        _scan_segs(ar, ai, sr, si, er, ei, cr, cim, L, 1, False)
        for b in range(NB):
            lanes = pl.ds(b * 128, 128)
            sr[b, pl.ds(0, PAD), :] = cr[b]
            si[b, pl.ds(0, PAD), :] = cim[b]
            lr[b] = lax.dot_general(dy, cbr_ref[0, lanes, :], nt, preferred_element_type=F32)
            li[b] = -lax.dot_general(dy, cbi_ref[0, lanes, :], nt, preferred_element_type=F32)
            dcbr_ref[0, lanes, :] = lax.dot_general(sr[b, pl.ds(PAD, L), :].astype(BF16), dy, tn, preferred_element_type=F32)
            dcbi_ref[0, lanes, :] = -lax.dot_general(si[b, pl.ds(PAD, L), :].astype(BF16), dy, tn, preferred_element_type=F32)
        _scan_segs(ar, -ai, lr, li, er, ei, cr, cim, L, 0, True)

        def da_step(j, carry):
            rows = pl.ds(pl.multiple_of(S5_SEG * j, S5_SEG), S5_SEG)
            out = []
            for b in range(NB):
                dar, dai = carry[b]
                pr_, pi_ = sr[b, rows, :], si[b, rows, :]
                gr, gi = lr[b, rows, :], li[b, rows, :]
                out.append((dar + (gr * pr_ + gi * pi_), dai + (gi * pr_ - gr * pi_)))
            return tuple(out)

        z8 = jnp.zeros((S5_SEG, 128), F32)
        acc = lax.fori_loop(0, S, da_step, tuple((z8, z8) for _ in range(NB)), unroll=4)
        du = dup[...]
        for b in range(NB):
            lanes = pl.ds(b * 128, 128)
            dar_ref[:, lanes] = jnp.sum(acc[b][0], axis=0, keepdims=True)
            dai_ref[:, lanes] = jnp.sum(acc[b][1], axis=0, keepdims=True)
            gr, gi = lr[b].astype(BF16), li[b].astype(BF16)
            du = du + (lax.dot_general(gr, wbr_ref[0, :, lanes], nt, preferred_element_type=F32)
                       + lax.dot_general(gi, wbi_ref[0, :, lanes], nt, preferred_element_type=F32))
            dwbr_ref[0, :, lanes] = lax.dot_general(u, gr, tn, preferred_element_type=F32)
            dwbi_ref[0, :, lanes] = lax.dot_general(u, gi, tn, preferred_element_type=F32)
        dup[...] = du

        def write(row, val):
            du_ref[pl.ds(row, 16), :] = val.astype(BF16)

        _from_segs(dup, L, write)

    wspec = pl.BlockSpec((1, ci, cs), lambda j: (j, 0, 0))
    aspec = pl.BlockSpec((1, cs), lambda j: (0, j))
    cspec = pl.BlockSpec((1, cs, ci), lambda j: (j, 0, 0))
    col = pl.BlockSpec((L, ci), lambda j: (0, j))
    return pl.pallas_call(
        body, name="s5_core_bwd", grid=(nb,),
        in_specs=[pl.BlockSpec((L, ci), lambda j: (0, OFF_US // ci + j)), wspec, wspec, aspec, aspec, cspec, cspec, col, col],
        out_specs=[col, wspec, wspec, cspec, cspec, aspec, aspec],
        out_shape=[_sds((L, nb * ci), BF16), _sds(wbr.shape), _sds(wbr.shape), _sds(cbr.shape), _sds(cbr.shape),
                   _sds((1, nb * cs)), _sds((1, nb * cs))],
        scratch_shapes=(_s5_seg_scratch(L, cs, PAD) + [pltpu.VMEM((NB, L, 128), F32), pltpu.VMEM((NB, L, 128), F32)]
                        + [pltpu.VMEM((L, ci), F32) for _ in range(3)]),
        compiler_params=_cp(("arbitrary",)),
    )(proj, wbr, wbi, a_re, a_im, cbr, cbi, dyc, du1)


def _s5_post_math(yc, u, z, d, wg):
    y = yc + d * u
    y1 = jax.nn.gelu(y)
    t = jnp.dot(y1.astype(BF16), wg, preferred_element_type=F32)
    sg = _sigmoid(t)
    return y, y1, sg


def _s5_post_fwd(yc, proj, d, wg):
    L, W = yc.shape
    tm = min(256, L)

    def body(yc_ref, u_ref, z_ref, d_ref, wg_ref, o_ref):
        _, y1, sg = _s5_post_math(yc_ref[...], u_ref[...], z_ref[...], d_ref[...], wg_ref[...])
        o_ref[...] = (y1 * sg * _silu(z_ref[...])).astype(BF16)

    row = pl.BlockSpec((tm, W), lambda i: (i, 0))
    return pl.pallas_call(
        body, name="s5_post_fwd", grid=(L // tm,),
        in_specs=[row, pl.BlockSpec((tm, W), lambda i: (i, OFF_US // W)), pl.BlockSpec((tm, W), lambda i: (i, OFF_ZS // W)),
                  pl.BlockSpec((1, W), lambda i: (0, 0)), pl.BlockSpec((W, W), lambda i: (0, 0))],
        out_specs=row, out_shape=_sds((L, W), BF16), compiler_params=_cp(("parallel",)),
    )(yc, proj, proj, d, wg)


def _s5_post_bwd(yc, proj, d, wg, dout):
    L, W = yc.shape
    tm = min(256, L)

    def body(yc_ref, u_ref, z_ref, d_ref, wg_ref, do_ref, dyc_ref, du_ref, dz_ref, dd_ref, dwg_ref):
        i = pl.program_id(0)
        u, z, d_, wgv = u_ref[...], z_ref[...], d_ref[...], wg_ref[...]
        y, y1, sg = _s5_post_math(yc_ref[...], u, z, d_, wgv)
        dout_ = do_ref[...]
        y2 = y1 * sg
        dy2 = dout_ * _silu(z)
        dz_ref[...] = (dout_ * y2 * _dsilu(z)).astype(BF16)
        dt = (dy2 * y1 * sg * (1.0 - sg)).astype(BF16)
        dy1 = dy2 * sg + lax.dot_general(dt, wgv, (((1,), (1,)), ((), ())), preferred_element_type=F32)
        _, gelu_vjp = jax.vjp(jax.nn.gelu, y)
        dy = gelu_vjp(dy1)[0]
        dyc_ref[...] = dy
        du_ref[...] = dy * d_
        dd_part = jnp.sum(dy * u, axis=0, keepdims=True)
        dwg_part = lax.dot_general(y1.astype(BF16), dt, (((0,), (0,)), ((), ())), preferred_element_type=F32)

        @pl.when(i == 0)
        def _():
            dd_ref[...] = dd_part
            dwg_ref[...] = dwg_part

        @pl.when(i > 0)
        def _():
            dd_ref[...] += dd_part
            dwg_ref[...] += dwg_part

    row = pl.BlockSpec((tm, W), lambda i: (i, 0))
    return pl.pallas_call(
        body, name="s5_post_bwd", grid=(L // tm,),
        in_specs=[row, pl.BlockSpec((tm, W), lambda i: (i, OFF_US // W)), pl.BlockSpec((tm, W), lambda i: (i, OFF_ZS // W)),
                  pl.BlockSpec((1, W), lambda i: (0, 0)), pl.BlockSpec((W, W), lambda i: (0, 0)), row],
        out_specs=[row, row, row, pl.BlockSpec((1, W), lambda i: (0, 0)), pl.BlockSpec((W, W), lambda i: (0, 0))],
        out_shape=[_sds((L, W)), _sds((L, W)), _sds((L, W), BF16), _sds((1, W)), _sds((W, W))],
        compiler_params=_cp(("arbitrary",)),
    )(yc, proj, proj, d, wg, dout)


def _shift_down(x, s):
    if s == 0:
        return x
    rows = lax.broadcasted_iota(jnp.int32, x.shape, 0)
    return jnp.where(rows >= s, pltpu.roll(x, s, 0), 0.0)


def _shift_up(x, s):
    if s == 0:
        return x
    L = x.shape[0]
    rows = lax.broadcasted_iota(jnp.int32, x.shape, 0)
    return jnp.where(rows < L - s, pltpu.roll(x, L - s, 0), 0.0)


def _conv_pre(x, w):
    acc = w[CONV_K - 1:CONV_K, :] * x
    for s in range(1, CONV_K):
        acc = acc + w[CONV_K - 1 - s:CONV_K - s, :] * _shift_down(x, s)
    return acc


def _dn_conv_fwd(proj, conv_w):
    L = proj.shape[0]
    W = DN_HEAD_DIM
    nq = 2 * DN_HEADS

    def body(x_ref, w_ref, o_ref):
        j = pl.program_id(0)
        act = _silu(_conv_pre(x_ref[...], w_ref[...]))
        r = lax.rsqrt(jnp.sum(act * act, axis=-1, keepdims=True) + EPS)
        scale = jnp.where(j < DN_HEADS, DN_HEAD_DIM ** -0.5, 1.0)
        o_ref[...] = jnp.where(j < nq, act * r * scale, act)

    return pl.pallas_call(
        body, name="dn_conv_fwd", grid=(3 * DN_HEADS,),
        in_specs=[pl.BlockSpec((L, W), lambda j: (0, OFF_Q // W + j)), pl.BlockSpec((CONV_K, W), lambda j: (0, j))],
        out_specs=pl.BlockSpec((L, W), lambda j: (0, j)), out_shape=_sds((L, 3 * D_DN)), compiler_params=_cp(("parallel",)),
    )(proj, conv_w)


def _dn_conv_bwd(proj, conv_w, dout):
    L = proj.shape[0]
    W = DN_HEAD_DIM
    nq = 2 * DN_HEADS

    def body(x_ref, w_ref, do_ref, dx_ref, dw_ref):
        j = pl.program_id(0)
        x, w, dout_ = x_ref[...], w_ref[...], do_ref[...]
        pre = _conv_pre(x, w)
        act = _silu(pre)
        r = lax.rsqrt(jnp.sum(act * act, axis=-1, keepdims=True) + EPS)
        scale = jnp.where(j < DN_HEADS, DN_HEAD_DIM ** -0.5, 1.0)
        g = dout_ * scale
        dact_n = r * g - act * (r * r * r) * jnp.sum(g * act, axis=-1, keepdims=True)
        dact = jnp.where(j < nq, dact_n, dout_)
        dpre = dact * _dsilu(pre)
        dx = w[CONV_K - 1:CONV_K, :] * dpre
        for s in range(1, CONV_K):
            dx = dx + w[CONV_K - 1 - s:CONV_K - s, :] * _shift_up(dpre, s)
        dx_ref[...] = dx.astype(BF16)
        for s in range(CONV_K):
            dw_ref[pl.ds(CONV_K - 1 - s, 1), :] = jnp.sum(dpre * _shift_down(x, s), axis=0, keepdims=True)

    col = pl.BlockSpec((L, W), lambda j: (0, j))
    wsp = pl.BlockSpec((CONV_K, W), lambda j: (0, j))
    return pl.pallas_call(
        body, name="dn_conv_bwd", grid=(3 * DN_HEADS,),
        in_specs=[pl.BlockSpec((L, W), lambda j: (0, OFF_Q // W + j)), wsp, col], out_specs=[col, wsp],
        out_shape=[_sds((L, 3 * D_DN), BF16), _sds((CONV_K, 3 * D_DN))], compiler_params=_cp(("parallel",)),
    )(proj, conv_w, dout)


def _softplus(x):
    return jnp.maximum(x, 0.0) + jnp.log(1.0 + jnp.exp(-jnp.abs(x)))


def _dn_gates_fwd(proj, alog, dtb):
    L = proj.shape[0]
    W = 128

    def body(p_ref, al_ref, db_ref, o_ref):
        p = p_ref[...]
        lane = lax.broadcasted_iota(jnp.int32, p.shape, 1)
        g = -jnp.exp(al_ref[...]) * _softplus(p + db_ref[...])
        o_ref[...] = jnp.where(lane < DN_HEADS, _sigmoid(p), jnp.where(lane < 2 * DN_HEADS, g, 0.0))

    return pl.pallas_call(
        body, name="dn_gates_fwd", grid=(1,),
        in_specs=[pl.BlockSpec((L, W), lambda i: (0, OFF_B // W)), pl.BlockSpec((1, W), lambda i: (0, 0)),
                  pl.BlockSpec((1, W), lambda i: (0, 0))],
        out_specs=pl.BlockSpec((L, W), lambda i: (0, 0)), out_shape=_sds((L, W)), compiler_params=_cp(("arbitrary",)),
    )(proj, alog, dtb)


def _dn_gates_bwd(proj, alog, dtb, dgates):
    L = proj.shape[0]
    W = 128

    def body(p_ref, al_ref, db_ref, dg_ref, dp_ref, dal_ref, ddb_ref):
        p, dg = p_ref[...], dg_ref[...]
        lane = lax.broadcasted_iota(jnp.int32, p.shape, 1)
        is_g = jnp.logical_and(lane >= DN_HEADS, lane < 2 * DN_HEADS)
        beta = _sigmoid(p)
        na = -jnp.exp(al_ref[...])
        xs = p + db_ref[...]
        dsp = dg * na * _sigmoid(xs)
        dp_ref[...] = jnp.where(lane < DN_HEADS, dg * beta * (1.0 - beta), jnp.where(is_g, dsp, 0.0)).astype(BF16)
        dal_ref[...] = jnp.sum(jnp.where(is_g, dg * na * _softplus(xs), 0.0), axis=0, keepdims=True)
        ddb_ref[...] = jnp.sum(jnp.where(is_g, dsp, 0.0), axis=0, keepdims=True)

    one = pl.BlockSpec((1, W), lambda i: (0, 0))
    full = pl.BlockSpec((L, W), lambda i: (0, 0))
    return pl.pallas_call(
        body, name="dn_gates_bwd", grid=(1,),
        in_specs=[pl.BlockSpec((L, W), lambda i: (0, OFF_B // W)), one, one, full], out_specs=[full, one, one],
        out_shape=[_sds((L, W), BF16), _sds((1, W)), _sds((1, W))], compiler_params=_cp(("arbitrary",)),
    )(proj, alog, dtb, dgates)


def _bdot(a, b, dims):
    return lax.dot_general(a.astype(BF16), b.astype(BF16), (dims, ((), ())), preferred_element_type=F32)


_NN, _NT, _TN = ((1,), (0,)), ((1,), (1,)), ((0,), (0,))


def _dot3(a, b, dims):
    ah, bh = a.astype(BF16), b.astype(BF16)
    al, bl = (a - ah.astype(F32)).astype(BF16), (b - bh.astype(F32)).astype(BF16)
    (ca,), (cb,) = dims
    a3 = jnp.concatenate([ah, ah, al], axis=ca)
    b3 = jnp.concatenate([bh, bl, bh], axis=cb)
    return lax.dot_general(a3, b3, (dims, ((), ())), preferred_element_type=F32)


def _mm_family(raw):
    nn = jax.custom_vjp(lambda a, b: raw(a, b, _NN))
    nt = jax.custom_vjp(lambda a, b: raw(a, b, _NT))
    tn = jax.custom_vjp(lambda a, b: raw(a, b, _TN))
    nn.defvjp(lambda a, b: (raw(a, b, _NN), (a, b)), lambda r, g: (raw(g, r[1], _NT), raw(r[0], g, _TN)))
    nt.defvjp(lambda a, b: (raw(a, b, _NT), (a, b)), lambda r, g: (raw(g, r[1], _NN), raw(g, r[0], _TN)))
    tn.defvjp(lambda a, b: (raw(a, b, _TN), (a, b)), lambda r, g: (raw(r[1], g, _NT), raw(r[0], g, _NN)))
    return nn, nt, tn


_mm_nn, _mm_nt, _mm_tn = _mm_family(_bdot)
_m3_nn, _m3_nt, _m3_tn = _mm_family(_dot3)


def _tri_apply(x, upper):
    C = x.shape[0]
    ii = lax.broadcasted_iota(jnp.int32, (C, 3 * C), 0)
    jj = lax.broadcasted_iota(jnp.int32, (C, 3 * C), 1) % C
    mat = ((ii <= jj) if upper else (ii >= jj)).astype(BF16)
    hi = x.astype(BF16)
    r = x - hi.astype(F32)
    mid = r.astype(BF16)
    lo = (r - mid.astype(F32)).astype(BF16)
    return jnp.dot(mat, jnp.concatenate([hi, mid, lo], axis=0), preferred_element_type=F32)


_cumsum_rows = jax.custom_vjp(lambda x: _tri_apply(x, False))
_cumsum_rows.defvjp(lambda x: (_tri_apply(x, False), None), lambda _, g: (_tri_apply(g, True),))


def _uli(a_s):
    C = a_s[0].shape[0]
    ii = lax.broadcasted_iota(jnp.int32, (C, C), 0)
    jj = lax.broadcasted_iota(jnp.int32, (C, C), 1)
    eye = jnp.where(ii == jj, 1.0, 0.0)
    ts = [eye - a for a in a_s]
    ms = list(a_s)
    for _ in range(int(math.log2(C)) - 1):
        ms = [_dot3(m, m, _NN) for m in ms]
        ts = [t + _dot3(t, m, _NN) for t, m in zip(ts, ms)]
    return tuple(ts)


def _uli_bwd(ts, gs):
    xs = [_dot3(t, g, _TN) for t, g in zip(ts, gs)]
    return (tuple(-_dot3(x, t, _NT) for x, t in zip(xs, ts)),)


_unit_lower_inverse = jax.custom_vjp(_uli)
_unit_lower_inverse.defvjp(lambda a_s: (lambda ts: (ts, ts))(_uli(a_s)), _uli_bwd)


def _prep_math(qs, ks, vs, gcols, bcols):
    n = len(qs)
    C, dv = vs[0].shape
    ii = lax.broadcasted_iota(jnp.int32, (C, C), 0)
    jj = lax.broadcasted_iota(jnp.int32, (C, C), 1)
    causal = ii >= jj
    strict = ii > jj
    sf = strict.astype(F32)
    ones = jnp.ones((C, dv), F32)
    dms = [_cumsum_rows(g * sf) for g in gcols]
    gcbs = [_cumsum_rows(g * ones) for g in gcols]
    kks = [_mm_nt(k, k) for k in ks]
    qks = [_mm_nt(q, k) for q, k in zip(qs, ks)]
    decays = [jnp.where(causal, jnp.exp(jnp.where(causal, dm, 0.0)), 0.0) for dm in dms]
    glasts = [jnp.sum(g * ones, axis=0, keepdims=True) for g in gcols]
    egs = [jnp.exp(gcb) for gcb in gcbs]
    ts = _unit_lower_inverse(tuple(jnp.where(strict, b * kk * dc, 0.0) for b, kk, dc in zip(bcols, kks, decays)))
    us = [_m3_nn(t, v * b) for t, v, b in zip(ts, vs, bcols)]
    ws = [_m3_nn(t, k * b * eg) for t, k, b, eg in zip(ts, ks, bcols, egs)]
    return tuple((us[i], ws[i], qs[i] * egs[i], ks[i] * jnp.exp(glasts[i] - gcbs[i]), qks[i] * decays[i],
                  jnp.exp(glasts[i])) for i in range(n))


def _gate_cols(gates, h):
    lane = lax.broadcasted_iota(jnp.int32, gates.shape, 1)
    bcol = jnp.sum(jnp.where(lane == h, gates, 0.0), axis=1, keepdims=True)
    gcol = jnp.sum(jnp.where(lane == h + DN_HEADS, gates, 0.0), axis=1, keepdims=True)
    return gcol, bcol


DN_HB = 8


def _dn_prep_fwd(qkv, gates):
    L = qkv.shape[0]
    N, H, d, HB = L // CHUNK, DN_HEADS, DN_HEAD_DIM, DN_HB

    def body(q_ref, k_ref, v_ref, g_ref, u_ref, w_ref, qd_ref, kd_ref, qk_ref, egl_ref):
        h0 = pl.program_id(1) * HB
        gates_ = g_ref[...]
        lanes_of = [pl.ds(i * d, d) for i in range(HB)]
        cols = [_gate_cols(gates_, h0 + i) for i in range(HB)]
        outs = _prep_math([q_ref[:, l] for l in lanes_of], [k_ref[:, l] for l in lanes_of], [v_ref[:, l] for l in lanes_of],
                          [c[0] for c in cols], [c[1] for c in cols])
        for i in range(HB):
            lanes = lanes_of[i]
            u, w, qd, kd, qk, egl = outs[i]
            u_ref[:, lanes] = u
            w_ref[:, lanes] = w
            qd_ref[:, lanes] = qd
            kd_ref[:, lanes] = kd
            qk_ref[0, i] = qk
            egl_ref[0, i] = jnp.broadcast_to(egl, (8, d))

    blk = lambda off: pl.BlockSpec((CHUNK, HB * d), lambda n, j: (n, off // HB + j))
    cc = pl.BlockSpec((1, HB, CHUNK, CHUNK), lambda n, j: (n, j, 0, 0))
    ee = pl.BlockSpec((1, HB, 8, d), lambda n, j: (n, j, 0, 0))
    big = _sds((L, D_DN))
    return pl.pallas_call(
        body, name="dn_prep_fwd", grid=(N, H // HB),
        in_specs=[blk(0), blk(H), blk(2 * H), pl.BlockSpec((CHUNK, 128), lambda n, j: (n, 0))],
        out_specs=[blk(0), blk(0), blk(0), blk(0), cc, ee],
        out_shape=[big, big, big, big, _sds((N, H, CHUNK, CHUNK)), _sds((N, H, 8, d))],
        compiler_params=_cp(("parallel", "parallel")),
    )(qkv, qkv, qkv, gates)


def _dn_scan_fwd(u, w, qd, kd, qk, egl):
    L = u.shape[0]
    N, H, d, HB = L // CHUNK, DN_HEADS, DN_HEAD_DIM, DN_HB

    def body(u_ref, w_ref, qd_ref, kd_ref, qk_ref, egl_ref, o_ref, st_ref, s_ref):
        n, h0 = pl.program_id(0), pl.program_id(1) * HB

        @pl.when(n == 0)
        def _():
            for i in range(HB):
                s_ref[h0 + i] = jnp.zeros((d, d), F32)

        hs = range(HB)
        ln = [pl.ds(i * d, d) for i in hs]
        st = [s_ref[h0 + i] for i in hs]
        ws = [_bdot(w_ref[:, ln[i]], st[i], _NN) for i in hs]
        qs = [_bdot(qd_ref[:, ln[i]], st[i], _NN) for i in hs]
        vn = [u_ref[:, ln[i]] - ws[i] for i in hs]
        qv = [_bdot(qk_ref[0, i], vn[i], _NN) for i in hs]
        kv = [_bdot(kd_ref[:, ln[i]], vn[i], _TN) for i in hs]
        for i in hs:
            st_ref[0, i] = st[i]
            o_ref[:, ln[i]] = qs[i] + qv[i]
            s_ref[h0 + i] = st[i] * egl_ref[0, i, pl.ds(0, 1), :] + kv[i]

    blk = pl.BlockSpec((CHUNK, HB * d), lambda n, j: (n, j))
    cc = pl.BlockSpec((1, HB, CHUNK, CHUNK), lambda n, j: (n, j, 0, 0))
    ee = pl.BlockSpec((1, HB, 8, d), lambda n, j: (n, j, 0, 0))
    return pl.pallas_call(
        body, name="dn_scan_fwd", grid=(N, H // HB), in_specs=[blk, blk, blk, blk, cc, ee],
        out_specs=[blk, pl.BlockSpec((1, HB, d, d), lambda n, j: (n, j, 0, 0))],
        out_shape=[_sds((L, D_DN)), _sds((N, H, d, d))], scratch_shapes=[pltpu.VMEM((H, d, d), F32)],
        compiler_params=_cp(("arbitrary", "arbitrary")),
    )(u, w, qd, kd, qk, egl)


def _dn_scan_bwd(u, w, qd, kd, qk, egl, states, do):
    L = u.shape[0]
    N, H, d, HB = L // CHUNK, DN_HEADS, DN_HEAD_DIM, DN_HB

    def body(u_ref, w_ref, qd_ref, kd_ref, qk_ref, egl_ref, st_ref, do_ref,
             du_ref, dw_ref, dqd_ref, dkd_ref, dqk_ref, degl_ref, ds_ref):
        n, h0 = pl.program_id(0), pl.program_id(1) * HB

        @pl.when(n == 0)
        def _():
            for i in range(HB):
                ds_ref[h0 + i] = jnp.zeros((d, d), F32)

        hs = range(HB)
        ln = [pl.ds(i * d, d) for i in hs]
        st = [st_ref[0, i] for i in hs]
        dsn = [ds_ref[h0 + i] for i in hs]
        do_ = [do_ref[:, ln[i]] for i in hs]
        ws = [_bdot(w_ref[:, ln[i]], st[i], _NN) for i in hs]
        d1 = [_bdot(qk_ref[0, i], do_[i], _TN) for i in hs]
        d2 = [_bdot(kd_ref[:, ln[i]], dsn[i], _NN) for i in hs]
        dqd = [_bdot(do_[i], st[i], _NT) for i in hs]
        qdo = [_bdot(qd_ref[:, ln[i]], do_[i], _TN) for i in hs]
        vn = [u_ref[:, ln[i]] - ws[i] for i in hs]
        dvn = [d1[i] + d2[i] for i in hs]
        dw = [_bdot(dvn[i], st[i], _NT) for i in hs]
        dkd = [_bdot(vn[i], dsn[i], _NT) for i in hs]
        dqk = [_bdot(do_[i], vn[i], _NT) for i in hs]
        wdv = [_bdot(w_ref[:, ln[i]], dvn[i], _TN) for i in hs]
        for i in hs:
            du_ref[:, ln[i]] = dvn[i]
            dw_ref[:, ln[i]] = -dw[i]
            dqd_ref[:, ln[i]] = dqd[i]
            dkd_ref[:, ln[i]] = dkd[i]
            dqk_ref[0, i] = dqk[i]
            degl_ref[0, i] = jnp.broadcast_to(jnp.sum(dsn[i] * st[i], keepdims=True), (8, d))
            ds_ref[h0 + i] = (qdo[i] - wdv[i]) + dsn[i] * egl_ref[0, i, pl.ds(0, 1), :]

    blk = pl.BlockSpec((CHUNK, HB * d), lambda n, j: (N - 1 - n, j))
    cc = pl.BlockSpec((1, HB, CHUNK, CHUNK), lambda n, j: (N - 1 - n, j, 0, 0))
    ee = pl.BlockSpec((1, HB, 8, d), lambda n, j: (N - 1 - n, j, 0, 0))
    ss = pl.BlockSpec((1, HB, d, d), lambda n, j: (N - 1 - n, j, 0, 0))
    big = _sds((L, D_DN))
    return pl.pallas_call(
        body, name="dn_scan_bwd", grid=(N, H // HB), in_specs=[blk, blk, blk, blk, cc, ee, ss, blk],
        out_specs=[blk, blk, blk, blk, cc, ee],
        out_shape=[big, big, big, big, _sds((N, H, CHUNK, CHUNK)), _sds((N, H, 8, d))],
        scratch_shapes=[pltpu.VMEM((H, d, d), F32)], compiler_params=_cp(("arbitrary", "arbitrary")),
    )(u, w, qd, kd, qk, egl, states, do)


def _dn_prep_bwd(qkv, gates, du, dw, dqd, dkd, dqk, degl):
    L = qkv.shape[0]
    N, H, d, HB = L // CHUNK, DN_HEADS, DN_HEAD_DIM, DN_HB

    def body(q_ref, k_ref, v_ref, g_ref, du_ref, dw_ref, dqd_ref, dkd_ref, dqk_ref, degl_ref, dq_ref, dk_ref, dv_ref, dg_ref):
        j = pl.program_id(1)
        h0 = j * HB
        gates_ = g_ref[...]
        lane = lax.broadcasted_iota(jnp.int32, gates_.shape, 1)
        lane1 = lax.broadcasted_iota(jnp.int32, (1, d), 1)
        part = jnp.zeros(gates_.shape, F32)
        lanes_of = [pl.ds(i * d, d) for i in range(HB)]
        cols = [_gate_cols(gates_, h0 + i) for i in range(HB)]
        _, f = jax.vjp(_prep_math, [q_ref[:, l] for l in lanes_of], [k_ref[:, l] for l in lanes_of],
                       [v_ref[:, l] for l in lanes_of], [c[0] for c in cols], [c[1] for c in cols])
        cots = tuple((du_ref[:, l], dw_ref[:, l], dqd_ref[:, l], dkd_ref[:, l], dqk_ref[0, i],
                      jnp.where(lane1 == 0, degl_ref[0, i, pl.ds(0, 1), :], 0.0)) for i, l in enumerate(lanes_of))
        dqs, dks, dvs, dgcs, dbcs = f(cots)
        for i in range(HB):
            lanes = lanes_of[i]
            dq_ref[:, lanes] = dqs[i]
            dk_ref[:, lanes] = dks[i]
            dv_ref[:, lanes] = dvs[i]
            part = part + jnp.where(lane == h0 + i, dbcs[i], 0.0) + jnp.where(lane == h0 + i + DN_HEADS, dgcs[i], 0.0)

        @pl.when(j == 0)
        def _():
            dg_ref[...] = part

        @pl.when(j > 0)
        def _():
            dg_ref[...] += part

    blk = lambda off: pl.BlockSpec((CHUNK, HB * d), lambda n, j: (n, off // HB + j))
    gsp = pl.BlockSpec((CHUNK, 128), lambda n, j: (n, 0))
    cc = pl.BlockSpec((1, HB, CHUNK, CHUNK), lambda n, j: (n, j, 0, 0))
    ee = pl.BlockSpec((1, HB, 8, d), lambda n, j: (n, j, 0, 0))
    big = _sds((L, D_DN))
    return pl.pallas_call(
        body, name="dn_prep_bwd", grid=(N, H // HB),
        in_specs=[blk(0), blk(H), blk(2 * H), gsp, blk(0), blk(0), blk(0), blk(0), cc, ee],
        out_specs=[blk(0), blk(0), blk(0), gsp], out_shape=[big, big, big, _sds((L, 128))],
        compiler_params=_cp(("parallel", "arbitrary")),
    )(qkv, qkv, qkv, gates, du, dw, dqd, dkd, dqk, degl)


def _dn_post_fwd(o, proj, nw):
    L = o.shape[0]
    d = DN_HEAD_DIM
    tm = min(512, L)

    def body(o_ref, z_ref, w_ref, y_ref):
        ov = o_ref[...]
        r = lax.rsqrt(jnp.mean(ov * ov, axis=-1, keepdims=True) + EPS)
        y_ref[...] = (ov * r * w_ref[...] * _silu(z_ref[...])).astype(BF16)

    blk = pl.BlockSpec((tm, d), lambda i, h: (i, h))
    return pl.pallas_call(
        body, name="dn_post_fwd", grid=(L // tm, DN_HEADS),
        in_specs=[blk, pl.BlockSpec((tm, d), lambda i, h: (i, OFF_ZD // d + h)), pl.BlockSpec((1, d), lambda i, h: (0, 0))],
        out_specs=blk, out_shape=_sds((L, D_DN), BF16), compiler_params=_cp(("parallel", "parallel")),
    )(o, proj, nw)


def _dn_post_bwd(o, proj, nw, dy):
    L = o.shape[0]
    d = DN_HEAD_DIM
    tm = min(512, L)

    def body(o_ref, z_ref, w_ref, dy_ref, do_ref, dz_ref, dw_ref):
        first = jnp.logical_and(pl.program_id(0) == 0, pl.program_id(1) == 0)
        ov, z, w, dyv = o_ref[...], z_ref[...], w_ref[...], dy_ref[...]
        r = lax.rsqrt(jnp.mean(ov * ov, axis=-1, keepdims=True) + EPS)
        xn = ov * r
        dz_ref[...] = (dyv * xn * w * _dsilu(z)).astype(BF16)
        dn = dyv * _silu(z)
        t = dn * w
        do_ref[...] = r * t - ov * (r * r * r) * jnp.mean(t * ov, axis=-1, keepdims=True)
        part = jnp.sum(dn * xn, axis=0, keepdims=True)

        @pl.when(first)
        def _():
            dw_ref[...] = part

        @pl.when(jnp.logical_not(first))
        def _():
            dw_ref[...] += part

    blk = pl.BlockSpec((tm, d), lambda i, h: (i, h))
    one = pl.BlockSpec((1, d), lambda i, h: (0, 0))
    return pl.pallas_call(
        body, name="dn_post_bwd", grid=(L // tm, DN_HEADS),
        in_specs=[blk, pl.BlockSpec((tm, d), lambda i, h: (i, OFF_ZD // d + h)), one, blk], out_specs=[blk, blk, one],
        out_shape=[_sds((L, D_DN)), _sds((L, D_DN), BF16), _sds((1, d))], compiler_params=_cp(("arbitrary", "arbitrary")),
    )(o, proj, nw, dy)


def _mix_fwd(s5o, dno, w_su, w_du, proj):
    L, K = s5o.shape
    N = w_su.shape[1]
    tm, tn = min(512, L), 512

    def body(a1, a2, b1, b2, gs, gd, ys_ref, yd_ref, mx_ref):
        ys = jnp.dot(a1[...], b1[...], preferred_element_type=F32)
        yd = jnp.dot(a2[...], b2[...], preferred_element_type=F32)
        ys_ref[...] = ys
        yd_ref[...] = yd
        mx_ref[...] = (_sigmoid(gs[...]) * ys + _sigmoid(gd[...]) * yd).astype(BF16)

    a = pl.BlockSpec((tm, K), lambda i, j: (i, 0))
    b = pl.BlockSpec((K, tn), lambda i, j: (0, j))
    o = pl.BlockSpec((tm, tn), lambda i, j: (i, j))
    return pl.pallas_call(
        body, name="mix_fwd", grid=(L // tm, N // tn),
        in_specs=[a, a, b, b, pl.BlockSpec((tm, tn), lambda i, j: (i, OFF_GS // tn + j)),
                  pl.BlockSpec((tm, tn), lambda i, j: (i, OFF_GD // tn + j))],
        out_specs=[o, o, o], out_shape=[_sds((L, N)), _sds((L, N)), _sds((L, N), BF16)],
        compiler_params=_cp(("parallel", "parallel")),
    )(s5o, dno, w_su, w_du, proj, proj)


def _mix_bwd(dx2b, w_out, proj, ys, yd):
    L, K = dx2b.shape
    N = w_out.shape[0]
    tm, tn = min(512, L), 512

    def body(a, b, gs, gd, ys_ref, yd_ref, dgs_ref, dgd_ref, dys_ref, dyd_ref):
        dm = lax.dot_general(a[...], b[...], (((1,), (1,)), ((), ())), preferred_element_type=F32)
        ss, sd = _sigmoid(gs[...]), _sigmoid(gd[...])
        dys_ref[...] = (dm * ss).astype(BF16)
        dyd_ref[...] = (dm * sd).astype(BF16)
        dgs_ref[...] = (dm * ys_ref[...] * ss * (1.0 - ss)).astype(BF16)
        dgd_ref[...] = (dm * yd_ref[...] * sd * (1.0 - sd)).astype(BF16)

    o = pl.BlockSpec((tm, tn), lambda i, j: (i, j))
    return pl.pallas_call(
        body, name="mix_bwd", grid=(L // tm, N // tn),
        in_specs=[pl.BlockSpec((tm, K), lambda i, j: (i, 0)), pl.BlockSpec((tn, K), lambda i, j: (j, 0)),
                  pl.BlockSpec((tm, tn), lambda i, j: (i, OFF_GS // tn + j)),
                  pl.BlockSpec((tm, tn), lambda i, j: (i, OFF_GD // tn + j)), o, o],
        out_specs=[o, o, o, o], out_shape=[_sds((L, N), BF16)] * 4, compiler_params=_cp(("parallel", "parallel")),
    )(dx2b, w_out, proj, proj, ys, yd)


def _final(mixed, w_out, x, tgt, fw):
    L, D = x.shape
    tm = min(256, L)

    def body(a_ref, b_ref, x_ref, t_ref, w_ref, dx_ref, dxb_ref, loss_ref, dw_ref):
        i = pl.program_id(0)
        x2 = x_ref[...] + jnp.dot(a_ref[...], b_ref[...], preferred_element_type=F32)
        w = w_ref[...]
        r = lax.rsqrt(jnp.mean(x2 * x2, axis=-1, keepdims=True) + EPS)
        xn = x2 * r
        e = xn * w - t_ref[...]
        lpart = 0.5 * jnp.sum(jnp.mean(e * e, axis=-1, keepdims=True), axis=0, keepdims=True)
        dy = e * (1.0 / D)
        t = dy * w
        dx2 = r * t - x2 * (r * r * r) * jnp.mean(t * x2, axis=-1, keepdims=True)
        dx_ref[...] = dx2
        dxb_ref[...] = dx2.astype(BF16)
        dwp = jnp.sum(dy * xn, axis=0, keepdims=True)
        lrow = jnp.broadcast_to(lpart, loss_ref.shape)

        @pl.when(i == 0)
        def _():
            loss_ref[...] = lrow
            dw_ref[...] = dwp

        @pl.when(i > 0)
        def _():
            loss_ref[...] += lrow
            dw_ref[...] += dwp

    row = pl.BlockSpec((tm, D), lambda i: (i, 0))
    one = pl.BlockSpec((1, D), lambda i: (0, 0))
    return pl.pallas_call(
        body, name="final", grid=(L // tm,),
        in_specs=[row, pl.BlockSpec((D, D), lambda i: (0, 0)), row, row, one],
        out_specs=[row, row, pl.BlockSpec((1, 128), lambda i: (0, 0)), one],
        out_shape=[_sds((L, D)), _sds((L, D), BF16), _sds((1, 128)), _sds((1, D))], compiler_params=_cp(("arbitrary",)),
    )(mixed, w_out, x, tgt, fw)


def _block_diag(t):
    J, g, a, b = t.shape
    eye = jnp.eye(g, dtype=t.dtype)
    return (t[:, :, :, None, :] * eye[None, :, None, :, None]).reshape(J, g * a, g * b)


def _block_diag_take(m, g):
    J, ga, gb = m.shape
    a, b = ga // g, gb // g
    m5 = m.reshape(J, g, a, g, b)
    idx = jnp.arange(g)
    return m5[:, idx, :, idx, :].transpose(1, 0, 2, 3)


class _PlainOps:
    def __init__(self, w_rest):
        self.w_rest = w_rest

    def in_proj(self, h, wt_perm):
        return _mm(h, wt_perm, tb=True, name="in_proj", tm=1024, tn=1152), self.w_rest

    def rest_grads(self, d_w_glu, d_w_su, d_w_du, d_w_out):
        pass

    def d_w_in(self, h, dproj):
        return _mm(dproj, h, ta=True, name="d_w_in", tm=1152, tn=1024)

    def d_h(self, dproj, wt_perm, d_wt_perm):
        return _mm(dproj, wt_perm, name="d_h", tm=2048, tn=1024, tk=1152)


def _local_step(x, tgt, ln_w, w_perm, lam_re, lam_im, log_step, b_re, b_im, c_re, c_im, s5_d,
                conv_w, a_log, dt_bias, norm_w, fw, ops):
    G, P, gb = S5_GROUPS, S5_STATE, S5_GROUPS // S5_BLOCKS
    h, rstd = _ln_fwd(x, ln_w)
    proj, (w_glu, w_su, w_du, w_out) = ops.in_proj(h, w_perm)

    b_re2, b_im2 = b_re.reshape(G, P * S5_GROUP), b_im.reshape(G, P * S5_GROUP)
    ls2 = log_step.reshape(G, 1)
    abar_re, abar_im, bb_re, bb_im = _s5_param_fwd(lam_re, lam_im, ls2, b_re2, b_im2)

    def to_wb(bb):
        return _block_diag(bb.reshape(S5_BLOCKS, gb, P, S5_GROUP).transpose(0, 1, 3, 2)).astype(BF16)

    def to_cb(cc):
        return _block_diag(cc.reshape(S5_BLOCKS, gb, S5_GROUP, P).transpose(0, 1, 3, 2)).astype(BF16)

    wbr, wbi, cbr, cbi = to_wb(bb_re), to_wb(bb_im), to_cb(c_re), to_cb(c_im)
    a_re_row, a_im_row = abar_re.reshape(1, G * P), abar_im.reshape(1, G * P)
    yc = _s5_core_fwd(proj, wbr, wbi, a_re_row, a_im_row, cbr, cbi)
    s5o = _s5_post_fwd(yc, proj, s5_d, w_glu)

    pad = lambda v: jnp.pad(v, ((0, 0), (DN_HEADS, 128 - 2 * DN_HEADS)))
    alog_row, dtb_row = pad(a_log), pad(dt_bias)
    qkv = _dn_conv_fwd(proj, conv_w)
    gates = _dn_gates_fwd(proj, alog_row, dtb_row)
    prep = _dn_prep_fwd(qkv, gates)
    o_dn, states = _dn_scan_fwd(*prep)
    dno = _dn_post_fwd(o_dn, proj, norm_w)

    ys, yd, mixed = _mix_fwd(s5o, dno, w_su, w_du, proj)
    dx2, dx2b, loss_row, d_fw = _final(mixed, w_out, x, tgt, fw)
    d_w_out = _mm(mixed, dx2b, ta=True, name="d_w_out")
    dgs, dgd, dys, dyd = _mix_bwd(dx2b, w_out, proj, ys, yd)
    d_w_su = _mm(s5o, dys, ta=True, name="d_w_su", shard_out=True)
    d_w_du = _mm(dno, dyd, ta=True, name="d_w_du", shard_out=True)
    ds5o = _mm(dys, w_su, tb=True, name="d_s5o")
    ddno = _mm(dyd, w_du, tb=True, name="d_dno")

    dyc, du1, dz_s, d_s5d, d_w_glu = _s5_post_bwd(yc, proj, s5_d, w_glu, ds5o)
    ops.rest_grads(d_w_glu, d_w_su, d_w_du, d_w_out)
    du, dwbr, dwbi, dcbr, dcbi, dar, dai = _s5_core_bwd(proj, wbr, wbi, a_re_row, a_im_row, cbr, cbi, dyc, du1)

    def from_wb(dwb):
        return _block_diag_take(dwb, gb).transpose(0, 1, 3, 2).reshape(G, P * S5_GROUP)

    def from_cb(dcb):
        return _block_diag_take(dcb, gb).transpose(0, 1, 3, 2).reshape(G, S5_GROUP, P)

    d_lam_re, d_lam_im, d_ls, d_b_re, d_b_im = _s5_param_bwd(
        lam_re, lam_im, ls2, b_re2, b_im2, dar.reshape(G, P), dai.reshape(G, P), from_wb(dwbr), from_wb(dwbi))

    do_dn, dz_d, d_norm_w = _dn_post_bwd(o_dn, proj, norm_w, ddno)
    dq, dk, dv, dgates = _dn_prep_bwd(qkv, gates, *_dn_scan_bwd(*prep, states, do_dn))
    dqkv, d_conv = _dn_conv_bwd(proj, conv_w, jnp.concatenate([dq, dk, dv], axis=1))
    dpb, d_alog_row, d_dtb_row = _dn_gates_bwd(proj, alog_row, dtb_row, dgates)

    dproj = jnp.concatenate([du, dz_s, dqkv, dz_d, dgs, dgd, dpb], axis=1)
    d_w_perm = ops.d_w_in(h, dproj)
    dh = ops.d_h(dproj, w_perm, d_w_perm)
    grad_x, d_ln_w = _ln_bwd(x, rstd, ln_w, dh, dx2)

    grads = dict(
        ln_w=d_ln_w, w_perm=d_w_perm, s5_lam_re=d_lam_re, s5_lam_im=d_lam_im, s5_log_step=d_ls.reshape(1, G),
        s5_b_re=d_b_re.reshape(G, P, S5_GROUP), s5_b_im=d_b_im.reshape(G, P, S5_GROUP),
        s5_c_re=from_cb(dcbr), s5_c_im=from_cb(dcbi), s5_d=d_s5d, s5_w_glu=d_w_glu, s5_w_up=d_w_su,
        dn_conv_w=d_conv, dn_a_log=d_alog_row[:, DN_HEADS:2 * DN_HEADS], dn_dt_bias=d_dtb_row[:, DN_HEADS:2 * DN_HEADS],
        dn_norm_w=d_norm_w, dn_w_up=d_w_du, w_out=d_w_out, final_norm_w=d_fw)
    return loss_row, grad_x, grads


def _place():
    x, y, c = lax.axis_index("x"), lax.axis_index("y"), lax.axis_index("c")
    return x, y, c


def _remote(src, dst, send_sem, recv_sem, to):
    return pltpu.make_async_remote_copy(src_ref=src, dst_ref=dst, send_sem=send_sem, recv_sem=recv_sem,
                                        device_id=to, device_id_type=MESH)


def _gather_exchange(shards, whole=()):
    na, nw = len(shards), len(whole)

    def half_of(ref, a, half):
        rows = shards[a].shape[0]
        return ref.at[pl.ds(half * (rows // 2), rows // 2)]

    def plan(ins, outs, send_sems, recv_sems, local_sems, receiving):
        x, y, c = _place()
        me = 2 * x + y
        sibling = (x, y, 1 - c)
        chips = [(1 - x, y), (x, 1 - y), (1 - x, 1 - y)]

        def part(a, chip, half):
            return half_of(outs[a].at[chip], a, half)

        own = [pltpu.make_async_copy(ins[a], outs[a].at[me], local_sems.at[a]) for a in range(na + nw)]
        sends, landed, passed, arrivals = [], [], [], []
        for a in range(na):
            for j, (px, py) in enumerate(chips):
                k = 6 * a + j
                sends.append(_remote(half_of(ins[a], a, c), part(a, me, c), send_sems.at[k], recv_sems.at[k], (px, py, c)))
                if receiving:
                    got, other = part(a, 2 * px + py, c), part(a, 2 * px + py, 1 - c)
                    landed.append(_remote(got, got, send_sems.at[k], recv_sems.at[k], (px, py, c)))
                    passed.append(_remote(got, got, send_sems.at[k + 3], recv_sems.at[k + 3], sibling))
                    arrivals.append(_remote(other, other, send_sems.at[k + 3], recv_sems.at[k + 3], sibling))
        for a in range(na, na + nw):
            for j, (px, py) in enumerate(chips):
                k = 6 * na + 3 * (a - na) + j
                sends.append(_remote(ins[a], outs[a].at[me], send_sems.at[k], recv_sems.at[k], (px, py, c)))
                if receiving:
                    arrivals.append(_remote(ins[a], outs[a].at[2 * px + py], send_sems.at[k], recv_sems.at[k], (px, py, c)))
        return own, sends, landed, passed, arrivals

    def start(ins, outs, *sems):
        own, sends, _, _, _ = plan(ins, outs, *sems, False)
        for cp in own + sends:
            cp.start()

    def finish(ins, outs, *sems):
        own, sends, landed, passed, arrivals = plan(ins, outs, *sems, True)
        for got, fwd in zip(landed, passed):
            got.wait_recv()
            fwd.start()
        for cp in arrivals:
            cp.wait_recv()
        for cp in sends + passed:
            cp.wait_send()
        for cp in own:
            cp.wait()

    arrays = list(shards) + list(whole)
    return _Exchange(arrays, [_sds((N_CHIPS,) + s.shape, s.dtype) for s in arrays], 6 * na + 3 * nw, start, finish)


def _gather_relayed(shard, whole, name):
    rows, cols = shard.shape
    nw = len(whole)

    def body(*refs):
        in_ref, w_in = refs[0], refs[1:1 + nw]
        out_ref, w_out = refs[1 + nw], refs[2 + nw:2 + 2 * nw]
        send_sems, recv_sems, local_sems = refs[2 + 2 * nw:]
        x, y, c = _place()
        me = 2 * x + y
        near = (jnp.where(c == 1, 1 - x, x), jnp.where(c == 1, y, 1 - y))
        far = (jnp.where(c == 1, x, 1 - x), jnp.where(c == 1, 1 - y, y))
        diag = (1 - x, 1 - y)
        sibling = (x, y, 1 - c)
        chip_of = lambda p: 2 * p[0] + p[1]

        def half(ref, h):
            return ref.at[pl.ds(0, rows), pl.ds(h * (cols // 2), cols // 2)]

        own = [pltpu.make_async_copy(in_ref, out_ref.at[me], local_sems.at[0])]
        own += [pltpu.make_async_copy(w_in[a], w_out[a].at[me], local_sems.at[1 + a]) for a in range(nw)]
        others = [(1 - x, y), (x, 1 - y), (1 - x, 1 - y)]
        small = [_remote(w_in[a], w_out[a].at[me], send_sems.at[4 + 3 * a + j], recv_sems.at[4 + 3 * a + j], (*p, c))
                 for a in range(nw) for j, p in enumerate(others)]
        sends = [_remote(in_ref, out_ref.at[me], send_sems.at[0], recv_sems.at[0], (*near, c))]
        for cp in own + small + sends:
            cp.start()
        from_near = out_ref.at[chip_of(near)]
        _remote(from_near, from_near, send_sems.at[0], recv_sems.at[0], (*near, c)).wait_recv()
        sends.append(_remote(from_near, from_near, send_sems.at[1], recv_sems.at[1], sibling))
        sends[-1].start()
        from_far = out_ref.at[chip_of(far)]
        _remote(from_far, from_far, send_sems.at[1], recv_sems.at[1], sibling).wait_recv()
        sends.append(_remote(half(from_far, c), half(from_far, c), send_sems.at[2], recv_sems.at[2], (*near, c)))
        sends[-1].start()
        of_diag = out_ref.at[chip_of(diag)]
        _remote(half(of_diag, c), half(of_diag, c), send_sems.at[2], recv_sems.at[2], (*near, c)).wait_recv()
        sends.append(_remote(half(of_diag, c), half(of_diag, c), send_sems.at[3], recv_sems.at[3], sibling))
        sends[-1].start()
        _remote(half(of_diag, 1 - c), half(of_diag, 1 - c), send_sems.at[3], recv_sems.at[3], sibling).wait_recv()
        for a in range(nw):
            for j, p in enumerate(others):
                _remote(w_in[a], w_out[a].at[chip_of(p)], send_sems.at[4 + 3 * a + j], recv_sems.at[4 + 3 * a + j], (*p, c)).wait_recv()
        for cp in sends + small:
            cp.wait_send()
        for cp in own:
            cp.wait()

    arrays = [shard] + list(whole)
    n_sems = 4 + 3 * nw
    return pl.pallas_call(
        body, name=name, in_specs=[ANY] * (1 + nw), out_specs=[ANY] * (1 + nw),
        out_shape=[_sds((N_CHIPS,) + a.shape, a.dtype) for a in arrays],
        scratch_shapes=[pltpu.SemaphoreType.DMA((n_sems,)) for _ in range(3)],
    )(*arrays)


def _owners_exchange(csbs):
    na = len(csbs)

    def plan(ins, outs, send_sems, recv_sems, local_sems, receiving):
        x, y, c = _place()
        me = 2 * x + y
        sends, arrivals = [], []
        for a in range(na):
            for k in range(N_CHIPS - 1):
                j = (me + 1 + k) % N_CHIPS
                sends.append(_remote(ins[a].at[k], outs[a].at[2 - k], send_sems.at[3 * a + k], recv_sems.at[3 * a + 2 - k],
                                     (j // 2, j % 2, c)))
                if receiving:
                    arrivals.append(_remote(ins[a].at[k], outs[a].at[k], send_sems.at[3 * a + k], recv_sems.at[3 * a + k], (x, y, c)))
        return sends, arrivals

    def start(ins, outs, *sems):
        for cp in plan(ins, outs, *sems, False)[0]:
            cp.start()

    def finish(ins, outs, *sems):
        sends, arrivals = plan(ins, outs, *sems, True)
        for cp in arrivals:
            cp.wait_recv()
        for cp in sends:
            cp.wait_send()

    return _Exchange(csbs, [_sds(g.shape, g.dtype) for g in csbs], 3 * na, start, finish)


def _swap_halves(gxs, name):
    na = len(gxs)

    def body(*refs):
        ins, outs = refs[:na], refs[na:2 * na]
        send_sems, recv_sems = refs[2 * na:]
        x, y, c = _place()
        cps = [_remote(ins[a].at[pl.ds(0, N_CHIPS), pl.ds(1 - c, 1)], outs[a], send_sems.at[a], recv_sems.at[a], (x, y, 1 - c))
               for a in range(na)]
        for cp in cps:
            cp.start()
        for cp in cps:
            cp.wait()

    return pl.pallas_call(
        body, name=name, in_specs=[ANY] * na, out_specs=[ANY] * na,
        out_shape=[_sds((N_CHIPS, 1) + g.shape[2:], g.dtype) for g in gxs],
        scratch_shapes=[pltpu.SemaphoreType.DMA((na,)), pltpu.SemaphoreType.DMA((na,))],
    )(*gxs)


def _share_halves(gfs):
    na = len(gfs)

    def body(*refs):
        ins, outs = refs[:na], refs[na:2 * na]
        send_sems, recv_sems = refs[2 * na:]
        x, y, c = _place()
        cps = [_remote(ins[a].at[pl.ds(c, 1)], outs[a].at[pl.ds(c, 1)], send_sems.at[a], recv_sems.at[a], (x, y, 1 - c))
               for a in range(na)]
        for cp in cps:
            cp.start()
        for a in range(na):
            cps[a].wait_send()
            _remote(ins[a].at[pl.ds(1 - c, 1)], outs[a].at[pl.ds(1 - c, 1)], send_sems.at[a], recv_sems.at[a], (x, y, 1 - c)).wait_recv()

    return pl.pallas_call(
        body, name="rs_share_halves", in_specs=[ANY] * na, out_specs=[ANY] * na,
        out_shape=[_sds(g.shape, g.dtype) for g in gfs], input_output_aliases={a: a for a in range(na)},
        scratch_shapes=[pltpu.SemaphoreType.DMA((na,)), pltpu.SemaphoreType.DMA((na,))],
    )(*gfs)


def _row_tile(rows, cols, budget=5 << 18):
    fits = [t for t in range(16, rows + 1, 16) if rows % t == 0 and t * cols * 4 <= budget]
    return max(fits) if fits else rows


def _chip_sums(gx, r1, where):
    _, _, r2, cd = gx.shape
    tr = _row_tile(r2, cd)

    def body(w_ref, a_ref, b_ref, o_ref):
        o_ref[...] = (a_ref[0] + b_ref[0]).astype(BF16)

    other = lambda k, i, w: ((w[1] + 1 + k) % N_CHIPS, w[0], i, 0)
    other0 = lambda k, i, w: ((w[1] + 1 + k) % N_CHIPS, 0, i, 0)
    return pl.pallas_call(
        body, name="rs_chip_sums",
        grid_spec=pltpu.PrefetchScalarGridSpec(
            num_scalar_prefetch=1, grid=(N_CHIPS - 1, r2 // tr),
            in_specs=[pl.BlockSpec((1, 1, tr, cd), other), pl.BlockSpec((1, 1, tr, cd), other0)],
            out_specs=pl.BlockSpec((1, tr, cd), lambda k, i, w: (k, i, 0))),
        out_shape=_sds((N_CHIPS - 1, r2, cd), BF16), compiler_params=_cp(("parallel", "parallel")),
    )(where, gx, r1)


def _owner_sum(gx, r1, r2x, where):
    _, _, r2, cd = gx.shape
    tr = _row_tile(r2, cd)

    def body(w_ref, a_ref, b_ref, r_ref, o_ref):
        acc = a_ref[0, 0] + b_ref[0, 0]
        for k in range(N_CHIPS - 1):
            acc = acc + r_ref[k].astype(F32)
        o_ref[0] = acc

    return pl.pallas_call(
        body, name="rs_owner_sum",
        grid_spec=pltpu.PrefetchScalarGridSpec(
            num_scalar_prefetch=1, grid=(r2 // tr,),
            in_specs=[pl.BlockSpec((1, 1, tr, cd), lambda i, w: (w[1], w[0], i, 0)),
                      pl.BlockSpec((1, 1, tr, cd), lambda i, w: (w[1], 0, i, 0)),
                      pl.BlockSpec((N_CHIPS - 1, tr, cd), lambda i, w: (0, i, 0))],
            out_specs=pl.BlockSpec((1, tr, cd), lambda i, w: (w[0], i, 0))),
        out_shape=_sds((2, r2, cd)), compiler_params=_cp(("parallel",)),
    )(where, gx, r1, r2x)


def _adamw_math(w, g, m, v):
    m = ADAM_B1 * m + (1.0 - ADAM_B1) * g
    v = ADAM_B2 * v + (1.0 - ADAM_B2) * (g * g)
    m_hat = m / (1.0 - ADAM_B1 ** ADAM_STEP)
    v_hat = v / (1.0 - ADAM_B2 ** ADAM_STEP)
    delta = -ADAM_LR * (m_hat / (jnp.sqrt(v_hat) + ADAM_EPS) + ADAM_WD * w)
    return delta, m, v


def _adamw(w, g, m, v, name):
    rows, cd = w.shape
    if rows % 16 == 0:
        tr, tc = _row_tile(rows, cd, budget=3 << 19), cd
    else:
        tr, tc = rows, (128 if rows * cd * 4 > (3 << 19) else cd)
    assert rows % tr == 0 and cd % tc == 0

    def body(w_ref, g_ref, m_ref, v_ref, d_ref, mo_ref, vo_ref):
        d, mm, vv = _adamw_math(w_ref[...], g_ref[...], m_ref[...], v_ref[...])
        d_ref[...] = d
        mo_ref[...] = mm
        vo_ref[...] = vv

    blk = pl.BlockSpec((tr, tc), lambda i, j: (i, j))
    return pl.pallas_call(
        body, name=name, grid=(rows // tr, cd // tc), in_specs=[blk] * 4, out_specs=[blk] * 3, out_shape=[_sds(w.shape)] * 3,
        compiler_params=_cp(("parallel", "parallel")),
    )(w, g, m, v)


def _small_allreduce(gp):
    R = gp.shape[0]
    R2 = R // 2
    assert R2 % 8 == 0

    def body(g_ref, go_ref, sib, csum, land, send_sems, recv_sems):
        x, y, c = _place()
        me = 2 * x + y
        sibling = (x, y, 1 - c)
        chips = [(1 - x, y), (x, 1 - y), (1 - x, 1 - y)]
        swap = _remote(g_ref, sib, send_sems.at[0], recv_sems.at[0], sibling)
        swap.start()
        swap.wait()
        csum[...] = g_ref[...] + sib[...]
        half = csum.at[pl.ds(c * R2, R2)]
        land[me] = csum[pl.ds(c * R2, R2), :]
        cps = [_remote(half, land.at[me], send_sems.at[1 + j], recv_sems.at[1 + j], (px, py, c))
               for j, (px, py) in enumerate(chips)]
        for cp in cps:
            cp.start()
        for j, (px, py) in enumerate(chips):
            _remote(half, land.at[2 * px + py], send_sems.at[1 + j], recv_sems.at[1 + j], (px, py, c)).wait_recv()
        for cp in cps:
            cp.wait_send()
        mine = go_ref.at[pl.ds(c * R2, R2)]
        go_ref[pl.ds(c * R2, R2), :] = (land[0] + land[1]) + (land[2] + land[3])
        share = _remote(mine, mine, send_sems.at[4], recv_sems.at[4], sibling)
        share.start()
        share.wait_send()
        other = go_ref.at[pl.ds((1 - c) * R2, R2)]
        _remote(other, other, send_sems.at[4], recv_sems.at[4], sibling).wait_recv()

    vm = pl.BlockSpec(memory_space=pltpu.VMEM)
    return pl.pallas_call(
        body, name="small_allreduce", in_specs=[vm], out_specs=vm, out_shape=_sds((R, 128)),
        scratch_shapes=[pltpu.VMEM((R, 128), F32), pltpu.VMEM((R, 128), F32), pltpu.VMEM((N_CHIPS, R2, 128), F32),
                        pltpu.SemaphoreType.DMA((5,)), pltpu.SemaphoreType.DMA((5,))],
        compiler_params=_cp(),
    )(gp)


def _adamw_many(ws, gs, ms, vs):
    n = len(ws)

    def body(*refs):
        w_r, g_r, m_r, v_r = refs[:n], refs[n:2 * n], refs[2 * n:3 * n], refs[3 * n:4 * n]
        d_r, mo_r, vo_r = refs[4 * n:5 * n], refs[5 * n:6 * n], refs[6 * n:]
        for i in range(n):
            d_r[i][...], mo_r[i][...], vo_r[i][...] = _adamw_math(w_r[i][...], g_r[i][...], m_r[i][...], v_r[i][...])

    vm = pl.BlockSpec(memory_space=pltpu.VMEM)
    shapes = [_sds(a.shape) for a in ws]
    outs = pl.pallas_call(
        body, name="adamw_small", in_specs=[vm] * (4 * n), out_specs=[vm] * (3 * n), out_shape=shapes * 3, compiler_params=_cp(),
    )(*ws, *gs, *ms, *vs)
    return outs[:n], outs[n:2 * n], outs[2 * n:]


def _pack(arrs):
    rows = []
    for a in arrs:
        f = a.reshape(-1)
        f = jnp.pad(f, (0, (-f.shape[0]) % 128))
        rows.append(f.reshape(-1, 128))
    p = jnp.concatenate(rows, axis=0)
    return jnp.pad(p, ((0, (-p.shape[0]) % 8), (0, 0)))


def _unpack(p, shapes):
    out, r = [], 0
    for s in shapes:
        n = math.prod(s)
        nr = -(-n // 128)
        out.append(p[r:r + nr].reshape(-1)[:n].reshape(s))
        r += nr
    return out


class _ExchangeOps(_PlainOps):
    def __init__(self, rest_shards, where):
        self.rest_shards, self.where = rest_shards, where
        self.reduced = []

    def in_proj(self, h, wt_perm):
        proj, g_glu, g_su, g_du, g_out = _mm(h, wt_perm, tb=True, name="in_proj", tm=1024, tn=1152,
                                             exchange=_gather_exchange(self.rest_shards))
        cat = lambda g: jnp.concatenate([g[j] for j in range(N_CHIPS)], axis=1)
        return proj, (g_glu.reshape(D_S5, D_S5), cat(g_su), cat(g_du), g_out.reshape(D_MODEL, D_MODEL))

    def _chip_sums(self, gxs, name):
        r1s = _swap_halves(gxs, name)
        return r1s, [_chip_sums(gx, r1, self.where) for gx, r1 in zip(gxs, r1s)]

    def rest_grads(self, d_w_glu, d_w_su, d_w_du, d_w_out):
        gxs = [d_w_glu.reshape(N_CHIPS, 2, D_S5 // 8, D_S5), d_w_su.reshape(N_CHIPS, 2, D_S5 // 2, D_MODEL // N_CHIPS),
               d_w_du.reshape(N_CHIPS, 2, D_DN // 2, D_MODEL // N_CHIPS), d_w_out.reshape(N_CHIPS, 2, D_MODEL // 8, D_MODEL)]
        r1s, csbs = self._chip_sums(gxs, "rs_swap_rest")
        self.rest = (gxs, r1s, csbs)

    def d_w_in(self, h, dproj):
        gxs, r1s, csbs = self.rest
        d_wt_perm, *r2s = _mm(dproj, h, ta=True, name="d_w_in", tm=1152, tn=1024, exchange=_owners_exchange(csbs))
        self.reduced = list(zip(gxs, r1s, r2s))
        return d_wt_perm

    def d_h(self, dproj, wt_perm, d_wt_perm):
        gx = _wt_windows(d_wt_perm).reshape(N_CHIPS, 2, WT_ROWS // 2, D_MODEL)
        self.beta_a = d_wt_perm[OFF_B:OFF_B + WT_NB]
        (r1,), (csb,) = self._chip_sums([gx], "rs_swap_w_in")
        dh, r2 = _mm(dproj, wt_perm, name="d_h", tm=2048, tn=1024, tk=1152, exchange=_owners_exchange([csb]))
        self.reduced = [(gx, r1, r2)] + self.reduced
        return dh


WT_SHARD = D_IN // N_CHIPS
WT_NB = 2 * DN_HEADS
WT_B, WT_LO = divmod(OFF_GS, WT_SHARD)
WT_FIRST = [i * WT_SHARD - (WT_NB if i > WT_B else 0) for i in range(N_CHIPS)]
WT_WIN = [16 * (r // 16) for r in WT_FIRST]
WT_SHIFT = [r - s for r, s in zip(WT_FIRST, WT_WIN)]
WT_ROWS = 2592
assert (WT_LO + WT_SHIFT[WT_B]) % 16 == 0 and max(WT_SHIFT) + WT_SHARD <= WT_ROWS and WT_WIN[-1] + WT_ROWS <= D_IN_PAD


def _wt_to_window(shard, chip):
    d = jnp.asarray(WT_SHIFT, jnp.int32)[chip]
    gap = jnp.where(chip == WT_B, 0, WT_NB)
    win = jnp.zeros((WT_ROWS, shard.shape[1]), shard.dtype)
    win = lax.dynamic_update_slice(win, shard[:WT_LO], (d, 0))
    win = lax.dynamic_update_slice(win, shard[WT_LO:WT_LO + WT_NB], (d + WT_LO, 0))
    win = lax.dynamic_update_slice(win, shard[WT_LO + WT_NB:], (d + WT_LO + gap, 0))
    return win, shard[WT_LO:WT_LO + WT_NB]


def _wt_from_window(win, beta_a, chip):
    d = jnp.asarray(WT_SHIFT, jnp.int32)[chip]
    gap = jnp.where(chip == WT_B, 0, WT_NB)
    cols = win.shape[1]
    head = lax.dynamic_slice(win, (d, 0), (WT_LO, cols))
    mid = jnp.where(chip == WT_B, beta_a, lax.dynamic_slice(win, (d + WT_LO, 0), (WT_NB, cols)))
    tail = lax.dynamic_slice(win, (d + WT_LO + gap, 0), (WT_SHARD - WT_LO - WT_NB, cols))
    return jnp.concatenate([head, mid, tail], axis=0)


def _wt_regroup(wins, beta_a):
    parts, at = [], 0
    for i in range(N_CHIPS):
        end = WT_WIN[i + 1] if i + 1 < N_CHIPS else OFF_B
        lo = at - WT_WIN[i]
        over = WT_WIN[i] + WT_ROWS - end if i + 1 < N_CHIPS else 0
        parts.append(wins[i, lo:end - WT_WIN[i]])
        if over:
            parts.append(wins[i, end - WT_WIN[i]:] + wins[i + 1, :over])
        at = end + over
    pad = jnp.zeros((D_IN_PAD - OFF_B - WT_NB, wins.shape[2]), wins.dtype)
    return jnp.concatenate(parts + [beta_a, pad], axis=0)


def _wt_windows(regrouped):
    return jnp.stack([regrouped[s:s + WT_ROWS] for s in WT_WIN])


_SMALL = ("ln_w", "s5_lam_re", "s5_lam_im", "s5_log_step", "s5_b_re", "s5_b_im", "s5_c_re", "s5_c_im", "s5_d",
          "dn_a_log", "dn_dt_bias", "dn_norm_w", "final_norm_w")
_BIG = ("w_in", "s5_w_glu", "s5_w_up", "dn_w_up", "w_out")
_ORDER = ("ln_w", "w_in", "s5_lam_re", "s5_lam_im", "s5_log_step", "s5_b_re", "s5_b_im", "s5_c_re", "s5_c_im", "s5_d",
          "s5_w_glu", "s5_w_up", "dn_conv_w", "dn_a_log", "dn_dt_bias", "dn_norm_w", "dn_w_up", "w_out", "final_norm_w")


def kernel(x, ln_w, w_in, s5_lam_re, s5_lam_im, s5_log_step, s5_b_re, s5_b_im, s5_c_re, s5_c_im, s5_d, s5_w_glu, s5_w_up, dn_conv_w, dn_a_log, dn_dt_bias, dn_norm_w, dn_w_up, w_out, final_norm_w, loss_target, m_ln_w, m_w_in, m_s5_lam_re, m_s5_lam_im, m_s5_log_step, m_s5_b_re, m_s5_b_im, m_s5_c_re, m_s5_c_im, m_s5_d, m_s5_w_glu, m_s5_w_up, m_dn_conv_w, m_dn_a_log, m_dn_dt_bias, m_dn_norm_w, m_dn_w_up, m_w_out, m_final_norm_w, v_ln_w, v_w_in, v_s5_lam_re, v_s5_lam_im, v_s5_log_step, v_s5_b_re, v_s5_b_im, v_s5_c_re, v_s5_c_im, v_s5_d, v_s5_w_glu, v_s5_w_up, v_dn_conv_w, v_dn_a_log, v_dn_dt_bias, v_dn_norm_w, v_dn_w_up, v_w_out, v_final_norm_w):
    w = dict(ln_w=ln_w, w_in=w_in, s5_lam_re=s5_lam_re, s5_lam_im=s5_lam_im, s5_log_step=s5_log_step, s5_b_re=s5_b_re,
             s5_b_im=s5_b_im, s5_c_re=s5_c_re, s5_c_im=s5_c_im, s5_d=s5_d, s5_w_glu=s5_w_glu, s5_w_up=s5_w_up,
             dn_conv_w=dn_conv_w, dn_a_log=dn_a_log, dn_dt_bias=dn_dt_bias, dn_norm_w=dn_norm_w, dn_w_up=dn_w_up, w_out=w_out,
             final_norm_w=final_norm_w)
    m = dict(ln_w=m_ln_w, w_in=m_w_in, s5_lam_re=m_s5_lam_re, s5_lam_im=m_s5_lam_im, s5_log_step=m_s5_log_step,
             s5_b_re=m_s5_b_re, s5_b_im=m_s5_b_im, s5_c_re=m_s5_c_re, s5_c_im=m_s5_c_im, s5_d=m_s5_d, s5_w_glu=m_s5_w_glu,
             s5_w_up=m_s5_w_up, dn_conv_w=m_dn_conv_w, dn_a_log=m_dn_a_log, dn_dt_bias=m_dn_dt_bias, dn_norm_w=m_dn_norm_w,
             dn_w_up=m_dn_w_up, w_out=m_w_out, final_norm_w=m_final_norm_w)
    v = dict(ln_w=v_ln_w, w_in=v_w_in, s5_lam_re=v_s5_lam_re, s5_lam_im=v_s5_lam_im, s5_log_step=v_s5_log_step,
             s5_b_re=v_s5_b_re, s5_b_im=v_s5_b_im, s5_c_re=v_s5_c_re, s5_c_im=v_s5_c_im, s5_d=v_s5_d, s5_w_glu=v_s5_w_glu,
             s5_w_up=v_s5_w_up, dn_conv_w=v_dn_conv_w, dn_a_log=v_dn_a_log, dn_dt_bias=v_dn_dt_bias, dn_norm_w=v_dn_norm_w,
             dn_w_up=v_dn_w_up, w_out=v_w_out, final_norm_w=v_final_norm_w)
    xi, yi, ci = _place()
    chip = 2 * xi + yi
    where = jnp.stack([ci, chip]).astype(jnp.int32)

    tr = lambda a: jnp.swapaxes(a[0], 0, 1)
    win, beta_a = _wt_to_window(tr(w_in).astype(BF16), chip)
    g_win, g_ba, g_conv = _gather_relayed(win, [beta_a, dn_conv_w[0]], "gather_w_in")
    cat = lambda g: jnp.concatenate([g[j] for j in range(N_CHIPS)], axis=1)
    w_perm = _wt_regroup(g_win, g_ba[WT_B])

    ops = _ExchangeOps([w[n][0].astype(BF16) for n in _BIG[1:]], where)
    loss_row, grad_x, g = _local_step(
        x[0], loss_target[0], ln_w, w_perm, s5_lam_re[0], s5_lam_im[0], s5_log_step, s5_b_re[0], s5_b_im[0], s5_c_re[0],
        s5_c_im[0], s5_d, cat(g_conv), dn_a_log, dn_dt_bias, dn_norm_w, final_norm_w[None], ops)
    loss = lax.psum(loss_row[0, 0], ("x", "y", "c"))

    gfs = [_owner_sum(gx, r1, r2x, where) for gx, r1, r2x in ops.reduced]
    gfs = _share_halves(gfs)
    grads, deltas, new_m, new_v = {}, {}, {}, {}
    for n, gf in zip(_BIG[1:], gfs[1:]):
        shp = w[n].shape
        g2 = gf.reshape(shp[1:])
        d_, m_, v_ = _adamw(w[n][0], g2, m[n][0], v[n][0], "adamw_" + n)
        grads[n], deltas[n], new_m[n], new_v[n] = g2.reshape(shp), d_.reshape(shp), m_.reshape(shp), v_.reshape(shp)

    go = _small_allreduce(_pack([g[n] for n in _SMALL] + [g["dn_conv_w"], ops.beta_a]))
    lanes = {"s5_b_re": (S5_GROUPS, S5_STATE * S5_GROUP), "s5_b_im": (S5_GROUPS, S5_STATE * S5_GROUP)}
    flat = [lanes.get(n, (math.prod(w[n].shape[:-1]), w[n].shape[-1])) for n in _SMALL]
    *gs, g_conv, g_beta_a = _unpack(go, flat + [(CONV_K, 3 * D_DN), (WT_NB, D_MODEL)])
    gt = _wt_from_window(gfs[0].reshape(WT_ROWS, D_MODEL), g_beta_a, chip)
    d_, m_, v_ = _adamw(tr(w_in), gt, tr(m_w_in), tr(v_w_in), "adamw_w_in")
    grads["w_in"], deltas["w_in"], new_m["w_in"], new_v["w_in"] = (jnp.swapaxes(a, 0, 1)[None] for a in (gt, d_, m_, v_))
    as2d = lambda t: [t[n].reshape(s) for n, s in zip(_SMALL, flat)]
    for dst, src in zip((grads, deltas, new_m, new_v), (gs, *_adamw_many(as2d(w), gs, as2d(m), as2d(v)))):
        for n, a in zip(_SMALL, src):
            dst[n] = a.reshape(w[n].shape)
    cc = 3 * D_DN // N_CHIPS
    g_conv_mine = lax.dynamic_slice(g_conv, (0, chip * cc), (CONV_K, cc))
    d_, m_, v_ = _adamw(dn_conv_w[0], g_conv_mine, m_dn_conv_w[0], v_dn_conv_w[0], "adamw_dn_conv_w")
    grads["dn_conv_w"], deltas["dn_conv_w"], new_m["dn_conv_w"], new_v["dn_conv_w"] = (
        g_conv_mine[None], d_[None], m_[None], v_[None])

    return (loss, grad_x[None], *[grads[n] for n in _ORDER], *[deltas[n] for n in _ORDER], *[new_m[n] for n in _ORDER],
            *[new_v[n] for n in _ORDER])
```

```python
import functools
import math

import jax
import jax.numpy as jnp
from jax import lax
from jax.experimental import pallas as pl
from jax.experimental.pallas import tpu as pltpu

F32 = jnp.float32
BF16 = jnp.bfloat16
HI = lax.Precision.HIGHEST
MESH = pl.DeviceIdType.MESH
ANY = pl.BlockSpec(memory_space=pl.ANY)

EPS = 1e-6
D_MODEL = 2048
D_S5 = 1024
S5_GROUP = 16
S5_GROUPS = 64
S5_STATE = 64
S5_BLOCKS = 8
S5_SEG = 8
DN_HEADS = 8
DN_HEAD_DIM = 128
D_DN = 1024
CONV_K = 4
CHUNK = 64
D_IN = 10256
D_IN_PAD = 10368
OFF_US, OFF_ZS, OFF_Q, OFF_K, OFF_V, OFF_ZD, OFF_GS, OFF_GD, OFF_B = 0, 1024, 2048, 3072, 4096, 5120, 6144, 8192, 10240
N_CHIPS = 4
N_DEV = 8
VMEM_LIMIT = 56 * 1024 * 1024

ADAM_LR = 0.001
ADAM_B1 = 0.9
ADAM_B2 = 0.999
ADAM_EPS = 1e-08
ADAM_WD = 0.01
ADAM_STEP = 10


def _cp(sem=None):
    return pltpu.CompilerParams(dimension_semantics=sem, vmem_limit_bytes=VMEM_LIMIT)


def _sds(shape, dtype=F32):
    return jax.ShapeDtypeStruct(tuple(shape), dtype)


def _sigmoid(x):
    return 1.0 / (1.0 + jnp.exp(-x))


def _silu(x):
    return x * _sigmoid(x)


def _dsilu(x):
    s = _sigmoid(x)
    return s * (1.0 + x * (1.0 - s))


class _Exchange:
    def __init__(self, ins, out_shapes, n_sems, start, finish):
        self.ins, self.out_shapes, self.n_sems, self.start, self.finish = list(ins), list(out_shapes), n_sems, start, finish


def _mm(a, b, *, name, ta=False, tb=False, out_dtype=F32, tm=512, tn=512, tk=2048, shard_out=False, exchange=None):
    if ta:
        K, M = a.shape
    else:
        M, K = a.shape
    if tb:
        N, K2 = b.shape
    else:
        K2, N = b.shape
    assert K == K2, (a.shape, b.shape)
    tm, tn, tk = min(tm, M), min(tn, N), min(tk, K)
    assert M % tm == 0 and N % tn == 0 and K % tk == 0, (M, N, K, tm, tn, tk)
    nk = K // tk
    dims = (((0 if ta else 1,), (1 if tb else 0,)), ((), ()))

    gm, gn = M // tm, N // tn
    n_in = len(exchange.ins) if exchange else 0
    n_out = len(exchange.out_shapes) if exchange else 0

    def body(*refs):
        a_ref, b_ref, xin, o_ref = refs[0], refs[1], refs[2:2 + n_in], refs[2 + n_in]
        xout, rest = refs[3 + n_in:3 + n_in + n_out], refs[3 + n_in + n_out:]
        i, j, k = pl.program_id(0), pl.program_id(1), pl.program_id(2)
        if exchange:
            sems = rest[-3:]

            @pl.when(jnp.logical_and(jnp.logical_and(i == 0, j == 0), k == 0))
            def _():
                exchange.start(xin, xout, *sems)

        p = lax.dot_general(a_ref[...].astype(BF16), b_ref[...].astype(BF16), dims, preferred_element_type=F32)
        if nk == 1:
            o_ref[...] = p.astype(out_dtype).reshape(o_ref.shape)
        else:
            acc_ref = rest[0]

            @pl.when(k == 0)
            def _():
                acc_ref[...] = p

            @pl.when(k > 0)
            def _():
                acc_ref[...] += p

            @pl.when(k == nk - 1)
            def _():
                o_ref[...] = acc_ref[...].astype(out_dtype).reshape(o_ref.shape)

        if exchange:
            @pl.when(jnp.logical_and(jnp.logical_and(i == gm - 1, j == gn - 1), k == nk - 1))
            def _():
                exchange.finish(xin, xout, *sems)

    a_spec = pl.BlockSpec((tk, tm), lambda i, j, k: (k, i)) if ta else pl.BlockSpec((tm, tk), lambda i, j, k: (i, k))
    b_spec = pl.BlockSpec((tn, tk), lambda i, j, k: (j, k)) if tb else pl.BlockSpec((tk, tn), lambda i, j, k: (k, j))
    if shard_out:
        o_spec = pl.BlockSpec((1, tm, tn), lambda i, j, k: (j, i, 0))
        o_shape = _sds((N // tn, M, tn), out_dtype)
    else:
        o_spec = pl.BlockSpec((tm, tn), lambda i, j, k: (i, j))
        o_shape = _sds((M, N), out_dtype)
    scratch = [pltpu.VMEM((tm, tn), F32)] if nk > 1 else []
    if not exchange:
        return pl.pallas_call(
            body, name=name, grid=(gm, gn, nk), in_specs=[a_spec, b_spec], out_specs=o_spec, out_shape=o_shape,
            scratch_shapes=scratch, compiler_params=_cp(("parallel", "parallel", "arbitrary")),
        )(a, b)
    scratch += [pltpu.SemaphoreType.DMA((exchange.n_sems,)) for _ in range(3)]
    return pl.pallas_call(
        body, name=name, grid=(gm, gn, nk), in_specs=[a_spec, b_spec] + [ANY] * n_in, out_specs=[o_spec] + [ANY] * n_out,
        out_shape=[o_shape] + exchange.out_shapes, scratch_shapes=scratch,
        compiler_params=_cp(("arbitrary", "arbitrary", "arbitrary")),
    )(a, b, *exchange.ins)


def _ln_fwd(x, w):
    L, D = x.shape
    tm = min(256, L)

    def body(x_ref, w_ref, h_ref, r_ref):
        xv = x_ref[...]
        r = lax.rsqrt(jnp.mean(xv * xv, axis=-1, keepdims=True) + EPS)
        h_ref[...] = (xv * r * w_ref[...]).astype(BF16)
        r_ref[...] = r

    return pl.pallas_call(
        body, name="ln_fwd", grid=(L // tm,),
        in_specs=[pl.BlockSpec((tm, D), lambda i: (i, 0)), pl.BlockSpec((1, D), lambda i: (0, 0))],
        out_specs=[pl.BlockSpec((tm, D), lambda i: (i, 0)), pl.BlockSpec((tm, 1), lambda i: (i, 0))],
        out_shape=[_sds((L, D), BF16), _sds((L, 1))], compiler_params=_cp(("parallel",)),
    )(x, w)


def _ln_bwd(x, r, w, dh, dx2):
    L, D = x.shape
    tm = min(256, L)

    def body(x_ref, r_ref, w_ref, dh_ref, dx2_ref, dx_ref, dw_ref):
        i = pl.program_id(0)
        xv, rv, dhv = x_ref[...], r_ref[...], dh_ref[...]
        t = dhv * w_ref[...]
        m = jnp.mean(t * xv, axis=-1, keepdims=True)
        dx_ref[...] = dx2_ref[...] + rv * t - xv * (rv * rv * rv) * m
        part = jnp.sum(dhv * xv * rv, axis=0, keepdims=True)

        @pl.when(i == 0)
        def _():
            dw_ref[...] = part

        @pl.when(i > 0)
        def _():
            dw_ref[...] += part

    row = pl.BlockSpec((tm, D), lambda i: (i, 0))
    return pl.pallas_call(
        body, name="ln_bwd", grid=(L // tm,),
        in_specs=[row, pl.BlockSpec((tm, 1), lambda i: (i, 0)), pl.BlockSpec((1, D), lambda i: (0, 0)), row, row],
        out_specs=[row, pl.BlockSpec((1, D), lambda i: (0, 0))],
        out_shape=[_sds((L, D)), _sds((1, D))], compiler_params=_cp(("arbitrary",)),
    )(x, r, w, dh, dx2)


def _s5_param_math(lam_re, lam_im, log_step, b_re, b_im, expand):
    step = jnp.exp(log_step)
    mag = jnp.exp(lam_re * step)
    abar_re = mag * jnp.cos(lam_im * step)
    abar_im = mag * jnp.sin(lam_im * step)
    den = lam_re * lam_re + lam_im * lam_im
    xr = abar_re - 1.0
    f_re = (xr * lam_re + abar_im * lam_im) / den
    f_im = (abar_im * lam_re - xr * lam_im) / den
    fe_re = jnp.dot(f_re, expand, precision=HI, preferred_element_type=F32)
    fe_im = jnp.dot(f_im, expand, precision=HI, preferred_element_type=F32)
    bb_re = fe_re * b_re - fe_im * b_im
    bb_im = fe_re * b_im + fe_im * b_re
    return abar_re, abar_im, bb_re, bb_im


def _s5_expand():
    p = lax.broadcasted_iota(jnp.int32, (S5_STATE, S5_STATE * S5_GROUP), 0)
    q = lax.broadcasted_iota(jnp.int32, (S5_STATE, S5_STATE * S5_GROUP), 1)
    return (q // S5_GROUP == p).astype(F32)


def _s5_param_fwd(lam_re, lam_im, log_step, b_re, b_im):
    G, P = lam_re.shape

    def body(lr, li, ls, br, bi, ar_o, ai_o, bbr_o, bbi_o):
        outs = _s5_param_math(lr[...], li[...], ls[...], br[...], bi[...], _s5_expand())
        for o, v in zip((ar_o, ai_o, bbr_o, bbi_o), outs):
            o[...] = v

    return pl.pallas_call(
        body, name="s5_param_fwd",
        out_shape=[_sds((G, P)), _sds((G, P)), _sds(b_re.shape), _sds(b_re.shape)], compiler_params=_cp(),
    )(lam_re, lam_im, log_step, b_re, b_im)


def _s5_param_bwd(lam_re, lam_im, log_step, b_re, b_im, dar, dai, dbbr, dbbi):
    G, P = lam_re.shape

    def body(lr, li, ls, br, bi, g0, g1, g2, g3, dlr, dli, dls, dbr, dbi):
        ex = _s5_expand()
        _, f = jax.vjp(lambda a, b, c, d, e: _s5_param_math(a, b, c, d, e, ex), lr[...], li[...], ls[...], br[...], bi[...])
        grads = f((g0[...], g1[...], g2[...], g3[...]))
        for o, v in zip((dlr, dli, dls, dbr, dbi), grads):
            o[...] = v

    return pl.pallas_call(
        body, name="s5_param_bwd",
        out_shape=[_sds((G, P)), _sds((G, P)), _sds((G, 1)), _sds(b_re.shape), _sds(b_re.shape)], compiler_params=_cp(),
    )(lam_re, lam_im, log_step, b_re, b_im, dar, dai, dbbr, dbbi)


def _to_segs(src_ref, dst_ref, L):
    S = L // S5_SEG

    def body(j, carry):
        dst_ref[pl.ds(pl.multiple_of(S5_SEG * j, S5_SEG), S5_SEG), :] = src_ref[pl.ds(j, S5_SEG, stride=S), :]
        return carry

    lax.fori_loop(0, S, body, 0, unroll=8)


def _from_segs(src_ref, L, write):
    S = L // S5_SEG
    for seg in range(S5_SEG):
        def body(jb, carry, seg=seg):
            j0 = 16 * jb
            write(pl.multiple_of(seg * S + j0, 16), src_ref[pl.ds(S5_SEG * j0 + seg, 16, stride=S5_SEG), :])
            return carry

        lax.fori_loop(0, S // 16, body, 0, unroll=4)


def _scan_segs(ar, ai, re_ref, im_ref, end_r_ref, end_i_ref, c_r_ref, c_i_ref, L, tile0, reverse):
    S = L // S5_SEG
    NB, LN = re_ref.shape[0], 128
    assert S & (S - 1) == 0
    tile = lambda j: pl.ds(pl.multiple_of(S5_SEG * (tile0 + j), S5_SEG), S5_SEG)
    ar8 = [jnp.broadcast_to(ar[:, b * LN:(b + 1) * LN], (S5_SEG, LN)) for b in range(NB)]
    ai8 = [jnp.broadcast_to(ai[:, b * LN:(b + 1) * LN], (S5_SEG, LN)) for b in range(NB)]

    def step(idx, carry):
        rows = tile(S - 1 - idx if reverse else idx)
        out = []
        for b in range(NB):
            sr, si = carry[b]
            nr = ar8[b] * sr - ai8[b] * si + re_ref[b, rows, :]
            ni = ar8[b] * si + ai8[b] * sr + im_ref[b, rows, :]
            re_ref[b, rows, :] = nr
            im_ref[b, rows, :] = ni
            out.append((nr, ni))
        return tuple(out)

    z8 = jnp.zeros((S5_SEG, LN), F32)
    fin = lax.fori_loop(0, S, step, tuple((z8, z8) for _ in range(NB)), unroll=4)
    order = range(S5_SEG - 2, -1, -1) if reverse else range(1, S5_SEG)
    for b in range(NB):
        end_r_ref[b], end_i_ref[b] = fin[b]
        pr, pi = ar8[b][:1], ai8[b][:1]
        for _ in range(int(math.log2(S))):
            pr, pi = pr * pr - pi * pi, 2.0 * pr * pi
        first = S5_SEG - 1 if reverse else 0
        c_r_ref[b, pl.ds(first, 1), :] = jnp.zeros((1, LN), F32)
        c_i_ref[b, pl.ds(first, 1), :] = jnp.zeros((1, LN), F32)
        cr, ci = end_r_ref[b, pl.ds(first, 1), :], end_i_ref[b, pl.ds(first, 1), :]
        for i in order:
            c_r_ref[b, pl.ds(i, 1), :] = cr
            c_i_ref[b, pl.ds(i, 1), :] = ci
            er, ei = end_r_ref[b, pl.ds(i, 1), :], end_i_ref[b, pl.ds(i, 1), :]
            cr, ci = er + pr * cr - pi * ci, ei + pr * ci + pi * cr

    entering = [(c_r_ref[b], c_i_ref[b]) for b in range(NB)]

    def fix(idx, carry):
        rows = tile(S - 1 - idx if reverse else idx)
        out = []
        for b in range(NB):
            pr, pi = carry[b]
            cr, ci = entering[b]
            re_ref[b, rows, :] += pr * cr - pi * ci
            im_ref[b, rows, :] += pr * ci + pi * cr
            out.append((pr * ar8[b] - pi * ai8[b], pr * ai8[b] + pi * ar8[b]))
        return tuple(out)

    lax.fori_loop(0, S, fix, tuple((ar8[b], ai8[b]) for b in range(NB)), unroll=4)


def _s5_seg_scratch(L, cs, pad):
    NB = cs // 128
    small = [pltpu.VMEM((NB, S5_SEG, 128), F32) for _ in range(4)]
    return [pltpu.VMEM((NB, L + pad, 128), F32), pltpu.VMEM((NB, L + pad, 128), F32)] + small


def _s5_core_fwd(proj, wbr, wbi, a_re, a_im, cbr, cbi):
    L = proj.shape[0]
    nb, ci, cs = wbr.shape
    NB = cs // 128

    def body(u_ref, wbr_ref, wbi_ref, ar_ref, ai_ref, cbr_ref, cbi_ref, y_ref, sr, si, er, ei, cr, cim, up, yp):
        _to_segs(u_ref, up, L)
        u = up[...].astype(BF16)
        for b in range(NB):
            lanes = pl.ds(b * 128, 128)
            sr[b] = jnp.dot(u, wbr_ref[0, :, lanes], preferred_element_type=F32)
            si[b] = jnp.dot(u, wbi_ref[0, :, lanes], preferred_element_type=F32)
        _scan_segs(ar_ref[...], ai_ref[...], sr, si, er, ei, cr, cim, L, 0, False)
        y = jnp.zeros((L, ci), F32)
        for b in range(NB):
            lanes = pl.ds(b * 128, 128)
            y = y + (jnp.dot(sr[b].astype(BF16), cbr_ref[0, lanes, :], preferred_element_type=F32)
                     - jnp.dot(si[b].astype(BF16), cbi_ref[0, lanes, :], preferred_element_type=F32))
        yp[...] = y

        def write(row, val):
            y_ref[pl.ds(row, 16), :] = val

        _from_segs(yp, L, write)

    wspec = pl.BlockSpec((1, ci, cs), lambda j: (j, 0, 0))
    aspec = pl.BlockSpec((1, cs), lambda j: (0, j))
    cspec = pl.BlockSpec((1, cs, ci), lambda j: (j, 0, 0))
    return pl.pallas_call(
        body, name="s5_core_fwd", grid=(nb,),
        in_specs=[pl.BlockSpec((L, ci), lambda j: (0, OFF_US // ci + j)), wspec, wspec, aspec, aspec, cspec, cspec],
        out_specs=pl.BlockSpec((L, ci), lambda j: (0, j)), out_shape=_sds((L, nb * ci)),
        scratch_shapes=_s5_seg_scratch(L, cs, 0) + [pltpu.VMEM((L, ci), F32), pltpu.VMEM((L, ci), F32)],
        compiler_params=_cp(("arbitrary",)),
    )(proj, wbr, wbi, a_re, a_im, cbr, cbi)


def _s5_core_bwd(proj, wbr, wbi, a_re, a_im, cbr, cbi, dyc, du1):
    L = proj.shape[0]
    nb, ci, cs = wbr.shape
    NB = cs // 128
    S = L // S5_SEG
    PAD = S5_SEG

    def body(u_ref, wbr_ref, wbi_ref, ar_ref, ai_ref, cbr_ref, cbi_ref, dy_ref, du1_ref,
             du_ref, dwbr_ref, dwbi_ref, dcbr_ref, dcbi_ref, dar_ref, dai_ref,
             sr, si, er, ei, cr, cim, lr, li, up, dyp, dup):
        tn = (((0,), (0,)), ((), ()))
        nt = (((1,), (1,)), ((), ()))
        _to_segs(u_ref, up, L)
        _to_segs(dy_ref, dyp, L)
        _to_segs(du1_ref, dup, L)
        u = up[...].astype(BF16)
        dy = dyp[...].astype(BF16)
        ar, ai = ar_ref[...], ai_ref[...]
        for b in range(NB):
            lanes = pl.ds(b * 128, 128)
            sr[b, pl.ds(PAD, L), :] = jnp.dot(u, wbr_ref[0, :, lanes], preferred_element_type=F32)
            si[b, pl.ds(PAD, L), :] = jnp.dot(u, wbi_ref[0, :, lanes], preferred_element_type=F32)
        _scan_segs(ar, ai, sr, si, er, ei, cr, cim, L, 1, False)
        for b in range(NB):
            lanes = pl.ds(b * 128, 128)
            sr[b, pl.ds(0, PAD), :] = cr[b]
            si[b, pl.ds(0, PAD), :] = cim[b]
            lr[b] = lax.dot_general(dy, cbr_ref[0, lanes, :], nt, preferred_element_type=F32)
            li[b] = -lax.dot_general(dy, cbi_ref[0, lanes, :], nt, preferred_element_type=F32)
            dcbr_ref[0, lanes, :] = lax.dot_general(sr[b, pl.ds(PAD, L), :].astype(BF16), dy, tn, preferred_element_type=F32)
            dcbi_ref[0, lanes, :] = -lax.dot_general(si[b, pl.ds(PAD, L), :].astype(BF16), dy, tn, preferred_element_type=F32)
        _scan_segs(ar, -ai, lr, li, er, ei, cr, cim, L, 0, True)

        def da_step(j, carry):
            rows = pl.ds(pl.multiple_of(S5_SEG * j, S5_SEG), S5_SEG)
            out = []
            for b in range(NB):
                dar, dai = carry[b]
                pr_, pi_ = sr[b, rows, :], si[b, rows, :]
                gr, gi = lr[b, rows, :], li[b, rows, :]
                out.append((dar + (gr * pr_ + gi * pi_), dai + (gi * pr_ - gr * pi_)))
            return tuple(out)

        z8 = jnp.zeros((S5_SEG, 128), F32)
        acc = lax.fori_loop(0, S, da_step, tuple((z8, z8) for _ in range(NB)), unroll=4)
        du = dup[...]
        for b in range(NB):
            lanes = pl.ds(b * 128, 128)
            dar_ref[:, lanes] = jnp.sum(acc[b][0], axis=0, keepdims=True)
            dai_ref[:, lanes] = jnp.sum(acc[b][1], axis=0, keepdims=True)
            gr, gi = lr[b].astype(BF16), li[b].astype(BF16)
            du = du + (lax.dot_general(gr, wbr_ref[0, :, lanes], nt, preferred_element_type=F32)
                       + lax.dot_general(gi, wbi_ref[0, :, lanes], nt, preferred_element_type=F32))
            dwbr_ref[0, :, lanes] = lax.dot_general(u, gr, tn, preferred_element_type=F32)
            dwbi_ref[0, :, lanes] = lax.dot_general(u, gi, tn, preferred_element_type=F32)
        dup[...] = du

        def write(row, val):
            du_ref[pl.ds(row, 16), :] = val.astype(BF16)

        _from_segs(dup, L, write)

    wspec = pl.BlockSpec((1, ci, cs), lambda j: (j, 0, 0))
    aspec = pl.BlockSpec((1, cs), lambda j: (0, j))
    cspec = pl.BlockSpec((1, cs, ci), lambda j: (j, 0, 0))
    col = pl.BlockSpec((L, ci), lambda j: (0, j))
    return pl.pallas_call(
        body, name="s5_core_bwd", grid=(nb,),
        in_specs=[pl.BlockSpec((L, ci), lambda j: (0, OFF_US // ci + j)), wspec, wspec, aspec, aspec, cspec, cspec, col, col],
        out_specs=[col, wspec, wspec, cspec, cspec, aspec, aspec],
        out_shape=[_sds((L, nb * ci), BF16), _sds(wbr.shape), _sds(wbr.shape), _sds(cbr.shape), _sds(cbr.shape),
                   _sds((1, nb * cs)), _sds((1, nb * cs))],
        scratch_shapes=(_s5_seg_scratch(L, cs, PAD) + [pltpu.VMEM((NB, L, 128), F32), pltpu.VMEM((NB, L, 128), F32)]
                        + [pltpu.VMEM((L, ci), F32) for _ in range(3)]),
        compiler_params=_cp(("arbitrary",)),
    )(proj, wbr, wbi, a_re, a_im, cbr, cbi, dyc, du1)


def _s5_post_math(yc, u, z, d, wg):
    y = yc + d * u
    y1 = jax.nn.gelu(y)
    t = jnp.dot(y1.astype(BF16), wg, preferred_element_type=F32)
    sg = _sigmoid(t)
    return y, y1, sg


def _s5_post_fwd(yc, proj, d, wg):
    L, W = yc.shape
    tm = min(256, L)

    def body(yc_ref, u_ref, z_ref, d_ref, wg_ref, o_ref):
        _, y1, sg = _s5_post_math(yc_ref[...], u_ref[...], z_ref[...], d_ref[...], wg_ref[...])
        o_ref[...] = (y1 * sg * _silu(z_ref[...])).astype(BF16)

    row = pl.BlockSpec((tm, W), lambda i: (i, 0))
    return pl.pallas_call(
        body, name="s5_post_fwd", grid=(L // tm,),
        in_specs=[row, pl.BlockSpec((tm, W), lambda i: (i, OFF_US // W)), pl.BlockSpec((tm, W), lambda i: (i, OFF_ZS // W)),
                  pl.BlockSpec((1, W), lambda i: (0, 0)), pl.BlockSpec((W, W), lambda i: (0, 0))],
        out_specs=row, out_shape=_sds((L, W), BF16), compiler_params=_cp(("parallel",)),
    )(yc, proj, proj, d, wg)


def _s5_post_bwd(yc, proj, d, wg, dout):
    L, W = yc.shape
    tm = min(256, L)

    def body(yc_ref, u_ref, z_ref, d_ref, wg_ref, do_ref, dyc_ref, du_ref, dz_ref, dd_ref, dwg_ref):
        i = pl.program_id(0)
        u, z, d_, wgv = u_ref[...], z_ref[...], d_ref[...], wg_ref[...]
        y, y1, sg = _s5_post_math(yc_ref[...], u, z, d_, wgv)
        dout_ = do_ref[...]
        y2 = y1 * sg
        dy2 = dout_ * _silu(z)
        dz_ref[...] = (dout_ * y2 * _dsilu(z)).astype(BF16)
        dt = (dy2 * y1 * sg * (1.0 - sg)).astype(BF16)
        dy1 = dy2 * sg + lax.dot_general(dt, wgv, (((1,), (1,)), ((), ())), preferred_element_type=F32)
        _, gelu_vjp = jax.vjp(jax.nn.gelu, y)
        dy = gelu_vjp(dy1)[0]
        dyc_ref[...] = dy
        du_ref[...] = dy * d_
        dd_part = jnp.sum(dy * u, axis=0, keepdims=True)
        dwg_part = lax.dot_general(y1.astype(BF16), dt, (((0,), (0,)), ((), ())), preferred_element_type=F32)

        @pl.when(i == 0)
        def _():
            dd_ref[...] = dd_part
            dwg_ref[...] = dwg_part

        @pl.when(i > 0)
        def _():
            dd_ref[...] += dd_part
            dwg_ref[...] += dwg_part

    row = pl.BlockSpec((tm, W), lambda i: (i, 0))
    return pl.pallas_call(
        body, name="s5_post_bwd", grid=(L // tm,),
        in_specs=[row, pl.BlockSpec((tm, W), lambda i: (i, OFF_US // W)), pl.BlockSpec((tm, W), lambda i: (i, OFF_ZS // W)),
                  pl.BlockSpec((1, W), lambda i: (0, 0)), pl.BlockSpec((W, W), lambda i: (0, 0)), row],
        out_specs=[row, row, row, pl.BlockSpec((1, W), lambda i: (0, 0)), pl.BlockSpec((W, W), lambda i: (0, 0))],
        out_shape=[_sds((L, W)), _sds((L, W)), _sds((L, W), BF16), _sds((1, W)), _sds((W, W))],
        compiler_params=_cp(("arbitrary",)),
    )(yc, proj, proj, d, wg, dout)


def _shift_down(x, s):
    if s == 0:
        return x
    rows = lax.broadcasted_iota(jnp.int32, x.shape, 0)
    return jnp.where(rows >= s, pltpu.roll(x, s, 0), 0.0)


def _shift_up(x, s):
    if s == 0:
        return x
    L = x.shape[0]
    rows = lax.broadcasted_iota(jnp.int32, x.shape, 0)
    return jnp.where(rows < L - s, pltpu.roll(x, L - s, 0), 0.0)


def _conv_pre(x, w):
    acc = w[CONV_K - 1:CONV_K, :] * x
    for s in range(1, CONV_K):
        acc = acc + w[CONV_K - 1 - s:CONV_K - s, :] * _shift_down(x, s)
    return acc


def _dn_conv_fwd(proj, conv_w):
    L = proj.shape[0]
    W = DN_HEAD_DIM
    nq = 2 * DN_HEADS

    def body(x_ref, w_ref, o_ref):
        j = pl.program_id(0)
        act = _silu(_conv_pre(x_ref[...], w_ref[...]))
        r = lax.rsqrt(jnp.sum(act * act, axis=-1, keepdims=True) + EPS)
        scale = jnp.where(j < DN_HEADS, DN_HEAD_DIM ** -0.5, 1.0)
        o_ref[...] = jnp.where(j < nq, act * r * scale, act)

    return pl.pallas_call(
        body, name="dn_conv_fwd", grid=(3 * DN_HEADS,),
        in_specs=[pl.BlockSpec((L, W), lambda j: (0, OFF_Q // W + j)), pl.BlockSpec((CONV_K, W), lambda j: (0, j))],
        out_specs=pl.BlockSpec((L, W), lambda j: (0, j)), out_shape=_sds((L, 3 * D_DN)), compiler_params=_cp(("parallel",)),
    )(proj, conv_w)


def _dn_conv_bwd(proj, conv_w, dout):
    L = proj.shape[0]
    W = DN_HEAD_DIM
    nq = 2 * DN_HEADS

    def body(x_ref, w_ref, do_ref, dx_ref, dw_ref):
        j = pl.program_id(0)
        x, w, dout_ = x_ref[...], w_ref[...], do_ref[...]
        pre = _conv_pre(x, w)
        act = _silu(pre)
        r = lax.rsqrt(jnp.sum(act * act, axis=-1, keepdims=True) + EPS)
        scale = jnp.where(j < DN_HEADS, DN_HEAD_DIM ** -0.5, 1.0)
        g = dout_ * scale
        dact_n = r * g - act * (r * r * r) * jnp.sum(g * act, axis=-1, keepdims=True)
        dact = jnp.where(j < nq, dact_n, dout_)
        dpre = dact * _dsilu(pre)
        dx = w[CONV_K - 1:CONV_K, :] * dpre
        for s in range(1, CONV_K):
            dx = dx + w[CONV_K - 1 - s:CONV_K - s, :] * _shift_up(dpre, s)
        dx_ref[...] = dx.astype(BF16)
        for s in range(CONV_K):
            dw_ref[pl.ds(CONV_K - 1 - s, 1), :] = jnp.sum(dpre * _shift_down(x, s), axis=0, keepdims=True)

    col = pl.BlockSpec((L, W), lambda j: (0, j))
    wsp = pl.BlockSpec((CONV_K, W), lambda j: (0, j))
    return pl.pallas_call(
        body, name="dn_conv_bwd", grid=(3 * DN_HEADS,),
        in_specs=[pl.BlockSpec((L, W), lambda j: (0, OFF_Q // W + j)), wsp, col], out_specs=[col, wsp],
        out_shape=[_sds((L, 3 * D_DN), BF16), _sds((CONV_K, 3 * D_DN))], compiler_params=_cp(("parallel",)),
    )(proj, conv_w, dout)


def _softplus(x):
    return jnp.maximum(x, 0.0) + jnp.log(1.0 + jnp.exp(-jnp.abs(x)))


def _dn_gates_fwd(proj, alog, dtb):
    L = proj.shape[0]
    W = 128

    def body(p_ref, al_ref, db_ref, o_ref):
        p = p_ref[...]
        lane = lax.broadcasted_iota(jnp.int32, p.shape, 1)
        g = -jnp.exp(al_ref[...]) * _softplus(p + db_ref[...])
        o_ref[...] = jnp.where(lane < DN_HEADS, _sigmoid(p), jnp.where(lane < 2 * DN_HEADS, g, 0.0))

    return pl.pallas_call(
        body, name="dn_gates_fwd", grid=(1,),
        in_specs=[pl.BlockSpec((L, W), lambda i: (0, OFF_B // W)), pl.BlockSpec((1, W), lambda i: (0, 0)),
                  pl.BlockSpec((1, W), lambda i: (0, 0))],
        out_specs=pl.BlockSpec((L, W), lambda i: (0, 0)), out_shape=_sds((L, W)), compiler_params=_cp(("arbitrary",)),
    )(proj, alog, dtb)


def _dn_gates_bwd(proj, alog, dtb, dgates):
    L = proj.shape[0]
    W = 128

    def body(p_ref, al_ref, db_ref, dg_ref, dp_ref, dal_ref, ddb_ref):
        p, dg = p_ref[...], dg_ref[...]
        lane = lax.broadcasted_iota(jnp.int32, p.shape, 1)
        is_g = jnp.logical_and(lane >= DN_HEADS, lane < 2 * DN_HEADS)
        beta = _sigmoid(p)
        na = -jnp.exp(al_ref[...])
        xs = p + db_ref[...]
        dsp = dg * na * _sigmoid(xs)
        dp_ref[...] = jnp.where(lane < DN_HEADS, dg * beta * (1.0 - beta), jnp.where(is_g, dsp, 0.0)).astype(BF16)
        dal_ref[...] = jnp.sum(jnp.where(is_g, dg * na * _softplus(xs), 0.0), axis=0, keepdims=True)
        ddb_ref[...] = jnp.sum(jnp.where(is_g, dsp, 0.0), axis=0, keepdims=True)

    one = pl.BlockSpec((1, W), lambda i: (0, 0))
    full = pl.BlockSpec((L, W), lambda i: (0, 0))
    return pl.pallas_call(
        body, name="dn_gates_bwd", grid=(1,),
        in_specs=[pl.BlockSpec((L, W), lambda i: (0, OFF_B // W)), one, one, full], out_specs=[full, one, one],
        out_shape=[_sds((L, W), BF16), _sds((1, W)), _sds((1, W))], compiler_params=_cp(("arbitrary",)),
    )(proj, alog, dtb, dgates)


def _bdot(a, b, dims):
    return lax.dot_general(a.astype(BF16), b.astype(BF16), (dims, ((), ())), preferred_element_type=F32)


_NN, _NT, _TN = ((1,), (0,)), ((1,), (1,)), ((0,), (0,))


def _dot3(a, b, dims):
    ah, bh = a.astype(BF16), b.astype(BF16)
    al, bl = (a - ah.astype(F32)).astype(BF16), (b - bh.astype(F32)).astype(BF16)
    (ca,), (cb,) = dims
    a3 = jnp.concatenate([ah, ah, al], axis=ca)
    b3 = jnp.concatenate([bh, bl, bh], axis=cb)
    return lax.dot_general(a3, b3, (dims, ((), ())), preferred_element_type=F32)


def _mm_family(raw):
    nn = jax.custom_vjp(lambda a, b: raw(a, b, _NN))
    nt = jax.custom_vjp(lambda a, b: raw(a, b, _NT))
    tn = jax.custom_vjp(lambda a, b: raw(a, b, _TN))
    nn.defvjp(lambda a, b: (raw(a, b, _NN), (a, b)), lambda r, g: (raw(g, r[1], _NT), raw(r[0], g, _TN)))
    nt.defvjp(lambda a, b: (raw(a, b, _NT), (a, b)), lambda r, g: (raw(g, r[1], _NN), raw(g, r[0], _TN)))
    tn.defvjp(lambda a, b: (raw(a, b, _TN), (a, b)), lambda r, g: (raw(r[1], g, _NT), raw(r[0], g, _NN)))
    return nn, nt, tn


_mm_nn, _mm_nt, _mm_tn = _mm_family(_bdot)
_m3_nn, _m3_nt, _m3_tn = _mm_family(_dot3)


def _tri_apply(x, upper):
    C = x.shape[0]
    ii = lax.broadcasted_iota(jnp.int32, (C, 3 * C), 0)
    jj = lax.broadcasted_iota(jnp.int32, (C, 3 * C), 1) % C
    mat = ((ii <= jj) if upper else (ii >= jj)).astype(BF16)
    hi = x.astype(BF16)
    r = x - hi.astype(F32)
    mid = r.astype(BF16)
    lo = (r - mid.astype(F32)).astype(BF16)
    return jnp.dot(mat, jnp.concatenate([hi, mid, lo], axis=0), preferred_element_type=F32)


_cumsum_rows = jax.custom_vjp(lambda x: _tri_apply(x, False))
_cumsum_rows.defvjp(lambda x: (_tri_apply(x, False), None), lambda _, g: (_tri_apply(g, True),))


def _uli(a_s):
    C = a_s[0].shape[0]
    ii = lax.broadcasted_iota(jnp.int32, (C, C), 0)
    jj = lax.broadcasted_iota(jnp.int32, (C, C), 1)
    eye = jnp.where(ii == jj, 1.0, 0.0)
    ts = [eye - a for a in a_s]
    ms = list(a_s)
    for _ in range(int(math.log2(C)) - 1):
        ms = [_dot3(m, m, _NN) for m in ms]
        ts = [t + _dot3(t, m, _NN) for t, m in zip(ts, ms)]
    return tuple(ts)


def _uli_bwd(ts, gs):
    xs = [_dot3(t, g, _TN) for t, g in zip(ts, gs)]
    return (tuple(-_dot3(x, t, _NT) for x, t in zip(xs, ts)),)


_unit_lower_inverse = jax.custom_vjp(_uli)
_unit_lower_inverse.defvjp(lambda a_s: (lambda ts: (ts, ts))(_uli(a_s)), _uli_bwd)


def _prep_math(qs, ks, vs, gcols, bcols):
    n = len(qs)
    C, dv = vs[0].shape
    ii = lax.broadcasted_iota(jnp.int32, (C, C), 0)
    jj = lax.broadcasted_iota(jnp.int32, (C, C), 1)
    causal = ii >= jj
    strict = ii > jj
    sf = strict.astype(F32)
    ones = jnp.ones((C, dv), F32)
    dms = [_cumsum_rows(g * sf) for g in gcols]
    gcbs = [_cumsum_rows(g * ones) for g in gcols]
    kks = [_mm_nt(k, k) for k in ks]
    qks = [_mm_nt(q, k) for q, k in zip(qs, ks)]
    decays = [jnp.where(causal, jnp.exp(jnp.where(causal, dm, 0.0)), 0.0) for dm in dms]
    glasts = [jnp.sum(g * ones, axis=0, keepdims=True) for g in gcols]
    egs = [jnp.exp(gcb) for gcb in gcbs]
    ts = _unit_lower_inverse(tuple(jnp.where(strict, b * kk * dc, 0.0) for b, kk, dc in zip(bcols, kks, decays)))
    us = [_m3_nn(t, v * b) for t, v, b in zip(ts, vs, bcols)]
    ws = [_m3_nn(t, k * b * eg) for t, k, b, eg in zip(ts, ks, bcols, egs)]
    return tuple((us[i], ws[i], qs[i] * egs[i], ks[i] * jnp.exp(glasts[i] - gcbs[i]), qks[i] * decays[i],
                  jnp.exp(glasts[i])) for i in range(n))


def _gate_cols(gates, h):
    lane = lax.broadcasted_iota(jnp.int32, gates.shape, 1)
    bcol = jnp.sum(jnp.where(lane == h, gates, 0.0), axis=1, keepdims=True)
    gcol = jnp.sum(jnp.where(lane == h + DN_HEADS, gates, 0.0), axis=1, keepdims=True)
    return gcol, bcol


DN_HB = 8


def _dn_prep_fwd(qkv, gates):
    L = qkv.shape[0]
    N, H, d, HB = L // CHUNK, DN_HEADS, DN_HEAD_DIM, DN_HB

    def body(q_ref, k_ref, v_ref, g_ref, u_ref, w_ref, qd_ref, kd_ref, qk_ref, egl_ref):
        h0 = pl.program_id(1) * HB
        gates_ = g_ref[...]
        lanes_of = [pl.ds(i * d, d) for i in range(HB)]
        cols = [_gate_cols(gates_, h0 + i) for i in range(HB)]
        outs = _prep_math([q_ref[:, l] for l in lanes_of], [k_ref[:, l] for l in lanes_of], [v_ref[:, l] for l in lanes_of],
                          [c[0] for c in cols], [c[1] for c in cols])
        for i in range(HB):
            lanes = lanes_of[i]
            u, w, qd, kd, qk, egl = outs[i]
            u_ref[:, lanes] = u
            w_ref[:, lanes] = w
            qd_ref[:, lanes] = qd
            kd_ref[:, lanes] = kd
            qk_ref[0, i] = qk
            egl_ref[0, i] = jnp.broadcast_to(egl, (8, d))

    blk = lambda off: pl.BlockSpec((CHUNK, HB * d), lambda n, j: (n, off // HB + j))
    cc = pl.BlockSpec((1, HB, CHUNK, CHUNK), lambda n, j: (n, j, 0, 0))
    ee = pl.BlockSpec((1, HB, 8, d), lambda n, j: (n, j, 0, 0))
    big = _sds((L, D_DN))
    return pl.pallas_call(
        body, name="dn_prep_fwd", grid=(N, H // HB),
        in_specs=[blk(0), blk(H), blk(2 * H), pl.BlockSpec((CHUNK, 128), lambda n, j: (n, 0))],
        out_specs=[blk(0), blk(0), blk(0), blk(0), cc, ee],
        out_shape=[big, big, big, big, _sds((N, H, CHUNK, CHUNK)), _sds((N, H, 8, d))],
        compiler_params=_cp(("parallel", "parallel")),
    )(qkv, qkv, qkv, gates)


def _dn_scan_fwd(u, w, qd, kd, qk, egl):
    L = u.shape[0]
    N, H, d, HB = L // CHUNK, DN_HEADS, DN_HEAD_DIM, DN_HB

    def body(u_ref, w_ref, qd_ref, kd_ref, qk_ref, egl_ref, o_ref, st_ref, s_ref):
        n, h0 = pl.program_id(0), pl.program_id(1) * HB

        @pl.when(n == 0)
        def _():
            for i in range(HB):
                s_ref[h0 + i] = jnp.zeros((d, d), F32)

        hs = range(HB)
        ln = [pl.ds(i * d, d) for i in hs]
        st = [s_ref[h0 + i] for i in hs]
        ws = [_bdot(w_ref[:, ln[i]], st[i], _NN) for i in hs]
        qs = [_bdot(qd_ref[:, ln[i]], st[i], _NN) for i in hs]
        vn = [u_ref[:, ln[i]] - ws[i] for i in hs]
        qv = [_bdot(qk_ref[0, i], vn[i], _NN) for i in hs]
        kv = [_bdot(kd_ref[:, ln[i]], vn[i], _TN) for i in hs]
        for i in hs:
            st_ref[0, i] = st[i]
            o_ref[:, ln[i]] = qs[i] + qv[i]
            s_ref[h0 + i] = st[i] * egl_ref[0, i, pl.ds(0, 1), :] + kv[i]

    blk = pl.BlockSpec((CHUNK, HB * d), lambda n, j: (n, j))
    cc = pl.BlockSpec((1, HB, CHUNK, CHUNK), lambda n, j: (n, j, 0, 0))
    ee = pl.BlockSpec((1, HB, 8, d), lambda n, j: (n, j, 0, 0))
    return pl.pallas_call(
        body, name="dn_scan_fwd", grid=(N, H // HB), in_specs=[blk, blk, blk, blk, cc, ee],
        out_specs=[blk, pl.BlockSpec((1, HB, d, d), lambda n, j: (n, j, 0, 0))],
        out_shape=[_sds((L, D_DN)), _sds((N, H, d, d))], scratch_shapes=[pltpu.VMEM((H, d, d), F32)],
        compiler_params=_cp(("arbitrary", "arbitrary")),
    )(u, w, qd, kd, qk, egl)


def _dn_scan_bwd(u, w, qd, kd, qk, egl, states, do):
    L = u.shape[0]
    N, H, d, HB = L // CHUNK, DN_HEADS, DN_HEAD_DIM, DN_HB

    def body(u_ref, w_ref, qd_ref, kd_ref, qk_ref, egl_ref, st_ref, do_ref,
             du_ref, dw_ref, dqd_ref, dkd_ref, dqk_ref, degl_ref, ds_ref):
        n, h0 = pl.program_id(0), pl.program_id(1) * HB

        @pl.when(n == 0)
        def _():
            for i in range(HB):
                ds_ref[h0 + i] = jnp.zeros((d, d), F32)

        hs = range(HB)
        ln = [pl.ds(i * d, d) for i in hs]
        st = [st_ref[0, i] for i in hs]
        dsn = [ds_ref[h0 + i] for i in hs]
        do_ = [do_ref[:, ln[i]] for i in hs]
        ws = [_bdot(w_ref[:, ln[i]], st[i], _NN) for i in hs]
        d1 = [_bdot(qk_ref[0, i], do_[i], _TN) for i in hs]
        d2 = [_bdot(kd_ref[:, ln[i]], dsn[i], _NN) for i in hs]
        dqd = [_bdot(do_[i], st[i], _NT) for i in hs]
        qdo = [_bdot(qd_ref[:, ln[i]], do_[i], _TN) for i in hs]
        vn = [u_ref[:, ln[i]] - ws[i] for i in hs]
        dvn = [d1[i] + d2[i] for i in hs]
        dw = [_bdot(dvn[i], st[i], _NT) for i in hs]
        dkd = [_bdot(vn[i], dsn[i], _NT) for i in hs]
        dqk = [_bdot(do_[i], vn[i], _NT) for i in hs]
        wdv = [_bdot(w_ref[:, ln[i]], dvn[i], _TN) for i in hs]
        for i in hs:
            du_ref[:, ln[i]] = dvn[i]
            dw_ref[:, ln[i]] = -dw[i]
            dqd_ref[:, ln[i]] = dqd[i]
            dkd_ref[:, ln[i]] = dkd[i]
            dqk_ref[0, i] = dqk[i]
            degl_ref[0, i] = jnp.broadcast_to(jnp.sum(dsn[i] * st[i], keepdims=True), (8, d))
            ds_ref[h0 + i] = (qdo[i] - wdv[i]) + dsn[i] * egl_ref[0, i, pl.ds(0, 1), :]

    blk = pl.BlockSpec((CHUNK, HB * d), lambda n, j: (N - 1 - n, j))
    cc = pl.BlockSpec((1, HB, CHUNK, CHUNK), lambda n, j: (N - 1 - n, j, 0, 0))
    ee = pl.BlockSpec((1, HB, 8, d), lambda n, j: (N - 1 - n, j, 0, 0))
    ss = pl.BlockSpec((1, HB, d, d), lambda n, j: (N - 1 - n, j, 0, 0))
    big = _sds((L, D_DN))
    return pl.pallas_call(
        body, name="dn_scan_bwd", grid=(N, H // HB), in_specs=[blk, blk, blk, blk, cc, ee, ss, blk],
        out_specs=[blk, blk, blk, blk, cc, ee],
        out_shape=[big, big, big, big, _sds((N, H, CHUNK, CHUNK)), _sds((N, H, 8, d))],
        scratch_shapes=[pltpu.VMEM((H, d, d), F32)], compiler_params=_cp(("arbitrary", "arbitrary")),
    )(u, w, qd, kd, qk, egl, states, do)


def _dn_prep_bwd(qkv, gates, du, dw, dqd, dkd, dqk, degl):
    L = qkv.shape[0]
    N, H, d, HB = L // CHUNK, DN_HEADS, DN_HEAD_DIM, DN_HB

    def body(q_ref, k_ref, v_ref, g_ref, du_ref, dw_ref, dqd_ref, dkd_ref, dqk_ref, degl_ref, dq_ref, dk_ref, dv_ref, dg_ref):
        j = pl.program_id(1)
        h0 = j * HB
        gates_ = g_ref[...]
        lane = lax.broadcasted_iota(jnp.int32, gates_.shape, 1)
        lane1 = lax.broadcasted_iota(jnp.int32, (1, d), 1)
        part = jnp.zeros(gates_.shape, F32)
        lanes_of = [pl.ds(i * d, d) for i in range(HB)]
        cols = [_gate_cols(gates_, h0 + i) for i in range(HB)]
        _, f = jax.vjp(_prep_math, [q_ref[:, l] for l in lanes_of], [k_ref[:, l] for l in lanes_of],
                       [v_ref[:, l] for l in lanes_of], [c[0] for c in cols], [c[1] for c in cols])
        cots = tuple((du_ref[:, l], dw_ref[:, l], dqd_ref[:, l], dkd_ref[:, l], dqk_ref[0, i],
                      jnp.where(lane1 == 0, degl_ref[0, i, pl.ds(0, 1), :], 0.0)) for i, l in enumerate(lanes_of))
        dqs, dks, dvs, dgcs, dbcs = f(cots)
        for i in range(HB):
            lanes = lanes_of[i]
            dq_ref[:, lanes] = dqs[i]
            dk_ref[:, lanes] = dks[i]
            dv_ref[:, lanes] = dvs[i]
            part = part + jnp.where(lane == h0 + i, dbcs[i], 0.0) + jnp.where(lane == h0 + i + DN_HEADS, dgcs[i], 0.0)

        @pl.when(j == 0)
        def _():
            dg_ref[...] = part

        @pl.when(j > 0)
        def _():
            dg_ref[...] += part

    blk = lambda off: pl.BlockSpec((CHUNK, HB * d), lambda n, j: (n, off // HB + j))
    gsp = pl.BlockSpec((CHUNK, 128), lambda n, j: (n, 0))
    cc = pl.BlockSpec((1, HB, CHUNK, CHUNK), lambda n, j: (n, j, 0, 0))
    ee = pl.BlockSpec((1, HB, 8, d), lambda n, j: (n, j, 0, 0))
    big = _sds((L, D_DN))
    return pl.pallas_call(
        body, name="dn_prep_bwd", grid=(N, H // HB),
        in_specs=[blk(0), blk(H), blk(2 * H), gsp, blk(0), blk(0), blk(0), blk(0), cc, ee],
        out_specs=[blk(0), blk(0), blk(0), gsp], out_shape=[big, big, big, _sds((L, 128))],
        compiler_params=_cp(("parallel", "arbitrary")),
    )(qkv, qkv, qkv, gates, du, dw, dqd, dkd, dqk, degl)


def _dn_post_fwd(o, proj, nw):
    L = o.shape[0]
    d = DN_HEAD_DIM
    tm = min(512, L)

    def body(o_ref, z_ref, w_ref, y_ref):
        ov = o_ref[...]
        r = lax.rsqrt(jnp.mean(ov * ov, axis=-1, keepdims=True) + EPS)
        y_ref[...] = (ov * r * w_ref[...] * _silu(z_ref[...])).astype(BF16)

    blk = pl.BlockSpec((tm, d), lambda i, h: (i, h))
    return pl.pallas_call(
        body, name="dn_post_fwd", grid=(L // tm, DN_HEADS),
        in_specs=[blk, pl.BlockSpec((tm, d), lambda i, h: (i, OFF_ZD // d + h)), pl.BlockSpec((1, d), lambda i, h: (0, 0))],
        out_specs=blk, out_shape=_sds((L, D_DN), BF16), compiler_params=_cp(("parallel", "parallel")),
    )(o, proj, nw)


def _dn_post_bwd(o, proj, nw, dy):
    L = o.shape[0]
    d = DN_HEAD_DIM
    tm = min(512, L)

    def body(o_ref, z_ref, w_ref, dy_ref, do_ref, dz_ref, dw_ref):
        first = jnp.logical_and(pl.program_id(0) == 0, pl.program_id(1) == 0)
        ov, z, w, dyv = o_ref[...], z_ref[...], w_ref[...], dy_ref[...]
        r = lax.rsqrt(jnp.mean(ov * ov, axis=-1, keepdims=True) + EPS)
        xn = ov * r
        dz_ref[...] = (dyv * xn * w * _dsilu(z)).astype(BF16)
        dn = dyv * _silu(z)
        t = dn * w
        do_ref[...] = r * t - ov * (r * r * r) * jnp.mean(t * ov, axis=-1, keepdims=True)
        part = jnp.sum(dn * xn, axis=0, keepdims=True)

        @pl.when(first)
        def _():
            dw_ref[...] = part

        @pl.when(jnp.logical_not(first))
        def _():
            dw_ref[...] += part

    blk = pl.BlockSpec((tm, d), lambda i, h: (i, h))
    one = pl.BlockSpec((1, d), lambda i, h: (0, 0))
    return pl.pallas_call(
        body, name="dn_post_bwd", grid=(L // tm, DN_HEADS),
        in_specs=[blk, pl.BlockSpec((tm, d), lambda i, h: (i, OFF_ZD // d + h)), one, blk], out_specs=[blk, blk, one],
        out_shape=[_sds((L, D_DN)), _sds((L, D_DN), BF16), _sds((1, d))], compiler_params=_cp(("arbitrary", "arbitrary")),
    )(o, proj, nw, dy)


def _mix_fwd(s5o, dno, w_su, w_du, proj):
    L, K = s5o.shape
    N = w_su.shape[1]
    tm, tn = min(512, L), 512

    def body(a1, a2, b1, b2, gs, gd, ys_ref, yd_ref, mx_ref):
        ys = jnp.dot(a1[...], b1[...], preferred_element_type=F32)
        yd = jnp.dot(a2[...], b2[...], preferred_element_type=F32)
        ys_ref[...] = ys
        yd_ref[...] = yd
        mx_ref[...] = (_sigmoid(gs[...]) * ys + _sigmoid(gd[...]) * yd).astype(BF16)

    a = pl.BlockSpec((tm, K), lambda i, j: (i, 0))
    b = pl.BlockSpec((K, tn), lambda i, j: (0, j))
    o = pl.BlockSpec((tm, tn), lambda i, j: (i, j))
    return pl.pallas_call(
        body, name="mix_fwd", grid=(L // tm, N // tn),
        in_specs=[a, a, b, b, pl.BlockSpec((tm, tn), lambda i, j: (i, OFF_GS // tn + j)),
                  pl.BlockSpec((tm, tn), lambda i, j: (i, OFF_GD // tn + j))],
        out_specs=[o, o, o], out_shape=[_sds((L, N)), _sds((L, N)), _sds((L, N), BF16)],
        compiler_params=_cp(("parallel", "parallel")),
    )(s5o, dno, w_su, w_du, proj, proj)


def _mix_bwd(dx2b, w_out, proj, ys, yd):
    L, K = dx2b.shape
    N = w_out.shape[0]
    tm, tn = min(512, L), 512

    def body(a, b, gs, gd, ys_ref, yd_ref, dgs_ref, dgd_ref, dys_ref, dyd_ref):
        dm = lax.dot_general(a[...], b[...], (((1,), (1,)), ((), ())), preferred_element_type=F32)
        ss, sd = _sigmoid(gs[...]), _sigmoid(gd[...])
        dys_ref[...] = (dm * ss).astype(BF16)
        dyd_ref[...] = (dm * sd).astype(BF16)
        dgs_ref[...] = (dm * ys_ref[...] * ss * (1.0 - ss)).astype(BF16)
        dgd_ref[...] = (dm * yd_ref[...] * sd * (1.0 - sd)).astype(BF16)

    o = pl.BlockSpec((tm, tn), lambda i, j: (i, j))
    return pl.pallas_call(
        body, name="mix_bwd", grid=(L // tm, N // tn),
        in_specs=[pl.BlockSpec((tm, K), lambda i, j: (i, 0)), pl.BlockSpec((tn, K), lambda i, j: (j, 0)),
                  pl.BlockSpec((tm, tn), lambda i, j: (i, OFF_GS // tn + j)),
                  pl.BlockSpec((tm, tn), lambda i, j: (i, OFF_GD // tn + j)), o, o],
        out_specs=[o, o, o, o], out_shape=[_sds((L, N), BF16)] * 4, compiler_params=_cp(("parallel", "parallel")),
    )(dx2b, w_out, proj, proj, ys, yd)


def _final(mixed, w_out, x, tgt, fw):
    L, D = x.shape
    tm = min(256, L)

    def body(a_ref, b_ref, x_ref, t_ref, w_ref, dx_ref, dxb_ref, loss_ref, dw_ref):
        i = pl.program_id(0)
        x2 = x_ref[...] + jnp.dot(a_ref[...], b_ref[...], preferred_element_type=F32)
        w = w_ref[...]
        r = lax.rsqrt(jnp.mean(x2 * x2, axis=-1, keepdims=True) + EPS)
        xn = x2 * r
        e = xn * w - t_ref[...]
        lpart = 0.5 * jnp.sum(jnp.mean(e * e, axis=-1, keepdims=True), axis=0, keepdims=True)
        dy = e * (1.0 / D)
        t = dy * w
        dx2 = r * t - x2 * (r * r * r) * jnp.mean(t * x2, axis=-1, keepdims=True)
        dx_ref[...] = dx2
        dxb_ref[...] = dx2.astype(BF16)
        dwp = jnp.sum(dy * xn, axis=0, keepdims=True)
        lrow = jnp.broadcast_to(lpart, loss_ref.shape)

        @pl.when(i == 0)
        def _():
            loss_ref[...] = lrow
            dw_ref[...] = dwp

        @pl.when(i > 0)
        def _():
            loss_ref[...] += lrow
            dw_ref[...] += dwp

    row = pl.BlockSpec((tm, D), lambda i: (i, 0))
    one = pl.BlockSpec((1, D), lambda i: (0, 0))
    return pl.pallas_call(
        body, name="final", grid=(L // tm,),
        in_specs=[row, pl.BlockSpec((D, D), lambda i: (0, 0)), row, row, one],
        out_specs=[row, row, pl.BlockSpec((1, 128), lambda i: (0, 0)), one],
        out_shape=[_sds((L, D)), _sds((L, D), BF16), _sds((1, 128)), _sds((1, D))], compiler_params=_cp(("arbitrary",)),
    )(mixed, w_out, x, tgt, fw)


def _block_diag(t):
    J, g, a, b = t.shape
    eye = jnp.eye(g, dtype=t.dtype)
    return (t[:, :, :, None, :] * eye[None, :, None, :, None]).reshape(J, g * a, g * b)


def _block_diag_take(m, g):
    J, ga, gb = m.shape
    a, b = ga // g, gb // g
    m5 = m.reshape(J, g, a, g, b)
    idx = jnp.arange(g)
    return m5[:, idx, :, idx, :].transpose(1, 0, 2, 3)


class _PlainOps:
    def __init__(self, w_rest):
        self.w_rest = w_rest

    def in_proj(self, h, wt_perm):
        return _mm(h, wt_perm, tb=True, name="in_proj", tm=1024, tn=1152), self.w_rest

    def rest_grads(self, d_w_glu, d_w_su, d_w_du, d_w_out):
        pass

    def d_w_in(self, h, dproj):
        return _mm(dproj, h, ta=True, name="d_w_in", tm=1152, tn=1024)

    def d_h(self, dproj, wt_perm, d_wt_perm):
        return _mm(dproj, wt_perm, name="d_h", tm=2048, tn=1024, tk=1152)


def _local_step(x, tgt, ln_w, w_perm, lam_re, lam_im, log_step, b_re, b_im, c_re, c_im, s5_d,
                conv_w, a_log, dt_bias, norm_w, fw, ops):
    G, P, gb = S5_GROUPS, S5_STATE, S5_GROUPS // S5_BLOCKS
    h, rstd = _ln_fwd(x, ln_w)
    proj, (w_glu, w_su, w_du, w_out) = ops.in_proj(h, w_perm)

    b_re2, b_im2 = b_re.reshape(G, P * S5_GROUP), b_im.reshape(G, P * S5_GROUP)
    ls2 = log_step.reshape(G, 1)
    abar_re, abar_im, bb_re, bb_im = _s5_param_fwd(lam_re, lam_im, ls2, b_re2, b_im2)

    def to_wb(bb):
        return _block_diag(bb.reshape(S5_BLOCKS, gb, P, S5_GROUP).transpose(0, 1, 3, 2)).astype(BF16)

    def to_cb(cc):
        return _block_diag(cc.reshape(S5_BLOCKS, gb, S5_GROUP, P).transpose(0, 1, 3, 2)).astype(BF16)

    wbr, wbi, cbr, cbi = to_wb(bb_re), to_wb(bb_im), to_cb(c_re), to_cb(c_im)
    a_re_row, a_im_row = abar_re.reshape(1, G * P), abar_im.reshape(1, G * P)
    yc = _s5_core_fwd(proj, wbr, wbi, a_re_row, a_im_row, cbr, cbi)
    s5o = _s5_post_fwd(yc, proj, s5_d, w_glu)

    pad = lambda v: jnp.pad(v, ((0, 0), (DN_HEADS, 128 - 2 * DN_HEADS)))
    alog_row, dtb_row = pad(a_log), pad(dt_bias)
    qkv = _dn_conv_fwd(proj, conv_w)
    gates = _dn_gates_fwd(proj, alog_row, dtb_row)
    prep = _dn_prep_fwd(qkv, gates)
    o_dn, states = _dn_scan_fwd(*prep)
    dno = _dn_post_fwd(o_dn, proj, norm_w)

    ys, yd, mixed = _mix_fwd(s5o, dno, w_su, w_du, proj)
    dx2, dx2b, loss_row, d_fw = _final(mixed, w_out, x, tgt, fw)
    d_w_out = _mm(mixed, dx2b, ta=True, name="d_w_out")
    dgs, dgd, dys, dyd = _mix_bwd(dx2b, w_out, proj, ys, yd)
    d_w_su = _mm(s5o, dys, ta=True, name="d_w_su", shard_out=True)
    d_w_du = _mm(dno, dyd, ta=True, name="d_w_du", shard_out=True)
    ds5o = _mm(dys, w_su, tb=True, name="d_s5o")
    ddno = _mm(dyd, w_du, tb=True, name="d_dno")

    dyc, du1, dz_s, d_s5d, d_w_glu = _s5_post_bwd(yc, proj, s5_d, w_glu, ds5o)
    ops.rest_grads(d_w_glu, d_w_su, d_w_du, d_w_out)
    du, dwbr, dwbi, dcbr, dcbi, dar, dai = _s5_core_bwd(proj, wbr, wbi, a_re_row, a_im_row, cbr, cbi, dyc, du1)

    def from_wb(dwb):
        return _block_diag_take(dwb, gb).transpose(0, 1, 3, 2).reshape(G, P * S5_GROUP)

    def from_cb(dcb):
        return _block_diag_take(dcb, gb).transpose(0, 1, 3, 2).reshape(G, S5_GROUP, P)

    d_lam_re, d_lam_im, d_ls, d_b_re, d_b_im = _s5_param_bwd(
        lam_re, lam_im, ls2, b_re2, b_im2, dar.reshape(G, P), dai.reshape(G, P), from_wb(dwbr), from_wb(dwbi))

    do_dn, dz_d, d_norm_w = _dn_post_bwd(o_dn, proj, norm_w, ddno)
    dq, dk, dv, dgates = _dn_prep_bwd(qkv, gates, *_dn_scan_bwd(*prep, states, do_dn))
    dqkv, d_conv = _dn_conv_bwd(proj, conv_w, jnp.concatenate([dq, dk, dv], axis=1))
    dpb, d_alog_row, d_dtb_row = _dn_gates_bwd(proj, alog_row, dtb_row, dgates)

    dproj = jnp.concatenate([du, dz_s, dqkv, dz_d, dgs, dgd, dpb], axis=1)
    d_w_perm = ops.d_w_in(h, dproj)
    dh = ops.d_h(dproj, w_perm, d_w_perm)
    grad_x, d_ln_w = _ln_bwd(x, rstd, ln_w, dh, dx2)

    grads = dict(
        ln_w=d_ln_w, w_perm=d_w_perm, s5_lam_re=d_lam_re, s5_lam_im=d_lam_im, s5_log_step=d_ls.reshape(1, G),
        s5_b_re=d_b_re.reshape(G, P, S5_GROUP), s5_b_im=d_b_im.reshape(G, P, S5_GROUP),
        s5_c_re=from_cb(dcbr), s5_c_im=from_cb(dcbi), s5_d=d_s5d, s5_w_glu=d_w_glu, s5_w_up=d_w_su,
        dn_conv_w=d_conv, dn_a_log=d_alog_row[:, DN_HEADS:2 * DN_HEADS], dn_dt_bias=d_dtb_row[:, DN_HEADS:2 * DN_HEADS],
        dn_norm_w=d_norm_w, dn_w_up=d_w_du, w_out=d_w_out, final_norm_w=d_fw)
    return loss_row, grad_x, grads


def _place():
    x, y, c = lax.axis_index("x"), lax.axis_index("y"), lax.axis_index("c")
    return x, y, c


def _remote(src, dst, send_sem, recv_sem, to):
    return pltpu.make_async_remote_copy(src_ref=src, dst_ref=dst, send_sem=send_sem, recv_sem=recv_sem,
                                        device_id=to, device_id_type=MESH)


def _gather_exchange(shards, whole=()):
    na, nw = len(shards), len(whole)

    def half_of(ref, a, half):
        rows = shards[a].shape[0]
        return ref.at[pl.ds(half * (rows // 2), rows // 2)]

    def plan(ins, outs, send_sems, recv_sems, local_sems, receiving):
        x, y, c = _place()
        me = 2 * x + y
        sibling = (x, y, 1 - c)
        chips = [(1 - x, y), (x, 1 - y), (1 - x, 1 - y)]

        def part(a, chip, half):
            return half_of(outs[a].at[chip], a, half)

        own = [pltpu.make_async_copy(ins[a], outs[a].at[me], local_sems.at[a]) for a in range(na + nw)]
        sends, landed, passed, arrivals = [], [], [], []
        for a in range(na):
            for j, (px, py) in enumerate(chips):
                k = 6 * a + j
                sends.append(_remote(half_of(ins[a], a, c), part(a, me, c), send_sems.at[k], recv_sems.at[k], (px, py, c)))
                if receiving:
                    got, other = part(a, 2 * px + py, c), part(a, 2 * px + py, 1 - c)
                    landed.append(_remote(got, got, send_sems.at[k], recv_sems.at[k], (px, py, c)))
                    passed.append(_remote(got, got, send_sems.at[k + 3], recv_sems.at[k + 3], sibling))
                    arrivals.append(_remote(other, other, send_sems.at[k + 3], recv_sems.at[k + 3], sibling))
        for a in range(na, na + nw):
            for j, (px, py) in enumerate(chips):
                k = 6 * na + 3 * (a - na) + j
                sends.append(_remote(ins[a], outs[a].at[me], send_sems.at[k], recv_sems.at[k], (px, py, c)))
                if receiving:
                    arrivals.append(_remote(ins[a], outs[a].at[2 * px + py], send_sems.at[k], recv_sems.at[k], (px, py, c)))
        return own, sends, landed, passed, arrivals

    def start(ins, outs, *sems):
        own, sends, _, _, _ = plan(ins, outs, *sems, False)
        for cp in own + sends:
            cp.start()

    def finish(ins, outs, *sems):
        own, sends, landed, passed, arrivals = plan(ins, outs, *sems, True)
        for got, fwd in zip(landed, passed):
            got.wait_recv()
            fwd.start()
        for cp in arrivals:
            cp.wait_recv()
        for cp in sends + passed:
            cp.wait_send()
        for cp in own:
            cp.wait()

    arrays = list(shards) + list(whole)
    return _Exchange(arrays, [_sds((N_CHIPS,) + s.shape, s.dtype) for s in arrays], 6 * na + 3 * nw, start, finish)


def _gather_relayed(shard, whole, name):
    rows, cols = shard.shape
    nw = len(whole)

    def body(*refs):
        in_ref, w_in = refs[0], refs[1:1 + nw]
        out_ref, w_out = refs[1 + nw], refs[2 + nw:2 + 2 * nw]
        send_sems, recv_sems, local_sems = refs[2 + 2 * nw:]
        x, y, c = _place()
        me = 2 * x + y
        near = (jnp.where(c == 1, 1 - x, x), jnp.where(c == 1, y, 1 - y))
        far = (jnp.where(c == 1, x, 1 - x), jnp.where(c == 1, 1 - y, y))
        diag = (1 - x, 1 - y)
        sibling = (x, y, 1 - c)
        chip_of = lambda p: 2 * p[0] + p[1]

        def half(ref, h):
            return ref.at[pl.ds(0, rows), pl.ds(h * (cols // 2), cols // 2)]

        own = [pltpu.make_async_copy(in_ref, out_ref.at[me], local_sems.at[0])]
        own += [pltpu.make_async_copy(w_in[a], w_out[a].at[me], local_sems.at[1 + a]) for a in range(nw)]
        others = [(1 - x, y), (x, 1 - y), (1 - x, 1 - y)]
        small = [_remote(w_in[a], w_out[a].at[me], send_sems.at[4 + 3 * a + j], recv_sems.at[4 + 3 * a + j], (*p, c))
                 for a in range(nw) for j, p in enumerate(others)]
        sends = [_remote(in_ref, out_ref.at[me], send_sems.at[0], recv_sems.at[0], (*near, c))]
        for cp in own + small + sends:
            cp.start()
        from_near = out_ref.at[chip_of(near)]
        _remote(from_near, from_near, send_sems.at[0], recv_sems.at[0], (*near, c)).wait_recv()
        sends.append(_remote(from_near, from_near, send_sems.at[1], recv_sems.at[1], sibling))
        sends[-1].start()
        from_far = out_ref.at[chip_of(far)]
        _remote(from_far, from_far, send_sems.at[1], recv_sems.at[1], sibling).wait_recv()
        sends.append(_remote(half(from_far, c), half(from_far, c), send_sems.at[2], recv_sems.at[2], (*near, c)))
        sends[-1].start()
        of_diag = out_ref.at[chip_of(diag)]
        _remote(half(of_diag, c), half(of_diag, c), send_sems.at[2], recv_sems.at[2], (*near, c)).wait_recv()
        sends.append(_remote(half(of_diag, c), half(of_diag, c), send_sems.at[3], recv_sems.at[3], sibling))
        sends[-1].start()
        _remote(half(of_diag, 1 - c), half(of_diag, 1 - c), send_sems.at[3], recv_sems.at[3], sibling).wait_recv()
        for a in range(nw):
            for j, p in enumerate(others):
                _remote(w_in[a], w_out[a].at[chip_of(p)], send_sems.at[4 + 3 * a + j], recv_sems.at[4 + 3 * a + j], (*p, c)).wait_recv()
        for cp in sends + small:
            cp.wait_send()
        for cp in own:
            cp.wait()

    arrays = [shard] + list(whole)
    n_sems = 4 + 3 * nw
    return pl.pallas_call(
        body, name=name, in_specs=[ANY] * (1 + nw), out_specs=[ANY] * (1 + nw),
        out_shape=[_sds((N_CHIPS,) + a.shape, a.dtype) for a in arrays],
        scratch_shapes=[pltpu.SemaphoreType.DMA((n_sems,)) for _ in range(3)],
    )(*arrays)


def _owners_exchange(csbs):
    na = len(csbs)

    def plan(ins, outs, send_sems, recv_sems, local_sems, receiving):
        x, y, c = _place()
        me = 2 * x + y
        sends, arrivals = [], []
        for a in range(na):
            for k in range(N_CHIPS - 1):
                j = (me + 1 + k) % N_CHIPS
                sends.append(_remote(ins[a].at[k], outs[a].at[2 - k], send_sems.at[3 * a + k], recv_sems.at[3 * a + 2 - k],
                                     (j // 2, j % 2, c)))
                if receiving:
                    arrivals.append(_remote(ins[a].at[k], outs[a].at[k], send_sems.at[3 * a + k], recv_sems.at[3 * a + k], (x, y, c)))
        return sends, arrivals

    def start(ins, outs, *sems):
        for cp in plan(ins, outs, *sems, False)[0]:
            cp.start()

    def finish(ins, outs, *sems):
        sends, arrivals = plan(ins, outs, *sems, True)
        for cp in arrivals:
            cp.wait_recv()
        for cp in sends:
            cp.wait_send()

    return _Exchange(csbs, [_sds(g.shape, g.dtype) for g in csbs], 3 * na, start, finish)


def _swap_halves(gxs, name):
    na = len(gxs)
    half_shape = lambda g: (WT_ROWS // 2, g.shape[1]) if g.ndim == 2 else g.shape[2:]

    def body(*refs):
        ins, outs = refs[:na], refs[na:2 * na]
        send_sems, recv_sems = refs[2 * na:]
        x, y, c = _place()
        cps = []
        for a in range(na):
            if gxs[a].ndim == 2:
                for j in range(N_CHIPS):
                    rows = pl.ds(pl.multiple_of(WT_WIN[j] + (1 - c) * (WT_ROWS // 2), 16), WT_ROWS // 2)
                    cps.append(_remote(ins[a].at[rows], outs[a].at[j, 0], send_sems.at[na + j], recv_sems.at[na + j], (x, y, 1 - c)))
            else:
                cps.append(_remote(ins[a].at[pl.ds(0, N_CHIPS), pl.ds(1 - c, 1)], outs[a], send_sems.at[a], recv_sems.at[a],
                                   (x, y, 1 - c)))
        for cp in cps:
            cp.start()
        for cp in cps:
            cp.wait()

    return pl.pallas_call(
        body, name=name, in_specs=[ANY] * na, out_specs=[ANY] * na,
        out_shape=[_sds((N_CHIPS, 1) + half_shape(g), g.dtype) for g in gxs],
        scratch_shapes=[pltpu.SemaphoreType.DMA((na + N_CHIPS,)), pltpu.SemaphoreType.DMA((na + N_CHIPS,))],
    )(*gxs)


def _half_block(gx, tr, shard):
    if gx.ndim == 4:
        return pl.BlockSpec((1, 1, tr, gx.shape[3]), lambda *g: (shard(*g), g[-1][0], g[-2], 0))
    return pl.BlockSpec(
        (pl.Element(tr), pl.Element(gx.shape[1])),
        lambda *g: (pl.multiple_of(g[-1][2 + shard(*g)] + g[-1][0] * (WT_ROWS // 2) + g[-2] * tr, 16), 0))


def _share_halves(gfs):
    na = len(gfs)

    def body(*refs):
        ins, outs = refs[:na], refs[na:2 * na]
        send_sems, recv_sems = refs[2 * na:]
        x, y, c = _place()
        cps = [_remote(ins[a].at[pl.ds(c, 1)], outs[a].at[pl.ds(c, 1)], send_sems.at[a], recv_sems.at[a], (x, y, 1 - c))
               for a in range(na)]
        for cp in cps:
            cp.start()
        for a in range(na):
            cps[a].wait_send()
            _remote(ins[a].at[pl.ds(1 - c, 1)], outs[a].at[pl.ds(1 - c, 1)], send_sems.at[a], recv_sems.at[a], (x, y, 1 - c)).wait_recv()

    return pl.pallas_call(
        body, name="rs_share_halves", in_specs=[ANY] * na, out_specs=[ANY] * na,
        out_shape=[_sds(g.shape, g.dtype) for g in gfs], input_output_aliases={a: a for a in range(na)},
        scratch_shapes=[pltpu.SemaphoreType.DMA((na,)), pltpu.SemaphoreType.DMA((na,))],
    )(*gfs)


def _row_tile(rows, cols, budget=5 << 18):
    fits = [t for t in range(16, rows + 1, 16) if rows % t == 0 and t * cols * 4 <= budget]
    return max(fits) if fits else rows


def _chip_sums(gx, r1, where):
    _, _, r2, cd = r1.shape
    tr = _row_tile(r2, cd)

    def body(w_ref, a_ref, b_ref, o_ref):
        o_ref[0] = (a_ref[...].reshape(tr, cd) + b_ref[0, 0]).astype(BF16)

    other = lambda k, i, w: (w[1] + 1 + k) % N_CHIPS
    return pl.pallas_call(
        body, name="rs_chip_sums",
        grid_spec=pltpu.PrefetchScalarGridSpec(
            num_scalar_prefetch=1, grid=(N_CHIPS - 1, r2 // tr),
            in_specs=[_half_block(gx, tr, other), pl.BlockSpec((1, 1, tr, cd), lambda k, i, w: (other(k, i, w), 0, i, 0))],
            out_specs=pl.BlockSpec((1, tr, cd), lambda k, i, w: (k, i, 0))),
        out_shape=_sds((N_CHIPS - 1, r2, cd), BF16), compiler_params=_cp(("parallel", "parallel")),
    )(where, gx, r1)


def _owner_sum(gx, r1, r2x, where):
    _, _, r2, cd = r1.shape
    tr = _row_tile(r2, cd)

    def body(w_ref, a_ref, b_ref, r_ref, o_ref):
        acc = a_ref[...].reshape(tr, cd) + b_ref[0, 0]
        for k in range(N_CHIPS - 1):
            acc = acc + r_ref[k].astype(F32)
        o_ref[0] = acc

    return pl.pallas_call(
        body, name="rs_owner_sum",
        grid_spec=pltpu.PrefetchScalarGridSpec(
            num_scalar_prefetch=1, grid=(r2 // tr,),
            in_specs=[_half_block(gx, tr, lambda i, w: w[1]),
                      pl.BlockSpec((1, 1, tr, cd), lambda i, w: (w[1], 0, i, 0)),
                      pl.BlockSpec((N_CHIPS - 1, tr, cd), lambda i, w: (0, i, 0))],
            out_specs=pl.BlockSpec((1, tr, cd), lambda i, w: (w[0], i, 0))),
        out_shape=_sds((2, r2, cd)), compiler_params=_cp(("parallel",)),
    )(where, gx, r1, r2x)


def _adamw_math(w, g, m, v):
    m = ADAM_B1 * m + (1.0 - ADAM_B1) * g
    v = ADAM_B2 * v + (1.0 - ADAM_B2) * (g * g)
    m_hat = m / (1.0 - ADAM_B1 ** ADAM_STEP)
    v_hat = v / (1.0 - ADAM_B2 ** ADAM_STEP)
    delta = -ADAM_LR * (m_hat / (jnp.sqrt(v_hat) + ADAM_EPS) + ADAM_WD * w)
    return delta, m, v


def _adamw(w, g, m, v, name):
    rows, cd = w.shape
    if rows % 16 == 0:
        tr, tc = _row_tile(rows, cd, budget=3 << 19), cd
    else:
        tr, tc = rows, (128 if rows * cd * 4 > (3 << 19) else cd)
    assert rows % tr == 0 and cd % tc == 0

    def body(w_ref, g_ref, m_ref, v_ref, d_ref, mo_ref, vo_ref):
        d, mm, vv = _adamw_math(w_ref[...], g_ref[...], m_ref[...], v_ref[...])
        d_ref[...] = d
        mo_ref[...] = mm
        vo_ref[...] = vv

    blk = pl.BlockSpec((tr, tc), lambda i, j: (i, j))
    return pl.pallas_call(
        body, name=name, grid=(rows // tr, cd // tc), in_specs=[blk] * 4, out_specs=[blk] * 3, out_shape=[_sds(w.shape)] * 3,
        compiler_params=_cp(("parallel", "parallel")),
    )(w, g, m, v)


def _small_allreduce(gp):
    R = gp.shape[0]
    R2 = R // 2
    assert R2 % 8 == 0

    def body(g_ref, go_ref, sib, csum, land, send_sems, recv_sems):
        x, y, c = _place()
        me = 2 * x + y
        sibling = (x, y, 1 - c)
        chips = [(1 - x, y), (x, 1 - y), (1 - x, 1 - y)]
        swap = _remote(g_ref, sib, send_sems.at[0], recv_sems.at[0], sibling)
        swap.start()
        swap.wait()
        csum[...] = g_ref[...] + sib[...]
        half = csum.at[pl.ds(c * R2, R2)]
        land[me] = csum[pl.ds(c * R2, R2), :]
        cps = [_remote(half, land.at[me], send_sems.at[1 + j], recv_sems.at[1 + j], (px, py, c))
               for j, (px, py) in enumerate(chips)]
        for cp in cps:
            cp.start()
        for j, (px, py) in enumerate(chips):
            _remote(half, land.at[2 * px + py], send_sems.at[1 + j], recv_sems.at[1 + j], (px, py, c)).wait_recv()
        for cp in cps:
            cp.wait_send()
        mine = go_ref.at[pl.ds(c * R2, R2)]
        go_ref[pl.ds(c * R2, R2), :] = (land[0] + land[1]) + (land[2] + land[3])
        share = _remote(mine, mine, send_sems.at[4], recv_sems.at[4], sibling)
        share.start()
        share.wait_send()
        other = go_ref.at[pl.ds((1 - c) * R2, R2)]
        _remote(other, other, send_sems.at[4], recv_sems.at[4], sibling).wait_recv()

    vm = pl.BlockSpec(memory_space=pltpu.VMEM)
    return pl.pallas_call(
        body, name="small_allreduce", in_specs=[vm], out_specs=vm, out_shape=_sds((R, 128)),
        scratch_shapes=[pltpu.VMEM((R, 128), F32), pltpu.VMEM((R, 128), F32), pltpu.VMEM((N_CHIPS, R2, 128), F32),
                        pltpu.SemaphoreType.DMA((5,)), pltpu.SemaphoreType.DMA((5,))],
        compiler_params=_cp(),
    )(gp)


def _adamw_many(ws, gs, ms, vs):
    n = len(ws)

    def body(*refs):
        w_r, g_r, m_r, v_r = refs[:n], refs[n:2 * n], refs[2 * n:3 * n], refs[3 * n:4 * n]
        d_r, mo_r, vo_r = refs[4 * n:5 * n], refs[5 * n:6 * n], refs[6 * n:]
        for i in range(n):
            d_r[i][...], mo_r[i][...], vo_r[i][...] = _adamw_math(w_r[i][...], g_r[i][...], m_r[i][...], v_r[i][...])

    vm = pl.BlockSpec(memory_space=pltpu.VMEM)
    shapes = [_sds(a.shape) for a in ws]
    outs = pl.pallas_call(
        body, name="adamw_small", in_specs=[vm] * (4 * n), out_specs=[vm] * (3 * n), out_shape=shapes * 3, compiler_params=_cp(),
    )(*ws, *gs, *ms, *vs)
    return outs[:n], outs[n:2 * n], outs[2 * n:]


def _pack(arrs):
    rows = []
    for a in arrs:
        f = a.reshape(-1)
        f = jnp.pad(f, (0, (-f.shape[0]) % 128))
        rows.append(f.reshape(-1, 128))
    p = jnp.concatenate(rows, axis=0)
    return jnp.pad(p, ((0, (-p.shape[0]) % 8), (0, 0)))


def _unpack(p, shapes):
    out, r = [], 0
    for s in shapes:
        n = math.prod(s)
        nr = -(-n // 128)
        out.append(p[r:r + nr].reshape(-1)[:n].reshape(s))
        r += nr
    return out


class _ExchangeOps(_PlainOps):
    def __init__(self, rest_shards, where):
        self.rest_shards, self.where = rest_shards, where
        self.reduced = []

    def in_proj(self, h, wt_perm):
        proj, g_glu, g_su, g_du, g_out = _mm(h, wt_perm, tb=True, name="in_proj", tm=1024, tn=1152,
                                             exchange=_gather_exchange(self.rest_shards))
        cat = lambda g: jnp.concatenate([g[j] for j in range(N_CHIPS)], axis=1)
        return proj, (g_glu.reshape(D_S5, D_S5), cat(g_su), cat(g_du), g_out.reshape(D_MODEL, D_MODEL))

    def _chip_sums(self, gxs, name):
        r1s = _swap_halves(gxs, name)
        return r1s, [_chip_sums(gx, r1, self.where) for gx, r1 in zip(gxs, r1s)]

    def rest_grads(self, d_w_glu, d_w_su, d_w_du, d_w_out):
        gxs = [d_w_glu.reshape(N_CHIPS, 2, D_S5 // 8, D_S5), d_w_su.reshape(N_CHIPS, 2, D_S5 // 2, D_MODEL // N_CHIPS),
               d_w_du.reshape(N_CHIPS, 2, D_DN // 2, D_MODEL // N_CHIPS), d_w_out.reshape(N_CHIPS, 2, D_MODEL // 8, D_MODEL)]
        r1s, csbs = self._chip_sums(gxs, "rs_swap_rest")
        self.rest = (gxs, r1s, csbs)

    def d_w_in(self, h, dproj):
        gxs, r1s, csbs = self.rest
        d_wt_perm, *r2s = _mm(dproj, h, ta=True, name="d_w_in", tm=1152, tn=1024, exchange=_owners_exchange(csbs))
        self.reduced = list(zip(gxs, r1s, r2s))
        return d_wt_perm

    def d_h(self, dproj, wt_perm, d_wt_perm):
        self.beta_a = d_wt_perm[OFF_B:OFF_B + WT_NB]
        (r1,), (csb,) = self._chip_sums([d_wt_perm], "rs_swap_w_in")
        dh, r2 = _mm(dproj, wt_perm, name="d_h", tm=2048, tn=1024, tk=1152, exchange=_owners_exchange([csb]))
        self.reduced = [(d_wt_perm, r1, r2)] + self.reduced
        return dh


WT_SHARD = D_IN // N_CHIPS
WT_NB = 2 * DN_HEADS
WT_B, WT_LO = divmod(OFF_GS, WT_SHARD)
WT_FIRST = [i * WT_SHARD - (WT_NB if i > WT_B else 0) for i in range(N_CHIPS)]
WT_WIN = [16 * (r // 16) for r in WT_FIRST]
WT_SHIFT = [r - s for r, s in zip(WT_FIRST, WT_WIN)]
WT_ROWS = 2592
assert (WT_LO + WT_SHIFT[WT_B]) % 16 == 0 and max(WT_SHIFT) + WT_SHARD <= WT_ROWS and WT_WIN[-1] + WT_ROWS <= D_IN_PAD


def _wt_to_window(shard, chip):
    d = jnp.asarray(WT_SHIFT, jnp.int32)[chip]
    gap = jnp.where(chip == WT_B, 0, WT_NB)
    win = jnp.zeros((WT_ROWS, shard.shape[1]), shard.dtype)
    win = lax.dynamic_update_slice(win, shard[:WT_LO], (d, 0))
    win = lax.dynamic_update_slice(win, shard[WT_LO:WT_LO + WT_NB], (d + WT_LO, 0))
    win = lax.dynamic_update_slice(win, shard[WT_LO + WT_NB:], (d + WT_LO + gap, 0))
    return win, shard[WT_LO:WT_LO + WT_NB]


def _wt_from_window(win, beta_a, chip):
    d = jnp.asarray(WT_SHIFT, jnp.int32)[chip]
    gap = jnp.where(chip == WT_B, 0, WT_NB)
    cols = win.shape[1]
    head = lax.dynamic_slice(win, (d, 0), (WT_LO, cols))
    mid = jnp.where(chip == WT_B, beta_a, lax.dynamic_slice(win, (d + WT_LO, 0), (WT_NB, cols)))
    tail = lax.dynamic_slice(win, (d + WT_LO + gap, 0), (WT_SHARD - WT_LO - WT_NB, cols))
    return jnp.concatenate([head, mid, tail], axis=0)


def _wt_regroup(wins, beta_a):
    parts, at = [], 0
    for i in range(N_CHIPS):
        end = WT_WIN[i + 1] if i + 1 < N_CHIPS else OFF_B
        lo = at - WT_WIN[i]
        over = WT_WIN[i] + WT_ROWS - end if i + 1 < N_CHIPS else 0
        parts.append(wins[i, lo:end - WT_WIN[i]])
        if over:
            parts.append(wins[i, end - WT_WIN[i]:] + wins[i + 1, :over])
        at = end + over
    pad = jnp.zeros((D_IN_PAD - OFF_B - WT_NB, wins.shape[2]), wins.dtype)
    return jnp.concatenate(parts + [beta_a, pad], axis=0)


_SMALL = ("ln_w", "s5_lam_re", "s5_lam_im", "s5_log_step", "s5_b_re", "s5_b_im", "s5_c_re", "s5_c_im", "s5_d",
          "dn_a_log", "dn_dt_bias", "dn_norm_w", "final_norm_w")
_BIG = ("w_in", "s5_w_glu", "s5_w_up", "dn_w_up", "w_out")
_ORDER = ("ln_w", "w_in", "s5_lam_re", "s5_lam_im", "s5_log_step", "s5_b_re", "s5_b_im", "s5_c_re", "s5_c_im", "s5_d",
          "s5_w_glu", "s5_w_up", "dn_conv_w", "dn_a_log", "dn_dt_bias", "dn_norm_w", "dn_w_up", "w_out", "final_norm_w")


def kernel(x, ln_w, w_in, s5_lam_re, s5_lam_im, s5_log_step, s5_b_re, s5_b_im, s5_c_re, s5_c_im, s5_d, s5_w_glu, s5_w_up, dn_conv_w, dn_a_log, dn_dt_bias, dn_norm_w, dn_w_up, w_out, final_norm_w, loss_target, m_ln_w, m_w_in, m_s5_lam_re, m_s5_lam_im, m_s5_log_step, m_s5_b_re, m_s5_b_im, m_s5_c_re, m_s5_c_im, m_s5_d, m_s5_w_glu, m_s5_w_up, m_dn_conv_w, m_dn_a_log, m_dn_dt_bias, m_dn_norm_w, m_dn_w_up, m_w_out, m_final_norm_w, v_ln_w, v_w_in, v_s5_lam_re, v_s5_lam_im, v_s5_log_step, v_s5_b_re, v_s5_b_im, v_s5_c_re, v_s5_c_im, v_s5_d, v_s5_w_glu, v_s5_w_up, v_dn_conv_w, v_dn_a_log, v_dn_dt_bias, v_dn_norm_w, v_dn_w_up, v_w_out, v_final_norm_w):
    w = dict(ln_w=ln_w, w_in=w_in, s5_lam_re=s5_lam_re, s5_lam_im=s5_lam_im, s5_log_step=s5_log_step, s5_b_re=s5_b_re,
             s5_b_im=s5_b_im, s5_c_re=s5_c_re, s5_c_im=s5_c_im, s5_d=s5_d, s5_w_glu=s5_w_glu, s5_w_up=s5_w_up,
             dn_conv_w=dn_conv_w, dn_a_log=dn_a_log, dn_dt_bias=dn_dt_bias, dn_norm_w=dn_norm_w, dn_w_up=dn_w_up, w_out=w_out,
             final_norm_w=final_norm_w)
    m = dict(ln_w=m_ln_w, w_in=m_w_in, s5_lam_re=m_s5_lam_re, s5_lam_im=m_s5_lam_im, s5_log_step=m_s5_log_step,
             s5_b_re=m_s5_b_re, s5_b_im=m_s5_b_im, s5_c_re=m_s5_c_re, s5_c_im=m_s5_c_im, s5_d=m_s5_d, s5_w_glu=m_s5_w_glu,
             s5_w_up=m_s5_w_up, dn_conv_w=m_dn_conv_w, dn_a_log=m_dn_a_log, dn_dt_bias=m_dn_dt_bias, dn_norm_w=m_dn_norm_w,
             dn_w_up=m_dn_w_up, w_out=m_w_out, final_norm_w=m_final_norm_w)
    v = dict(ln_w=v_ln_w, w_in=v_w_in, s5_lam_re=v_s5_lam_re, s5_lam_im=v_s5_lam_im, s5_log_step=v_s5_log_step,
             s5_b_re=v_s5_b_re, s5_b_im=v_s5_b_im, s5_c_re=v_s5_c_re, s5_c_im=v_s5_c_im, s5_d=v_s5_d, s5_w_glu=v_s5_w_glu,
             s5_w_up=v_s5_w_up, dn_conv_w=v_dn_conv_w, dn_a_log=v_dn_a_log, dn_dt_bias=v_dn_dt_bias, dn_norm_w=v_dn_norm_w,
             dn_w_up=v_dn_w_up, w_out=v_w_out, final_norm_w=v_final_norm_w)
    xi, yi, ci = _place()
    chip = 2 * xi + yi
    where = jnp.stack([ci, chip, *[jnp.int32(s) for s in WT_WIN]]).astype(jnp.int32)

    tr = lambda a: jnp.swapaxes(a[0], 0, 1)
    win, beta_a = _wt_to_window(tr(w_in).astype(BF16), chip)
    g_win, g_ba, g_conv = _gather_relayed(win, [beta_a, dn_conv_w[0]], "gather_w_in")
    cat = lambda g: jnp.concatenate([g[j] for j in range(N_CHIPS)], axis=1)
    w_perm = _wt_regroup(g_win, g_ba[WT_B])

    ops = _ExchangeOps([w[n][0].astype(BF16) for n in _BIG[1:]], where)
    loss_row, grad_x, g = _local_step(
        x[0], loss_target[0], ln_w, w_perm, s5_lam_re[0], s5_lam_im[0], s5_log_step, s5_b_re[0], s5_b_im[0], s5_c_re[0],
        s5_c_im[0], s5_d, cat(g_conv), dn_a_log, dn_dt_bias, dn_norm_w, final_norm_w[None], ops)
    loss = lax.psum(loss_row[0, 0], ("x", "y", "c"))

    gfs = [_owner_sum(gx, r1, r2x, where) for gx, r1, r2x in ops.reduced]
    gfs = _share_halves(gfs)
    grads, deltas, new_m, new_v = {}, {}, {}, {}
    for n, gf in zip(_BIG[1:], gfs[1:]):
        shp = w[n].shape
        g2 = gf.reshape(shp[1:])
        d_, m_, v_ = _adamw(w[n][0], g2, m[n][0], v[n][0], "adamw_" + n)
        grads[n], deltas[n], new_m[n], new_v[n] = g2.reshape(shp), d_.reshape(shp), m_.reshape(shp), v_.reshape(shp)

    go = _small_allreduce(_pack([g[n] for n in _SMALL] + [g["dn_conv_w"], ops.beta_a]))
    lanes = {"s5_b_re": (S5_GROUPS, S5_STATE * S5_GROUP), "s5_b_im": (S5_GROUPS, S5_STATE * S5_GROUP)}
    flat = [lanes.get(n, (math.prod(w[n].shape[:-1]), w[n].shape[-1])) for n in _SMALL]
    *gs, g_conv, g_beta_a = _unpack(go, flat + [(CONV_K, 3 * D_DN), (WT_NB, D_MODEL)])
    gt = _wt_from_window(gfs[0].reshape(WT_ROWS, D_MODEL), g_beta_a, chip)
    d_, m_, v_ = _adamw(tr(w_in), gt, tr(m_w_in), tr(v_w_in), "adamw_w_in")
    grads["w_in"], deltas["w_in"], new_m["w_in"], new_v["w_in"] = (jnp.swapaxes(a, 0, 1)[None] for a in (gt, d_, m_, v_))
    as2d = lambda t: [t[n].reshape(s) for n, s in zip(_SMALL, flat)]
    for dst, src in zip((grads, deltas, new_m, new_v), (gs, *_adamw_many(as2d(w), gs, as2d(m), as2d(v)))):
        for n, a in zip(_SMALL, src):
            dst[n] = a.reshape(w[n].shape)
    cc = 3 * D_DN // N_CHIPS
    g_conv_mine = lax.dynamic_slice(g_conv, (0, chip * cc), (CONV_K, cc))
    d_, m_, v_ = _adamw(dn_conv_w[0], g_conv_mine, m_dn_conv_w[0], v_dn_conv_w[0], "adamw_dn_conv_w")
    grads["dn_conv_w"], deltas["dn_conv_w"], new_m["dn_conv_w"], new_v["dn_conv_w"] = (
        g_conv_mine[None], d_[None], m_[None], v_[None])

    return (loss, grad_x[None], *[grads[n] for n in _ORDER], *[deltas[n] for n in _ORDER], *[new_m[n] for n in _ORDER],
            *[new_v[n] for n in _ORDER])
```

```python
import functools
import math

import jax
import jax.numpy as jnp
from jax import lax
from jax.experimental import pallas as pl
from jax.experimental.pallas import tpu as pltpu

F32 = jnp.float32
BF16 = jnp.bfloat16
HI = lax.Precision.HIGHEST
MESH = pl.DeviceIdType.MESH
ANY = pl.BlockSpec(memory_space=pl.ANY)

EPS = 1e-6
D_MODEL = 2048
D_S5 = 1024
S5_GROUP = 16
S5_GROUPS = 64
S5_STATE = 64
S5_BLOCKS = 8
S5_SEG = 8
DN_HEADS = 8
DN_HEAD_DIM = 128
D_DN = 1024
CONV_K = 4
CHUNK = 64
D_IN = 10256
D_IN_PAD = 10368
OFF_US, OFF_ZS, OFF_Q, OFF_K, OFF_V, OFF_ZD, OFF_GS, OFF_GD, OFF_B = 0, 1024, 2048, 3072, 4096, 5120, 6144, 8192, 10240
N_CHIPS = 4
N_DEV = 8
VMEM_LIMIT = 56 * 1024 * 1024

ADAM_LR = 0.001
ADAM_B1 = 0.9
ADAM_B2 = 0.999
ADAM_EPS = 1e-08
ADAM_WD = 0.01
ADAM_STEP = 10


def _cp(sem=None):
    return pltpu.CompilerParams(dimension_semantics=sem, vmem_limit_bytes=VMEM_LIMIT)


def _sds(shape, dtype=F32):
    return jax.ShapeDtypeStruct(tuple(shape), dtype)


def _sigmoid(x):
    return 1.0 / (1.0 + jnp.exp(-x))


def _silu(x):
    return x * _sigmoid(x)


def _dsilu(x):
    s = _sigmoid(x)
    return s * (1.0 + x * (1.0 - s))


class _Exchange:
    def __init__(self, ins, out_shapes, n_sems, start, finish):
        self.ins, self.out_shapes, self.n_sems, self.start, self.finish = list(ins), list(out_shapes), n_sems, start, finish


def _mm(a, b, *, name, ta=False, tb=False, out_dtype=F32, tm=512, tn=512, tk=2048, shard_out=False, twin=False, exchange=None):
    if ta:
        K, M = a.shape
    else:
        M, K = a.shape
    if tb:
        N, K2 = b.shape
    else:
        K2, N = b.shape
    assert K == K2, (a.shape, b.shape)
    tm, tn, tk = min(tm, M), min(tn, N), min(tk, K)
    assert M % tm == 0 and N % tn == 0 and K % tk == 0, (M, N, K, tm, tn, tk)
    nk = K // tk
    dims = (((0 if ta else 1,), (1 if tb else 0,)), ((), ()))

    gm, gn = M // tm, N // tn
    n_in = len(exchange.ins) if exchange else 0
    n_out = len(exchange.out_shapes) if exchange else 0

    n_o = 2 if twin else 1

    def body(*refs):
        a_ref, b_ref, xin, o_refs = refs[0], refs[1], refs[2:2 + n_in], refs[2 + n_in:2 + n_in + n_o]
        xout, rest = refs[2 + n_in + n_o:2 + n_in + n_o + n_out], refs[2 + n_in + n_o + n_out:]
        i, j, k = pl.program_id(0), pl.program_id(1), pl.program_id(2)

        def write(val):
            o_refs[0][...] = val.astype(out_dtype).reshape(o_refs[0].shape)
            if twin:
                o_refs[1][...] = val.astype(BF16).reshape(o_refs[1].shape)

        if exchange:
            sems = rest[-3:]

            @pl.when(jnp.logical_and(jnp.logical_and(i == 0, j == 0), k == 0))
            def _():
                exchange.start(xin, xout, *sems)

        p = lax.dot_general(a_ref[...].astype(BF16), b_ref[...].astype(BF16), dims, preferred_element_type=F32)
        if nk == 1:
            write(p)
        else:
            acc_ref = rest[0]

            @pl.when(k == 0)
            def _():
                acc_ref[...] = p

            @pl.when(k > 0)
            def _():
                acc_ref[...] += p

            @pl.when(k == nk - 1)
            def _():
                write(acc_ref[...])

        if exchange:
            @pl.when(jnp.logical_and(jnp.logical_and(i == gm - 1, j == gn - 1), k == nk - 1))
            def _():
                exchange.finish(xin, xout, *sems)

    a_spec = pl.BlockSpec((tk, tm), lambda i, j, k: (k, i)) if ta else pl.BlockSpec((tm, tk), lambda i, j, k: (i, k))
    b_spec = pl.BlockSpec((tn, tk), lambda i, j, k: (j, k)) if tb else pl.BlockSpec((tk, tn), lambda i, j, k: (k, j))
    if shard_out:
        o_spec = pl.BlockSpec((1, tm, tn), lambda i, j, k: (j, i, 0))
        o_shape = _sds((N // tn, M, tn), out_dtype)
    else:
        o_spec = pl.BlockSpec((tm, tn), lambda i, j, k: (i, j))
        o_shape = _sds((M, N), out_dtype)
    scratch = [pltpu.VMEM((tm, tn), F32)] if nk > 1 else []
    o_specs, o_shapes = [o_spec] * n_o, [o_shape, _sds(o_shape.shape, BF16)][:n_o]
    if not exchange:
        out = pl.pallas_call(
            body, name=name, grid=(gm, gn, nk), in_specs=[a_spec, b_spec], out_specs=o_specs, out_shape=o_shapes,
            scratch_shapes=scratch, compiler_params=_cp(("parallel", "parallel", "arbitrary")),
        )(a, b)
        return out if twin else out[0]
    scratch += [pltpu.SemaphoreType.DMA((exchange.n_sems,)) for _ in range(3)]
    return pl.pallas_call(
        body, name=name, grid=(gm, gn, nk), in_specs=[a_spec, b_spec] + [ANY] * n_in, out_specs=o_specs + [ANY] * n_out,
        out_shape=o_shapes + exchange.out_shapes, scratch_shapes=scratch,
        compiler_params=_cp(("arbitrary", "arbitrary", "arbitrary")),
    )(a, b, *exchange.ins)


def _ln_fwd(x, w):
    L, D = x.shape
    tm = min(256, L)

    def body(x_ref, w_ref, h_ref, r_ref):
        xv = x_ref[...]
        r = lax.rsqrt(jnp.mean(xv * xv, axis=-1, keepdims=True) + EPS)
        h_ref[...] = (xv * r * w_ref[...]).astype(BF16)
        r_ref[...] = r

    return pl.pallas_call(
        body, name="ln_fwd", grid=(L // tm,),
        in_specs=[pl.BlockSpec((tm, D), lambda i: (i, 0)), pl.BlockSpec((1, D), lambda i: (0, 0))],
        out_specs=[pl.BlockSpec((tm, D), lambda i: (i, 0)), pl.BlockSpec((tm, 1), lambda i: (i, 0))],
        out_shape=[_sds((L, D), BF16), _sds((L, 1))], compiler_params=_cp(("parallel",)),
    )(x, w)


def _ln_bwd(x, r, w, dh, dx2):
    L, D = x.shape
    tm = min(256, L)

    def body(x_ref, r_ref, w_ref, dh_ref, dx2_ref, dx_ref, dw_ref):
        i = pl.program_id(0)
        xv, rv, dhv = x_ref[...], r_ref[...], dh_ref[...]
        t = dhv * w_ref[...]
        m = jnp.mean(t * xv, axis=-1, keepdims=True)
        dx_ref[...] = dx2_ref[...] + rv * t - xv * (rv * rv * rv) * m
        part = jnp.sum(dhv * xv * rv, axis=0, keepdims=True)

        @pl.when(i == 0)
        def _():
            dw_ref[...] = part

        @pl.when(i > 0)
        def _():
            dw_ref[...] += part

    row = pl.BlockSpec((tm, D), lambda i: (i, 0))
    return pl.pallas_call(
        body, name="ln_bwd", grid=(L // tm,),
        in_specs=[row, pl.BlockSpec((tm, 1), lambda i: (i, 0)), pl.BlockSpec((1, D), lambda i: (0, 0)), row, row],
        out_specs=[row, pl.BlockSpec((1, D), lambda i: (0, 0))],
        out_shape=[_sds((L, D)), _sds((1, D))], compiler_params=_cp(("arbitrary",)),
    )(x, r, w, dh, dx2)


def _s5_param_math(lam_re, lam_im, log_step, b_re, b_im, expand):
    step = jnp.exp(log_step)
    mag = jnp.exp(lam_re * step)
    abar_re = mag * jnp.cos(lam_im * step)
    abar_im = mag * jnp.sin(lam_im * step)
    den = lam_re * lam_re + lam_im * lam_im
    xr = abar_re - 1.0
    f_re = (xr * lam_re + abar_im * lam_im) / den
    f_im = (abar_im * lam_re - xr * lam_im) / den
    fe_re = jnp.dot(f_re, expand, precision=HI, preferred_element_type=F32)
    fe_im = jnp.dot(f_im, expand, precision=HI, preferred_element_type=F32)
    bb_re = fe_re * b_re - fe_im * b_im
    bb_im = fe_re * b_im + fe_im * b_re
    return abar_re, abar_im, bb_re, bb_im


def _s5_expand():
    p = lax.broadcasted_iota(jnp.int32, (S5_STATE, S5_STATE * S5_GROUP), 0)
    q = lax.broadcasted_iota(jnp.int32, (S5_STATE, S5_STATE * S5_GROUP), 1)
    return (q // S5_GROUP == p).astype(F32)


def _s5_param_fwd(lam_re, lam_im, log_step, b_re, b_im):
    G, P = lam_re.shape

    def body(lr, li, ls, br, bi, ar_o, ai_o, bbr_o, bbi_o):
        outs = _s5_param_math(lr[...], li[...], ls[...], br[...], bi[...], _s5_expand())
        for o, v in zip((ar_o, ai_o, bbr_o, bbi_o), outs):
            o[...] = v

    return pl.pallas_call(
        body, name="s5_param_fwd",
        out_shape=[_sds((G, P)), _sds((G, P)), _sds(b_re.shape), _sds(b_re.shape)], compiler_params=_cp(),
    )(lam_re, lam_im, log_step, b_re, b_im)


def _s5_param_bwd(lam_re, lam_im, log_step, b_re, b_im, dar, dai, dbbr, dbbi):
    G, P = lam_re.shape

    def body(lr, li, ls, br, bi, g0, g1, g2, g3, dlr, dli, dls, dbr, dbi):
        ex = _s5_expand()
        _, f = jax.vjp(lambda a, b, c, d, e: _s5_param_math(a, b, c, d, e, ex), lr[...], li[...], ls[...], br[...], bi[...])
        grads = f((g0[...], g1[...], g2[...], g3[...]))
        for o, v in zip((dlr, dli, dls, dbr, dbi), grads):
            o[...] = v

    return pl.pallas_call(
        body, name="s5_param_bwd",
        out_shape=[_sds((G, P)), _sds((G, P)), _sds((G, 1)), _sds(b_re.shape), _sds(b_re.shape)], compiler_params=_cp(),
    )(lam_re, lam_im, log_step, b_re, b_im, dar, dai, dbbr, dbbi)


def _to_segs(src_ref, dst_ref, L):
    S = L // S5_SEG

    def body(j, carry):
        dst_ref[pl.ds(pl.multiple_of(S5_SEG * j, S5_SEG), S5_SEG), :] = src_ref[pl.ds(j, S5_SEG, stride=S), :]
        return carry

    lax.fori_loop(0, S, body, 0, unroll=8)


def _from_segs(src_ref, L, write):
    S = L // S5_SEG
    for seg in range(S5_SEG):
        def body(jb, carry, seg=seg):
            j0 = 16 * jb
            write(pl.multiple_of(seg * S + j0, 16), src_ref[pl.ds(S5_SEG * j0 + seg, 16, stride=S5_SEG), :])
            return carry

        lax.fori_loop(0, S // 16, body, 0, unroll=4)


def _scan_segs(ar, ai, re_ref, im_ref, end_r_ref, end_i_ref, c_r_ref, c_i_ref, L, tile0, reverse):
    S = L // S5_SEG
    NB, LN = re_ref.shape[0], 128
    assert S & (S - 1) == 0
    tile = lambda j: pl.ds(pl.multiple_of(S5_SEG * (tile0 + j), S5_SEG), S5_SEG)
    ar8 = [jnp.broadcast_to(ar[:, b * LN:(b + 1) * LN], (S5_SEG, LN)) for b in range(NB)]
    ai8 = [jnp.broadcast_to(ai[:, b * LN:(b + 1) * LN], (S5_SEG, LN)) for b in range(NB)]

    def step(idx, carry):
        rows = tile(S - 1 - idx if reverse else idx)
        out = []
        for b in range(NB):
            sr, si = carry[b]
            nr = ar8[b] * sr - ai8[b] * si + re_ref[b, rows, :]
            ni = ar8[b] * si + ai8[b] * sr + im_ref[b, rows, :]
            re_ref[b, rows, :] = nr
            im_ref[b, rows, :] = ni
            out.append((nr, ni))
        return tuple(out)

    z8 = jnp.zeros((S5_SEG, LN), F32)
    fin = lax.fori_loop(0, S, step, tuple((z8, z8) for _ in range(NB)), unroll=4)
    order = range(S5_SEG - 2, -1, -1) if reverse else range(1, S5_SEG)
    for b in range(NB):
        end_r_ref[b], end_i_ref[b] = fin[b]
        pr, pi = ar8[b][:1], ai8[b][:1]
        for _ in range(int(math.log2(S))):
            pr, pi = pr * pr - pi * pi, 2.0 * pr * pi
        first = S5_SEG - 1 if reverse else 0
        c_r_ref[b, pl.ds(first, 1), :] = jnp.zeros((1, LN), F32)
        c_i_ref[b, pl.ds(first, 1), :] = jnp.zeros((1, LN), F32)
        cr, ci = end_r_ref[b, pl.ds(first, 1), :], end_i_ref[b, pl.ds(first, 1), :]
        for i in order:
            c_r_ref[b, pl.ds(i, 1), :] = cr
            c_i_ref[b, pl.ds(i, 1), :] = ci
            er, ei = end_r_ref[b, pl.ds(i, 1), :], end_i_ref[b, pl.ds(i, 1), :]
            cr, ci = er + pr * cr - pi * ci, ei + pr * ci + pi * cr

    entering = [(c_r_ref[b], c_i_ref[b]) for b in range(NB)]

    def fix(idx, carry):
        rows = tile(S - 1 - idx if reverse else idx)
        out = []
        for b in range(NB):
            pr, pi = carry[b]
            cr, ci = entering[b]
            re_ref[b, rows, :] += pr * cr - pi * ci
            im_ref[b, rows, :] += pr * ci + pi * cr
            out.append((pr * ar8[b] - pi * ai8[b], pr * ai8[b] + pi * ar8[b]))
        return tuple(out)

    lax.fori_loop(0, S, fix, tuple((ar8[b], ai8[b]) for b in range(NB)), unroll=4)


def _s5_seg_scratch(L, cs, pad):
    NB = cs // 128
    small = [pltpu.VMEM((NB, S5_SEG, 128), F32) for _ in range(4)]
    return [pltpu.VMEM((NB, L + pad, 128), F32), pltpu.VMEM((NB, L + pad, 128), F32)] + small


def _s5_core_fwd(proj, wbr, wbi, a_re, a_im, cbr, cbi):
    L = proj.shape[0]
    nb, ci, cs = wbr.shape
    NB = cs // 128

    def body(u_ref, wbr_ref, wbi_ref, ar_ref, ai_ref, cbr_ref, cbi_ref, y_ref, sr, si, er, ei, cr, cim, up, yp):
        _to_segs(u_ref, up, L)
        u = up[...].astype(BF16)
        for b in range(NB):
            lanes = pl.ds(b * 128, 128)
            sr[b] = jnp.dot(u, wbr_ref[0, :, lanes], preferred_element_type=F32)
            si[b] = jnp.dot(u, wbi_ref[0, :, lanes], preferred_element_type=F32)
        _scan_segs(ar_ref[...], ai_ref[...], sr, si, er, ei, cr, cim, L, 0, False)
        y = jnp.zeros((L, ci), F32)
        for b in range(NB):
            lanes = pl.ds(b * 128, 128)
            y = y + (jnp.dot(sr[b].astype(BF16), cbr_ref[0, lanes, :], preferred_element_type=F32)
                     - jnp.dot(si[b].astype(BF16), cbi_ref[0, lanes, :], preferred_element_type=F32))
        yp[...] = y

        def write(row, val):
            y_ref[pl.ds(row, 16), :] = val

        _from_segs(yp, L, write)

    wspec = pl.BlockSpec((1, ci, cs), lambda j: (j, 0, 0))
    aspec = pl.BlockSpec((1, cs), lambda j: (0, j))
    cspec = pl.BlockSpec((1, cs, ci), lambda j: (j, 0, 0))
    return pl.pallas_call(
        body, name="s5_core_fwd", grid=(nb,),
        in_specs=[pl.BlockSpec((L, ci), lambda j: (0, OFF_US // ci + j)), wspec, wspec, aspec, aspec, cspec, cspec],
        out_specs=pl.BlockSpec((L, ci), lambda j: (0, j)), out_shape=_sds((L, nb * ci)),
        scratch_shapes=_s5_seg_scratch(L, cs, 0) + [pltpu.VMEM((L, ci), F32), pltpu.VMEM((L, ci), F32)],
        compiler_params=_cp(("arbitrary",)),
    )(proj, wbr, wbi, a_re, a_im, cbr, cbi)


def _s5_core_bwd(proj, wbr, wbi, a_re, a_im, cbr, cbi, dyc, du1):
    L = proj.shape[0]
    nb, ci, cs = wbr.shape
    NB = cs // 128
    S = L // S5_SEG
    PAD = S5_SEG

    def body(u_ref, wbr_ref, wbi_ref, ar_ref, ai_ref, cbr_ref, cbi_ref, dy_ref, du1_ref,
             du_ref, dwbr_ref, dwbi_ref, dcbr_ref, dcbi_ref, dar_ref, dai_ref,
             sr, si, er, ei, cr, cim, lr, li, up, dyp, dup):
        tn = (((0,), (0,)), ((), ()))
        nt = (((1,), (1,)), ((), ()))
        _to_segs(u_ref, up, L)
        _to_segs(dy_ref, dyp, L)
        _to_segs(du1_ref, dup, L)
        u = up[...].astype(BF16)
        dy = dyp[...].astype(BF16)
        ar, ai = ar_ref[...], ai_ref[...]
        for b in range(NB):
            lanes = pl.ds(b * 128, 128)
            sr[b, pl.ds(PAD, L), :] = jnp.dot(u, wbr_ref[0, :, lanes], preferred_element_type=F32)
            si[b, pl.ds(PAD, L), :] = jnp.dot(u, wbi_ref[0, :, lanes], preferred_element_type=F32)
        _scan_segs(ar, ai, sr, si, er, ei, cr, cim, L, 1, False)
        for b in range(NB):
            lanes = pl.ds(b * 128, 128)
            sr[b, pl.ds(0, PAD), :] = cr[b]
            si[b, pl.ds(0, PAD), :] = cim[b]
            lr[b] = lax.dot_general(dy, cbr_ref[0, lanes, :], nt, preferred_element_type=F32)
            li[b] = -lax.dot_general(dy, cbi_ref[0, lanes, :], nt, preferred_element_type=F32)
            dcbr_ref[0, lanes, :] = lax.dot_general(sr[b, pl.ds(PAD, L), :].astype(BF16), dy, tn, preferred_element_type=F32)
            dcbi_ref[0, lanes, :] = -lax.dot_general(si[b, pl.ds(PAD, L), :].astype(BF16), dy, tn, preferred_element_type=F32)
        _scan_segs(ar, -ai, lr, li, er, ei, cr, cim, L, 0, True)

        def da_step(j, carry):
            rows = pl.ds(pl.multiple_of(S5_SEG * j, S5_SEG), S5_SEG)
            out = []
            for b in range(NB):
                dar, dai = carry[b]
                pr_, pi_ = sr[b, rows, :], si[b, rows, :]
                gr, gi = lr[b, rows, :], li[b, rows, :]
                out.append((dar + (gr * pr_ + gi * pi_), dai + (gi * pr_ - gr * pi_)))
            return tuple(out)

        z8 = jnp.zeros((S5_SEG, 128), F32)
        acc = lax.fori_loop(0, S, da_step, tuple((z8, z8) for _ in range(NB)), unroll=4)
        du = dup[...]
        for b in range(NB):
            lanes = pl.ds(b * 128, 128)
            dar_ref[:, lanes] = jnp.sum(acc[b][0], axis=0, keepdims=True)
            dai_ref[:, lanes] = jnp.sum(acc[b][1], axis=0, keepdims=True)
            gr, gi = lr[b].astype(BF16), li[b].astype(BF16)
            du = du + (lax.dot_general(gr, wbr_ref[0, :, lanes], nt, preferred_element_type=F32)
                       + lax.dot_general(gi, wbi_ref[0, :, lanes], nt, preferred_element_type=F32))
            dwbr_ref[0, :, lanes] = lax.dot_general(u, gr, tn, preferred_element_type=F32)
            dwbi_ref[0, :, lanes] = lax.dot_general(u, gi, tn, preferred_element_type=F32)
        dup[...] = du

        def write(row, val):
            du_ref[pl.ds(row, 16), :] = val.astype(BF16)

        _from_segs(dup, L, write)

    wspec = pl.BlockSpec((1, ci, cs), lambda j: (j, 0, 0))
    aspec = pl.BlockSpec((1, cs), lambda j: (0, j))
    cspec = pl.BlockSpec((1, cs, ci), lambda j: (j, 0, 0))
    col = pl.BlockSpec((L, ci), lambda j: (0, j))
    return pl.pallas_call(
        body, name="s5_core_bwd", grid=(nb,),
        in_specs=[pl.BlockSpec((L, ci), lambda j: (0, OFF_US // ci + j)), wspec, wspec, aspec, aspec, cspec, cspec, col, col],
        out_specs=[col, wspec, wspec, cspec, cspec, aspec, aspec],
        out_shape=[_sds((L, nb * ci), BF16), _sds(wbr.shape), _sds(wbr.shape), _sds(cbr.shape), _sds(cbr.shape),
                   _sds((1, nb * cs)), _sds((1, nb * cs))],
        scratch_shapes=(_s5_seg_scratch(L, cs, PAD) + [pltpu.VMEM((NB, L, 128), F32), pltpu.VMEM((NB, L, 128), F32)]
                        + [pltpu.VMEM((L, ci), F32) for _ in range(3)]),
        compiler_params=_cp(("arbitrary",)),
    )(proj, wbr, wbi, a_re, a_im, cbr, cbi, dyc, du1)


def _s5_post_math(yc, u, z, d, wg):
    y = yc + d * u
    y1 = jax.nn.gelu(y)
    t = jnp.dot(y1.astype(BF16), wg, preferred_element_type=F32)
    sg = _sigmoid(t)
    return y, y1, sg


def _s5_post_fwd(yc, proj, d, wg):
    L, W = yc.shape
    tm = min(256, L)

    def body(yc_ref, u_ref, z_ref, d_ref, wg_ref, o_ref):
        _, y1, sg = _s5_post_math(yc_ref[...], u_ref[...], z_ref[...], d_ref[...], wg_ref[...])
        o_ref[...] = (y1 * sg * _silu(z_ref[...])).astype(BF16)

    row = pl.BlockSpec((tm, W), lambda i: (i, 0))
    return pl.pallas_call(
        body, name="s5_post_fwd", grid=(L // tm,),
        in_specs=[row, pl.BlockSpec((tm, W), lambda i: (i, OFF_US // W)), pl.BlockSpec((tm, W), lambda i: (i, OFF_ZS // W)),
                  pl.BlockSpec((1, W), lambda i: (0, 0)), pl.BlockSpec((W, W), lambda i: (0, 0))],
        out_specs=row, out_shape=_sds((L, W), BF16), compiler_params=_cp(("parallel",)),
    )(yc, proj, proj, d, wg)


def _s5_post_bwd(yc, proj, d, wg, dout):
    L, W = yc.shape
    tm = min(256, L)

    def body(yc_ref, u_ref, z_ref, d_ref, wg_ref, do_ref, dyc_ref, du_ref, dz_ref, dd_ref, dwg_ref):
        i = pl.program_id(0)
        u, z, d_, wgv = u_ref[...], z_ref[...], d_ref[...], wg_ref[...]
        y, y1, sg = _s5_post_math(yc_ref[...], u, z, d_, wgv)
        dout_ = do_ref[...]
        y2 = y1 * sg
        dy2 = dout_ * _silu(z)
        dz_ref[...] = (dout_ * y2 * _dsilu(z)).astype(BF16)
        dt = (dy2 * y1 * sg * (1.0 - sg)).astype(BF16)
        dy1 = dy2 * sg + lax.dot_general(dt, wgv, (((1,), (1,)), ((), ())), preferred_element_type=F32)
        _, gelu_vjp = jax.vjp(jax.nn.gelu, y)
        dy = gelu_vjp(dy1)[0]
        dyc_ref[...] = dy
        du_ref[...] = dy * d_
        dd_part = jnp.sum(dy * u, axis=0, keepdims=True)
        dwg_part = lax.dot_general(y1.astype(BF16), dt, (((0,), (0,)), ((), ())), preferred_element_type=F32)

        @pl.when(i == 0)
        def _():
            dd_ref[...] = dd_part
            dwg_ref[...] = dwg_part

        @pl.when(i > 0)
        def _():
            dd_ref[...] += dd_part
            dwg_ref[...] += dwg_part

    row = pl.BlockSpec((tm, W), lambda i: (i, 0))
    return pl.pallas_call(
        body, name="s5_post_bwd", grid=(L // tm,),
        in_specs=[row, pl.BlockSpec((tm, W), lambda i: (i, OFF_US // W)), pl.BlockSpec((tm, W), lambda i: (i, OFF_ZS // W)),
                  pl.BlockSpec((1, W), lambda i: (0, 0)), pl.BlockSpec((W, W), lambda i: (0, 0)), row],
        out_specs=[row, row, row, pl.BlockSpec((1, W), lambda i: (0, 0)), pl.BlockSpec((W, W), lambda i: (0, 0))],
        out_shape=[_sds((L, W)), _sds((L, W)), _sds((L, W), BF16), _sds((1, W)), _sds((W, W))],
        compiler_params=_cp(("arbitrary",)),
    )(yc, proj, proj, d, wg, dout)


def _shift_down(x, s):
    if s == 0:
        return x
    rows = lax.broadcasted_iota(jnp.int32, x.shape, 0)
    return jnp.where(rows >= s, pltpu.roll(x, s, 0), 0.0)


def _shift_up(x, s):
    if s == 0:
        return x
    L = x.shape[0]
    rows = lax.broadcasted_iota(jnp.int32, x.shape, 0)
    return jnp.where(rows < L - s, pltpu.roll(x, L - s, 0), 0.0)


def _conv_pre(x, w):
    acc = w[CONV_K - 1:CONV_K, :] * x
    for s in range(1, CONV_K):
        acc = acc + w[CONV_K - 1 - s:CONV_K - s, :] * _shift_down(x, s)
    return acc


def _dn_conv_fwd(proj, conv_w):
    L = proj.shape[0]
    W = DN_HEAD_DIM
    nq = 2 * DN_HEADS

    def body(x_ref, w_ref, o_ref):
        j = pl.program_id(0)
        act = _silu(_conv_pre(x_ref[...], w_ref[...]))
        r = lax.rsqrt(jnp.sum(act * act, axis=-1, keepdims=True) + EPS)
        scale = jnp.where(j < DN_HEADS, DN_HEAD_DIM ** -0.5, 1.0)
        o_ref[...] = jnp.where(j < nq, act * r * scale, act)

    return pl.pallas_call(
        body, name="dn_conv_fwd", grid=(3 * DN_HEADS,),
        in_specs=[pl.BlockSpec((L, W), lambda j: (0, OFF_Q // W + j)), pl.BlockSpec((CONV_K, W), lambda j: (0, j))],
        out_specs=pl.BlockSpec((L, W), lambda j: (0, j)), out_shape=_sds((L, 3 * D_DN)), compiler_params=_cp(("parallel",)),
    )(proj, conv_w)


def _dn_conv_bwd(proj, conv_w, dout):
    L = proj.shape[0]
    W = DN_HEAD_DIM
    nq = 2 * DN_HEADS

    def body(x_ref, w_ref, do_ref, dx_ref, dw_ref):
        j = pl.program_id(0)
        x, w, dout_ = x_ref[...], w_ref[...], do_ref[...]
        pre = _conv_pre(x, w)
        act = _silu(pre)
        r = lax.rsqrt(jnp.sum(act * act, axis=-1, keepdims=True) + EPS)
        scale = jnp.where(j < DN_HEADS, DN_HEAD_DIM ** -0.5, 1.0)
        g = dout_ * scale
        dact_n = r * g - act * (r * r * r) * jnp.sum(g * act, axis=-1, keepdims=True)
        dact = jnp.where(j < nq, dact_n, dout_)
        dpre = dact * _dsilu(pre)
        dx = w[CONV_K - 1:CONV_K, :] * dpre
        for s in range(1, CONV_K):
            dx = dx + w[CONV_K - 1 - s:CONV_K - s, :] * _shift_up(dpre, s)
        dx_ref[...] = dx.astype(BF16)
        for s in range(CONV_K):
            dw_ref[pl.ds(CONV_K - 1 - s, 1), :] = jnp.sum(dpre * _shift_down(x, s), axis=0, keepdims=True)

    col = pl.BlockSpec((L, W), lambda j: (0, j))
    wsp = pl.BlockSpec((CONV_K, W), lambda j: (0, j))
    return pl.pallas_call(
        body, name="dn_conv_bwd", grid=(3 * DN_HEADS,),
        in_specs=[pl.BlockSpec((L, W), lambda j: (0, OFF_Q // W + j)), wsp, col], out_specs=[col, wsp],
        out_shape=[_sds((L, 3 * D_DN), BF16), _sds((CONV_K, 3 * D_DN))], compiler_params=_cp(("parallel",)),
    )(proj, conv_w, dout)


def _softplus(x):
    return jnp.maximum(x, 0.0) + jnp.log(1.0 + jnp.exp(-jnp.abs(x)))


def _dn_gates_fwd(proj, alog, dtb):
    L = proj.shape[0]
    W = 128

    def body(p_ref, al_ref, db_ref, o_ref):
        p = p_ref[...]
        lane = lax.broadcasted_iota(jnp.int32, p.shape, 1)
        g = -jnp.exp(al_ref[...]) * _softplus(p + db_ref[...])
        o_ref[...] = jnp.where(lane < DN_HEADS, _sigmoid(p), jnp.where(lane < 2 * DN_HEADS, g, 0.0))

    return pl.pallas_call(
        body, name="dn_gates_fwd", grid=(1,),
        in_specs=[pl.BlockSpec((L, W), lambda i: (0, OFF_B // W)), pl.BlockSpec((1, W), lambda i: (0, 0)),
                  pl.BlockSpec((1, W), lambda i: (0, 0))],
        out_specs=pl.BlockSpec((L, W), lambda i: (0, 0)), out_shape=_sds((L, W)), compiler_params=_cp(("arbitrary",)),
    )(proj, alog, dtb)


def _dn_gates_bwd(proj, alog, dtb, dgates):
    L = proj.shape[0]
    W = 128

    def body(p_ref, al_ref, db_ref, dg_ref, dp_ref, dal_ref, ddb_ref):
        p, dg = p_ref[...], dg_ref[...]
        lane = lax.broadcasted_iota(jnp.int32, p.shape, 1)
        is_g = jnp.logical_and(lane >= DN_HEADS, lane < 2 * DN_HEADS)
        beta = _sigmoid(p)
        na = -jnp.exp(al_ref[...])
        xs = p + db_ref[...]
        dsp = dg * na * _sigmoid(xs)
        dp_ref[...] = jnp.where(lane < DN_HEADS, dg * beta * (1.0 - beta), jnp.where(is_g, dsp, 0.0)).astype(BF16)
        dal_ref[...] = jnp.sum(jnp.where(is_g, dg * na * _softplus(xs), 0.0), axis=0, keepdims=True)
        ddb_ref[...] = jnp.sum(jnp.where(is_g, dsp, 0.0), axis=0, keepdims=True)

    one = pl.BlockSpec((1, W), lambda i: (0, 0))
    full = pl.BlockSpec((L, W), lambda i: (0, 0))
    return pl.pallas_call(
        body, name="dn_gates_bwd", grid=(1,),
        in_specs=[pl.BlockSpec((L, W), lambda i: (0, OFF_B // W)), one, one, full], out_specs=[full, one, one],
        out_shape=[_sds((L, W), BF16), _sds((1, W)), _sds((1, W))], compiler_params=_cp(("arbitrary",)),
    )(proj, alog, dtb, dgates)


def _bdot(a, b, dims):
    return lax.dot_general(a.astype(BF16), b.astype(BF16), (dims, ((), ())), preferred_element_type=F32)


_NN, _NT, _TN = ((1,), (0,)), ((1,), (1,)), ((0,), (0,))


def _dot3(a, b, dims):
    ah, bh = a.astype(BF16), b.astype(BF16)
    al, bl = (a - ah.astype(F32)).astype(BF16), (b - bh.astype(F32)).astype(BF16)
    (ca,), (cb,) = dims
    a3 = jnp.concatenate([ah, ah, al], axis=ca)
    b3 = jnp.concatenate([bh, bl, bh], axis=cb)
    return lax.dot_general(a3, b3, (dims, ((), ())), preferred_element_type=F32)


def _mm_family(raw):
    nn = jax.custom_vjp(lambda a, b: raw(a, b, _NN))
    nt = jax.custom_vjp(lambda a, b: raw(a, b, _NT))
    tn = jax.custom_vjp(lambda a, b: raw(a, b, _TN))
    nn.defvjp(lambda a, b: (raw(a, b, _NN), (a, b)), lambda r, g: (raw(g, r[1], _NT), raw(r[0], g, _TN)))
    nt.defvjp(lambda a, b: (raw(a, b, _NT), (a, b)), lambda r, g: (raw(g, r[1], _NN), raw(g, r[0], _TN)))
    tn.defvjp(lambda a, b: (raw(a, b, _TN), (a, b)), lambda r, g: (raw(r[1], g, _NT), raw(r[0], g, _NN)))
    return nn, nt, tn


_mm_nn, _mm_nt, _mm_tn = _mm_family(_bdot)
_m3_nn, _m3_nt, _m3_tn = _mm_family(_dot3)


def _tri_apply(x, upper):
    C = x.shape[0]
    ii = lax.broadcasted_iota(jnp.int32, (C, 3 * C), 0)
    jj = lax.broadcasted_iota(jnp.int32, (C, 3 * C), 1) % C
    mat = ((ii <= jj) if upper else (ii >= jj)).astype(BF16)
    hi = x.astype(BF16)
    r = x - hi.astype(F32)
    mid = r.astype(BF16)
    lo = (r - mid.astype(F32)).astype(BF16)
    return jnp.dot(mat, jnp.concatenate([hi, mid, lo], axis=0), preferred_element_type=F32)


_cumsum_rows = jax.custom_vjp(lambda x: _tri_apply(x, False))
_cumsum_rows.defvjp(lambda x: (_tri_apply(x, False), None), lambda _, g: (_tri_apply(g, True),))


def _uli(a_s):
    C = a_s[0].shape[0]
    ii = lax.broadcasted_iota(jnp.int32, (C, C), 0)
    jj = lax.broadcasted_iota(jnp.int32, (C, C), 1)
    eye = jnp.where(ii == jj, 1.0, 0.0)
    ts = [eye - a for a in a_s]
    ms = list(a_s)
    for _ in range(int(math.log2(C)) - 1):
        ms = [_dot3(m, m, _NN) for m in ms]
        ts = [t + _dot3(t, m, _NN) for t, m in zip(ts, ms)]
    return tuple(ts)


def _uli_bwd(ts, gs):
    xs = [_dot3(t, g, _TN) for t, g in zip(ts, gs)]
    return (tuple(-_dot3(x, t, _NT) for x, t in zip(xs, ts)),)


_unit_lower_inverse = jax.custom_vjp(_uli)
_unit_lower_inverse.defvjp(lambda a_s: (lambda ts: (ts, ts))(_uli(a_s)), _uli_bwd)


def _prep_math(qs, ks, vs, gcols, bcols):
    n = len(qs)
    C, dv = vs[0].shape
    ii = lax.broadcasted_iota(jnp.int32, (C, C), 0)
    jj = lax.broadcasted_iota(jnp.int32, (C, C), 1)
    causal = ii >= jj
    strict = ii > jj
    sf = strict.astype(F32)
    ones = jnp.ones((C, dv), F32)
    dms = [_cumsum_rows(g * sf) for g in gcols]
    gcbs = [_cumsum_rows(g * ones) for g in gcols]
    kks = [_mm_nt(k, k) for k in ks]
    qks = [_mm_nt(q, k) for q, k in zip(qs, ks)]
    decays = [jnp.where(causal, jnp.exp(jnp.where(causal, dm, 0.0)), 0.0) for dm in dms]
    glasts = [jnp.sum(g * ones, axis=0, keepdims=True) for g in gcols]
    egs = [jnp.exp(gcb) for gcb in gcbs]
    ts = _unit_lower_inverse(tuple(jnp.where(strict, b * kk * dc, 0.0) for b, kk, dc in zip(bcols, kks, decays)))
    us = [_m3_nn(t, v * b) for t, v, b in zip(ts, vs, bcols)]
    ws = [_m3_nn(t, k * b * eg) for t, k, b, eg in zip(ts, ks, bcols, egs)]
    return tuple((us[i], ws[i], qs[i] * egs[i], ks[i] * jnp.exp(glasts[i] - gcbs[i]), qks[i] * decays[i],
                  jnp.exp(glasts[i])) for i in range(n))


def _gate_cols(gates, h):
    lane = lax.broadcasted_iota(jnp.int32, gates.shape, 1)
    bcol = jnp.sum(jnp.where(lane == h, gates, 0.0), axis=1, keepdims=True)
    gcol = jnp.sum(jnp.where(lane == h + DN_HEADS, gates, 0.0), axis=1, keepdims=True)
    return gcol, bcol


DN_HB = 8


def _dn_prep_fwd(qkv, gates):
    L = qkv.shape[0]
    N, H, d, HB = L // CHUNK, DN_HEADS, DN_HEAD_DIM, DN_HB

    def body(q_ref, k_ref, v_ref, g_ref, u_ref, w_ref, qd_ref, kd_ref, qk_ref, egl_ref):
        h0 = pl.program_id(1) * HB
        gates_ = g_ref[...]
        lanes_of = [pl.ds(i * d, d) for i in range(HB)]
        cols = [_gate_cols(gates_, h0 + i) for i in range(HB)]
        outs = _prep_math([q_ref[:, l] for l in lanes_of], [k_ref[:, l] for l in lanes_of], [v_ref[:, l] for l in lanes_of],
                          [c[0] for c in cols], [c[1] for c in cols])
        for i in range(HB):
            lanes = lanes_of[i]
            u, w, qd, kd, qk, egl = outs[i]
            u_ref[:, lanes] = u
            w_ref[:, lanes] = w
            qd_ref[:, lanes] = qd
            kd_ref[:, lanes] = kd
            qk_ref[0, i] = qk
            egl_ref[0, i] = jnp.broadcast_to(egl, (8, d))

    blk = lambda off: pl.BlockSpec((CHUNK, HB * d), lambda n, j: (n, off // HB + j))
    cc = pl.BlockSpec((1, HB, CHUNK, CHUNK), lambda n, j: (n, j, 0, 0))
    ee = pl.BlockSpec((1, HB, 8, d), lambda n, j: (n, j, 0, 0))
    big = _sds((L, D_DN))
    return pl.pallas_call(
        body, name="dn_prep_fwd", grid=(N, H // HB),
        in_specs=[blk(0), blk(H), blk(2 * H), pl.BlockSpec((CHUNK, 128), lambda n, j: (n, 0))],
        out_specs=[blk(0), blk(0), blk(0), blk(0), cc, ee],
        out_shape=[big, big, big, big, _sds((N, H, CHUNK, CHUNK)), _sds((N, H, 8, d))],
        compiler_params=_cp(("parallel", "parallel")),
    )(qkv, qkv, qkv, gates)


def _dn_scan_fwd(u, w, qd, kd, qk, egl):
    L = u.shape[0]
    N, H, d, HB = L // CHUNK, DN_HEADS, DN_HEAD_DIM, DN_HB

    def body(u_ref, w_ref, qd_ref, kd_ref, qk_ref, egl_ref, o_ref, st_ref, s_ref):
        n, h0 = pl.program_id(0), pl.program_id(1) * HB

        @pl.when(n == 0)
        def _():
            for i in range(HB):
                s_ref[h0 + i] = jnp.zeros((d, d), F32)

        hs = range(HB)
        ln = [pl.ds(i * d, d) for i in hs]
        st = [s_ref[h0 + i] for i in hs]
        ws = [_bdot(w_ref[:, ln[i]], st[i], _NN) for i in hs]
        qs = [_bdot(qd_ref[:, ln[i]], st[i], _NN) for i in hs]
        vn = [u_ref[:, ln[i]] - ws[i] for i in hs]
        qv = [_bdot(qk_ref[0, i], vn[i], _NN) for i in hs]
        kv = [_bdot(kd_ref[:, ln[i]], vn[i], _TN) for i in hs]
        for i in hs:
            st_ref[0, i] = st[i]
            o_ref[:, ln[i]] = qs[i] + qv[i]
            s_ref[h0 + i] = st[i] * egl_ref[0, i, pl.ds(0, 1), :] + kv[i]

    blk = pl.BlockSpec((CHUNK, HB * d), lambda n, j: (n, j))
    cc = pl.BlockSpec((1, HB, CHUNK, CHUNK), lambda n, j: (n, j, 0, 0))
    ee = pl.BlockSpec((1, HB, 8, d), lambda n, j: (n, j, 0, 0))
    return pl.pallas_call(
        body, name="dn_scan_fwd", grid=(N, H // HB), in_specs=[blk, blk, blk, blk, cc, ee],
        out_specs=[blk, pl.BlockSpec((1, HB, d, d), lambda n, j: (n, j, 0, 0))],
        out_shape=[_sds((L, D_DN)), _sds((N, H, d, d))], scratch_shapes=[pltpu.VMEM((H, d, d), F32)],
        compiler_params=_cp(("arbitrary", "arbitrary")),
    )(u, w, qd, kd, qk, egl)


def _dn_scan_bwd(u, w, qd, kd, qk, egl, states, do):
    L = u.shape[0]
    N, H, d, HB = L // CHUNK, DN_HEADS, DN_HEAD_DIM, DN_HB

    def body(u_ref, w_ref, qd_ref, kd_ref, qk_ref, egl_ref, st_ref, do_ref,
             du_ref, dw_ref, dqd_ref, dkd_ref, dqk_ref, degl_ref, ds_ref):
        n, h0 = pl.program_id(0), pl.program_id(1) * HB

        @pl.when(n == 0)
        def _():
            for i in range(HB):
                ds_ref[h0 + i] = jnp.zeros((d, d), F32)

        hs = range(HB)
        ln = [pl.ds(i * d, d) for i in hs]
        st = [st_ref[0, i] for i in hs]
        dsn = [ds_ref[h0 + i] for i in hs]
        do_ = [do_ref[:, ln[i]] for i in hs]
        ws = [_bdot(w_ref[:, ln[i]], st[i], _NN) for i in hs]
        d1 = [_bdot(qk_ref[0, i], do_[i], _TN) for i in hs]
        d2 = [_bdot(kd_ref[:, ln[i]], dsn[i], _NN) for i in hs]
        dqd = [_bdot(do_[i], st[i], _NT) for i in hs]
        qdo = [_bdot(qd_ref[:, ln[i]], do_[i], _TN) for i in hs]
        vn = [u_ref[:, ln[i]] - ws[i] for i in hs]
        dvn = [d1[i] + d2[i] for i in hs]
        dw = [_bdot(dvn[i], st[i], _NT) for i in hs]
        dkd = [_bdot(vn[i], dsn[i], _NT) for i in hs]
        dqk = [_bdot(do_[i], vn[i], _NT) for i in hs]
        wdv = [_bdot(w_ref[:, ln[i]], dvn[i], _TN) for i in hs]
        for i in hs:
            du_ref[:, ln[i]] = dvn[i]
            dw_ref[:, ln[i]] = -dw[i]
            dqd_ref[:, ln[i]] = dqd[i]
            dkd_ref[:, ln[i]] = dkd[i]
            dqk_ref[0, i] = dqk[i]
            degl_ref[0, i] = jnp.broadcast_to(jnp.sum(dsn[i] * st[i], keepdims=True), (8, d))
            ds_ref[h0 + i] = (qdo[i] - wdv[i]) + dsn[i] * egl_ref[0, i, pl.ds(0, 1), :]

    blk = pl.BlockSpec((CHUNK, HB * d), lambda n, j: (N - 1 - n, j))
    cc = pl.BlockSpec((1, HB, CHUNK, CHUNK), lambda n, j: (N - 1 - n, j, 0, 0))
    ee = pl.BlockSpec((1, HB, 8, d), lambda n, j: (N - 1 - n, j, 0, 0))
    ss = pl.BlockSpec((1, HB, d, d), lambda n, j: (N - 1 - n, j, 0, 0))
    big = _sds((L, D_DN))
    return pl.pallas_call(
        body, name="dn_scan_bwd", grid=(N, H // HB), in_specs=[blk, blk, blk, blk, cc, ee, ss, blk],
        out_specs=[blk, blk, blk, blk, cc, ee],
        out_shape=[big, big, big, big, _sds((N, H, CHUNK, CHUNK)), _sds((N, H, 8, d))],
        scratch_shapes=[pltpu.VMEM((H, d, d), F32)], compiler_params=_cp(("arbitrary", "arbitrary")),
    )(u, w, qd, kd, qk, egl, states, do)


def _dn_prep_bwd(qkv, gates, du, dw, dqd, dkd, dqk, degl):
    L = qkv.shape[0]
    N, H, d, HB = L // CHUNK, DN_HEADS, DN_HEAD_DIM, DN_HB
    assert HB == H

    def body(q_ref, k_ref, v_ref, g_ref, du_ref, dw_ref, dqd_ref, dkd_ref, dqk_ref, degl_ref, dqkv_ref, dg_ref):
        j = pl.program_id(1)
        h0 = j * HB
        gates_ = g_ref[...]
        lane = lax.broadcasted_iota(jnp.int32, gates_.shape, 1)
        lane1 = lax.broadcasted_iota(jnp.int32, (1, d), 1)
        part = jnp.zeros(gates_.shape, F32)
        lanes_of = [pl.ds(i * d, d) for i in range(HB)]
        cols = [_gate_cols(gates_, h0 + i) for i in range(HB)]
        _, f = jax.vjp(_prep_math, [q_ref[:, l] for l in lanes_of], [k_ref[:, l] for l in lanes_of],
                       [v_ref[:, l] for l in lanes_of], [c[0] for c in cols], [c[1] for c in cols])
        cots = tuple((du_ref[:, l], dw_ref[:, l], dqd_ref[:, l], dkd_ref[:, l], dqk_ref[0, i],
                      jnp.where(lane1 == 0, degl_ref[0, i, pl.ds(0, 1), :], 0.0)) for i, l in enumerate(lanes_of))
        dqs, dks, dvs, dgcs, dbcs = f(cots)
        for i in range(HB):
            for s, val in enumerate((dqs[i], dks[i], dvs[i])):
                dqkv_ref[:, pl.ds((s * H + i) * d, d)] = val
            part = part + jnp.where(lane == h0 + i, dbcs[i], 0.0) + jnp.where(lane == h0 + i + DN_HEADS, dgcs[i], 0.0)

        @pl.when(j == 0)
        def _():
            dg_ref[...] = part

        @pl.when(j > 0)
        def _():
            dg_ref[...] += part

    blk = lambda off: pl.BlockSpec((CHUNK, HB * d), lambda n, j: (n, off // HB + j))
    gsp = pl.BlockSpec((CHUNK, 128), lambda n, j: (n, 0))
    cc = pl.BlockSpec((1, HB, CHUNK, CHUNK), lambda n, j: (n, j, 0, 0))
    ee = pl.BlockSpec((1, HB, 8, d), lambda n, j: (n, j, 0, 0))
    return pl.pallas_call(
        body, name="dn_prep_bwd", grid=(N, H // HB),
        in_specs=[blk(0), blk(H), blk(2 * H), gsp, blk(0), blk(0), blk(0), blk(0), cc, ee],
        out_specs=[pl.BlockSpec((CHUNK, 3 * H * d), lambda n, j: (n, 0)), gsp], out_shape=[_sds((L, 3 * D_DN)), _sds((L, 128))],
        compiler_params=_cp(("parallel", "arbitrary")),
    )(qkv, qkv, qkv, gates, du, dw, dqd, dkd, dqk, degl)


def _dn_post_fwd(o, proj, nw):
    L = o.shape[0]
    d = DN_HEAD_DIM
    tm = min(512, L)

    def body(o_ref, z_ref, w_ref, y_ref):
        ov = o_ref[...]
        r = lax.rsqrt(jnp.mean(ov * ov, axis=-1, keepdims=True) + EPS)
        y_ref[...] = (ov * r * w_ref[...] * _silu(z_ref[...])).astype(BF16)

    blk = pl.BlockSpec((tm, d), lambda i, h: (i, h))
    return pl.pallas_call(
        body, name="dn_post_fwd", grid=(L // tm, DN_HEADS),
        in_specs=[blk, pl.BlockSpec((tm, d), lambda i, h: (i, OFF_ZD // d + h)), pl.BlockSpec((1, d), lambda i, h: (0, 0))],
        out_specs=blk, out_shape=_sds((L, D_DN), BF16), compiler_params=_cp(("parallel", "parallel")),
    )(o, proj, nw)


def _dn_post_bwd(o, proj, nw, dy):
    L = o.shape[0]
    d = DN_HEAD_DIM
    tm = min(512, L)

    def body(o_ref, z_ref, w_ref, dy_ref, do_ref, dz_ref, dw_ref):
        first = jnp.logical_and(pl.program_id(0) == 0, pl.program_id(1) == 0)
        ov, z, w, dyv = o_ref[...], z_ref[...], w_ref[...], dy_ref[...]
        r = lax.rsqrt(jnp.mean(ov * ov, axis=-1, keepdims=True) + EPS)
        xn = ov * r
        dz_ref[...] = (dyv * xn * w * _dsilu(z)).astype(BF16)
        dn = dyv * _silu(z)
        t = dn * w
        do_ref[...] = r * t - ov * (r * r * r) * jnp.mean(t * ov, axis=-1, keepdims=True)
        part = jnp.sum(dn * xn, axis=0, keepdims=True)

        @pl.when(first)
        def _():
            dw_ref[...] = part

        @pl.when(jnp.logical_not(first))
        def _():
            dw_ref[...] += part

    blk = pl.BlockSpec((tm, d), lambda i, h: (i, h))
    one = pl.BlockSpec((1, d), lambda i, h: (0, 0))
    return pl.pallas_call(
        body, name="dn_post_bwd", grid=(L // tm, DN_HEADS),
        in_specs=[blk, pl.BlockSpec((tm, d), lambda i, h: (i, OFF_ZD // d + h)), one, blk], out_specs=[blk, blk, one],
        out_shape=[_sds((L, D_DN)), _sds((L, D_DN), BF16), _sds((1, d))], compiler_params=_cp(("arbitrary", "arbitrary")),
    )(o, proj, nw, dy)


def _mix_fwd(s5o, dno, w_su, w_du, proj):
    L, K = s5o.shape
    N = w_su.shape[1]
    tm, tn = min(512, L), 512

    def body(a1, a2, b1, b2, gs, gd, ys_ref, yd_ref, mx_ref):
        ys = jnp.dot(a1[...], b1[...], preferred_element_type=F32)
        yd = jnp.dot(a2[...], b2[...], preferred_element_type=F32)
        ys_ref[...] = ys
        yd_ref[...] = yd
        mx_ref[...] = (_sigmoid(gs[...]) * ys + _sigmoid(gd[...]) * yd).astype(BF16)

    a = pl.BlockSpec((tm, K), lambda i, j: (i, 0))
    b = pl.BlockSpec((K, tn), lambda i, j: (0, j))
    o = pl.BlockSpec((tm, tn), lambda i, j: (i, j))
    return pl.pallas_call(
        body, name="mix_fwd", grid=(L // tm, N // tn),
        in_specs=[a, a, b, b, pl.BlockSpec((tm, tn), lambda i, j: (i, OFF_GS // tn + j)),
                  pl.BlockSpec((tm, tn), lambda i, j: (i, OFF_GD // tn + j))],
        out_specs=[o, o, o], out_shape=[_sds((L, N)), _sds((L, N)), _sds((L, N), BF16)],
        compiler_params=_cp(("parallel", "parallel")),
    )(s5o, dno, w_su, w_du, proj, proj)


def _mix_bwd(dx2b, w_out, proj, ys, yd):
    L, K = dx2b.shape
    N = w_out.shape[0]
    tm, tn = min(512, L), 512

    def body(a, b, gs, gd, ys_ref, yd_ref, dgs_ref, dgd_ref, dys_ref, dyd_ref):
        dm = lax.dot_general(a[...], b[...], (((1,), (1,)), ((), ())), preferred_element_type=F32)
        ss, sd = _sigmoid(gs[...]), _sigmoid(gd[...])
        dys_ref[...] = (dm * ss).astype(BF16)
        dyd_ref[...] = (dm * sd).astype(BF16)
        dgs_ref[...] = (dm * ys_ref[...] * ss * (1.0 - ss)).astype(BF16)
        dgd_ref[...] = (dm * yd_ref[...] * sd * (1.0 - sd)).astype(BF16)

    o = pl.BlockSpec((tm, tn), lambda i, j: (i, j))
    return pl.pallas_call(
        body, name="mix_bwd", grid=(L // tm, N // tn),
        in_specs=[pl.BlockSpec((tm, K), lambda i, j: (i, 0)), pl.BlockSpec((tn, K), lambda i, j: (j, 0)),
                  pl.BlockSpec((tm, tn), lambda i, j: (i, OFF_GS // tn + j)),
                  pl.BlockSpec((tm, tn), lambda i, j: (i, OFF_GD // tn + j)), o, o],
        out_specs=[o, o, o, o], out_shape=[_sds((L, N), BF16)] * 4, compiler_params=_cp(("parallel", "parallel")),
    )(dx2b, w_out, proj, proj, ys, yd)


def _final(mixed, w_out, x, tgt, fw):
    L, D = x.shape
    tm = min(256, L)

    def body(a_ref, b_ref, x_ref, t_ref, w_ref, dx_ref, dxb_ref, loss_ref, dw_ref):
        i = pl.program_id(0)
        x2 = x_ref[...] + jnp.dot(a_ref[...], b_ref[...], preferred_element_type=F32)
        w = w_ref[...]
        r = lax.rsqrt(jnp.mean(x2 * x2, axis=-1, keepdims=True) + EPS)
        xn = x2 * r
        e = xn * w - t_ref[...]
        lpart = 0.5 * jnp.sum(jnp.mean(e * e, axis=-1, keepdims=True), axis=0, keepdims=True)
        dy = e * (1.0 / D)
        t = dy * w
        dx2 = r * t - x2 * (r * r * r) * jnp.mean(t * x2, axis=-1, keepdims=True)
        dx_ref[...] = dx2
        dxb_ref[...] = dx2.astype(BF16)
        dwp = jnp.sum(dy * xn, axis=0, keepdims=True)
        lrow = jnp.broadcast_to(lpart, loss_ref.shape)

        @pl.when(i == 0)
        def _():
            loss_ref[...] = lrow
            dw_ref[...] = dwp

        @pl.when(i > 0)
        def _():
            loss_ref[...] += lrow
            dw_ref[...] += dwp

    row = pl.BlockSpec((tm, D), lambda i: (i, 0))
    one = pl.BlockSpec((1, D), lambda i: (0, 0))
    return pl.pallas_call(
        body, name="final", grid=(L // tm,),
        in_specs=[row, pl.BlockSpec((D, D), lambda i: (0, 0)), row, row, one],
        out_specs=[row, row, pl.BlockSpec((1, 128), lambda i: (0, 0)), one],
        out_shape=[_sds((L, D)), _sds((L, D), BF16), _sds((1, 128)), _sds((1, D))], compiler_params=_cp(("arbitrary",)),
    )(mixed, w_out, x, tgt, fw)


def _block_diag(t):
    J, g, a, b = t.shape
    eye = jnp.eye(g, dtype=t.dtype)
    return (t[:, :, :, None, :] * eye[None, :, None, :, None]).reshape(J, g * a, g * b)


def _block_diag_take(m, g):
    J, ga, gb = m.shape
    a, b = ga // g, gb // g
    m5 = m.reshape(J, g, a, g, b)
    idx = jnp.arange(g)
    return m5[:, idx, :, idx, :].transpose(1, 0, 2, 3)


class _PlainOps:
    def __init__(self, w_rest):
        self.w_rest = w_rest

    def in_proj(self, h, wt_perm):
        return _mm(h, wt_perm, tb=True, name="in_proj", tm=1024, tn=1152), self.w_rest

    def rest_grads(self, grads, twins):
        pass

    def d_w_in(self, h, dproj):
        return _mm(dproj, h, ta=True, name="d_w_in", tm=1152, tn=1024, twin=True)

    def d_h(self, dproj, wt_perm, d_wt_perm, d_wt_twin):
        return _mm(dproj, wt_perm, name="d_h", tm=2048, tn=1024, tk=1152)


def _local_step(x, tgt, ln_w, w_perm, lam_re, lam_im, log_step, b_re, b_im, c_re, c_im, s5_d,
                conv_w, a_log, dt_bias, norm_w, fw, ops):
    G, P, gb = S5_GROUPS, S5_STATE, S5_GROUPS // S5_BLOCKS
    h, rstd = _ln_fwd(x, ln_w)
    proj, (w_glu, w_su, w_du, w_out) = ops.in_proj(h, w_perm)

    b_re2, b_im2 = b_re.reshape(G, P * S5_GROUP), b_im.reshape(G, P * S5_GROUP)
    ls2 = log_step.reshape(G, 1)
    abar_re, abar_im, bb_re, bb_im = _s5_param_fwd(lam_re, lam_im, ls2, b_re2, b_im2)

    def to_wb(bb):
        return _block_diag(bb.reshape(S5_BLOCKS, gb, P, S5_GROUP).transpose(0, 1, 3, 2)).astype(BF16)

    def to_cb(cc):
        return _block_diag(cc.reshape(S5_BLOCKS, gb, S5_GROUP, P).transpose(0, 1, 3, 2)).astype(BF16)

    wbr, wbi, cbr, cbi = to_wb(bb_re), to_wb(bb_im), to_cb(c_re), to_cb(c_im)
    a_re_row, a_im_row = abar_re.reshape(1, G * P), abar_im.reshape(1, G * P)
    yc = _s5_core_fwd(proj, wbr, wbi, a_re_row, a_im_row, cbr, cbi)
    s5o = _s5_post_fwd(yc, proj, s5_d, w_glu)

    pad = lambda v: jnp.pad(v, ((0, 0), (DN_HEADS, 128 - 2 * DN_HEADS)))
    alog_row, dtb_row = pad(a_log), pad(dt_bias)
    qkv = _dn_conv_fwd(proj, conv_w)
    gates = _dn_gates_fwd(proj, alog_row, dtb_row)
    prep = _dn_prep_fwd(qkv, gates)
    o_dn, states = _dn_scan_fwd(*prep)
    dno = _dn_post_fwd(o_dn, proj, norm_w)

    ys, yd, mixed = _mix_fwd(s5o, dno, w_su, w_du, proj)
    dx2, dx2b, loss_row, d_fw = _final(mixed, w_out, x, tgt, fw)
    d_w_out, d_w_out_b = _mm(mixed, dx2b, ta=True, name="d_w_out", twin=True)
    dgs, dgd, dys, dyd = _mix_bwd(dx2b, w_out, proj, ys, yd)
    d_w_su, d_w_su_b = _mm(s5o, dys, ta=True, name="d_w_su", shard_out=True, twin=True)
    d_w_du, d_w_du_b = _mm(dno, dyd, ta=True, name="d_w_du", shard_out=True, twin=True)
    ds5o = _mm(dys, w_su, tb=True, name="d_s5o")
    ddno = _mm(dyd, w_du, tb=True, name="d_dno")

    dyc, du1, dz_s, d_s5d, d_w_glu = _s5_post_bwd(yc, proj, s5_d, w_glu, ds5o)
    ops.rest_grads((d_w_glu, d_w_su, d_w_du, d_w_out), (d_w_glu.astype(BF16), d_w_su_b, d_w_du_b, d_w_out_b))
    du, dwbr, dwbi, dcbr, dcbi, dar, dai = _s5_core_bwd(proj, wbr, wbi, a_re_row, a_im_row, cbr, cbi, dyc, du1)

    def from_wb(dwb):
        return _block_diag_take(dwb, gb).transpose(0, 1, 3, 2).reshape(G, P * S5_GROUP)

    def from_cb(dcb):
        return _block_diag_take(dcb, gb).transpose(0, 1, 3, 2).reshape(G, S5_GROUP, P)

    d_lam_re, d_lam_im, d_ls, d_b_re, d_b_im = _s5_param_bwd(
        lam_re, lam_im, ls2, b_re2, b_im2, dar.reshape(G, P), dai.reshape(G, P), from_wb(dwbr), from_wb(dwbi))

    do_dn, dz_d, d_norm_w = _dn_post_bwd(o_dn, proj, norm_w, ddno)
    dqkv_act, dgates = _dn_prep_bwd(qkv, gates, *_dn_scan_bwd(*prep, states, do_dn))
    dqkv, d_conv = _dn_conv_bwd(proj, conv_w, dqkv_act)
    dpb, d_alog_row, d_dtb_row = _dn_gates_bwd(proj, alog_row, dtb_row, dgates)

    dproj = jnp.concatenate([du, dz_s, dqkv, dz_d, dgs, dgd, dpb], axis=1)
    d_w_perm, d_w_twin = ops.d_w_in(h, dproj)
    dh = ops.d_h(dproj, w_perm, d_w_perm, d_w_twin)
    grad_x, d_ln_w = _ln_bwd(x, rstd, ln_w, dh, dx2)

    grads = dict(
        ln_w=d_ln_w, w_perm=d_w_perm, s5_lam_re=d_lam_re, s5_lam_im=d_lam_im, s5_log_step=d_ls.reshape(1, G),
        s5_b_re=d_b_re.reshape(G, P, S5_GROUP), s5_b_im=d_b_im.reshape(G, P, S5_GROUP),
        s5_c_re=from_cb(dcbr), s5_c_im=from_cb(dcbi), s5_d=d_s5d, s5_w_glu=d_w_glu, s5_w_up=d_w_su,
        dn_conv_w=d_conv, dn_a_log=d_alog_row[:, DN_HEADS:2 * DN_HEADS], dn_dt_bias=d_dtb_row[:, DN_HEADS:2 * DN_HEADS],
        dn_norm_w=d_norm_w, dn_w_up=d_w_du, w_out=d_w_out, final_norm_w=d_fw)
    return loss_row, grad_x, grads


def _place():
    x, y, c = lax.axis_index("x"), lax.axis_index("y"), lax.axis_index("c")
    return x, y, c


def _remote(src, dst, send_sem, recv_sem, to):
    return pltpu.make_async_remote_copy(src_ref=src, dst_ref=dst, send_sem=send_sem, recv_sem=recv_sem,
                                        device_id=to, device_id_type=MESH)


def _gather_exchange(shards, whole=()):
    na, nw = len(shards), len(whole)

    def half_of(ref, a, half):
        rows = shards[a].shape[0]
        return ref.at[pl.ds(half * (rows // 2), rows // 2)]

    def plan(ins, outs, send_sems, recv_sems, local_sems, receiving):
        x, y, c = _place()
        me = 2 * x + y
        sibling = (x, y, 1 - c)
        chips = [(1 - x, y), (x, 1 - y), (1 - x, 1 - y)]

        def part(a, chip, half):
            return half_of(outs[a].at[chip], a, half)

        own = [pltpu.make_async_copy(ins[a], outs[a].at[me], local_sems.at[a]) for a in range(na + nw)]
        sends, landed, passed, arrivals = [], [], [], []
        for a in range(na):
            for j, (px, py) in enumerate(chips):
                k = 6 * a + j
                sends.append(_remote(half_of(ins[a], a, c), part(a, me, c), send_sems.at[k], recv_sems.at[k], (px, py, c)))
                if receiving:
                    got, other = part(a, 2 * px + py, c), part(a, 2 * px + py, 1 - c)
                    landed.append(_remote(got, got, send_sems.at[k], recv_sems.at[k], (px, py, c)))
                    passed.append(_remote(got, got, send_sems.at[k + 3], recv_sems.at[k + 3], sibling))
                    arrivals.append(_remote(other, other, send_sems.at[k + 3], recv_sems.at[k + 3], sibling))
        for a in range(na, na + nw):
            for j, (px, py) in enumerate(chips):
                k = 6 * na + 3 * (a - na) + j
                sends.append(_remote(ins[a], outs[a].at[me], send_sems.at[k], recv_sems.at[k], (px, py, c)))
                if receiving:
                    arrivals.append(_remote(ins[a], outs[a].at[2 * px + py], send_sems.at[k], recv_sems.at[k], (px, py, c)))
        return own, sends, landed, passed, arrivals

    def start(ins, outs, *sems):
        own, sends, _, _, _ = plan(ins, outs, *sems, False)
        for cp in own + sends:
            cp.start()

    def finish(ins, outs, *sems):
        own, sends, landed, passed, arrivals = plan(ins, outs, *sems, True)
        for got, fwd in zip(landed, passed):
            got.wait_recv()
            fwd.start()
        for cp in arrivals:
            cp.wait_recv()
        for cp in sends + passed:
            cp.wait_send()
        for cp in own:
            cp.wait()

    arrays = list(shards) + list(whole)
    return _Exchange(arrays, [_sds((N_CHIPS,) + s.shape, s.dtype) for s in arrays], 6 * na + 3 * nw, start, finish)


def _gather_relayed(shard, whole, name):
    rows, cols = shard.shape
    nw = len(whole)

    def body(*refs):
        in_ref, w_in = refs[0], refs[1:1 + nw]
        out_ref, w_out = refs[1 + nw], refs[2 + nw:2 + 2 * nw]
        send_sems, recv_sems, local_sems = refs[2 + 2 * nw:]
        x, y, c = _place()
        me = 2 * x + y
        near = (jnp.where(c == 1, 1 - x, x), jnp.where(c == 1, y, 1 - y))
        far = (jnp.where(c == 1, x, 1 - x), jnp.where(c == 1, 1 - y, y))
        diag = (1 - x, 1 - y)
        sibling = (x, y, 1 - c)
        chip_of = lambda p: 2 * p[0] + p[1]

        def half(ref, h):
            return ref.at[pl.ds(0, rows), pl.ds(h * (cols // 2), cols // 2)]

        own = [pltpu.make_async_copy(in_ref, out_ref.at[me], local_sems.at[0])]
        own += [pltpu.make_async_copy(w_in[a], w_out[a].at[me], local_sems.at[1 + a]) for a in range(nw)]
        others = [(1 - x, y), (x, 1 - y), (1 - x, 1 - y)]
        small = [_remote(w_in[a], w_out[a].at[me], send_sems.at[4 + 3 * a + j], recv_sems.at[4 + 3 * a + j], (*p, c))
                 for a in range(nw) for j, p in enumerate(others)]
        sends = [_remote(in_ref, out_ref.at[me], send_sems.at[0], recv_sems.at[0], (*near, c))]
        for cp in own + small + sends:
            cp.start()
        from_near = out_ref.at[chip_of(near)]
        _remote(from_near, from_near, send_sems.at[0], recv_sems.at[0], (*near, c)).wait_recv()
        sends.append(_remote(from_near, from_near, send_sems.at[1], recv_sems.at[1], sibling))
        sends[-1].start()
        from_far = out_ref.at[chip_of(far)]
        _remote(from_far, from_far, send_sems.at[1], recv_sems.at[1], sibling).wait_recv()
        sends.append(_remote(half(from_far, c), half(from_far, c), send_sems.at[2], recv_sems.at[2], (*near, c)))
        sends[-1].start()
        of_diag = out_ref.at[chip_of(diag)]
        _remote(half(of_diag, c), half(of_diag, c), send_sems.at[2], recv_sems.at[2], (*near, c)).wait_recv()
        sends.append(_remote(half(of_diag, c), half(of_diag, c), send_sems.at[3], recv_sems.at[3], sibling))
        sends[-1].start()
        _remote(half(of_diag, 1 - c), half(of_diag, 1 - c), send_sems.at[3], recv_sems.at[3], sibling).wait_recv()
        for a in range(nw):
            for j, p in enumerate(others):
                _remote(w_in[a], w_out[a].at[chip_of(p)], send_sems.at[4 + 3 * a + j], recv_sems.at[4 + 3 * a + j], (*p, c)).wait_recv()
        for cp in sends + small:
            cp.wait_send()
        for cp in own:
            cp.wait()

    arrays = [shard] + list(whole)
    n_sems = 4 + 3 * nw
    return pl.pallas_call(
        body, name=name, in_specs=[ANY] * (1 + nw), out_specs=[ANY] * (1 + nw),
        out_shape=[_sds((N_CHIPS,) + a.shape, a.dtype) for a in arrays],
        scratch_shapes=[pltpu.SemaphoreType.DMA((n_sems,)) for _ in range(3)],
    )(*arrays)


def _owners_exchange(csbs):
    na = len(csbs)

    def plan(ins, outs, send_sems, recv_sems, local_sems, receiving):
        x, y, c = _place()
        me = 2 * x + y
        sends, arrivals = [], []
        for a in range(na):
            for k in range(N_CHIPS - 1):
                j = (me + 1 + k) % N_CHIPS
                sends.append(_remote(ins[a].at[k], outs[a].at[2 - k], send_sems.at[3 * a + k], recv_sems.at[3 * a + 2 - k],
                                     (j // 2, j % 2, c)))
                if receiving:
                    arrivals.append(_remote(ins[a].at[k], outs[a].at[k], send_sems.at[3 * a + k], recv_sems.at[3 * a + k], (x, y, c)))
        return sends, arrivals

    def start(ins, outs, *sems):
        for cp in plan(ins, outs, *sems, False)[0]:
            cp.start()

    def finish(ins, outs, *sems):
        sends, arrivals = plan(ins, outs, *sems, True)
        for cp in arrivals:
            cp.wait_recv()
        for cp in sends:
            cp.wait_send()

    return _Exchange(csbs, [_sds(g.shape, g.dtype) for g in csbs], 3 * na, start, finish)


def _swap_halves(gxs, name):
    na = len(gxs)
    half_shape = lambda g: (WT_ROWS // 2, g.shape[1]) if g.ndim == 2 else g.shape[2:]

    def body(*refs):
        ins, outs = refs[:na], refs[na:2 * na]
        send_sems, recv_sems = refs[2 * na:]
        x, y, c = _place()
        cps = []
        for a in range(na):
            if gxs[a].ndim == 2:
                for j in range(N_CHIPS):
                    rows = pl.ds(pl.multiple_of(WT_WIN[j] + (1 - c) * (WT_ROWS // 2), 16), WT_ROWS // 2)
                    cps.append(_remote(ins[a].at[rows], outs[a].at[j, 0], send_sems.at[na + j], recv_sems.at[na + j], (x, y, 1 - c)))
            else:
                cps.append(_remote(ins[a].at[pl.ds(0, N_CHIPS), pl.ds(1 - c, 1)], outs[a], send_sems.at[a], recv_sems.at[a],
                                   (x, y, 1 - c)))
        for cp in cps:
            cp.start()
        for cp in cps:
            cp.wait()

    return pl.pallas_call(
        body, name=name, in_specs=[ANY] * na, out_specs=[ANY] * na,
        out_shape=[_sds((N_CHIPS, 1) + half_shape(g), g.dtype) for g in gxs],
        scratch_shapes=[pltpu.SemaphoreType.DMA((na + N_CHIPS,)), pltpu.SemaphoreType.DMA((na + N_CHIPS,))],
    )(*gxs)


def _half_block(gx, tr, shard):
    if gx.ndim == 4:
        return pl.BlockSpec((1, 1, tr, gx.shape[3]), lambda *g: (shard(*g), g[-1][0], g[-2], 0))
    return pl.BlockSpec(
        (pl.Element(tr), pl.Element(gx.shape[1])),
        lambda *g: (pl.multiple_of(g[-1][2 + shard(*g)] + g[-1][0] * (WT_ROWS // 2) + g[-2] * tr, 16), 0))


def _share_halves(gfs):
    na = len(gfs)

    def body(*refs):
        ins, outs = refs[:na], refs[na:2 * na]
        send_sems, recv_sems = refs[2 * na:]
        x, y, c = _place()
        cps = [_remote(ins[a].at[pl.ds(c, 1)], outs[a].at[pl.ds(c, 1)], send_sems.at[a], recv_sems.at[a], (x, y, 1 - c))
               for a in range(na)]
        for cp in cps:
            cp.start()
        for a in range(na):
            cps[a].wait_send()
            _remote(ins[a].at[pl.ds(1 - c, 1)], outs[a].at[pl.ds(1 - c, 1)], send_sems.at[a], recv_sems.at[a], (x, y, 1 - c)).wait_recv()

    return pl.pallas_call(
        body, name="rs_share_halves", in_specs=[ANY] * na, out_specs=[ANY] * na,
        out_shape=[_sds(g.shape, g.dtype) for g in gfs], input_output_aliases={a: a for a in range(na)},
        scratch_shapes=[pltpu.SemaphoreType.DMA((na,)), pltpu.SemaphoreType.DMA((na,))],
    )(*gfs)


def _row_tile(rows, cols, budget=5 << 18):
    fits = [t for t in range(16, rows + 1, 16) if rows % t == 0 and t * cols * 4 <= budget]
    return max(fits) if fits else rows


def _chip_sums(gx, r1, where):
    _, _, r2, cd = r1.shape
    tr = _row_tile(r2, cd)

    def body(w_ref, a_ref, b_ref, o_ref):
        o_ref[0] = (a_ref[...].reshape(tr, cd) + b_ref[0, 0].astype(F32)).astype(BF16)

    other = lambda k, i, w: (w[1] + 1 + k) % N_CHIPS
    return pl.pallas_call(
        body, name="rs_chip_sums",
        grid_spec=pltpu.PrefetchScalarGridSpec(
            num_scalar_prefetch=1, grid=(N_CHIPS - 1, r2 // tr),
            in_specs=[_half_block(gx, tr, other), pl.BlockSpec((1, 1, tr, cd), lambda k, i, w: (other(k, i, w), 0, i, 0))],
            out_specs=pl.BlockSpec((1, tr, cd), lambda k, i, w: (k, i, 0))),
        out_shape=_sds((N_CHIPS - 1, r2, cd), BF16), compiler_params=_cp(("parallel", "parallel")),
    )(where, gx, r1)


def _owner_sum(gx, r1, r2x, where):
    _, _, r2, cd = r1.shape
    tr = _row_tile(r2, cd)

    def body(w_ref, a_ref, b_ref, r_ref, o_ref):
        acc = a_ref[...].reshape(tr, cd) + b_ref[0, 0].astype(F32)
        for k in range(N_CHIPS - 1):
            acc = acc + r_ref[k].astype(F32)
        o_ref[0] = acc

    return pl.pallas_call(
        body, name="rs_owner_sum",
        grid_spec=pltpu.PrefetchScalarGridSpec(
            num_scalar_prefetch=1, grid=(r2 // tr,),
            in_specs=[_half_block(gx, tr, lambda i, w: w[1]),
                      pl.BlockSpec((1, 1, tr, cd), lambda i, w: (w[1], 0, i, 0)),
                      pl.BlockSpec((N_CHIPS - 1, tr, cd), lambda i, w: (0, i, 0))],
            out_specs=pl.BlockSpec((1, tr, cd), lambda i, w: (w[0], i, 0))),
        out_shape=_sds((2, r2, cd)), compiler_params=_cp(("parallel",)),
    )(where, gx, r1, r2x)


def _adamw_math(w, g, m, v):
    m = ADAM_B1 * m + (1.0 - ADAM_B1) * g
    v = ADAM_B2 * v + (1.0 - ADAM_B2) * (g * g)
    m_hat = m / (1.0 - ADAM_B1 ** ADAM_STEP)
    v_hat = v / (1.0 - ADAM_B2 ** ADAM_STEP)
    delta = -ADAM_LR * (m_hat / (jnp.sqrt(v_hat) + ADAM_EPS) + ADAM_WD * w)
    return delta, m, v


def _adamw(w, g, m, v, name):
    rows, cd = w.shape
    if rows % 16 == 0:
        tr, tc = _row_tile(rows, cd, budget=3 << 19), cd
    else:
        tr, tc = rows, (128 if rows * cd * 4 > (3 << 19) else cd)
    assert rows % tr == 0 and cd % tc == 0

    def body(w_ref, g_ref, m_ref, v_ref, d_ref, mo_ref, vo_ref):
        d, mm, vv = _adamw_math(w_ref[...], g_ref[...], m_ref[...], v_ref[...])
        d_ref[...] = d
        mo_ref[...] = mm
        vo_ref[...] = vv

    blk = pl.BlockSpec((tr, tc), lambda i, j: (i, j))
    return pl.pallas_call(
        body, name=name, grid=(rows // tr, cd // tc), in_specs=[blk] * 4, out_specs=[blk] * 3, out_shape=[_sds(w.shape)] * 3,
        compiler_params=_cp(("parallel", "parallel")),
    )(w, g, m, v)


def _small_allreduce(gp):
    R = gp.shape[0]
    R2 = R // 2
    assert R2 % 8 == 0

    def body(g_ref, go_ref, sib, csum, land, send_sems, recv_sems):
        x, y, c = _place()
        me = 2 * x + y
        sibling = (x, y, 1 - c)
        chips = [(1 - x, y), (x, 1 - y), (1 - x, 1 - y)]
        swap = _remote(g_ref, sib, send_sems.at[0], recv_sems.at[0], sibling)
        swap.start()
        swap.wait()
        csum[...] = g_ref[...] + sib[...]
        half = csum.at[pl.ds(c * R2, R2)]
        land[me] = csum[pl.ds(c * R2, R2), :]
        cps = [_remote(half, land.at[me], send_sems.at[1 + j], recv_sems.at[1 + j], (px, py, c))
               for j, (px, py) in enumerate(chips)]
        for cp in cps:
            cp.start()
        for j, (px, py) in enumerate(chips):
            _remote(half, land.at[2 * px + py], send_sems.at[1 + j], recv_sems.at[1 + j], (px, py, c)).wait_recv()
        for cp in cps:
            cp.wait_send()
        mine = go_ref.at[pl.ds(c * R2, R2)]
        go_ref[pl.ds(c * R2, R2), :] = (land[0] + land[1]) + (land[2] + land[3])
        share = _remote(mine, mine, send_sems.at[4], recv_sems.at[4], sibling)
        share.start()
        share.wait_send()
        other = go_ref.at[pl.ds((1 - c) * R2, R2)]
        _remote(other, other, send_sems.at[4], recv_sems.at[4], sibling).wait_recv()

    vm = pl.BlockSpec(memory_space=pltpu.VMEM)
    return pl.pallas_call(
        body, name="small_allreduce", in_specs=[vm], out_specs=vm, out_shape=_sds((R, 128)),
        scratch_shapes=[pltpu.VMEM((R, 128), F32), pltpu.VMEM((R, 128), F32), pltpu.VMEM((N_CHIPS, R2, 128), F32),
                        pltpu.SemaphoreType.DMA((5,)), pltpu.SemaphoreType.DMA((5,))],
        compiler_params=_cp(),
    )(gp)


def _adamw_many(ws, gs, ms, vs):
    n = len(ws)

    def body(*refs):
        w_r, g_r, m_r, v_r = refs[:n], refs[n:2 * n], refs[2 * n:3 * n], refs[3 * n:4 * n]
        d_r, mo_r, vo_r = refs[4 * n:5 * n], refs[5 * n:6 * n], refs[6 * n:]
        for i in range(n):
            d_r[i][...], mo_r[i][...], vo_r[i][...] = _adamw_math(w_r[i][...], g_r[i][...], m_r[i][...], v_r[i][...])

    vm = pl.BlockSpec(memory_space=pltpu.VMEM)
    shapes = [_sds(a.shape) for a in ws]
    outs = pl.pallas_call(
        body, name="adamw_small", in_specs=[vm] * (4 * n), out_specs=[vm] * (3 * n), out_shape=shapes * 3, compiler_params=_cp(),
    )(*ws, *gs, *ms, *vs)
    return outs[:n], outs[n:2 * n], outs[2 * n:]


def _pack(arrs):
    rows = []
    for a in arrs:
        f = a.reshape(-1)
        f = jnp.pad(f, (0, (-f.shape[0]) % 128))
        rows.append(f.reshape(-1, 128))
    p = jnp.concatenate(rows, axis=0)
    return jnp.pad(p, ((0, (-p.shape[0]) % 8), (0, 0)))


def _unpack(p, shapes):
    out, r = [], 0
    for s in shapes:
        n = math.prod(s)
        nr = -(-n // 128)
        out.append(p[r:r + nr].reshape(-1)[:n].reshape(s))
        r += nr
    return out


class _ExchangeOps(_PlainOps):
    def __init__(self, rest_shards, where):
        self.rest_shards, self.where = rest_shards, where
        self.reduced = []

    def in_proj(self, h, wt_perm):
        proj, g_glu, g_su, g_du, g_out = _mm(h, wt_perm, tb=True, name="in_proj", tm=1024, tn=1152,
                                             exchange=_gather_exchange(self.rest_shards))
        cat = lambda g: jnp.concatenate([g[j] for j in range(N_CHIPS)], axis=1)
        return proj, (g_glu.reshape(D_S5, D_S5), cat(g_su), cat(g_du), g_out.reshape(D_MODEL, D_MODEL))

    def _chip_sums(self, gxs, twins, name):
        r1s = _swap_halves(twins, name)
        return r1s, [_chip_sums(gx, r1, self.where) for gx, r1 in zip(gxs, r1s)]

    def rest_grads(self, grads, twins):
        shapes = [(N_CHIPS, 2, D_S5 // 8, D_S5), (N_CHIPS, 2, D_S5 // 2, D_MODEL // N_CHIPS),
                  (N_CHIPS, 2, D_DN // 2, D_MODEL // N_CHIPS), (N_CHIPS, 2, D_MODEL // 8, D_MODEL)]
        gxs = [g.reshape(s) for g, s in zip(grads, shapes)]
        r1s, csbs = self._chip_sums(gxs, [t.reshape(s) for t, s in zip(twins, shapes)], "rs_swap_rest")
        self.rest = (gxs, r1s, csbs)

    def d_w_in(self, h, dproj):
        gxs, r1s, csbs = self.rest
        d_wt_perm, twin, *r2s = _mm(dproj, h, ta=True, name="d_w_in", tm=1152, tn=1024, twin=True,
                                    exchange=_owners_exchange(csbs))
        self.reduced = list(zip(gxs, r1s, r2s))
        return d_wt_perm, twin

    def d_h(self, dproj, wt_perm, d_wt_perm, d_wt_twin):
        self.beta_a = d_wt_perm[OFF_B:OFF_B + WT_NB]
        (r1,), (csb,) = self._chip_sums([d_wt_perm], [d_wt_twin], "rs_swap_w_in")
        dh, r2 = _mm(dproj, wt_perm, name="d_h", tm=2048, tn=1024, tk=1152, exchange=_owners_exchange([csb]))
        self.reduced = [(d_wt_perm, r1, r2)] + self.reduced
        return dh


WT_SHARD = D_IN // N_CHIPS
WT_NB = 2 * DN_HEADS
WT_B, WT_LO = divmod(OFF_GS, WT_SHARD)
WT_FIRST = [i * WT_SHARD - (WT_NB if i > WT_B else 0) for i in range(N_CHIPS)]
WT_WIN = [16 * (r // 16) for r in WT_FIRST]
WT_SHIFT = [r - s for r, s in zip(WT_FIRST, WT_WIN)]
WT_ROWS = 2592
assert (WT_LO + WT_SHIFT[WT_B]) % 16 == 0 and max(WT_SHIFT) + WT_SHARD <= WT_ROWS and WT_WIN[-1] + WT_ROWS <= D_IN_PAD


def _wt_to_window(shard, chip):
    d = jnp.asarray(WT_SHIFT, jnp.int32)[chip]
    gap = jnp.where(chip == WT_B, 0, WT_NB)
    win = jnp.zeros((WT_ROWS, shard.shape[1]), shard.dtype)
    win = lax.dynamic_update_slice(win, shard[:WT_LO], (d, 0))
    win = lax.dynamic_update_slice(win, shard[WT_LO:WT_LO + WT_NB], (d + WT_LO, 0))
    win = lax.dynamic_update_slice(win, shard[WT_LO + WT_NB:], (d + WT_LO + gap, 0))
    return win, shard[WT_LO:WT_LO + WT_NB]


def _wt_from_window(win, beta_a, chip):
    d = jnp.asarray(WT_SHIFT, jnp.int32)[chip]
    gap = jnp.where(chip == WT_B, 0, WT_NB)
    cols = win.shape[1]
    head = lax.dynamic_slice(win, (d, 0), (WT_LO, cols))
    mid = jnp.where(chip == WT_B, beta_a, lax.dynamic_slice(win, (d + WT_LO, 0), (WT_NB, cols)))
    tail = lax.dynamic_slice(win, (d + WT_LO + gap, 0), (WT_SHARD - WT_LO - WT_NB, cols))
    return jnp.concatenate([head, mid, tail], axis=0)


def _wt_regroup(wins, beta_a):
    parts, at = [], 0
    for i in range(N_CHIPS):
        end = WT_WIN[i + 1] if i + 1 < N_CHIPS else OFF_B
        lo = at - WT_WIN[i]
        over = WT_WIN[i] + WT_ROWS - end if i + 1 < N_CHIPS else 0
        parts.append(wins[i, lo:end - WT_WIN[i]])
        if over:
            parts.append(wins[i, end - WT_WIN[i]:] + wins[i + 1, :over])
        at = end + over
    pad = jnp.zeros((D_IN_PAD - OFF_B - WT_NB, wins.shape[2]), wins.dtype)
    return jnp.concatenate(parts + [beta_a, pad], axis=0)


_SMALL = ("ln_w", "s5_lam_re", "s5_lam_im", "s5_log_step", "s5_b_re", "s5_b_im", "s5_c_re", "s5_c_im", "s5_d",
          "dn_a_log", "dn_dt_bias", "dn_norm_w", "final_norm_w")
_BIG = ("w_in", "s5_w_glu", "s5_w_up", "dn_w_up", "w_out")
_ORDER = ("ln_w", "w_in", "s5_lam_re", "s5_lam_im", "s5_log_step", "s5_b_re", "s5_b_im", "s5_c_re", "s5_c_im", "s5_d",
          "s5_w_glu", "s5_w_up", "dn_conv_w", "dn_a_log", "dn_dt_bias", "dn_norm_w", "dn_w_up", "w_out", "final_norm_w")


def kernel(x, ln_w, w_in, s5_lam_re, s5_lam_im, s5_log_step, s5_b_re, s5_b_im, s5_c_re, s5_c_im, s5_d, s5_w_glu, s5_w_up, dn_conv_w, dn_a_log, dn_dt_bias, dn_norm_w, dn_w_up, w_out, final_norm_w, loss_target, m_ln_w, m_w_in, m_s5_lam_re, m_s5_lam_im, m_s5_log_step, m_s5_b_re, m_s5_b_im, m_s5_c_re, m_s5_c_im, m_s5_d, m_s5_w_glu, m_s5_w_up, m_dn_conv_w, m_dn_a_log, m_dn_dt_bias, m_dn_norm_w, m_dn_w_up, m_w_out, m_final_norm_w, v_ln_w, v_w_in, v_s5_lam_re, v_s5_lam_im, v_s5_log_step, v_s5_b_re, v_s5_b_im, v_s5_c_re, v_s5_c_im, v_s5_d, v_s5_w_glu, v_s5_w_up, v_dn_conv_w, v_dn_a_log, v_dn_dt_bias, v_dn_norm_w, v_dn_w_up, v_w_out, v_final_norm_w):
    w = dict(ln_w=ln_w, w_in=w_in, s5_lam_re=s5_lam_re, s5_lam_im=s5_lam_im, s5_log_step=s5_log_step, s5_b_re=s5_b_re,
             s5_b_im=s5_b_im, s5_c_re=s5_c_re, s5_c_im=s5_c_im, s5_d=s5_d, s5_w_glu=s5_w_glu, s5_w_up=s5_w_up,
             dn_conv_w=dn_conv_w, dn_a_log=dn_a_log, dn_dt_bias=dn_dt_bias, dn_norm_w=dn_norm_w, dn_w_up=dn_w_up, w_out=w_out,
             final_norm_w=final_norm_w)
    m = dict(ln_w=m_ln_w, w_in=m_w_in, s5_lam_re=m_s5_lam_re, s5_lam_im=m_s5_lam_im, s5_log_step=m_s5_log_step,
             s5_b_re=m_s5_b_re, s5_b_im=m_s5_b_im, s5_c_re=m_s5_c_re, s5_c_im=m_s5_c_im, s5_d=m_s5_d, s5_w_glu=m_s5_w_glu,
             s5_w_up=m_s5_w_up, dn_conv_w=m_dn_conv_w, dn_a_log=m_dn_a_log, dn_dt_bias=m_dn_dt_bias, dn_norm_w=m_dn_norm_w,
             dn_w_up=m_dn_w_up, w_out=m_w_out, final_norm_w=m_final_norm_w)
    v = dict(ln_w=v_ln_w, w_in=v_w_in, s5_lam_re=v_s5_lam_re, s5_lam_im=v_s5_lam_im, s5_log_step=v_s5_log_step,
             s5_b_re=v_s5_b_re, s5_b_im=v_s5_b_im, s5_c_re=v_s5_c_re, s5_c_im=v_s5_c_im, s5_d=v_s5_d, s5_w_glu=v_s5_w_glu,
             s5_w_up=v_s5_w_up, dn_conv_w=v_dn_conv_w, dn_a_log=v_dn_a_log, dn_dt_bias=v_dn_dt_bias, dn_norm_w=v_dn_norm_w,
             dn_w_up=v_dn_w_up, w_out=v_w_out, final_norm_w=v_final_norm_w)
    xi, yi, ci = _place()
    chip = 2 * xi + yi
    where = jnp.stack([ci, chip, *[jnp.int32(s) for s in WT_WIN]]).astype(jnp.int32)

    tr = lambda a: jnp.swapaxes(a[0], 0, 1)
    win, beta_a = _wt_to_window(tr(w_in).astype(BF16), chip)
    g_win, g_ba, g_conv = _gather_relayed(win, [beta_a, dn_conv_w[0]], "gather_w_in")
    cat = lambda g: jnp.concatenate([g[j] for j in range(N_CHIPS)], axis=1)
    w_perm = _wt_regroup(g_win, g_ba[WT_B])

    ops = _ExchangeOps([w[n][0].astype(BF16) for n in _BIG[1:]], where)
    loss_row, grad_x, g = _local_step(
        x[0], loss_target[0], ln_w, w_perm, s5_lam_re[0], s5_lam_im[0], s5_log_step, s5_b_re[0], s5_b_im[0], s5_c_re[0],
        s5_c_im[0], s5_d, cat(g_conv), dn_a_log, dn_dt_bias, dn_norm_w, final_norm_w[None], ops)
    loss = lax.psum(loss_row[0, 0], ("x", "y", "c"))

    gfs = [_owner_sum(gx, r1, r2x, where) for gx, r1, r2x in ops.reduced]
    gfs = _share_halves(gfs)
    grads, deltas, new_m, new_v = {}, {}, {}, {}
    for n, gf in zip(_BIG[1:], gfs[1:]):
        shp = w[n].shape
        g2 = gf.reshape(shp[1:])
        d_, m_, v_ = _adamw(w[n][0], g2, m[n][0], v[n][0], "adamw_" + n)
        grads[n], deltas[n], new_m[n], new_v[n] = g2.reshape(shp), d_.reshape(shp), m_.reshape(shp), v_.reshape(shp)

    go = _small_allreduce(_pack([g[n] for n in _SMALL] + [g["dn_conv_w"], ops.beta_a]))
    lanes = {"s5_b_re": (S5_GROUPS, S5_STATE * S5_GROUP), "s5_b_im": (S5_GROUPS, S5_STATE * S5_GROUP)}
    flat = [lanes.get(n, (math.prod(w[n].shape[:-1]), w[n].shape[-1])) for n in _SMALL]
    *gs, g_conv, g_beta_a = _unpack(go, flat + [(CONV_K, 3 * D_DN), (WT_NB, D_MODEL)])
    gt = _wt_from_window(gfs[0].reshape(WT_ROWS, D_MODEL), g_beta_a, chip)
    d_, m_, v_ = _adamw(tr(w_in), gt, tr(m_w_in), tr(v_w_in), "adamw_w_in")
    grads["w_in"], deltas["w_in"], new_m["w_in"], new_v["w_in"] = (jnp.swapaxes(a, 0, 1)[None] for a in (gt, d_, m_, v_))
    as2d = lambda t: [t[n].reshape(s) for n, s in zip(_SMALL, flat)]
    for dst, src in zip((grads, deltas, new_m, new_v), (gs, *_adamw_many(as2d(w), gs, as2d(m), as2d(v)))):
        for n, a in zip(_SMALL, src):
            dst[n] = a.reshape(w[n].shape)
    cc = 3 * D_DN // N_CHIPS
    g_conv_mine = lax.dynamic_slice(g_conv, (0, chip * cc), (CONV_K, cc))
    d_, m_, v_ = _adamw(dn_conv_w[0], g_conv_mine, m_dn_conv_w[0], v_dn_conv_w[0], "adamw_dn_conv_w")
    grads["dn_conv_w"], deltas["dn_conv_w"], new_m["dn_conv_w"], new_v["dn_conv_w"] = (
        g_conv_mine[None], d_[None], m_[None], v_[None])

    return (loss, grad_x[None], *[grads[n] for n in _ORDER], *[deltas[n] for n in _ORDER], *[new_m[n] for n in _ORDER],
            *[new_v[n] for n in _ORDER])
```

```python
import functools
import math

import jax
import jax.numpy as jnp
from jax import lax
from jax.experimental import pallas as pl
from jax.experimental.pallas import tpu as pltpu

F32 = jnp.float32
BF16 = jnp.bfloat16
HI = lax.Precision.HIGHEST
MESH = pl.DeviceIdType.MESH
ANY = pl.BlockSpec(memory_space=pl.ANY)

EPS = 1e-6
D_MODEL = 2048
D_S5 = 1024
S5_GROUP = 16
S5_GROUPS = 64
S5_STATE = 64
S5_BLOCKS = 8
S5_SEG = 8
DN_HEADS = 8
DN_HEAD_DIM = 128
D_DN = 1024
CONV_K = 4
CHUNK = 64
D_IN = 10256
D_IN_PAD = 10368
OFF_US, OFF_ZS, OFF_Q, OFF_K, OFF_V, OFF_ZD, OFF_GS, OFF_GD, OFF_B = 0, 1024, 2048, 3072, 4096, 5120, 6144, 8192, 10240
N_CHIPS = 4
N_DEV = 8
VMEM_LIMIT = 56 * 1024 * 1024

ADAM_LR = 0.001
ADAM_B1 = 0.9
ADAM_B2 = 0.999
ADAM_EPS = 1e-08
ADAM_WD = 0.01
ADAM_STEP = 10


def _cp(sem=None):
    return pltpu.CompilerParams(dimension_semantics=sem, vmem_limit_bytes=VMEM_LIMIT)


def _sds(shape, dtype=F32):
    return jax.ShapeDtypeStruct(tuple(shape), dtype)


def _sigmoid(x):
    return 1.0 / (1.0 + jnp.exp(-x))


def _silu(x):
    return x * _sigmoid(x)


def _dsilu(x):
    s = _sigmoid(x)
    return s * (1.0 + x * (1.0 - s))


class _Exchange:
    def __init__(self, ins, out_shapes, n_sems, start, finish):
        self.ins, self.out_shapes, self.n_sems, self.start, self.finish = list(ins), list(out_shapes), n_sems, start, finish


def _mm(a, b, *, name, ta=False, tb=False, out_dtype=F32, tm=512, tn=512, tk=2048, shard_out=False, twin=False, exchange=None):
    if ta:
        K, M = a.shape
    else:
        M, K = a.shape
    if tb:
        N, K2 = b.shape
    else:
        K2, N = b.shape
    assert K == K2, (a.shape, b.shape)
    tm, tn, tk = min(tm, M), min(tn, N), min(tk, K)
    assert M % tm == 0 and N % tn == 0 and K % tk == 0, (M, N, K, tm, tn, tk)
    nk = K // tk
    dims = (((0 if ta else 1,), (1 if tb else 0,)), ((), ()))

    gm, gn = M // tm, N // tn
    n_in = len(exchange.ins) if exchange else 0
    n_out = len(exchange.out_shapes) if exchange else 0

    n_o = 2 if twin else 1

    def body(*refs):
        a_ref, b_ref, xin, o_refs = refs[0], refs[1], refs[2:2 + n_in], refs[2 + n_in:2 + n_in + n_o]
        xout, rest = refs[2 + n_in + n_o:2 + n_in + n_o + n_out], refs[2 + n_in + n_o + n_out:]
        i, j, k = pl.program_id(0), pl.program_id(1), pl.program_id(2)

        def write(val):
            o_refs[0][...] = val.astype(out_dtype).reshape(o_refs[0].shape)
            if twin:
                o_refs[1][...] = val.astype(BF16).reshape(o_refs[1].shape)

        if exchange:
            sems = rest[-3:]

            @pl.when(jnp.logical_and(jnp.logical_and(i == 0, j == 0), k == 0))
            def _():
                exchange.start(xin, xout, *sems)

        p = lax.dot_general(a_ref[...].astype(BF16), b_ref[...].astype(BF16), dims, preferred_element_type=F32)
        if nk == 1:
            write(p)
        else:
            acc_ref = rest[0]

            @pl.when(k == 0)
            def _():
                acc_ref[...] = p

            @pl.when(k > 0)
            def _():
                acc_ref[...] += p

            @pl.when(k == nk - 1)
            def _():
                write(acc_ref[...])

        if exchange:
            @pl.when(jnp.logical_and(jnp.logical_and(i == gm - 1, j == gn - 1), k == nk - 1))
            def _():
                exchange.finish(xin, xout, *sems)

    a_spec = pl.BlockSpec((tk, tm), lambda i, j, k: (k, i)) if ta else pl.BlockSpec((tm, tk), lambda i, j, k: (i, k))
    b_spec = pl.BlockSpec((tn, tk), lambda i, j, k: (j, k)) if tb else pl.BlockSpec((tk, tn), lambda i, j, k: (k, j))
    if shard_out:
        o_spec = pl.BlockSpec((1, tm, tn), lambda i, j, k: (j, i, 0))
        o_shape = _sds((N // tn, M, tn), out_dtype)
    else:
        o_spec = pl.BlockSpec((tm, tn), lambda i, j, k: (i, j))
        o_shape = _sds((M, N), out_dtype)
    scratch = [pltpu.VMEM((tm, tn), F32)] if nk > 1 else []
    o_specs, o_shapes = [o_spec] * n_o, [o_shape, _sds(o_shape.shape, BF16)][:n_o]
    if not exchange:
        out = pl.pallas_call(
            body, name=name, grid=(gm, gn, nk), in_specs=[a_spec, b_spec], out_specs=o_specs, out_shape=o_shapes,
            scratch_shapes=scratch, compiler_params=_cp(("parallel", "parallel", "arbitrary")),
        )(a, b)
        return out if twin else out[0]
    scratch += [pltpu.SemaphoreType.DMA((exchange.n_sems,)) for _ in range(3)]
    return pl.pallas_call(
        body, name=name, grid=(gm, gn, nk), in_specs=[a_spec, b_spec] + [ANY] * n_in, out_specs=o_specs + [ANY] * n_out,
        out_shape=o_shapes + exchange.out_shapes, scratch_shapes=scratch,
        compiler_params=_cp(("arbitrary", "arbitrary", "arbitrary")),
    )(a, b, *exchange.ins)


def _ln_fwd(x, w):
    L, D = x.shape
    tm = min(256, L)

    def body(x_ref, w_ref, h_ref, r_ref):
        xv = x_ref[...]
        r = lax.rsqrt(jnp.mean(xv * xv, axis=-1, keepdims=True) + EPS)
        h_ref[...] = (xv * r * w_ref[...]).astype(BF16)
        r_ref[...] = r

    return pl.pallas_call(
        body, name="ln_fwd", grid=(L // tm,),
        in_specs=[pl.BlockSpec((tm, D), lambda i: (i, 0)), pl.BlockSpec((1, D), lambda i: (0, 0))],
        out_specs=[pl.BlockSpec((tm, D), lambda i: (i, 0)), pl.BlockSpec((tm, 1), lambda i: (i, 0))],
        out_shape=[_sds((L, D), BF16), _sds((L, 1))], compiler_params=_cp(("parallel",)),
    )(x, w)


def _ln_bwd(x, r, w, dh, dx2):
    L, D = x.shape
    tm = min(256, L)

    def body(x_ref, r_ref, w_ref, dh_ref, dx2_ref, dx_ref, dw_ref):
        i = pl.program_id(0)
        xv, rv, dhv = x_ref[...], r_ref[...], dh_ref[...]
        t = dhv * w_ref[...]
        m = jnp.mean(t * xv, axis=-1, keepdims=True)
        dx_ref[...] = dx2_ref[...] + rv * t - xv * (rv * rv * rv) * m
        part = jnp.sum(dhv * xv * rv, axis=0, keepdims=True)

        @pl.when(i == 0)
        def _():
            dw_ref[...] = part

        @pl.when(i > 0)
        def _():
            dw_ref[...] += part

    row = pl.BlockSpec((tm, D), lambda i: (i, 0))
    return pl.pallas_call(
        body, name="ln_bwd", grid=(L // tm,),
        in_specs=[row, pl.BlockSpec((tm, 1), lambda i: (i, 0)), pl.BlockSpec((1, D), lambda i: (0, 0)), row, row],
        out_specs=[row, pl.BlockSpec((1, D), lambda i: (0, 0))],
        out_shape=[_sds((L, D)), _sds((1, D))], compiler_params=_cp(("arbitrary",)),
    )(x, r, w, dh, dx2)


def _s5_param_math(lam_re, lam_im, log_step, b_re, b_im, expand):
    step = jnp.exp(log_step)
    mag = jnp.exp(lam_re * step)
    abar_re = mag * jnp.cos(lam_im * step)
    abar_im = mag * jnp.sin(lam_im * step)
    den = lam_re * lam_re + lam_im * lam_im
    xr = abar_re - 1.0
    f_re = (xr * lam_re + abar_im * lam_im) / den
    f_im = (abar_im * lam_re - xr * lam_im) / den
    fe_re = jnp.dot(f_re, expand, precision=HI, preferred_element_type=F32)
    fe_im = jnp.dot(f_im, expand, precision=HI, preferred_element_type=F32)
    bb_re = fe_re * b_re - fe_im * b_im
    bb_im = fe_re * b_im + fe_im * b_re
    return abar_re, abar_im, bb_re, bb_im


def _s5_expand():
    p = lax.broadcasted_iota(jnp.int32, (S5_STATE, S5_STATE * S5_GROUP), 0)
    q = lax.broadcasted_iota(jnp.int32, (S5_STATE, S5_STATE * S5_GROUP), 1)
    return (q // S5_GROUP == p).astype(F32)


def _s5_param_fwd(lam_re, lam_im, log_step, b_re, b_im):
    G, P = lam_re.shape

    def body(lr, li, ls, br, bi, ar_o, ai_o, bbr_o, bbi_o):
        outs = _s5_param_math(lr[...], li[...], ls[...], br[...], bi[...], _s5_expand())
        for o, v in zip((ar_o, ai_o, bbr_o, bbi_o), outs):
            o[...] = v

    return pl.pallas_call(
        body, name="s5_param_fwd",
        out_shape=[_sds((G, P)), _sds((G, P)), _sds(b_re.shape), _sds(b_re.shape)], compiler_params=_cp(),
    )(lam_re, lam_im, log_step, b_re, b_im)


def _s5_param_bwd(lam_re, lam_im, log_step, b_re, b_im, dar, dai, dbbr, dbbi):
    G, P = lam_re.shape

    def body(lr, li, ls, br, bi, g0, g1, g2, g3, dlr, dli, dls, dbr, dbi):
        ex = _s5_expand()
        _, f = jax.vjp(lambda a, b, c, d, e: _s5_param_math(a, b, c, d, e, ex), lr[...], li[...], ls[...], br[...], bi[...])
        grads = f((g0[...], g1[...], g2[...], g3[...]))
        for o, v in zip((dlr, dli, dls, dbr, dbi), grads):
            o[...] = v

    return pl.pallas_call(
        body, name="s5_param_bwd",
        out_shape=[_sds((G, P)), _sds((G, P)), _sds((G, 1)), _sds(b_re.shape), _sds(b_re.shape)], compiler_params=_cp(),
    )(lam_re, lam_im, log_step, b_re, b_im, dar, dai, dbbr, dbbi)


def _to_segs(src_ref, dst_ref, L):
    S = L // S5_SEG

    def body(j, carry):
        dst_ref[pl.ds(pl.multiple_of(S5_SEG * j, S5_SEG), S5_SEG), :] = src_ref[pl.ds(j, S5_SEG, stride=S), :]
        return carry

    lax.fori_loop(0, S, body, 0, unroll=8)


def _from_segs(src_ref, L, write):
    S = L // S5_SEG
    for seg in range(S5_SEG):
        def body(jb, carry, seg=seg):
            j0 = 16 * jb
            write(pl.multiple_of(seg * S + j0, 16), src_ref[pl.ds(S5_SEG * j0 + seg, 16, stride=S5_SEG), :])
            return carry

        lax.fori_loop(0, S // 16, body, 0, unroll=4)


def _scan_segs(ar, ai, re_ref, im_ref, end_r_ref, end_i_ref, c_r_ref, c_i_ref, L, tile0, reverse):
    S = L // S5_SEG
    NB, LN = re_ref.shape[0], 128
    assert S & (S - 1) == 0
    tile = lambda j: pl.ds(pl.multiple_of(S5_SEG * (tile0 + j), S5_SEG), S5_SEG)
    ar8 = [jnp.broadcast_to(ar[:, b * LN:(b + 1) * LN], (S5_SEG, LN)) for b in range(NB)]
    ai8 = [jnp.broadcast_to(ai[:, b * LN:(b + 1) * LN], (S5_SEG, LN)) for b in range(NB)]

    def step(idx, carry):
        rows = tile(S - 1 - idx if reverse else idx)
        out = []
        for b in range(NB):
            sr, si = carry[b]
            nr = ar8[b] * sr - ai8[b] * si + re_ref[b, rows, :]
            ni = ar8[b] * si + ai8[b] * sr + im_ref[b, rows, :]
            re_ref[b, rows, :] = nr
            im_ref[b, rows, :] = ni
            out.append((nr, ni))
        return tuple(out)

    z8 = jnp.zeros((S5_SEG, LN), F32)
    fin = lax.fori_loop(0, S, step, tuple((z8, z8) for _ in range(NB)), unroll=4)
    order = range(S5_SEG - 2, -1, -1) if reverse else range(1, S5_SEG)
    for b in range(NB):
        end_r_ref[b], end_i_ref[b] = fin[b]
        pr, pi = ar8[b][:1], ai8[b][:1]
        for _ in range(int(math.log2(S))):
            pr, pi = pr * pr - pi * pi, 2.0 * pr * pi
        first = S5_SEG - 1 if reverse else 0
        c_r_ref[b, pl.ds(first, 1), :] = jnp.zeros((1, LN), F32)
        c_i_ref[b, pl.ds(first, 1), :] = jnp.zeros((1, LN), F32)
        cr, ci = end_r_ref[b, pl.ds(first, 1), :], end_i_ref[b, pl.ds(first, 1), :]
        for i in order:
            c_r_ref[b, pl.ds(i, 1), :] = cr
            c_i_ref[b, pl.ds(i, 1), :] = ci
            er, ei = end_r_ref[b, pl.ds(i, 1), :], end_i_ref[b, pl.ds(i, 1), :]
            cr, ci = er + pr * cr - pi * ci, ei + pr * ci + pi * cr

    entering = [(c_r_ref[b], c_i_ref[b]) for b in range(NB)]

    def fix(idx, carry):
        rows = tile(S - 1 - idx if reverse else idx)
        out = []
        for b in range(NB):
            pr, pi = carry[b]
            cr, ci = entering[b]
            re_ref[b, rows, :] += pr * cr - pi * ci
            im_ref[b, rows, :] += pr * ci + pi * cr
            out.append((pr * ar8[b] - pi * ai8[b], pr * ai8[b] + pi * ar8[b]))
        return tuple(out)

    lax.fori_loop(0, S, fix, tuple((ar8[b], ai8[b]) for b in range(NB)), unroll=4)


def _s5_seg_scratch(L, cs, pad):
    NB = cs // 128
    small = [pltpu.VMEM((NB, S5_SEG, 128), F32) for _ in range(4)]
    return [pltpu.VMEM((NB, L + pad, 128), F32), pltpu.VMEM((NB, L + pad, 128), F32)] + small


def _s5_core_fwd(proj, wbr, wbi, a_re, a_im, cbr, cbi):
    L = proj.shape[0]
    nb, ci, cs = wbr.shape
    NB = cs // 128

    def body(u_ref, wbr_ref, wbi_ref, ar_ref, ai_ref, cbr_ref, cbi_ref, y_ref, sr, si, er, ei, cr, cim, up, yp):
        _to_segs(u_ref, up, L)
        u = up[...].astype(BF16)
        for b in range(NB):
            lanes = pl.ds(b * 128, 128)
            sr[b] = jnp.dot(u, wbr_ref[0, :, lanes], preferred_element_type=F32)
            si[b] = jnp.dot(u, wbi_ref[0, :, lanes], preferred_element_type=F32)
        _scan_segs(ar_ref[...], ai_ref[...], sr, si, er, ei, cr, cim, L, 0, False)
        y = jnp.zeros((L, ci), F32)
        for b in range(NB):
            lanes = pl.ds(b * 128, 128)
            y = y + (jnp.dot(sr[b].astype(BF16), cbr_ref[0, lanes, :], preferred_element_type=F32)
                     - jnp.dot(si[b].astype(BF16), cbi_ref[0, lanes, :], preferred_element_type=F32))
        yp[...] = y

        def write(row, val):
            y_ref[pl.ds(row, 16), :] = val

        _from_segs(yp, L, write)

    wspec = pl.BlockSpec((1, ci, cs), lambda j: (j, 0, 0))
    aspec = pl.BlockSpec((1, cs), lambda j: (0, j))
    cspec = pl.BlockSpec((1, cs, ci), lambda j: (j, 0, 0))
    return pl.pallas_call(
        body, name="s5_core_fwd", grid=(nb,),
        in_specs=[pl.BlockSpec((L, ci), lambda j: (0, OFF_US // ci + j)), wspec, wspec, aspec, aspec, cspec, cspec],
        out_specs=pl.BlockSpec((L, ci), lambda j: (0, j)), out_shape=_sds((L, nb * ci)),
        scratch_shapes=_s5_seg_scratch(L, cs, 0) + [pltpu.VMEM((L, ci), F32), pltpu.VMEM((L, ci), F32)],
        compiler_params=_cp(("arbitrary",)),
    )(proj, wbr, wbi, a_re, a_im, cbr, cbi)


def _s5_core_bwd(proj, wbr, wbi, a_re, a_im, cbr, cbi, dyc, du1):
    L = proj.shape[0]
    nb, ci, cs = wbr.shape
    NB = cs // 128
    S = L // S5_SEG
    PAD = S5_SEG

    def body(u_ref, wbr_ref, wbi_ref, ar_ref, ai_ref, cbr_ref, cbi_ref, dy_ref, du1_ref,
             du_ref, dwbr_ref, dwbi_ref, dcbr_ref, dcbi_ref, dar_ref, dai_ref,
             sr, si, er, ei, cr, cim, lr, li, up, dyp, dup):
        tn = (((0,), (0,)), ((), ()))
        nt = (((1,), (1,)), ((), ()))
        _to_segs(u_ref, up, L)
        _to_segs(dy_ref, dyp, L)
        _to_segs(du1_ref, dup, L)
        u = up[...].astype(BF16)
        dy = dyp[...].astype(BF16)
        ar, ai = ar_ref[...], ai_ref[...]
        for b in range(NB):
            lanes = pl.ds(b * 128, 128)
            sr[b, pl.ds(PAD, L), :] = jnp.dot(u, wbr_ref[0, :, lanes], preferred_element_type=F32)
            si[b, pl.ds(PAD, L), :] = jnp.dot(u, wbi_ref[0, :, lanes], preferred_element_type=F32)
        _scan_segs(ar, ai, sr, si, er, ei, cr, cim, L, 1, False)
        for b in range(NB):
            lanes = pl.ds(b * 128, 128)
            sr[b, pl.ds(0, PAD), :] = cr[b]
            si[b, pl.ds(0, PAD), :] = cim[b]
            lr[b] = lax.dot_general(dy, cbr_ref[0, lanes, :], nt, preferred_element_type=F32)
            li[b] = -lax.dot_general(dy, cbi_ref[0, lanes, :], nt, preferred_element_type=F32)
            dcbr_ref[0, lanes, :] = lax.dot_general(sr[b, pl.ds(PAD, L), :].astype(BF16), dy, tn, preferred_element_type=F32)
            dcbi_ref[0, lanes, :] = -lax.dot_general(si[b, pl.ds(PAD, L), :].astype(BF16), dy, tn, preferred_element_type=F32)
        _scan_segs(ar, -ai, lr, li, er, ei, cr, cim, L, 0, True)

        def da_step(j, carry):
            rows = pl.ds(pl.multiple_of(S5_SEG * j, S5_SEG), S5_SEG)
            out = []
            for b in range(NB):
                dar, dai = carry[b]
                pr_, pi_ = sr[b, rows, :], si[b, rows, :]
                gr, gi = lr[b, rows, :], li[b, rows, :]
                out.append((dar + (gr * pr_ + gi * pi_), dai + (gi * pr_ - gr * pi_)))
            return tuple(out)

        z8 = jnp.zeros((S5_SEG, 128), F32)
        acc = lax.fori_loop(0, S, da_step, tuple((z8, z8) for _ in range(NB)), unroll=4)
        du = dup[...]
        for b in range(NB):
            lanes = pl.ds(b * 128, 128)
            dar_ref[:, lanes] = jnp.sum(acc[b][0], axis=0, keepdims=True)
            dai_ref[:, lanes] = jnp.sum(acc[b][1], axis=0, keepdims=True)
            gr, gi = lr[b].astype(BF16), li[b].astype(BF16)
            du = du + (lax.dot_general(gr, wbr_ref[0, :, lanes], nt, preferred_element_type=F32)
                       + lax.dot_general(gi, wbi_ref[0, :, lanes], nt, preferred_element_type=F32))
            dwbr_ref[0, :, lanes] = lax.dot_general(u, gr, tn, preferred_element_type=F32)
            dwbi_ref[0, :, lanes] = lax.dot_general(u, gi, tn, preferred_element_type=F32)
        dup[...] = du

        def write(row, val):
            du_ref[pl.ds(row, 16), :] = val.astype(BF16)

        _from_segs(dup, L, write)

    wspec = pl.BlockSpec((1, ci, cs), lambda j: (j, 0, 0))
    aspec = pl.BlockSpec((1, cs), lambda j: (0, j))
    cspec = pl.BlockSpec((1, cs, ci), lambda j: (j, 0, 0))
    col = pl.BlockSpec((L, ci), lambda j: (0, j))
    return pl.pallas_call(
        body, name="s5_core_bwd", grid=(nb,),
        in_specs=[pl.BlockSpec((L, ci), lambda j: (0, OFF_US // ci + j)), wspec, wspec, aspec, aspec, cspec, cspec, col, col],
        out_specs=[col, wspec, wspec, cspec, cspec, aspec, aspec],
        out_shape=[_sds((L, nb * ci), BF16), _sds(wbr.shape), _sds(wbr.shape), _sds(cbr.shape), _sds(cbr.shape),
                   _sds((1, nb * cs)), _sds((1, nb * cs))],
        scratch_shapes=(_s5_seg_scratch(L, cs, PAD) + [pltpu.VMEM((NB, L, 128), F32), pltpu.VMEM((NB, L, 128), F32)]
                        + [pltpu.VMEM((L, ci), F32) for _ in range(3)]),
        compiler_params=_cp(("arbitrary",)),
    )(proj, wbr, wbi, a_re, a_im, cbr, cbi, dyc, du1)


def _s5_post_math(yc, u, z, d, wg):
    y = yc + d * u
    y1 = jax.nn.gelu(y)
    t = jnp.dot(y1.astype(BF16), wg, preferred_element_type=F32)
    sg = _sigmoid(t)
    return y, y1, sg


def _s5_post_fwd(yc, proj, d, wg):
    L, W = yc.shape
    tm = min(256, L)

    def body(yc_ref, u_ref, z_ref, d_ref, wg_ref, o_ref):
        _, y1, sg = _s5_post_math(yc_ref[...], u_ref[...], z_ref[...], d_ref[...], wg_ref[...])
        o_ref[...] = (y1 * sg * _silu(z_ref[...])).astype(BF16)

    row = pl.BlockSpec((tm, W), lambda i: (i, 0))
    return pl.pallas_call(
        body, name="s5_post_fwd", grid=(L // tm,),
        in_specs=[row, pl.BlockSpec((tm, W), lambda i: (i, OFF_US // W)), pl.BlockSpec((tm, W), lambda i: (i, OFF_ZS // W)),
                  pl.BlockSpec((1, W), lambda i: (0, 0)), pl.BlockSpec((W, W), lambda i: (0, 0))],
        out_specs=row, out_shape=_sds((L, W), BF16), compiler_params=_cp(("parallel",)),
    )(yc, proj, proj, d, wg)


def _s5_post_bwd(yc, proj, d, wg, dout):
    L, W = yc.shape
    tm = min(256, L)

    def body(yc_ref, u_ref, z_ref, d_ref, wg_ref, do_ref, dyc_ref, du_ref, dz_ref, dd_ref, dwg_ref):
        i = pl.program_id(0)
        u, z, d_, wgv = u_ref[...], z_ref[...], d_ref[...], wg_ref[...]
        y, y1, sg = _s5_post_math(yc_ref[...], u, z, d_, wgv)
        dout_ = do_ref[...]
        y2 = y1 * sg
        dy2 = dout_ * _silu(z)
        dz_ref[...] = (dout_ * y2 * _dsilu(z)).astype(BF16)
        dt = (dy2 * y1 * sg * (1.0 - sg)).astype(BF16)
        dy1 = dy2 * sg + lax.dot_general(dt, wgv, (((1,), (1,)), ((), ())), preferred_element_type=F32)
        _, gelu_vjp = jax.vjp(jax.nn.gelu, y)
        dy = gelu_vjp(dy1)[0]
        dyc_ref[...] = dy
        du_ref[...] = dy * d_
        dd_part = jnp.sum(dy * u, axis=0, keepdims=True)
        dwg_part = lax.dot_general(y1.astype(BF16), dt, (((0,), (0,)), ((), ())), preferred_element_type=F32)

        @pl.when(i == 0)
        def _():
            dd_ref[...] = dd_part
            dwg_ref[...] = dwg_part

        @pl.when(i > 0)
        def _():
            dd_ref[...] += dd_part
            dwg_ref[...] += dwg_part

    row = pl.BlockSpec((tm, W), lambda i: (i, 0))
    return pl.pallas_call(
        body, name="s5_post_bwd", grid=(L // tm,),
        in_specs=[row, pl.BlockSpec((tm, W), lambda i: (i, OFF_US // W)), pl.BlockSpec((tm, W), lambda i: (i, OFF_ZS // W)),
                  pl.BlockSpec((1, W), lambda i: (0, 0)), pl.BlockSpec((W, W), lambda i: (0, 0)), row],
        out_specs=[row, row, row, pl.BlockSpec((1, W), lambda i: (0, 0)), pl.BlockSpec((W, W), lambda i: (0, 0))],
        out_shape=[_sds((L, W)), _sds((L, W)), _sds((L, W), BF16), _sds((1, W)), _sds((W, W))],
        compiler_params=_cp(("arbitrary",)),
    )(yc, proj, proj, d, wg, dout)


def _shift_down(x, s):
    if s == 0:
        return x
    rows = lax.broadcasted_iota(jnp.int32, x.shape, 0)
    return jnp.where(rows >= s, pltpu.roll(x, s, 0), 0.0)


def _shift_up(x, s):
    if s == 0:
        return x
    L = x.shape[0]
    rows = lax.broadcasted_iota(jnp.int32, x.shape, 0)
    return jnp.where(rows < L - s, pltpu.roll(x, L - s, 0), 0.0)


def _conv_pre(x, w):
    acc = w[CONV_K - 1:CONV_K, :] * x
    for s in range(1, CONV_K):
        acc = acc + w[CONV_K - 1 - s:CONV_K - s, :] * _shift_down(x, s)
    return acc


def _dn_conv_fwd(proj, conv_w):
    L = proj.shape[0]
    W = DN_HEAD_DIM
    nq = 2 * DN_HEADS

    def body(x_ref, w_ref, o_ref):
        j = pl.program_id(0)
        act = _silu(_conv_pre(x_ref[...], w_ref[...]))
        r = lax.rsqrt(jnp.sum(act * act, axis=-1, keepdims=True) + EPS)
        scale = jnp.where(j < DN_HEADS, DN_HEAD_DIM ** -0.5, 1.0)
        o_ref[...] = jnp.where(j < nq, act * r * scale, act)

    return pl.pallas_call(
        body, name="dn_conv_fwd", grid=(3 * DN_HEADS,),
        in_specs=[pl.BlockSpec((L, W), lambda j: (0, OFF_Q // W + j)), pl.BlockSpec((CONV_K, W), lambda j: (0, j))],
        out_specs=pl.BlockSpec((L, W), lambda j: (0, j)), out_shape=_sds((L, 3 * D_DN)), compiler_params=_cp(("parallel",)),
    )(proj, conv_w)


def _dn_conv_bwd(proj, conv_w, dout):
    L = proj.shape[0]
    W = DN_HEAD_DIM
    nq = 2 * DN_HEADS

    def body(x_ref, w_ref, do_ref, dx_ref, dw_ref):
        j = pl.program_id(0)
        x, w, dout_ = x_ref[...], w_ref[...], do_ref[...]
        pre = _conv_pre(x, w)
        act = _silu(pre)
        r = lax.rsqrt(jnp.sum(act * act, axis=-1, keepdims=True) + EPS)
        scale = jnp.where(j < DN_HEADS, DN_HEAD_DIM ** -0.5, 1.0)
        g = dout_ * scale
        dact_n = r * g - act * (r * r * r) * jnp.sum(g * act, axis=-1, keepdims=True)
        dact = jnp.where(j < nq, dact_n, dout_)
        dpre = dact * _dsilu(pre)
        dx = w[CONV_K - 1:CONV_K, :] * dpre
        for s in range(1, CONV_K):
            dx = dx + w[CONV_K - 1 - s:CONV_K - s, :] * _shift_up(dpre, s)
        dx_ref[...] = dx.astype(BF16)
        for s in range(CONV_K):
            dw_ref[pl.ds(CONV_K - 1 - s, 1), :] = jnp.sum(dpre * _shift_down(x, s), axis=0, keepdims=True)

    col = pl.BlockSpec((L, W), lambda j: (0, j))
    wsp = pl.BlockSpec((CONV_K, W), lambda j: (0, j))
    return pl.pallas_call(
        body, name="dn_conv_bwd", grid=(3 * DN_HEADS,),
        in_specs=[pl.BlockSpec((L, W), lambda j: (0, OFF_Q // W + j)), wsp, col], out_specs=[col, wsp],
        out_shape=[_sds((L, 3 * D_DN), BF16), _sds((CONV_K, 3 * D_DN))], compiler_params=_cp(("parallel",)),
    )(proj, conv_w, dout)


def _softplus(x):
    return jnp.maximum(x, 0.0) + jnp.log(1.0 + jnp.exp(-jnp.abs(x)))


def _dn_gates_fwd(proj, alog, dtb):
    L = proj.shape[0]
    W = 128

    def body(p_ref, al_ref, db_ref, o_ref):
        p = p_ref[...]
        lane = lax.broadcasted_iota(jnp.int32, p.shape, 1)
        g = -jnp.exp(al_ref[...]) * _softplus(p + db_ref[...])
        o_ref[...] = jnp.where(lane < DN_HEADS, _sigmoid(p), jnp.where(lane < 2 * DN_HEADS, g, 0.0))

    return pl.pallas_call(
        body, name="dn_gates_fwd", grid=(1,),
        in_specs=[pl.BlockSpec((L, W), lambda i: (0, OFF_B // W)), pl.BlockSpec((1, W), lambda i: (0, 0)),
                  pl.BlockSpec((1, W), lambda i: (0, 0))],
        out_specs=pl.BlockSpec((L, W), lambda i: (0, 0)), out_shape=_sds((L, W)), compiler_params=_cp(("arbitrary",)),
    )(proj, alog, dtb)


def _dn_gates_bwd(proj, alog, dtb, dgates):
    L = proj.shape[0]
    W = 128

    def body(p_ref, al_ref, db_ref, dg_ref, dp_ref, dal_ref, ddb_ref):
        p, dg = p_ref[...], dg_ref[...]
        lane = lax.broadcasted_iota(jnp.int32, p.shape, 1)
        is_g = jnp.logical_and(lane >= DN_HEADS, lane < 2 * DN_HEADS)
        beta = _sigmoid(p)
        na = -jnp.exp(al_ref[...])
        xs = p + db_ref[...]
        dsp = dg * na * _sigmoid(xs)
        dp_ref[...] = jnp.where(lane < DN_HEADS, dg * beta * (1.0 - beta), jnp.where(is_g, dsp, 0.0)).astype(BF16)
        dal_ref[...] = jnp.sum(jnp.where(is_g, dg * na * _softplus(xs), 0.0), axis=0, keepdims=True)
        ddb_ref[...] = jnp.sum(jnp.where(is_g, dsp, 0.0), axis=0, keepdims=True)

    one = pl.BlockSpec((1, W), lambda i: (0, 0))
    full = pl.BlockSpec((L, W), lambda i: (0, 0))
    return pl.pallas_call(
        body, name="dn_gates_bwd", grid=(1,),
        in_specs=[pl.BlockSpec((L, W), lambda i: (0, OFF_B // W)), one, one, full], out_specs=[full, one, one],
        out_shape=[_sds((L, W), BF16), _sds((1, W)), _sds((1, W))], compiler_params=_cp(("arbitrary",)),
    )(proj, alog, dtb, dgates)


def _bdot(a, b, dims):
    return lax.dot_general(a.astype(BF16), b.astype(BF16), (dims, ((), ())), preferred_element_type=F32)


_NN, _NT, _TN = ((1,), (0,)), ((1,), (1,)), ((0,), (0,))


def _dot3(a, b, dims):
    ah, bh = a.astype(BF16), b.astype(BF16)
    al, bl = (a - ah.astype(F32)).astype(BF16), (b - bh.astype(F32)).astype(BF16)
    (ca,), (cb,) = dims
    a3 = jnp.concatenate([ah, ah, al], axis=ca)
    b3 = jnp.concatenate([bh, bl, bh], axis=cb)
    return lax.dot_general(a3, b3, (dims, ((), ())), preferred_element_type=F32)


def _mm_family(raw):
    nn = jax.custom_vjp(lambda a, b: raw(a, b, _NN))
    nt = jax.custom_vjp(lambda a, b: raw(a, b, _NT))
    tn = jax.custom_vjp(lambda a, b: raw(a, b, _TN))
    nn.defvjp(lambda a, b: (raw(a, b, _NN), (a, b)), lambda r, g: (raw(g, r[1], _NT), raw(r[0], g, _TN)))
    nt.defvjp(lambda a, b: (raw(a, b, _NT), (a, b)), lambda r, g: (raw(g, r[1], _NN), raw(g, r[0], _TN)))
    tn.defvjp(lambda a, b: (raw(a, b, _TN), (a, b)), lambda r, g: (raw(r[1], g, _NT), raw(r[0], g, _NN)))
    return nn, nt, tn


_mm_nn, _mm_nt, _mm_tn = _mm_family(_bdot)
_m3_nn, _m3_nt, _m3_tn = _mm_family(_dot3)


def _tri_apply(x, upper):
    C = x.shape[0]
    ii = lax.broadcasted_iota(jnp.int32, (C, 3 * C), 0)
    jj = lax.broadcasted_iota(jnp.int32, (C, 3 * C), 1) % C
    mat = ((ii <= jj) if upper else (ii >= jj)).astype(BF16)
    hi = x.astype(BF16)
    r = x - hi.astype(F32)
    mid = r.astype(BF16)
    lo = (r - mid.astype(F32)).astype(BF16)
    return jnp.dot(mat, jnp.concatenate([hi, mid, lo], axis=0), preferred_element_type=F32)


_cumsum_rows = jax.custom_vjp(lambda x: _tri_apply(x, False))
_cumsum_rows.defvjp(lambda x: (_tri_apply(x, False), None), lambda _, g: (_tri_apply(g, True),))


def _uli(a_s):
    C = a_s[0].shape[0]
    ii = lax.broadcasted_iota(jnp.int32, (C, C), 0)
    jj = lax.broadcasted_iota(jnp.int32, (C, C), 1)
    eye = jnp.where(ii == jj, 1.0, 0.0)
    ts = [eye - a for a in a_s]
    ms = list(a_s)
    for _ in range(int(math.log2(C)) - 1):
        ms = [_dot3(m, m, _NN) for m in ms]
        ts = [t + _dot3(t, m, _NN) for t, m in zip(ts, ms)]
    return tuple(ts)


def _uli_bwd(ts, gs):
    xs = [_dot3(t, g, _TN) for t, g in zip(ts, gs)]
    return (tuple(-_dot3(x, t, _NT) for x, t in zip(xs, ts)),)


_unit_lower_inverse = jax.custom_vjp(_uli)
_unit_lower_inverse.defvjp(lambda a_s: (lambda ts: (ts, ts))(_uli(a_s)), _uli_bwd)


def _prep_math(qs, ks, vs, gcols, bcols):
    n = len(qs)
    C, dv = vs[0].shape
    ii = lax.broadcasted_iota(jnp.int32, (C, C), 0)
    jj = lax.broadcasted_iota(jnp.int32, (C, C), 1)
    causal = ii >= jj
    strict = ii > jj
    sf = strict.astype(F32)
    ones = jnp.ones((C, dv), F32)
    dms = [_cumsum_rows(g * sf) for g in gcols]
    gcbs = [_cumsum_rows(g * ones) for g in gcols]
    kks = [_mm_nt(k, k) for k in ks]
    qks = [_mm_nt(q, k) for q, k in zip(qs, ks)]
    decays = [jnp.where(causal, jnp.exp(jnp.where(causal, dm, 0.0)), 0.0) for dm in dms]
    glasts = [jnp.sum(g * ones, axis=0, keepdims=True) for g in gcols]
    egs = [jnp.exp(gcb) for gcb in gcbs]
    ts = _unit_lower_inverse(tuple(jnp.where(strict, b * kk * dc, 0.0) for b, kk, dc in zip(bcols, kks, decays)))
    us = [_m3_nn(t, v * b) for t, v, b in zip(ts, vs, bcols)]
    ws = [_m3_nn(t, k * b * eg) for t, k, b, eg in zip(ts, ks, bcols, egs)]
    return tuple((us[i], ws[i], qs[i] * egs[i], ks[i] * jnp.exp(glasts[i] - gcbs[i]), qks[i] * decays[i],
                  jnp.exp(glasts[i])) for i in range(n))


def _gate_cols(gates, h):
    lane = lax.broadcasted_iota(jnp.int32, gates.shape, 1)
    bcol = jnp.sum(jnp.where(lane == h, gates, 0.0), axis=1, keepdims=True)
    gcol = jnp.sum(jnp.where(lane == h + DN_HEADS, gates, 0.0), axis=1, keepdims=True)
    return gcol, bcol


DN_HB = 8


def _dn_prep_fwd(qkv, gates):
    L = qkv.shape[0]
    N, H, d, HB = L // CHUNK, DN_HEADS, DN_HEAD_DIM, DN_HB

    def body(q_ref, k_ref, v_ref, g_ref, u_ref, w_ref, qd_ref, kd_ref, qk_ref, egl_ref):
        h0 = pl.program_id(1) * HB
        gates_ = g_ref[...]
        lanes_of = [pl.ds(i * d, d) for i in range(HB)]
        cols = [_gate_cols(gates_, h0 + i) for i in range(HB)]
        outs = _prep_math([q_ref[:, l] for l in lanes_of], [k_ref[:, l] for l in lanes_of], [v_ref[:, l] for l in lanes_of],
                          [c[0] for c in cols], [c[1] for c in cols])
        for i in range(HB):
            lanes = lanes_of[i]
            u, w, qd, kd, qk, egl = outs[i]
            u_ref[:, lanes] = u
            w_ref[:, lanes] = w
            qd_ref[:, lanes] = qd
            kd_ref[:, lanes] = kd
            qk_ref[0, i] = qk
            egl_ref[0, i] = jnp.broadcast_to(egl, (8, d))

    blk = lambda off: pl.BlockSpec((CHUNK, HB * d), lambda n, j: (n, off // HB + j))
    cc = pl.BlockSpec((1, HB, CHUNK, CHUNK), lambda n, j: (n, j, 0, 0))
    ee = pl.BlockSpec((1, HB, 8, d), lambda n, j: (n, j, 0, 0))
    big = _sds((L, D_DN))
    return pl.pallas_call(
        body, name="dn_prep_fwd", grid=(N, H // HB),
        in_specs=[blk(0), blk(H), blk(2 * H), pl.BlockSpec((CHUNK, 128), lambda n, j: (n, 0))],
        out_specs=[blk(0), blk(0), blk(0), blk(0), cc, ee],
        out_shape=[big, big, big, big, _sds((N, H, CHUNK, CHUNK)), _sds((N, H, 8, d))],
        compiler_params=_cp(("parallel", "parallel")),
    )(qkv, qkv, qkv, gates)


def _dn_scan_fwd(u, w, qd, kd, qk, egl):
    L = u.shape[0]
    N, H, d, HB = L // CHUNK, DN_HEADS, DN_HEAD_DIM, DN_HB

    def body(u_ref, w_ref, qd_ref, kd_ref, qk_ref, egl_ref, o_ref, st_ref, s_ref):
        n, h0 = pl.program_id(0), pl.program_id(1) * HB

        @pl.when(n == 0)
        def _():
            for i in range(HB):
                s_ref[h0 + i] = jnp.zeros((d, d), F32)

        hs = range(HB)
        ln = [pl.ds(i * d, d) for i in hs]
        st = [s_ref[h0 + i] for i in hs]
        ws = [_bdot(w_ref[:, ln[i]], st[i], _NN) for i in hs]
        qs = [_bdot(qd_ref[:, ln[i]], st[i], _NN) for i in hs]
        vn = [u_ref[:, ln[i]] - ws[i] for i in hs]
        qv = [_bdot(qk_ref[0, i], vn[i], _NN) for i in hs]
        kv = [_bdot(kd_ref[:, ln[i]], vn[i], _TN) for i in hs]
        for i in hs:
            st_ref[0, i] = st[i]
            o_ref[:, ln[i]] = qs[i] + qv[i]
            s_ref[h0 + i] = st[i] * egl_ref[0, i, pl.ds(0, 1), :] + kv[i]

    blk = pl.BlockSpec((CHUNK, HB * d), lambda n, j: (n, j))
    cc = pl.BlockSpec((1, HB, CHUNK, CHUNK), lambda n, j: (n, j, 0, 0))
    ee = pl.BlockSpec((1, HB, 8, d), lambda n, j: (n, j, 0, 0))
    return pl.pallas_call(
        body, name="dn_scan_fwd", grid=(N, H // HB), in_specs=[blk, blk, blk, blk, cc, ee],
        out_specs=[blk, pl.BlockSpec((1, HB, d, d), lambda n, j: (n, j, 0, 0))],
        out_shape=[_sds((L, D_DN)), _sds((N, H, d, d))], scratch_shapes=[pltpu.VMEM((H, d, d), F32)],
        compiler_params=_cp(("arbitrary", "arbitrary")),
    )(u, w, qd, kd, qk, egl)


def _dn_scan_bwd(u, w, qd, kd, qk, egl, states, do):
    L = u.shape[0]
    N, H, d, HB = L // CHUNK, DN_HEADS, DN_HEAD_DIM, DN_HB

    def body(u_ref, w_ref, qd_ref, kd_ref, qk_ref, egl_ref, st_ref, do_ref,
             du_ref, dw_ref, dqd_ref, dkd_ref, dqk_ref, degl_ref, ds_ref):
        n, h0 = pl.program_id(0), pl.program_id(1) * HB

        @pl.when(n == 0)
        def _():
            for i in range(HB):
                ds_ref[h0 + i] = jnp.zeros((d, d), F32)

        hs = range(HB)
        ln = [pl.ds(i * d, d) for i in hs]
        st = [st_ref[0, i] for i in hs]
        dsn = [ds_ref[h0 + i] for i in hs]
        do_ = [do_ref[:, ln[i]] for i in hs]
        ws = [_bdot(w_ref[:, ln[i]], st[i], _NN) for i in hs]
        d1 = [_bdot(qk_ref[0, i], do_[i], _TN) for i in hs]
        d2 = [_bdot(kd_ref[:, ln[i]], dsn[i], _NN) for i in hs]
        dqd = [_bdot(do_[i], st[i], _NT) for i in hs]
        qdo = [_bdot(qd_ref[:, ln[i]], do_[i], _TN) for i in hs]
        vn = [u_ref[:, ln[i]] - ws[i] for i in hs]
        dvn = [d1[i] + d2[i] for i in hs]
        dw = [_bdot(dvn[i], st[i], _NT) for i in hs]
        dkd = [_bdot(vn[i], dsn[i], _NT) for i in hs]
        dqk = [_bdot(do_[i], vn[i], _NT) for i in hs]
        wdv = [_bdot(w_ref[:, ln[i]], dvn[i], _TN) for i in hs]
        for i in hs:
            du_ref[:, ln[i]] = dvn[i]
            dw_ref[:, ln[i]] = -dw[i]
            dqd_ref[:, ln[i]] = dqd[i]
            dkd_ref[:, ln[i]] = dkd[i]
            dqk_ref[0, i] = dqk[i]
            degl_ref[0, i] = jnp.broadcast_to(jnp.sum(dsn[i] * st[i], keepdims=True), (8, d))
            ds_ref[h0 + i] = (qdo[i] - wdv[i]) + dsn[i] * egl_ref[0, i, pl.ds(0, 1), :]

    blk = pl.BlockSpec((CHUNK, HB * d), lambda n, j: (N - 1 - n, j))
    cc = pl.BlockSpec((1, HB, CHUNK, CHUNK), lambda n, j: (N - 1 - n, j, 0, 0))
    ee = pl.BlockSpec((1, HB, 8, d), lambda n, j: (N - 1 - n, j, 0, 0))
    ss = pl.BlockSpec((1, HB, d, d), lambda n, j: (N - 1 - n, j, 0, 0))
    big = _sds((L, D_DN))
    return pl.pallas_call(
        body, name="dn_scan_bwd", grid=(N, H // HB), in_specs=[blk, blk, blk, blk, cc, ee, ss, blk],
        out_specs=[blk, blk, blk, blk, cc, ee],
        out_shape=[big, big, big, big, _sds((N, H, CHUNK, CHUNK)), _sds((N, H, 8, d))],
        scratch_shapes=[pltpu.VMEM((H, d, d), F32)], compiler_params=_cp(("arbitrary", "arbitrary")),
    )(u, w, qd, kd, qk, egl, states, do)


def _dn_prep_bwd(qkv, gates, du, dw, dqd, dkd, dqk, degl):
    L = qkv.shape[0]
    N, H, d, HB = L // CHUNK, DN_HEADS, DN_HEAD_DIM, DN_HB
    assert HB == H

    def body(q_ref, k_ref, v_ref, g_ref, du_ref, dw_ref, dqd_ref, dkd_ref, dqk_ref, degl_ref, dqkv_ref, dg_ref):
        j = pl.program_id(1)
        h0 = j * HB
        gates_ = g_ref[...]
        lane = lax.broadcasted_iota(jnp.int32, gates_.shape, 1)
        lane1 = lax.broadcasted_iota(jnp.int32, (1, d), 1)
        part = jnp.zeros(gates_.shape, F32)
        lanes_of = [pl.ds(i * d, d) for i in range(HB)]
        cols = [_gate_cols(gates_, h0 + i) for i in range(HB)]
        _, f = jax.vjp(_prep_math, [q_ref[:, l] for l in lanes_of], [k_ref[:, l] for l in lanes_of],
                       [v_ref[:, l] for l in lanes_of], [c[0] for c in cols], [c[1] for c in cols])
        cots = tuple((du_ref[:, l], dw_ref[:, l], dqd_ref[:, l], dkd_ref[:, l], dqk_ref[0, i],
                      jnp.where(lane1 == 0, degl_ref[0, i, pl.ds(0, 1), :], 0.0)) for i, l in enumerate(lanes_of))
        dqs, dks, dvs, dgcs, dbcs = f(cots)
        for i in range(HB):
            for s, val in enumerate((dqs[i], dks[i], dvs[i])):
                dqkv_ref[:, pl.ds((s * H + i) * d, d)] = val
            part = part + jnp.where(lane == h0 + i, dbcs[i], 0.0) + jnp.where(lane == h0 + i + DN_HEADS, dgcs[i], 0.0)

        @pl.when(j == 0)
        def _():
            dg_ref[...] = part

        @pl.when(j > 0)
        def _():
            dg_ref[...] += part

    blk = lambda off: pl.BlockSpec((CHUNK, HB * d), lambda n, j: (n, off // HB + j))
    gsp = pl.BlockSpec((CHUNK, 128), lambda n, j: (n, 0))
    cc = pl.BlockSpec((1, HB, CHUNK, CHUNK), lambda n, j: (n, j, 0, 0))
    ee = pl.BlockSpec((1, HB, 8, d), lambda n, j: (n, j, 0, 0))
    return pl.pallas_call(
        body, name="dn_prep_bwd", grid=(N, H // HB),
        in_specs=[blk(0), blk(H), blk(2 * H), gsp, blk(0), blk(0), blk(0), blk(0), cc, ee],
        out_specs=[pl.BlockSpec((CHUNK, 3 * H * d), lambda n, j: (n, 0)), gsp], out_shape=[_sds((L, 3 * D_DN)), _sds((L, 128))],
        compiler_params=_cp(("parallel", "arbitrary")),
    )(qkv, qkv, qkv, gates, du, dw, dqd, dkd, dqk, degl)


def _dn_post_fwd(o, proj, nw):
    L = o.shape[0]
    d = DN_HEAD_DIM
    tm = min(512, L)

    def body(o_ref, z_ref, w_ref, y_ref):
        ov = o_ref[...]
        r = lax.rsqrt(jnp.mean(ov * ov, axis=-1, keepdims=True) + EPS)
        y_ref[...] = (ov * r * w_ref[...] * _silu(z_ref[...])).astype(BF16)

    blk = pl.BlockSpec((tm, d), lambda i, h: (i, h))
    return pl.pallas_call(
        body, name="dn_post_fwd", grid=(L // tm, DN_HEADS),
        in_specs=[blk, pl.BlockSpec((tm, d), lambda i, h: (i, OFF_ZD // d + h)), pl.BlockSpec((1, d), lambda i, h: (0, 0))],
        out_specs=blk, out_shape=_sds((L, D_DN), BF16), compiler_params=_cp(("parallel", "parallel")),
    )(o, proj, nw)


def _dn_post_bwd(o, proj, nw, dy):
    L = o.shape[0]
    d = DN_HEAD_DIM
    tm = min(512, L)

    def body(o_ref, z_ref, w_ref, dy_ref, do_ref, dz_ref, dw_ref):
        first = jnp.logical_and(pl.program_id(0) == 0, pl.program_id(1) == 0)
        ov, z, w, dyv = o_ref[...], z_ref[...], w_ref[...], dy_ref[...]
        r = lax.rsqrt(jnp.mean(ov * ov, axis=-1, keepdims=True) + EPS)
        xn = ov * r
        dz_ref[...] = (dyv * xn * w * _dsilu(z)).astype(BF16)
        dn = dyv * _silu(z)
        t = dn * w
        do_ref[...] = r * t - ov * (r * r * r) * jnp.mean(t * ov, axis=-1, keepdims=True)
        part = jnp.sum(dn * xn, axis=0, keepdims=True)

        @pl.when(first)
        def _():
            dw_ref[...] = part

        @pl.when(jnp.logical_not(first))
        def _():
            dw_ref[...] += part

    blk = pl.BlockSpec((tm, d), lambda i, h: (i, h))
    one = pl.BlockSpec((1, d), lambda i, h: (0, 0))
    return pl.pallas_call(
        body, name="dn_post_bwd", grid=(L // tm, DN_HEADS),
        in_specs=[blk, pl.BlockSpec((tm, d), lambda i, h: (i, OFF_ZD // d + h)), one, blk], out_specs=[blk, blk, one],
        out_shape=[_sds((L, D_DN)), _sds((L, D_DN), BF16), _sds((1, d))], compiler_params=_cp(("arbitrary", "arbitrary")),
    )(o, proj, nw, dy)


def _mix_fwd(s5o, dno, w_su, w_du, proj):
    L, K = s5o.shape
    N = w_su.shape[1]
    tm, tn = min(512, L), 512

    def body(a1, a2, b1, b2, gs, gd, ys_ref, yd_ref, mx_ref):
        ys = jnp.dot(a1[...], b1[...], preferred_element_type=F32)
        yd = jnp.dot(a2[...], b2[...], preferred_element_type=F32)
        ys_ref[...] = ys
        yd_ref[...] = yd
        mx_ref[...] = (_sigmoid(gs[...]) * ys + _sigmoid(gd[...]) * yd).astype(BF16)

    a = pl.BlockSpec((tm, K), lambda i, j: (i, 0))
    b = pl.BlockSpec((K, tn), lambda i, j: (0, j))
    o = pl.BlockSpec((tm, tn), lambda i, j: (i, j))
    return pl.pallas_call(
        body, name="mix_fwd", grid=(L // tm, N // tn),
        in_specs=[a, a, b, b, pl.BlockSpec((tm, tn), lambda i, j: (i, OFF_GS // tn + j)),
                  pl.BlockSpec((tm, tn), lambda i, j: (i, OFF_GD // tn + j))],
        out_specs=[o, o, o], out_shape=[_sds((L, N)), _sds((L, N)), _sds((L, N), BF16)],
        compiler_params=_cp(("parallel", "parallel")),
    )(s5o, dno, w_su, w_du, proj, proj)


def _mix_bwd(dx2b, w_out, proj, ys, yd):
    L, K = dx2b.shape
    N = w_out.shape[0]
    tm, tn = min(512, L), 512

    def body(a, b, gs, gd, ys_ref, yd_ref, dgs_ref, dgd_ref, dys_ref, dyd_ref):
        dm = lax.dot_general(a[...], b[...], (((1,), (1,)), ((), ())), preferred_element_type=F32)
        ss, sd = _sigmoid(gs[...]), _sigmoid(gd[...])
        dys_ref[...] = (dm * ss).astype(BF16)
        dyd_ref[...] = (dm * sd).astype(BF16)
        dgs_ref[...] = (dm * ys_ref[...] * ss * (1.0 - ss)).astype(BF16)
        dgd_ref[...] = (dm * yd_ref[...] * sd * (1.0 - sd)).astype(BF16)

    o = pl.BlockSpec((tm, tn), lambda i, j: (i, j))
    return pl.pallas_call(
        body, name="mix_bwd", grid=(L // tm, N // tn),
        in_specs=[pl.BlockSpec((tm, K), lambda i, j: (i, 0)), pl.BlockSpec((tn, K), lambda i, j: (j, 0)),
                  pl.BlockSpec((tm, tn), lambda i, j: (i, OFF_GS // tn + j)),
                  pl.BlockSpec((tm, tn), lambda i, j: (i, OFF_GD // tn + j)), o, o],
        out_specs=[o, o, o, o], out_shape=[_sds((L, N), BF16)] * 4, compiler_params=_cp(("parallel", "parallel")),
    )(dx2b, w_out, proj, proj, ys, yd)


def _final(mixed, w_out, x, tgt, fw):
    L, D = x.shape
    tm = min(256, L)

    def body(a_ref, b_ref, x_ref, t_ref, w_ref, dx_ref, dxb_ref, loss_ref, dw_ref):
        i = pl.program_id(0)
        x2 = x_ref[...] + jnp.dot(a_ref[...], b_ref[...], preferred_element_type=F32)
        w = w_ref[...]
        r = lax.rsqrt(jnp.mean(x2 * x2, axis=-1, keepdims=True) + EPS)
        xn = x2 * r
        e = xn * w - t_ref[...]
        lpart = 0.5 * jnp.sum(jnp.mean(e * e, axis=-1, keepdims=True), axis=0, keepdims=True)
        dy = e * (1.0 / D)
        t = dy * w
        dx2 = r * t - x2 * (r * r * r) * jnp.mean(t * x2, axis=-1, keepdims=True)
        dx_ref[...] = dx2
        dxb_ref[...] = dx2.astype(BF16)
        dwp = jnp.sum(dy * xn, axis=0, keepdims=True)
        lrow = jnp.broadcast_to(lpart, loss_ref.shape)

        @pl.when(i == 0)
        def _():
            loss_ref[...] = lrow
            dw_ref[...] = dwp

        @pl.when(i > 0)
        def _():
            loss_ref[...] += lrow
            dw_ref[...] += dwp

    row = pl.BlockSpec((tm, D), lambda i: (i, 0))
    one = pl.BlockSpec((1, D), lambda i: (0, 0))
    return pl.pallas_call(
        body, name="final", grid=(L // tm,),
        in_specs=[row, pl.BlockSpec((D, D), lambda i: (0, 0)), row, row, one],
        out_specs=[row, row, pl.BlockSpec((1, 128), lambda i: (0, 0)), one],
        out_shape=[_sds((L, D)), _sds((L, D), BF16), _sds((1, 128)), _sds((1, D))], compiler_params=_cp(("arbitrary",)),
    )(mixed, w_out, x, tgt, fw)


def _block_diag(t):
    J, g, a, b = t.shape
    eye = jnp.eye(g, dtype=t.dtype)
    return (t[:, :, :, None, :] * eye[None, :, None, :, None]).reshape(J, g * a, g * b)


def _block_diag_take(m, g):
    J, ga, gb = m.shape
    a, b = ga // g, gb // g
    m5 = m.reshape(J, g, a, g, b)
    idx = jnp.arange(g)
    return m5[:, idx, :, idx, :].transpose(1, 0, 2, 3)


class _PlainOps:
    def __init__(self, w_rest):
        self.w_rest = w_rest

    def in_proj(self, h, wt_perm):
        return _mm(h, wt_perm, tb=True, name="in_proj", tm=1024, tn=1152), self.w_rest

    def rest_grads(self, grads, twins):
        pass

    def d_w_in(self, h, dproj):
        return _mm(dproj, h, ta=True, name="d_w_in", tm=1152, tn=1024, twin=True)

    def d_h(self, dproj, wt_perm, d_wt_perm, d_wt_twin):
        return _mm(dproj, wt_perm, name="d_h", tm=2048, tn=1024, tk=1152)


def _local_step(x, tgt, ln_w, w_perm, lam_re, lam_im, log_step, b_re, b_im, c_re, c_im, s5_d,
                conv_w, a_log, dt_bias, norm_w, fw, ops):
    G, P, gb = S5_GROUPS, S5_STATE, S5_GROUPS // S5_BLOCKS
    h, rstd = _ln_fwd(x, ln_w)
    proj, (w_glu, w_su, w_du, w_out) = ops.in_proj(h, w_perm)

    b_re2, b_im2 = b_re.reshape(G, P * S5_GROUP), b_im.reshape(G, P * S5_GROUP)
    ls2 = log_step.reshape(G, 1)
    abar_re, abar_im, bb_re, bb_im = _s5_param_fwd(lam_re, lam_im, ls2, b_re2, b_im2)

    def to_wb(bb):
        return _block_diag(bb.reshape(S5_BLOCKS, gb, P, S5_GROUP).transpose(0, 1, 3, 2)).astype(BF16)

    def to_cb(cc):
        return _block_diag(cc.reshape(S5_BLOCKS, gb, S5_GROUP, P).transpose(0, 1, 3, 2)).astype(BF16)

    wbr, wbi, cbr, cbi = to_wb(bb_re), to_wb(bb_im), to_cb(c_re), to_cb(c_im)
    a_re_row, a_im_row = abar_re.reshape(1, G * P), abar_im.reshape(1, G * P)
    yc = _s5_core_fwd(proj, wbr, wbi, a_re_row, a_im_row, cbr, cbi)
    s5o = _s5_post_fwd(yc, proj, s5_d, w_glu)

    pad = lambda v: jnp.pad(v, ((0, 0), (DN_HEADS, 128 - 2 * DN_HEADS)))
    alog_row, dtb_row = pad(a_log), pad(dt_bias)
    qkv = _dn_conv_fwd(proj, conv_w)
    gates = _dn_gates_fwd(proj, alog_row, dtb_row)
    prep = _dn_prep_fwd(qkv, gates)
    o_dn, states = _dn_scan_fwd(*prep)
    dno = _dn_post_fwd(o_dn, proj, norm_w)

    ys, yd, mixed = _mix_fwd(s5o, dno, w_su, w_du, proj)
    dx2, dx2b, loss_row, d_fw = _final(mixed, w_out, x, tgt, fw)
    d_w_out, d_w_out_b = _mm(mixed, dx2b, ta=True, name="d_w_out", twin=True)
    dgs, dgd, dys, dyd = _mix_bwd(dx2b, w_out, proj, ys, yd)
    d_w_su, d_w_su_b = _mm(s5o, dys, ta=True, name="d_w_su", shard_out=True, twin=True)
    d_w_du, d_w_du_b = _mm(dno, dyd, ta=True, name="d_w_du", shard_out=True, twin=True)
    ds5o = _mm(dys, w_su, tb=True, name="d_s5o")
    ddno = _mm(dyd, w_du, tb=True, name="d_dno")

    dyc, du1, dz_s, d_s5d, d_w_glu = _s5_post_bwd(yc, proj, s5_d, w_glu, ds5o)
    ops.rest_grads((d_w_glu, d_w_su, d_w_du, d_w_out), (d_w_glu.astype(BF16), d_w_su_b, d_w_du_b, d_w_out_b))
    du, dwbr, dwbi, dcbr, dcbi, dar, dai = _s5_core_bwd(proj, wbr, wbi, a_re_row, a_im_row, cbr, cbi, dyc, du1)

    def from_wb(dwb):
        return _block_diag_take(dwb, gb).transpose(0, 1, 3, 2).reshape(G, P * S5_GROUP)

    def from_cb(dcb):
        return _block_diag_take(dcb, gb).transpose(0, 1, 3, 2).reshape(G, S5_GROUP, P)

    d_lam_re, d_lam_im, d_ls, d_b_re, d_b_im = _s5_param_bwd(
        lam_re, lam_im, ls2, b_re2, b_im2, dar.reshape(G, P), dai.reshape(G, P), from_wb(dwbr), from_wb(dwbi))

    do_dn, dz_d, d_norm_w = _dn_post_bwd(o_dn, proj, norm_w, ddno)
    dqkv_act, dgates = _dn_prep_bwd(qkv, gates, *_dn_scan_bwd(*prep, states, do_dn))
    dqkv, d_conv = _dn_conv_bwd(proj, conv_w, dqkv_act)
    dpb, d_alog_row, d_dtb_row = _dn_gates_bwd(proj, alog_row, dtb_row, dgates)

    dproj = jnp.concatenate([du, dz_s, dqkv, dz_d, dgs, dgd, dpb], axis=1)
    d_w_perm, d_w_twin = ops.d_w_in(h, dproj)
    dh = ops.d_h(dproj, w_perm, d_w_perm, d_w_twin)
    grad_x, d_ln_w = _ln_bwd(x, rstd, ln_w, dh, dx2)

    grads = dict(
        ln_w=d_ln_w, w_perm=d_w_perm, s5_lam_re=d_lam_re, s5_lam_im=d_lam_im, s5_log_step=d_ls.reshape(1, G),
        s5_b_re=d_b_re.reshape(G, P, S5_GROUP), s5_b_im=d_b_im.reshape(G, P, S5_GROUP),
        s5_c_re=from_cb(dcbr), s5_c_im=from_cb(dcbi), s5_d=d_s5d, s5_w_glu=d_w_glu, s5_w_up=d_w_su,
        dn_conv_w=d_conv, dn_a_log=d_alog_row[:, DN_HEADS:2 * DN_HEADS], dn_dt_bias=d_dtb_row[:, DN_HEADS:2 * DN_HEADS],
        dn_norm_w=d_norm_w, dn_w_up=d_w_du, w_out=d_w_out, final_norm_w=d_fw)
    return loss_row, grad_x, grads


def _place():
    x, y, c = lax.axis_index("x"), lax.axis_index("y"), lax.axis_index("c")
    return x, y, c


def _remote(src, dst, send_sem, recv_sem, to):
    return pltpu.make_async_remote_copy(src_ref=src, dst_ref=dst, send_sem=send_sem, recv_sem=recv_sem,
                                        device_id=to, device_id_type=MESH)


def _gather_exchange(shards, whole=()):
    na, nw = len(shards), len(whole)

    def half_of(ref, a, half):
        rows = shards[a].shape[0]
        return ref.at[pl.ds(half * (rows // 2), rows // 2)]

    def plan(ins, outs, send_sems, recv_sems, local_sems, receiving):
        x, y, c = _place()
        me = 2 * x + y
        sibling = (x, y, 1 - c)
        chips = [(1 - x, y), (x, 1 - y), (1 - x, 1 - y)]

        def part(a, chip, half):
            return half_of(outs[a].at[chip], a, half)

        own = [pltpu.make_async_copy(ins[a], outs[a].at[me], local_sems.at[a]) for a in range(na + nw)]
        sends, landed, passed, arrivals = [], [], [], []
        for a in range(na):
            for j, (px, py) in enumerate(chips):
                k = 6 * a + j
                sends.append(_remote(half_of(ins[a], a, c), part(a, me, c), send_sems.at[k], recv_sems.at[k], (px, py, c)))
                if receiving:
                    got, other = part(a, 2 * px + py, c), part(a, 2 * px + py, 1 - c)
                    landed.append(_remote(got, got, send_sems.at[k], recv_sems.at[k], (px, py, c)))
                    passed.append(_remote(got, got, send_sems.at[k + 3], recv_sems.at[k + 3], sibling))
                    arrivals.append(_remote(other, other, send_sems.at[k + 3], recv_sems.at[k + 3], sibling))
        for a in range(na, na + nw):
            for j, (px, py) in enumerate(chips):
                k = 6 * na + 3 * (a - na) + j
                sends.append(_remote(ins[a], outs[a].at[me], send_sems.at[k], recv_sems.at[k], (px, py, c)))
                if receiving:
                    arrivals.append(_remote(ins[a], outs[a].at[2 * px + py], send_sems.at[k], recv_sems.at[k], (px, py, c)))
        return own, sends, landed, passed, arrivals

    def start(ins, outs, *sems):
        own, sends, _, _, _ = plan(ins, outs, *sems, False)
        for cp in own + sends:
            cp.start()

    def finish(ins, outs, *sems):
        own, sends, landed, passed, arrivals = plan(ins, outs, *sems, True)
        for got, fwd in zip(landed, passed):
            got.wait_recv()
            fwd.start()
        for cp in arrivals:
            cp.wait_recv()
        for cp in sends + passed:
            cp.wait_send()
        for cp in own:
            cp.wait()

    arrays = list(shards) + list(whole)
    return _Exchange(arrays, [_sds((N_CHIPS,) + s.shape, s.dtype) for s in arrays], 6 * na + 3 * nw, start, finish)


def _gather_relayed(shard, whole, name):
    rows, cols = shard.shape
    nw = len(whole)

    def body(*refs):
        in_ref, w_in = refs[0], refs[1:1 + nw]
        out_ref, w_out = refs[1 + nw], refs[2 + nw:2 + 2 * nw]
        send_sems, recv_sems, local_sems = refs[2 + 2 * nw:]
        x, y, c = _place()
        me = 2 * x + y
        near = (jnp.where(c == 1, 1 - x, x), jnp.where(c == 1, y, 1 - y))
        far = (jnp.where(c == 1, x, 1 - x), jnp.where(c == 1, 1 - y, y))
        diag = (1 - x, 1 - y)
        sibling = (x, y, 1 - c)
        chip_of = lambda p: 2 * p[0] + p[1]

        def half(ref, h):
            return ref.at[pl.ds(0, rows), pl.ds(h * (cols // 2), cols // 2)]

        own = [pltpu.make_async_copy(in_ref, out_ref.at[me], local_sems.at[0])]
        own += [pltpu.make_async_copy(w_in[a], w_out[a].at[me], local_sems.at[1 + a]) for a in range(nw)]
        others = [(1 - x, y), (x, 1 - y), (1 - x, 1 - y)]
        small = [_remote(w_in[a], w_out[a].at[me], send_sems.at[4 + 3 * a + j], recv_sems.at[4 + 3 * a + j], (*p, c))
                 for a in range(nw) for j, p in enumerate(others)]
        sends = [_remote(in_ref, out_ref.at[me], send_sems.at[0], recv_sems.at[0], (*near, c))]
        for cp in own + small + sends:
            cp.start()
        from_near = out_ref.at[chip_of(near)]
        _remote(from_near, from_near, send_sems.at[0], recv_sems.at[0], (*near, c)).wait_recv()
        sends.append(_remote(from_near, from_near, send_sems.at[1], recv_sems.at[1], sibling))
        sends[-1].start()
        from_far = out_ref.at[chip_of(far)]
        _remote(from_far, from_far, send_sems.at[1], recv_sems.at[1], sibling).wait_recv()
        sends.append(_remote(half(from_far, c), half(from_far, c), send_sems.at[2], recv_sems.at[2], (*near, c)))
        sends[-1].start()
        of_diag = out_ref.at[chip_of(diag)]
        _remote(half(of_diag, c), half(of_diag, c), send_sems.at[2], recv_sems.at[2], (*near, c)).wait_recv()
        sends.append(_remote(half(of_diag, c), half(of_diag, c), send_sems.at[3], recv_sems.at[3], sibling))
        sends[-1].start()
        _remote(half(of_diag, 1 - c), half(of_diag, 1 - c), send_sems.at[3], recv_sems.at[3], sibling).wait_recv()
        for a in range(nw):
            for j, p in enumerate(others):
                _remote(w_in[a], w_out[a].at[chip_of(p)], send_sems.at[4 + 3 * a + j], recv_sems.at[4 + 3 * a + j], (*p, c)).wait_recv()
        for cp in sends + small:
            cp.wait_send()
        for cp in own:
            cp.wait()

    arrays = [shard] + list(whole)
    n_sems = 4 + 3 * nw
    return pl.pallas_call(
        body, name=name, in_specs=[ANY] * (1 + nw), out_specs=[ANY] * (1 + nw),
        out_shape=[_sds((N_CHIPS,) + a.shape, a.dtype) for a in arrays],
        scratch_shapes=[pltpu.SemaphoreType.DMA((n_sems,)) for _ in range(3)],
    )(*arrays)


def _owners_exchange(csbs):
    na = len(csbs)

    def plan(ins, outs, send_sems, recv_sems, local_sems, receiving):
        x, y, c = _place()
        me = 2 * x + y
        sends, arrivals = [], []
        for a in range(na):
            for k in range(N_CHIPS - 1):
                j = (me + 1 + k) % N_CHIPS
                sends.append(_remote(ins[a].at[k], outs[a].at[2 - k], send_sems.at[3 * a + k], recv_sems.at[3 * a + 2 - k],
                                     (j // 2, j % 2, c)))
                if receiving:
                    arrivals.append(_remote(ins[a].at[k], outs[a].at[k], send_sems.at[3 * a + k], recv_sems.at[3 * a + k], (x, y, c)))
        return sends, arrivals

    def start(ins, outs, *sems):
        for cp in plan(ins, outs, *sems, False)[0]:
            cp.start()

    def finish(ins, outs, *sems):
        sends, arrivals = plan(ins, outs, *sems, True)
        for cp in arrivals:
            cp.wait_recv()
        for cp in sends:
            cp.wait_send()

    return _Exchange(csbs, [_sds(g.shape, g.dtype) for g in csbs], 3 * na, start, finish)


def _swap_halves(gxs, name):
    na = len(gxs)
    half_shape = lambda g: (WT_ROWS // 2, g.shape[1]) if g.ndim == 2 else g.shape[2:]

    def body(*refs):
        ins, outs = refs[:na], refs[na:2 * na]
        send_sems, recv_sems = refs[2 * na:]
        x, y, c = _place()
        cps = []
        for a in range(na):
            if gxs[a].ndim == 2:
                for j in range(N_CHIPS):
                    rows = pl.ds(pl.multiple_of(WT_WIN[j] + (1 - c) * (WT_ROWS // 2), 16), WT_ROWS // 2)
                    cps.append(_remote(ins[a].at[rows], outs[a].at[j, 0], send_sems.at[na + j], recv_sems.at[na + j], (x, y, 1 - c)))
            else:
                cps.append(_remote(ins[a].at[pl.ds(0, N_CHIPS), pl.ds(1 - c, 1)], outs[a], send_sems.at[a], recv_sems.at[a],
                                   (x, y, 1 - c)))
        for cp in cps:
            cp.start()
        for cp in cps:
            cp.wait()

    return pl.pallas_call(
        body, name=name, in_specs=[ANY] * na, out_specs=[ANY] * na,
        out_shape=[_sds((N_CHIPS, 1) + half_shape(g), g.dtype) for g in gxs],
        scratch_shapes=[pltpu.SemaphoreType.DMA((na + N_CHIPS,)), pltpu.SemaphoreType.DMA((na + N_CHIPS,))],
    )(*gxs)


def _half_block(gx, tr, shard):
    if gx.ndim == 4:
        return pl.BlockSpec((1, 1, tr, gx.shape[3]), lambda *g: (shard(*g), g[-1][0], g[-2], 0))
    return pl.BlockSpec(
        (pl.Element(tr), pl.Element(gx.shape[1])),
        lambda *g: (pl.multiple_of(g[-1][2 + shard(*g)] + g[-1][0] * (WT_ROWS // 2) + g[-2] * tr, 16), 0))


def _share_halves(gfs):
    na = len(gfs)

    def body(*refs):
        ins, outs = refs[:na], refs[na:2 * na]
        send_sems, recv_sems = refs[2 * na:]
        x, y, c = _place()
        cps = [_remote(ins[a].at[pl.ds(c, 1)], outs[a].at[pl.ds(c, 1)], send_sems.at[a], recv_sems.at[a], (x, y, 1 - c))
               for a in range(na)]
        for cp in cps:
            cp.start()
        for a in range(na):
            cps[a].wait_send()
            _remote(ins[a].at[pl.ds(1 - c, 1)], outs[a].at[pl.ds(1 - c, 1)], send_sems.at[a], recv_sems.at[a], (x, y, 1 - c)).wait_recv()

    return pl.pallas_call(
        body, name="rs_share_halves", in_specs=[ANY] * na, out_specs=[ANY] * na,
        out_shape=[_sds(g.shape, g.dtype) for g in gfs], input_output_aliases={a: a for a in range(na)},
        scratch_shapes=[pltpu.SemaphoreType.DMA((na,)), pltpu.SemaphoreType.DMA((na,))],
    )(*gfs)


def _row_tile(rows, cols, budget=5 << 18):
    fits = [t for t in range(16, rows + 1, 16) if rows % t == 0 and t * cols * 4 <= budget]
    return max(fits) if fits else rows


def _chip_sums(gx, r1, where):
    _, _, r2, cd = r1.shape
    tr = _row_tile(r2, cd)

    def body(w_ref, a_ref, b_ref, o_ref):
        o_ref[0] = (a_ref[...].reshape(tr, cd) + b_ref[0, 0].astype(F32)).astype(BF16)

    other = lambda k, i, w: (w[1] + 1 + k) % N_CHIPS
    return pl.pallas_call(
        body, name="rs_chip_sums",
        grid_spec=pltpu.PrefetchScalarGridSpec(
            num_scalar_prefetch=1, grid=(N_CHIPS - 1, r2 // tr),
            in_specs=[_half_block(gx, tr, other), pl.BlockSpec((1, 1, tr, cd), lambda k, i, w: (other(k, i, w), 0, i, 0))],
            out_specs=pl.BlockSpec((1, tr, cd), lambda k, i, w: (k, i, 0))),
        out_shape=_sds((N_CHIPS - 1, r2, cd), BF16), compiler_params=_cp(("parallel", "parallel")),
    )(where, gx, r1)


def _owner_sum(gx, r1, r2x, where):
    _, _, r2, cd = r1.shape
    tr = _row_tile(r2, cd)

    def body(w_ref, a_ref, b_ref, r_ref, o_ref):
        acc = a_ref[...].reshape(tr, cd) + b_ref[0, 0].astype(F32)
        for k in range(N_CHIPS - 1):
            acc = acc + r_ref[k].astype(F32)
        o_ref[0] = acc

    return pl.pallas_call(
        body, name="rs_owner_sum",
        grid_spec=pltpu.PrefetchScalarGridSpec(
            num_scalar_prefetch=1, grid=(r2 // tr,),
            in_specs=[_half_block(gx, tr, lambda i, w: w[1]),
                      pl.BlockSpec((1, 1, tr, cd), lambda i, w: (w[1], 0, i, 0)),
                      pl.BlockSpec((N_CHIPS - 1, tr, cd), lambda i, w: (0, i, 0))],
            out_specs=pl.BlockSpec((1, tr, cd), lambda i, w: (w[0], i, 0))),
        out_shape=_sds((2, r2, cd)), compiler_params=_cp(("parallel",)),
    )(where, gx, r1, r2x)


def _adamw_math(w, g, m, v):
    m = ADAM_B1 * m + (1.0 - ADAM_B1) * g
    v = ADAM_B2 * v + (1.0 - ADAM_B2) * (g * g)
    m_hat = m / (1.0 - ADAM_B1 ** ADAM_STEP)
    v_hat = v / (1.0 - ADAM_B2 ** ADAM_STEP)
    delta = -ADAM_LR * (m_hat / (jnp.sqrt(v_hat) + ADAM_EPS) + ADAM_WD * w)
    return delta, m, v


def _adamw(w, g, m, v, name, echo=False):
    rows, cd = w.shape
    if rows % 16 == 0:
        tr, tc = _row_tile(rows, cd, budget=3 << 19), cd
    else:
        tr, tc = rows, (128 if rows * cd * 4 > (3 << 19) else cd)
    assert rows % tr == 0 and cd % tc == 0
    n_out = 4 if echo else 3

    def body(w_ref, g_ref, m_ref, v_ref, d_ref, mo_ref, vo_ref, *go_ref):
        gv = g_ref[...]
        d, mm, vv = _adamw_math(w_ref[...], gv, m_ref[...], v_ref[...])
        d_ref[...] = d
        mo_ref[...] = mm
        vo_ref[...] = vv
        if echo:
            go_ref[0][...] = gv

    blk = pl.BlockSpec((tr, tc), lambda i, j: (i, j))
    return pl.pallas_call(
        body, name=name, grid=(rows // tr, cd // tc), in_specs=[blk] * 4, out_specs=[blk] * n_out,
        out_shape=[_sds(w.shape)] * n_out, compiler_params=_cp(("parallel", "parallel")),
    )(w, g, m, v)


def _small_allreduce(gp):
    R = gp.shape[0]
    R2 = R // 2
    assert R2 % 8 == 0

    def body(g_ref, go_ref, sib, csum, land, send_sems, recv_sems):
        x, y, c = _place()
        me = 2 * x + y
        sibling = (x, y, 1 - c)
        chips = [(1 - x, y), (x, 1 - y), (1 - x, 1 - y)]
        swap = _remote(g_ref, sib, send_sems.at[0], recv_sems.at[0], sibling)
        swap.start()
        swap.wait()
        csum[...] = g_ref[...] + sib[...]
        half = csum.at[pl.ds(c * R2, R2)]
        land[me] = csum[pl.ds(c * R2, R2), :]
        cps = [_remote(half, land.at[me], send_sems.at[1 + j], recv_sems.at[1 + j], (px, py, c))
               for j, (px, py) in enumerate(chips)]
        for cp in cps:
            cp.start()
        for j, (px, py) in enumerate(chips):
            _remote(half, land.at[2 * px + py], send_sems.at[1 + j], recv_sems.at[1 + j], (px, py, c)).wait_recv()
        for cp in cps:
            cp.wait_send()
        mine = go_ref.at[pl.ds(c * R2, R2)]
        go_ref[pl.ds(c * R2, R2), :] = (land[0] + land[1]) + (land[2] + land[3])
        share = _remote(mine, mine, send_sems.at[4], recv_sems.at[4], sibling)
        share.start()
        share.wait_send()
        other = go_ref.at[pl.ds((1 - c) * R2, R2)]
        _remote(other, other, send_sems.at[4], recv_sems.at[4], sibling).wait_recv()

    vm = pl.BlockSpec(memory_space=pltpu.VMEM)
    return pl.pallas_call(
        body, name="small_allreduce", in_specs=[vm], out_specs=vm, out_shape=_sds((R, 128)),
        scratch_shapes=[pltpu.VMEM((R, 128), F32), pltpu.VMEM((R, 128), F32), pltpu.VMEM((N_CHIPS, R2, 128), F32),
                        pltpu.SemaphoreType.DMA((5,)), pltpu.SemaphoreType.DMA((5,))],
        compiler_params=_cp(),
    )(gp)


def _adamw_many(ws, gs, ms, vs):
    n = len(ws)

    def body(*refs):
        w_r, g_r, m_r, v_r = refs[:n], refs[n:2 * n], refs[2 * n:3 * n], refs[3 * n:4 * n]
        d_r, mo_r, vo_r = refs[4 * n:5 * n], refs[5 * n:6 * n], refs[6 * n:]
        for i in range(n):
            d_r[i][...], mo_r[i][...], vo_r[i][...] = _adamw_math(w_r[i][...], g_r[i][...], m_r[i][...], v_r[i][...])

    vm = pl.BlockSpec(memory_space=pltpu.VMEM)
    shapes = [_sds(a.shape) for a in ws]
    outs = pl.pallas_call(
        body, name="adamw_small", in_specs=[vm] * (4 * n), out_specs=[vm] * (3 * n), out_shape=shapes * 3, compiler_params=_cp(),
    )(*ws, *gs, *ms, *vs)
    return outs[:n], outs[n:2 * n], outs[2 * n:]


def _pack(arrs):
    rows = []
    for a in arrs:
        f = a.reshape(-1)
        f = jnp.pad(f, (0, (-f.shape[0]) % 128))
        rows.append(f.reshape(-1, 128))
    p = jnp.concatenate(rows, axis=0)
    return jnp.pad(p, ((0, (-p.shape[0]) % 8), (0, 0)))


def _unpack(p, shapes):
    out, r = [], 0
    for s in shapes:
        n = math.prod(s)
        nr = -(-n // 128)
        out.append(p[r:r + nr].reshape(-1)[:n].reshape(s))
        r += nr
    return out


class _ExchangeOps(_PlainOps):
    def __init__(self, rest_shards, where):
        self.rest_shards, self.where = rest_shards, where
        self.reduced = []

    def in_proj(self, h, wt_perm):
        proj, g_glu, g_su, g_du, g_out = _mm(h, wt_perm, tb=True, name="in_proj", tm=1024, tn=1152,
                                             exchange=_gather_exchange(self.rest_shards))
        cat = lambda g: jnp.concatenate([g[j] for j in range(N_CHIPS)], axis=1)
        return proj, (g_glu.reshape(D_S5, D_S5), cat(g_su), cat(g_du), g_out.reshape(D_MODEL, D_MODEL))

    def _chip_sums(self, gxs, twins, name):
        r1s = _swap_halves(twins, name)
        return r1s, [_chip_sums(gx, r1, self.where) for gx, r1 in zip(gxs, r1s)]

    def rest_grads(self, grads, twins):
        shapes = [(N_CHIPS, 2, D_S5 // 8, D_S5), (N_CHIPS, 2, D_S5 // 2, D_MODEL // N_CHIPS),
                  (N_CHIPS, 2, D_DN // 2, D_MODEL // N_CHIPS), (N_CHIPS, 2, D_MODEL // 8, D_MODEL)]
        gxs = [g.reshape(s) for g, s in zip(grads, shapes)]
        r1s, csbs = self._chip_sums(gxs, [t.reshape(s) for t, s in zip(twins, shapes)], "rs_swap_rest")
        self.rest = (gxs, r1s, csbs)

    def d_w_in(self, h, dproj):
        gxs, r1s, csbs = self.rest
        d_wt_perm, twin, *r2s = _mm(dproj, h, ta=True, name="d_w_in", tm=1152, tn=1024, twin=True,
                                    exchange=_owners_exchange(csbs))
        self.reduced = list(zip(gxs, r1s, r2s))
        return d_wt_perm, twin

    def d_h(self, dproj, wt_perm, d_wt_perm, d_wt_twin):
        self.beta_a = d_wt_perm[OFF_B:OFF_B + WT_NB]
        (r1,), (csb,) = self._chip_sums([d_wt_perm], [d_wt_twin], "rs_swap_w_in")
        dh, r2 = _mm(dproj, wt_perm, name="d_h", tm=2048, tn=1024, tk=1152, exchange=_owners_exchange([csb]))
        self.reduced = [(d_wt_perm, r1, r2)] + self.reduced
        return dh


WT_SHARD = D_IN // N_CHIPS
WT_NB = 2 * DN_HEADS
WT_B, WT_LO = divmod(OFF_GS, WT_SHARD)
WT_FIRST = [i * WT_SHARD - (WT_NB if i > WT_B else 0) for i in range(N_CHIPS)]
WT_WIN = [16 * (r // 16) for r in WT_FIRST]
WT_SHIFT = [r - s for r, s in zip(WT_FIRST, WT_WIN)]
WT_ROWS = 2592
assert (WT_LO + WT_SHIFT[WT_B]) % 16 == 0 and max(WT_SHIFT) + WT_SHARD <= WT_ROWS and WT_WIN[-1] + WT_ROWS <= D_IN_PAD


def _wt_to_window(shard, chip):
    d = jnp.asarray(WT_SHIFT, jnp.int32)[chip]
    gap = jnp.where(chip == WT_B, 0, WT_NB)
    win = jnp.zeros((WT_ROWS, shard.shape[1]), shard.dtype)
    win = lax.dynamic_update_slice(win, shard[:WT_LO], (d, 0))
    win = lax.dynamic_update_slice(win, shard[WT_LO:WT_LO + WT_NB], (d + WT_LO, 0))
    win = lax.dynamic_update_slice(win, shard[WT_LO + WT_NB:], (d + WT_LO + gap, 0))
    return win, shard[WT_LO:WT_LO + WT_NB]


def _wt_from_window(win, beta_a, chip):
    d = jnp.asarray(WT_SHIFT, jnp.int32)[chip]
    gap = jnp.where(chip == WT_B, 0, WT_NB)
    cols = win.shape[1]
    head = lax.dynamic_slice(win, (d, 0), (WT_LO, cols))
    mid = jnp.where(chip == WT_B, beta_a, lax.dynamic_slice(win, (d + WT_LO, 0), (WT_NB, cols)))
    tail = lax.dynamic_slice(win, (d + WT_LO + gap, 0), (WT_SHARD - WT_LO - WT_NB, cols))
    return jnp.concatenate([head, mid, tail], axis=0)


def _wt_regroup(wins, beta_a):
    parts, at = [], 0
    for i in range(N_CHIPS):
        end = WT_WIN[i + 1] if i + 1 < N_CHIPS else OFF_B
        lo = at - WT_WIN[i]
        over = WT_WIN[i] + WT_ROWS - end if i + 1 < N_CHIPS else 0
        parts.append(wins[i, lo:end - WT_WIN[i]])
        if over:
            parts.append(wins[i, end - WT_WIN[i]:] + wins[i + 1, :over])
        at = end + over
    pad = jnp.zeros((D_IN_PAD - OFF_B - WT_NB, wins.shape[2]), wins.dtype)
    return jnp.concatenate(parts + [beta_a, pad], axis=0)


_SMALL = ("ln_w", "s5_lam_re", "s5_lam_im", "s5_log_step", "s5_b_re", "s5_b_im", "s5_c_re", "s5_c_im", "s5_d",
          "dn_a_log", "dn_dt_bias", "dn_norm_w", "final_norm_w")
_BIG = ("w_in", "s5_w_glu", "s5_w_up", "dn_w_up", "w_out")
_ORDER = ("ln_w", "w_in", "s5_lam_re", "s5_lam_im", "s5_log_step", "s5_b_re", "s5_b_im", "s5_c_re", "s5_c_im", "s5_d",
          "s5_w_glu", "s5_w_up", "dn_conv_w", "dn_a_log", "dn_dt_bias", "dn_norm_w", "dn_w_up", "w_out", "final_norm_w")


def kernel(x, ln_w, w_in, s5_lam_re, s5_lam_im, s5_log_step, s5_b_re, s5_b_im, s5_c_re, s5_c_im, s5_d, s5_w_glu, s5_w_up, dn_conv_w, dn_a_log, dn_dt_bias, dn_norm_w, dn_w_up, w_out, final_norm_w, loss_target, m_ln_w, m_w_in, m_s5_lam_re, m_s5_lam_im, m_s5_log_step, m_s5_b_re, m_s5_b_im, m_s5_c_re, m_s5_c_im, m_s5_d, m_s5_w_glu, m_s5_w_up, m_dn_conv_w, m_dn_a_log, m_dn_dt_bias, m_dn_norm_w, m_dn_w_up, m_w_out, m_final_norm_w, v_ln_w, v_w_in, v_s5_lam_re, v_s5_lam_im, v_s5_log_step, v_s5_b_re, v_s5_b_im, v_s5_c_re, v_s5_c_im, v_s5_d, v_s5_w_glu, v_s5_w_up, v_dn_conv_w, v_dn_a_log, v_dn_dt_bias, v_dn_norm_w, v_dn_w_up, v_w_out, v_final_norm_w):
    w = dict(ln_w=ln_w, w_in=w_in, s5_lam_re=s5_lam_re, s5_lam_im=s5_lam_im, s5_log_step=s5_log_step, s5_b_re=s5_b_re,
             s5_b_im=s5_b_im, s5_c_re=s5_c_re, s5_c_im=s5_c_im, s5_d=s5_d, s5_w_glu=s5_w_glu, s5_w_up=s5_w_up,
             dn_conv_w=dn_conv_w, dn_a_log=dn_a_log, dn_dt_bias=dn_dt_bias, dn_norm_w=dn_norm_w, dn_w_up=dn_w_up, w_out=w_out,
             final_norm_w=final_norm_w)
    m = dict(ln_w=m_ln_w, w_in=m_w_in, s5_lam_re=m_s5_lam_re, s5_lam_im=m_s5_lam_im, s5_log_step=m_s5_log_step,
             s5_b_re=m_s5_b_re, s5_b_im=m_s5_b_im, s5_c_re=m_s5_c_re, s5_c_im=m_s5_c_im, s5_d=m_s5_d, s5_w_glu=m_s5_w_glu,
             s5_w_up=m_s5_w_up, dn_conv_w=m_dn_conv_w, dn_a_log=m_dn_a_log, dn_dt_bias=m_dn_dt_bias, dn_norm_w=m_dn_norm_w,
             dn_w_up=m_dn_w_up, w_out=m_w_out, final_norm_w=m_final_norm_w)
    v = dict(ln_w=v_ln_w, w_in=v_w_in, s5_lam_re=v_s5_lam_re, s5_lam_im=v_s5_lam_im, s5_log_step=v_s5_log_step,
             s5_b_re=v_s5_b_re, s5_b_im=v_s5_b_im, s5_c_re=v_s5_c_re, s5_c_im=v_s5_c_im, s5_d=v_s5_d, s5_w_glu=v_s5_w_glu,
             s5_w_up=v_s5_w_up, dn_conv_w=v_dn_conv_w, dn_a_log=v_dn_a_log, dn_dt_bias=v_dn_dt_bias, dn_norm_w=v_dn_norm_w,
             dn_w_up=v_dn_w_up, w_out=v_w_out, final_norm_w=v_final_norm_w)
    xi, yi, ci = _place()
    chip = 2 * xi + yi
    where = jnp.stack([ci, chip, *[jnp.int32(s) for s in WT_WIN]]).astype(jnp.int32)

    tr = lambda a: jnp.swapaxes(a[0], 0, 1)
    win, beta_a = _wt_to_window(tr(w_in).astype(BF16), chip)
    g_win, g_ba, g_conv = _gather_relayed(win, [beta_a, dn_conv_w[0]], "gather_w_in")
    cat = lambda g: jnp.concatenate([g[j] for j in range(N_CHIPS)], axis=1)
    w_perm = _wt_regroup(g_win, g_ba[WT_B])

    ops = _ExchangeOps([w[n][0].astype(BF16) for n in _BIG[1:]], where)
    loss_row, grad_x, g = _local_step(
        x[0], loss_target[0], ln_w, w_perm, s5_lam_re[0], s5_lam_im[0], s5_log_step, s5_b_re[0], s5_b_im[0], s5_c_re[0],
        s5_c_im[0], s5_d, cat(g_conv), dn_a_log, dn_dt_bias, dn_norm_w, final_norm_w[None], ops)
    loss = lax.psum(loss_row[0, 0], ("x", "y", "c"))

    gfs = [_owner_sum(gx, r1, r2x, where) for gx, r1, r2x in ops.reduced]
    gfs = _share_halves(gfs)
    grads, deltas, new_m, new_v = {}, {}, {}, {}
    for n, gf in zip(_BIG[1:], gfs[1:]):
        shp = w[n].shape
        g2 = gf.reshape(shp[1:])
        d_, m_, v_ = _adamw(w[n][0], g2, m[n][0], v[n][0], "adamw_" + n)
        grads[n], deltas[n], new_m[n], new_v[n] = g2.reshape(shp), d_.reshape(shp), m_.reshape(shp), v_.reshape(shp)

    go = _small_allreduce(_pack([g[n] for n in _SMALL] + [g["dn_conv_w"], ops.beta_a]))
    lanes = {"s5_b_re": (S5_GROUPS, S5_STATE * S5_GROUP), "s5_b_im": (S5_GROUPS, S5_STATE * S5_GROUP)}
    flat = [lanes.get(n, (math.prod(w[n].shape[:-1]), w[n].shape[-1])) for n in _SMALL]
    *gs, g_conv, g_beta_a = _unpack(go, flat + [(CONV_K, 3 * D_DN), (WT_NB, D_MODEL)])
    gt = _wt_from_window(gfs[0].reshape(WT_ROWS, D_MODEL), g_beta_a, chip)
    d_, m_, v_, g_ = _adamw(tr(w_in), gt, tr(m_w_in), tr(v_w_in), "adamw_w_in", echo=True)
    grads["w_in"], deltas["w_in"], new_m["w_in"], new_v["w_in"] = (jnp.swapaxes(a, 0, 1)[None] for a in (g_, d_, m_, v_))
    as2d = lambda t: [t[n].reshape(s) for n, s in zip(_SMALL, flat)]
    for dst, src in zip((grads, deltas, new_m, new_v), (gs, *_adamw_many(as2d(w), gs, as2d(m), as2d(v)))):
        for n, a in zip(_SMALL, src):
            dst[n] = a.reshape(w[n].shape)
    cc = 3 * D_DN // N_CHIPS
    g_conv_mine = lax.dynamic_slice(g_conv, (0, chip * cc), (CONV_K, cc))
    d_, m_, v_ = _adamw(dn_conv_w[0], g_conv_mine, m_dn_conv_w[0], v_dn_conv_w[0], "adamw_dn_conv_w")
    grads["dn_conv_w"], deltas["dn_conv_w"], new_m["dn_conv_w"], new_v["dn_conv_w"] = (
        g_conv_mine[None], d_[None], m_[None], v_[None])

    return (loss, grad_x[None], *[grads[n] for n in _ORDER], *[deltas[n] for n in _ORDER], *[new_m[n] for n in _ORDER],
            *[new_v[n] for n in _ORDER])
```

```python
import functools
import math

import jax
import jax.numpy as jnp
from jax import lax
from jax.experimental import pallas as pl
from jax.experimental.pallas import tpu as pltpu

F32 = jnp.float32
BF16 = jnp.bfloat16
HI = lax.Precision.HIGHEST
MESH = pl.DeviceIdType.MESH
ANY = pl.BlockSpec(memory_space=pl.ANY)

EPS = 1e-6
D_MODEL = 2048
D_S5 = 1024
S5_GROUP = 16
S5_GROUPS = 64
S5_STATE = 64
S5_BLOCKS = 8
S5_SEG = 8
DN_HEADS = 8
DN_HEAD_DIM = 128
D_DN = 1024
CONV_K = 4
CHUNK = 64
D_IN = 10256
D_IN_PAD = 10368
OFF_US, OFF_ZS, OFF_Q, OFF_K, OFF_V, OFF_ZD, OFF_GS, OFF_GD, OFF_B = 0, 1024, 2048, 3072, 4096, 5120, 6144, 8192, 10240
N_CHIPS = 4
N_DEV = 8
VMEM_LIMIT = 56 * 1024 * 1024

ADAM_LR = 0.001
ADAM_B1 = 0.9
ADAM_B2 = 0.999
ADAM_EPS = 1e-08
ADAM_WD = 0.01
ADAM_STEP = 10


def _cp(sem=None):
    return pltpu.CompilerParams(dimension_semantics=sem, vmem_limit_bytes=VMEM_LIMIT)


def _sds(shape, dtype=F32):
    return jax.ShapeDtypeStruct(tuple(shape), dtype)


def _sigmoid(x):
    return 1.0 / (1.0 + jnp.exp(-x))


def _silu(x):
    return x * _sigmoid(x)


def _dsilu(x):
    s = _sigmoid(x)
    return s * (1.0 + x * (1.0 - s))


class _Exchange:
    def __init__(self, ins, out_shapes, n_sems, start, finish):
        self.ins, self.out_shapes, self.n_sems, self.start, self.finish = list(ins), list(out_shapes), n_sems, start, finish


def _mm(a, b, *, name, ta=False, tb=False, out_dtype=F32, tm=512, tn=512, tk=2048, shard_out=False, twin=False, exchange=None):
    if ta:
        K, M = a.shape
    else:
        M, K = a.shape
    if tb:
        N, K2 = b.shape
    else:
        K2, N = b.shape
    assert K == K2, (a.shape, b.shape)
    tm, tn, tk = min(tm, M), min(tn, N), min(tk, K)
    assert M % tm == 0 and N % tn == 0 and K % tk == 0, (M, N, K, tm, tn, tk)
    nk = K // tk
    dims = (((0 if ta else 1,), (1 if tb else 0,)), ((), ()))

    gm, gn = M // tm, N // tn
    n_in = len(exchange.ins) if exchange else 0
    n_out = len(exchange.out_shapes) if exchange else 0

    n_o = 2 if twin else 1

    def body(*refs):
        a_ref, b_ref, xin, o_refs = refs[0], refs[1], refs[2:2 + n_in], refs[2 + n_in:2 + n_in + n_o]
        xout, rest = refs[2 + n_in + n_o:2 + n_in + n_o + n_out], refs[2 + n_in + n_o + n_out:]
        i, j, k = pl.program_id(0), pl.program_id(1), pl.program_id(2)

        def write(val):
            o_refs[0][...] = val.astype(out_dtype).reshape(o_refs[0].shape)
            if twin:
                o_refs[1][...] = val.astype(BF16).reshape(o_refs[1].shape)

        if exchange:
            sems = rest[-3:]

            @pl.when(jnp.logical_and(jnp.logical_and(i == 0, j == 0), k == 0))
            def _():
                exchange.start(xin, xout, *sems)

        p = lax.dot_general(a_ref[...].astype(BF16), b_ref[...].astype(BF16), dims, preferred_element_type=F32)
        if nk == 1:
            write(p)
        else:
            acc_ref = rest[0]

            @pl.when(k == 0)
            def _():
                acc_ref[...] = p

            @pl.when(k > 0)
            def _():
                acc_ref[...] += p

            @pl.when(k == nk - 1)
            def _():
                write(acc_ref[...])

        if exchange:
            @pl.when(jnp.logical_and(jnp.logical_and(i == gm - 1, j == gn - 1), k == nk - 1))
            def _():
                exchange.finish(xin, xout, *sems)

    a_spec = pl.BlockSpec((tk, tm), lambda i, j, k: (k, i)) if ta else pl.BlockSpec((tm, tk), lambda i, j, k: (i, k))
    b_spec = pl.BlockSpec((tn, tk), lambda i, j, k: (j, k)) if tb else pl.BlockSpec((tk, tn), lambda i, j, k: (k, j))
    if shard_out:
        o_spec = pl.BlockSpec((1, tm, tn), lambda i, j, k: (j, i, 0))
        o_shape = _sds((N // tn, M, tn), out_dtype)
    else:
        o_spec = pl.BlockSpec((tm, tn), lambda i, j, k: (i, j))
        o_shape = _sds((M, N), out_dtype)
    scratch = [pltpu.VMEM((tm, tn), F32)] if nk > 1 else []
    o_specs, o_shapes = [o_spec] * n_o, [o_shape, _sds(o_shape.shape, BF16)][:n_o]
    if not exchange:
        out = pl.pallas_call(
            body, name=name, grid=(gm, gn, nk), in_specs=[a_spec, b_spec], out_specs=o_specs, out_shape=o_shapes,
            scratch_shapes=scratch, compiler_params=_cp(("parallel", "parallel", "arbitrary")),
        )(a, b)
        return out if twin else out[0]
    scratch += [pltpu.SemaphoreType.DMA((exchange.n_sems,)) for _ in range(3)]
    return pl.pallas_call(
        body, name=name, grid=(gm, gn, nk), in_specs=[a_spec, b_spec] + [ANY] * n_in, out_specs=o_specs + [ANY] * n_out,
        out_shape=o_shapes + exchange.out_shapes, scratch_shapes=scratch,
        compiler_params=_cp(("arbitrary", "arbitrary", "arbitrary")),
    )(a, b, *exchange.ins)


def _ln_fwd(x, w):
    L, D = x.shape
    tm = min(256, L)

    def body(x_ref, w_ref, h_ref, r_ref):
        xv = x_ref[...]
        r = lax.rsqrt(jnp.mean(xv * xv, axis=-1, keepdims=True) + EPS)
        h_ref[...] = (xv * r * w_ref[...]).astype(BF16)
        r_ref[...] = r

    return pl.pallas_call(
        body, name="ln_fwd", grid=(L // tm,),
        in_specs=[pl.BlockSpec((tm, D), lambda i: (i, 0)), pl.BlockSpec((1, D), lambda i: (0, 0))],
        out_specs=[pl.BlockSpec((tm, D), lambda i: (i, 0)), pl.BlockSpec((tm, 1), lambda i: (i, 0))],
        out_shape=[_sds((L, D), BF16), _sds((L, 1))], compiler_params=_cp(("parallel",)),
    )(x, w)


def _ln_bwd(x, r, w, dh, dx2):
    L, D = x.shape
    tm = min(256, L)

    def body(x_ref, r_ref, w_ref, dh_ref, dx2_ref, dx_ref, dw_ref):
        i = pl.program_id(0)
        xv, rv, dhv = x_ref[...], r_ref[...], dh_ref[...]
        t = dhv * w_ref[...]
        m = jnp.mean(t * xv, axis=-1, keepdims=True)
        dx_ref[...] = dx2_ref[...] + rv * t - xv * (rv * rv * rv) * m
        part = jnp.sum(dhv * xv * rv, axis=0, keepdims=True)

        @pl.when(i == 0)
        def _():
            dw_ref[...] = part

        @pl.when(i > 0)
        def _():
            dw_ref[...] += part

    row = pl.BlockSpec((tm, D), lambda i: (i, 0))
    return pl.pallas_call(
        body, name="ln_bwd", grid=(L // tm,),
        in_specs=[row, pl.BlockSpec((tm, 1), lambda i: (i, 0)), pl.BlockSpec((1, D), lambda i: (0, 0)), row, row],
        out_specs=[row, pl.BlockSpec((1, D), lambda i: (0, 0))],
        out_shape=[_sds((L, D)), _sds((1, D))], compiler_params=_cp(("arbitrary",)),
    )(x, r, w, dh, dx2)


def _s5_param_math(lam_re, lam_im, log_step, b_re, b_im, expand):
    step = jnp.exp(log_step)
    mag = jnp.exp(lam_re * step)
    abar_re = mag * jnp.cos(lam_im * step)
    abar_im = mag * jnp.sin(lam_im * step)
    den = lam_re * lam_re + lam_im * lam_im
    xr = abar_re - 1.0
    f_re = (xr * lam_re + abar_im * lam_im) / den
    f_im = (abar_im * lam_re - xr * lam_im) / den
    fe_re = jnp.dot(f_re, expand, precision=HI, preferred_element_type=F32)
    fe_im = jnp.dot(f_im, expand, precision=HI, preferred_element_type=F32)
    bb_re = fe_re * b_re - fe_im * b_im
    bb_im = fe_re * b_im + fe_im * b_re
    return abar_re, abar_im, bb_re, bb_im


def _s5_expand():
    p = lax.broadcasted_iota(jnp.int32, (S5_STATE, S5_STATE * S5_GROUP), 0)
    q = lax.broadcasted_iota(jnp.int32, (S5_STATE, S5_STATE * S5_GROUP), 1)
    return (q // S5_GROUP == p).astype(F32)


def _s5_param_fwd(lam_re, lam_im, log_step, b_re, b_im):
    G, P = lam_re.shape

    def body(lr, li, ls, br, bi, ar_o, ai_o, bbr_o, bbi_o):
        outs = _s5_param_math(lr[...], li[...], ls[...], br[...], bi[...], _s5_expand())
        for o, v in zip((ar_o, ai_o, bbr_o, bbi_o), outs):
            o[...] = v

    return pl.pallas_call(
        body, name="s5_param_fwd",
        out_shape=[_sds((G, P)), _sds((G, P)), _sds(b_re.shape), _sds(b_re.shape)], compiler_params=_cp(),
    )(lam_re, lam_im, log_step, b_re, b_im)


def _s5_param_bwd(lam_re, lam_im, log_step, b_re, b_im, dar, dai, dbbr, dbbi):
    G, P = lam_re.shape

    def body(lr, li, ls, br, bi, g0, g1, g2, g3, dlr, dli, dls, dbr, dbi):
        ex = _s5_expand()
        _, f = jax.vjp(lambda a, b, c, d, e: _s5_param_math(a, b, c, d, e, ex), lr[...], li[...], ls[...], br[...], bi[...])
        grads = f((g0[...], g1[...], g2[...], g3[...]))
        for o, v in zip((dlr, dli, dls, dbr, dbi), grads):
            o[...] = v

    return pl.pallas_call(
        body, name="s5_param_bwd",
        out_shape=[_sds((G, P)), _sds((G, P)), _sds((G, 1)), _sds(b_re.shape), _sds(b_re.shape)], compiler_params=_cp(),
    )(lam_re, lam_im, log_step, b_re, b_im, dar, dai, dbbr, dbbi)


def _to_segs(src_ref, dst_ref, L):
    S = L // S5_SEG

    def body(j, carry):
        dst_ref[pl.ds(pl.multiple_of(S5_SEG * j, S5_SEG), S5_SEG), :] = src_ref[pl.ds(j, S5_SEG, stride=S), :]
        return carry

    lax.fori_loop(0, S, body, 0, unroll=8)


def _from_segs(src_ref, L, write):
    S = L // S5_SEG
    for seg in range(S5_SEG):
        def body(jb, carry, seg=seg):
            j0 = 16 * jb
            write(pl.multiple_of(seg * S + j0, 16), src_ref[pl.ds(S5_SEG * j0 + seg, 16, stride=S5_SEG), :])
            return carry

        lax.fori_loop(0, S // 16, body, 0, unroll=4)


def _scan_segs(ar, ai, re_ref, im_ref, end_r_ref, end_i_ref, c_r_ref, c_i_ref, L, tile0, reverse):
    S = L // S5_SEG
    NB, LN = re_ref.shape[0], 128
    assert S & (S - 1) == 0
    tile = lambda j: pl.ds(pl.multiple_of(S5_SEG * (tile0 + j), S5_SEG), S5_SEG)
    ar8 = [jnp.broadcast_to(ar[:, b * LN:(b + 1) * LN], (S5_SEG, LN)) for b in range(NB)]
    ai8 = [jnp.broadcast_to(ai[:, b * LN:(b + 1) * LN], (S5_SEG, LN)) for b in range(NB)]

    def step(idx, carry):
        rows = tile(S - 1 - idx if reverse else idx)
        out = []
        for b in range(NB):
            sr, si = carry[b]
            nr = ar8[b] * sr - ai8[b] * si + re_ref[b, rows, :]
            ni = ar8[b] * si + ai8[b] * sr + im_ref[b, rows, :]
            re_ref[b, rows, :] = nr
            im_ref[b, rows, :] = ni
            out.append((nr, ni))
        return tuple(out)

    z8 = jnp.zeros((S5_SEG, LN), F32)
    fin = lax.fori_loop(0, S, step, tuple((z8, z8) for _ in range(NB)), unroll=4)
    order = range(S5_SEG - 2, -1, -1) if reverse else range(1, S5_SEG)
    for b in range(NB):
        end_r_ref[b], end_i_ref[b] = fin[b]
        pr, pi = ar8[b][:1], ai8[b][:1]
        for _ in range(int(math.log2(S))):
            pr, pi = pr * pr - pi * pi, 2.0 * pr * pi
        first = S5_SEG - 1 if reverse else 0
        c_r_ref[b, pl.ds(first, 1), :] = jnp.zeros((1, LN), F32)
        c_i_ref[b, pl.ds(first, 1), :] = jnp.zeros((1, LN), F32)
        cr, ci = end_r_ref[b, pl.ds(first, 1), :], end_i_ref[b, pl.ds(first, 1), :]
        for i in order:
            c_r_ref[b, pl.ds(i, 1), :] = cr
            c_i_ref[b, pl.ds(i, 1), :] = ci
            er, ei = end_r_ref[b, pl.ds(i, 1), :], end_i_ref[b, pl.ds(i, 1), :]
            cr, ci = er + pr * cr - pi * ci, ei + pr * ci + pi * cr

    entering = [(c_r_ref[b], c_i_ref[b]) for b in range(NB)]

    def fix(idx, carry):
        rows = tile(S - 1 - idx if reverse else idx)
        out = []
        for b in range(NB):
            pr, pi = carry[b]
            cr, ci = entering[b]
            re_ref[b, rows, :] += pr * cr - pi * ci
            im_ref[b, rows, :] += pr * ci + pi * cr
            out.append((pr * ar8[b] - pi * ai8[b], pr * ai8[b] + pi * ar8[b]))
        return tuple(out)

    lax.fori_loop(0, S, fix, tuple((ar8[b], ai8[b]) for b in range(NB)), unroll=4)


def _s5_seg_scratch(L, cs, pad):
    NB = cs // 128
    small = [pltpu.VMEM((NB, S5_SEG, 128), F32) for _ in range(4)]
    return [pltpu.VMEM((NB, L + pad, 128), F32), pltpu.VMEM((NB, L + pad, 128), F32)] + small


def _s5_core_fwd(proj, wbr, wbi, a_re, a_im, cbr, cbi, exchange=None):
    L = proj.shape[0]
    nb, ci, cs = wbr.shape
    NB = cs // 128
    n_in = len(exchange.ins) if exchange else 0
    n_out = len(exchange.out_shapes) if exchange else 0

    def body(u_ref, wbr_ref, wbi_ref, ar_ref, ai_ref, cbr_ref, cbi_ref, *rest):
        xin, y_ref, xout = rest[:n_in], rest[n_in], rest[n_in + 1:n_in + 1 + n_out]
        sr, si, er, ei, cr, cim, up, yp = rest[n_in + 1 + n_out:n_in + 9 + n_out]
        sems = rest[n_in + 9 + n_out:]
        if exchange:
            @pl.when(pl.program_id(0) == 0)
            def _():
                exchange.start(xin, xout, *sems)

        _to_segs(u_ref, up, L)
        u = up[...].astype(BF16)
        for b in range(NB):
            lanes = pl.ds(b * 128, 128)
            sr[b] = jnp.dot(u, wbr_ref[0, :, lanes], preferred_element_type=F32)
            si[b] = jnp.dot(u, wbi_ref[0, :, lanes], preferred_element_type=F32)
        _scan_segs(ar_ref[...], ai_ref[...], sr, si, er, ei, cr, cim, L, 0, False)
        y = jnp.zeros((L, ci), F32)
        for b in range(NB):
            lanes = pl.ds(b * 128, 128)
            y = y + (jnp.dot(sr[b].astype(BF16), cbr_ref[0, lanes, :], preferred_element_type=F32)
                     - jnp.dot(si[b].astype(BF16), cbi_ref[0, lanes, :], preferred_element_type=F32))
        yp[...] = y

        def write(row, val):
            y_ref[pl.ds(row, 16), :] = val

        _from_segs(yp, L, write)
        if exchange:
            @pl.when(pl.program_id(0) == nb - 1)
            def _():
                exchange.finish(xin, xout, *sems)

    wspec = pl.BlockSpec((1, ci, cs), lambda j: (j, 0, 0))
    aspec = pl.BlockSpec((1, cs), lambda j: (0, j))
    cspec = pl.BlockSpec((1, cs, ci), lambda j: (j, 0, 0))
    scratch = _s5_seg_scratch(L, cs, 0) + [pltpu.VMEM((L, ci), F32), pltpu.VMEM((L, ci), F32)]
    if exchange:
        scratch += [pltpu.SemaphoreType.DMA((exchange.n_sems,)) for _ in range(3)]
    outs = pl.pallas_call(
        body, name="s5_core_fwd", grid=(nb,),
        in_specs=[pl.BlockSpec((L, ci), lambda j: (0, OFF_US // ci + j)), wspec, wspec, aspec, aspec, cspec, cspec] + [ANY] * n_in,
        out_specs=[pl.BlockSpec((L, ci), lambda j: (0, j))] + [ANY] * n_out,
        out_shape=[_sds((L, nb * ci))] + (exchange.out_shapes if exchange else []),
        scratch_shapes=scratch, compiler_params=_cp(("arbitrary",)),
    )(proj, wbr, wbi, a_re, a_im, cbr, cbi, *(exchange.ins if exchange else []))
    return outs if exchange else outs[0]


def _s5_core_bwd(proj, wbr, wbi, a_re, a_im, cbr, cbi, dyc, du1):
    L = proj.shape[0]
    nb, ci, cs = wbr.shape
    NB = cs // 128
    S = L // S5_SEG
    PAD = S5_SEG

    def body(u_ref, wbr_ref, wbi_ref, ar_ref, ai_ref, cbr_ref, cbi_ref, dy_ref, du1_ref,
             du_ref, dwbr_ref, dwbi_ref, dcbr_ref, dcbi_ref, dar_ref, dai_ref,
             sr, si, er, ei, cr, cim, lr, li, up, dyp, dup):
        tn = (((0,), (0,)), ((), ()))
        nt = (((1,), (1,)), ((), ()))
        _to_segs(u_ref, up, L)
        _to_segs(dy_ref, dyp, L)
        _to_segs(du1_ref, dup, L)
        u = up[...].astype(BF16)
        dy = dyp[...].astype(BF16)
        ar, ai = ar_ref[...], ai_ref[...]
        for b in range(NB):
            lanes = pl.ds(b * 128, 128)
            sr[b, pl.ds(PAD, L), :] = jnp.dot(u, wbr_ref[0, :, lanes], preferred_element_type=F32)
            si[b, pl.ds(PAD, L), :] = jnp.dot(u, wbi_ref[0, :, lanes], preferred_element_type=F32)
        _scan_segs(ar, ai, sr, si, er, ei, cr, cim, L, 1, False)
        for b in range(NB):
            lanes = pl.ds(b * 128, 128)
            sr[b, pl.ds(0, PAD), :] = cr[b]
            si[b, pl.ds(0, PAD), :] = cim[b]
            lr[b] = lax.dot_general(dy, cbr_ref[0, lanes, :], nt, preferred_element_type=F32)
            li[b] = -lax.dot_general(dy, cbi_ref[0, lanes, :], nt, preferred_element_type=F32)
            dcbr_ref[0, lanes, :] = lax.dot_general(sr[b, pl.ds(PAD, L), :].astype(BF16), dy, tn, preferred_element_type=F32)
            dcbi_ref[0, lanes, :] = -lax.dot_general(si[b, pl.ds(PAD, L), :].astype(BF16), dy, tn, preferred_element_type=F32)
        _scan_segs(ar, -ai, lr, li, er, ei, cr, cim, L, 0, True)

        def da_step(j, carry):
            rows = pl.ds(pl.multiple_of(S5_SEG * j, S5_SEG), S5_SEG)
            out = []
            for b in range(NB):
                dar, dai = carry[b]
                pr_, pi_ = sr[b, rows, :], si[b, rows, :]
                gr, gi = lr[b, rows, :], li[b, rows, :]
                out.append((dar + (gr * pr_ + gi * pi_), dai + (gi * pr_ - gr * pi_)))
            return tuple(out)

        z8 = jnp.zeros((S5_SEG, 128), F32)
        acc = lax.fori_loop(0, S, da_step, tuple((z8, z8) for _ in range(NB)), unroll=4)
        du = dup[...]
        for b in range(NB):
            lanes = pl.ds(b * 128, 128)
            dar_ref[:, lanes] = jnp.sum(acc[b][0], axis=0, keepdims=True)
            dai_ref[:, lanes] = jnp.sum(acc[b][1], axis=0, keepdims=True)
            gr, gi = lr[b].astype(BF16), li[b].astype(BF16)
            du = du + (lax.dot_general(gr, wbr_ref[0, :, lanes], nt, preferred_element_type=F32)
                       + lax.dot_general(gi, wbi_ref[0, :, lanes], nt, preferred_element_type=F32))
            dwbr_ref[0, :, lanes] = lax.dot_general(u, gr, tn, preferred_element_type=F32)
            dwbi_ref[0, :, lanes] = lax.dot_general(u, gi, tn, preferred_element_type=F32)
        dup[...] = du

        def write(row, val):
            du_ref[pl.ds(row, 16), :] = val.astype(BF16)

        _from_segs(dup, L, write)

    wspec = pl.BlockSpec((1, ci, cs), lambda j: (j, 0, 0))
    aspec = pl.BlockSpec((1, cs), lambda j: (0, j))
    cspec = pl.BlockSpec((1, cs, ci), lambda j: (j, 0, 0))
    col = pl.BlockSpec((L, ci), lambda j: (0, j))
    return pl.pallas_call(
        body, name="s5_core_bwd", grid=(nb,),
        in_specs=[pl.BlockSpec((L, ci), lambda j: (0, OFF_US // ci + j)), wspec, wspec, aspec, aspec, cspec, cspec, col, col],
        out_specs=[col, wspec, wspec, cspec, cspec, aspec, aspec],
        out_shape=[_sds((L, nb * ci), BF16), _sds(wbr.shape), _sds(wbr.shape), _sds(cbr.shape), _sds(cbr.shape),
                   _sds((1, nb * cs)), _sds((1, nb * cs))],
        scratch_shapes=(_s5_seg_scratch(L, cs, PAD) + [pltpu.VMEM((NB, L, 128), F32), pltpu.VMEM((NB, L, 128), F32)]
                        + [pltpu.VMEM((L, ci), F32) for _ in range(3)]),
        compiler_params=_cp(("arbitrary",)),
    )(proj, wbr, wbi, a_re, a_im, cbr, cbi, dyc, du1)


def _s5_post_math(yc, u, z, d, wg):
    y = yc + d * u
    y1 = jax.nn.gelu(y)
    t = jnp.dot(y1.astype(BF16), wg, preferred_element_type=F32)
    sg = _sigmoid(t)
    return y, y1, sg


def _s5_post_fwd(yc, proj, d, wg):
    L, W = yc.shape
    tm = min(256, L)

    def body(yc_ref, u_ref, z_ref, d_ref, wg_ref, o_ref):
        _, y1, sg = _s5_post_math(yc_ref[...], u_ref[...], z_ref[...], d_ref[...], wg_ref[...])
        o_ref[...] = (y1 * sg * _silu(z_ref[...])).astype(BF16)

    row = pl.BlockSpec((tm, W), lambda i: (i, 0))
    return pl.pallas_call(
        body, name="s5_post_fwd", grid=(L // tm,),
        in_specs=[row, pl.BlockSpec((tm, W), lambda i: (i, OFF_US // W)), pl.BlockSpec((tm, W), lambda i: (i, OFF_ZS // W)),
                  pl.BlockSpec((1, W), lambda i: (0, 0)), pl.BlockSpec((W, W), lambda i: (0, 0))],
        out_specs=row, out_shape=_sds((L, W), BF16), compiler_params=_cp(("parallel",)),
    )(yc, proj, proj, d, wg)


def _s5_post_bwd(yc, proj, d, wg, dout):
    L, W = yc.shape
    tm = min(256, L)

    def body(yc_ref, u_ref, z_ref, d_ref, wg_ref, do_ref, dyc_ref, du_ref, dz_ref, dd_ref, dwg_ref):
        i = pl.program_id(0)
        u, z, d_, wgv = u_ref[...], z_ref[...], d_ref[...], wg_ref[...]
        y, y1, sg = _s5_post_math(yc_ref[...], u, z, d_, wgv)
        dout_ = do_ref[...]
        y2 = y1 * sg
        dy2 = dout_ * _silu(z)
        dz_ref[...] = (dout_ * y2 * _dsilu(z)).astype(BF16)
        dt = (dy2 * y1 * sg * (1.0 - sg)).astype(BF16)
        dy1 = dy2 * sg + lax.dot_general(dt, wgv, (((1,), (1,)), ((), ())), preferred_element_type=F32)
        _, gelu_vjp = jax.vjp(jax.nn.gelu, y)
        dy = gelu_vjp(dy1)[0]
        dyc_ref[...] = dy
        du_ref[...] = dy * d_
        dd_part = jnp.sum(dy * u, axis=0, keepdims=True)
        dwg_part = lax.dot_general(y1.astype(BF16), dt, (((0,), (0,)), ((), ())), preferred_element_type=F32)

        @pl.when(i == 0)
        def _():
            dd_ref[...] = dd_part
            dwg_ref[...] = dwg_part

        @pl.when(i > 0)
        def _():
            dd_ref[...] += dd_part
            dwg_ref[...] += dwg_part

    row = pl.BlockSpec((tm, W), lambda i: (i, 0))
    return pl.pallas_call(
        body, name="s5_post_bwd", grid=(L // tm,),
        in_specs=[row, pl.BlockSpec((tm, W), lambda i: (i, OFF_US // W)), pl.BlockSpec((tm, W), lambda i: (i, OFF_ZS // W)),
                  pl.BlockSpec((1, W), lambda i: (0, 0)), pl.BlockSpec((W, W), lambda i: (0, 0)), row],
        out_specs=[row, row, row, pl.BlockSpec((1, W), lambda i: (0, 0)), pl.BlockSpec((W, W), lambda i: (0, 0))],
        out_shape=[_sds((L, W)), _sds((L, W)), _sds((L, W), BF16), _sds((1, W)), _sds((W, W))],
        compiler_params=_cp(("arbitrary",)),
    )(yc, proj, proj, d, wg, dout)


def _shift_down(x, s):
    if s == 0:
        return x
    rows = lax.broadcasted_iota(jnp.int32, x.shape, 0)
    return jnp.where(rows >= s, pltpu.roll(x, s, 0), 0.0)


def _shift_up(x, s):
    if s == 0:
        return x
    L = x.shape[0]
    rows = lax.broadcasted_iota(jnp.int32, x.shape, 0)
    return jnp.where(rows < L - s, pltpu.roll(x, L - s, 0), 0.0)


def _conv_pre(x, w):
    acc = w[CONV_K - 1:CONV_K, :] * x
    for s in range(1, CONV_K):
        acc = acc + w[CONV_K - 1 - s:CONV_K - s, :] * _shift_down(x, s)
    return acc


def _dn_conv_fwd(proj, conv_w):
    L = proj.shape[0]
    W = DN_HEAD_DIM
    nq = 2 * DN_HEADS

    def body(x_ref, w_ref, o_ref):
        j = pl.program_id(0)
        act = _silu(_conv_pre(x_ref[...], w_ref[...]))
        r = lax.rsqrt(jnp.sum(act * act, axis=-1, keepdims=True) + EPS)
        scale = jnp.where(j < DN_HEADS, DN_HEAD_DIM ** -0.5, 1.0)
        o_ref[...] = jnp.where(j < nq, act * r * scale, act)

    return pl.pallas_call(
        body, name="dn_conv_fwd", grid=(3 * DN_HEADS,),
        in_specs=[pl.BlockSpec((L, W), lambda j: (0, OFF_Q // W + j)), pl.BlockSpec((CONV_K, W), lambda j: (0, j))],
        out_specs=pl.BlockSpec((L, W), lambda j: (0, j)), out_shape=_sds((L, 3 * D_DN)), compiler_params=_cp(("parallel",)),
    )(proj, conv_w)


def _dn_conv_bwd(proj, conv_w, dout):
    L = proj.shape[0]
    W = DN_HEAD_DIM
    nq = 2 * DN_HEADS

    def body(x_ref, w_ref, do_ref, dx_ref, dw_ref):
        j = pl.program_id(0)
        x, w, dout_ = x_ref[...], w_ref[...], do_ref[...]
        pre = _conv_pre(x, w)
        act = _silu(pre)
        r = lax.rsqrt(jnp.sum(act * act, axis=-1, keepdims=True) + EPS)
        scale = jnp.where(j < DN_HEADS, DN_HEAD_DIM ** -0.5, 1.0)
        g = dout_ * scale
        dact_n = r * g - act * (r * r * r) * jnp.sum(g * act, axis=-1, keepdims=True)
        dact = jnp.where(j < nq, dact_n, dout_)
        dpre = dact * _dsilu(pre)
        dx = w[CONV_K - 1:CONV_K, :] * dpre
        for s in range(1, CONV_K):
            dx = dx + w[CONV_K - 1 - s:CONV_K - s, :] * _shift_up(dpre, s)
        dx_ref[...] = dx.astype(BF16)
        for s in range(CONV_K):
            dw_ref[pl.ds(CONV_K - 1 - s, 1), :] = jnp.sum(dpre * _shift_down(x, s), axis=0, keepdims=True)

    col = pl.BlockSpec((L, W), lambda j: (0, j))
    wsp = pl.BlockSpec((CONV_K, W), lambda j: (0, j))
    return pl.pallas_call(
        body, name="dn_conv_bwd", grid=(3 * DN_HEADS,),
        in_specs=[pl.BlockSpec((L, W), lambda j: (0, OFF_Q // W + j)), wsp, col], out_specs=[col, wsp],
        out_shape=[_sds((L, 3 * D_DN), BF16), _sds((CONV_K, 3 * D_DN))], compiler_params=_cp(("parallel",)),
    )(proj, conv_w, dout)


def _softplus(x):
    return jnp.maximum(x, 0.0) + jnp.log(1.0 + jnp.exp(-jnp.abs(x)))


def _dn_gates_fwd(proj, alog, dtb):
    L = proj.shape[0]
    W = 128

    def body(p_ref, al_ref, db_ref, o_ref):
        p = p_ref[...]
        lane = lax.broadcasted_iota(jnp.int32, p.shape, 1)
        g = -jnp.exp(al_ref[...]) * _softplus(p + db_ref[...])
        o_ref[...] = jnp.where(lane < DN_HEADS, _sigmoid(p), jnp.where(lane < 2 * DN_HEADS, g, 0.0))

    return pl.pallas_call(
        body, name="dn_gates_fwd", grid=(1,),
        in_specs=[pl.BlockSpec((L, W), lambda i: (0, OFF_B // W)), pl.BlockSpec((1, W), lambda i: (0, 0)),
                  pl.BlockSpec((1, W), lambda i: (0, 0))],
        out_specs=pl.BlockSpec((L, W), lambda i: (0, 0)), out_shape=_sds((L, W)), compiler_params=_cp(("arbitrary",)),
    )(proj, alog, dtb)


def _dn_gates_bwd(proj, alog, dtb, dgates):
    L = proj.shape[0]
    W = 128

    def body(p_ref, al_ref, db_ref, dg_ref, dp_ref, dal_ref, ddb_ref):
        p, dg = p_ref[...], dg_ref[...]
        lane = lax.broadcasted_iota(jnp.int32, p.shape, 1)
        is_g = jnp.logical_and(lane >= DN_HEADS, lane < 2 * DN_HEADS)
        beta = _sigmoid(p)
        na = -jnp.exp(al_ref[...])
        xs = p + db_ref[...]
        dsp = dg * na * _sigmoid(xs)
        dp_ref[...] = jnp.where(lane < DN_HEADS, dg * beta * (1.0 - beta), jnp.where(is_g, dsp, 0.0)).astype(BF16)
        dal_ref[...] = jnp.sum(jnp.where(is_g, dg * na * _softplus(xs), 0.0), axis=0, keepdims=True)
        ddb_ref[...] = jnp.sum(jnp.where(is_g, dsp, 0.0), axis=0, keepdims=True)

    one = pl.BlockSpec((1, W), lambda i: (0, 0))
    full = pl.BlockSpec((L, W), lambda i: (0, 0))
    return pl.pallas_call(
        body, name="dn_gates_bwd", grid=(1,),
        in_specs=[pl.BlockSpec((L, W), lambda i: (0, OFF_B // W)), one, one, full], out_specs=[full, one, one],
        out_shape=[_sds((L, W), BF16), _sds((1, W)), _sds((1, W))], compiler_params=_cp(("arbitrary",)),
    )(proj, alog, dtb, dgates)


def _bdot(a, b, dims):
    return lax.dot_general(a.astype(BF16), b.astype(BF16), (dims, ((), ())), preferred_element_type=F32)


_NN, _NT, _TN = ((1,), (0,)), ((1,), (1,)), ((0,), (0,))


def _dot3(a, b, dims):
    ah, bh = a.astype(BF16), b.astype(BF16)
    al, bl = (a - ah.astype(F32)).astype(BF16), (b - bh.astype(F32)).astype(BF16)
    (ca,), (cb,) = dims
    a3 = jnp.concatenate([ah, ah, al], axis=ca)
    b3 = jnp.concatenate([bh, bl, bh], axis=cb)
    return lax.dot_general(a3, b3, (dims, ((), ())), preferred_element_type=F32)


def _mm_family(raw):
    nn = jax.custom_vjp(lambda a, b: raw(a, b, _NN))
    nt = jax.custom_vjp(lambda a, b: raw(a, b, _NT))
    tn = jax.custom_vjp(lambda a, b: raw(a, b, _TN))
    nn.defvjp(lambda a, b: (raw(a, b, _NN), (a, b)), lambda r, g: (raw(g, r[1], _NT), raw(r[0], g, _TN)))
    nt.defvjp(lambda a, b: (raw(a, b, _NT), (a, b)), lambda r, g: (raw(g, r[1], _NN), raw(g, r[0], _TN)))
    tn.defvjp(lambda a, b: (raw(a, b, _TN), (a, b)), lambda r, g: (raw(r[1], g, _NT), raw(r[0], g, _NN)))
    return nn, nt, tn


_mm_nn, _mm_nt, _mm_tn = _mm_family(_bdot)
_m3_nn, _m3_nt, _m3_tn = _mm_family(_dot3)


def _tri_apply(x, upper):
    C = x.shape[0]
    ii = lax.broadcasted_iota(jnp.int32, (C, 3 * C), 0)
    jj = lax.broadcasted_iota(jnp.int32, (C, 3 * C), 1) % C
    mat = ((ii <= jj) if upper else (ii >= jj)).astype(BF16)
    hi = x.astype(BF16)
    r = x - hi.astype(F32)
    mid = r.astype(BF16)
    lo = (r - mid.astype(F32)).astype(BF16)
    return jnp.dot(mat, jnp.concatenate([hi, mid, lo], axis=0), preferred_element_type=F32)


_cumsum_rows = jax.custom_vjp(lambda x: _tri_apply(x, False))
_cumsum_rows.defvjp(lambda x: (_tri_apply(x, False), None), lambda _, g: (_tri_apply(g, True),))


def _uli(a_s):
    C = a_s[0].shape[0]
    ii = lax.broadcasted_iota(jnp.int32, (C, C), 0)
    jj = lax.broadcasted_iota(jnp.int32, (C, C), 1)
    eye = jnp.where(ii == jj, 1.0, 0.0)
    ts = [eye - a for a in a_s]
    ms = list(a_s)
    for _ in range(int(math.log2(C)) - 1):
        ms = [_dot3(m, m, _NN) for m in ms]
        ts = [t + _dot3(t, m, _NN) for t, m in zip(ts, ms)]
    return tuple(ts)


def _uli_bwd(ts, gs):
    xs = [_dot3(t, g, _TN) for t, g in zip(ts, gs)]
    return (tuple(-_dot3(x, t, _NT) for x, t in zip(xs, ts)),)


_unit_lower_inverse = jax.custom_vjp(_uli)
_unit_lower_inverse.defvjp(lambda a_s: (lambda ts: (ts, ts))(_uli(a_s)), _uli_bwd)


def _prep_math(qs, ks, vs, gcols, bcols):
    n = len(qs)
    C, dv = vs[0].shape
    ii = lax.broadcasted_iota(jnp.int32, (C, C), 0)
    jj = lax.broadcasted_iota(jnp.int32, (C, C), 1)
    causal = ii >= jj
    strict = ii > jj
    sf = strict.astype(F32)
    ones = jnp.ones((C, dv), F32)
    dms = [_cumsum_rows(g * sf) for g in gcols]
    gcbs = [_cumsum_rows(g * ones) for g in gcols]
    kks = [_mm_nt(k, k) for k in ks]
    qks = [_mm_nt(q, k) for q, k in zip(qs, ks)]
    decays = [jnp.where(causal, jnp.exp(jnp.where(causal, dm, 0.0)), 0.0) for dm in dms]
    glasts = [jnp.sum(g * ones, axis=0, keepdims=True) for g in gcols]
    egs = [jnp.exp(gcb) for gcb in gcbs]
    ts = _unit_lower_inverse(tuple(jnp.where(strict, b * kk * dc, 0.0) for b, kk, dc in zip(bcols, kks, decays)))
    us = [_m3_nn(t, v * b) for t, v, b in zip(ts, vs, bcols)]
    ws = [_m3_nn(t, k * b * eg) for t, k, b, eg in zip(ts, ks, bcols, egs)]
    return tuple((us[i], ws[i], qs[i] * egs[i], ks[i] * jnp.exp(glasts[i] - gcbs[i]), qks[i] * decays[i],
                  jnp.exp(glasts[i])) for i in range(n))


def _gate_cols(gates, h):
    lane = lax.broadcasted_iota(jnp.int32, gates.shape, 1)
    bcol = jnp.sum(jnp.where(lane == h, gates, 0.0), axis=1, keepdims=True)
    gcol = jnp.sum(jnp.where(lane == h + DN_HEADS, gates, 0.0), axis=1, keepdims=True)
    return gcol, bcol


DN_HB = 8


def _dn_prep_fwd(qkv, gates):
    L = qkv.shape[0]
    N, H, d, HB = L // CHUNK, DN_HEADS, DN_HEAD_DIM, DN_HB

    def body(q_ref, k_ref, v_ref, g_ref, u_ref, w_ref, qd_ref, kd_ref, qk_ref, egl_ref):
        h0 = pl.program_id(1) * HB
        gates_ = g_ref[...]
        lanes_of = [pl.ds(i * d, d) for i in range(HB)]
        cols = [_gate_cols(gates_, h0 + i) for i in range(HB)]
        outs = _prep_math([q_ref[:, l] for l in lanes_of], [k_ref[:, l] for l in lanes_of], [v_ref[:, l] for l in lanes_of],
                          [c[0] for c in cols], [c[1] for c in cols])
        for i in range(HB):
            lanes = lanes_of[i]
            u, w, qd, kd, qk, egl = outs[i]
            u_ref[:, lanes] = u
            w_ref[:, lanes] = w
            qd_ref[:, lanes] = qd
            kd_ref[:, lanes] = kd
            qk_ref[0, i] = qk
            egl_ref[0, i] = jnp.broadcast_to(egl, (8, d))

    blk = lambda off: pl.BlockSpec((CHUNK, HB * d), lambda n, j: (n, off // HB + j))
    cc = pl.BlockSpec((1, HB, CHUNK, CHUNK), lambda n, j: (n, j, 0, 0))
    ee = pl.BlockSpec((1, HB, 8, d), lambda n, j: (n, j, 0, 0))
    big = _sds((L, D_DN))
    return pl.pallas_call(
        body, name="dn_prep_fwd", grid=(N, H // HB),
        in_specs=[blk(0), blk(H), blk(2 * H), pl.BlockSpec((CHUNK, 128), lambda n, j: (n, 0))],
        out_specs=[blk(0), blk(0), blk(0), blk(0), cc, ee],
        out_shape=[big, big, big, big, _sds((N, H, CHUNK, CHUNK)), _sds((N, H, 8, d))],
        compiler_params=_cp(("parallel", "parallel")),
    )(qkv, qkv, qkv, gates)


def _dn_scan_fwd(u, w, qd, kd, qk, egl):
    L = u.shape[0]
    N, H, d, HB = L // CHUNK, DN_HEADS, DN_HEAD_DIM, DN_HB

    def body(u_ref, w_ref, qd_ref, kd_ref, qk_ref, egl_ref, o_ref, st_ref, s_ref):
        n, h0 = pl.program_id(0), pl.program_id(1) * HB

        @pl.when(n == 0)
        def _():
            for i in range(HB):
                s_ref[h0 + i] = jnp.zeros((d, d), F32)

        hs = range(HB)
        ln = [pl.ds(i * d, d) for i in hs]
        st = [s_ref[h0 + i] for i in hs]
        ws = [_bdot(w_ref[:, ln[i]], st[i], _NN) for i in hs]
        qs = [_bdot(qd_ref[:, ln[i]], st[i], _NN) for i in hs]
        vn = [u_ref[:, ln[i]] - ws[i] for i in hs]
        qv = [_bdot(qk_ref[0, i], vn[i], _NN) for i in hs]
        kv = [_bdot(kd_ref[:, ln[i]], vn[i], _TN) for i in hs]
        for i in hs:
            st_ref[0, i] = st[i]
            o_ref[:, ln[i]] = qs[i] + qv[i]
            s_ref[h0 + i] = st[i] * egl_ref[0, i, pl.ds(0, 1), :] + kv[i]

    blk = pl.BlockSpec((CHUNK, HB * d), lambda n, j: (n, j))
    cc = pl.BlockSpec((1, HB, CHUNK, CHUNK), lambda n, j: (n, j, 0, 0))
    ee = pl.BlockSpec((1, HB, 8, d), lambda n, j: (n, j, 0, 0))
    return pl.pallas_call(
        body, name="dn_scan_fwd", grid=(N, H // HB), in_specs=[blk, blk, blk, blk, cc, ee],
        out_specs=[blk, pl.BlockSpec((1, HB, d, d), lambda n, j: (n, j, 0, 0))],
        out_shape=[_sds((L, D_DN)), _sds((N, H, d, d))], scratch_shapes=[pltpu.VMEM((H, d, d), F32)],
        compiler_params=_cp(("arbitrary", "arbitrary")),
    )(u, w, qd, kd, qk, egl)


def _dn_scan_bwd(u, w, qd, kd, qk, egl, states, do):
    L = u.shape[0]
    N, H, d, HB = L // CHUNK, DN_HEADS, DN_HEAD_DIM, DN_HB

    def body(u_ref, w_ref, qd_ref, kd_ref, qk_ref, egl_ref, st_ref, do_ref,
             du_ref, dw_ref, dqd_ref, dkd_ref, dqk_ref, degl_ref, ds_ref):
        n, h0 = pl.program_id(0), pl.program_id(1) * HB

        @pl.when(n == 0)
        def _():
            for i in range(HB):
                ds_ref[h0 + i] = jnp.zeros((d, d), F32)

        hs = range(HB)
        ln = [pl.ds(i * d, d) for i in hs]
        st = [st_ref[0, i] for i in hs]
        dsn = [ds_ref[h0 + i] for i in hs]
        do_ = [do_ref[:, ln[i]] for i in hs]
        ws = [_bdot(w_ref[:, ln[i]], st[i], _NN) for i in hs]
        d1 = [_bdot(qk_ref[0, i], do_[i], _TN) for i in hs]
        d2 = [_bdot(kd_ref[:, ln[i]], dsn[i], _NN) for i in hs]
        dqd = [_bdot(do_[i], st[i], _NT) for i in hs]
        qdo = [_bdot(qd_ref[:, ln[i]], do_[i], _TN) for i in hs]
        vn = [u_ref[:, ln[i]] - ws[i] for i in hs]
        dvn = [d1[i] + d2[i] for i in hs]
        dw = [_bdot(dvn[i], st[i], _NT) for i in hs]
        dkd = [_bdot(vn[i], dsn[i], _NT) for i in hs]
        dqk = [_bdot(do_[i], vn[i], _NT) for i in hs]
        wdv = [_bdot(w_ref[:, ln[i]], dvn[i], _TN) for i in hs]
        for i in hs:
            du_ref[:, ln[i]] = dvn[i]
            dw_ref[:, ln[i]] = -dw[i]
            dqd_ref[:, ln[i]] = dqd[i]
            dkd_ref[:, ln[i]] = dkd[i]
            dqk_ref[0, i] = dqk[i]
            degl_ref[0, i] = jnp.broadcast_to(jnp.sum(dsn[i] * st[i], keepdims=True), (8, d))
            ds_ref[h0 + i] = (qdo[i] - wdv[i]) + dsn[i] * egl_ref[0, i, pl.ds(0, 1), :]

    blk = pl.BlockSpec((CHUNK, HB * d), lambda n, j: (N - 1 - n, j))
    cc = pl.BlockSpec((1, HB, CHUNK, CHUNK), lambda n, j: (N - 1 - n, j, 0, 0))
    ee = pl.BlockSpec((1, HB, 8, d), lambda n, j: (N - 1 - n, j, 0, 0))
    ss = pl.BlockSpec((1, HB, d, d), lambda n, j: (N - 1 - n, j, 0, 0))
    big = _sds((L, D_DN))
    return pl.pallas_call(
        body, name="dn_scan_bwd", grid=(N, H // HB), in_specs=[blk, blk, blk, blk, cc, ee, ss, blk],
        out_specs=[blk, blk, blk, blk, cc, ee],
        out_shape=[big, big, big, big, _sds((N, H, CHUNK, CHUNK)), _sds((N, H, 8, d))],
        scratch_shapes=[pltpu.VMEM((H, d, d), F32)], compiler_params=_cp(("arbitrary", "arbitrary")),
    )(u, w, qd, kd, qk, egl, states, do)


def _dn_prep_bwd(qkv, gates, du, dw, dqd, dkd, dqk, degl):
    L = qkv.shape[0]
    N, H, d, HB = L // CHUNK, DN_HEADS, DN_HEAD_DIM, DN_HB
    assert HB == H

    def body(q_ref, k_ref, v_ref, g_ref, du_ref, dw_ref, dqd_ref, dkd_ref, dqk_ref, degl_ref, dqkv_ref, dg_ref):
        j = pl.program_id(1)
        h0 = j * HB
        gates_ = g_ref[...]
        lane = lax.broadcasted_iota(jnp.int32, gates_.shape, 1)
        lane1 = lax.broadcasted_iota(jnp.int32, (1, d), 1)
        part = jnp.zeros(gates_.shape, F32)
        lanes_of = [pl.ds(i * d, d) for i in range(HB)]
        cols = [_gate_cols(gates_, h0 + i) for i in range(HB)]
        _, f = jax.vjp(_prep_math, [q_ref[:, l] for l in lanes_of], [k_ref[:, l] for l in lanes_of],
                       [v_ref[:, l] for l in lanes_of], [c[0] for c in cols], [c[1] for c in cols])
        cots = tuple((du_ref[:, l], dw_ref[:, l], dqd_ref[:, l], dkd_ref[:, l], dqk_ref[0, i],
                      jnp.where(lane1 == 0, degl_ref[0, i, pl.ds(0, 1), :], 0.0)) for i, l in enumerate(lanes_of))
        dqs, dks, dvs, dgcs, dbcs = f(cots)
        for i in range(HB):
            for s, val in enumerate((dqs[i], dks[i], dvs[i])):
                dqkv_ref[:, pl.ds((s * H + i) * d, d)] = val
            part = part + jnp.where(lane == h0 + i, dbcs[i], 0.0) + jnp.where(lane == h0 + i + DN_HEADS, dgcs[i], 0.0)

        @pl.when(j == 0)
        def _():
            dg_ref[...] = part

        @pl.when(j > 0)
        def _():
            dg_ref[...] += part

    blk = lambda off: pl.BlockSpec((CHUNK, HB * d), lambda n, j: (n, off // HB + j))
    gsp = pl.BlockSpec((CHUNK, 128), lambda n, j: (n, 0))
    cc = pl.BlockSpec((1, HB, CHUNK, CHUNK), lambda n, j: (n, j, 0, 0))
    ee = pl.BlockSpec((1, HB, 8, d), lambda n, j: (n, j, 0, 0))
    return pl.pallas_call(
        body, name="dn_prep_bwd", grid=(N, H // HB),
        in_specs=[blk(0), blk(H), blk(2 * H), gsp, blk(0), blk(0), blk(0), blk(0), cc, ee],
        out_specs=[pl.BlockSpec((CHUNK, 3 * H * d), lambda n, j: (n, 0)), gsp], out_shape=[_sds((L, 3 * D_DN)), _sds((L, 128))],
        compiler_params=_cp(("parallel", "arbitrary")),
    )(qkv, qkv, qkv, gates, du, dw, dqd, dkd, dqk, degl)


def _dn_post_fwd(o, proj, nw):
    L = o.shape[0]
    d = DN_HEAD_DIM
    tm = min(512, L)

    def body(o_ref, z_ref, w_ref, y_ref):
        ov = o_ref[...]
        r = lax.rsqrt(jnp.mean(ov * ov, axis=-1, keepdims=True) + EPS)
        y_ref[...] = (ov * r * w_ref[...] * _silu(z_ref[...])).astype(BF16)

    blk = pl.BlockSpec((tm, d), lambda i, h: (i, h))
    return pl.pallas_call(
        body, name="dn_post_fwd", grid=(L // tm, DN_HEADS),
        in_specs=[blk, pl.BlockSpec((tm, d), lambda i, h: (i, OFF_ZD // d + h)), pl.BlockSpec((1, d), lambda i, h: (0, 0))],
        out_specs=blk, out_shape=_sds((L, D_DN), BF16), compiler_params=_cp(("parallel", "parallel")),
    )(o, proj, nw)


def _dn_post_bwd(o, proj, nw, dy):
    L = o.shape[0]
    d = DN_HEAD_DIM
    tm = min(512, L)

    def body(o_ref, z_ref, w_ref, dy_ref, do_ref, dz_ref, dw_ref):
        first = jnp.logical_and(pl.program_id(0) == 0, pl.program_id(1) == 0)
        ov, z, w, dyv = o_ref[...], z_ref[...], w_ref[...], dy_ref[...]
        r = lax.rsqrt(jnp.mean(ov * ov, axis=-1, keepdims=True) + EPS)
        xn = ov * r
        dz_ref[...] = (dyv * xn * w * _dsilu(z)).astype(BF16)
        dn = dyv * _silu(z)
        t = dn * w
        do_ref[...] = r * t - ov * (r * r * r) * jnp.mean(t * ov, axis=-1, keepdims=True)
        part = jnp.sum(dn * xn, axis=0, keepdims=True)

        @pl.when(first)
        def _():
            dw_ref[...] = part

        @pl.when(jnp.logical_not(first))
        def _():
            dw_ref[...] += part

    blk = pl.BlockSpec((tm, d), lambda i, h: (i, h))
    one = pl.BlockSpec((1, d), lambda i, h: (0, 0))
    return pl.pallas_call(
        body, name="dn_post_bwd", grid=(L // tm, DN_HEADS),
        in_specs=[blk, pl.BlockSpec((tm, d), lambda i, h: (i, OFF_ZD // d + h)), one, blk], out_specs=[blk, blk, one],
        out_shape=[_sds((L, D_DN)), _sds((L, D_DN), BF16), _sds((1, d))], compiler_params=_cp(("arbitrary", "arbitrary")),
    )(o, proj, nw, dy)


def _mix_fwd(s5o, dno, w_su, w_du, proj):
    L, K = s5o.shape
    N = w_su.shape[1]
    tm, tn = min(512, L), 512

    def body(a1, a2, b1, b2, gs, gd, ys_ref, yd_ref, mx_ref):
        ys = jnp.dot(a1[...], b1[...], preferred_element_type=F32)
        yd = jnp.dot(a2[...], b2[...], preferred_element_type=F32)
        ys_ref[...] = ys
        yd_ref[...] = yd
        mx_ref[...] = (_sigmoid(gs[...]) * ys + _sigmoid(gd[...]) * yd).astype(BF16)

    a = pl.BlockSpec((tm, K), lambda i, j: (i, 0))
    b = pl.BlockSpec((K, tn), lambda i, j: (0, j))
    o = pl.BlockSpec((tm, tn), lambda i, j: (i, j))
    return pl.pallas_call(
        body, name="mix_fwd", grid=(L // tm, N // tn),
        in_specs=[a, a, b, b, pl.BlockSpec((tm, tn), lambda i, j: (i, OFF_GS // tn + j)),
                  pl.BlockSpec((tm, tn), lambda i, j: (i, OFF_GD // tn + j))],
        out_specs=[o, o, o], out_shape=[_sds((L, N)), _sds((L, N)), _sds((L, N), BF16)],
        compiler_params=_cp(("parallel", "parallel")),
    )(s5o, dno, w_su, w_du, proj, proj)


def _mix_bwd(dx2b, w_out, proj, ys, yd):
    L, K = dx2b.shape
    N = w_out.shape[0]
    tm, tn = min(512, L), 512

    def body(a, b, gs, gd, ys_ref, yd_ref, dgs_ref, dgd_ref, dys_ref, dyd_ref):
        dm = lax.dot_general(a[...], b[...], (((1,), (1,)), ((), ())), preferred_element_type=F32)
        ss, sd = _sigmoid(gs[...]), _sigmoid(gd[...])
        dys_ref[...] = (dm * ss).astype(BF16)
        dyd_ref[...] = (dm * sd).astype(BF16)
        dgs_ref[...] = (dm * ys_ref[...] * ss * (1.0 - ss)).astype(BF16)
        dgd_ref[...] = (dm * yd_ref[...] * sd * (1.0 - sd)).astype(BF16)

    o = pl.BlockSpec((tm, tn), lambda i, j: (i, j))
    return pl.pallas_call(
        body, name="mix_bwd", grid=(L // tm, N // tn),
        in_specs=[pl.BlockSpec((tm, K), lambda i, j: (i, 0)), pl.BlockSpec((tn, K), lambda i, j: (j, 0)),
                  pl.BlockSpec((tm, tn), lambda i, j: (i, OFF_GS // tn + j)),
                  pl.BlockSpec((tm, tn), lambda i, j: (i, OFF_GD // tn + j)), o, o],
        out_specs=[o, o, o, o], out_shape=[_sds((L, N), BF16)] * 4, compiler_params=_cp(("parallel", "parallel")),
    )(dx2b, w_out, proj, proj, ys, yd)


def _final(mixed, w_out, x, tgt, fw):
    L, D = x.shape
    tm = min(256, L)

    def body(a_ref, b_ref, x_ref, t_ref, w_ref, dx_ref, dxb_ref, loss_ref, dw_ref):
        i = pl.program_id(0)
        x2 = x_ref[...] + jnp.dot(a_ref[...], b_ref[...], preferred_element_type=F32)
        w = w_ref[...]
        r = lax.rsqrt(jnp.mean(x2 * x2, axis=-1, keepdims=True) + EPS)
        xn = x2 * r
        e = xn * w - t_ref[...]
        lpart = 0.5 * jnp.sum(jnp.mean(e * e, axis=-1, keepdims=True), axis=0, keepdims=True)
        dy = e * (1.0 / D)
        t = dy * w
        dx2 = r * t - x2 * (r * r * r) * jnp.mean(t * x2, axis=-1, keepdims=True)
        dx_ref[...] = dx2
        dxb_ref[...] = dx2.astype(BF16)
        dwp = jnp.sum(dy * xn, axis=0, keepdims=True)
        lrow = jnp.broadcast_to(lpart, loss_ref.shape)

        @pl.when(i == 0)
        def _():
            loss_ref[...] = lrow
            dw_ref[...] = dwp

        @pl.when(i > 0)
        def _():
            loss_ref[...] += lrow
            dw_ref[...] += dwp

    row = pl.BlockSpec((tm, D), lambda i: (i, 0))
    one = pl.BlockSpec((1, D), lambda i: (0, 0))
    return pl.pallas_call(
        body, name="final", grid=(L // tm,),
        in_specs=[row, pl.BlockSpec((D, D), lambda i: (0, 0)), row, row, one],
        out_specs=[row, row, pl.BlockSpec((1, 128), lambda i: (0, 0)), one],
        out_shape=[_sds((L, D)), _sds((L, D), BF16), _sds((1, 128)), _sds((1, D))], compiler_params=_cp(("arbitrary",)),
    )(mixed, w_out, x, tgt, fw)


def _block_diag(t):
    J, g, a, b = t.shape
    eye = jnp.eye(g, dtype=t.dtype)
    return (t[:, :, :, None, :] * eye[None, :, None, :, None]).reshape(J, g * a, g * b)


def _block_diag_take(m, g):
    J, ga, gb = m.shape
    a, b = ga // g, gb // g
    m5 = m.reshape(J, g, a, g, b)
    idx = jnp.arange(g)
    return m5[:, idx, :, idx, :].transpose(1, 0, 2, 3)


class _PlainOps:
    def __init__(self, w_rest):
        self.w_rest = w_rest

    def in_proj(self, h, wt_perm):
        return _mm(h, wt_perm, tb=True, name="in_proj", tm=1024, tn=1152), self.w_rest[:3]

    def s5_core(self, *args):
        return _s5_core_fwd(*args), self.w_rest[3]

    def rest_grads(self, grads, twins):
        pass

    def d_w_in(self, h, dproj):
        return _mm(dproj, h, ta=True, name="d_w_in", tm=1152, tn=1024, twin=True)

    def d_h(self, dproj, wt_perm, d_wt_perm, d_wt_twin):
        return _mm(dproj, wt_perm, name="d_h", tm=2048, tn=1024, tk=1152)


def _local_step(x, tgt, ln_w, w_perm, lam_re, lam_im, log_step, b_re, b_im, c_re, c_im, s5_d,
                conv_w, a_log, dt_bias, norm_w, fw, ops):
    G, P, gb = S5_GROUPS, S5_STATE, S5_GROUPS // S5_BLOCKS
    h, rstd = _ln_fwd(x, ln_w)
    proj, (w_glu, w_su, w_du) = ops.in_proj(h, w_perm)

    b_re2, b_im2 = b_re.reshape(G, P * S5_GROUP), b_im.reshape(G, P * S5_GROUP)
    ls2 = log_step.reshape(G, 1)
    abar_re, abar_im, bb_re, bb_im = _s5_param_fwd(lam_re, lam_im, ls2, b_re2, b_im2)

    def to_wb(bb):
        return _block_diag(bb.reshape(S5_BLOCKS, gb, P, S5_GROUP).transpose(0, 1, 3, 2)).astype(BF16)

    def to_cb(cc):
        return _block_diag(cc.reshape(S5_BLOCKS, gb, S5_GROUP, P).transpose(0, 1, 3, 2)).astype(BF16)

    wbr, wbi, cbr, cbi = to_wb(bb_re), to_wb(bb_im), to_cb(c_re), to_cb(c_im)
    a_re_row, a_im_row = abar_re.reshape(1, G * P), abar_im.reshape(1, G * P)
    yc, w_out = ops.s5_core(proj, wbr, wbi, a_re_row, a_im_row, cbr, cbi)
    s5o = _s5_post_fwd(yc, proj, s5_d, w_glu)

    pad = lambda v: jnp.pad(v, ((0, 0), (DN_HEADS, 128 - 2 * DN_HEADS)))
    alog_row, dtb_row = pad(a_log), pad(dt_bias)
    qkv = _dn_conv_fwd(proj, conv_w)
    gates = _dn_gates_fwd(proj, alog_row, dtb_row)
    prep = _dn_prep_fwd(qkv, gates)
    o_dn, states = _dn_scan_fwd(*prep)
    dno = _dn_post_fwd(o_dn, proj, norm_w)

    ys, yd, mixed = _mix_fwd(s5o, dno, w_su, w_du, proj)
    dx2, dx2b, loss_row, d_fw = _final(mixed, w_out, x, tgt, fw)
    d_w_out, d_w_out_b = _mm(mixed, dx2b, ta=True, name="d_w_out", twin=True)
    dgs, dgd, dys, dyd = _mix_bwd(dx2b, w_out, proj, ys, yd)
    d_w_su, d_w_su_b = _mm(s5o, dys, ta=True, name="d_w_su", shard_out=True, twin=True)
    d_w_du, d_w_du_b = _mm(dno, dyd, ta=True, name="d_w_du", shard_out=True, twin=True)
    ds5o = _mm(dys, w_su, tb=True, name="d_s5o")
    ddno = _mm(dyd, w_du, tb=True, name="d_dno")

    dyc, du1, dz_s, d_s5d, d_w_glu = _s5_post_bwd(yc, proj, s5_d, w_glu, ds5o)
    ops.rest_grads((d_w_glu, d_w_su, d_w_du, d_w_out), (d_w_glu.astype(BF16), d_w_su_b, d_w_du_b, d_w_out_b))
    du, dwbr, dwbi, dcbr, dcbi, dar, dai = _s5_core_bwd(proj, wbr, wbi, a_re_row, a_im_row, cbr, cbi, dyc, du1)

    def from_wb(dwb):
        return _block_diag_take(dwb, gb).transpose(0, 1, 3, 2).reshape(G, P * S5_GROUP)

    def from_cb(dcb):
        return _block_diag_take(dcb, gb).transpose(0, 1, 3, 2).reshape(G, S5_GROUP, P)

    d_lam_re, d_lam_im, d_ls, d_b_re, d_b_im = _s5_param_bwd(
        lam_re, lam_im, ls2, b_re2, b_im2, dar.reshape(G, P), dai.reshape(G, P), from_wb(dwbr), from_wb(dwbi))

    do_dn, dz_d, d_norm_w = _dn_post_bwd(o_dn, proj, norm_w, ddno)
    dqkv_act, dgates = _dn_prep_bwd(qkv, gates, *_dn_scan_bwd(*prep, states, do_dn))
    dqkv, d_conv = _dn_conv_bwd(proj, conv_w, dqkv_act)
    dpb, d_alog_row, d_dtb_row = _dn_gates_bwd(proj, alog_row, dtb_row, dgates)

    dproj = jnp.concatenate([du, dz_s, dqkv, dz_d, dgs, dgd, dpb], axis=1)
    d_w_perm, d_w_twin = ops.d_w_in(h, dproj)
    dh = ops.d_h(dproj, w_perm, d_w_perm, d_w_twin)
    grad_x, d_ln_w = _ln_bwd(x, rstd, ln_w, dh, dx2)

    grads = dict(
        ln_w=d_ln_w, w_perm=d_w_perm, s5_lam_re=d_lam_re, s5_lam_im=d_lam_im, s5_log_step=d_ls.reshape(1, G),
        s5_b_re=d_b_re.reshape(G, P, S5_GROUP), s5_b_im=d_b_im.reshape(G, P, S5_GROUP),
        s5_c_re=from_cb(dcbr), s5_c_im=from_cb(dcbi), s5_d=d_s5d, s5_w_glu=d_w_glu, s5_w_up=d_w_su,
        dn_conv_w=d_conv, dn_a_log=d_alog_row[:, DN_HEADS:2 * DN_HEADS], dn_dt_bias=d_dtb_row[:, DN_HEADS:2 * DN_HEADS],
        dn_norm_w=d_norm_w, dn_w_up=d_w_du, w_out=d_w_out, final_norm_w=d_fw)
    return loss_row, grad_x, grads


def _place():
    x, y, c = lax.axis_index("x"), lax.axis_index("y"), lax.axis_index("c")
    return x, y, c


def _remote(src, dst, send_sem, recv_sem, to):
    return pltpu.make_async_remote_copy(src_ref=src, dst_ref=dst, send_sem=send_sem, recv_sem=recv_sem,
                                        device_id=to, device_id_type=MESH)


def _gather_exchange(shards, whole=()):
    na, nw = len(shards), len(whole)

    def half_of(ref, a, half):
        rows = shards[a].shape[0]
        return ref.at[pl.ds(half * (rows // 2), rows // 2)]

    def plan(ins, outs, send_sems, recv_sems, local_sems, receiving):
        x, y, c = _place()
        me = 2 * x + y
        sibling = (x, y, 1 - c)
        chips = [(1 - x, y), (x, 1 - y), (1 - x, 1 - y)]

        def part(a, chip, half):
            return half_of(outs[a].at[chip], a, half)

        own = [pltpu.make_async_copy(ins[a], outs[a].at[me], local_sems.at[a]) for a in range(na + nw)]
        sends, landed, passed, arrivals = [], [], [], []
        for a in range(na):
            for j, (px, py) in enumerate(chips):
                k = 6 * a + j
                sends.append(_remote(half_of(ins[a], a, c), part(a, me, c), send_sems.at[k], recv_sems.at[k], (px, py, c)))
                if receiving:
                    got, other = part(a, 2 * px + py, c), part(a, 2 * px + py, 1 - c)
                    landed.append(_remote(got, got, send_sems.at[k], recv_sems.at[k], (px, py, c)))
                    passed.append(_remote(got, got, send_sems.at[k + 3], recv_sems.at[k + 3], sibling))
                    arrivals.append(_remote(other, other, send_sems.at[k + 3], recv_sems.at[k + 3], sibling))
        for a in range(na, na + nw):
            for j, (px, py) in enumerate(chips):
                k = 6 * na + 3 * (a - na) + j
                sends.append(_remote(ins[a], outs[a].at[me], send_sems.at[k], recv_sems.at[k], (px, py, c)))
                if receiving:
                    arrivals.append(_remote(ins[a], outs[a].at[2 * px + py], send_sems.at[k], recv_sems.at[k], (px, py, c)))
        return own, sends, landed, passed, arrivals

    def start(ins, outs, *sems):
        own, sends, _, _, _ = plan(ins, outs, *sems, False)
        for cp in own + sends:
            cp.start()

    def finish(ins, outs, *sems):
        own, sends, landed, passed, arrivals = plan(ins, outs, *sems, True)
        for got, fwd in zip(landed, passed):
            got.wait_recv()
            fwd.start()
        for cp in arrivals:
            cp.wait_recv()
        for cp in sends + passed:
            cp.wait_send()
        for cp in own:
            cp.wait()

    arrays = list(shards) + list(whole)
    return _Exchange(arrays, [_sds((N_CHIPS,) + s.shape, s.dtype) for s in arrays], 6 * na + 3 * nw, start, finish)


def _gather_relayed(shard, whole, name):
    rows, cols = shard.shape
    nw = len(whole)

    def body(*refs):
        in_ref, w_in = refs[0], refs[1:1 + nw]
        out_ref, w_out = refs[1 + nw], refs[2 + nw:2 + 2 * nw]
        send_sems, recv_sems, local_sems = refs[2 + 2 * nw:]
        x, y, c = _place()
        me = 2 * x + y
        near = (jnp.where(c == 1, 1 - x, x), jnp.where(c == 1, y, 1 - y))
        far = (jnp.where(c == 1, x, 1 - x), jnp.where(c == 1, 1 - y, y))
        diag = (1 - x, 1 - y)
        sibling = (x, y, 1 - c)
        chip_of = lambda p: 2 * p[0] + p[1]

        def half(ref, h):
            return ref.at[pl.ds(0, rows), pl.ds(h * (cols // 2), cols // 2)]

        own = [pltpu.make_async_copy(in_ref, out_ref.at[me], local_sems.at[0])]
        own += [pltpu.make_async_copy(w_in[a], w_out[a].at[me], local_sems.at[1 + a]) for a in range(nw)]
        others = [(1 - x, y), (x, 1 - y), (1 - x, 1 - y)]
        small = [_remote(w_in[a], w_out[a].at[me], send_sems.at[4 + 3 * a + j], recv_sems.at[4 + 3 * a + j], (*p, c))
                 for a in range(nw) for j, p in enumerate(others)]
        sends = [_remote(in_ref, out_ref.at[me], send_sems.at[0], recv_sems.at[0], (*near, c))]
        for cp in own + small + sends:
            cp.start()
        from_near = out_ref.at[chip_of(near)]
        _remote(from_near, from_near, send_sems.at[0], recv_sems.at[0], (*near, c)).wait_recv()
        sends.append(_remote(from_near, from_near, send_sems.at[1], recv_sems.at[1], sibling))
        sends[-1].start()
        from_far = out_ref.at[chip_of(far)]
        _remote(from_far, from_far, send_sems.at[1], recv_sems.at[1], sibling).wait_recv()
        sends.append(_remote(half(from_far, c), half(from_far, c), send_sems.at[2], recv_sems.at[2], (*near, c)))
        sends[-1].start()
        of_diag = out_ref.at[chip_of(diag)]
        _remote(half(of_diag, c), half(of_diag, c), send_sems.at[2], recv_sems.at[2], (*near, c)).wait_recv()
        sends.append(_remote(half(of_diag, c), half(of_diag, c), send_sems.at[3], recv_sems.at[3], sibling))
        sends[-1].start()
        _remote(half(of_diag, 1 - c), half(of_diag, 1 - c), send_sems.at[3], recv_sems.at[3], sibling).wait_recv()
        for a in range(nw):
            for j, p in enumerate(others):
                _remote(w_in[a], w_out[a].at[chip_of(p)], send_sems.at[4 + 3 * a + j], recv_sems.at[4 + 3 * a + j], (*p, c)).wait_recv()
        for cp in sends + small:
            cp.wait_send()
        for cp in own:
            cp.wait()

    arrays = [shard] + list(whole)
    n_sems = 4 + 3 * nw
    return pl.pallas_call(
        body, name=name, in_specs=[ANY] * (1 + nw), out_specs=[ANY] * (1 + nw),
        out_shape=[_sds((N_CHIPS,) + a.shape, a.dtype) for a in arrays],
        scratch_shapes=[pltpu.SemaphoreType.DMA((n_sems,)) for _ in range(3)],
    )(*arrays)


def _owners_exchange(csbs):
    na = len(csbs)

    def plan(ins, outs, send_sems, recv_sems, local_sems, receiving):
        x, y, c = _place()
        me = 2 * x + y
        sends, arrivals = [], []
        for a in range(na):
            for k in range(N_CHIPS - 1):
                j = (me + 1 + k) % N_CHIPS
                sends.append(_remote(ins[a].at[k], outs[a].at[2 - k], send_sems.at[3 * a + k], recv_sems.at[3 * a + 2 - k],
                                     (j // 2, j % 2, c)))
                if receiving:
                    arrivals.append(_remote(ins[a].at[k], outs[a].at[k], send_sems.at[3 * a + k], recv_sems.at[3 * a + k], (x, y, c)))
        return sends, arrivals

    def start(ins, outs, *sems):
        for cp in plan(ins, outs, *sems, False)[0]:
            cp.start()

    def finish(ins, outs, *sems):
        sends, arrivals = plan(ins, outs, *sems, True)
        for cp in arrivals:
            cp.wait_recv()
        for cp in sends:
            cp.wait_send()

    return _Exchange(csbs, [_sds(g.shape, g.dtype) for g in csbs], 3 * na, start, finish)


def _swap_halves(gxs, name):
    na = len(gxs)
    half_shape = lambda g: (WT_ROWS // 2, g.shape[1]) if g.ndim == 2 else g.shape[2:]

    def body(*refs):
        ins, outs = refs[:na], refs[na:2 * na]
        send_sems, recv_sems = refs[2 * na:]
        x, y, c = _place()
        cps = []
        for a in range(na):
            if gxs[a].ndim == 2:
                for j in range(N_CHIPS):
                    rows = pl.ds(pl.multiple_of(WT_WIN[j] + (1 - c) * (WT_ROWS // 2), 16), WT_ROWS // 2)
                    cps.append(_remote(ins[a].at[rows], outs[a].at[j, 0], send_sems.at[na + j], recv_sems.at[na + j], (x, y, 1 - c)))
            else:
                cps.append(_remote(ins[a].at[pl.ds(0, N_CHIPS), pl.ds(1 - c, 1)], outs[a], send_sems.at[a], recv_sems.at[a],
                                   (x, y, 1 - c)))
        for cp in cps:
            cp.start()
        for cp in cps:
            cp.wait()

    return pl.pallas_call(
        body, name=name, in_specs=[ANY] * na, out_specs=[ANY] * na,
        out_shape=[_sds((N_CHIPS, 1) + half_shape(g), g.dtype) for g in gxs],
        scratch_shapes=[pltpu.SemaphoreType.DMA((na + N_CHIPS,)), pltpu.SemaphoreType.DMA((na + N_CHIPS,))],
    )(*gxs)


def _half_block(gx, tr, shard):
    if gx.ndim == 4:
        return pl.BlockSpec((1, 1, tr, gx.shape[3]), lambda *g: (shard(*g), g[-1][0], g[-2], 0))
    return pl.BlockSpec(
        (pl.Element(tr), pl.Element(gx.shape[1])),
        lambda *g: (pl.multiple_of(g[-1][2 + shard(*g)] + g[-1][0] * (WT_ROWS // 2) + g[-2] * tr, 16), 0))


def _share_halves(gfs):
    na = len(gfs)

    def body(*refs):
        ins, outs = refs[:na], refs[na:2 * na]
        send_sems, recv_sems = refs[2 * na:]
        x, y, c = _place()
        cps = [_remote(ins[a].at[pl.ds(c, 1)], outs[a].at[pl.ds(c, 1)], send_sems.at[a], recv_sems.at[a], (x, y, 1 - c))
               for a in range(na)]
        for cp in cps:
            cp.start()
        for a in range(na):
            cps[a].wait_send()
            _remote(ins[a].at[pl.ds(1 - c, 1)], outs[a].at[pl.ds(1 - c, 1)], send_sems.at[a], recv_sems.at[a], (x, y, 1 - c)).wait_recv()

    return pl.pallas_call(
        body, name="rs_share_halves", in_specs=[ANY] * na, out_specs=[ANY] * na,
        out_shape=[_sds(g.shape, g.dtype) for g in gfs], input_output_aliases={a: a for a in range(na)},
        scratch_shapes=[pltpu.SemaphoreType.DMA((na,)), pltpu.SemaphoreType.DMA((na,))],
    )(*gfs)


def _row_tile(rows, cols, budget=5 << 18):
    fits = [t for t in range(16, rows + 1, 16) if rows % t == 0 and t * cols * 4 <= budget]
    return max(fits) if fits else rows


def _chip_sums(gx, r1, where):
    _, _, r2, cd = r1.shape
    tr = _row_tile(r2, cd)

    def body(w_ref, a_ref, b_ref, o_ref):
        o_ref[0] = (a_ref[...].reshape(tr, cd) + b_ref[0, 0].astype(F32)).astype(BF16)

    other = lambda k, i, w: (w[1] + 1 + k) % N_CHIPS
    return pl.pallas_call(
        body, name="rs_chip_sums",
        grid_spec=pltpu.PrefetchScalarGridSpec(
            num_scalar_prefetch=1, grid=(N_CHIPS - 1, r2 // tr),
            in_specs=[_half_block(gx, tr, other), pl.BlockSpec((1, 1, tr, cd), lambda k, i, w: (other(k, i, w), 0, i, 0))],
            out_specs=pl.BlockSpec((1, tr, cd), lambda k, i, w: (k, i, 0))),
        out_shape=_sds((N_CHIPS - 1, r2, cd), BF16), compiler_params=_cp(("parallel", "parallel")),
    )(where, gx, r1)


def _owner_sum(gx, r1, r2x, where):
    _, _, r2, cd = r1.shape
    tr = _row_tile(r2, cd)

    def body(w_ref, a_ref, b_ref, r_ref, o_ref):
        acc = a_ref[...].reshape(tr, cd) + b_ref[0, 0].astype(F32)
        for k in range(N_CHIPS - 1):
            acc = acc + r_ref[k].astype(F32)
        o_ref[0] = acc

    return pl.pallas_call(
        body, name="rs_owner_sum",
        grid_spec=pltpu.PrefetchScalarGridSpec(
            num_scalar_prefetch=1, grid=(r2 // tr,),
            in_specs=[_half_block(gx, tr, lambda i, w: w[1]),
                      pl.BlockSpec((1, 1, tr, cd), lambda i, w: (w[1], 0, i, 0)),
                      pl.BlockSpec((N_CHIPS - 1, tr, cd), lambda i, w: (0, i, 0))],
            out_specs=pl.BlockSpec((1, tr, cd), lambda i, w: (w[0], i, 0))),
        out_shape=_sds((2, r2, cd)), compiler_params=_cp(("parallel",)),
    )(where, gx, r1, r2x)


def _adamw_math(w, g, m, v):
    m = ADAM_B1 * m + (1.0 - ADAM_B1) * g
    v = ADAM_B2 * v + (1.0 - ADAM_B2) * (g * g)
    m_hat = m / (1.0 - ADAM_B1 ** ADAM_STEP)
    v_hat = v / (1.0 - ADAM_B2 ** ADAM_STEP)
    delta = -ADAM_LR * (m_hat / (jnp.sqrt(v_hat) + ADAM_EPS) + ADAM_WD * w)
    return delta, m, v


def _adamw(w, g, m, v, name, echo=False):
    rows, cd = w.shape
    if rows % 16 == 0:
        tr, tc = _row_tile(rows, cd, budget=3 << 19), cd
    else:
        tr, tc = rows, (128 if rows * cd * 4 > (3 << 19) else cd)
    assert rows % tr == 0 and cd % tc == 0
    n_out = 4 if echo else 3

    def body(w_ref, g_ref, m_ref, v_ref, d_ref, mo_ref, vo_ref, *go_ref):
        gv = g_ref[...]
        d, mm, vv = _adamw_math(w_ref[...], gv, m_ref[...], v_ref[...])
        d_ref[...] = d
        mo_ref[...] = mm
        vo_ref[...] = vv
        if echo:
            go_ref[0][...] = gv

    blk = pl.BlockSpec((tr, tc), lambda i, j: (i, j))
    return pl.pallas_call(
        body, name=name, grid=(rows // tr, cd // tc), in_specs=[blk] * 4, out_specs=[blk] * n_out,
        out_shape=[_sds(w.shape)] * n_out, compiler_params=_cp(("parallel", "parallel")),
    )(w, g, m, v)


def _small_allreduce(gp):
    R = gp.shape[0]
    R2 = R // 2
    assert R2 % 8 == 0

    def body(g_ref, go_ref, sib, csum, land, send_sems, recv_sems):
        x, y, c = _place()
        me = 2 * x + y
        sibling = (x, y, 1 - c)
        chips = [(1 - x, y), (x, 1 - y), (1 - x, 1 - y)]
        swap = _remote(g_ref, sib, send_sems.at[0], recv_sems.at[0], sibling)
        swap.start()
        swap.wait()
        csum[...] = g_ref[...] + sib[...]
        half = csum.at[pl.ds(c * R2, R2)]
        land[me] = csum[pl.ds(c * R2, R2), :]
        cps = [_remote(half, land.at[me], send_sems.at[1 + j], recv_sems.at[1 + j], (px, py, c))
               for j, (px, py) in enumerate(chips)]
        for cp in cps:
            cp.start()
        for j, (px, py) in enumerate(chips):
            _remote(half, land.at[2 * px + py], send_sems.at[1 + j], recv_sems.at[1 + j], (px, py, c)).wait_recv()
        for cp in cps:
            cp.wait_send()
        mine = go_ref.at[pl.ds(c * R2, R2)]
        go_ref[pl.ds(c * R2, R2), :] = (land[0] + land[1]) + (land[2] + land[3])
        share = _remote(mine, mine, send_sems.at[4], recv_sems.at[4], sibling)
        share.start()
        share.wait_send()
        other = go_ref.at[pl.ds((1 - c) * R2, R2)]
        _remote(other, other, send_sems.at[4], recv_sems.at[4], sibling).wait_recv()

    vm = pl.BlockSpec(memory_space=pltpu.VMEM)
    return pl.pallas_call(
        body, name="small_allreduce", in_specs=[vm], out_specs=vm, out_shape=_sds((R, 128)),
        scratch_shapes=[pltpu.VMEM((R, 128), F32), pltpu.VMEM((R, 128), F32), pltpu.VMEM((N_CHIPS, R2, 128), F32),
                        pltpu.SemaphoreType.DMA((5,)), pltpu.SemaphoreType.DMA((5,))],
        compiler_params=_cp(),
    )(gp)


def _adamw_many(ws, gs, ms, vs):
    n = len(ws)

    def body(*refs):
        w_r, g_r, m_r, v_r = refs[:n], refs[n:2 * n], refs[2 * n:3 * n], refs[3 * n:4 * n]
        d_r, mo_r, vo_r = refs[4 * n:5 * n], refs[5 * n:6 * n], refs[6 * n:]
        for i in range(n):
            d_r[i][...], mo_r[i][...], vo_r[i][...] = _adamw_math(w_r[i][...], g_r[i][...], m_r[i][...], v_r[i][...])

    vm = pl.BlockSpec(memory_space=pltpu.VMEM)
    shapes = [_sds(a.shape) for a in ws]
    outs = pl.pallas_call(
        body, name="adamw_small", in_specs=[vm] * (4 * n), out_specs=[vm] * (3 * n), out_shape=shapes * 3, compiler_params=_cp(),
    )(*ws, *gs, *ms, *vs)
    return outs[:n], outs[n:2 * n], outs[2 * n:]


def _pack(arrs):
    rows = []
    for a in arrs:
        f = a.reshape(-1)
        f = jnp.pad(f, (0, (-f.shape[0]) % 128))
        rows.append(f.reshape(-1, 128))
    p = jnp.concatenate(rows, axis=0)
    return jnp.pad(p, ((0, (-p.shape[0]) % 8), (0, 0)))


def _unpack(p, shapes):
    out, r = [], 0
    for s in shapes:
        n = math.prod(s)
        nr = -(-n // 128)
        out.append(p[r:r + nr].reshape(-1)[:n].reshape(s))
        r += nr
    return out


class _ExchangeOps(_PlainOps):
    def __init__(self, rest_shards, where):
        self.rest_shards, self.where = rest_shards, where
        self.reduced = []

    def in_proj(self, h, wt_perm):
        proj, g_glu, g_su, g_du = _mm(h, wt_perm, tb=True, name="in_proj", tm=1024, tn=1152,
                                      exchange=_gather_exchange(self.rest_shards[:3]))
        cat = lambda g: jnp.concatenate([g[j] for j in range(N_CHIPS)], axis=1)
        return proj, (g_glu.reshape(D_S5, D_S5), cat(g_su), cat(g_du))

    def s5_core(self, *args):
        yc, g_out = _s5_core_fwd(*args, exchange=_gather_exchange(self.rest_shards[3:]))
        return yc, g_out.reshape(D_MODEL, D_MODEL)

    def _chip_sums(self, gxs, twins, name):
        r1s = _swap_halves(twins, name)
        return r1s, [_chip_sums(gx, r1, self.where) for gx, r1 in zip(gxs, r1s)]

    def rest_grads(self, grads, twins):
        shapes = [(N_CHIPS, 2, D_S5 // 8, D_S5), (N_CHIPS, 2, D_S5 // 2, D_MODEL // N_CHIPS),
                  (N_CHIPS, 2, D_DN // 2, D_MODEL // N_CHIPS), (N_CHIPS, 2, D_MODEL // 8, D_MODEL)]
        gxs = [g.reshape(s) for g, s in zip(grads, shapes)]
        r1s, csbs = self._chip_sums(gxs, [t.reshape(s) for t, s in zip(twins, shapes)], "rs_swap_rest")
        self.rest = (gxs, r1s, csbs)

    def d_w_in(self, h, dproj):
        gxs, r1s, csbs = self.rest
        d_wt_perm, twin, *r2s = _mm(dproj, h, ta=True, name="d_w_in", tm=1152, tn=1024, twin=True,
                                    exchange=_owners_exchange(csbs))
        self.reduced = list(zip(gxs, r1s, r2s))
        return d_wt_perm, twin

    def d_h(self, dproj, wt_perm, d_wt_perm, d_wt_twin):
        self.beta_a = d_wt_perm[OFF_B:OFF_B + WT_NB]
        (r1,), (csb,) = self._chip_sums([d_wt_perm], [d_wt_twin], "rs_swap_w_in")
        dh, r2 = _mm(dproj, wt_perm, name="d_h", tm=2048, tn=1024, tk=1152, exchange=_owners_exchange([csb]))
        self.reduced = [(d_wt_perm, r1, r2)] + self.reduced
        return dh


WT_SHARD = D_IN // N_CHIPS
WT_NB = 2 * DN_HEADS
WT_B, WT_LO = divmod(OFF_GS, WT_SHARD)
WT_FIRST = [i * WT_SHARD - (WT_NB if i > WT_B else 0) for i in range(N_CHIPS)]
WT_WIN = [16 * (r // 16) for r in WT_FIRST]
WT_SHIFT = [r - s for r, s in zip(WT_FIRST, WT_WIN)]
WT_ROWS = 2592
assert (WT_LO + WT_SHIFT[WT_B]) % 16 == 0 and max(WT_SHIFT) + WT_SHARD <= WT_ROWS and WT_WIN[-1] + WT_ROWS <= D_IN_PAD


def _wt_to_window(shard, chip):
    d = jnp.asarray(WT_SHIFT, jnp.int32)[chip]
    gap = jnp.where(chip == WT_B, 0, WT_NB)
    win = jnp.zeros((WT_ROWS, shard.shape[1]), shard.dtype)
    win = lax.dynamic_update_slice(win, shard[:WT_LO], (d, 0))
    win = lax.dynamic_update_slice(win, shard[WT_LO:WT_LO + WT_NB], (d + WT_LO, 0))
    win = lax.dynamic_update_slice(win, shard[WT_LO + WT_NB:], (d + WT_LO + gap, 0))
    return win, shard[WT_LO:WT_LO + WT_NB]


def _wt_from_window(win, beta_a, chip):
    d = jnp.asarray(WT_SHIFT, jnp.int32)[chip]
    gap = jnp.where(chip == WT_B, 0, WT_NB)
    cols = win.shape[1]
    head = lax.dynamic_slice(win, (d, 0), (WT_LO, cols))
    mid = jnp.where(chip == WT_B, beta_a, lax.dynamic_slice(win, (d + WT_LO, 0), (WT_NB, cols)))
    tail = lax.dynamic_slice(win, (d + WT_LO + gap, 0), (WT_SHARD - WT_LO - WT_NB, cols))
    return jnp.concatenate([head, mid, tail], axis=0)


def _wt_regroup(wins, beta_a):
    parts, at = [], 0
    for i in range(N_CHIPS):
        end = WT_WIN[i + 1] if i + 1 < N_CHIPS else OFF_B
        lo = at - WT_WIN[i]
        over = WT_WIN[i] + WT_ROWS - end if i + 1 < N_CHIPS else 0
        parts.append(wins[i, lo:end - WT_WIN[i]])
        if over:
            parts.append(wins[i, end - WT_WIN[i]:] + wins[i + 1, :over])
        at = end + over
    pad = jnp.zeros((D_IN_PAD - OFF_B - WT_NB, wins.shape[2]), wins.dtype)
    return jnp.concatenate(parts + [beta_a, pad], axis=0)


_SMALL = ("ln_w", "s5_lam_re", "s5_lam_im", "s5_log_step", "s5_b_re", "s5_b_im", "s5_c_re", "s5_c_im", "s5_d",
          "dn_a_log", "dn_dt_bias", "dn_norm_w", "final_norm_w")
_BIG = ("w_in", "s5_w_glu", "s5_w_up", "dn_w_up", "w_out")
_ORDER = ("ln_w", "w_in", "s5_lam_re", "s5_lam_im", "s5_log_step", "s5_b_re", "s5_b_im", "s5_c_re", "s5_c_im", "s5_d",
          "s5_w_glu", "s5_w_up", "dn_conv_w", "dn_a_log", "dn_dt_bias", "dn_norm_w", "dn_w_up", "w_out", "final_norm_w")


def kernel(x, ln_w, w_in, s5_lam_re, s5_lam_im, s5_log_step, s5_b_re, s5_b_im, s5_c_re, s5_c_im, s5_d, s5_w_glu, s5_w_up, dn_conv_w, dn_a_log, dn_dt_bias, dn_norm_w, dn_w_up, w_out, final_norm_w, loss_target, m_ln_w, m_w_in, m_s5_lam_re, m_s5_lam_im, m_s5_log_step, m_s5_b_re, m_s5_b_im, m_s5_c_re, m_s5_c_im, m_s5_d, m_s5_w_glu, m_s5_w_up, m_dn_conv_w, m_dn_a_log, m_dn_dt_bias, m_dn_norm_w, m_dn_w_up, m_w_out, m_final_norm_w, v_ln_w, v_w_in, v_s5_lam_re, v_s5_lam_im, v_s5_log_step, v_s5_b_re, v_s5_b_im, v_s5_c_re, v_s5_c_im, v_s5_d, v_s5_w_glu, v_s5_w_up, v_dn_conv_w, v_dn_a_log, v_dn_dt_bias, v_dn_norm_w, v_dn_w_up, v_w_out, v_final_norm_w):
    w = dict(ln_w=ln_w, w_in=w_in, s5_lam_re=s5_lam_re, s5_lam_im=s5_lam_im, s5_log_step=s5_log_step, s5_b_re=s5_b_re,
             s5_b_im=s5_b_im, s5_c_re=s5_c_re, s5_c_im=s5_c_im, s5_d=s5_d, s5_w_glu=s5_w_glu, s5_w_up=s5_w_up,
             dn_conv_w=dn_conv_w, dn_a_log=dn_a_log, dn_dt_bias=dn_dt_bias, dn_norm_w=dn_norm_w, dn_w_up=dn_w_up, w_out=w_out,
             final_norm_w=final_norm_w)
    m = dict(ln_w=m_ln_w, w_in=m_w_in, s5_lam_re=m_s5_lam_re, s5_lam_im=m_s5_lam_im, s5_log_step=m_s5_log_step,
             s5_b_re=m_s5_b_re, s5_b_im=m_s5_b_im, s5_c_re=m_s5_c_re, s5_c_im=m_s5_c_im, s5_d=m_s5_d, s5_w_glu=m_s5_w_glu,
             s5_w_up=m_s5_w_up, dn_conv_w=m_dn_conv_w, dn_a_log=m_dn_a_log, dn_dt_bias=m_dn_dt_bias, dn_norm_w=m_dn_norm_w,
             dn_w_up=m_dn_w_up, w_out=m_w_out, final_norm_w=m_final_norm_w)
    v = dict(ln_w=v_ln_w, w_in=v_w_in, s5_lam_re=v_s5_lam_re, s5_lam_im=v_s5_lam_im, s5_log_step=v_s5_log_step,
             s5_b_re=v_s5_b_re, s5_b_im=v_s5_b_im, s5_c_re=v_s5_c_re, s5_c_im=v_s5_c_im, s5_d=v_s5_d, s5_w_glu=v_s5_w_glu,
             s5_w_up=v_s5_w_up, dn_conv_w=v_dn_conv_w, dn_a_log=v_dn_a_log, dn_dt_bias=v_dn_dt_bias, dn_norm_w=v_dn_norm_w,
             dn_w_up=v_dn_w_up, w_out=v_w_out, final_norm_w=v_final_norm_w)
    xi, yi, ci = _place()
    chip = 2 * xi + yi
    where = jnp.stack([ci, chip, *[jnp.int32(s) for s in WT_WIN]]).astype(jnp.int32)

    tr = lambda a: jnp.swapaxes(a[0], 0, 1)
    win, beta_a = _wt_to_window(tr(w_in).astype(BF16), chip)
    g_win, g_ba, g_conv = _gather_relayed(win, [beta_a, dn_conv_w[0]], "gather_w_in")
    cat = lambda g: jnp.concatenate([g[j] for j in range(N_CHIPS)], axis=1)
    w_perm = _wt_regroup(g_win, g_ba[WT_B])

    ops = _ExchangeOps([w[n][0].astype(BF16) for n in _BIG[1:]], where)
    loss_row, grad_x, g = _local_step(
        x[0], loss_target[0], ln_w, w_perm, s5_lam_re[0], s5_lam_im[0], s5_log_step, s5_b_re[0], s5_b_im[0], s5_c_re[0],
        s5_c_im[0], s5_d, cat(g_conv), dn_a_log, dn_dt_bias, dn_norm_w, final_norm_w[None], ops)
    loss = lax.psum(loss_row[0, 0], ("x", "y", "c"))

    gfs = [_owner_sum(gx, r1, r2x, where) for gx, r1, r2x in ops.reduced]
    gfs = _share_halves(gfs)
    grads, deltas, new_m, new_v = {}, {}, {}, {}
    for n, gf in zip(_BIG[1:], gfs[1:]):
        shp = w[n].shape
        g2 = gf.reshape(shp[1:])
        d_, m_, v_ = _adamw(w[n][0], g2, m[n][0], v[n][0], "adamw_" + n)
        grads[n], deltas[n], new_m[n], new_v[n] = g2.reshape(shp), d_.reshape(shp), m_.reshape(shp), v_.reshape(shp)

    go = _small_allreduce(_pack([g[n] for n in _SMALL] + [g["dn_conv_w"], ops.beta_a]))
    lanes = {"s5_b_re": (S5_GROUPS, S5_STATE * S5_GROUP), "s5_b_im": (S5_GROUPS, S5_STATE * S5_GROUP)}
    flat = [lanes.get(n, (math.prod(w[n].shape[:-1]), w[n].shape[-1])) for n in _SMALL]
    *gs, g_conv, g_beta_a = _unpack(go, flat + [(CONV_K, 3 * D_DN), (WT_NB, D_MODEL)])
    gt = _wt_from_window(gfs[0].reshape(WT_ROWS, D_MODEL), g_beta_a, chip)
    d_, m_, v_, g_ = _adamw(tr(w_in), gt, tr(m_w_in), tr(v_w_in), "adamw_w_in", echo=True)
    grads["w_in"], deltas["w_in"], new_m["w_in"], new_v["w_in"] = (jnp.swapaxes(a, 0, 1)[None] for a in (g_, d_, m_, v_))
    as2d = lambda t: [t[n].reshape(s) for n, s in zip(_SMALL, flat)]
    for dst, src in zip((grads, deltas, new_m, new_v), (gs, *_adamw_many(as2d(w), gs, as2d(m), as2d(v)))):
        for n, a in zip(_SMALL, src):
            dst[n] = a.reshape(w[n].shape)
    cc = 3 * D_DN // N_CHIPS
    g_conv_mine = lax.dynamic_slice(g_conv, (0, chip * cc), (CONV_K, cc))
    d_, m_, v_ = _adamw(dn_conv_w[0], g_conv_mine, m_dn_conv_w[0], v_dn_conv_w[0], "adamw_dn_conv_w")
    grads["dn_conv_w"], deltas["dn_conv_w"], new_m["dn_conv_w"], new_v["dn_conv_w"] = (
        g_conv_mine[None], d_[None], m_[None], v_[None])

    return (loss, grad_x[None], *[grads[n] for n in _ORDER], *[deltas[n] for n in _ORDER], *[new_m[n] for n in _ORDER],
            *[new_v[n] for n in _ORDER])
```

```python
import functools
import math

import jax
import jax.numpy as jnp
from jax import lax
from jax.experimental import pallas as pl
from jax.experimental.pallas import tpu as pltpu

F32 = jnp.float32
BF16 = jnp.bfloat16
HI = lax.Precision.HIGHEST
MESH = pl.DeviceIdType.MESH
ANY = pl.BlockSpec(memory_space=pl.ANY)

EPS = 1e-6
D_MODEL = 2048
D_S5 = 1024
S5_GROUP = 16
S5_GROUPS = 64
S5_STATE = 64
S5_BLOCKS = 8
S5_SEG = 8
DN_HEADS = 8
DN_HEAD_DIM = 128
D_DN = 1024
CONV_K = 4
CHUNK = 64
D_IN = 10256
D_IN_PAD = 10368
OFF_US, OFF_ZS, OFF_Q, OFF_K, OFF_V, OFF_ZD, OFF_GS, OFF_GD, OFF_B = 0, 1024, 2048, 3072, 4096, 5120, 6144, 8192, 10240
N_CHIPS = 4
N_DEV = 8
VMEM_LIMIT = 56 * 1024 * 1024

ADAM_LR = 0.001
ADAM_B1 = 0.9
ADAM_B2 = 0.999
ADAM_EPS = 1e-08
ADAM_WD = 0.01
ADAM_STEP = 10


def _cp(sem=None):
    return pltpu.CompilerParams(dimension_semantics=sem, vmem_limit_bytes=VMEM_LIMIT)


def _sds(shape, dtype=F32):
    return jax.ShapeDtypeStruct(tuple(shape), dtype)


def _sigmoid(x):
    return 1.0 / (1.0 + jnp.exp(-x))


def _silu(x):
    return x * _sigmoid(x)


def _dsilu(x):
    s = _sigmoid(x)
    return s * (1.0 + x * (1.0 - s))


class _Exchange:
    def __init__(self, ins, out_shapes, n_sems, start, finish):
        self.ins, self.out_shapes, self.n_sems, self.start, self.finish = list(ins), list(out_shapes), n_sems, start, finish


def _mm(a, b, *, name, ta=False, tb=False, out_dtype=F32, tm=512, tn=512, tk=2048, shard_out=False, twin=False, exchange=None):
    if ta:
        K, M = a.shape
    else:
        M, K = a.shape
    if tb:
        N, K2 = b.shape
    else:
        K2, N = b.shape
    assert K == K2, (a.shape, b.shape)
    tm, tn, tk = min(tm, M), min(tn, N), min(tk, K)
    assert M % tm == 0 and N % tn == 0 and K % tk == 0, (M, N, K, tm, tn, tk)
    nk = K // tk
    dims = (((0 if ta else 1,), (1 if tb else 0,)), ((), ()))

    gm, gn = M // tm, N // tn
    n_in = len(exchange.ins) if exchange else 0
    n_out = len(exchange.out_shapes) if exchange else 0

    n_o = 2 if twin else 1

    def body(*refs):
        a_ref, b_ref, xin, o_refs = refs[0], refs[1], refs[2:2 + n_in], refs[2 + n_in:2 + n_in + n_o]
        xout, rest = refs[2 + n_in + n_o:2 + n_in + n_o + n_out], refs[2 + n_in + n_o + n_out:]
        i, j, k = pl.program_id(0), pl.program_id(1), pl.program_id(2)

        def write(val):
            o_refs[0][...] = val.astype(out_dtype).reshape(o_refs[0].shape)
            if twin:
                o_refs[1][...] = val.astype(BF16).reshape(o_refs[1].shape)

        if exchange:
            sems = rest[-3:]

            @pl.when(jnp.logical_and(jnp.logical_and(i == 0, j == 0), k == 0))
            def _():
                exchange.start(xin, xout, *sems)

        p = lax.dot_general(a_ref[...].astype(BF16), b_ref[...].astype(BF16), dims, preferred_element_type=F32)
        if nk == 1:
            write(p)
        else:
            acc_ref = rest[0]

            @pl.when(k == 0)
            def _():
                acc_ref[...] = p

            @pl.when(k > 0)
            def _():
                acc_ref[...] += p

            @pl.when(k == nk - 1)
            def _():
                write(acc_ref[...])

        if exchange:
            @pl.when(jnp.logical_and(jnp.logical_and(i == gm - 1, j == gn - 1), k == nk - 1))
            def _():
                exchange.finish(xin, xout, *sems)

    a_spec = pl.BlockSpec((tk, tm), lambda i, j, k: (k, i)) if ta else pl.BlockSpec((tm, tk), lambda i, j, k: (i, k))
    b_spec = pl.BlockSpec((tn, tk), lambda i, j, k: (j, k)) if tb else pl.BlockSpec((tk, tn), lambda i, j, k: (k, j))
    if shard_out:
        o_spec = pl.BlockSpec((1, tm, tn), lambda i, j, k: (j, i, 0))
        o_shape = _sds((N // tn, M, tn), out_dtype)
    else:
        o_spec = pl.BlockSpec((tm, tn), lambda i, j, k: (i, j))
        o_shape = _sds((M, N), out_dtype)
    scratch = [pltpu.VMEM((tm, tn), F32)] if nk > 1 else []
    o_specs, o_shapes = [o_spec] * n_o, [o_shape, _sds(o_shape.shape, BF16)][:n_o]
    if not exchange:
        out = pl.pallas_call(
            body, name=name, grid=(gm, gn, nk), in_specs=[a_spec, b_spec], out_specs=o_specs, out_shape=o_shapes,
            scratch_shapes=scratch, compiler_params=_cp(("parallel", "parallel", "arbitrary")),
        )(a, b)
        return out if twin else out[0]
    scratch += [pltpu.SemaphoreType.DMA((exchange.n_sems,)) for _ in range(3)]
    return pl.pallas_call(
        body, name=name, grid=(gm, gn, nk), in_specs=[a_spec, b_spec] + [ANY] * n_in, out_specs=o_specs + [ANY] * n_out,
        out_shape=o_shapes + exchange.out_shapes, scratch_shapes=scratch,
        compiler_params=_cp(("arbitrary", "arbitrary", "arbitrary")),
    )(a, b, *exchange.ins)


def _ln_fwd(x, w):
    L, D = x.shape
    tm = min(256, L)

    def body(x_ref, w_ref, h_ref, r_ref):
        xv = x_ref[...]
        r = lax.rsqrt(jnp.mean(xv * xv, axis=-1, keepdims=True) + EPS)
        h_ref[...] = (xv * r * w_ref[...]).astype(BF16)
        r_ref[...] = r

    return pl.pallas_call(
        body, name="ln_fwd", grid=(L // tm,),
        in_specs=[pl.BlockSpec((tm, D), lambda i: (i, 0)), pl.BlockSpec((1, D), lambda i: (0, 0))],
        out_specs=[pl.BlockSpec((tm, D), lambda i: (i, 0)), pl.BlockSpec((tm, 1), lambda i: (i, 0))],
        out_shape=[_sds((L, D), BF16), _sds((L, 1))], compiler_params=_cp(("parallel",)),
    )(x, w)


def _ln_bwd(x, r, w, dh, dx2):
    L, D = x.shape
    tm = min(256, L)

    def body(x_ref, r_ref, w_ref, dh_ref, dx2_ref, dx_ref, dw_ref):
        i = pl.program_id(0)
        xv, rv, dhv = x_ref[...], r_ref[...], dh_ref[...]
        t = dhv * w_ref[...]
        m = jnp.mean(t * xv, axis=-1, keepdims=True)
        dx_ref[...] = dx2_ref[...] + rv * t - xv * (rv * rv * rv) * m
        part = jnp.sum(dhv * xv * rv, axis=0, keepdims=True)

        @pl.when(i == 0)
        def _():
            dw_ref[...] = part

        @pl.when(i > 0)
        def _():
            dw_ref[...] += part

    row = pl.BlockSpec((tm, D), lambda i: (i, 0))
    return pl.pallas_call(
        body, name="ln_bwd", grid=(L // tm,),
        in_specs=[row, pl.BlockSpec((tm, 1), lambda i: (i, 0)), pl.BlockSpec((1, D), lambda i: (0, 0)), row, row],
        out_specs=[row, pl.BlockSpec((1, D), lambda i: (0, 0))],
        out_shape=[_sds((L, D)), _sds((1, D))], compiler_params=_cp(("arbitrary",)),
    )(x, r, w, dh, dx2)


def _s5_param_math(lam_re, lam_im, log_step, b_re, b_im, expand):
    step = jnp.exp(log_step)
    mag = jnp.exp(lam_re * step)
    abar_re = mag * jnp.cos(lam_im * step)
    abar_im = mag * jnp.sin(lam_im * step)
    den = lam_re * lam_re + lam_im * lam_im
    xr = abar_re - 1.0
    f_re = (xr * lam_re + abar_im * lam_im) / den
    f_im = (abar_im * lam_re - xr * lam_im) / den
    fe_re = jnp.dot(f_re, expand, precision=HI, preferred_element_type=F32)
    fe_im = jnp.dot(f_im, expand, precision=HI, preferred_element_type=F32)
    bb_re = fe_re * b_re - fe_im * b_im
    bb_im = fe_re * b_im + fe_im * b_re
    return abar_re, abar_im, bb_re, bb_im


def _s5_expand():
    p = lax.broadcasted_iota(jnp.int32, (S5_STATE, S5_STATE * S5_GROUP), 0)
    q = lax.broadcasted_iota(jnp.int32, (S5_STATE, S5_STATE * S5_GROUP), 1)
    return (q // S5_GROUP == p).astype(F32)


def _s5_param_fwd(lam_re, lam_im, log_step, b_re, b_im):
    G, P = lam_re.shape

    def body(lr, li, ls, br, bi, ar_o, ai_o, bbr_o, bbi_o):
        outs = _s5_param_math(lr[...], li[...], ls[...], br[...], bi[...], _s5_expand())
        for o, v in zip((ar_o, ai_o, bbr_o, bbi_o), outs):
            o[...] = v

    return pl.pallas_call(
        body, name="s5_param_fwd",
        out_shape=[_sds((G, P)), _sds((G, P)), _sds(b_re.shape), _sds(b_re.shape)], compiler_params=_cp(),
    )(lam_re, lam_im, log_step, b_re, b_im)


def _s5_param_bwd(lam_re, lam_im, log_step, b_re, b_im, dar, dai, dbbr, dbbi):
    G, P = lam_re.shape

    def body(lr, li, ls, br, bi, g0, g1, g2, g3, dlr, dli, dls, dbr, dbi):
        ex = _s5_expand()
        _, f = jax.vjp(lambda a, b, c, d, e: _s5_param_math(a, b, c, d, e, ex), lr[...], li[...], ls[...], br[...], bi[...])
        grads = f((g0[...], g1[...], g2[...], g3[...]))
        for o, v in zip((dlr, dli, dls, dbr, dbi), grads):
            o[...] = v

    return pl.pallas_call(
        body, name="s5_param_bwd",
        out_shape=[_sds((G, P)), _sds((G, P)), _sds((G, 1)), _sds(b_re.shape), _sds(b_re.shape)], compiler_params=_cp(),
    )(lam_re, lam_im, log_step, b_re, b_im, dar, dai, dbbr, dbbi)


def _to_segs(src_ref, dst_ref, L):
    S = L // S5_SEG

    def body(j, carry):
        dst_ref[pl.ds(pl.multiple_of(S5_SEG * j, S5_SEG), S5_SEG), :] = src_ref[pl.ds(j, S5_SEG, stride=S), :]
        return carry

    lax.fori_loop(0, S, body, 0, unroll=8)


def _from_segs(src_ref, L, write):
    S = L // S5_SEG
    for seg in range(S5_SEG):
        def body(jb, carry, seg=seg):
            j0 = 16 * jb
            write(pl.multiple_of(seg * S + j0, 16), src_ref[pl.ds(S5_SEG * j0 + seg, 16, stride=S5_SEG), :])
            return carry

        lax.fori_loop(0, S // 16, body, 0, unroll=4)


def _scan_segs(ar, ai, re_ref, im_ref, end_r_ref, end_i_ref, c_r_ref, c_i_ref, L, tile0, reverse):
    S = L // S5_SEG
    NB, LN = re_ref.shape[0], 128
    assert S & (S - 1) == 0
    tile = lambda j: pl.ds(pl.multiple_of(S5_SEG * (tile0 + j), S5_SEG), S5_SEG)
    ar8 = [jnp.broadcast_to(ar[:, b * LN:(b + 1) * LN], (S5_SEG, LN)) for b in range(NB)]
    ai8 = [jnp.broadcast_to(ai[:, b * LN:(b + 1) * LN], (S5_SEG, LN)) for b in range(NB)]

    def step(idx, carry):
        rows = tile(S - 1 - idx if reverse else idx)
        out = []
        for b in range(NB):
            sr, si = carry[b]
            nr = ar8[b] * sr - ai8[b] * si + re_ref[b, rows, :]
            ni = ar8[b] * si + ai8[b] * sr + im_ref[b, rows, :]
            re_ref[b, rows, :] = nr
            im_ref[b, rows, :] = ni
            out.append((nr, ni))
        return tuple(out)

    z8 = jnp.zeros((S5_SEG, LN), F32)
    fin = lax.fori_loop(0, S, step, tuple((z8, z8) for _ in range(NB)), unroll=4)
    order = range(S5_SEG - 2, -1, -1) if reverse else range(1, S5_SEG)
    for b in range(NB):
        end_r_ref[b], end_i_ref[b] = fin[b]
        pr, pi = ar8[b][:1], ai8[b][:1]
        for _ in range(int(math.log2(S))):
            pr, pi = pr * pr - pi * pi, 2.0 * pr * pi
        first = S5_SEG - 1 if reverse else 0
        c_r_ref[b, pl.ds(first, 1), :] = jnp.zeros((1, LN), F32)
        c_i_ref[b, pl.ds(first, 1), :] = jnp.zeros((1, LN), F32)
        cr, ci = end_r_ref[b, pl.ds(first, 1), :], end_i_ref[b, pl.ds(first, 1), :]
        for i in order:
            c_r_ref[b, pl.ds(i, 1), :] = cr
            c_i_ref[b, pl.ds(i, 1), :] = ci
            er, ei = end_r_ref[b, pl.ds(i, 1), :], end_i_ref[b, pl.ds(i, 1), :]
            cr, ci = er + pr * cr - pi * ci, ei + pr * ci + pi * cr

    entering = [(c_r_ref[b], c_i_ref[b]) for b in range(NB)]

    def fix(idx, carry):
        rows = tile(S - 1 - idx if reverse else idx)
        out = []
        for b in range(NB):
            pr, pi = carry[b]
            cr, ci = entering[b]
            re_ref[b, rows, :] += pr * cr - pi * ci
            im_ref[b, rows, :] += pr * ci + pi * cr
            out.append((pr * ar8[b] - pi * ai8[b], pr * ai8[b] + pi * ar8[b]))
        return tuple(out)

    lax.fori_loop(0, S, fix, tuple((ar8[b], ai8[b]) for b in range(NB)), unroll=4)


def _s5_seg_scratch(L, cs, pad):
    NB = cs // 128
    small = [pltpu.VMEM((NB, S5_SEG, 128), F32) for _ in range(4)]
    return [pltpu.VMEM((NB, L + pad, 128), F32), pltpu.VMEM((NB, L + pad, 128), F32)] + small


def _s5_core_fwd(proj, wbr, wbi, a_re, a_im, cbr, cbi, exchange=None):
    L = proj.shape[0]
    nb, ci, cs = wbr.shape
    NB = cs // 128
    n_in = len(exchange.ins) if exchange else 0
    n_out = len(exchange.out_shapes) if exchange else 0

    def body(u_ref, wbr_ref, wbi_ref, ar_ref, ai_ref, cbr_ref, cbi_ref, *rest):
        xin, y_ref, xout = rest[:n_in], rest[n_in], rest[n_in + 1:n_in + 1 + n_out]
        sr, si, er, ei, cr, cim, up, yp = rest[n_in + 1 + n_out:n_in + 9 + n_out]
        sems = rest[n_in + 9 + n_out:]
        if exchange:
            @pl.when(pl.program_id(0) == 0)
            def _():
                exchange.start(xin, xout, *sems)

        _to_segs(u_ref, up, L)
        u = up[...].astype(BF16)
        for b in range(NB):
            lanes = pl.ds(b * 128, 128)
            sr[b] = jnp.dot(u, wbr_ref[0, :, lanes], preferred_element_type=F32)
            si[b] = jnp.dot(u, wbi_ref[0, :, lanes], preferred_element_type=F32)
        _scan_segs(ar_ref[...], ai_ref[...], sr, si, er, ei, cr, cim, L, 0, False)
        y = jnp.zeros((L, ci), F32)
        for b in range(NB):
            lanes = pl.ds(b * 128, 128)
            y = y + (jnp.dot(sr[b].astype(BF16), cbr_ref[0, lanes, :], preferred_element_type=F32)
                     - jnp.dot(si[b].astype(BF16), cbi_ref[0, lanes, :], preferred_element_type=F32))
        yp[...] = y

        def write(row, val):
            y_ref[pl.ds(row, 16), :] = val

        _from_segs(yp, L, write)
        if exchange:
            @pl.when(pl.program_id(0) == nb - 1)
            def _():
                exchange.finish(xin, xout, *sems)

    wspec = pl.BlockSpec((1, ci, cs), lambda j: (j, 0, 0))
    aspec = pl.BlockSpec((1, cs), lambda j: (0, j))
    cspec = pl.BlockSpec((1, cs, ci), lambda j: (j, 0, 0))
    scratch = _s5_seg_scratch(L, cs, 0) + [pltpu.VMEM((L, ci), F32), pltpu.VMEM((L, ci), F32)]
    if exchange:
        scratch += [pltpu.SemaphoreType.DMA((exchange.n_sems,)) for _ in range(3)]
    outs = pl.pallas_call(
        body, name="s5_core_fwd", grid=(nb,),
        in_specs=[pl.BlockSpec((L, ci), lambda j: (0, OFF_US // ci + j)), wspec, wspec, aspec, aspec, cspec, cspec] + [ANY] * n_in,
        out_specs=[pl.BlockSpec((L, ci), lambda j: (0, j))] + [ANY] * n_out,
        out_shape=[_sds((L, nb * ci))] + (exchange.out_shapes if exchange else []),
        scratch_shapes=scratch, compiler_params=_cp(("arbitrary",)),
    )(proj, wbr, wbi, a_re, a_im, cbr, cbi, *(exchange.ins if exchange else []))
    return outs if exchange else outs[0]


def _s5_core_bwd(proj, wbr, wbi, a_re, a_im, cbr, cbi, dyc, du1):
    L = proj.shape[0]
    nb, ci, cs = wbr.shape
    NB = cs // 128
    S = L // S5_SEG
    PAD = S5_SEG

    def body(u_ref, wbr_ref, wbi_ref, ar_ref, ai_ref, cbr_ref, cbi_ref, dy_ref, du1_ref,
             du_ref, dwbr_ref, dwbi_ref, dcbr_ref, dcbi_ref, dar_ref, dai_ref,
             sr, si, er, ei, cr, cim, lr, li, up, dyp, dup):
        tn = (((0,), (0,)), ((), ()))
        nt = (((1,), (1,)), ((), ()))
        _to_segs(u_ref, up, L)
        _to_segs(dy_ref, dyp, L)
        u = up[...].astype(BF16)
        dy = dyp[...].astype(BF16)
        ar, ai = ar_ref[...], ai_ref[...]
        for b in range(NB):
            lanes = pl.ds(b * 128, 128)
            sr[b, pl.ds(PAD, L), :] = jnp.dot(u, wbr_ref[0, :, lanes], preferred_element_type=F32)
            si[b, pl.ds(PAD, L), :] = jnp.dot(u, wbi_ref[0, :, lanes], preferred_element_type=F32)
        _scan_segs(ar, ai, sr, si, er, ei, cr, cim, L, 1, False)
        for b in range(NB):
            lanes = pl.ds(b * 128, 128)
            sr[b, pl.ds(0, PAD), :] = cr[b]
            si[b, pl.ds(0, PAD), :] = cim[b]
            lr[b] = lax.dot_general(dy, cbr_ref[0, lanes, :], nt, preferred_element_type=F32)
            li[b] = -lax.dot_general(dy, cbi_ref[0, lanes, :], nt, preferred_element_type=F32)
            dcbr_ref[0, lanes, :] = lax.dot_general(sr[b, pl.ds(PAD, L), :].astype(BF16), dy, tn, preferred_element_type=F32)
            dcbi_ref[0, lanes, :] = -lax.dot_general(si[b, pl.ds(PAD, L), :].astype(BF16), dy, tn, preferred_element_type=F32)
        _scan_segs(ar, -ai, lr, li, er, ei, cr, cim, L, 0, True)

        def da_step(j, carry):
            rows = pl.ds(pl.multiple_of(S5_SEG * j, S5_SEG), S5_SEG)
            out = []
            for b in range(NB):
                dar, dai = carry[b]
                pr_, pi_ = sr[b, rows, :], si[b, rows, :]
                gr, gi = lr[b, rows, :], li[b, rows, :]
                out.append((dar + (gr * pr_ + gi * pi_), dai + (gi * pr_ - gr * pi_)))
            return tuple(out)

        z8 = jnp.zeros((S5_SEG, 128), F32)
        acc = lax.fori_loop(0, S, da_step, tuple((z8, z8) for _ in range(NB)), unroll=4)
        du = jnp.zeros((L, ci), F32)
        for b in range(NB):
            lanes = pl.ds(b * 128, 128)
            dar_ref[:, lanes] = jnp.sum(acc[b][0], axis=0, keepdims=True)
            dai_ref[:, lanes] = jnp.sum(acc[b][1], axis=0, keepdims=True)
            gr, gi = lr[b].astype(BF16), li[b].astype(BF16)
            du = du + (lax.dot_general(gr, wbr_ref[0, :, lanes], nt, preferred_element_type=F32)
                       + lax.dot_general(gi, wbi_ref[0, :, lanes], nt, preferred_element_type=F32))
            dwbr_ref[0, :, lanes] = lax.dot_general(u, gr, tn, preferred_element_type=F32)
            dwbi_ref[0, :, lanes] = lax.dot_general(u, gi, tn, preferred_element_type=F32)
        dup[...] = du

        def write(row, val):
            du_ref[pl.ds(row, 16), :] = (val + du1_ref[pl.ds(row, 16), :]).astype(BF16)

        _from_segs(dup, L, write)

    wspec = pl.BlockSpec((1, ci, cs), lambda j: (j, 0, 0))
    aspec = pl.BlockSpec((1, cs), lambda j: (0, j))
    cspec = pl.BlockSpec((1, cs, ci), lambda j: (j, 0, 0))
    col = pl.BlockSpec((L, ci), lambda j: (0, j))
    return pl.pallas_call(
        body, name="s5_core_bwd", grid=(nb,),
        in_specs=[pl.BlockSpec((L, ci), lambda j: (0, OFF_US // ci + j)), wspec, wspec, aspec, aspec, cspec, cspec, col, col],
        out_specs=[col, wspec, wspec, cspec, cspec, aspec, aspec],
        out_shape=[_sds((L, nb * ci), BF16), _sds(wbr.shape), _sds(wbr.shape), _sds(cbr.shape), _sds(cbr.shape),
                   _sds((1, nb * cs)), _sds((1, nb * cs))],
        scratch_shapes=(_s5_seg_scratch(L, cs, PAD) + [pltpu.VMEM((NB, L, 128), F32), pltpu.VMEM((NB, L, 128), F32)]
                        + [pltpu.VMEM((L, ci), F32) for _ in range(3)]),
        compiler_params=_cp(("arbitrary",)),
    )(proj, wbr, wbi, a_re, a_im, cbr, cbi, dyc, du1)


def _s5_post_math(yc, u, z, d, wg):
    y = yc + d * u
    y1 = jax.nn.gelu(y)
    t = jnp.dot(y1.astype(BF16), wg, preferred_element_type=F32)
    sg = _sigmoid(t)
    return y, y1, sg


def _s5_post_fwd(yc, proj, d, wg):
    L, W = yc.shape
    tm = min(256, L)

    def body(yc_ref, u_ref, z_ref, d_ref, wg_ref, o_ref):
        _, y1, sg = _s5_post_math(yc_ref[...], u_ref[...], z_ref[...], d_ref[...], wg_ref[...])
        o_ref[...] = (y1 * sg * _silu(z_ref[...])).astype(BF16)

    row = pl.BlockSpec((tm, W), lambda i: (i, 0))
    return pl.pallas_call(
        body, name="s5_post_fwd", grid=(L // tm,),
        in_specs=[row, pl.BlockSpec((tm, W), lambda i: (i, OFF_US // W)), pl.BlockSpec((tm, W), lambda i: (i, OFF_ZS // W)),
                  pl.BlockSpec((1, W), lambda i: (0, 0)), pl.BlockSpec((W, W), lambda i: (0, 0))],
        out_specs=row, out_shape=_sds((L, W), BF16), compiler_params=_cp(("parallel",)),
    )(yc, proj, proj, d, wg)


def _s5_post_bwd(yc, proj, d, wg, dout):
    L, W = yc.shape
    tm = min(256, L)

    def body(yc_ref, u_ref, z_ref, d_ref, wg_ref, do_ref, dyc_ref, du_ref, dz_ref, dd_ref, dwg_ref):
        i = pl.program_id(0)
        u, z, d_, wgv = u_ref[...], z_ref[...], d_ref[...], wg_ref[...]
        y, y1, sg = _s5_post_math(yc_ref[...], u, z, d_, wgv)
        dout_ = do_ref[...]
        y2 = y1 * sg
        dy2 = dout_ * _silu(z)
        dz_ref[...] = (dout_ * y2 * _dsilu(z)).astype(BF16)
        dt = (dy2 * y1 * sg * (1.0 - sg)).astype(BF16)
        dy1 = dy2 * sg + lax.dot_general(dt, wgv, (((1,), (1,)), ((), ())), preferred_element_type=F32)
        _, gelu_vjp = jax.vjp(jax.nn.gelu, y)
        dy = gelu_vjp(dy1)[0]
        dyc_ref[...] = dy
        du_ref[...] = dy * d_
        dd_part = jnp.sum(dy * u, axis=0, keepdims=True)
        dwg_part = lax.dot_general(y1.astype(BF16), dt, (((0,), (0,)), ((), ())), preferred_element_type=F32)

        @pl.when(i == 0)
        def _():
            dd_ref[...] = dd_part
            dwg_ref[...] = dwg_part

        @pl.when(i > 0)
        def _():
            dd_ref[...] += dd_part
            dwg_ref[...] += dwg_part

    row = pl.BlockSpec((tm, W), lambda i: (i, 0))
    return pl.pallas_call(
        body, name="s5_post_bwd", grid=(L // tm,),
        in_specs=[row, pl.BlockSpec((tm, W), lambda i: (i, OFF_US // W)), pl.BlockSpec((tm, W), lambda i: (i, OFF_ZS // W)),
                  pl.BlockSpec((1, W), lambda i: (0, 0)), pl.BlockSpec((W, W), lambda i: (0, 0)), row],
        out_specs=[row, row, row, pl.BlockSpec((1, W), lambda i: (0, 0)), pl.BlockSpec((W, W), lambda i: (0, 0))],
        out_shape=[_sds((L, W)), _sds((L, W)), _sds((L, W), BF16), _sds((1, W)), _sds((W, W))],
        compiler_params=_cp(("arbitrary",)),
    )(yc, proj, proj, d, wg, dout)


def _shift_down(x, s):
    if s == 0:
        return x
    rows = lax.broadcasted_iota(jnp.int32, x.shape, 0)
    return jnp.where(rows >= s, pltpu.roll(x, s, 0), 0.0)


def _shift_up(x, s):
    if s == 0:
        return x
    L = x.shape[0]
    rows = lax.broadcasted_iota(jnp.int32, x.shape, 0)
    return jnp.where(rows < L - s, pltpu.roll(x, L - s, 0), 0.0)


def _conv_pre(x, w):
    acc = w[CONV_K - 1:CONV_K, :] * x
    for s in range(1, CONV_K):
        acc = acc + w[CONV_K - 1 - s:CONV_K - s, :] * _shift_down(x, s)
    return acc


def _dn_conv_fwd(proj, conv_w):
    L = proj.shape[0]
    W = DN_HEAD_DIM
    nq = 2 * DN_HEADS

    def body(x_ref, w_ref, o_ref):
        j = pl.program_id(0)
        act = _silu(_conv_pre(x_ref[...], w_ref[...]))
        r = lax.rsqrt(jnp.sum(act * act, axis=-1, keepdims=True) + EPS)
        scale = jnp.where(j < DN_HEADS, DN_HEAD_DIM ** -0.5, 1.0)
        o_ref[...] = jnp.where(j < nq, act * r * scale, act)

    return pl.pallas_call(
        body, name="dn_conv_fwd", grid=(3 * DN_HEADS,),
        in_specs=[pl.BlockSpec((L, W), lambda j: (0, OFF_Q // W + j)), pl.BlockSpec((CONV_K, W), lambda j: (0, j))],
        out_specs=pl.BlockSpec((L, W), lambda j: (0, j)), out_shape=_sds((L, 3 * D_DN)), compiler_params=_cp(("parallel",)),
    )(proj, conv_w)


def _dn_conv_bwd(proj, conv_w, dout):
    L = proj.shape[0]
    W = DN_HEAD_DIM
    nq = 2 * DN_HEADS

    def body(x_ref, w_ref, do_ref, dx_ref, dw_ref):
        j = pl.program_id(0)
        x, w, dout_ = x_ref[...], w_ref[...], do_ref[...]
        pre = _conv_pre(x, w)
        act = _silu(pre)
        r = lax.rsqrt(jnp.sum(act * act, axis=-1, keepdims=True) + EPS)
        scale = jnp.where(j < DN_HEADS, DN_HEAD_DIM ** -0.5, 1.0)
        g = dout_ * scale
        dact_n = r * g - act * (r * r * r) * jnp.sum(g * act, axis=-1, keepdims=True)
        dact = jnp.where(j < nq, dact_n, dout_)
        dpre = dact * _dsilu(pre)
        dx = w[CONV_K - 1:CONV_K, :] * dpre
        for s in range(1, CONV_K):
            dx = dx + w[CONV_K - 1 - s:CONV_K - s, :] * _shift_up(dpre, s)
        dx_ref[...] = dx.astype(BF16)
        for s in range(CONV_K):
            dw_ref[pl.ds(CONV_K - 1 - s, 1), :] = jnp.sum(dpre * _shift_down(x, s), axis=0, keepdims=True)

    col = pl.BlockSpec((L, W), lambda j: (0, j))
    wsp = pl.BlockSpec((CONV_K, W), lambda j: (0, j))
    return pl.pallas_call(
        body, name="dn_conv_bwd", grid=(3 * DN_HEADS,),
        in_specs=[pl.BlockSpec((L, W), lambda j: (0, OFF_Q // W + j)), wsp, col], out_specs=[col, wsp],
        out_shape=[_sds((L, 3 * D_DN), BF16), _sds((CONV_K, 3 * D_DN))], compiler_params=_cp(("parallel",)),
    )(proj, conv_w, dout)


def _softplus(x):
    return jnp.maximum(x, 0.0) + jnp.log(1.0 + jnp.exp(-jnp.abs(x)))


def _dn_gates_fwd(proj, alog, dtb):
    L = proj.shape[0]
    W = 128

    def body(p_ref, al_ref, db_ref, o_ref):
        p = p_ref[...]
        lane = lax.broadcasted_iota(jnp.int32, p.shape, 1)
        g = -jnp.exp(al_ref[...]) * _softplus(p + db_ref[...])
        o_ref[...] = jnp.where(lane < DN_HEADS, _sigmoid(p), jnp.where(lane < 2 * DN_HEADS, g, 0.0))

    return pl.pallas_call(
        body, name="dn_gates_fwd", grid=(1,),
        in_specs=[pl.BlockSpec((L, W), lambda i: (0, OFF_B // W)), pl.BlockSpec((1, W), lambda i: (0, 0)),
                  pl.BlockSpec((1, W), lambda i: (0, 0))],
        out_specs=pl.BlockSpec((L, W), lambda i: (0, 0)), out_shape=_sds((L, W)), compiler_params=_cp(("arbitrary",)),
    )(proj, alog, dtb)


def _dn_gates_bwd(proj, alog, dtb, dgates):
    L = proj.shape[0]
    W = 128

    def body(p_ref, al_ref, db_ref, dg_ref, dp_ref, dal_ref, ddb_ref):
        p, dg = p_ref[...], dg_ref[...]
        lane = lax.broadcasted_iota(jnp.int32, p.shape, 1)
        is_g = jnp.logical_and(lane >= DN_HEADS, lane < 2 * DN_HEADS)
        beta = _sigmoid(p)
        na = -jnp.exp(al_ref[...])
        xs = p + db_ref[...]
        dsp = dg * na * _sigmoid(xs)
        dp_ref[...] = jnp.where(lane < DN_HEADS, dg * beta * (1.0 - beta), jnp.where(is_g, dsp, 0.0)).astype(BF16)
        dal_ref[...] = jnp.sum(jnp.where(is_g, dg * na * _softplus(xs), 0.0), axis=0, keepdims=True)
        ddb_ref[...] = jnp.sum(jnp.where(is_g, dsp, 0.0), axis=0, keepdims=True)

    one = pl.BlockSpec((1, W), lambda i: (0, 0))
    full = pl.BlockSpec((L, W), lambda i: (0, 0))
    return pl.pallas_call(
        body, name="dn_gates_bwd", grid=(1,),
        in_specs=[pl.BlockSpec((L, W), lambda i: (0, OFF_B // W)), one, one, full], out_specs=[full, one, one],
        out_shape=[_sds((L, W), BF16), _sds((1, W)), _sds((1, W))], compiler_params=_cp(("arbitrary",)),
    )(proj, alog, dtb, dgates)


def _bdot(a, b, dims):
    return lax.dot_general(a.astype(BF16), b.astype(BF16), (dims, ((), ())), preferred_element_type=F32)


_NN, _NT, _TN = ((1,), (0,)), ((1,), (1,)), ((0,), (0,))


def _dot3(a, b, dims):
    ah, bh = a.astype(BF16), b.astype(BF16)
    al, bl = (a - ah.astype(F32)).astype(BF16), (b - bh.astype(F32)).astype(BF16)
    (ca,), (cb,) = dims
    a3 = jnp.concatenate([ah, ah, al], axis=ca)
    b3 = jnp.concatenate([bh, bl, bh], axis=cb)
    return lax.dot_general(a3, b3, (dims, ((), ())), preferred_element_type=F32)


def _mm_family(raw):
    nn = jax.custom_vjp(lambda a, b: raw(a, b, _NN))
    nt = jax.custom_vjp(lambda a, b: raw(a, b, _NT))
    tn = jax.custom_vjp(lambda a, b: raw(a, b, _TN))
    nn.defvjp(lambda a, b: (raw(a, b, _NN), (a, b)), lambda r, g: (raw(g, r[1], _NT), raw(r[0], g, _TN)))
    nt.defvjp(lambda a, b: (raw(a, b, _NT), (a, b)), lambda r, g: (raw(g, r[1], _NN), raw(g, r[0], _TN)))
    tn.defvjp(lambda a, b: (raw(a, b, _TN), (a, b)), lambda r, g: (raw(r[1], g, _NT), raw(r[0], g, _NN)))
    return nn, nt, tn


_mm_nn, _mm_nt, _mm_tn = _mm_family(_bdot)
_m3_nn, _m3_nt, _m3_tn = _mm_family(_dot3)


def _tri_apply(x, upper):
    C = x.shape[0]
    ii = lax.broadcasted_iota(jnp.int32, (C, 3 * C), 0)
    jj = lax.broadcasted_iota(jnp.int32, (C, 3 * C), 1) % C
    mat = ((ii <= jj) if upper else (ii >= jj)).astype(BF16)
    hi = x.astype(BF16)
    r = x - hi.astype(F32)
    mid = r.astype(BF16)
    lo = (r - mid.astype(F32)).astype(BF16)
    return jnp.dot(mat, jnp.concatenate([hi, mid, lo], axis=0), preferred_element_type=F32)


_cumsum_rows = jax.custom_vjp(lambda x: _tri_apply(x, False))
_cumsum_rows.defvjp(lambda x: (_tri_apply(x, False), None), lambda _, g: (_tri_apply(g, True),))


def _uli(a_s):
    C = a_s[0].shape[0]
    ii = lax.broadcasted_iota(jnp.int32, (C, C), 0)
    jj = lax.broadcasted_iota(jnp.int32, (C, C), 1)
    eye = jnp.where(ii == jj, 1.0, 0.0)
    ts = [eye - a for a in a_s]
    ms = list(a_s)
    for _ in range(int(math.log2(C)) - 1):
        ms = [_dot3(m, m, _NN) for m in ms]
        ts = [t + _dot3(t, m, _NN) for t, m in zip(ts, ms)]
    return tuple(ts)


def _uli_bwd(ts, gs):
    xs = [_dot3(t, g, _TN) for t, g in zip(ts, gs)]
    return (tuple(-_dot3(x, t, _NT) for x, t in zip(xs, ts)),)


_unit_lower_inverse = jax.custom_vjp(_uli)
_unit_lower_inverse.defvjp(lambda a_s: (lambda ts: (ts, ts))(_uli(a_s)), _uli_bwd)


def _prep_math(qs, ks, vs, gcols, bcols):
    n = len(qs)
    C, dv = vs[0].shape
    ii = lax.broadcasted_iota(jnp.int32, (C, C), 0)
    jj = lax.broadcasted_iota(jnp.int32, (C, C), 1)
    causal = ii >= jj
    strict = ii > jj
    sf = strict.astype(F32)
    ones = jnp.ones((C, dv), F32)
    dms = [_cumsum_rows(g * sf) for g in gcols]
    gcbs = [_cumsum_rows(g * ones) for g in gcols]
    kks = [_mm_nt(k, k) for k in ks]
    qks = [_mm_nt(q, k) for q, k in zip(qs, ks)]
    decays = [jnp.where(causal, jnp.exp(jnp.where(causal, dm, 0.0)), 0.0) for dm in dms]
    glasts = [jnp.sum(g * ones, axis=0, keepdims=True) for g in gcols]
    egs = [jnp.exp(gcb) for gcb in gcbs]
    ts = _unit_lower_inverse(tuple(jnp.where(strict, b * kk * dc, 0.0) for b, kk, dc in zip(bcols, kks, decays)))
    us = [_m3_nn(t, v * b) for t, v, b in zip(ts, vs, bcols)]
    ws = [_m3_nn(t, k * b * eg) for t, k, b, eg in zip(ts, ks, bcols, egs)]
    return tuple((us[i], ws[i], qs[i] * egs[i], ks[i] * jnp.exp(glasts[i] - gcbs[i]), qks[i] * decays[i],
                  jnp.exp(glasts[i])) for i in range(n))


def _gate_cols(gates, h):
    lane = lax.broadcasted_iota(jnp.int32, gates.shape, 1)
    bcol = jnp.sum(jnp.where(lane == h, gates, 0.0), axis=1, keepdims=True)
    gcol = jnp.sum(jnp.where(lane == h + DN_HEADS, gates, 0.0), axis=1, keepdims=True)
    return gcol, bcol


DN_HB = 8


def _dn_prep_fwd(qkv, gates):
    L = qkv.shape[0]
    N, H, d, HB = L // CHUNK, DN_HEADS, DN_HEAD_DIM, DN_HB

    def body(q_ref, k_ref, v_ref, g_ref, u_ref, w_ref, qd_ref, kd_ref, qk_ref, egl_ref):
        h0 = pl.program_id(1) * HB
        gates_ = g_ref[...]
        lanes_of = [pl.ds(i * d, d) for i in range(HB)]
        cols = [_gate_cols(gates_, h0 + i) for i in range(HB)]
        outs = _prep_math([q_ref[:, l] for l in lanes_of], [k_ref[:, l] for l in lanes_of], [v_ref[:, l] for l in lanes_of],
                          [c[0] for c in cols], [c[1] for c in cols])
        for i in range(HB):
            lanes = lanes_of[i]
            u, w, qd, kd, qk, egl = outs[i]
            u_ref[:, lanes] = u
            w_ref[:, lanes] = w
            qd_ref[:, lanes] = qd
            kd_ref[:, lanes] = kd
            qk_ref[0, i] = qk
            egl_ref[0, i] = jnp.broadcast_to(egl, (8, d))

    blk = lambda off: pl.BlockSpec((CHUNK, HB * d), lambda n, j: (n, off // HB + j))
    cc = pl.BlockSpec((1, HB, CHUNK, CHUNK), lambda n, j: (n, j, 0, 0))
    ee = pl.BlockSpec((1, HB, 8, d), lambda n, j: (n, j, 0, 0))
    big = _sds((L, D_DN))
    return pl.pallas_call(
        body, name="dn_prep_fwd", grid=(N, H // HB),
        in_specs=[blk(0), blk(H), blk(2 * H), pl.BlockSpec((CHUNK, 128), lambda n, j: (n, 0))],
        out_specs=[blk(0), blk(0), blk(0), blk(0), cc, ee],
        out_shape=[big, big, big, big, _sds((N, H, CHUNK, CHUNK)), _sds((N, H, 8, d))],
        compiler_params=_cp(("parallel", "parallel")),
    )(qkv, qkv, qkv, gates)


def _dn_scan_fwd(u, w, qd, kd, qk, egl):
    L = u.shape[0]
    N, H, d, HB = L // CHUNK, DN_HEADS, DN_HEAD_DIM, DN_HB

    def body(u_ref, w_ref, qd_ref, kd_ref, qk_ref, egl_ref, o_ref, st_ref, s_ref):
        n, h0 = pl.program_id(0), pl.program_id(1) * HB

        @pl.when(n == 0)
        def _():
            for i in range(HB):
                s_ref[h0 + i] = jnp.zeros((d, d), F32)

        hs = range(HB)
        ln = [pl.ds(i * d, d) for i in hs]
        st = [s_ref[h0 + i] for i in hs]
        ws = [_bdot(w_ref[:, ln[i]], st[i], _NN) for i in hs]
        qs = [_bdot(qd_ref[:, ln[i]], st[i], _NN) for i in hs]
        vn = [u_ref[:, ln[i]] - ws[i] for i in hs]
        qv = [_bdot(qk_ref[0, i], vn[i], _NN) for i in hs]
        kv = [_bdot(kd_ref[:, ln[i]], vn[i], _TN) for i in hs]
        for i in hs:
            st_ref[0, i] = st[i]
            o_ref[:, ln[i]] = qs[i] + qv[i]
            s_ref[h0 + i] = st[i] * egl_ref[0, i, pl.ds(0, 1), :] + kv[i]

    blk = pl.BlockSpec((CHUNK, HB * d), lambda n, j: (n, j))
    cc = pl.BlockSpec((1, HB, CHUNK, CHUNK), lambda n, j: (n, j, 0, 0))
    ee = pl.BlockSpec((1, HB, 8, d), lambda n, j: (n, j, 0, 0))
    return pl.pallas_call(
        body, name="dn_scan_fwd", grid=(N, H // HB), in_specs=[blk, blk, blk, blk, cc, ee],
        out_specs=[blk, pl.BlockSpec((1, HB, d, d), lambda n, j: (n, j, 0, 0))],
        out_shape=[_sds((L, D_DN)), _sds((N, H, d, d))], scratch_shapes=[pltpu.VMEM((H, d, d), F32)],
        compiler_params=_cp(("arbitrary", "arbitrary")),
    )(u, w, qd, kd, qk, egl)


def _dn_scan_bwd(u, w, qd, kd, qk, egl, states, do):
    L = u.shape[0]
    N, H, d, HB = L // CHUNK, DN_HEADS, DN_HEAD_DIM, DN_HB

    def body(u_ref, w_ref, qd_ref, kd_ref, qk_ref, egl_ref, st_ref, do_ref,
             du_ref, dw_ref, dqd_ref, dkd_ref, dqk_ref, degl_ref, ds_ref):
        n, h0 = pl.program_id(0), pl.program_id(1) * HB

        @pl.when(n == 0)
        def _():
            for i in range(HB):
                ds_ref[h0 + i] = jnp.zeros((d, d), F32)

        hs = range(HB)
        ln = [pl.ds(i * d, d) for i in hs]
        st = [st_ref[0, i] for i in hs]
        dsn = [ds_ref[h0 + i] for i in hs]
        do_ = [do_ref[:, ln[i]] for i in hs]
        ws = [_bdot(w_ref[:, ln[i]], st[i], _NN) for i in hs]
        d1 = [_bdot(qk_ref[0, i], do_[i], _TN) for i in hs]
        d2 = [_bdot(kd_ref[:, ln[i]], dsn[i], _NN) for i in hs]
        dqd = [_bdot(do_[i], st[i], _NT) for i in hs]
        qdo = [_bdot(qd_ref[:, ln[i]], do_[i], _TN) for i in hs]
        vn = [u_ref[:, ln[i]] - ws[i] for i in hs]
        dvn = [d1[i] + d2[i] for i in hs]
        dw = [_bdot(dvn[i], st[i], _NT) for i in hs]
        dkd = [_bdot(vn[i], dsn[i], _NT) for i in hs]
        dqk = [_bdot(do_[i], vn[i], _NT) for i in hs]
        wdv = [_bdot(w_ref[:, ln[i]], dvn[i], _TN) for i in hs]
        for i in hs:
            du_ref[:, ln[i]] = dvn[i]
            dw_ref[:, ln[i]] = -dw[i]
            dqd_ref[:, ln[i]] = dqd[i]
            dkd_ref[:, ln[i]] = dkd[i]
            dqk_ref[0, i] = dqk[i]
            degl_ref[0, i] = jnp.broadcast_to(jnp.sum(dsn[i] * st[i], keepdims=True), (8, d))
            ds_ref[h0 + i] = (qdo[i] - wdv[i]) + dsn[i] * egl_ref[0, i, pl.ds(0, 1), :]

    blk = pl.BlockSpec((CHUNK, HB * d), lambda n, j: (N - 1 - n, j))
    cc = pl.BlockSpec((1, HB, CHUNK, CHUNK), lambda n, j: (N - 1 - n, j, 0, 0))
    ee = pl.BlockSpec((1, HB, 8, d), lambda n, j: (N - 1 - n, j, 0, 0))
    ss = pl.BlockSpec((1, HB, d, d), lambda n, j: (N - 1 - n, j, 0, 0))
    big = _sds((L, D_DN))
    return pl.pallas_call(
        body, name="dn_scan_bwd", grid=(N, H // HB), in_specs=[blk, blk, blk, blk, cc, ee, ss, blk],
        out_specs=[blk, blk, blk, blk, cc, ee],
        out_shape=[big, big, big, big, _sds((N, H, CHUNK, CHUNK)), _sds((N, H, 8, d))],
        scratch_shapes=[pltpu.VMEM((H, d, d), F32)], compiler_params=_cp(("arbitrary", "arbitrary")),
    )(u, w, qd, kd, qk, egl, states, do)


def _dn_prep_bwd(qkv, gates, du, dw, dqd, dkd, dqk, degl):
    L = qkv.shape[0]
    N, H, d, HB = L // CHUNK, DN_HEADS, DN_HEAD_DIM, DN_HB
    assert HB == H

    def body(q_ref, k_ref, v_ref, g_ref, du_ref, dw_ref, dqd_ref, dkd_ref, dqk_ref, degl_ref, dqkv_ref, dg_ref):
        j = pl.program_id(1)
        h0 = j * HB
        gates_ = g_ref[...]
        lane = lax.broadcasted_iota(jnp.int32, gates_.shape, 1)
        lane1 = lax.broadcasted_iota(jnp.int32, (1, d), 1)
        part = jnp.zeros(gates_.shape, F32)
        lanes_of = [pl.ds(i * d, d) for i in range(HB)]
        cols = [_gate_cols(gates_, h0 + i) for i in range(HB)]
        _, f = jax.vjp(_prep_math, [q_ref[:, l] for l in lanes_of], [k_ref[:, l] for l in lanes_of],
                       [v_ref[:, l] for l in lanes_of], [c[0] for c in cols], [c[1] for c in cols])
        cots = tuple((du_ref[:, l], dw_ref[:, l], dqd_ref[:, l], dkd_ref[:, l], dqk_ref[0, i],
                      jnp.where(lane1 == 0, degl_ref[0, i, pl.ds(0, 1), :], 0.0)) for i, l in enumerate(lanes_of))
        dqs, dks, dvs, dgcs, dbcs = f(cots)
        for i in range(HB):
            for s, val in enumerate((dqs[i], dks[i], dvs[i])):
                dqkv_ref[:, pl.ds((s * H + i) * d, d)] = val
            part = part + jnp.where(lane == h0 + i, dbcs[i], 0.0) + jnp.where(lane == h0 + i + DN_HEADS, dgcs[i], 0.0)

        @pl.when(j == 0)
        def _():
            dg_ref[...] = part

        @pl.when(j > 0)
        def _():
            dg_ref[...] += part

    blk = lambda off: pl.BlockSpec((CHUNK, HB * d), lambda n, j: (n, off // HB + j))
    gsp = pl.BlockSpec((CHUNK, 128), lambda n, j: (n, 0))
    cc = pl.BlockSpec((1, HB, CHUNK, CHUNK), lambda n, j: (n, j, 0, 0))
    ee = pl.BlockSpec((1, HB, 8, d), lambda n, j: (n, j, 0, 0))
    return pl.pallas_call(
        body, name="dn_prep_bwd", grid=(N, H // HB),
        in_specs=[blk(0), blk(H), blk(2 * H), gsp, blk(0), blk(0), blk(0), blk(0), cc, ee],
        out_specs=[pl.BlockSpec((CHUNK, 3 * H * d), lambda n, j: (n, 0)), gsp], out_shape=[_sds((L, 3 * D_DN)), _sds((L, 128))],
        compiler_params=_cp(("parallel", "arbitrary")),
    )(qkv, qkv, qkv, gates, du, dw, dqd, dkd, dqk, degl)


def _dn_post_fwd(o, proj, nw):
    L = o.shape[0]
    d = DN_HEAD_DIM
    tm = min(512, L)

    def body(o_ref, z_ref, w_ref, y_ref):
        ov = o_ref[...]
        r = lax.rsqrt(jnp.mean(ov * ov, axis=-1, keepdims=True) + EPS)
        y_ref[...] = (ov * r * w_ref[...] * _silu(z_ref[...])).astype(BF16)

    blk = pl.BlockSpec((tm, d), lambda i, h: (i, h))
    return pl.pallas_call(
        body, name="dn_post_fwd", grid=(L // tm, DN_HEADS),
        in_specs=[blk, pl.BlockSpec((tm, d), lambda i, h: (i, OFF_ZD // d + h)), pl.BlockSpec((1, d), lambda i, h: (0, 0))],
        out_specs=blk, out_shape=_sds((L, D_DN), BF16), compiler_params=_cp(("parallel", "parallel")),
    )(o, proj, nw)


def _dn_post_bwd(o, proj, nw, dy):
    L = o.shape[0]
    d = DN_HEAD_DIM
    tm = min(512, L)

    def body(o_ref, z_ref, w_ref, dy_ref, do_ref, dz_ref, dw_ref):
        first = jnp.logical_and(pl.program_id(0) == 0, pl.program_id(1) == 0)
        ov, z, w, dyv = o_ref[...], z_ref[...], w_ref[...], dy_ref[...]
        r = lax.rsqrt(jnp.mean(ov * ov, axis=-1, keepdims=True) + EPS)
        xn = ov * r
        dz_ref[...] = (dyv * xn * w * _dsilu(z)).astype(BF16)
        dn = dyv * _silu(z)
        t = dn * w
        do_ref[...] = r * t - ov * (r * r * r) * jnp.mean(t * ov, axis=-1, keepdims=True)
        part = jnp.sum(dn * xn, axis=0, keepdims=True)

        @pl.when(first)
        def _():
            dw_ref[...] = part

        @pl.when(jnp.logical_not(first))
        def _():
            dw_ref[...] += part

    blk = pl.BlockSpec((tm, d), lambda i, h: (i, h))
    one = pl.BlockSpec((1, d), lambda i, h: (0, 0))
    return pl.pallas_call(
        body, name="dn_post_bwd", grid=(L // tm, DN_HEADS),
        in_specs=[blk, pl.BlockSpec((tm, d), lambda i, h: (i, OFF_ZD // d + h)), one, blk], out_specs=[blk, blk, one],
        out_shape=[_sds((L, D_DN)), _sds((L, D_DN), BF16), _sds((1, d))], compiler_params=_cp(("arbitrary", "arbitrary")),
    )(o, proj, nw, dy)


def _mix_fwd(s5o, dno, w_su, w_du, proj):
    L, K = s5o.shape
    N = w_su.shape[1]
    tm, tn = min(512, L), 512

    def body(a1, a2, b1, b2, gs, gd, ys_ref, yd_ref, mx_ref):
        ys = jnp.dot(a1[...], b1[...], preferred_element_type=F32)
        yd = jnp.dot(a2[...], b2[...], preferred_element_type=F32)
        ys_ref[...] = ys
        yd_ref[...] = yd
        mx_ref[...] = (_sigmoid(gs[...]) * ys + _sigmoid(gd[...]) * yd).astype(BF16)

    a = pl.BlockSpec((tm, K), lambda i, j: (i, 0))
    b = pl.BlockSpec((K, tn), lambda i, j: (0, j))
    o = pl.BlockSpec((tm, tn), lambda i, j: (i, j))
    return pl.pallas_call(
        body, name="mix_fwd", grid=(L // tm, N // tn),
        in_specs=[a, a, b, b, pl.BlockSpec((tm, tn), lambda i, j: (i, OFF_GS // tn + j)),
                  pl.BlockSpec((tm, tn), lambda i, j: (i, OFF_GD // tn + j))],
        out_specs=[o, o, o], out_shape=[_sds((L, N)), _sds((L, N)), _sds((L, N), BF16)],
        compiler_params=_cp(("parallel", "parallel")),
    )(s5o, dno, w_su, w_du, proj, proj)


def _mix_bwd(dx2b, w_out, proj, ys, yd):
    L, K = dx2b.shape
    N = w_out.shape[0]
    tm, tn = min(512, L), 512

    def body(a, b, gs, gd, ys_ref, yd_ref, dgs_ref, dgd_ref, dys_ref, dyd_ref):
        dm = lax.dot_general(a[...], b[...], (((1,), (1,)), ((), ())), preferred_element_type=F32)
        ss, sd = _sigmoid(gs[...]), _sigmoid(gd[...])
        dys_ref[...] = (dm * ss).astype(BF16)
        dyd_ref[...] = (dm * sd).astype(BF16)
        dgs_ref[...] = (dm * ys_ref[...] * ss * (1.0 - ss)).astype(BF16)
        dgd_ref[...] = (dm * yd_ref[...] * sd * (1.0 - sd)).astype(BF16)

    o = pl.BlockSpec((tm, tn), lambda i, j: (i, j))
    return pl.pallas_call(
        body, name="mix_bwd", grid=(L // tm, N // tn),
        in_specs=[pl.BlockSpec((tm, K), lambda i, j: (i, 0)), pl.BlockSpec((tn, K), lambda i, j: (j, 0)),
                  pl.BlockSpec((tm, tn), lambda i, j: (i, OFF_GS // tn + j)),
                  pl.BlockSpec((tm, tn), lambda i, j: (i, OFF_GD // tn + j)), o, o],
        out_specs=[o, o, o, o], out_shape=[_sds((L, N), BF16)] * 4, compiler_params=_cp(("parallel", "parallel")),
    )(dx2b, w_out, proj, proj, ys, yd)


def _final(mixed, w_out, x, tgt, fw):
    L, D = x.shape
    tm = min(256, L)

    def body(a_ref, b_ref, x_ref, t_ref, w_ref, dx_ref, dxb_ref, loss_ref, dw_ref):
        i = pl.program_id(0)
        x2 = x_ref[...] + jnp.dot(a_ref[...], b_ref[...], preferred_element_type=F32)
        w = w_ref[...]
        r = lax.rsqrt(jnp.mean(x2 * x2, axis=-1, keepdims=True) + EPS)
        xn = x2 * r
        e = xn * w - t_ref[...]
        lpart = 0.5 * jnp.sum(jnp.mean(e * e, axis=-1, keepdims=True), axis=0, keepdims=True)
        dy = e * (1.0 / D)
        t = dy * w
        dx2 = r * t - x2 * (r * r * r) * jnp.mean(t * x2, axis=-1, keepdims=True)
        dx_ref[...] = dx2
        dxb_ref[...] = dx2.astype(BF16)
        dwp = jnp.sum(dy * xn, axis=0, keepdims=True)
        lrow = jnp.broadcast_to(lpart, loss_ref.shape)

        @pl.when(i == 0)
        def _():
            loss_ref[...] = lrow
            dw_ref[...] = dwp

        @pl.when(i > 0)
        def _():
            loss_ref[...] += lrow
            dw_ref[...] += dwp

    row = pl.BlockSpec((tm, D), lambda i: (i, 0))
    one = pl.BlockSpec((1, D), lambda i: (0, 0))
    return pl.pallas_call(
        body, name="final", grid=(L // tm,),
        in_specs=[row, pl.BlockSpec((D, D), lambda i: (0, 0)), row, row, one],
        out_specs=[row, row, pl.BlockSpec((1, 128), lambda i: (0, 0)), one],
        out_shape=[_sds((L, D)), _sds((L, D), BF16), _sds((1, 128)), _sds((1, D))], compiler_params=_cp(("arbitrary",)),
    )(mixed, w_out, x, tgt, fw)


def _block_diag(t):
    J, g, a, b = t.shape
    eye = jnp.eye(g, dtype=t.dtype)
    return (t[:, :, :, None, :] * eye[None, :, None, :, None]).reshape(J, g * a, g * b)


def _block_diag_take(m, g):
    J, ga, gb = m.shape
    a, b = ga // g, gb // g
    m5 = m.reshape(J, g, a, g, b)
    idx = jnp.arange(g)
    return m5[:, idx, :, idx, :].transpose(1, 0, 2, 3)


class _PlainOps:
    def __init__(self, w_rest):
        self.w_rest = w_rest

    def in_proj(self, h, wt_perm):
        return _mm(h, wt_perm, tb=True, name="in_proj", tm=1024, tn=1152), self.w_rest[:3]

    def s5_core(self, *args):
        return _s5_core_fwd(*args), self.w_rest[3]

    def rest_grads(self, grads, twins):
        pass

    def d_w_in(self, h, dproj):
        return _mm(dproj, h, ta=True, name="d_w_in", tm=1152, tn=1024, twin=True)

    def d_h(self, dproj, wt_perm, d_wt_perm, d_wt_twin):
        return _mm(dproj, wt_perm, name="d_h", tm=2048, tn=1024, tk=1152)


def _local_step(x, tgt, ln_w, w_perm, lam_re, lam_im, log_step, b_re, b_im, c_re, c_im, s5_d,
                conv_w, a_log, dt_bias, norm_w, fw, ops):
    G, P, gb = S5_GROUPS, S5_STATE, S5_GROUPS // S5_BLOCKS
    h, rstd = _ln_fwd(x, ln_w)
    proj, (w_glu, w_su, w_du) = ops.in_proj(h, w_perm)

    b_re2, b_im2 = b_re.reshape(G, P * S5_GROUP), b_im.reshape(G, P * S5_GROUP)
    ls2 = log_step.reshape(G, 1)
    abar_re, abar_im, bb_re, bb_im = _s5_param_fwd(lam_re, lam_im, ls2, b_re2, b_im2)

    def to_wb(bb):
        return _block_diag(bb.reshape(S5_BLOCKS, gb, P, S5_GROUP).transpose(0, 1, 3, 2)).astype(BF16)

    def to_cb(cc):
        return _block_diag(cc.reshape(S5_BLOCKS, gb, S5_GROUP, P).transpose(0, 1, 3, 2)).astype(BF16)

    wbr, wbi, cbr, cbi = to_wb(bb_re), to_wb(bb_im), to_cb(c_re), to_cb(c_im)
    a_re_row, a_im_row = abar_re.reshape(1, G * P), abar_im.reshape(1, G * P)
    yc, w_out = ops.s5_core(proj, wbr, wbi, a_re_row, a_im_row, cbr, cbi)
    s5o = _s5_post_fwd(yc, proj, s5_d, w_glu)

    pad = lambda v: jnp.pad(v, ((0, 0), (DN_HEADS, 128 - 2 * DN_HEADS)))
    alog_row, dtb_row = pad(a_log), pad(dt_bias)
    qkv = _dn_conv_fwd(proj, conv_w)
    gates = _dn_gates_fwd(proj, alog_row, dtb_row)
    prep = _dn_prep_fwd(qkv, gates)
    o_dn, states = _dn_scan_fwd(*prep)
    dno = _dn_post_fwd(o_dn, proj, norm_w)

    ys, yd, mixed = _mix_fwd(s5o, dno, w_su, w_du, proj)
    dx2, dx2b, loss_row, d_fw = _final(mixed, w_out, x, tgt, fw)
    d_w_out, d_w_out_b = _mm(mixed, dx2b, ta=True, name="d_w_out", twin=True)
    dgs, dgd, dys, dyd = _mix_bwd(dx2b, w_out, proj, ys, yd)
    d_w_su, d_w_su_b = _mm(s5o, dys, ta=True, name="d_w_su", shard_out=True, twin=True)
    d_w_du, d_w_du_b = _mm(dno, dyd, ta=True, name="d_w_du", shard_out=True, twin=True)
    ds5o = _mm(dys, w_su, tb=True, name="d_s5o")
    ddno = _mm(dyd, w_du, tb=True, name="d_dno")

    dyc, du1, dz_s, d_s5d, d_w_glu = _s5_post_bwd(yc, proj, s5_d, w_glu, ds5o)
    ops.rest_grads((d_w_glu, d_w_su, d_w_du, d_w_out), (d_w_glu.astype(BF16), d_w_su_b, d_w_du_b, d_w_out_b))
    du, dwbr, dwbi, dcbr, dcbi, dar, dai = _s5_core_bwd(proj, wbr, wbi, a_re_row, a_im_row, cbr, cbi, dyc, du1)

    def from_wb(dwb):
        return _block_diag_take(dwb, gb).transpose(0, 1, 3, 2).reshape(G, P * S5_GROUP)

    def from_cb(dcb):
        return _block_diag_take(dcb, gb).transpose(0, 1, 3, 2).reshape(G, S5_GROUP, P)

    d_lam_re, d_lam_im, d_ls, d_b_re, d_b_im = _s5_param_bwd(
        lam_re, lam_im, ls2, b_re2, b_im2, dar.reshape(G, P), dai.reshape(G, P), from_wb(dwbr), from_wb(dwbi))

    do_dn, dz_d, d_norm_w = _dn_post_bwd(o_dn, proj, norm_w, ddno)
    dqkv_act, dgates = _dn_prep_bwd(qkv, gates, *_dn_scan_bwd(*prep, states, do_dn))
    dqkv, d_conv = _dn_conv_bwd(proj, conv_w, dqkv_act)
    dpb, d_alog_row, d_dtb_row = _dn_gates_bwd(proj, alog_row, dtb_row, dgates)

    dproj = jnp.concatenate([du, dz_s, dqkv, dz_d, dgs, dgd, dpb], axis=1)
    d_w_perm, d_w_twin = ops.d_w_in(h, dproj)
    dh = ops.d_h(dproj, w_perm, d_w_perm, d_w_twin)
    grad_x, d_ln_w = _ln_bwd(x, rstd, ln_w, dh, dx2)

    grads = dict(
        ln_w=d_ln_w, w_perm=d_w_perm, s5_lam_re=d_lam_re, s5_lam_im=d_lam_im, s5_log_step=d_ls.reshape(1, G),
        s5_b_re=d_b_re.reshape(G, P, S5_GROUP), s5_b_im=d_b_im.reshape(G, P, S5_GROUP),
        s5_c_re=from_cb(dcbr), s5_c_im=from_cb(dcbi), s5_d=d_s5d, s5_w_glu=d_w_glu, s5_w_up=d_w_su,
        dn_conv_w=d_conv, dn_a_log=d_alog_row[:, DN_HEADS:2 * DN_HEADS], dn_dt_bias=d_dtb_row[:, DN_HEADS:2 * DN_HEADS],
        dn_norm_w=d_norm_w, dn_w_up=d_w_du, w_out=d_w_out, final_norm_w=d_fw)
    return loss_row, grad_x, grads


def _place():
    x, y, c = lax.axis_index("x"), lax.axis_index("y"), lax.axis_index("c")
    return x, y, c


def _remote(src, dst, send_sem, recv_sem, to):
    return pltpu.make_async_remote_copy(src_ref=src, dst_ref=dst, send_sem=send_sem, recv_sem=recv_sem,
                                        device_id=to, device_id_type=MESH)


def _gather_exchange(shards, whole=()):
    na, nw = len(shards), len(whole)

    def half_of(ref, a, half):
        rows = shards[a].shape[0]
        return ref.at[pl.ds(half * (rows // 2), rows // 2)]

    def plan(ins, outs, send_sems, recv_sems, local_sems, receiving):
        x, y, c = _place()
        me = 2 * x + y
        sibling = (x, y, 1 - c)
        chips = [(1 - x, y), (x, 1 - y), (1 - x, 1 - y)]

        def part(a, chip, half):
            return half_of(outs[a].at[chip], a, half)

        own = [pltpu.make_async_copy(ins[a], outs[a].at[me], local_sems.at[a]) for a in range(na + nw)]
        sends, landed, passed, arrivals = [], [], [], []
        for a in range(na):
            for j, (px, py) in enumerate(chips):
                k = 6 * a + j
                sends.append(_remote(half_of(ins[a], a, c), part(a, me, c), send_sems.at[k], recv_sems.at[k], (px, py, c)))
                if receiving:
                    got, other = part(a, 2 * px + py, c), part(a, 2 * px + py, 1 - c)
                    landed.append(_remote(got, got, send_sems.at[k], recv_sems.at[k], (px, py, c)))
                    passed.append(_remote(got, got, send_sems.at[k + 3], recv_sems.at[k + 3], sibling))
                    arrivals.append(_remote(other, other, send_sems.at[k + 3], recv_sems.at[k + 3], sibling))
        for a in range(na, na + nw):
            for j, (px, py) in enumerate(chips):
                k = 6 * na + 3 * (a - na) + j
                sends.append(_remote(ins[a], outs[a].at[me], send_sems.at[k], recv_sems.at[k], (px, py, c)))
                if receiving:
                    arrivals.append(_remote(ins[a], outs[a].at[2 * px + py], send_sems.at[k], recv_sems.at[k], (px, py, c)))
        return own, sends, landed, passed, arrivals

    def start(ins, outs, *sems):
        own, sends, _, _, _ = plan(ins, outs, *sems, False)
        for cp in own + sends:
            cp.start()

    def finish(ins, outs, *sems):
        own, sends, landed, passed, arrivals = plan(ins, outs, *sems, True)
        for got, fwd in zip(landed, passed):
            got.wait_recv()
            fwd.start()
        for cp in arrivals:
            cp.wait_recv()
        for cp in sends + passed:
            cp.wait_send()
        for cp in own:
            cp.wait()

    arrays = list(shards) + list(whole)
    return _Exchange(arrays, [_sds((N_CHIPS,) + s.shape, s.dtype) for s in arrays], 6 * na + 3 * nw, start, finish)


def _gather_relayed(shard, whole, name):
    rows, cols = shard.shape
    nw = len(whole)

    def body(*refs):
        in_ref, w_in = refs[0], refs[1:1 + nw]
        out_ref, w_out = refs[1 + nw], refs[2 + nw:2 + 2 * nw]
        send_sems, recv_sems, local_sems = refs[2 + 2 * nw:]
        x, y, c = _place()
        me = 2 * x + y
        near = (jnp.where(c == 1, 1 - x, x), jnp.where(c == 1, y, 1 - y))
        far = (jnp.where(c == 1, x, 1 - x), jnp.where(c == 1, 1 - y, y))
        diag = (1 - x, 1 - y)
        sibling = (x, y, 1 - c)
        chip_of = lambda p: 2 * p[0] + p[1]

        def half(ref, h):
            return ref.at[pl.ds(0, rows), pl.ds(h * (cols // 2), cols // 2)]

        own = [pltpu.make_async_copy(in_ref, out_ref.at[me], local_sems.at[0])]
        own += [pltpu.make_async_copy(w_in[a], w_out[a].at[me], local_sems.at[1 + a]) for a in range(nw)]
        others = [(1 - x, y), (x, 1 - y), (1 - x, 1 - y)]
        small = [_remote(w_in[a], w_out[a].at[me], send_sems.at[4 + 3 * a + j], recv_sems.at[4 + 3 * a + j], (*p, c))
                 for a in range(nw) for j, p in enumerate(others)]
        sends = [_remote(in_ref, out_ref.at[me], send_sems.at[0], recv_sems.at[0], (*near, c))]
        for cp in own + small + sends:
            cp.start()
        from_near = out_ref.at[chip_of(near)]
        _remote(from_near, from_near, send_sems.at[0], recv_sems.at[0], (*near, c)).wait_recv()
        sends.append(_remote(from_near, from_near, send_sems.at[1], recv_sems.at[1], sibling))
        sends[-1].start()
        from_far = out_ref.at[chip_of(far)]
        _remote(from_far, from_far, send_sems.at[1], recv_sems.at[1], sibling).wait_recv()
        sends.append(_remote(half(from_far, c), half(from_far, c), send_sems.at[2], recv_sems.at[2], (*near, c)))
        sends[-1].start()
        of_diag = out_ref.at[chip_of(diag)]
        _remote(half(of_diag, c), half(of_diag, c), send_sems.at[2], recv_sems.at[2], (*near, c)).wait_recv()
        sends.append(_remote(half(of_diag, c), half(of_diag, c), send_sems.at[3], recv_sems.at[3], sibling))
        sends[-1].start()
        _remote(half(of_diag, 1 - c), half(of_diag, 1 - c), send_sems.at[3], recv_sems.at[3], sibling).wait_recv()
        for a in range(nw):
            for j, p in enumerate(others):
                _remote(w_in[a], w_out[a].at[chip_of(p)], send_sems.at[4 + 3 * a + j], recv_sems.at[4 + 3 * a + j], (*p, c)).wait_recv()
        for cp in sends + small:
            cp.wait_send()
        for cp in own:
            cp.wait()

    arrays = [shard] + list(whole)
    n_sems = 4 + 3 * nw
    return pl.pallas_call(
        body, name=name, in_specs=[ANY] * (1 + nw), out_specs=[ANY] * (1 + nw),
        out_shape=[_sds((N_CHIPS,) + a.shape, a.dtype) for a in arrays],
        scratch_shapes=[pltpu.SemaphoreType.DMA((n_sems,)) for _ in range(3)],
    )(*arrays)


def _owners_exchange(csbs):
    na = len(csbs)

    def plan(ins, outs, send_sems, recv_sems, local_sems, receiving):
        x, y, c = _place()
        me = 2 * x + y
        sends, arrivals = [], []
        for a in range(na):
            for k in range(N_CHIPS - 1):
                j = (me + 1 + k) % N_CHIPS
                sends.append(_remote(ins[a].at[k], outs[a].at[2 - k], send_sems.at[3 * a + k], recv_sems.at[3 * a + 2 - k],
                                     (j // 2, j % 2, c)))
                if receiving:
                    arrivals.append(_remote(ins[a].at[k], outs[a].at[k], send_sems.at[3 * a + k], recv_sems.at[3 * a + k], (x, y, c)))
        return sends, arrivals

    def start(ins, outs, *sems):
        for cp in plan(ins, outs, *sems, False)[0]:
            cp.start()

    def finish(ins, outs, *sems):
        sends, arrivals = plan(ins, outs, *sems, True)
        for cp in arrivals:
            cp.wait_recv()
        for cp in sends:
            cp.wait_send()

    return _Exchange(csbs, [_sds(g.shape, g.dtype) for g in csbs], 3 * na, start, finish)


def _swap_halves(gxs, name):
    na = len(gxs)
    half_shape = lambda g: (WT_ROWS // 2, g.shape[1]) if g.ndim == 2 else g.shape[2:]

    def body(*refs):
        ins, outs = refs[:na], refs[na:2 * na]
        send_sems, recv_sems = refs[2 * na:]
        x, y, c = _place()
        cps = []
        for a in range(na):
            if gxs[a].ndim == 2:
                for j in range(N_CHIPS):
                    rows = pl.ds(pl.multiple_of(WT_WIN[j] + (1 - c) * (WT_ROWS // 2), 16), WT_ROWS // 2)
                    cps.append(_remote(ins[a].at[rows], outs[a].at[j, 0], send_sems.at[na + j], recv_sems.at[na + j], (x, y, 1 - c)))
            else:
                cps.append(_remote(ins[a].at[pl.ds(0, N_CHIPS), pl.ds(1 - c, 1)], outs[a], send_sems.at[a], recv_sems.at[a],
                                   (x, y, 1 - c)))
        for cp in cps:
            cp.start()
        for cp in cps:
            cp.wait()

    return pl.pallas_call(
        body, name=name, in_specs=[ANY] * na, out_specs=[ANY] * na,
        out_shape=[_sds((N_CHIPS, 1) + half_shape(g), g.dtype) for g in gxs],
        scratch_shapes=[pltpu.SemaphoreType.DMA((na + N_CHIPS,)), pltpu.SemaphoreType.DMA((na + N_CHIPS,))],
    )(*gxs)


def _half_block(gx, tr, shard):
    if gx.ndim == 4:
        return pl.BlockSpec((1, 1, tr, gx.shape[3]), lambda *g: (shard(*g), g[-1][0], g[-2], 0))
    return pl.BlockSpec(
        (pl.Element(tr), pl.Element(gx.shape[1])),
        lambda *g: (pl.multiple_of(g[-1][2 + shard(*g)] + g[-1][0] * (WT_ROWS // 2) + g[-2] * tr, 16), 0))


def _share_halves(gfs):
    na = len(gfs)

    def body(*refs):
        ins, outs = refs[:na], refs[na:2 * na]
        send_sems, recv_sems = refs[2 * na:]
        x, y, c = _place()
        cps = [_remote(ins[a].at[pl.ds(c, 1)], outs[a].at[pl.ds(c, 1)], send_sems.at[a], recv_sems.at[a], (x, y, 1 - c))
               for a in range(na)]
        for cp in cps:
            cp.start()
        for a in range(na):
            cps[a].wait_send()
            _remote(ins[a].at[pl.ds(1 - c, 1)], outs[a].at[pl.ds(1 - c, 1)], send_sems.at[a], recv_sems.at[a], (x, y, 1 - c)).wait_recv()

    return pl.pallas_call(
        body, name="rs_share_halves", in_specs=[ANY] * na, out_specs=[ANY] * na,
        out_shape=[_sds(g.shape, g.dtype) for g in gfs], input_output_aliases={a: a for a in range(na)},
        scratch_shapes=[pltpu.SemaphoreType.DMA((na,)), pltpu.SemaphoreType.DMA((na,))],
    )(*gfs)


def _row_tile(rows, cols, budget=5 << 18):
    fits = [t for t in range(16, rows + 1, 16) if rows % t == 0 and t * cols * 4 <= budget]
    return max(fits) if fits else rows


def _chip_sums(gx, r1, where):
    _, _, r2, cd = r1.shape
    tr = _row_tile(r2, cd)

    def body(w_ref, a_ref, b_ref, o_ref):
        o_ref[0] = (a_ref[...].reshape(tr, cd) + b_ref[0, 0].astype(F32)).astype(BF16)

    other = lambda k, i, w: (w[1] + 1 + k) % N_CHIPS
    return pl.pallas_call(
        body, name="rs_chip_sums",
        grid_spec=pltpu.PrefetchScalarGridSpec(
            num_scalar_prefetch=1, grid=(N_CHIPS - 1, r2 // tr),
            in_specs=[_half_block(gx, tr, other), pl.BlockSpec((1, 1, tr, cd), lambda k, i, w: (other(k, i, w), 0, i, 0))],
            out_specs=pl.BlockSpec((1, tr, cd), lambda k, i, w: (k, i, 0))),
        out_shape=_sds((N_CHIPS - 1, r2, cd), BF16), compiler_params=_cp(("parallel", "parallel")),
    )(where, gx, r1)


def _owner_sum(gx, r1, r2x, where):
    _, _, r2, cd = r1.shape
    tr = _row_tile(r2, cd)

    def body(w_ref, a_ref, b_ref, r_ref, o_ref):
        acc = a_ref[...].reshape(tr, cd) + b_ref[0, 0].astype(F32)
        for k in range(N_CHIPS - 1):
            acc = acc + r_ref[k].astype(F32)
        o_ref[0] = acc

    return pl.pallas_call(
        body, name="rs_owner_sum",
        grid_spec=pltpu.PrefetchScalarGridSpec(
            num_scalar_prefetch=1, grid=(r2 // tr,),
            in_specs=[_half_block(gx, tr, lambda i, w: w[1]),
                      pl.BlockSpec((1, 1, tr, cd), lambda i, w: (w[1], 0, i, 0)),
                      pl.BlockSpec((N_CHIPS - 1, tr, cd), lambda i, w: (0, i, 0))],
            out_specs=pl.BlockSpec((1, tr, cd), lambda i, w: (w[0], i, 0))),
        out_shape=_sds((2, r2, cd)), compiler_params=_cp(("parallel",)),
    )(where, gx, r1, r2x)


def _adamw_math(w, g, m, v):
    m = ADAM_B1 * m + (1.0 - ADAM_B1) * g
    v = ADAM_B2 * v + (1.0 - ADAM_B2) * (g * g)
    m_hat = m / (1.0 - ADAM_B1 ** ADAM_STEP)
    v_hat = v / (1.0 - ADAM_B2 ** ADAM_STEP)
    delta = -ADAM_LR * (m_hat / (jnp.sqrt(v_hat) + ADAM_EPS) + ADAM_WD * w)
    return delta, m, v


def _adamw(w, g, m, v, name, echo=False):
    rows, cd = w.shape
    if rows % 16 == 0:
        tr, tc = _row_tile(rows, cd, budget=3 << 19), cd
    else:
        tr, tc = rows, (128 if rows * cd * 4 > (3 << 19) else cd)
    assert rows % tr == 0 and cd % tc == 0
    n_out = 4 if echo else 3

    def body(w_ref, g_ref, m_ref, v_ref, d_ref, mo_ref, vo_ref, *go_ref):
        gv = g_ref[...]
        d, mm, vv = _adamw_math(w_ref[...], gv, m_ref[...], v_ref[...])
        d_ref[...] = d
        mo_ref[...] = mm
        vo_ref[...] = vv
        if echo:
            go_ref[0][...] = gv

    blk = pl.BlockSpec((tr, tc), lambda i, j: (i, j))
    return pl.pallas_call(
        body, name=name, grid=(rows // tr, cd // tc), in_specs=[blk] * 4, out_specs=[blk] * n_out,
        out_shape=[_sds(w.shape)] * n_out, compiler_params=_cp(("parallel", "parallel")),
    )(w, g, m, v)


def _small_allreduce(gp):
    R = gp.shape[0]
    R2 = R // 2
    assert R2 % 8 == 0

    def body(g_ref, go_ref, sib, csum, land, send_sems, recv_sems):
        x, y, c = _place()
        me = 2 * x + y
        sibling = (x, y, 1 - c)
        chips = [(1 - x, y), (x, 1 - y), (1 - x, 1 - y)]
        swap = _remote(g_ref, sib, send_sems.at[0], recv_sems.at[0], sibling)
        swap.start()
        swap.wait()
        csum[...] = g_ref[...] + sib[...]
        half = csum.at[pl.ds(c * R2, R2)]
        land[me] = csum[pl.ds(c * R2, R2), :]
        cps = [_remote(half, land.at[me], send_sems.at[1 + j], recv_sems.at[1 + j], (px, py, c))
               for j, (px, py) in enumerate(chips)]
        for cp in cps:
            cp.start()
        for j, (px, py) in enumerate(chips):
            _remote(half, land.at[2 * px + py], send_sems.at[1 + j], recv_sems.at[1 + j], (px, py, c)).wait_recv()
        for cp in cps:
            cp.wait_send()
        mine = go_ref.at[pl.ds(c * R2, R2)]
        go_ref[pl.ds(c * R2, R2), :] = (land[0] + land[1]) + (land[2] + land[3])
        share = _remote(mine, mine, send_sems.at[4], recv_sems.at[4], sibling)
        share.start()
        share.wait_send()
        other = go_ref.at[pl.ds((1 - c) * R2, R2)]
        _remote(other, other, send_sems.at[4], recv_sems.at[4], sibling).wait_recv()

    vm = pl.BlockSpec(memory_space=pltpu.VMEM)
    return pl.pallas_call(
        body, name="small_allreduce", in_specs=[vm], out_specs=vm, out_shape=_sds((R, 128)),
        scratch_shapes=[pltpu.VMEM((R, 128), F32), pltpu.VMEM((R, 128), F32), pltpu.VMEM((N_CHIPS, R2, 128), F32),
                        pltpu.SemaphoreType.DMA((5,)), pltpu.SemaphoreType.DMA((5,))],
        compiler_params=_cp(),
    )(gp)


def _adamw_many(ws, gs, ms, vs):
    n = len(ws)

    def body(*refs):
        w_r, g_r, m_r, v_r = refs[:n], refs[n:2 * n], refs[2 * n:3 * n], refs[3 * n:4 * n]
        d_r, mo_r, vo_r = refs[4 * n:5 * n], refs[5 * n:6 * n], refs[6 * n:]
        for i in range(n):
            d_r[i][...], mo_r[i][...], vo_r[i][...] = _adamw_math(w_r[i][...], g_r[i][...], m_r[i][...], v_r[i][...])

    vm = pl.BlockSpec(memory_space=pltpu.VMEM)
    shapes = [_sds(a.shape) for a in ws]
    outs = pl.pallas_call(
        body, name="adamw_small", in_specs=[vm] * (4 * n), out_specs=[vm] * (3 * n), out_shape=shapes * 3, compiler_params=_cp(),
    )(*ws, *gs, *ms, *vs)
    return outs[:n], outs[n:2 * n], outs[2 * n:]


def _pack(arrs):
    rows = []
    for a in arrs:
        f = a.reshape(-1)
        f = jnp.pad(f, (0, (-f.shape[0]) % 128))
        rows.append(f.reshape(-1, 128))
    p = jnp.concatenate(rows, axis=0)
    return jnp.pad(p, ((0, (-p.shape[0]) % 8), (0, 0)))


def _unpack(p, shapes):
    out, r = [], 0
    for s in shapes:
        n = math.prod(s)
        nr = -(-n // 128)
        out.append(p[r:r + nr].reshape(-1)[:n].reshape(s))
        r += nr
    return out


class _ExchangeOps(_PlainOps):
    def __init__(self, rest_shards, where):
        self.rest_shards, self.where = rest_shards, where
        self.reduced = []

    def in_proj(self, h, wt_perm):
        proj, g_glu, g_su, g_du = _mm(h, wt_perm, tb=True, name="in_proj", tm=1024, tn=1152,
                                      exchange=_gather_exchange(self.rest_shards[:3]))
        cat = lambda g: jnp.concatenate([g[j] for j in range(N_CHIPS)], axis=1)
        return proj, (g_glu.reshape(D_S5, D_S5), cat(g_su), cat(g_du))

    def s5_core(self, *args):
        yc, g_out = _s5_core_fwd(*args, exchange=_gather_exchange(self.rest_shards[3:]))
        return yc, g_out.reshape(D_MODEL, D_MODEL)

    def _chip_sums(self, gxs, twins, name):
        r1s = _swap_halves(twins, name)
        return r1s, [_chip_sums(gx, r1, self.where) for gx, r1 in zip(gxs, r1s)]

    def rest_grads(self, grads, twins):
        shapes = [(N_CHIPS, 2, D_S5 // 8, D_S5), (N_CHIPS, 2, D_S5 // 2, D_MODEL // N_CHIPS),
                  (N_CHIPS, 2, D_DN // 2, D_MODEL // N_CHIPS), (N_CHIPS, 2, D_MODEL // 8, D_MODEL)]
        gxs = [g.reshape(s) for g, s in zip(grads, shapes)]
        r1s, csbs = self._chip_sums(gxs, [t.reshape(s) for t, s in zip(twins, shapes)], "rs_swap_rest")
        self.rest = (gxs, r1s, csbs)

    def d_w_in(self, h, dproj):
        gxs, r1s, csbs = self.rest
        d_wt_perm, twin, *r2s = _mm(dproj, h, ta=True, name="d_w_in", tm=1152, tn=1024, twin=True,
                                    exchange=_owners_exchange(csbs))
        self.reduced = list(zip(gxs, r1s, r2s))
        return d_wt_perm, twin

    def d_h(self, dproj, wt_perm, d_wt_perm, d_wt_twin):
        self.beta_a = d_wt_perm[OFF_B:OFF_B + WT_NB]
        (r1,), (csb,) = self._chip_sums([d_wt_perm], [d_wt_twin], "rs_swap_w_in")
        dh, r2 = _mm(dproj, wt_perm, name="d_h", tm=2048, tn=1024, tk=1152, exchange=_owners_exchange([csb]))
        self.reduced = [(d_wt_perm, r1, r2)] + self.reduced
        return dh


WT_SHARD = D_IN // N_CHIPS
WT_NB = 2 * DN_HEADS
WT_B, WT_LO = divmod(OFF_GS, WT_SHARD)
WT_FIRST = [i * WT_SHARD - (WT_NB if i > WT_B else 0) for i in range(N_CHIPS)]
WT_WIN = [16 * (r // 16) for r in WT_FIRST]
WT_SHIFT = [r - s for r, s in zip(WT_FIRST, WT_WIN)]
WT_ROWS = 2592
assert (WT_LO + WT_SHIFT[WT_B]) % 16 == 0 and max(WT_SHIFT) + WT_SHARD <= WT_ROWS and WT_WIN[-1] + WT_ROWS <= D_IN_PAD


def _wt_to_window(shard, chip):
    d = jnp.asarray(WT_SHIFT, jnp.int32)[chip]
    gap = jnp.where(chip == WT_B, 0, WT_NB)
    win = jnp.zeros((WT_ROWS, shard.shape[1]), shard.dtype)
    win = lax.dynamic_update_slice(win, shard[:WT_LO], (d, 0))
    win = lax.dynamic_update_slice(win, shard[WT_LO:WT_LO + WT_NB], (d + WT_LO, 0))
    win = lax.dynamic_update_slice(win, shard[WT_LO + WT_NB:], (d + WT_LO + gap, 0))
    return win, shard[WT_LO:WT_LO + WT_NB]


def _wt_from_window(win, beta_a, chip):
    d = jnp.asarray(WT_SHIFT, jnp.int32)[chip]
    gap = jnp.where(chip == WT_B, 0, WT_NB)
    cols = win.shape[1]
    head = lax.dynamic_slice(win, (d, 0), (WT_LO, cols))
    mid = jnp.where(chip == WT_B, beta_a, lax.dynamic_slice(win, (d + WT_LO, 0), (WT_NB, cols)))
    tail = lax.dynamic_slice(win, (d + WT_LO + gap, 0), (WT_SHARD - WT_LO - WT_NB, cols))
    return jnp.concatenate([head, mid, tail], axis=0)


def _wt_regroup(wins, beta_a):
    parts, at = [], 0
    for i in range(N_CHIPS):
        end = WT_WIN[i + 1] if i + 1 < N_CHIPS else OFF_B
        lo = at - WT_WIN[i]
        over = WT_WIN[i] + WT_ROWS - end if i + 1 < N_CHIPS else 0
        parts.append(wins[i, lo:end - WT_WIN[i]])
        if over:
            parts.append(wins[i, end - WT_WIN[i]:] + wins[i + 1, :over])
        at = end + over
    pad = jnp.zeros((D_IN_PAD - OFF_B - WT_NB, wins.shape[2]), wins.dtype)
    return jnp.concatenate(parts + [beta_a, pad], axis=0)


_SMALL = ("ln_w", "s5_lam_re", "s5_lam_im", "s5_log_step", "s5_b_re", "s5_b_im", "s5_c_re", "s5_c_im", "s5_d",
          "dn_a_log", "dn_dt_bias", "dn_norm_w", "final_norm_w")
_BIG = ("w_in", "s5_w_glu", "s5_w_up", "dn_w_up", "w_out")
_ORDER = ("ln_w", "w_in", "s5_lam_re", "s5_lam_im", "s5_log_step", "s5_b_re", "s5_b_im", "s5_c_re", "s5_c_im", "s5_d",
          "s5_w_glu", "s5_w_up", "dn_conv_w", "dn_a_log", "dn_dt_bias", "dn_norm_w", "dn_w_up", "w_out", "final_norm_w")


def kernel(x, ln_w, w_in, s5_lam_re, s5_lam_im, s5_log_step, s5_b_re, s5_b_im, s5_c_re, s5_c_im, s5_d, s5_w_glu, s5_w_up, dn_conv_w, dn_a_log, dn_dt_bias, dn_norm_w, dn_w_up, w_out, final_norm_w, loss_target, m_ln_w, m_w_in, m_s5_lam_re, m_s5_lam_im, m_s5_log_step, m_s5_b_re, m_s5_b_im, m_s5_c_re, m_s5_c_im, m_s5_d, m_s5_w_glu, m_s5_w_up, m_dn_conv_w, m_dn_a_log, m_dn_dt_bias, m_dn_norm_w, m_dn_w_up, m_w_out, m_final_norm_w, v_ln_w, v_w_in, v_s5_lam_re, v_s5_lam_im, v_s5_log_step, v_s5_b_re, v_s5_b_im, v_s5_c_re, v_s5_c_im, v_s5_d, v_s5_w_glu, v_s5_w_up, v_dn_conv_w, v_dn_a_log, v_dn_dt_bias, v_dn_norm_w, v_dn_w_up, v_w_out, v_final_norm_w):
    w = dict(ln_w=ln_w, w_in=w_in, s5_lam_re=s5_lam_re, s5_lam_im=s5_lam_im, s5_log_step=s5_log_step, s5_b_re=s5_b_re,
             s5_b_im=s5_b_im, s5_c_re=s5_c_re, s5_c_im=s5_c_im, s5_d=s5_d, s5_w_glu=s5_w_glu, s5_w_up=s5_w_up,
             dn_conv_w=dn_conv_w, dn_a_log=dn_a_log, dn_dt_bias=dn_dt_bias, dn_norm_w=dn_norm_w, dn_w_up=dn_w_up, w_out=w_out,
             final_norm_w=final_norm_w)
    m = dict(ln_w=m_ln_w, w_in=m_w_in, s5_lam_re=m_s5_lam_re, s5_lam_im=m_s5_lam_im, s5_log_step=m_s5_log_step,
             s5_b_re=m_s5_b_re, s5_b_im=m_s5_b_im, s5_c_re=m_s5_c_re, s5_c_im=m_s5_c_im, s5_d=m_s5_d, s5_w_glu=m_s5_w_glu,
             s5_w_up=m_s5_w_up, dn_conv_w=m_dn_conv_w, dn_a_log=m_dn_a_log, dn_dt_bias=m_dn_dt_bias, dn_norm_w=m_dn_norm_w,
             dn_w_up=m_dn_w_up, w_out=m_w_out, final_norm_w=m_final_norm_w)
    v = dict(ln_w=v_ln_w, w_in=v_w_in, s5_lam_re=v_s5_lam_re, s5_lam_im=v_s5_lam_im, s5_log_step=v_s5_log_step,
             s5_b_re=v_s5_b_re, s5_b_im=v_s5_b_im, s5_c_re=v_s5_c_re, s5_c_im=v_s5_c_im, s5_d=v_s5_d, s5_w_glu=v_s5_w_glu,
             s5_w_up=v_s5_w_up, dn_conv_w=v_dn_conv_w, dn_a_log=v_dn_a_log, dn_dt_bias=v_dn_dt_bias, dn_norm_w=v_dn_norm_w,
             dn_w_up=v_dn_w_up, w_out=v_w_out, final_norm_w=v_final_norm_w)
    xi, yi, ci = _place()
    chip = 2 * xi + yi
    where = jnp.stack([ci, chip, *[jnp.int32(s) for s in WT_WIN]]).astype(jnp.int32)

    tr = lambda a: jnp.swapaxes(a[0], 0, 1)
    win, beta_a = _wt_to_window(tr(w_in).astype(BF16), chip)
    g_win, g_ba, g_conv = _gather_relayed(win, [beta_a, dn_conv_w[0]], "gather_w_in")
    cat = lambda g: jnp.concatenate([g[j] for j in range(N_CHIPS)], axis=1)
    w_perm = _wt_regroup(g_win, g_ba[WT_B])

    ops = _ExchangeOps([w[n][0].astype(BF16) for n in _BIG[1:]], where)
    loss_row, grad_x, g = _local_step(
        x[0], loss_target[0], ln_w, w_perm, s5_lam_re[0], s5_lam_im[0], s5_log_step, s5_b_re[0], s5_b_im[0], s5_c_re[0],
        s5_c_im[0], s5_d, cat(g_conv), dn_a_log, dn_dt_bias, dn_norm_w, final_norm_w[None], ops)
    loss = lax.psum(loss_row[0, 0], ("x", "y", "c"))

    gfs = [_owner_sum(gx, r1, r2x, where) for gx, r1, r2x in ops.reduced]
    gfs = _share_halves(gfs)
    grads, deltas, new_m, new_v = {}, {}, {}, {}
    for n, gf in zip(_BIG[1:], gfs[1:]):
        shp = w[n].shape
        g2 = gf.reshape(shp[1:])
        d_, m_, v_ = _adamw(w[n][0], g2, m[n][0], v[n][0], "adamw_" + n)
        grads[n], deltas[n], new_m[n], new_v[n] = g2.reshape(shp), d_.reshape(shp), m_.reshape(shp), v_.reshape(shp)

    go = _small_allreduce(_pack([g[n] for n in _SMALL] + [g["dn_conv_w"], ops.beta_a]))
    lanes = {"s5_b_re": (S5_GROUPS, S5_STATE * S5_GROUP), "s5_b_im": (S5_GROUPS, S5_STATE * S5_GROUP)}
    flat = [lanes.get(n, (math.prod(w[n].shape[:-1]), w[n].shape[-1])) for n in _SMALL]
    *gs, g_conv, g_beta_a = _unpack(go, flat + [(CONV_K, 3 * D_DN), (WT_NB, D_MODEL)])
    gt = _wt_from_window(gfs[0].reshape(WT_ROWS, D_MODEL), g_beta_a, chip)
    d_, m_, v_, g_ = _adamw(tr(w_in), gt, tr(m_w_in), tr(v_w_in), "adamw_w_in", echo=True)
    grads["w_in"], deltas["w_in"], new_m["w_in"], new_v["w_in"] = (jnp.swapaxes(a, 0, 1)[None] for a in (g_, d_, m_, v_))
    as2d = lambda t: [t[n].reshape(s) for n, s in zip(_SMALL, flat)]
    for dst, src in zip((grads, deltas, new_m, new_v), (gs, *_adamw_many(as2d(w), gs, as2d(m), as2d(v)))):
        for n, a in zip(_SMALL, src):
            dst[n] = a.reshape(w[n].shape)
    cc = 3 * D_DN // N_CHIPS
    g_conv_mine = lax.dynamic_slice(g_conv, (0, chip * cc), (CONV_K, cc))
    d_, m_, v_ = _adamw(dn_conv_w[0], g_conv_mine, m_dn_conv_w[0], v_dn_conv_w[0], "adamw_dn_conv_w")
    grads["dn_conv_w"], deltas["dn_conv_w"], new_m["dn_conv_w"], new_v["dn_conv_w"] = (
        g_conv_mine[None], d_[None], m_[None], v_[None])

    return (loss, grad_x[None], *[grads[n] for n in _ORDER], *[deltas[n] for n in _ORDER], *[new_m[n] for n in _ORDER],
            *[new_v[n] for n in _ORDER])
```
